```python
import jax, jax.numpy as jnp
from jax import lax
import numpy as np

D_MODEL = 2048
BATCH = 8
SEQ = 4096
DEPTH = 2

GRID_W = 64
CTX_LEN = 256
HEAD_DIM = 128
N_HEADS = D_MODEL // HEAD_DIM
A_Q_HEADS = N_HEADS // 2
A_KV_HEADS = max(1, A_Q_HEADS // 4)
B_Q_HEADS = N_HEADS - A_Q_HEADS
B_KV_HEADS = max(1, B_Q_HEADS // 4)
C_HEADS = N_HEADS
Q_BLOCK = 128
WINDOW = 128
NA_KH = 8
NA_KW = 16
D_FF = 4 * D_MODEL
ROPE_THETA = 10000.0
ROPE_PAIRS = HEAD_DIM // 4
NORM_EPS = 1e-6
N_EVEN = (DEPTH + 1) // 2
N_ODD = DEPTH // 2
EVEN_IN = (A_Q_HEADS + 2 * A_KV_HEADS + B_Q_HEADS + 2 * B_KV_HEADS) * HEAD_DIM
ODD_IN = 3 * C_HEADS * HEAD_DIM
NEG = -1e30

kernel_name = 'hybrid_dit_gqa_swa_natten'


def _rmsnorm(x, w):
    xf = x.astype(jnp.float32)
    y = xf * lax.rsqrt(jnp.mean(xf * xf, axis=-1, keepdims=True) + NORM_EPS) * w.astype(jnp.float32)
    return y.astype(x.dtype)


def _modulate(h, shift, scale):
    return h * (1 + scale) + shift


def _rope_tables(S):
    t = jnp.arange(S)
    row = (t // GRID_W).astype(jnp.float32)
    col = (t % GRID_W).astype(jnp.float32)
    inv = ROPE_THETA ** (-jnp.arange(ROPE_PAIRS, dtype=jnp.float32) / ROPE_PAIRS)
    ang_r = row[:, None] * inv
    ang_c = col[:, None] * inv
    ang = jnp.concatenate([ang_r, ang_r, ang_c, ang_c], axis=-1)
    return jnp.cos(ang), jnp.sin(ang)


def _rot_half(u):
    u1, u2 = jnp.split(u, 2, axis=-1)
    return jnp.concatenate([-u2, u1], axis=-1)


def _apply_rope(x, cos, sin):
    xr, xc = jnp.split(x, 2, axis=-1)
    xrot = jnp.concatenate([_rot_half(xr), _rot_half(xc)], axis=-1)
    return (x.astype(jnp.float32) * cos + xrot.astype(jnp.float32) * sin).astype(x.dtype)


def _heads(x, n):
    B, N, _ = x.shape
    return x.reshape(B, N, n, HEAD_DIM).transpose(0, 2, 1, 3)


def _q_groups(x, n_kv):
    B, N, F = x.shape
    g = F // HEAD_DIM // n_kv
    return x.reshape(B, N, n_kv, g, HEAD_DIM).transpose(0, 2, 3, 1, 4)


def _merge(o):
    B, K, G, N, dh = o.shape
    return o.transpose(0, 3, 1, 2, 4).reshape(B, N, K * G * dh)


def _dense_attn(q, k, v, sink=None):
    B, K, G, N, dh = q.shape
    s = jnp.einsum('bkgnd,bkmd->bkgnm', q, k).astype(jnp.float32) * (dh ** -0.5)
    M = k.shape[2]
    if sink is not None:
        sk = jnp.broadcast_to(sink.astype(jnp.float32)[None, :, :, None, None], (B, K, G, N, 1))
        s = jnp.concatenate([s, sk], axis=-1)
    p = jax.nn.softmax(s, axis=-1)[..., :M].astype(v.dtype)
    return jnp.einsum('bkgnm,bkmd->bkgnd', p, v)


def _global_attn(q, k_all, v_all):
    B, K, G, S, dh = q.shape
    nblk = S // Q_BLOCK
    qb = jnp.moveaxis(q.reshape(B, K, G, nblk, Q_BLOCK, dh), 3, 0)

    def block(qi):
        s = jnp.einsum('bkgqd,bkmd->bkgqm', qi, k_all).astype(jnp.float32) * (dh ** -0.5)
        p = jax.nn.softmax(s, axis=-1).astype(v_all.dtype)
        return jnp.einsum('bkgqm,bkmd->bkgqd', p, v_all)

    o = lax.map(block, qb)
    return jnp.moveaxis(o, 0, 3).reshape(B, K, G, S, dh)


def _window_attn(q, k, v, k_ctx, v_ctx, sink):
    B, K, G, S, dh = q.shape
    nblk = S // Q_BLOCK
    pad = ((0, 0), (0, 0), (WINDOW, WINDOW), (0, 0))
    kp = jnp.pad(k, pad).reshape(B, K, nblk + 2, Q_BLOCK, dh)
    vp = jnp.pad(v, pad).reshape(B, K, nblk + 2, Q_BLOCK, dh)
    kw = jnp.concatenate([kp[:, :, 0:nblk], kp[:, :, 1:nblk + 1], kp[:, :, 2:nblk + 2]], axis=3)
    vw = jnp.concatenate([vp[:, :, 0:nblk], vp[:, :, 1:nblk + 1], vp[:, :, 2:nblk + 2]], axis=3)
    qb = q.reshape(B, K, G, nblk, Q_BLOCK, dh)
    scale = dh ** -0.5
    s_win = jnp.einsum('bkgnqd,bknjd->bkgnqj', qb, kw).astype(jnp.float32) * scale
    blk = jnp.arange(nblk)[:, None]
    qpos = blk * Q_BLOCK + jnp.arange(Q_BLOCK)[None, :]
    kpos = blk * Q_BLOCK - WINDOW + jnp.arange(3 * Q_BLOCK)[None, :]
    kk = kpos[:, None, :]
    valid = (kk >= 0) & (kk < S) & (jnp.abs(kk - qpos[:, :, None]) <= WINDOW)
    s_win = jnp.where(valid, s_win, NEG)
    s_ctx = jnp.einsum('bkgnqd,bkcd->bkgnqc', qb, k_ctx).astype(jnp.float32) * scale
    s_sink = jnp.broadcast_to(sink.astype(jnp.float32)[None, :, :, None, None, None],
                              (B, K, G, nblk, Q_BLOCK, 1))
    p = jax.nn.softmax(jnp.concatenate([s_ctx, s_win, s_sink], axis=-1), axis=-1)
    C = k_ctx.shape[2]
    p_ctx = p[..., :C].astype(v.dtype)
    p_win = p[..., C:C + 3 * Q_BLOCK].astype(v.dtype)
    o = (jnp.einsum('bkgnqc,bkcd->bkgnqd', p_ctx, v_ctx)
         + jnp.einsum('bkgnqj,bknjd->bkgnqd', p_win, vw))
    return o.reshape(B, K, G, S, dh)


def _even_mixer(h, hc, w_in, w_out, q_norm_w, k_norm_w, sink, cos, sin, need_ctx):
    sizes = [A_Q_HEADS, A_KV_HEADS, A_KV_HEADS, B_Q_HEADS, B_KV_HEADS, B_KV_HEADS]
    splits = np.cumsum([n * HEAD_DIM for n in sizes])[:-1].tolist()
    qa, ka, va, qb, kb, vb = jnp.split(h @ w_in, splits, axis=-1)
    qac, kac, vac, qbc, kbc, vbc = jnp.split(hc @ w_in, splits, axis=-1)
    qa = _apply_rope(_rmsnorm(_q_groups(qa, A_KV_HEADS), q_norm_w), cos, sin)
    ka = _apply_rope(_rmsnorm(_heads(ka, A_KV_HEADS), k_norm_w), cos, sin)
    va = _heads(va, A_KV_HEADS)
    kac = _rmsnorm(_heads(kac, A_KV_HEADS), k_norm_w)
    vac = _heads(vac, A_KV_HEADS)
    oa = _global_attn(qa, jnp.concatenate([kac, ka], axis=2), jnp.concatenate([vac, va], axis=2))
    sink_kg = sink.reshape(B_KV_HEADS, B_Q_HEADS // B_KV_HEADS)
    qb = _apply_rope(_q_groups(qb, B_KV_HEADS), cos, sin)
    kb = _apply_rope(_heads(kb, B_KV_HEADS), cos, sin)
    vb = _heads(vb, B_KV_HEADS)
    kbc = _heads(kbc, B_KV_HEADS)
    vbc = _heads(vbc, B_KV_HEADS)
    ob = _window_attn(qb, kb, vb, kbc, vbc, sink_kg)
    y = jnp.concatenate([_merge(oa), _merge(ob)], axis=-1) @ w_out
    if not need_ctx:
        return y, None
    qac = _rmsnorm(_q_groups(qac, A_KV_HEADS), q_norm_w)
    oac = _dense_attn(qac, kac, vac)
    obc = _dense_attn(_q_groups(qbc, B_KV_HEADS), kbc, vbc, sink_kg)
    yc = jnp.concatenate([_merge(oac), _merge(obc)], axis=-1) @ w_out
    return y, yc


def _odd_mixer(h, hc, w_in, w_out, rpb, need_ctx):
    B, S, _ = h.shape
    rows = S // GRID_W
    kh = min(NA_KH, rows)
    kw = NA_KW
    q, k, v = [_heads(t, C_HEADS) for t in jnp.split(h @ w_in, 3, axis=-1)]
    qc, kc, vc = [_heads(t, C_HEADS) for t in jnp.split(hc @ w_in, 3, axis=-1)]
    qg = q.reshape(B, C_HEADS, rows, GRID_W, HEAD_DIM)
    kg = k.reshape(B, C_HEADS, rows, GRID_W, HEAD_DIM)
    vg = v.reshape(B, C_HEADS, rows, GRID_W, HEAD_DIM)
    col = jnp.arange(GRID_W)
    cs = jnp.clip(col - kw // 2, 0, GRID_W - kw)
    col_valid = (col[None, :] >= cs[:, None]) & (col[None, :] < cs[:, None] + kw)
    ci = jnp.clip(col[None, :] - col[:, None] + NA_KW - 1, 0, 2 * NA_KW - 2)
    scale = HEAD_DIM ** -0.5
    C = kc.shape[2]

    def row_block(args):
        r, qr = args
        rs = jnp.clip(r - kh // 2, 0, rows - kh)
        kr = lax.dynamic_slice_in_dim(kg, rs, kh, axis=2)
        vr = lax.dynamic_slice_in_dim(vg, rs, kh, axis=2)
        ri = rs + jnp.arange(kh) - r + NA_KH - 1
        bias = rpb[:, ri[None, :, None], ci[:, None, :]]
        s_nb = jnp.einsum('bhqd,bhiwd->bhqiw', qr, kr).astype(jnp.float32) * scale
        s_nb = s_nb + bias[None].astype(jnp.float32)
        s_nb = jnp.where(col_valid[:, None, :], s_nb, NEG).reshape(B, C_HEADS, GRID_W, kh * GRID_W)
        s_ctx = jnp.einsum('bhqd,bhcd->bhqc', qr, kc).astype(jnp.float32) * scale
        p = jax.nn.softmax(jnp.concatenate([s_ctx, s_nb], axis=-1), axis=-1).astype(v.dtype)
        p_nb = p[..., C:].reshape(B, C_HEADS, GRID_W, kh, GRID_W)
        return (jnp.einsum('bhqc,bhcd->bhqd', p[..., :C], vc)
                + jnp.einsum('bhqiw,bhiwd->bhqd', p_nb, vr))

    o = lax.map(row_block, (jnp.arange(rows), jnp.moveaxis(qg, 2, 0)))
    o = jnp.moveaxis(o, 0, 2).reshape(B, C_HEADS, S, HEAD_DIM)
    y = _merge(o[:, :, None]) @ w_out
    if not need_ctx:
        return y, None
    oc = _dense_attn(qc[:, :, None], kc, vc)
    return y, _merge(oc) @ w_out


def _mlp(h, w1, w2):
    return jnp.square(jax.nn.relu(h @ w1)) @ w2


def _fwd_setup_inputs(seed: int = 0) -> dict:
    key = jax.random.key(seed)
    ks = jax.random.split(key, 20)
    f32 = jnp.float32

    def nrm(k, shape, fan_in, gain=1.0):
        return gain * fan_in ** -0.5 * jax.random.normal(k, shape, f32)

    return {
        'x': jax.random.normal(ks[0], (BATCH, SEQ, D_MODEL), f32),
        'c': jax.random.normal(ks[1], (BATCH, D_MODEL), f32),
        'ctx': jax.random.normal(ks[2], (BATCH, CTX_LEN, D_MODEL), f32),
        'c_ctx': jax.random.normal(ks[3], (D_MODEL,), f32),
        'ada_w': nrm(ks[4], (DEPTH, D_MODEL, 6 * D_MODEL), D_MODEL, 0.5),
        'ada_b': 0.02 * jax.random.normal(ks[5], (DEPTH, 6 * D_MODEL), f32),
        'norm_w': 1.0 + 0.02 * jax.random.normal(ks[6], (DEPTH, 2, D_MODEL), f32),
        'mlp_w1': nrm(ks[7], (DEPTH, D_MODEL, D_FF), D_MODEL),
        'mlp_w2': nrm(ks[8], (DEPTH, D_FF, D_MODEL), D_FF),
        'ev_w_in': nrm(ks[9], (N_EVEN, D_MODEL, EVEN_IN), D_MODEL),
        'ev_w_out': nrm(ks[10], (N_EVEN, N_HEADS * HEAD_DIM, D_MODEL), N_HEADS * HEAD_DIM),
        'ev_q_norm': 1.0 + 0.02 * jax.random.normal(ks[11], (N_EVEN, HEAD_DIM), f32),
        'ev_k_norm': 1.0 + 0.02 * jax.random.normal(ks[12], (N_EVEN, HEAD_DIM), f32),
        'ev_sink': 0.5 * jax.random.normal(ks[13], (N_EVEN, B_Q_HEADS), f32),
        'od_w_in': nrm(ks[14], (N_ODD, D_MODEL, ODD_IN), D_MODEL),
        'od_w_out': nrm(ks[15], (N_ODD, C_HEADS * HEAD_DIM, D_MODEL), C_HEADS * HEAD_DIM),
        'od_rpb': 0.1 * jax.random.normal(ks[16], (N_ODD, C_HEADS, 2 * NA_KH - 1, 2 * NA_KW - 1), f32),
        'final_norm_w': 1.0 + 0.02 * jax.random.normal(ks[17], (D_MODEL,), f32),
    }


def _fwd_reference(x, c, ctx, c_ctx, ada_w, ada_b, norm_w, mlp_w1, mlp_w2, ev_w_in, ev_w_out,
              ev_q_norm, ev_k_norm, ev_sink, od_w_in, od_w_out, od_rpb, final_norm_w):
    S = x.shape[1]
    cos, sin = _rope_tables(S)
    for i in range(DEPTH):
        need_ctx = i < DEPTH - 1
        mod = (jax.nn.silu(c) @ ada_w[i] + ada_b[i])[:, None, :]
        sh1, sc1, g1, sh2, sc2, g2 = jnp.split(mod, 6, axis=-1)
        modc = jax.nn.silu(c_ctx) @ ada_w[i] + ada_b[i]
        sh1c, sc1c, g1c, sh2c, sc2c, g2c = jnp.split(modc, 6, axis=-1)
        h = _modulate(_rmsnorm(x, norm_w[i, 0]), sh1, sc1)
        hc = _modulate(_rmsnorm(ctx, norm_w[i, 0]), sh1c, sc1c)
        if i % 2 == 0:
            j = i // 2
            y, yc = _even_mixer(h, hc, ev_w_in[j], ev_w_out[j], ev_q_norm[j], ev_k_norm[j],
                                ev_sink[j], cos, sin, need_ctx)
        else:
            j = i // 2
            y, yc = _odd_mixer(h, hc, od_w_in[j], od_w_out[j], od_rpb[j], need_ctx)
        x = x + g1 * y
        h = _modulate(_rmsnorm(x, norm_w[i, 1]), sh2, sc2)
        x = x + g2 * _mlp(h, mlp_w1[i], mlp_w2[i])
        if need_ctx:
            ctx = ctx + g1c * yc
            hc = _modulate(_rmsnorm(ctx, norm_w[i, 1]), sh2c, sc2c)
            ctx = ctx + g2c * _mlp(hc, mlp_w1[i], mlp_w2[i])
    return _rmsnorm(x, final_norm_w)


import jax as _jax
import jax.numpy as _jnp

TWIN_FORMAT = 'train_step'
FWD_PARAMS = ['x', 'c', 'ctx', 'c_ctx', 'ada_w', 'ada_b', 'norm_w', 'mlp_w1', 'mlp_w2', 'ev_w_in', 'ev_w_out', 'ev_q_norm', 'ev_k_norm', 'ev_sink', 'od_w_in', 'od_w_out', 'od_rpb', 'final_norm_w']
TWIN_WEIGHTS = ['c_ctx', 'ada_w', 'ada_b', 'norm_w', 'mlp_w1', 'mlp_w2', 'ev_w_in', 'ev_w_out', 'ev_q_norm', 'ev_k_norm', 'ev_sink', 'od_w_in', 'od_w_out', 'od_rpb', 'final_norm_w']
TWIN_DIFF_INPUT = 'x'
TWIN_INPUTS = ['x', 'c', 'ctx', 'c_ctx', 'ada_w', 'ada_b', 'norm_w', 'mlp_w1', 'mlp_w2', 'ev_w_in', 'ev_w_out', 'ev_q_norm', 'ev_k_norm', 'ev_sink', 'od_w_in', 'od_w_out', 'od_rpb', 'final_norm_w', 'loss_target', 'm_c_ctx', 'm_ada_w', 'm_ada_b', 'm_norm_w', 'm_mlp_w1', 'm_mlp_w2', 'm_ev_w_in', 'm_ev_w_out', 'm_ev_q_norm', 'm_ev_k_norm', 'm_ev_sink', 'm_od_w_in', 'm_od_w_out', 'm_od_rpb', 'm_final_norm_w', 'v_c_ctx', 'v_ada_w', 'v_ada_b', 'v_norm_w', 'v_mlp_w1', 'v_mlp_w2', 'v_ev_w_in', 'v_ev_w_out', 'v_ev_q_norm', 'v_ev_k_norm', 'v_ev_sink', 'v_od_w_in', 'v_od_w_out', 'v_od_rpb', 'v_final_norm_w']
TWIN_OUTPUTS = ['loss', 'grad_x', 'grad_c_ctx', 'grad_ada_w', 'grad_ada_b', 'grad_norm_w', 'grad_mlp_w1', 'grad_mlp_w2', 'grad_ev_w_in', 'grad_ev_w_out', 'grad_ev_q_norm', 'grad_ev_k_norm', 'grad_ev_sink', 'grad_od_w_in', 'grad_od_w_out', 'grad_od_rpb', 'grad_final_norm_w', 'delta_c_ctx', 'delta_ada_w', 'delta_ada_b', 'delta_norm_w', 'delta_mlp_w1', 'delta_mlp_w2', 'delta_ev_w_in', 'delta_ev_w_out', 'delta_ev_q_norm', 'delta_ev_k_norm', 'delta_ev_sink', 'delta_od_w_in', 'delta_od_w_out', 'delta_od_rpb', 'delta_final_norm_w', 'new_m_c_ctx', 'new_m_ada_w', 'new_m_ada_b', 'new_m_norm_w', 'new_m_mlp_w1', 'new_m_mlp_w2', 'new_m_ev_w_in', 'new_m_ev_w_out', 'new_m_ev_q_norm', 'new_m_ev_k_norm', 'new_m_ev_sink', 'new_m_od_w_in', 'new_m_od_w_out', 'new_m_od_rpb', 'new_m_final_norm_w', 'new_v_c_ctx', 'new_v_ada_w', 'new_v_ada_b', 'new_v_norm_w', 'new_v_mlp_w1', 'new_v_mlp_w2', 'new_v_ev_w_in', 'new_v_ev_w_out', 'new_v_ev_q_norm', 'new_v_ev_k_norm', 'new_v_ev_sink', 'new_v_od_w_in', 'new_v_od_w_out', 'new_v_od_rpb', 'new_v_final_norm_w']
TWIN_LEAF_KINDS = {'loss': 'loss', 'grad_x': 'grad_x', 'grad_c_ctx': 'grad_w', 'grad_ada_w': 'grad_w', 'grad_ada_b': 'grad_w', 'grad_norm_w': 'grad_w', 'grad_mlp_w1': 'grad_w', 'grad_mlp_w2': 'grad_w', 'grad_ev_w_in': 'grad_w', 'grad_ev_w_out': 'grad_w', 'grad_ev_q_norm': 'grad_w', 'grad_ev_k_norm': 'grad_w', 'grad_ev_sink': 'grad_w', 'grad_od_w_in': 'grad_w', 'grad_od_w_out': 'grad_w', 'grad_od_rpb': 'grad_w', 'grad_final_norm_w': 'grad_w', 'delta_c_ctx': 'delta_w', 'delta_ada_w': 'delta_w', 'delta_ada_b': 'delta_w', 'delta_norm_w': 'delta_w', 'delta_mlp_w1': 'delta_w', 'delta_mlp_w2': 'delta_w', 'delta_ev_w_in': 'delta_w', 'delta_ev_w_out': 'delta_w', 'delta_ev_q_norm': 'delta_w', 'delta_ev_k_norm': 'delta_w', 'delta_ev_sink': 'delta_w', 'delta_od_w_in': 'delta_w', 'delta_od_w_out': 'delta_w', 'delta_od_rpb': 'delta_w', 'delta_final_norm_w': 'delta_w', 'new_m_c_ctx': 'new_m', 'new_m_ada_w': 'new_m', 'new_m_ada_b': 'new_m', 'new_m_norm_w': 'new_m', 'new_m_mlp_w1': 'new_m', 'new_m_mlp_w2': 'new_m', 'new_m_ev_w_in': 'new_m', 'new_m_ev_w_out': 'new_m', 'new_m_ev_q_norm': 'new_m', 'new_m_ev_k_norm': 'new_m', 'new_m_ev_sink': 'new_m', 'new_m_od_w_in': 'new_m', 'new_m_od_w_out': 'new_m', 'new_m_od_rpb': 'new_m', 'new_m_final_norm_w': 'new_m', 'new_v_c_ctx': 'new_v', 'new_v_ada_w': 'new_v', 'new_v_ada_b': 'new_v', 'new_v_norm_w': 'new_v', 'new_v_mlp_w1': 'new_v', 'new_v_mlp_w2': 'new_v', 'new_v_ev_w_in': 'new_v', 'new_v_ev_w_out': 'new_v', 'new_v_ev_q_norm': 'new_v', 'new_v_ev_k_norm': 'new_v', 'new_v_ev_sink': 'new_v', 'new_v_od_w_in': 'new_v', 'new_v_od_w_out': 'new_v', 'new_v_od_rpb': 'new_v', 'new_v_final_norm_w': 'new_v'}


def _forward(args):
    return _fwd_reference(*[args[k] for k in FWD_PARAMS])


def _output_shape():
    def fwd():
        inp = _fwd_setup_inputs(0)
        return _fwd_reference(*[inp[k] for k in FWD_PARAMS])
    out = _jax.eval_shape(fwd)
    return out.shape, out.dtype

N_MICROBATCH = 1
ADAM_LR = 0.001
ADAM_B1 = 0.9
ADAM_B2 = 0.999
ADAM_EPS = 1e-08
ADAM_WD = 0.01
ADAM_STEP = 10
PER_EXAMPLE_BATCH_AXIS = {'x': 0, 'c': 0, 'ctx': 0, 'loss_target': 0}
SHARED_INPUTS = []
_WEIGHT_DTYPES = {'c_ctx': _jnp.float32, 'ada_w': _jnp.float32, 'ada_b': _jnp.float32, 'norm_w': _jnp.float32, 'mlp_w1': _jnp.float32, 'mlp_w2': _jnp.float32, 'ev_w_in': _jnp.float32, 'ev_w_out': _jnp.float32, 'ev_q_norm': _jnp.float32, 'ev_k_norm': _jnp.float32, 'ev_sink': _jnp.float32, 'od_w_in': _jnp.float32, 'od_w_out': _jnp.float32, 'od_rpb': _jnp.float32, 'final_norm_w': _jnp.float32}
MOMENT_SCALE = {'c_ctx': 9.117042e-03, 'ada_w': 3.757914e-02, 'ada_b': 6.987637e-02, 'norm_w': 2.727742e-02, 'mlp_w1': 1.984103e-02, 'mlp_w2': 3.709829e-02, 'ev_w_in': 6.871573e-03, 'ev_w_out': 7.866091e-03, 'ev_q_norm': 5.916917e-03, 'ev_k_norm': 5.722113e-03, 'ev_sink': 1.220412e-04, 'od_w_in': 5.268580e-03, 'od_w_out': 7.742440e-03, 'od_rpb': 8.888776e-04, 'final_norm_w': 1.609855e+01}


def _to_microbatches(a, axis):
    t = _jnp.moveaxis(a, axis, 0)
    t = t.reshape((N_MICROBATCH, t.shape[0] // N_MICROBATCH) + t.shape[1:])
    return _jnp.moveaxis(t, 1, axis + 1)


def setup_inputs(seed: int = 0) -> dict:
    inp = _fwd_setup_inputs(seed)
    key = _jax.random.fold_in(_jax.random.key(seed), 7919)
    shape, _ = _output_shape()
    out = dict(inp)
    out["loss_target"] = _jax.random.normal(_jax.random.fold_in(key, 0), shape, _jnp.float32)
    for i, name in enumerate(TWIN_WEIGHTS):
        w = inp[name].astype(_jnp.float32)
        if MOMENT_SCALE is None:
            s = _jnp.sqrt(_jnp.mean(_jnp.square(w)) + 1e-30)
        else:
            s = MOMENT_SCALE[name]
        km, kv = _jax.random.split(_jax.random.fold_in(key, i + 1))
        out[name] = w
        out["m_" + name] = s * _jax.random.normal(km, w.shape, _jnp.float32)
        out["v_" + name] = (s * s) * _jax.random.uniform(kv, w.shape, _jnp.float32, 0.5, 1.5)
    if N_MICROBATCH > 1:
        for name, axis in PER_EXAMPLE_BATCH_AXIS.items():
            out[name] = _to_microbatches(out[name], axis)
    return {'x': out['x'], 'c': out['c'], 'ctx': out['ctx'], 'c_ctx': out['c_ctx'], 'ada_w': out['ada_w'], 'ada_b': out['ada_b'], 'norm_w': out['norm_w'], 'mlp_w1': out['mlp_w1'], 'mlp_w2': out['mlp_w2'], 'ev_w_in': out['ev_w_in'], 'ev_w_out': out['ev_w_out'], 'ev_q_norm': out['ev_q_norm'], 'ev_k_norm': out['ev_k_norm'], 'ev_sink': out['ev_sink'], 'od_w_in': out['od_w_in'], 'od_w_out': out['od_w_out'], 'od_rpb': out['od_rpb'], 'final_norm_w': out['final_norm_w'], 'loss_target': out['loss_target'], 'm_c_ctx': out['m_c_ctx'], 'm_ada_w': out['m_ada_w'], 'm_ada_b': out['m_ada_b'], 'm_norm_w': out['m_norm_w'], 'm_mlp_w1': out['m_mlp_w1'], 'm_mlp_w2': out['m_mlp_w2'], 'm_ev_w_in': out['m_ev_w_in'], 'm_ev_w_out': out['m_ev_w_out'], 'm_ev_q_norm': out['m_ev_q_norm'], 'm_ev_k_norm': out['m_ev_k_norm'], 'm_ev_sink': out['m_ev_sink'], 'm_od_w_in': out['m_od_w_in'], 'm_od_w_out': out['m_od_w_out'], 'm_od_rpb': out['m_od_rpb'], 'm_final_norm_w': out['m_final_norm_w'], 'v_c_ctx': out['v_c_ctx'], 'v_ada_w': out['v_ada_w'], 'v_ada_b': out['v_ada_b'], 'v_norm_w': out['v_norm_w'], 'v_mlp_w1': out['v_mlp_w1'], 'v_mlp_w2': out['v_mlp_w2'], 'v_ev_w_in': out['v_ev_w_in'], 'v_ev_w_out': out['v_ev_w_out'], 'v_ev_q_norm': out['v_ev_q_norm'], 'v_ev_k_norm': out['v_ev_k_norm'], 'v_ev_sink': out['v_ev_sink'], 'v_od_w_in': out['v_od_w_in'], 'v_od_w_out': out['v_od_w_out'], 'v_od_rpb': out['v_od_rpb'], 'v_final_norm_w': out['v_final_norm_w']}


def _loss(weights, diff, rest, loss_target):
    with _jax.named_scope("forward"):
        args = {**rest, TWIN_DIFF_INPUT: diff, **{k: w.astype(_WEIGHT_DTYPES[k]) for k, w in weights.items()}}
        y = _forward(args)
    with _jax.named_scope("loss_head"):
        err = _jnp.square(y.astype(_jnp.float32) - loss_target)
        return 0.5 * _jnp.sum(_jnp.mean(err, axis=-1)) if err.ndim else 0.5 * err


def _adamw(w, g, m, v):
    m = ADAM_B1 * m + (1.0 - ADAM_B1) * g
    v = ADAM_B2 * v + (1.0 - ADAM_B2) * _jnp.square(g)
    m_hat = m / (1.0 - ADAM_B1 ** ADAM_STEP)
    v_hat = v / (1.0 - ADAM_B2 ** ADAM_STEP)
    delta = -ADAM_LR * (m_hat / (_jnp.sqrt(v_hat) + ADAM_EPS) + ADAM_WD * w)
    return delta, m, v


def reference(x, c, ctx, c_ctx, ada_w, ada_b, norm_w, mlp_w1, mlp_w2, ev_w_in, ev_w_out, ev_q_norm, ev_k_norm, ev_sink, od_w_in, od_w_out, od_rpb, final_norm_w, loss_target, m_c_ctx, m_ada_w, m_ada_b, m_norm_w, m_mlp_w1, m_mlp_w2, m_ev_w_in, m_ev_w_out, m_ev_q_norm, m_ev_k_norm, m_ev_sink, m_od_w_in, m_od_w_out, m_od_rpb, m_final_norm_w, v_c_ctx, v_ada_w, v_ada_b, v_norm_w, v_mlp_w1, v_mlp_w2, v_ev_w_in, v_ev_w_out, v_ev_q_norm, v_ev_k_norm, v_ev_sink, v_od_w_in, v_od_w_out, v_od_rpb, v_final_norm_w):
    given = dict(x=x, c=c, ctx=ctx, c_ctx=c_ctx, ada_w=ada_w, ada_b=ada_b, norm_w=norm_w, mlp_w1=mlp_w1, mlp_w2=mlp_w2, ev_w_in=ev_w_in, ev_w_out=ev_w_out, ev_q_norm=ev_q_norm, ev_k_norm=ev_k_norm, ev_sink=ev_sink, od_w_in=od_w_in, od_w_out=od_w_out, od_rpb=od_rpb, final_norm_w=final_norm_w, loss_target=loss_target, m_c_ctx=m_c_ctx, m_ada_w=m_ada_w, m_ada_b=m_ada_b, m_norm_w=m_norm_w, m_mlp_w1=m_mlp_w1, m_mlp_w2=m_mlp_w2, m_ev_w_in=m_ev_w_in, m_ev_w_out=m_ev_w_out, m_ev_q_norm=m_ev_q_norm, m_ev_k_norm=m_ev_k_norm, m_ev_sink=m_ev_sink, m_od_w_in=m_od_w_in, m_od_w_out=m_od_w_out, m_od_rpb=m_od_rpb, m_final_norm_w=m_final_norm_w, v_c_ctx=v_c_ctx, v_ada_w=v_ada_w, v_ada_b=v_ada_b, v_norm_w=v_norm_w, v_mlp_w1=v_mlp_w1, v_mlp_w2=v_mlp_w2, v_ev_w_in=v_ev_w_in, v_ev_w_out=v_ev_w_out, v_ev_q_norm=v_ev_q_norm, v_ev_k_norm=v_ev_k_norm, v_ev_sink=v_ev_sink, v_od_w_in=v_od_w_in, v_od_w_out=v_od_w_out, v_od_rpb=v_od_rpb, v_final_norm_w=v_final_norm_w)
    weights = {n: given[n] for n in TWIN_WEIGHTS}
    shared = {n: given[n] for n in SHARED_INPUTS}
    per_example = {n: given[n] for n in ['x', 'c', 'ctx']}
    grad_fn = _jax.value_and_grad(_loss, argnums=(0, 1))

    def one_microbatch(ex, loss_target):
        ex = dict(ex)
        diff = ex.pop(TWIN_DIFF_INPUT)
        return grad_fn(weights, diff, {**shared, **ex}, loss_target)

    if N_MICROBATCH == 1:
        loss, (grad_w, grad_x) = one_microbatch(per_example, given["loss_target"])
    else:
        def body(carry, xs):
            loss_sum, grad_sum = carry
            l_k, (gw_k, gx_k) = one_microbatch(xs[0], xs[1])
            with _jax.named_scope("update"):
                return (loss_sum + l_k, _jax.tree.map(_jnp.add, grad_sum, gw_k)), gx_k

        init = (_jnp.zeros((), _jnp.float32), _jax.tree.map(_jnp.zeros_like, weights))
        (loss, grad_w), grad_x = _jax.lax.scan(body, init, (per_example, given["loss_target"]))
    with _jax.named_scope("update"):
        delta_w, new_m, new_v = {}, {}, {}
        for n in TWIN_WEIGHTS:
            delta_w[n], new_m[n], new_v[n] = _adamw(weights[n], grad_w[n], given["m_" + n], given["v_" + n])
    return (loss, grad_x, *[grad_w[n] for n in TWIN_WEIGHTS], *[delta_w[n] for n in TWIN_WEIGHTS],
            *[new_m[n] for n in TWIN_WEIGHTS], *[new_v[n] for n in TWIN_WEIGHTS])
```

```python
import numpy as np
import jax
import jax.numpy as jnp
from jax import lax
from jax.experimental import pallas as pl
from jax.experimental.pallas import tpu as pltpu

F32 = jnp.float32
BF16 = jnp.bfloat16
MESH = pl.DeviceIdType.MESH

NDEV = 8
HEAD = 128
GRID_W = 64
NA_KH, NA_KW = 8, 16
WINDOW = 128
ROPE_THETA = 10000.0
EPS = 1e-6
NEG = -1e30
SCALE = HEAD ** -0.5
ROW_TILE = 256
VMEM_LIMIT = 56 * 1024 * 1024

ADAM_LR, ADAM_B1, ADAM_B2, ADAM_EPS, ADAM_WD, ADAM_STEP = 0.001, 0.9, 0.999, 1e-08, 0.01, 10

NT = (((1,), (1,)), ((), ()))
NN = (((1,), (0,)), ((), ()))
TN = (((0,), (0,)), ((), ()))


def _cparams(sem):
    return pltpu.CompilerParams(dimension_semantics=sem, vmem_limit_bytes=VMEM_LIMIT)


def _tile(n, cap):
    if n <= cap:
        return n
    t = cap - cap % 64
    while t >= 64:
        if n % t == 0:
            return t
        t -= 64
    raise ValueError((n, cap))


def _dot(a, b, dims):
    return lax.dot_general(a.astype(BF16), b.astype(BF16), dims, preferred_element_type=F32)


def _slot(d):
    return (d % 2) * 4 + d // 2


def _mm_core(name, grid, ins, in_specs, out_shape, out_specs, dims, acc_shape, epi):
    nk = grid[2]
    n_extra = len(ins) - 2

    def body(*refs):
        a_ref, b_ref = refs[0], refs[1]
        ex = refs[2:2 + n_extra]
        outs = refs[2 + n_extra:-1]
        acc = refs[-1]
        k = pl.program_id(2)

        @pl.when(k == 0)
        def _():
            acc[...] = jnp.zeros_like(acc)

        acc[...] += _dot(a_ref[...], b_ref[...], dims)

        @pl.when(k == nk - 1)
        def _():
            epi(acc[...], ex, outs)

    return pl.pallas_call(
        body, grid=grid, in_specs=in_specs, out_specs=out_specs, out_shape=out_shape,
        scratch_shapes=[pltpu.VMEM(acc_shape, F32)],
        compiler_params=_cparams(("parallel", "parallel", "arbitrary")), name=name)(*ins)


def _epi_store(dtype):
    def epi(acc, ex, outs):
        outs[0][...] = acc.astype(dtype)
    return epi


def _epi_bias(acc, ex, outs):
    outs[0][...] = acc + ex[0][...]


def _epi_relu2(acc, ex, outs):
    r = jnp.maximum(acc, 0.0)
    outs[0][...] = (r * r).astype(BF16)
    outs[1][...] = r.astype(BF16)


def _epi_mul2r(acc, ex, outs):
    outs[0][...] = (acc * (2.0 * ex[0][...].astype(F32))).astype(BF16)


def _epi_resid_gate(nctx, tm):
    def epi(acc, ex, outs):
        rows = pl.program_id(0) * tm + lax.broadcasted_iota(jnp.int32, (tm, 1), 0)
        g = jnp.where(rows < nctx, ex[1][0:1, :], ex[1][1:2, :])
        outs[0][...] = ex[0][...] + g * acc
        outs[1][...] = acc
    return epi


def mm_nn(name, a, w, epi, outs, extras=(), extra_kinds=(), tm_cap=1100, tn_cap=512, tk_cap=2048):
    M, K = a.shape
    if w.ndim == 3:
        ns = w.shape[2]
        N = NDEV * ns
        tn = _tile(ns, tn_cap)
        nper = ns // tn
    else:
        N = w.shape[1]
        tn = _tile(N, tn_cap)
    tm = _tile(M, tm_cap)
    tk = _tile(K, tk_cap)
    grid = (M // tm, N // tn, K // tk)
    a_spec = pl.BlockSpec((tm, tk), lambda i, j, k: (i, k))
    if w.ndim == 3:
        b_spec = pl.BlockSpec((None, tk, tn), lambda i, j, k: (j // nper, k, j % nper))
    else:
        b_spec = pl.BlockSpec((tk, tn), lambda i, j, k: (k, j))
    ex_specs = []
    for e, kind in zip(extras, extra_kinds):
        if kind == 'mn':
            ex_specs.append(pl.BlockSpec((tm, tn), lambda i, j, k: (i, j)))
        else:
            ex_specs.append(pl.BlockSpec((e.shape[0], tn), lambda i, j, k: (0, j)))
    out_shape = [jax.ShapeDtypeStruct((M, N), dt) for dt in outs]
    out_specs = [pl.BlockSpec((tm, tn), lambda i, j, k: (i, j)) for _ in outs]
    return _mm_core(name, grid, (a, w, *extras), [a_spec, b_spec, *ex_specs], out_shape, out_specs, NN, (tm, tn), epi)


def mm_nt(name, a, w, epi, out_dtype, extras=(), tm_cap=1100, to_cap=512, tc_cap=2048):
    M, N = a.shape
    tm = _tile(M, tm_cap)
    if w.ndim == 3:
        Kw, ns = w.shape[1], w.shape[2]
        tc = _tile(ns, tc_cap)
        cper = ns // tc
    else:
        Kw = w.shape[0]
        tc = _tile(N, tc_cap)
    to = _tile(Kw, to_cap)
    grid = (M // tm, Kw // to, N // tc)
    a_spec = pl.BlockSpec((tm, tc), lambda i, j, k: (i, k))
    if w.ndim == 3:
        b_spec = pl.BlockSpec((None, to, tc), lambda i, j, k: (k // cper, j, k % cper))
    else:
        b_spec = pl.BlockSpec((to, tc), lambda i, j, k: (j, k))
    ex_specs = [pl.BlockSpec((tm, to), lambda i, j, k: (i, j)) for _ in extras]
    out_shape = [jax.ShapeDtypeStruct((M, Kw), out_dtype)]
    out_specs = [pl.BlockSpec((tm, to), lambda i, j, k: (i, j))]
    return _mm_core(name, grid, (a, w, *extras), [a_spec, b_spec, *ex_specs], out_shape, out_specs, NT, (tm, to), epi)[0]


def mm_tn(name, a, b, shard_axis, to_cap=1024, tn_cap=512, tc_cap=1100):
    M, Ka = a.shape
    N = b.shape[1]
    tc = _tile(M, tc_cap)
    if shard_axis is None:
        to, tn = _tile(Ka, to_cap), _tile(N, tn_cap)
        shape = (Ka, N)
        oblk = (to, tn)
        omap = lambda i, j, k: (i, j)
    elif shard_axis == 1:
        ns = N // NDEV
        to, tn = _tile(Ka, to_cap), _tile(ns, tn_cap)
        per = ns // tn
        shape = (NDEV, Ka, ns)
        oblk = (None, to, tn)
        omap = lambda i, j, k: (_slot(j // per), i, j % per)
    else:
        rs = Ka // NDEV
        to, tn = _tile(rs, to_cap), _tile(N, tn_cap)
        per = rs // to
        shape = (NDEV, rs, N)
        oblk = (None, to, tn)
        omap = lambda i, j, k: (_slot(i // per), i % per, j)
    grid = (Ka // to, N // tn, M // tc)
    a_spec = pl.BlockSpec((tc, to), lambda i, j, k: (k, i))
    b_spec = pl.BlockSpec((tc, tn), lambda i, j, k: (k, j))
    out_shape = [jax.ShapeDtypeStruct(shape, F32)]
    out_specs = [pl.BlockSpec(oblk, omap)]
    return _mm_core(name, grid, (a, b), [a_spec, b_spec], out_shape, out_specs, TN, (to, tn), _epi_store(F32))[0]


def _row_spec(D):
    return pl.BlockSpec((ROW_TILE, D), lambda i: (i, 0))


def _const_spec(r, D):
    return pl.BlockSpec((r, D), lambda i: (0, 0))


def _grp(ref, is_ctx):
    return jnp.where(is_ctx, ref[0:1, :], ref[1:2, :])


def norm_mod(name, x, nw, sh, sc, nctx):
    R, D = x.shape
    assert R % ROW_TILE == 0 and nctx % ROW_TILE == 0

    def body(x_ref, nw_ref, sh_ref, sc_ref, o_ref):
        is_ctx = pl.program_id(0) * ROW_TILE < nctx
        xv = x_ref[...]
        rstd = lax.rsqrt(jnp.mean(xv * xv, axis=-1, keepdims=True) + EPS)
        n = xv * rstd * nw_ref[...]
        o_ref[...] = (n * (1.0 + _grp(sc_ref, is_ctx)) + _grp(sh_ref, is_ctx)).astype(BF16)

    return pl.pallas_call(
        body, grid=(R // ROW_TILE,),
        in_specs=[_row_spec(D), _const_spec(1, D), _const_spec(2, D), _const_spec(2, D)],
        out_specs=_row_spec(D), out_shape=jax.ShapeDtypeStruct((R, D), BF16),
        compiler_params=_cparams(("parallel",)), name=name)(x, nw, sh, sc)


def norm_bwd(name, x, dh, dres, nw, sc, nctx):
    R, D = x.shape
    assert R % ROW_TILE == 0 and nctx % ROW_TILE == 0

    def body(x_ref, dh_ref, dres_ref, nw_ref, sc_ref, dx_ref, part_ref):
        i = pl.program_id(0)
        is_ctx = i * ROW_TILE < nctx

        @pl.when(i == 0)
        def _():
            part_ref[...] = jnp.zeros_like(part_ref)

        xv = x_ref[...]
        dhv = dh_ref[...]
        w = nw_ref[...]
        rstd = lax.rsqrt(jnp.mean(xv * xv, axis=-1, keepdims=True) + EPS)
        xhat = xv * rstd
        n = xhat * w
        dn = dhv * (1.0 + _grp(sc_ref, is_ctx))
        dxhat = dn * w
        dx_ref[...] = dres_ref[...] + rstd * (dxhat - xhat * jnp.mean(dxhat * xhat, axis=-1, keepdims=True))
        s_sh = jnp.sum(dhv, axis=0, keepdims=True)
        s_sc = jnp.sum(dhv * n, axis=0, keepdims=True)
        s_nw = jnp.sum(dn * xhat, axis=0, keepdims=True)
        zero = jnp.zeros_like(s_sh)
        part_ref[0:1, :] += jnp.where(is_ctx, s_sh, zero)
        part_ref[1:2, :] += jnp.where(is_ctx, zero, s_sh)
        part_ref[2:3, :] += jnp.where(is_ctx, s_sc, zero)
        part_ref[3:4, :] += jnp.where(is_ctx, zero, s_sc)
        part_ref[4:5, :] += s_nw

    return pl.pallas_call(
        body, grid=(R // ROW_TILE,),
        in_specs=[_row_spec(D), _row_spec(D), _row_spec(D), _const_spec(1, D), _const_spec(2, D)],
        out_specs=[_row_spec(D), _const_spec(8, D)],
        out_shape=[jax.ShapeDtypeStruct((R, D), F32), jax.ShapeDtypeStruct((8, D), F32)],
        compiler_params=_cparams(("arbitrary",)), name=name)(x, dh, dres, nw, sc)


def gate_bwd(name, dx, y, g, nctx):
    R, D = dx.shape
    assert R % ROW_TILE == 0 and nctx % ROW_TILE == 0

    def body(dx_ref, y_ref, g_ref, dy_ref, part_ref):
        i = pl.program_id(0)
        is_ctx = i * ROW_TILE < nctx

        @pl.when(i == 0)
        def _():
            part_ref[...] = jnp.zeros_like(part_ref)

        dxv = dx_ref[...]
        dy_ref[...] = (dxv * _grp(g_ref, is_ctx)).astype(BF16)
        s = jnp.sum(dxv * y_ref[...], axis=0, keepdims=True)
        zero = jnp.zeros_like(s)
        part_ref[0:1, :] += jnp.where(is_ctx, s, zero)
        part_ref[1:2, :] += jnp.where(is_ctx, zero, s)

    return pl.pallas_call(
        body, grid=(R // ROW_TILE,),
        in_specs=[_row_spec(D), _row_spec(D), _const_spec(2, D)],
        out_specs=[_row_spec(D), _const_spec(8, D)],
        out_shape=[jax.ShapeDtypeStruct((R, D), BF16), jax.ShapeDtypeStruct((8, D), F32)],
        compiler_params=_cparams(("arbitrary",)), name=name)(dx, y, g)


def final_loss(name, x, fw, tgt):
    S, D = x.shape

    def body(x_ref, fw_ref, t_ref, dx_ref, loss_ref, dfw_ref):
        i = pl.program_id(0)

        @pl.when(i == 0)
        def _():
            loss_ref[...] = jnp.zeros_like(loss_ref)
            dfw_ref[...] = jnp.zeros_like(dfw_ref)

        xv = x_ref[...]
        w = fw_ref[...]
        rstd = lax.rsqrt(jnp.mean(xv * xv, axis=-1, keepdims=True) + EPS)
        xhat = xv * rstd
        e = xhat * w - t_ref[...]
        loss_ref[...] += 0.5 * jnp.sum(jnp.mean(e * e, axis=-1, keepdims=True))
        dout = e * (1.0 / D)
        dfw_ref[0:1, :] += jnp.sum(dout * xhat, axis=0, keepdims=True)
        dxhat = dout * w
        dx_ref[...] = rstd * (dxhat - xhat * jnp.mean(dxhat * xhat, axis=-1, keepdims=True))

    return pl.pallas_call(
        body, grid=(S // ROW_TILE,),
        in_specs=[_row_spec(D), _const_spec(1, D), _row_spec(D)],
        out_specs=[_row_spec(D), pl.BlockSpec((8, 128), lambda i: (0, 0)), _const_spec(8, D)],
        out_shape=[jax.ShapeDtypeStruct((S, D), F32), jax.ShapeDtypeStruct((8, 128), F32),
                   jax.ShapeDtypeStruct((8, D), F32)],
        compiler_params=_cparams(("arbitrary",)), name=name)(x, fw, tgt)


def _rope(x, cos, sa, sb):
    return x * cos + pltpu.roll(x, 96, 1) * sa + pltpu.roll(x, 32, 1) * sb


def _rope_t(dy, cos, sa, sb):
    return dy * cos + pltpu.roll(dy * sa, 32, 1) + pltpu.roll(dy * sb, 96, 1)


def prep_even(name, qkv, qn, kn, cos, sa, sb):
    T = qkv.shape[0]
    nh = qkv.shape[1] // HEAD

    def body(x_ref, qn_ref, kn_ref, cos_ref, sa_ref, sb_ref, o_ref):
        j = pl.program_id(1)
        is_qa = j < 8
        do_norm = j < 10
        is_v = jnp.logical_or(jnp.logical_and(j >= 10, j < 12), j >= 22)
        x = x_ref[...]
        rstd = lax.rsqrt(jnp.mean(x * x, axis=-1, keepdims=True) + EPS)
        w = jnp.where(is_qa, qn_ref[...], kn_ref[...])
        x1 = jnp.where(do_norm, x * rstd * w, x)
        y = jnp.where(is_v, x1, _rope(x1, cos_ref[...], sa_ref[...], sb_ref[...]))
        o_ref[...] = y.astype(BF16)

    blk = pl.BlockSpec((ROW_TILE, HEAD), lambda i, j: (i, j))
    tab = pl.BlockSpec((ROW_TILE, HEAD), lambda i, j: (i, 0))
    one = pl.BlockSpec((1, HEAD), lambda i, j: (0, 0))
    return pl.pallas_call(
        body, grid=(T // ROW_TILE, nh), in_specs=[blk, one, one, tab, tab, tab], out_specs=blk,
        out_shape=jax.ShapeDtypeStruct(qkv.shape, BF16),
        compiler_params=_cparams(("parallel", "parallel")), name=name)(qkv, qn, kn, cos, sa, sb)


def prep_even_bwd(name, qkv, dq, dk, dv, qn, kn, cos, sa, sb):
    T = qkv.shape[0]
    nh = qkv.shape[1] // HEAD

    def body(x_ref, dq_ref, dk_ref, dv_ref, qn_ref, kn_ref, cos_ref, sa_ref, sb_ref, o_ref, part_ref):
        i, j = pl.program_id(0), pl.program_id(1)

        @pl.when(jnp.logical_and(i == 0, j == 0))
        def _():
            part_ref[...] = jnp.zeros_like(part_ref)

        is_qa = j < 8
        is_ka = jnp.logical_and(j >= 8, j < 10)
        do_norm = j < 10
        is_q = jnp.logical_or(is_qa, jnp.logical_and(j >= 12, j < 20))
        is_k = jnp.logical_or(is_ka, jnp.logical_and(j >= 20, j < 22))
        is_v = jnp.logical_not(jnp.logical_or(is_q, is_k))
        dy = jnp.where(is_q, dq_ref[...], jnp.where(is_k, dk_ref[...], dv_ref[...]))
        dn = jnp.where(is_v, dy, _rope_t(dy, cos_ref[...], sa_ref[...], sb_ref[...]))
        x = x_ref[...]
        rstd = lax.rsqrt(jnp.mean(x * x, axis=-1, keepdims=True) + EPS)
        xhat = x * rstd
        w = jnp.where(is_qa, qn_ref[...], kn_ref[...])
        dxhat = dn * w
        dx_n = rstd * (dxhat - xhat * jnp.mean(dxhat * xhat, axis=-1, keepdims=True))
        o_ref[...] = jnp.where(do_norm, dx_n, dn).astype(BF16)
        s = jnp.sum(dn * xhat, axis=0, keepdims=True)
        zero = jnp.zeros_like(s)
        part_ref[0:1, :] += jnp.where(is_qa, s, zero)
        part_ref[1:2, :] += jnp.where(is_ka, s, zero)

    def dq_map(i, j):
        return (i, jnp.where(j < 8, j, jnp.clip(j - 4, 8, 15)))

    def dkv_map(first_a, first_b):
        def m(i, j):
            return (i, jnp.where(j < 12, jnp.clip(j - first_a, 0, 1), 2 + jnp.clip(j - first_b, 0, 1)))
        return m

    blk = pl.BlockSpec((ROW_TILE, HEAD), lambda i, j: (i, j))
    tab = pl.BlockSpec((ROW_TILE, HEAD), lambda i, j: (i, 0))
    one = pl.BlockSpec((1, HEAD), lambda i, j: (0, 0))
    return pl.pallas_call(
        body, grid=(T // ROW_TILE, nh),
        in_specs=[blk, pl.BlockSpec((ROW_TILE, HEAD), dq_map), pl.BlockSpec((ROW_TILE, HEAD), dkv_map(8, 20)),
                  pl.BlockSpec((ROW_TILE, HEAD), dkv_map(10, 22)), one, one, tab, tab, tab],
        out_specs=[blk, pl.BlockSpec((8, HEAD), lambda i, j: (0, 0))],
        out_shape=[jax.ShapeDtypeStruct(qkv.shape, BF16), jax.ShapeDtypeStruct((8, HEAD), F32)],
        compiler_params=_cparams(("arbitrary", "arbitrary")), name=name)(qkv, dq, dk, dv, qn, kn, cos, sa, sb)


def _even_maps():
    qmap = lambda h, qb: (qb, jnp.where(h < 8, h, h + 4))
    kmap = lambda h, qb: (0, jnp.where(h < 8, 8 + h // 4, 18 + h // 4))
    vmap = lambda h, qb: (0, jnp.where(h < 8, 10 + h // 4, 20 + h // 4))
    return qmap, kmap, vmap


def _softmax_parts(parts, extra=None):
    m = parts[0].max(axis=-1, keepdims=True)
    for p in parts[1:]:
        m = jnp.maximum(m, p.max(axis=-1, keepdims=True))
    if extra is not None:
        m = jnp.maximum(m, extra)
    es = [jnp.exp(p - m) for p in parts]
    l = es[0].sum(axis=-1, keepdims=True)
    for e in es[1:]:
        l = l + e.sum(axis=-1, keepdims=True)
    ex = None
    if extra is not None:
        ex = jnp.exp(extra - m)
        l = l + ex
    inv = 1.0 / l
    return [e * inv for e in es], (None if ex is None else ex * inv)


def _win_scores(q, k_ref, qb, tq, nctx, S):
    L = tq + 2 * WINDOW
    nqc = nctx // tq
    qlat = (qb - nqc) * tq
    start = pl.multiple_of(jnp.clip(qlat - WINDOW, 0, S - L), 128)
    kc = k_ref[0:nctx, :]
    kw = k_ref[pl.ds(nctx + start, L), :]
    s_c = _dot(q, kc, NT) * SCALE
    s_w = _dot(q, kw, NT) * SCALE
    qpos = qlat + lax.broadcasted_iota(jnp.int32, (tq, 1), 0)
    kpos = start + lax.broadcasted_iota(jnp.int32, (1, L), 1)
    valid = jnp.logical_and(jnp.abs(kpos - qpos) <= WINDOW, qb >= nqc)
    return s_c, jnp.where(valid, s_w, NEG), start, L


def _glob_scores(q, k_ref, qb, tq, nctx):
    T = k_ref.shape[0]
    s = _dot(q, k_ref[...], NT) * SCALE
    col = lax.broadcasted_iota(jnp.int32, (1, T), 1)
    return jnp.where(jnp.logical_and(qb < nctx // tq, col >= nctx), NEG, s)


def attn_even_fwd(name, qkvh, sink, nctx, tq):
    T = qkvh.shape[0]
    S = T - nctx
    qmap, kmap, vmap = _even_maps()

    def body(sink_ref, q_ref, k_ref, v_ref, o_ref):
        h, qb = pl.program_id(0), pl.program_id(1)
        q = q_ref[...]

        @pl.when(h < 8)
        def _():
            (p,), _ = _softmax_parts([_glob_scores(q, k_ref, qb, tq, nctx)])
            o_ref[...] = _dot(p, v_ref[...], NN).astype(BF16)

        @pl.when(h >= 8)
        def _():
            s_c, s_w, start, L = _win_scores(q, k_ref, qb, tq, nctx, S)
            sk = jnp.full((tq, 1), sink_ref[jnp.maximum(h - 8, 0)], F32)
            (p_c, p_w), _ = _softmax_parts([s_c, s_w], sk)
            o = _dot(p_c, v_ref[0:nctx, :], NN) + _dot(p_w, v_ref[pl.ds(nctx + start, L), :], NN)
            o_ref[...] = o.astype(BF16)

    return pl.pallas_call(
        body, grid=(16, T // tq),
        in_specs=[pl.BlockSpec(memory_space=pltpu.SMEM), pl.BlockSpec((tq, HEAD), qmap),
                  pl.BlockSpec((T, HEAD), kmap), pl.BlockSpec((T, HEAD), vmap)],
        out_specs=pl.BlockSpec((tq, HEAD), lambda h, qb: (qb, h)),
        out_shape=jax.ShapeDtypeStruct((T, 16 * HEAD), BF16),
        compiler_params=_cparams(("parallel", "arbitrary")), name=name)(sink, qkvh, qkvh, qkvh)


def attn_even_bwd(name, qkvh, sink, do, nctx, tq):
    T = qkvh.shape[0]
    S = T - nctx
    qmap, kmap, vmap = _even_maps()

    def body(sink_ref, q_ref, k_ref, v_ref, do_ref, dq_ref, dk_ref, dv_ref, ds_ref):
        h, qb = pl.program_id(0), pl.program_id(1)
        q = q_ref[...]
        dov = do_ref[...]

        @pl.when(jnp.logical_and(h % 4 == 0, qb == 0))
        def _():
            dk_ref[...] = jnp.zeros_like(dk_ref)
            dv_ref[...] = jnp.zeros_like(dv_ref)

        @pl.when(qb == 0)
        def _():
            ds_ref[...] = jnp.zeros_like(ds_ref)

        @pl.when(h < 8)
        def _():
            (p,), _ = _softmax_parts([_glob_scores(q, k_ref, qb, tq, nctx)])
            dp = _dot(dov, v_ref[...], NT)
            row = jnp.sum(p * dp, axis=-1, keepdims=True)
            dsb = (p * (dp - row) * SCALE).astype(BF16)
            dq_ref[...] = _dot(dsb, k_ref[...], NN)
            dk_ref[...] += _dot(dsb, q, TN)
            dv_ref[...] += _dot(p, dov, TN)

        @pl.when(h >= 8)
        def _():
            s_c, s_w, start, L = _win_scores(q, k_ref, qb, tq, nctx, S)
            sk = jnp.full((tq, 1), sink_ref[jnp.maximum(h - 8, 0)], F32)
            (p_c, p_w), p_s = _softmax_parts([s_c, s_w], sk)
            win = pl.ds(nctx + start, L)
            dp_c = _dot(dov, v_ref[0:nctx, :], NT)
            dp_w = _dot(dov, v_ref[win, :], NT)
            row = jnp.sum(p_c * dp_c, axis=-1, keepdims=True) + jnp.sum(p_w * dp_w, axis=-1, keepdims=True)
            ds_c = (p_c * (dp_c - row) * SCALE).astype(BF16)
            ds_w = (p_w * (dp_w - row) * SCALE).astype(BF16)
            dq_ref[...] = _dot(ds_c, k_ref[0:nctx, :], NN) + _dot(ds_w, k_ref[win, :], NN)
            dk_ref[0:nctx, :] += _dot(ds_c, q, TN)
            dk_ref[win, :] += _dot(ds_w, q, TN)
            dv_ref[0:nctx, :] += _dot(p_c, dov, TN)
            dv_ref[win, :] += _dot(p_w, dov, TN)
            ds_ref[...] += jnp.sum(-(p_s * row))

    kv_out = pl.BlockSpec((T, HEAD), lambda h, qb: (0, h // 4))
    return pl.pallas_call(
        body, grid=(16, T // tq),
        in_specs=[pl.BlockSpec(memory_space=pltpu.SMEM), pl.BlockSpec((tq, HEAD), qmap),
                  pl.BlockSpec((T, HEAD), kmap), pl.BlockSpec((T, HEAD), vmap),
                  pl.BlockSpec((tq, HEAD), lambda h, qb: (qb, h))],
        out_specs=[pl.BlockSpec((tq, HEAD), lambda h, qb: (qb, h)), kv_out, kv_out,
                   pl.BlockSpec((None, 8, 128), lambda h, qb: (h, 0, 0))],
        out_shape=[jax.ShapeDtypeStruct((T, 16 * HEAD), F32), jax.ShapeDtypeStruct((T, 4 * HEAD), F32),
                   jax.ShapeDtypeStruct((T, 4 * HEAD), F32), jax.ShapeDtypeStruct((16, 8, 128), F32)],
        compiler_params=_cparams(("arbitrary", "arbitrary")), name=name)(sink, qkvh, qkvh, qkvh, do)


def _na_row_start(r, rows):
    return jnp.clip(r - NA_KH // 2, 0, rows - NA_KH)


def _na_scores(q, k_ref, bias_ref, r, rows, nctx):
    LW = NA_KH * GRID_W
    rs = _na_row_start(r, rows)
    win = pl.ds(pl.multiple_of(nctx + rs * GRID_W, GRID_W), LW)
    s_c = _dot(q, k_ref[0:nctx, :], NT) * SCALE
    s_w = _dot(q, k_ref[win, :], NT) * SCALE + bias_ref[...]
    return s_c, s_w, win


def attn_odd_fwd(name, qkv, bias8, nctx):
    T = qkv.shape[0]
    S = T - nctx
    rows = S // GRID_W
    LW = NA_KH * GRID_W
    qoff = nctx // GRID_W

    def body(q_ref, k_ref, v_ref, b_ref, o_ref):
        r = pl.program_id(1)
        s_c, s_w, win = _na_scores(q_ref[...], k_ref, b_ref, r, rows, nctx)
        (p_c, p_w), _ = _softmax_parts([s_c, s_w])
        o_ref[...] = (_dot(p_c, v_ref[0:nctx, :], NN) + _dot(p_w, v_ref[win, :], NN)).astype(BF16)

    return pl.pallas_call(
        body, grid=(16, rows),
        in_specs=[pl.BlockSpec((GRID_W, HEAD), lambda h, r: (r + qoff, h)),
                  pl.BlockSpec((T, HEAD), lambda h, r: (0, 16 + h)),
                  pl.BlockSpec((T, HEAD), lambda h, r: (0, 32 + h)),
                  pl.BlockSpec((None, None, GRID_W, LW), lambda h, r: (h, r - _na_row_start(r, rows), 0, 0))],
        out_specs=pl.BlockSpec((GRID_W, HEAD), lambda h, r: (r, h)),
        out_shape=jax.ShapeDtypeStruct((S, 16 * HEAD), BF16),
        compiler_params=_cparams(("parallel", "arbitrary")), name=name)(qkv, qkv, qkv, bias8)


def attn_odd_bwd(name, qkv, bias8, do, nctx):
    T = qkv.shape[0]
    S = T - nctx
    rows = S // GRID_W
    LW = NA_KH * GRID_W
    qoff = nctx // GRID_W

    def off(r):
        return r - _na_row_start(r, rows)

    def body(q_ref, k_ref, v_ref, b_ref, do_ref, dq_ref, dk_ref, dv_ref, db_ref):
        r = pl.program_id(1)
        q = q_ref[...]
        dov = do_ref[...]

        @pl.when(r == 0)
        def _():
            dk_ref[...] = jnp.zeros_like(dk_ref)
            dv_ref[...] = jnp.zeros_like(dv_ref)

        s_c, s_w, win = _na_scores(q, k_ref, b_ref, r, rows, nctx)
        (p_c, p_w), _ = _softmax_parts([s_c, s_w])
        dp_c = _dot(dov, v_ref[0:nctx, :], NT)
        dp_w = _dot(dov, v_ref[win, :], NT)
        row = jnp.sum(p_c * dp_c, axis=-1, keepdims=True) + jnp.sum(p_w * dp_w, axis=-1, keepdims=True)
        dsw = p_w * (dp_w - row)
        first = jnp.logical_or(r == 0, off(r) != off(jnp.maximum(r - 1, 0)))

        @pl.when(first)
        def _():
            db_ref[...] = dsw

        @pl.when(jnp.logical_not(first))
        def _():
            db_ref[...] += dsw

        ds_c = (p_c * (dp_c - row) * SCALE).astype(BF16)
        ds_w = (dsw * SCALE).astype(BF16)
        dq_ref[...] = _dot(ds_c, k_ref[0:nctx, :], NN) + _dot(ds_w, k_ref[win, :], NN)
        dk_ref[0:nctx, :] += _dot(ds_c, q, TN)
        dk_ref[win, :] += _dot(ds_w, q, TN)
        dv_ref[0:nctx, :] += _dot(p_c, dov, TN)
        dv_ref[win, :] += _dot(p_w, dov, TN)

    bspec = pl.BlockSpec((None, None, GRID_W, LW), lambda h, r: (h, off(r), 0, 0))
    kv_out = pl.BlockSpec((T, HEAD), lambda h, r: (0, h))
    return pl.pallas_call(
        body, grid=(16, rows),
        in_specs=[pl.BlockSpec((GRID_W, HEAD), lambda h, r: (r + qoff, h)),
                  pl.BlockSpec((T, HEAD), lambda h, r: (0, 16 + h)),
                  pl.BlockSpec((T, HEAD), lambda h, r: (0, 32 + h)),
                  bspec, pl.BlockSpec((GRID_W, HEAD), lambda h, r: (r, h))],
        out_specs=[pl.BlockSpec((GRID_W, HEAD), lambda h, r: (r, h)), kv_out, kv_out, bspec],
        out_shape=[jax.ShapeDtypeStruct((S, 16 * HEAD), F32), jax.ShapeDtypeStruct((T, 16 * HEAD), F32),
                   jax.ShapeDtypeStruct((T, 16 * HEAD), F32), jax.ShapeDtypeStruct(bias8.shape, F32)],
        compiler_params=_cparams(("arbitrary", "arbitrary")), name=name)(qkv, qkv, qkv, bias8, do)


def _na_onehots():
    o = np.arange(NA_KH)[:, None]
    i = np.arange(NA_KH)[None, :]
    a = i - o + NA_KH - 1
    A = (a[..., None] == np.arange(2 * NA_KH - 1)).astype(np.float32)
    qc = np.arange(GRID_W)[:, None]
    kc = np.arange(GRID_W)[None, :]
    b = np.clip(kc - qc + NA_KW - 1, 0, 2 * NA_KW - 2)
    cs = np.clip(qc - NA_KW // 2, 0, GRID_W - NA_KW)
    valid = (kc >= cs) & (kc < cs + NA_KW)
    B = ((b[..., None] == np.arange(2 * NA_KW - 1)) & valid[..., None]).astype(np.float32)
    return A, B, valid


def na_bias_table(rpb):
    A, B, valid = _na_onehots()
    hp = lax.Precision.HIGHEST
    t = jnp.einsum('hab,oia->hoib', rpb, jnp.asarray(A), precision=hp)
    bias = jnp.einsum('hoib,qkb->hoqik', t, jnp.asarray(B), precision=hp)
    bias = jnp.where(jnp.asarray(valid)[None, None, :, None, :], bias, NEG)
    return bias.reshape(rpb.shape[0], NA_KH, GRID_W, NA_KH * GRID_W)


def na_bias_grad(name, dbias8):
    A, B, _ = _na_onehots()
    H = dbias8.shape[0]
    nb, na = 2 * NA_KW - 1, 2 * NA_KH - 1
    d = dbias8.reshape(H, NA_KH, GRID_W, NA_KH, GRID_W).transpose(0, 1, 3, 2, 4)
    d = d.reshape(H * NA_KH * NA_KH, GRID_W * GRID_W)
    Bp = np.zeros((GRID_W * GRID_W, 128), np.float32)
    Bp[:, :nb] = B.reshape(GRID_W * GRID_W, nb)
    Ap = np.zeros((16, NA_KH * NA_KH), np.float32)
    Ap[:na] = A.reshape(NA_KH * NA_KH, na).T
    rows_per_head = NA_KH * NA_KH

    def split3(x):
        hi = x.astype(BF16)
        r1 = x - hi.astype(F32)
        mid = r1.astype(BF16)
        return hi, mid, (r1 - mid.astype(F32)).astype(BF16)

    def body(d_ref, b_ref, a_ref, o_ref):
        bm, am = b_ref[...], a_ref[...]
        g = sum(lax.dot_general(p, bm, NN, preferred_element_type=F32) for p in split3(d_ref[...]))
        o_ref[...] = sum(lax.dot_general(am, p, NN, preferred_element_type=F32) for p in split3(g))

    out = pl.pallas_call(
        body, grid=(H,),
        in_specs=[pl.BlockSpec((rows_per_head, GRID_W * GRID_W), lambda h: (h, 0)),
                  pl.BlockSpec((GRID_W * GRID_W, 128), lambda h: (0, 0)),
                  pl.BlockSpec((16, rows_per_head), lambda h: (0, 0))],
        out_specs=pl.BlockSpec((None, 16, 128), lambda h: (h, 0, 0)),
        out_shape=jax.ShapeDtypeStruct((H, 16, 128), F32),
        compiler_params=_cparams(("parallel",)), name=name)(d, jnp.asarray(Bp, BF16), jnp.asarray(Ap, BF16))
    return out[:, :na, :nb]


def _vmem_call(name, fn, out_shape, *arrays):
    def body(*refs):
        n = len(arrays)
        res = fn(*[r[...] for r in refs[:n]])
        if not isinstance(res, (tuple, list)):
            res = (res,)
        for o, v in zip(refs[n:], res):
            o[...] = v
    return pl.pallas_call(body, out_shape=out_shape, name=name,
                          compiler_params=pltpu.CompilerParams(vmem_limit_bytes=VMEM_LIMIT))(*arrays)


def _silu(v):
    return v / (1.0 + jnp.exp(-v))


def _adamw_math(w, g, m, v):
    m2 = ADAM_B1 * m + (1.0 - ADAM_B1) * g
    v2 = ADAM_B2 * v + (1.0 - ADAM_B2) * (g * g)
    m_hat = m2 / (1.0 - ADAM_B1 ** ADAM_STEP)
    v_hat = v2 / (1.0 - ADAM_B2 ** ADAM_STEP)
    delta = -ADAM_LR * (m_hat / (jnp.sqrt(v_hat) + ADAM_EPS) + ADAM_WD * w)
    return delta, m2, v2


def _ew_tile(R, C):
    return _tile(R, max(64, (262144 // C) // 64 * 64))


def adamw_rows(name, w, g, m, v, extra_g=None):
    R, C = w.shape
    tr = _ew_tile(R, C)
    extra_g = list(extra_g or [])
    ne = len(extra_g)

    def body(*refs):
        w_ref, g_ref, m_ref, v_ref = refs[:4]
        gs = g_ref[...]
        for e in refs[4:4 + ne]:
            gs = gs + e[...].astype(F32)
        go, do, mo, vo = refs[4 + ne:]
        d, m2, v2 = _adamw_math(w_ref[...], gs, m_ref[...], v_ref[...])
        go[...] = gs
        do[...] = d
        mo[...] = m2
        vo[...] = v2

    spec = pl.BlockSpec((tr, C), lambda i: (i, 0))
    return pl.pallas_call(
        body, grid=(R // tr,), in_specs=[spec] * (4 + ne), out_specs=[spec] * 4,
        out_shape=[jax.ShapeDtypeStruct((R, C), F32)] * 4,
        compiler_params=_cparams(("parallel",)), name=name)(w, g, m, v, *extra_g)


def add_cast_rows(name, a, b):
    R, C = a.shape
    tr = _ew_tile(R, C)

    def body(a_ref, b_ref, o_ref):
        o_ref[...] = (a_ref[...] + b_ref[...]).astype(BF16)

    spec = pl.BlockSpec((tr, C), lambda i: (i, 0))
    return pl.pallas_call(body, grid=(R // tr,), in_specs=[spec, spec], out_specs=spec,
                          out_shape=jax.ShapeDtypeStruct((R, C), BF16),
                          compiler_params=_cparams(("parallel",)), name=name)(a, b)


def _me():
    x, y, c = lax.axis_index("x"), lax.axis_index("y"), lax.axis_index("c")
    return x, y, c


def _flip(v, bit):
    return 1 - v if bit else v


def ag_small(name, x, with_sum=False):
    R, C = x.shape

    def body(x_ref, out_ref, *rest):
        if with_sum:
            sum_ref, send_sems, recv_sems, lsem = rest
        else:
            send_sems, recv_sems, lsem = rest
        mx, my, mc = _me()
        me = 4 * mx + 2 * my + mc
        local = pltpu.make_async_copy(x_ref, out_ref.at[me], lsem)
        local.start()
        sends = []
        for k in range(1, NDEV):
            peer = (_flip(mx, k & 4), _flip(my, k & 2), _flip(mc, k & 1))
            cp = pltpu.make_async_remote_copy(src_ref=x_ref, dst_ref=out_ref.at[me], send_sem=send_sems.at[k - 1],
                                              recv_sem=recv_sems.at[k - 1], device_id=peer, device_id_type=MESH)
            cp.start()
            sends.append(cp)
        for k in range(1, NDEV):
            px, py, pc = _flip(mx, k & 4), _flip(my, k & 2), _flip(mc, k & 1)
            pltpu.make_async_remote_copy(src_ref=x_ref, dst_ref=out_ref.at[4 * px + 2 * py + pc],
                                         send_sem=send_sems.at[k - 1], recv_sem=recv_sems.at[k - 1],
                                         device_id=(px, py, pc), device_id_type=MESH).wait_recv()
        for cp in sends:
            cp.wait_send()
        local.wait()
        if with_sum:
            acc = out_ref[0]
            for d in range(1, NDEV):
                acc = acc + out_ref[d]
            sum_ref[...] = acc

    out_shape = [jax.ShapeDtypeStruct((NDEV, R, C), F32)]
    if with_sum:
        out_shape.append(jax.ShapeDtypeStruct((R, C), F32))
    vm = pl.BlockSpec(memory_space=pltpu.VMEM)
    res = pl.pallas_call(
        body, out_shape=out_shape, in_specs=[vm], out_specs=[vm] * len(out_shape),
        scratch_shapes=[pltpu.SemaphoreType.DMA((NDEV - 1,)), pltpu.SemaphoreType.DMA((NDEV - 1,)),
                        pltpu.SemaphoreType.DMA],
        compiler_params=pltpu.CompilerParams(vmem_limit_bytes=VMEM_LIMIT), name=name)(x)
    return res if with_sum else res[0]


def ag_big(name, shards):
    n = len(shards)

    def body(*refs):
        ins, outs = refs[:n], refs[n:2 * n]
        send_sems, recv_sems, lsems = refs[2 * n:]
        mx, my, mc = _me()
        me = (mx, my, mc)
        sibling = (mx, my, 1 - mc)
        chips = [(1 - mx, my), (mx, 1 - my), (1 - mx, 1 - my)]

        def idx(p):
            return 4 * p[0] + 2 * p[1] + p[2]

        def copy(t, k, block, to, src=None):
            dst = outs[t].at[idx(block)]
            return pltpu.make_async_remote_copy(
                src_ref=dst if src is None else src, dst_ref=dst, send_sem=send_sems.at[7 * t + k],
                recv_sem=recv_sems.at[7 * t + k], device_id=to, device_id_type=MESH)

        started = []
        locals_ = []
        for t in range(n):
            mine = pltpu.make_async_copy(ins[t], outs[t].at[idx(me)], lsems.at[t])
            mine.start()
            locals_.append(mine)
            first = [copy(t, 0, me, sibling, src=ins[t])]
            first += [copy(t, 1 + j, me, (*chip, mc), src=ins[t]) for j, chip in enumerate(chips)]
            for cp in first:
                cp.start()
            started += first
        for t in range(n):
            for j, chip in enumerate(chips):
                copy(t, 1 + j, (*chip, mc), me).wait_recv()
                fwd = copy(t, 4 + j, (*chip, mc), sibling)
                fwd.start()
                started.append(fwd)
        for t in range(n):
            copy(t, 0, sibling, me).wait_recv()
            for j, chip in enumerate(chips):
                copy(t, 4 + j, (*chip, 1 - mc), me).wait_recv()
        for cp in started:
            cp.wait_send()
        for mine in locals_:
            mine.wait()

    anyspec = pl.BlockSpec(memory_space=pl.ANY)
    return pl.pallas_call(
        body, out_shape=[jax.ShapeDtypeStruct((NDEV,) + s.shape, s.dtype) for s in shards],
        in_specs=[anyspec] * n, out_specs=[anyspec] * n,
        scratch_shapes=[pltpu.SemaphoreType.DMA((7 * n,)), pltpu.SemaphoreType.DMA((7 * n,)),
                        pltpu.SemaphoreType.DMA((n,))],
        name=name)(*shards)


def rs_sibling(name, grads):
    n = len(grads)

    def body(*refs):
        ins, outs = refs[:n], refs[n:2 * n]
        send_sems, recv_sems = refs[2 * n:]
        mx, my, mc = _me()
        sibling = (mx, my, 1 - mc)
        cps = []
        for t in range(n):
            cp = pltpu.make_async_remote_copy(
                src_ref=ins[t].at[pl.ds((1 - mc) * 4, 4)], dst_ref=outs[t], send_sem=send_sems.at[t],
                recv_sem=recv_sems.at[t], device_id=sibling, device_id_type=MESH)
            cp.start()
            cps.append(cp)
        for cp in cps:
            cp.wait()

    anyspec = pl.BlockSpec(memory_space=pl.ANY)
    return pl.pallas_call(
        body, out_shape=[jax.ShapeDtypeStruct((4,) + g.shape[1:], F32) for g in grads],
        in_specs=[anyspec] * n, out_specs=[anyspec] * n,
        scratch_shapes=[pltpu.SemaphoreType.DMA((n,)), pltpu.SemaphoreType.DMA((n,))],
        name=name)(*grads)


def rs_chips(name, parts):
    n = len(parts)

    def body(*refs):
        ins, outs = refs[:n], refs[n:2 * n]
        send_sems, recv_sems = refs[2 * n:]
        mx, my, mc = _me()
        cps = []
        for t in range(n):
            for k in range(1, 4):
                px, py = _flip(mx, k & 2), _flip(my, k & 1)
                cp = pltpu.make_async_remote_copy(
                    src_ref=ins[t].at[2 * px + py], dst_ref=outs[t].at[k - 1], send_sem=send_sems.at[3 * t + k - 1],
                    recv_sem=recv_sems.at[3 * t + k - 1], device_id=(px, py, mc), device_id_type=MESH)
                cp.start()
                cps.append(cp)
        for cp in cps:
            cp.wait()

    anyspec = pl.BlockSpec(memory_space=pl.ANY)
    return pl.pallas_call(
        body, out_shape=[jax.ShapeDtypeStruct((3,) + p.shape[1:], p.dtype) for p in parts],
        in_specs=[anyspec] * n, out_specs=[anyspec] * n,
        scratch_shapes=[pltpu.SemaphoreType.DMA((3 * n,)), pltpu.SemaphoreType.DMA((3 * n,))],
        name=name)(*parts)


def _rope_tables(S, nctx):
    t = jnp.arange(S)
    row = (t // GRID_W).astype(F32)
    col = (t % GRID_W).astype(F32)
    pairs = HEAD // 4
    inv = ROPE_THETA ** (-jnp.arange(pairs, dtype=F32) / pairs)
    ang_r = row[:, None] * inv
    ang_c = col[:, None] * inv
    ang = jnp.concatenate([ang_r, ang_r, ang_c, ang_c], axis=-1)
    cos = jnp.concatenate([jnp.ones((nctx, HEAD), F32), jnp.cos(ang)], axis=0)
    sin = jnp.concatenate([jnp.zeros((nctx, HEAD), F32), jnp.sin(ang)], axis=0)
    lane = jnp.arange(HEAD)[None, :]
    first = (lane & 32) == 0
    return cos, jnp.where(first, -sin, 0.0), jnp.where(first, 0.0, sin)


def _pad_rows(v, rows):
    v = v.reshape(-1).astype(F32)
    return jnp.pad(v, (0, rows * 128 - v.shape[0])).reshape(rows, 128)


def _rows8(n):
    return -(-n // 1024) * 8


def kernel(x, c, ctx, c_ctx, ada_w, ada_b, norm_w, mlp_w1, mlp_w2, ev_w_in, ev_w_out, ev_q_norm, ev_k_norm, ev_sink, od_w_in, od_w_out, od_rpb, final_norm_w, loss_target, m_c_ctx, m_ada_w, m_ada_b, m_norm_w, m_mlp_w1, m_mlp_w2, m_ev_w_in, m_ev_w_out, m_ev_q_norm, m_ev_k_norm, m_ev_sink, m_od_w_in, m_od_w_out, m_od_rpb, m_final_norm_w, v_c_ctx, v_ada_w, v_ada_b, v_norm_w, v_mlp_w1, v_mlp_w2, v_ev_w_in, v_ev_w_out, v_ev_q_norm, v_ev_k_norm, v_ev_sink, v_od_w_in, v_od_w_out, v_od_rpb, v_final_norm_w):
    S, D = x.shape[1], x.shape[2]
    NC = ctx.shape[1]
    T = NC + S
    assert NC == ROW_TILE and S % GRID_W == 0
    ada_cols = ada_w.shape[2]
    nw_cols = norm_w.shape[2]
    me = 4 * lax.axis_index("x") + 2 * lax.axis_index("y") + lax.axis_index("c")

    pack1 = jnp.concatenate([_pad_rows(c, _rows8(D)), _pad_rows(norm_w, _rows8(4 * nw_cols))], axis=0)
    g1 = ag_small("ag_c_normw", pack1)
    c_all = g1[:, :D // 128].reshape(NDEV, D)
    nw_rows = _rows8(D)
    nw = g1[:, nw_rows:nw_rows + 4 * nw_cols // 128].reshape(NDEV, 2, 2, nw_cols)
    nw = nw.transpose(1, 2, 0, 3).reshape(2, 2, D)
    cin = jnp.concatenate([c_all, jnp.broadcast_to(c_ctx[None], (NDEV, D))], axis=0)
    act = _vmem_call("silu_c", lambda v: _silu(v).astype(BF16), jax.ShapeDtypeStruct((2 * NDEV, D), BF16), cin)
    ada_b_loc = lax.dynamic_slice_in_dim(ada_b, me * ada_cols, ada_cols, axis=1)
    mods = [mm_nn(f"mod{i}", act, ada_w[i], _epi_bias, [F32], extras=(ada_b_loc[i:i + 1],), extra_kinds=('n',))[0]
            for i in range(2)]
    gm = ag_small("ag_mod", jnp.concatenate(mods, axis=1))
    gm = gm.reshape(NDEV, 2 * NDEV, 2, ada_cols).transpose(2, 1, 0, 3).reshape(2, 2 * NDEV, NDEV * ada_cols)
    mod_lat = lax.dynamic_index_in_dim(gm, me, axis=1, keepdims=False)
    mod_ctx = gm[:, NDEV]
    mod2 = jnp.stack([mod_ctx, mod_lat], axis=1).reshape(2, 2, 6, D)

    def chunk(i, j):
        return mod2[i, :, j, :]

    shards = [ev_w_in[0], ev_w_out[0], mlp_w1[0], mlp_w2[0], od_w_in[0], od_w_out[0], mlp_w1[1], mlp_w2[1]]
    gw = ag_big("ag_weights", [s.astype(BF16) for s in shards])
    w_in = [gw[0], gw[4]]
    w_out = [gw[1].reshape(-1, D), gw[5].reshape(-1, D)]
    w1 = [gw[2], gw[6]]
    w2 = [gw[3].reshape(-1, D), gw[7].reshape(-1, D)]

    cos, sa, sb = _rope_tables(S, NC)
    bias8 = na_bias_table(od_rpb[0])
    sink = ev_sink[0]
    TQ_F, TQ_B = 256, 128

    X0 = jnp.concatenate([ctx[0], x[0]], axis=0)
    h_a = norm_mod("l0_norm1", X0, nw[0, 0][None], chunk(0, 0), chunk(0, 1), NC)
    qkv0 = mm_nn("l0_qkv", h_a, w_in[0], _epi_store(F32), [F32])[0]
    qkvh0 = prep_even("l0_prep", qkv0, ev_q_norm, ev_k_norm, cos, sa, sb)
    o0 = attn_even_fwd("l0_attn", qkvh0, sink, NC, TQ_F)
    tm0 = _tile(T, 1100)
    X1, y0 = mm_nn("l0_out", o0, w_out[0], _epi_resid_gate(NC, tm0), [F32, F32], extras=(X0, chunk(0, 2)),
                   extra_kinds=('mn', 'n'))
    h_b = norm_mod("l0_norm2", X1, nw[0, 1][None], chunk(0, 3), chunk(0, 4), NC)
    a0, r0 = mm_nn("l0_up", h_b, w1[0], _epi_relu2, [BF16, BF16])
    X2, z0 = mm_nn("l0_down", a0, w2[0], _epi_resid_gate(NC, tm0), [F32, F32], extras=(X1, chunk(0, 5)),
                   extra_kinds=('mn', 'n'))

    h_c = norm_mod("l1_norm1", X2, nw[1, 0][None], chunk(1, 0), chunk(1, 1), NC)
    qkv1 = mm_nn("l1_qkv", h_c, w_in[1], _epi_store(BF16), [BF16])[0]
    o1 = attn_odd_fwd("l1_attn", qkv1, bias8, NC)
    X2l = X2[NC:]
    tm1 = _tile(S, 1100)
    X3, y1 = mm_nn("l1_out", o1, w_out[1], _epi_resid_gate(0, tm1), [F32, F32], extras=(X2l, chunk(1, 2)),
                   extra_kinds=('mn', 'n'))
    h_d = norm_mod("l1_norm2", X3, nw[1, 1][None], chunk(1, 3), chunk(1, 4), 0)
    a1, r1 = mm_nn("l1_up", h_d, w1[1], _epi_relu2, [BF16, BF16])
    X4, z1 = mm_nn("l1_down", a1, w2[1], _epi_resid_gate(0, tm1), [F32, F32], extras=(X3, chunk(1, 5)),
                   extra_kinds=('mn', 'n'))
    dX4, loss_p, dfw_p = final_loss("final_loss", X4, final_norm_w[None], loss_target[0])

    dz1, pg2_1 = gate_bwd("l1_gate2_bwd", dX4, z1, chunk(1, 5), 0)
    du1 = mm_nt("l1_down_dx", dz1, w2[1], _epi_mul2r, BF16, extras=(r1,))
    g_w2_1 = mm_tn("l1_down_dw", a1, dz1, 0)
    dh_d = mm_nt("l1_up_dx", du1, w1[1], _epi_store(F32), F32)
    g_w1_1 = mm_tn("l1_up_dw", h_d, du1, 1)
    dX3, pn2_1 = norm_bwd("l1_norm2_bwd", X3, dh_d, dX4, nw[1, 1][None], chunk(1, 4), 0)
    dy1, pg1_1 = gate_bwd("l1_gate1_bwd", dX3, y1, chunk(1, 2), 0)
    do1 = mm_nt("l1_out_dx", dy1, w_out[1], _epi_store(BF16), BF16)
    g_wout_1 = mm_tn("l1_out_dw", o1, dy1, 0)
    dq1, dk1, dv1, dbias8 = attn_odd_bwd("l1_attn_bwd", qkv1, bias8, do1, NC)
    dqkv1 = jnp.concatenate([jnp.pad(dq1, ((NC, 0), (0, 0))), dk1, dv1], axis=1).astype(BF16)
    dh_c = mm_nt("l1_qkv_dx", dqkv1, w_in[1], _epi_store(F32), F32)
    g_win_1 = mm_tn("l1_qkv_dw", h_c, dqkv1, 1)
    dX2, pn1_1 = norm_bwd("l1_norm1_bwd", X2, dh_c, jnp.pad(dX3, ((NC, 0), (0, 0))), nw[1, 0][None], chunk(1, 1), NC)
    d_rpb = na_bias_grad("rpb_grad", dbias8)

    dz0, pg2_0 = gate_bwd("l0_gate2_bwd", dX2, z0, chunk(0, 5), NC)
    du0 = mm_nt("l0_down_dx", dz0, w2[0], _epi_mul2r, BF16, extras=(r0,))
    g_w2_0 = mm_tn("l0_down_dw", a0, dz0, 0)
    dh_b = mm_nt("l0_up_dx", du0, w1[0], _epi_store(F32), F32)
    g_w1_0 = mm_tn("l0_up_dw", h_b, du0, 1)
    dX1, pn2_0 = norm_bwd("l0_norm2_bwd", X1, dh_b, dX2, nw[0, 1][None], chunk(0, 4), NC)
    dy0, pg1_0 = gate_bwd("l0_gate1_bwd", dX1, y0, chunk(0, 2), NC)
    do0 = mm_nt("l0_out_dx", dy0, w_out[0], _epi_store(BF16), BF16)
    g_wout_0 = mm_tn("l0_out_dw", o0, dy0, 0)
    dq0, dk0, dv0, dsink_p = attn_even_bwd("l0_attn_bwd", qkvh0, sink, do0, NC, TQ_B)
    dqkv0, pqk = prep_even_bwd("l0_prep_bwd", qkv0, dq0, dk0, dv0, ev_q_norm, ev_k_norm, cos, sa, sb)
    dh_a = mm_nt("l0_qkv_dx", dqkv0, w_in[0], _epi_store(F32), F32)
    g_win_0 = mm_tn("l0_qkv_dw", h_a, dqkv0, 1)
    dX0, pn1_0 = norm_bwd("l0_norm1_bwd", X0, dh_a, dX1, nw[0, 0][None], chunk(0, 1), NC)
    grad_x = dX0[NC:][None]

    def dmod(grp, pn1, pg1, pn2, pg2):
        return jnp.concatenate([pn1[grp], pn1[2 + grp], pg1[grp], pn2[grp], pn2[2 + grp], pg2[grp]])

    dmod_lat = jnp.stack([dmod(1, pn1_0, pg1_0, pn2_0, pg2_0), dmod(1, pn1_1, pg1_1, pn2_1, pg2_1)])
    dmod_ctx = jnp.stack([dmod(0, pn1_0, pg1_0, pn2_0, pg2_0), dmod(0, pn1_1, pg1_1, pn2_1, pg2_1)])
    dnw_p = jnp.stack([pn1_0[4], pn2_0[4], pn1_1[4], pn2_1[4]])
    pieces = [dmod_lat, dmod_ctx, dnw_p, pqk[0], pqk[1], dsink_p[8:, 0, 0], d_rpb, dfw_p[0], loss_p[0, 0]]
    sizes = [int(np.prod(p.shape)) for p in pieces]
    rows = [_rows8(s) for s in sizes]
    pack2 = jnp.concatenate([_pad_rows(p, r) for p, r in zip(pieces, rows)], axis=0)
    g2, s2 = ag_small("ag_small_grads", pack2, with_sum=True)
    offs = np.concatenate([[0], np.cumsum(rows)])

    def piece(arr, i, shape):
        return arr[..., offs[i]:offs[i + 1], :].reshape(arr.shape[:-2] + (-1,))[..., :sizes[i]].reshape(
            arr.shape[:-2] + shape)

    dmod_all = piece(g2, 0, (2, 6 * D))
    dmodc_sum = piece(s2, 1, (2, 6 * D))
    dnw_sum = piece(s2, 2, (2, 2, D))
    g_qn = piece(s2, 3, ev_q_norm.shape)
    g_kn = piece(s2, 4, ev_k_norm.shape)
    g_sink = piece(s2, 5, ev_sink.shape)
    g_rpb = piece(s2, 6, od_rpb.shape)
    g_fw = piece(s2, 7, final_norm_w.shape)
    loss = piece(s2, 8, ())

    dm16 = jnp.concatenate([dmod_all.transpose(1, 0, 2), dmodc_sum[:, None, :],
                            jnp.zeros((2, NDEV - 1, 6 * D), F32)], axis=1)
    dm16_loc = lax.dynamic_slice_in_dim(dm16.reshape(2, 2 * NDEV, NDEV, ada_cols), me, 1, axis=2)[:, :, 0, :]
    g_ada_b = _vmem_call("ada_b_grad", lambda v: jnp.sum(v, axis=1),
                         jax.ShapeDtypeStruct((2, 6 * D), F32), dm16)
    g_ada_w = []
    dact_p = None
    for i in range(2):
        dmb = dm16_loc[i].astype(BF16)
        g_ada_w.append(mm_tn(f"ada_w_grad{i}", act, dmb, None))
        part = mm_nt(f"ada_dact{i}", dmb, ada_w[i], _epi_store(F32), F32)
        dact_p = part if dact_p is None else dact_p + part
    _, dact = ag_small("ag_cctx", dact_p, with_sum=True)

    def cctx_grad(da, cc):
        sg = 1.0 / (1.0 + jnp.exp(-cc))
        return da[NDEV:NDEV + 1] * (sg * (1.0 + cc * (1.0 - sg)))

    g_cctx = _vmem_call("cctx_grad", cctx_grad, jax.ShapeDtypeStruct((1, D), F32), dact, c_ctx[None])[0]

    grads = [g_win_0, g_wout_0, g_w1_0, g_w2_0, g_win_1, g_wout_1, g_w1_1, g_w2_1]
    sib = rs_sibling("rs_sibling", grads)
    mc4 = lax.axis_index("c") * 4
    parts = []
    for t, (g, s_) in enumerate(zip(grads, sib)):
        r_, c_ = g.shape[1], g.shape[2]
        own_half = lax.dynamic_slice_in_dim(g, mc4, 4, axis=0)
        parts.append(add_cast_rows(f"rs_chip_sum{t}", own_half.reshape(4 * r_, c_), s_.reshape(4 * r_, c_))
                     .reshape(4, r_, c_))
    rem = rs_chips("rs_chips", parts)
    my_chip = 2 * lax.axis_index("x") + lax.axis_index("y")
    big = {}
    names = ['ev_w_in', 'ev_w_out', 'mlp_w1_0', 'mlp_w2_0', 'od_w_in', 'od_w_out', 'mlp_w1_1', 'mlp_w2_1']
    wts = [ev_w_in[0], ev_w_out[0], mlp_w1[0], mlp_w2[0], od_w_in[0], od_w_out[0], mlp_w1[1], mlp_w2[1]]
    ms = [m_ev_w_in[0], m_ev_w_out[0], m_mlp_w1[0], m_mlp_w2[0], m_od_w_in[0], m_od_w_out[0], m_mlp_w1[1], m_mlp_w2[1]]
    vs = [v_ev_w_in[0], v_ev_w_out[0], v_mlp_w1[0], v_mlp_w2[0], v_od_w_in[0], v_od_w_out[0], v_mlp_w1[1], v_mlp_w2[1]]
    for t in range(8):
        own_f32 = lax.dynamic_index_in_dim(grads[t], mc4 + my_chip, axis=0, keepdims=False)
        own_sib = lax.dynamic_index_in_dim(sib[t], my_chip, axis=0, keepdims=False)
        big[names[t]] = adamw_rows(f"adamw_{names[t]}", wts[t], own_f32, ms[t], vs[t],
                                   extra_g=[own_sib, rem[t][0], rem[t][1], rem[t][2]])

    def stack2(a, b):
        return tuple(jnp.stack([u, v_]) for u, v_ in zip(big[a], big[b]))

    def one(a):
        return tuple(u[None] for u in big[a])

    r_mlp_w1, r_mlp_w2 = stack2('mlp_w1_0', 'mlp_w1_1'), stack2('mlp_w2_0', 'mlp_w2_1')
    r_ev_w_in, r_ev_w_out, r_od_w_in, r_od_w_out = one('ev_w_in'), one('ev_w_out'), one('od_w_in'), one('od_w_out')

    g_ada = jnp.stack(g_ada_w)
    r_ada_w = adamw_rows("adamw_ada_w", ada_w.reshape(2 * D, ada_cols), g_ada.reshape(2 * D, ada_cols),
                         m_ada_w.reshape(2 * D, ada_cols), v_ada_w.reshape(2 * D, ada_cols))
    r_ada_w = tuple(u.reshape(2, D, ada_cols) for u in r_ada_w)

    g_nw_loc = lax.dynamic_slice_in_dim(dnw_sum, me * nw_cols, nw_cols, axis=2)
    small = [(c_ctx, g_cctx, m_c_ctx, v_c_ctx), (ada_b, g_ada_b, m_ada_b, v_ada_b),
             (norm_w, g_nw_loc, m_norm_w, v_norm_w), (ev_q_norm, g_qn, m_ev_q_norm, v_ev_q_norm),
             (ev_k_norm, g_kn, m_ev_k_norm, v_ev_k_norm), (ev_sink, g_sink, m_ev_sink, v_ev_sink),
             (od_rpb, g_rpb, m_od_rpb, v_od_rpb), (final_norm_w, g_fw, m_final_norm_w, v_final_norm_w)]
    srows = [_rows8(int(np.prod(w.shape))) for w, _, _, _ in small]
    packs = [jnp.concatenate([_pad_rows(tup[k], r) for tup, r in zip(small, srows)], axis=0) for k in range(4)]
    sres = adamw_rows("adamw_small", *packs)
    soffs = np.concatenate([[0], np.cumsum(srows)])

    def unpack(arr, i):
        w = small[i][0]
        return arr[soffs[i]:soffs[i + 1]].reshape(-1)[:int(np.prod(w.shape))].reshape(w.shape)

    sm = [[unpack(sres[k], i) for i in range(len(small))] for k in range(4)]

    def outs(k):
        big_k = {'ada_w': r_ada_w[k], 'mlp_w1': r_mlp_w1[k], 'mlp_w2': r_mlp_w2[k], 'ev_w_in': r_ev_w_in[k],
                 'ev_w_out': r_ev_w_out[k], 'od_w_in': r_od_w_in[k], 'od_w_out': r_od_w_out[k]}
        return (sm[k][0], big_k['ada_w'], sm[k][1], sm[k][2], big_k['mlp_w1'], big_k['mlp_w2'], big_k['ev_w_in'],
                big_k['ev_w_out'], sm[k][3], sm[k][4], sm[k][5], big_k['od_w_in'], big_k['od_w_out'], sm[k][6],
                sm[k][7])

    return (loss, grad_x, *outs(0), *outs(1), *outs(2), *outs(3))
```

```python
import numpy as np
import jax
import jax.numpy as jnp
from jax import lax
from jax.experimental import pallas as pl
from jax.experimental.pallas import tpu as pltpu

F32 = jnp.float32
BF16 = jnp.bfloat16
MESH = pl.DeviceIdType.MESH

NDEV = 8
HEAD = 128
GRID_W = 64
NA_KH, NA_KW = 8, 16
WINDOW = 128
ROPE_THETA = 10000.0
EPS = 1e-6
NEG = -1e30
SCALE = HEAD ** -0.5
ROW_TILE = 256
VMEM_LIMIT = 56 * 1024 * 1024

ADAM_LR, ADAM_B1, ADAM_B2, ADAM_EPS, ADAM_WD, ADAM_STEP = 0.001, 0.9, 0.999, 1e-08, 0.01, 10

NT = (((1,), (1,)), ((), ()))
NN = (((1,), (0,)), ((), ()))
TN = (((0,), (0,)), ((), ()))


def _cparams(sem):
    return pltpu.CompilerParams(dimension_semantics=sem, vmem_limit_bytes=VMEM_LIMIT)


def _tile(n, cap):
    if n <= cap:
        return n
    t = cap - cap % 64
    while t >= 64:
        if n % t == 0:
            return t
        t -= 64
    raise ValueError((n, cap))


def _dot(a, b, dims):
    return lax.dot_general(a.astype(BF16), b.astype(BF16), dims, preferred_element_type=F32)


def _slot(d):
    return (d % 2) * 4 + d // 2


def _mm_core(name, grid, ins, in_specs, out_shape, out_specs, dims, acc_shape, epi):
    nk = grid[2]
    n_extra = len(ins) - 2

    def body(*refs):
        a_ref, b_ref = refs[0], refs[1]
        ex = refs[2:2 + n_extra]
        outs = refs[2 + n_extra:-1]
        acc = refs[-1]
        k = pl.program_id(2)

        @pl.when(k == 0)
        def _():
            acc[...] = jnp.zeros_like(acc)

        acc[...] += _dot(a_ref[...], b_ref[...], dims)

        @pl.when(k == nk - 1)
        def _():
            epi(acc[...], ex, outs)

    return pl.pallas_call(
        body, grid=grid, in_specs=in_specs, out_specs=out_specs, out_shape=out_shape,
        scratch_shapes=[pltpu.VMEM(acc_shape, F32)],
        compiler_params=_cparams(("parallel", "parallel", "arbitrary")), name=name)(*ins)


def _epi_store(dtype):
    def epi(acc, ex, outs):
        outs[0][...] = acc.astype(dtype)
    return epi


def _epi_bias(acc, ex, outs):
    outs[0][...] = acc + ex[0][...]


def _epi_relu2(acc, ex, outs):
    r = jnp.maximum(acc, 0.0)
    outs[0][...] = (r * r).astype(BF16)
    outs[1][...] = r.astype(BF16)


def _epi_mul2r(acc, ex, outs):
    outs[0][...] = (acc * (2.0 * ex[0][...].astype(F32))).astype(BF16)


def _epi_resid_gate(nctx, tm):
    def epi(acc, ex, outs):
        rows = pl.program_id(0) * tm + lax.broadcasted_iota(jnp.int32, (tm, 1), 0)
        g = jnp.where(rows < nctx, ex[1][0:1, :], ex[1][1:2, :])
        outs[0][...] = ex[0][...] + g * acc
        outs[1][...] = acc
    return epi


def mm_nn(name, a, w, epi, outs, extras=(), extra_kinds=(), tm_cap=1100, tn_cap=512, tk_cap=2048):
    M, K = a.shape
    if w.ndim == 3:
        ns = w.shape[2]
        N = NDEV * ns
        tn = _tile(ns, tn_cap)
        nper = ns // tn
    else:
        N = w.shape[1]
        tn = _tile(N, tn_cap)
    tm = _tile(M, tm_cap)
    tk = _tile(K, tk_cap)
    grid = (M // tm, N // tn, K // tk)
    a_spec = pl.BlockSpec((tm, tk), lambda i, j, k: (i, k))
    if w.ndim == 3:
        b_spec = pl.BlockSpec((None, tk, tn), lambda i, j, k: (j // nper, k, j % nper))
    else:
        b_spec = pl.BlockSpec((tk, tn), lambda i, j, k: (k, j))
    ex_specs = []
    for e, kind in zip(extras, extra_kinds):
        if kind == 'mn':
            ex_specs.append(pl.BlockSpec((tm, tn), lambda i, j, k: (i, j)))
        else:
            ex_specs.append(pl.BlockSpec((e.shape[0], tn), lambda i, j, k: (0, j)))
    out_shape = [jax.ShapeDtypeStruct((M, N), dt) for dt in outs]
    out_specs = [pl.BlockSpec((tm, tn), lambda i, j, k: (i, j)) for _ in outs]
    return _mm_core(name, grid, (a, w, *extras), [a_spec, b_spec, *ex_specs], out_shape, out_specs, NN, (tm, tn), epi)


def mm_nt(name, a, w, epi, out_dtype, extras=(), tm_cap=1100, to_cap=512, tc_cap=2048):
    M, N = a.shape
    tm = _tile(M, tm_cap)
    if w.ndim == 3:
        Kw, ns = w.shape[1], w.shape[2]
        tc = _tile(ns, tc_cap)
        cper = ns // tc
    else:
        Kw = w.shape[0]
        tc = _tile(N, tc_cap)
    to = _tile(Kw, to_cap)
    grid = (M // tm, Kw // to, N // tc)
    a_spec = pl.BlockSpec((tm, tc), lambda i, j, k: (i, k))
    if w.ndim == 3:
        b_spec = pl.BlockSpec((None, to, tc), lambda i, j, k: (k // cper, j, k % cper))
    else:
        b_spec = pl.BlockSpec((to, tc), lambda i, j, k: (j, k))
    ex_specs = [pl.BlockSpec((tm, to), lambda i, j, k: (i, j)) for _ in extras]
    out_shape = [jax.ShapeDtypeStruct((M, Kw), out_dtype)]
    out_specs = [pl.BlockSpec((tm, to), lambda i, j, k: (i, j))]
    return _mm_core(name, grid, (a, w, *extras), [a_spec, b_spec, *ex_specs], out_shape, out_specs, NT, (tm, to), epi)[0]


def mm_tn(name, a, b, shard_axis, to_cap=1024, tn_cap=512, tc_cap=1100):
    M, Ka = a.shape
    N = b.shape[1]
    tc = _tile(M, tc_cap)
    if shard_axis is None:
        to, tn = _tile(Ka, to_cap), _tile(N, tn_cap)
        shape = (Ka, N)
        oblk = (to, tn)
        omap = lambda i, j, k: (i, j)
    elif shard_axis == 1:
        ns = N // NDEV
        to, tn = _tile(Ka, to_cap), _tile(ns, tn_cap)
        per = ns // tn
        shape = (NDEV, Ka, ns)
        oblk = (None, to, tn)
        omap = lambda i, j, k: (_slot(j // per), i, j % per)
    else:
        rs = Ka // NDEV
        to, tn = _tile(rs, to_cap), _tile(N, tn_cap)
        per = rs // to
        shape = (NDEV, rs, N)
        oblk = (None, to, tn)
        omap = lambda i, j, k: (_slot(i // per), i % per, j)
    grid = (Ka // to, N // tn, M // tc)
    a_spec = pl.BlockSpec((tc, to), lambda i, j, k: (k, i))
    b_spec = pl.BlockSpec((tc, tn), lambda i, j, k: (k, j))
    out_shape = [jax.ShapeDtypeStruct(shape, F32)]
    out_specs = [pl.BlockSpec(oblk, omap)]
    return _mm_core(name, grid, (a, b), [a_spec, b_spec], out_shape, out_specs, TN, (to, tn), _epi_store(F32))[0]


def _row_spec(D):
    return pl.BlockSpec((ROW_TILE, D), lambda i: (i, 0))


def _const_spec(r, D):
    return pl.BlockSpec((r, D), lambda i: (0, 0))


def _grp(ref, is_ctx):
    return jnp.where(is_ctx, ref[0:1, :], ref[1:2, :])


def norm_mod(name, x, nw, sh, sc, nctx):
    R, D = x.shape
    assert R % ROW_TILE == 0 and nctx % ROW_TILE == 0

    def body(x_ref, nw_ref, sh_ref, sc_ref, o_ref):
        is_ctx = pl.program_id(0) * ROW_TILE < nctx
        xv = x_ref[...]
        rstd = lax.rsqrt(jnp.mean(xv * xv, axis=-1, keepdims=True) + EPS)
        n = xv * rstd * nw_ref[...]
        o_ref[...] = (n * (1.0 + _grp(sc_ref, is_ctx)) + _grp(sh_ref, is_ctx)).astype(BF16)

    return pl.pallas_call(
        body, grid=(R // ROW_TILE,),
        in_specs=[_row_spec(D), _const_spec(1, D), _const_spec(2, D), _const_spec(2, D)],
        out_specs=_row_spec(D), out_shape=jax.ShapeDtypeStruct((R, D), BF16),
        compiler_params=_cparams(("parallel",)), name=name)(x, nw, sh, sc)


def norm_bwd(name, x, dh, dres, nw, sc, nctx):
    R, D = x.shape
    assert R % ROW_TILE == 0 and nctx % ROW_TILE == 0

    def body(x_ref, dh_ref, dres_ref, nw_ref, sc_ref, dx_ref, part_ref):
        i = pl.program_id(0)
        is_ctx = i * ROW_TILE < nctx

        @pl.when(i == 0)
        def _():
            part_ref[...] = jnp.zeros_like(part_ref)

        xv = x_ref[...]
        dhv = dh_ref[...]
        w = nw_ref[...]
        rstd = lax.rsqrt(jnp.mean(xv * xv, axis=-1, keepdims=True) + EPS)
        xhat = xv * rstd
        n = xhat * w
        dn = dhv * (1.0 + _grp(sc_ref, is_ctx))
        dxhat = dn * w
        dx_ref[...] = dres_ref[...] + rstd * (dxhat - xhat * jnp.mean(dxhat * xhat, axis=-1, keepdims=True))
        s_sh = jnp.sum(dhv, axis=0, keepdims=True)
        s_sc = jnp.sum(dhv * n, axis=0, keepdims=True)
        s_nw = jnp.sum(dn * xhat, axis=0, keepdims=True)
        zero = jnp.zeros_like(s_sh)
        part_ref[0:1, :] += jnp.where(is_ctx, s_sh, zero)
        part_ref[1:2, :] += jnp.where(is_ctx, zero, s_sh)
        part_ref[2:3, :] += jnp.where(is_ctx, s_sc, zero)
        part_ref[3:4, :] += jnp.where(is_ctx, zero, s_sc)
        part_ref[4:5, :] += s_nw

    return pl.pallas_call(
        body, grid=(R // ROW_TILE,),
        in_specs=[_row_spec(D), _row_spec(D), _row_spec(D), _const_spec(1, D), _const_spec(2, D)],
        out_specs=[_row_spec(D), _const_spec(8, D)],
        out_shape=[jax.ShapeDtypeStruct((R, D), F32), jax.ShapeDtypeStruct((8, D), F32)],
        compiler_params=_cparams(("arbitrary",)), name=name)(x, dh, dres, nw, sc)


def gate_bwd(name, dx, y, g, nctx):
    R, D = dx.shape
    assert R % ROW_TILE == 0 and nctx % ROW_TILE == 0

    def body(dx_ref, y_ref, g_ref, dy_ref, part_ref):
        i = pl.program_id(0)
        is_ctx = i * ROW_TILE < nctx

        @pl.when(i == 0)
        def _():
            part_ref[...] = jnp.zeros_like(part_ref)

        dxv = dx_ref[...]
        dy_ref[...] = (dxv * _grp(g_ref, is_ctx)).astype(BF16)
        s = jnp.sum(dxv * y_ref[...], axis=0, keepdims=True)
        zero = jnp.zeros_like(s)
        part_ref[0:1, :] += jnp.where(is_ctx, s, zero)
        part_ref[1:2, :] += jnp.where(is_ctx, zero, s)

    return pl.pallas_call(
        body, grid=(R // ROW_TILE,),
        in_specs=[_row_spec(D), _row_spec(D), _const_spec(2, D)],
        out_specs=[_row_spec(D), _const_spec(8, D)],
        out_shape=[jax.ShapeDtypeStruct((R, D), BF16), jax.ShapeDtypeStruct((8, D), F32)],
        compiler_params=_cparams(("arbitrary",)), name=name)(dx, y, g)


def final_loss(name, x, fw, tgt):
    S, D = x.shape

    def body(x_ref, fw_ref, t_ref, dx_ref, loss_ref, dfw_ref):
        i = pl.program_id(0)

        @pl.when(i == 0)
        def _():
            loss_ref[...] = jnp.zeros_like(loss_ref)
            dfw_ref[...] = jnp.zeros_like(dfw_ref)

        xv = x_ref[...]
        w = fw_ref[...]
        rstd = lax.rsqrt(jnp.mean(xv * xv, axis=-1, keepdims=True) + EPS)
        xhat = xv * rstd
        e = xhat * w - t_ref[...]
        loss_ref[...] += 0.5 * jnp.sum(jnp.mean(e * e, axis=-1, keepdims=True))
        dout = e * (1.0 / D)
        dfw_ref[0:1, :] += jnp.sum(dout * xhat, axis=0, keepdims=True)
        dxhat = dout * w
        dx_ref[...] = rstd * (dxhat - xhat * jnp.mean(dxhat * xhat, axis=-1, keepdims=True))

    return pl.pallas_call(
        body, grid=(S // ROW_TILE,),
        in_specs=[_row_spec(D), _const_spec(1, D), _row_spec(D)],
        out_specs=[_row_spec(D), pl.BlockSpec((8, 128), lambda i: (0, 0)), _const_spec(8, D)],
        out_shape=[jax.ShapeDtypeStruct((S, D), F32), jax.ShapeDtypeStruct((8, 128), F32),
                   jax.ShapeDtypeStruct((8, D), F32)],
        compiler_params=_cparams(("arbitrary",)), name=name)(x, fw, tgt)


def _rope(x, cos, sa, sb):
    return x * cos + pltpu.roll(x, 96, 1) * sa + pltpu.roll(x, 32, 1) * sb


def _rope_t(dy, cos, sa, sb):
    return dy * cos + pltpu.roll(dy * sa, 32, 1) + pltpu.roll(dy * sb, 96, 1)


_EVEN_KINDS = ['qa'] * 8 + ['ka'] * 2 + ['v'] * 2 + ['qb'] * 8 + ['kb'] * 2 + ['v'] * 2
_EVEN_DSRC = ([('q', j) for j in range(8)] + [('k', 0), ('k', 1), ('v', 0), ('v', 1)]
              + [('q', 8 + j) for j in range(8)] + [('k', 2), ('k', 3), ('v', 2), ('v', 3)])


def _cols(j):
    return slice(j * HEAD, (j + 1) * HEAD)


def prep_even(name, qkv, qn, kn, cos, sa, sb):
    T, W = qkv.shape

    def body(x_ref, qn_ref, kn_ref, cos_ref, sa_ref, sb_ref, o_ref):
        cos_, sa_, sb_ = cos_ref[...], sa_ref[...], sb_ref[...]
        for j, kind in enumerate(_EVEN_KINDS):
            x = x_ref[:, _cols(j)]
            if kind in ('qa', 'ka'):
                rstd = lax.rsqrt(jnp.mean(x * x, axis=-1, keepdims=True) + EPS)
                x = x * rstd * (qn_ref[...] if kind == 'qa' else kn_ref[...])
            if kind != 'v':
                x = _rope(x, cos_, sa_, sb_)
            o_ref[:, _cols(j)] = x.astype(BF16)

    blk = pl.BlockSpec((ROW_TILE, W), lambda i: (i, 0))
    tab = pl.BlockSpec((ROW_TILE, HEAD), lambda i: (i, 0))
    one = pl.BlockSpec((1, HEAD), lambda i: (0, 0))
    return pl.pallas_call(
        body, grid=(T // ROW_TILE,), in_specs=[blk, one, one, tab, tab, tab], out_specs=blk,
        out_shape=jax.ShapeDtypeStruct(qkv.shape, BF16),
        compiler_params=_cparams(("parallel",)), name=name)(qkv, qn, kn, cos, sa, sb)


def prep_even_bwd(name, qkv, dq, dk, dv, qn, kn, cos, sa, sb):
    T, W = qkv.shape

    def body(x_ref, dq_ref, dk_ref, dv_ref, qn_ref, kn_ref, cos_ref, sa_ref, sb_ref, o_ref, part_ref):
        @pl.when(pl.program_id(0) == 0)
        def _():
            part_ref[...] = jnp.zeros_like(part_ref)

        cos_, sa_, sb_ = cos_ref[...], sa_ref[...], sb_ref[...]
        src = {'q': dq_ref, 'k': dk_ref, 'v': dv_ref}
        sums = {'qa': None, 'ka': None}
        for j, kind in enumerate(_EVEN_KINDS):
            which, blk_j = _EVEN_DSRC[j]
            d = src[which][:, _cols(blk_j)]
            if kind != 'v':
                d = _rope_t(d, cos_, sa_, sb_)
            if kind in ('qa', 'ka'):
                x = x_ref[:, _cols(j)]
                rstd = lax.rsqrt(jnp.mean(x * x, axis=-1, keepdims=True) + EPS)
                xhat = x * rstd
                s = jnp.sum(d * xhat, axis=0, keepdims=True)
                sums[kind] = s if sums[kind] is None else sums[kind] + s
                dxhat = d * (qn_ref[...] if kind == 'qa' else kn_ref[...])
                d = rstd * (dxhat - xhat * jnp.mean(dxhat * xhat, axis=-1, keepdims=True))
            o_ref[:, _cols(j)] = d.astype(BF16)
        part_ref[0:1, :] += sums['qa']
        part_ref[1:2, :] += sums['ka']

    def rows(w):
        return pl.BlockSpec((ROW_TILE, w), lambda i: (i, 0))

    one = pl.BlockSpec((1, HEAD), lambda i: (0, 0))
    return pl.pallas_call(
        body, grid=(T // ROW_TILE,),
        in_specs=[rows(W), rows(dq.shape[1]), rows(dk.shape[1]), rows(dv.shape[1]), one, one,
                  rows(HEAD), rows(HEAD), rows(HEAD)],
        out_specs=[rows(W), pl.BlockSpec((8, HEAD), lambda i: (0, 0))],
        out_shape=[jax.ShapeDtypeStruct(qkv.shape, BF16), jax.ShapeDtypeStruct((8, HEAD), F32)],
        compiler_params=_cparams(("arbitrary",)), name=name)(qkv, dq, dk, dv, qn, kn, cos, sa, sb)


def _even_maps():
    qmap = lambda h, qb: (qb, jnp.where(h < 8, h, h + 4))
    kmap = lambda h, qb: (0, jnp.where(h < 8, 8 + h // 4, 18 + h // 4))
    vmap = lambda h, qb: (0, jnp.where(h < 8, 10 + h // 4, 20 + h // 4))
    return qmap, kmap, vmap


def _softmax_parts(parts, extra=None):
    m = parts[0].max(axis=-1, keepdims=True)
    for p in parts[1:]:
        m = jnp.maximum(m, p.max(axis=-1, keepdims=True))
    if extra is not None:
        m = jnp.maximum(m, extra)
    es = [jnp.exp(p - m) for p in parts]
    l = es[0].sum(axis=-1, keepdims=True)
    for e in es[1:]:
        l = l + e.sum(axis=-1, keepdims=True)
    ex = None
    if extra is not None:
        ex = jnp.exp(extra - m)
        l = l + ex
    inv = 1.0 / l
    return [e * inv for e in es], (None if ex is None else ex * inv)


def _win_scores(q, k_ref, qb, tq, nctx, S):
    L = tq + 2 * WINDOW
    nqc = nctx // tq
    qlat = (qb - nqc) * tq
    start = pl.multiple_of(jnp.clip(qlat - WINDOW, 0, S - L), 128)
    kc = k_ref[0:nctx, :]
    kw = k_ref[pl.ds(nctx + start, L), :]
    s_c = _dot(q, kc, NT) * SCALE
    s_w = _dot(q, kw, NT) * SCALE
    qpos = qlat + lax.broadcasted_iota(jnp.int32, (tq, 1), 0)
    kpos = start + lax.broadcasted_iota(jnp.int32, (1, L), 1)
    valid = jnp.logical_and(jnp.abs(kpos - qpos) <= WINDOW, qb >= nqc)
    return s_c, jnp.where(valid, s_w, NEG), start, L


def _softmax_raw(raw):
    m = raw.max(axis=-1, keepdims=True)
    e = jnp.exp2((raw - m) * (SCALE * np.log2(np.e)))
    return e * (1.0 / e.sum(axis=-1, keepdims=True))


def _glob_keys(qb, tq, nctx, T):
    is_ctx = qb < nctx // tq
    return [(is_ctx, slice(0, nctx)), (jnp.logical_not(is_ctx), slice(0, T))]


def attn_even_fwd(name, qkvh, sink, nctx, tq):
    T = qkvh.shape[0]
    S = T - nctx
    qmap, kmap, vmap = _even_maps()

    def body(sink_ref, q_ref, k_ref, v_ref, o_ref):
        h, qb = pl.program_id(0), pl.program_id(1)
        q = q_ref[...]

        for pred, keys in _glob_keys(qb, tq, nctx, T):
            @pl.when(jnp.logical_and(h < 8, pred))
            def _():
                p = _softmax_raw(_dot(q, k_ref[keys, :], NT))
                o_ref[...] = _dot(p, v_ref[keys, :], NN).astype(BF16)

        @pl.when(h >= 8)
        def _():
            s_c, s_w, start, L = _win_scores(q, k_ref, qb, tq, nctx, S)
            sk = jnp.full((tq, 1), sink_ref[jnp.maximum(h - 8, 0)], F32)
            (p_c, p_w), _ = _softmax_parts([s_c, s_w], sk)
            o = _dot(p_c, v_ref[0:nctx, :], NN) + _dot(p_w, v_ref[pl.ds(nctx + start, L), :], NN)
            o_ref[...] = o.astype(BF16)

    return pl.pallas_call(
        body, grid=(16, T // tq),
        in_specs=[pl.BlockSpec(memory_space=pltpu.SMEM), pl.BlockSpec((tq, HEAD), qmap),
                  pl.BlockSpec((T, HEAD), kmap), pl.BlockSpec((T, HEAD), vmap)],
        out_specs=pl.BlockSpec((tq, HEAD), lambda h, qb: (qb, h)),
        out_shape=jax.ShapeDtypeStruct((T, 16 * HEAD), BF16),
        compiler_params=_cparams(("parallel", "arbitrary")), name=name)(sink, qkvh, qkvh, qkvh)


def attn_even_bwd(name, qkvh, sink, do, nctx, tq):
    T = qkvh.shape[0]
    S = T - nctx
    qmap, kmap, vmap = _even_maps()

    def body(sink_ref, q_ref, k_ref, v_ref, do_ref, dq_ref, dk_ref, dv_ref, ds_ref):
        h, qb = pl.program_id(0), pl.program_id(1)
        q = q_ref[...]
        dov = do_ref[...]

        @pl.when(jnp.logical_and(h % 4 == 0, qb == 0))
        def _():
            dk_ref[...] = jnp.zeros_like(dk_ref)
            dv_ref[...] = jnp.zeros_like(dv_ref)

        @pl.when(qb == 0)
        def _():
            ds_ref[...] = jnp.zeros_like(ds_ref)

        for pred, keys in _glob_keys(qb, tq, nctx, T):
            @pl.when(jnp.logical_and(h < 8, pred))
            def _():
                p = _softmax_raw(_dot(q, k_ref[keys, :], NT))
                dp = _dot(dov, v_ref[keys, :], NT)
                row = jnp.sum(p * dp, axis=-1, keepdims=True)
                dsb = (p * (dp - row) * SCALE).astype(BF16)
                dq_ref[...] = _dot(dsb, k_ref[keys, :], NN)
                dk_ref[keys, :] += _dot(dsb, q, TN)
                dv_ref[keys, :] += _dot(p, dov, TN)

        @pl.when(h >= 8)
        def _():
            s_c, s_w, start, L = _win_scores(q, k_ref, qb, tq, nctx, S)
            sk = jnp.full((tq, 1), sink_ref[jnp.maximum(h - 8, 0)], F32)
            (p_c, p_w), p_s = _softmax_parts([s_c, s_w], sk)
            win = pl.ds(nctx + start, L)
            dp_c = _dot(dov, v_ref[0:nctx, :], NT)
            dp_w = _dot(dov, v_ref[win, :], NT)
            row = jnp.sum(p_c * dp_c, axis=-1, keepdims=True) + jnp.sum(p_w * dp_w, axis=-1, keepdims=True)
            ds_c = (p_c * (dp_c - row) * SCALE).astype(BF16)
            ds_w = (p_w * (dp_w - row) * SCALE).astype(BF16)
            dq_ref[...] = _dot(ds_c, k_ref[0:nctx, :], NN) + _dot(ds_w, k_ref[win, :], NN)
            dk_ref[0:nctx, :] += _dot(ds_c, q, TN)
            dk_ref[win, :] += _dot(ds_w, q, TN)
            dv_ref[0:nctx, :] += _dot(p_c, dov, TN)
            dv_ref[win, :] += _dot(p_w, dov, TN)
            ds_ref[...] += jnp.sum(-(p_s * row))

    kv_out = pl.BlockSpec((T, HEAD), lambda h, qb: (0, h // 4))
    return pl.pallas_call(
        body, grid=(16, T // tq),
        in_specs=[pl.BlockSpec(memory_space=pltpu.SMEM), pl.BlockSpec((tq, HEAD), qmap),
                  pl.BlockSpec((T, HEAD), kmap), pl.BlockSpec((T, HEAD), vmap),
                  pl.BlockSpec((tq, HEAD), lambda h, qb: (qb, h))],
        out_specs=[pl.BlockSpec((tq, HEAD), lambda h, qb: (qb, h)), kv_out, kv_out,
                   pl.BlockSpec((None, 8, 128), lambda h, qb: (h, 0, 0))],
        out_shape=[jax.ShapeDtypeStruct((T, 16 * HEAD), F32), jax.ShapeDtypeStruct((T, 4 * HEAD), F32),
                   jax.ShapeDtypeStruct((T, 4 * HEAD), F32), jax.ShapeDtypeStruct((16, 8, 128), F32)],
        compiler_params=_cparams(("arbitrary", "arbitrary")), name=name)(sink, qkvh, qkvh, qkvh, do)


def _na_row_start(r, rows):
    return jnp.clip(r - NA_KH // 2, 0, rows - NA_KH)


NA_ROWS_PER_STEP = 8


def _na_scores(q, k_ref, bias_ref, r, rows, nctx):
    LW = NA_KH * GRID_W
    rs = _na_row_start(r, rows)
    win = pl.ds(pl.multiple_of(nctx + rs * GRID_W, GRID_W), LW)
    s_c = _dot(q, k_ref[0:nctx, :], NT) * SCALE
    s_w = _dot(q, k_ref[win, :], NT) * SCALE + bias_ref[r - rs]
    return s_c, s_w, win


def _na_specs(T, nctx, rb):
    LW = NA_KH * GRID_W
    qoff = nctx // (rb * GRID_W)
    assert nctx % (rb * GRID_W) == 0
    q_spec = pl.BlockSpec((rb * GRID_W, HEAD), lambda h, g: (g + qoff, h))
    k_spec = pl.BlockSpec((T, HEAD), lambda h, g: (0, 16 + h))
    v_spec = pl.BlockSpec((T, HEAD), lambda h, g: (0, 32 + h))
    b_spec = pl.BlockSpec((None, NA_KH, GRID_W, LW), lambda h, g: (h, 0, 0, 0))
    row_spec = pl.BlockSpec((rb * GRID_W, HEAD), lambda h, g: (g, h))
    return q_spec, k_spec, v_spec, b_spec, row_spec


def attn_odd_fwd(name, qkv, bias8, nctx):
    T = qkv.shape[0]
    S = T - nctx
    rows = S // GRID_W
    rb = min(NA_ROWS_PER_STEP, nctx // GRID_W)
    q_spec, k_spec, v_spec, b_spec, row_spec = _na_specs(T, nctx, rb)

    def body(q_ref, k_ref, v_ref, b_ref, o_ref):
        g = pl.program_id(1)
        for j in range(rb):
            qr = slice(j * GRID_W, (j + 1) * GRID_W)
            s_c, s_w, win = _na_scores(q_ref[qr, :], k_ref, b_ref, g * rb + j, rows, nctx)
            (p_c, p_w), _ = _softmax_parts([s_c, s_w])
            o_ref[qr, :] = (_dot(p_c, v_ref[0:nctx, :], NN) + _dot(p_w, v_ref[win, :], NN)).astype(BF16)

    return pl.pallas_call(
        body, grid=(16, rows // rb), in_specs=[q_spec, k_spec, v_spec, b_spec], out_specs=row_spec,
        out_shape=jax.ShapeDtypeStruct((S, 16 * HEAD), BF16),
        compiler_params=_cparams(("parallel", "arbitrary")), name=name)(qkv, qkv, qkv, bias8)


def attn_odd_bwd(name, qkv, bias8, do, nctx):
    T = qkv.shape[0]
    S = T - nctx
    rows = S // GRID_W
    rb = min(NA_ROWS_PER_STEP, nctx // GRID_W)
    q_spec, k_spec, v_spec, b_spec, row_spec = _na_specs(T, nctx, rb)

    def body(q_ref, k_ref, v_ref, b_ref, do_ref, dq_ref, dk_ref, dv_ref, db_ref):
        g = pl.program_id(1)

        @pl.when(g == 0)
        def _():
            dk_ref[...] = jnp.zeros_like(dk_ref)
            dv_ref[...] = jnp.zeros_like(dv_ref)
            db_ref[...] = jnp.zeros_like(db_ref)

        for j in range(rb):
            qr = slice(j * GRID_W, (j + 1) * GRID_W)
            r = g * rb + j
            q = q_ref[qr, :]
            dov = do_ref[qr, :]
            s_c, s_w, win = _na_scores(q, k_ref, b_ref, r, rows, nctx)
            (p_c, p_w), _ = _softmax_parts([s_c, s_w])
            dp_c = _dot(dov, v_ref[0:nctx, :], NT)
            dp_w = _dot(dov, v_ref[win, :], NT)
            row = jnp.sum(p_c * dp_c, axis=-1, keepdims=True) + jnp.sum(p_w * dp_w, axis=-1, keepdims=True)
            dsw = p_w * (dp_w - row)
            db_ref[r - _na_row_start(r, rows)] += dsw
            ds_c = (p_c * (dp_c - row) * SCALE).astype(BF16)
            ds_w = (dsw * SCALE).astype(BF16)
            dq_ref[qr, :] = _dot(ds_c, k_ref[0:nctx, :], NN) + _dot(ds_w, k_ref[win, :], NN)
            dk_ref[0:nctx, :] += _dot(ds_c, q, TN)
            dk_ref[win, :] += _dot(ds_w, q, TN)
            dv_ref[0:nctx, :] += _dot(p_c, dov, TN)
            dv_ref[win, :] += _dot(p_w, dov, TN)

    kv_out = pl.BlockSpec((T, HEAD), lambda h, g: (0, h))
    return pl.pallas_call(
        body, grid=(16, rows // rb), in_specs=[q_spec, k_spec, v_spec, b_spec, row_spec],
        out_specs=[row_spec, kv_out, kv_out, b_spec],
        out_shape=[jax.ShapeDtypeStruct((S, 16 * HEAD), F32), jax.ShapeDtypeStruct((T, 16 * HEAD), F32),
                   jax.ShapeDtypeStruct((T, 16 * HEAD), F32), jax.ShapeDtypeStruct(bias8.shape, F32)],
        compiler_params=_cparams(("arbitrary", "arbitrary")), name=name)(qkv, qkv, qkv, bias8, do)


def _na_onehots():
    o = np.arange(NA_KH)[:, None]
    i = np.arange(NA_KH)[None, :]
    a = i - o + NA_KH - 1
    A = (a[..., None] == np.arange(2 * NA_KH - 1)).astype(np.float32)
    qc = np.arange(GRID_W)[:, None]
    kc = np.arange(GRID_W)[None, :]
    b = np.clip(kc - qc + NA_KW - 1, 0, 2 * NA_KW - 2)
    cs = np.clip(qc - NA_KW // 2, 0, GRID_W - NA_KW)
    valid = (kc >= cs) & (kc < cs + NA_KW)
    B = ((b[..., None] == np.arange(2 * NA_KW - 1)) & valid[..., None]).astype(np.float32)
    return A, B, valid


def na_bias_table(rpb):
    A, B, valid = _na_onehots()
    hp = lax.Precision.HIGHEST
    t = jnp.einsum('hab,oia->hoib', rpb, jnp.asarray(A), precision=hp)
    bias = jnp.einsum('hoib,qkb->hoqik', t, jnp.asarray(B), precision=hp)
    bias = jnp.where(jnp.asarray(valid)[None, None, :, None, :], bias, NEG)
    return bias.reshape(rpb.shape[0], NA_KH, GRID_W, NA_KH * GRID_W)


def na_bias_grad(name, dbias8):
    A, B, _ = _na_onehots()
    H = dbias8.shape[0]
    nb, na = 2 * NA_KW - 1, 2 * NA_KH - 1
    d = dbias8.reshape(H, NA_KH, GRID_W, NA_KH, GRID_W).transpose(0, 1, 3, 2, 4)
    d = d.reshape(H * NA_KH * NA_KH, GRID_W * GRID_W)
    Bp = np.zeros((GRID_W * GRID_W, 128), np.float32)
    Bp[:, :nb] = B.reshape(GRID_W * GRID_W, nb)
    Ap = np.zeros((16, NA_KH * NA_KH), np.float32)
    Ap[:na] = A.reshape(NA_KH * NA_KH, na).T
    rows_per_head = NA_KH * NA_KH

    def split3(x):
        hi = x.astype(BF16)
        r1 = x - hi.astype(F32)
        mid = r1.astype(BF16)
        return hi, mid, (r1 - mid.astype(F32)).astype(BF16)

    def body(d_ref, b_ref, a_ref, o_ref):
        bm, am = b_ref[...], a_ref[...]
        g = sum(lax.dot_general(p, bm, NN, preferred_element_type=F32) for p in split3(d_ref[...]))
        o_ref[...] = sum(lax.dot_general(am, p, NN, preferred_element_type=F32) for p in split3(g))

    out = pl.pallas_call(
        body, grid=(H,),
        in_specs=[pl.BlockSpec((rows_per_head, GRID_W * GRID_W), lambda h: (h, 0)),
                  pl.BlockSpec((GRID_W * GRID_W, 128), lambda h: (0, 0)),
                  pl.BlockSpec((16, rows_per_head), lambda h: (0, 0))],
        out_specs=pl.BlockSpec((None, 16, 128), lambda h: (h, 0, 0)),
        out_shape=jax.ShapeDtypeStruct((H, 16, 128), F32),
        compiler_params=_cparams(("parallel",)), name=name)(d, jnp.asarray(Bp, BF16), jnp.asarray(Ap, BF16))
    return out[:, :na, :nb]


def _vmem_call(name, fn, out_shape, *arrays):
    def body(*refs):
        n = len(arrays)
        res = fn(*[r[...] for r in refs[:n]])
        if not isinstance(res, (tuple, list)):
            res = (res,)
        for o, v in zip(refs[n:], res):
            o[...] = v
    return pl.pallas_call(body, out_shape=out_shape, name=name,
                          compiler_params=pltpu.CompilerParams(vmem_limit_bytes=VMEM_LIMIT))(*arrays)


def _silu(v):
    return v / (1.0 + jnp.exp(-v))


def _adamw_math(w, g, m, v):
    m2 = ADAM_B1 * m + (1.0 - ADAM_B1) * g
    v2 = ADAM_B2 * v + (1.0 - ADAM_B2) * (g * g)
    m_hat = m2 / (1.0 - ADAM_B1 ** ADAM_STEP)
    v_hat = v2 / (1.0 - ADAM_B2 ** ADAM_STEP)
    delta = -ADAM_LR * (m_hat / (jnp.sqrt(v_hat) + ADAM_EPS) + ADAM_WD * w)
    return delta, m2, v2


def _ew_tile(R, C):
    return _tile(R, max(64, (262144 // C) // 64 * 64))


def adamw_rows(name, w, g, m, v, extra_g=None):
    R, C = w.shape
    tr = _ew_tile(R, C)
    extra_g = list(extra_g or [])
    ne = len(extra_g)

    def body(*refs):
        w_ref, g_ref, m_ref, v_ref = refs[:4]
        gs = g_ref[...]
        for e in refs[4:4 + ne]:
            gs = gs + e[...].astype(F32)
        go, do, mo, vo = refs[4 + ne:]
        d, m2, v2 = _adamw_math(w_ref[...], gs, m_ref[...], v_ref[...])
        go[...] = gs
        do[...] = d
        mo[...] = m2
        vo[...] = v2

    spec = pl.BlockSpec((tr, C), lambda i: (i, 0))
    return pl.pallas_call(
        body, grid=(R // tr,), in_specs=[spec] * (4 + ne), out_specs=[spec] * 4,
        out_shape=[jax.ShapeDtypeStruct((R, C), F32)] * 4,
        compiler_params=_cparams(("parallel",)), name=name)(w, g, m, v, *extra_g)


def rs_chip_sum(name, g8, sib4, half):
    _, R, C = g8.shape
    tr = _ew_tile(R, C)

    def body(s_ref, g_ref, b_ref, o_ref):
        o_ref[...] = (g_ref[...] + b_ref[...]).astype(BF16)

    blk = (None, tr, C)
    grid_spec = pltpu.PrefetchScalarGridSpec(
        num_scalar_prefetch=1, grid=(4, R // tr),
        in_specs=[pl.BlockSpec(blk, lambda q, i, s: (s[0] + q, i, 0)), pl.BlockSpec(blk, lambda q, i, s: (q, i, 0))],
        out_specs=pl.BlockSpec(blk, lambda q, i, s: (q, i, 0)))
    return pl.pallas_call(body, grid_spec=grid_spec, out_shape=jax.ShapeDtypeStruct((4, R, C), BF16),
                          compiler_params=_cparams(("parallel", "parallel")), name=name)(half, g8, sib4)


def adamw_rs(name, w, g8, sib4, rem3, m, v, idx):
    R, C = w.shape
    tr = _ew_tile(R, C)

    def body(s_ref, w_ref, g_ref, sb_ref, r0_ref, r1_ref, r2_ref, m_ref, v_ref, go, do, mo, vo):
        gs = g_ref[...] + sb_ref[...]
        for r_ref in (r0_ref, r1_ref, r2_ref):
            gs = gs + r_ref[...].astype(F32)
        d, m2, v2 = _adamw_math(w_ref[...], gs, m_ref[...], v_ref[...])
        go[...] = gs
        do[...] = d
        mo[...] = m2
        vo[...] = v2

    flat = pl.BlockSpec((tr, C), lambda i, s: (i, 0))
    blk = (None, tr, C)

    def rem(k):
        return pl.BlockSpec(blk, lambda i, s: (k, i, 0))

    grid_spec = pltpu.PrefetchScalarGridSpec(
        num_scalar_prefetch=1, grid=(R // tr,),
        in_specs=[flat, pl.BlockSpec(blk, lambda i, s: (s[0], i, 0)), pl.BlockSpec(blk, lambda i, s: (s[1], i, 0)),
                  rem(0), rem(1), rem(2), flat, flat],
        out_specs=[flat] * 4)
    return pl.pallas_call(body, grid_spec=grid_spec, out_shape=[jax.ShapeDtypeStruct((R, C), F32)] * 4,
                          compiler_params=_cparams(("parallel",)), name=name)(idx, w, g8, sib4, rem3, rem3, rem3, m, v)


def _me():
    x, y, c = lax.axis_index("x"), lax.axis_index("y"), lax.axis_index("c")
    return x, y, c


def _flip(v, bit):
    return 1 - v if bit else v


def ag_small(name, x, with_sum=False):
    R, C = x.shape

    def body(x_ref, out_ref, *rest):
        if with_sum:
            sum_ref, send_sems, recv_sems, lsem = rest
        else:
            send_sems, recv_sems, lsem = rest
        mx, my, mc = _me()
        me = 4 * mx + 2 * my + mc
        local = pltpu.make_async_copy(x_ref, out_ref.at[me], lsem)
        local.start()
        sends = []
        for k in range(1, NDEV):
            peer = (_flip(mx, k & 4), _flip(my, k & 2), _flip(mc, k & 1))
            cp = pltpu.make_async_remote_copy(src_ref=x_ref, dst_ref=out_ref.at[me], send_sem=send_sems.at[k - 1],
                                              recv_sem=recv_sems.at[k - 1], device_id=peer, device_id_type=MESH)
            cp.start()
            sends.append(cp)
        for k in range(1, NDEV):
            px, py, pc = _flip(mx, k & 4), _flip(my, k & 2), _flip(mc, k & 1)
            pltpu.make_async_remote_copy(src_ref=x_ref, dst_ref=out_ref.at[4 * px + 2 * py + pc],
                                         send_sem=send_sems.at[k - 1], recv_sem=recv_sems.at[k - 1],
                                         device_id=(px, py, pc), device_id_type=MESH).wait_recv()
        for cp in sends:
            cp.wait_send()
        local.wait()
        if with_sum:
            acc = out_ref[0]
            for d in range(1, NDEV):
                acc = acc + out_ref[d]
            sum_ref[...] = acc

    out_shape = [jax.ShapeDtypeStruct((NDEV, R, C), F32)]
    if with_sum:
        out_shape.append(jax.ShapeDtypeStruct((R, C), F32))
    vm = pl.BlockSpec(memory_space=pltpu.VMEM)
    res = pl.pallas_call(
        body, out_shape=out_shape, in_specs=[vm], out_specs=[vm] * len(out_shape),
        scratch_shapes=[pltpu.SemaphoreType.DMA((NDEV - 1,)), pltpu.SemaphoreType.DMA((NDEV - 1,)),
                        pltpu.SemaphoreType.DMA],
        compiler_params=pltpu.CompilerParams(vmem_limit_bytes=VMEM_LIMIT), name=name)(x)
    return res if with_sum else res[0]


def ag_big(name, shards):
    n = len(shards)

    def body(*refs):
        ins, outs = refs[:n], refs[n:2 * n]
        send_sems, recv_sems, lsems = refs[2 * n:]
        mx, my, mc = _me()
        me = (mx, my, mc)
        sibling = (mx, my, 1 - mc)
        chips = [(1 - mx, my), (mx, 1 - my), (1 - mx, 1 - my)]

        def idx(p):
            return 4 * p[0] + 2 * p[1] + p[2]

        def copy(t, k, block, to, src=None):
            dst = outs[t].at[idx(block)]
            return pltpu.make_async_remote_copy(
                src_ref=dst if src is None else src, dst_ref=dst, send_sem=send_sems.at[7 * t + k],
                recv_sem=recv_sems.at[7 * t + k], device_id=to, device_id_type=MESH)

        started = []
        locals_ = []
        for t in range(n):
            mine = pltpu.make_async_copy(ins[t], outs[t].at[idx(me)], lsems.at[t])
            mine.start()
            locals_.append(mine)
            first = [copy(t, 0, me, sibling, src=ins[t])]
            first += [copy(t, 1 + j, me, (*chip, mc), src=ins[t]) for j, chip in enumerate(chips)]
            for cp in first:
                cp.start()
            started += first
        for t in range(n):
            for j, chip in enumerate(chips):
                copy(t, 1 + j, (*chip, mc), me).wait_recv()
                fwd = copy(t, 4 + j, (*chip, mc), sibling)
                fwd.start()
                started.append(fwd)
        for t in range(n):
            copy(t, 0, sibling, me).wait_recv()
            for j, chip in enumerate(chips):
                copy(t, 4 + j, (*chip, 1 - mc), me).wait_recv()
        for cp in started:
            cp.wait_send()
        for mine in locals_:
            mine.wait()

    anyspec = pl.BlockSpec(memory_space=pl.ANY)
    return pl.pallas_call(
        body, out_shape=[jax.ShapeDtypeStruct((NDEV,) + s.shape, s.dtype) for s in shards],
        in_specs=[anyspec] * n, out_specs=[anyspec] * n,
        scratch_shapes=[pltpu.SemaphoreType.DMA((7 * n,)), pltpu.SemaphoreType.DMA((7 * n,)),
                        pltpu.SemaphoreType.DMA((n,))],
        name=name)(*shards)


def rs_sibling(name, grads):
    n = len(grads)

    def body(*refs):
        ins, outs = refs[:n], refs[n:2 * n]
        send_sems, recv_sems = refs[2 * n:]
        mx, my, mc = _me()
        sibling = (mx, my, 1 - mc)
        cps = []
        for t in range(n):
            cp = pltpu.make_async_remote_copy(
                src_ref=ins[t].at[pl.ds((1 - mc) * 4, 4)], dst_ref=outs[t], send_sem=send_sems.at[t],
                recv_sem=recv_sems.at[t], device_id=sibling, device_id_type=MESH)
            cp.start()
            cps.append(cp)
        for cp in cps:
            cp.wait()

    anyspec = pl.BlockSpec(memory_space=pl.ANY)
    return pl.pallas_call(
        body, out_shape=[jax.ShapeDtypeStruct((4,) + g.shape[1:], F32) for g in grads],
        in_specs=[anyspec] * n, out_specs=[anyspec] * n,
        scratch_shapes=[pltpu.SemaphoreType.DMA((n,)), pltpu.SemaphoreType.DMA((n,))],
        name=name)(*grads)


def rs_chips(name, parts):
    n = len(parts)

    def body(*refs):
        ins, outs = refs[:n], refs[n:2 * n]
        send_sems, recv_sems = refs[2 * n:]
        mx, my, mc = _me()
        cps = []
        for t in range(n):
            for k in range(1, 4):
                px, py = _flip(mx, k & 2), _flip(my, k & 1)
                cp = pltpu.make_async_remote_copy(
                    src_ref=ins[t].at[2 * px + py], dst_ref=outs[t].at[k - 1], send_sem=send_sems.at[3 * t + k - 1],
                    recv_sem=recv_sems.at[3 * t + k - 1], device_id=(px, py, mc), device_id_type=MESH)
                cp.start()
                cps.append(cp)
        for cp in cps:
            cp.wait()

    anyspec = pl.BlockSpec(memory_space=pl.ANY)
    return pl.pallas_call(
        body, out_shape=[jax.ShapeDtypeStruct((3,) + p.shape[1:], p.dtype) for p in parts],
        in_specs=[anyspec] * n, out_specs=[anyspec] * n,
        scratch_shapes=[pltpu.SemaphoreType.DMA((3 * n,)), pltpu.SemaphoreType.DMA((3 * n,))],
        name=name)(*parts)


def _rope_tables(S, nctx):
    t = jnp.arange(S)
    row = (t // GRID_W).astype(F32)
    col = (t % GRID_W).astype(F32)
    pairs = HEAD // 4
    inv = ROPE_THETA ** (-jnp.arange(pairs, dtype=F32) / pairs)
    ang_r = row[:, None] * inv
    ang_c = col[:, None] * inv
    ang = jnp.concatenate([ang_r, ang_r, ang_c, ang_c], axis=-1)
    cos = jnp.concatenate([jnp.ones((nctx, HEAD), F32), jnp.cos(ang)], axis=0)
    sin = jnp.concatenate([jnp.zeros((nctx, HEAD), F32), jnp.sin(ang)], axis=0)
    lane = jnp.arange(HEAD)[None, :]
    first = (lane & 32) == 0
    return cos, jnp.where(first, -sin, 0.0), jnp.where(first, 0.0, sin)


def _pad_rows(v, rows):
    v = v.reshape(-1).astype(F32)
    return jnp.pad(v, (0, rows * 128 - v.shape[0])).reshape(rows, 128)


def _rows8(n):
    return -(-n // 1024) * 8


def kernel(x, c, ctx, c_ctx, ada_w, ada_b, norm_w, mlp_w1, mlp_w2, ev_w_in, ev_w_out, ev_q_norm, ev_k_norm, ev_sink, od_w_in, od_w_out, od_rpb, final_norm_w, loss_target, m_c_ctx, m_ada_w, m_ada_b, m_norm_w, m_mlp_w1, m_mlp_w2, m_ev_w_in, m_ev_w_out, m_ev_q_norm, m_ev_k_norm, m_ev_sink, m_od_w_in, m_od_w_out, m_od_rpb, m_final_norm_w, v_c_ctx, v_ada_w, v_ada_b, v_norm_w, v_mlp_w1, v_mlp_w2, v_ev_w_in, v_ev_w_out, v_ev_q_norm, v_ev_k_norm, v_ev_sink, v_od_w_in, v_od_w_out, v_od_rpb, v_final_norm_w):
    S, D = x.shape[1], x.shape[2]
    NC = ctx.shape[1]
    T = NC + S
    assert NC == ROW_TILE and S % GRID_W == 0
    ada_cols = ada_w.shape[2]
    nw_cols = norm_w.shape[2]
    me = 4 * lax.axis_index("x") + 2 * lax.axis_index("y") + lax.axis_index("c")

    pack1 = jnp.concatenate([_pad_rows(c, _rows8(D)), _pad_rows(norm_w, _rows8(4 * nw_cols))], axis=0)
    g1 = ag_small("ag_c_normw", pack1)
    c_all = g1[:, :D // 128].reshape(NDEV, D)
    nw_rows = _rows8(D)
    nw = g1[:, nw_rows:nw_rows + 4 * nw_cols // 128].reshape(NDEV, 2, 2, nw_cols)
    nw = nw.transpose(1, 2, 0, 3).reshape(2, 2, D)
    cin = jnp.concatenate([c_all, jnp.broadcast_to(c_ctx[None], (NDEV, D))], axis=0)
    act = _vmem_call("silu_c", lambda v: _silu(v).astype(BF16), jax.ShapeDtypeStruct((2 * NDEV, D), BF16), cin)
    ada_b_loc = lax.dynamic_slice_in_dim(ada_b, me * ada_cols, ada_cols, axis=1)
    mods = [mm_nn(f"mod{i}", act, ada_w[i], _epi_bias, [F32], extras=(ada_b_loc[i:i + 1],), extra_kinds=('n',))[0]
            for i in range(2)]
    gm = ag_small("ag_mod", jnp.concatenate(mods, axis=1))
    gm = gm.reshape(NDEV, 2 * NDEV, 2, ada_cols).transpose(2, 1, 0, 3).reshape(2, 2 * NDEV, NDEV * ada_cols)
    mod_lat = lax.dynamic_index_in_dim(gm, me, axis=1, keepdims=False)
    mod_ctx = gm[:, NDEV]
    mod2 = jnp.stack([mod_ctx, mod_lat], axis=1).reshape(2, 2, 6, D)

    def chunk(i, j):
        return mod2[i, :, j, :]

    shards = [ev_w_in[0], ev_w_out[0], mlp_w1[0], mlp_w2[0], od_w_in[0], od_w_out[0], mlp_w1[1], mlp_w2[1]]
    gw = ag_big("ag_weights", [s.astype(BF16) for s in shards])
    w_in = [gw[0], gw[4]]
    w_out = [gw[1].reshape(-1, D), gw[5].reshape(-1, D)]
    w1 = [gw[2], gw[6]]
    w2 = [gw[3].reshape(-1, D), gw[7].reshape(-1, D)]

    cos, sa, sb = _rope_tables(S, NC)
    bias8 = na_bias_table(od_rpb[0])
    sink = ev_sink[0]
    TQ_F, TQ_B = 256, 128

    X0 = jnp.concatenate([ctx[0], x[0]], axis=0)
    h_a = norm_mod("l0_norm1", X0, nw[0, 0][None], chunk(0, 0), chunk(0, 1), NC)
    qkv0 = mm_nn("l0_qkv", h_a, w_in[0], _epi_store(F32), [F32])[0]
    qkvh0 = prep_even("l0_prep", qkv0, ev_q_norm, ev_k_norm, cos, sa, sb)
    o0 = attn_even_fwd("l0_attn", qkvh0, sink, NC, TQ_F)
    tm0 = _tile(T, 1100)
    X1, y0 = mm_nn("l0_out", o0, w_out[0], _epi_resid_gate(NC, tm0), [F32, F32], extras=(X0, chunk(0, 2)),
                   extra_kinds=('mn', 'n'))
    h_b = norm_mod("l0_norm2", X1, nw[0, 1][None], chunk(0, 3), chunk(0, 4), NC)
    a0, r0 = mm_nn("l0_up", h_b, w1[0], _epi_relu2, [BF16, BF16])
    X2, z0 = mm_nn("l0_down", a0, w2[0], _epi_resid_gate(NC, tm0), [F32, F32], extras=(X1, chunk(0, 5)),
                   extra_kinds=('mn', 'n'))

    h_c = norm_mod("l1_norm1", X2, nw[1, 0][None], chunk(1, 0), chunk(1, 1), NC)
    qkv1 = mm_nn("l1_qkv", h_c, w_in[1], _epi_store(BF16), [BF16])[0]
    o1 = attn_odd_fwd("l1_attn", qkv1, bias8, NC)
    X2l = X2[NC:]
    tm1 = _tile(S, 1100)
    X3, y1 = mm_nn("l1_out", o1, w_out[1], _epi_resid_gate(0, tm1), [F32, F32], extras=(X2l, chunk(1, 2)),
                   extra_kinds=('mn', 'n'))
    h_d = norm_mod("l1_norm2", X3, nw[1, 1][None], chunk(1, 3), chunk(1, 4), 0)
    a1, r1 = mm_nn("l1_up", h_d, w1[1], _epi_relu2, [BF16, BF16])
    X4, z1 = mm_nn("l1_down", a1, w2[1], _epi_resid_gate(0, tm1), [F32, F32], extras=(X3, chunk(1, 5)),
                   extra_kinds=('mn', 'n'))
    dX4, loss_p, dfw_p = final_loss("final_loss", X4, final_norm_w[None], loss_target[0])

    dz1, pg2_1 = gate_bwd("l1_gate2_bwd", dX4, z1, chunk(1, 5), 0)
    du1 = mm_nt("l1_down_dx", dz1, w2[1], _epi_mul2r, BF16, extras=(r1,))
    g_w2_1 = mm_tn("l1_down_dw", a1, dz1, 0)
    dh_d = mm_nt("l1_up_dx", du1, w1[1], _epi_store(F32), F32)
    g_w1_1 = mm_tn("l1_up_dw", h_d, du1, 1)
    dX3, pn2_1 = norm_bwd("l1_norm2_bwd", X3, dh_d, dX4, nw[1, 1][None], chunk(1, 4), 0)
    dy1, pg1_1 = gate_bwd("l1_gate1_bwd", dX3, y1, chunk(1, 2), 0)
    do1 = mm_nt("l1_out_dx", dy1, w_out[1], _epi_store(BF16), BF16)
    g_wout_1 = mm_tn("l1_out_dw", o1, dy1, 0)
    dq1, dk1, dv1, dbias8 = attn_odd_bwd("l1_attn_bwd", qkv1, bias8, do1, NC)
    dqkv1 = jnp.concatenate([jnp.pad(dq1, ((NC, 0), (0, 0))), dk1, dv1], axis=1).astype(BF16)
    dh_c = mm_nt("l1_qkv_dx", dqkv1, w_in[1], _epi_store(F32), F32)
    g_win_1 = mm_tn("l1_qkv_dw", h_c, dqkv1, 1)
    dX2, pn1_1 = norm_bwd("l1_norm1_bwd", X2, dh_c, jnp.pad(dX3, ((NC, 0), (0, 0))), nw[1, 0][None], chunk(1, 1), NC)
    d_rpb = na_bias_grad("rpb_grad", dbias8)

    dz0, pg2_0 = gate_bwd("l0_gate2_bwd", dX2, z0, chunk(0, 5), NC)
    du0 = mm_nt("l0_down_dx", dz0, w2[0], _epi_mul2r, BF16, extras=(r0,))
    g_w2_0 = mm_tn("l0_down_dw", a0, dz0, 0)
    dh_b = mm_nt("l0_up_dx", du0, w1[0], _epi_store(F32), F32)
    g_w1_0 = mm_tn("l0_up_dw", h_b, du0, 1)
    dX1, pn2_0 = norm_bwd("l0_norm2_bwd", X1, dh_b, dX2, nw[0, 1][None], chunk(0, 4), NC)
    dy0, pg1_0 = gate_bwd("l0_gate1_bwd", dX1, y0, chunk(0, 2), NC)
    do0 = mm_nt("l0_out_dx", dy0, w_out[0], _epi_store(BF16), BF16)
    g_wout_0 = mm_tn("l0_out_dw", o0, dy0, 0)
    dq0, dk0, dv0, dsink_p = attn_even_bwd("l0_attn_bwd", qkvh0, sink, do0, NC, TQ_B)
    dqkv0, pqk = prep_even_bwd("l0_prep_bwd", qkv0, dq0, dk0, dv0, ev_q_norm, ev_k_norm, cos, sa, sb)
    dh_a = mm_nt("l0_qkv_dx", dqkv0, w_in[0], _epi_store(F32), F32)
    g_win_0 = mm_tn("l0_qkv_dw", h_a, dqkv0, 1)
    dX0, pn1_0 = norm_bwd("l0_norm1_bwd", X0, dh_a, dX1, nw[0, 0][None], chunk(0, 1), NC)
    grad_x = dX0[NC:][None]

    def dmod(grp, pn1, pg1, pn2, pg2):
        return jnp.concatenate([pn1[grp], pn1[2 + grp], pg1[grp], pn2[grp], pn2[2 + grp], pg2[grp]])

    dmod_lat = jnp.stack([dmod(1, pn1_0, pg1_0, pn2_0, pg2_0), dmod(1, pn1_1, pg1_1, pn2_1, pg2_1)])
    dmod_ctx = jnp.stack([dmod(0, pn1_0, pg1_0, pn2_0, pg2_0), dmod(0, pn1_1, pg1_1, pn2_1, pg2_1)])
    dnw_p = jnp.stack([pn1_0[4], pn2_0[4], pn1_1[4], pn2_1[4]])
    pieces = [dmod_lat, dmod_ctx, dnw_p, pqk[0], pqk[1], dsink_p[8:, 0, 0], d_rpb, dfw_p[0], loss_p[0, 0]]
    sizes = [int(np.prod(p.shape)) for p in pieces]
    rows = [_rows8(s) for s in sizes]
    pack2 = jnp.concatenate([_pad_rows(p, r) for p, r in zip(pieces, rows)], axis=0)
    g2, s2 = ag_small("ag_small_grads", pack2, with_sum=True)
    offs = np.concatenate([[0], np.cumsum(rows)])

    def piece(arr, i, shape):
        return arr[..., offs[i]:offs[i + 1], :].reshape(arr.shape[:-2] + (-1,))[..., :sizes[i]].reshape(
            arr.shape[:-2] + shape)

    dmod_all = piece(g2, 0, (2, 6 * D))
    dmodc_sum = piece(s2, 1, (2, 6 * D))
    dnw_sum = piece(s2, 2, (2, 2, D))
    g_qn = piece(s2, 3, ev_q_norm.shape)
    g_kn = piece(s2, 4, ev_k_norm.shape)
    g_sink = piece(s2, 5, ev_sink.shape)
    g_rpb = piece(s2, 6, od_rpb.shape)
    g_fw = piece(s2, 7, final_norm_w.shape)
    loss = piece(s2, 8, ())

    dm16 = jnp.concatenate([dmod_all.transpose(1, 0, 2), dmodc_sum[:, None, :],
                            jnp.zeros((2, NDEV - 1, 6 * D), F32)], axis=1)
    dm16_loc = lax.dynamic_slice_in_dim(dm16.reshape(2, 2 * NDEV, NDEV, ada_cols), me, 1, axis=2)[:, :, 0, :]
    g_ada_b = _vmem_call("ada_b_grad", lambda v: jnp.sum(v, axis=1),
                         jax.ShapeDtypeStruct((2, 6 * D), F32), dm16)
    g_ada_w = []
    dact_p = None
    for i in range(2):
        dmb = dm16_loc[i].astype(BF16)
        g_ada_w.append(mm_tn(f"ada_w_grad{i}", act, dmb, None))
        part = mm_nt(f"ada_dact{i}", dmb, ada_w[i], _epi_store(F32), F32)
        dact_p = part if dact_p is None else dact_p + part
    _, dact = ag_small("ag_cctx", dact_p, with_sum=True)

    def cctx_grad(da, cc):
        sg = 1.0 / (1.0 + jnp.exp(-cc))
        return da[NDEV:NDEV + 1] * (sg * (1.0 + cc * (1.0 - sg)))

    g_cctx = _vmem_call("cctx_grad", cctx_grad, jax.ShapeDtypeStruct((1, D), F32), dact, c_ctx[None])[0]

    grads = [g_win_0, g_wout_0, g_w1_0, g_w2_0, g_win_1, g_wout_1, g_w1_1, g_w2_1]
    sib = rs_sibling("rs_sibling", grads)
    mc4 = (lax.axis_index("c") * 4).astype(jnp.int32)
    my_chip = (2 * lax.axis_index("x") + lax.axis_index("y")).astype(jnp.int32)
    parts = [rs_chip_sum(f"rs_chip_sum{t}", g, s_, mc4[None]) for t, (g, s_) in enumerate(zip(grads, sib))]
    rem = rs_chips("rs_chips", parts)
    own_idx = jnp.stack([mc4 + my_chip, my_chip])
    big = {}
    names = ['ev_w_in', 'ev_w_out', 'mlp_w1_0', 'mlp_w2_0', 'od_w_in', 'od_w_out', 'mlp_w1_1', 'mlp_w2_1']
    wts = [ev_w_in[0], ev_w_out[0], mlp_w1[0], mlp_w2[0], od_w_in[0], od_w_out[0], mlp_w1[1], mlp_w2[1]]
    ms = [m_ev_w_in[0], m_ev_w_out[0], m_mlp_w1[0], m_mlp_w2[0], m_od_w_in[0], m_od_w_out[0], m_mlp_w1[1], m_mlp_w2[1]]
    vs = [v_ev_w_in[0], v_ev_w_out[0], v_mlp_w1[0], v_mlp_w2[0], v_od_w_in[0], v_od_w_out[0], v_mlp_w1[1], v_mlp_w2[1]]
    for t in range(8):
        big[names[t]] = adamw_rs(f"adamw_{names[t]}", wts[t], grads[t], sib[t], rem[t], ms[t], vs[t], own_idx)

    def stack2(a, b):
        return tuple(jnp.stack([u, v_]) for u, v_ in zip(big[a], big[b]))

    def one(a):
        return tuple(u[None] for u in big[a])

    r_mlp_w1, r_mlp_w2 = stack2('mlp_w1_0', 'mlp_w1_1'), stack2('mlp_w2_0', 'mlp_w2_1')
    r_ev_w_in, r_ev_w_out, r_od_w_in, r_od_w_out = one('ev_w_in'), one('ev_w_out'), one('od_w_in'), one('od_w_out')

    g_ada = jnp.stack(g_ada_w)
    r_ada_w = adamw_rows("adamw_ada_w", ada_w.reshape(2 * D, ada_cols), g_ada.reshape(2 * D, ada_cols),
                         m_ada_w.reshape(2 * D, ada_cols), v_ada_w.reshape(2 * D, ada_cols))
    r_ada_w = tuple(u.reshape(2, D, ada_cols) for u in r_ada_w)

    g_nw_loc = lax.dynamic_slice_in_dim(dnw_sum, me * nw_cols, nw_cols, axis=2)
    small = [(c_ctx, g_cctx, m_c_ctx, v_c_ctx), (ada_b, g_ada_b, m_ada_b, v_ada_b),
             (norm_w, g_nw_loc, m_norm_w, v_norm_w), (ev_q_norm, g_qn, m_ev_q_norm, v_ev_q_norm),
             (ev_k_norm, g_kn, m_ev_k_norm, v_ev_k_norm), (ev_sink, g_sink, m_ev_sink, v_ev_sink),
             (od_rpb, g_rpb, m_od_rpb, v_od_rpb), (final_norm_w, g_fw, m_final_norm_w, v_final_norm_w)]
    srows = [_rows8(int(np.prod(w.shape))) for w, _, _, _ in small]
    packs = [jnp.concatenate([_pad_rows(tup[k], r) for tup, r in zip(small, srows)], axis=0) for k in range(4)]
    sres = adamw_rows("adamw_small", *packs)
    soffs = np.concatenate([[0], np.cumsum(srows)])

    def unpack(arr, i):
        w = small[i][0]
        return arr[soffs[i]:soffs[i + 1]].reshape(-1)[:int(np.prod(w.shape))].reshape(w.shape)

    sm = [[unpack(sres[k], i) for i in range(len(small))] for k in range(4)]

    def outs(k):
        big_k = {'ada_w': r_ada_w[k], 'mlp_w1': r_mlp_w1[k], 'mlp_w2': r_mlp_w2[k], 'ev_w_in': r_ev_w_in[k],
                 'ev_w_out': r_ev_w_out[k], 'od_w_in': r_od_w_in[k], 'od_w_out': r_od_w_out[k]}
        return (sm[k][0], big_k['ada_w'], sm[k][1], sm[k][2], big_k['mlp_w1'], big_k['mlp_w2'], big_k['ev_w_in'],
                big_k['ev_w_out'], sm[k][3], sm[k][4], sm[k][5], big_k['od_w_in'], big_k['od_w_out'], sm[k][6],
                sm[k][7])

    return (loss, grad_x, *outs(0), *outs(1), *outs(2), *outs(3))
```

```python
import numpy as np
import jax
import jax.numpy as jnp
from jax import lax
from jax.experimental import pallas as pl
from jax.experimental.pallas import tpu as pltpu

F32 = jnp.float32
BF16 = jnp.bfloat16
MESH = pl.DeviceIdType.MESH

NDEV = 8
HEAD = 128
GRID_W = 64
NA_KH, NA_KW = 8, 16
WINDOW = 128
ROPE_THETA = 10000.0
EPS = 1e-6
NEG = -1e30
SCALE = HEAD ** -0.5
ROW_TILE = 256
VMEM_LIMIT = 56 * 1024 * 1024

ADAM_LR, ADAM_B1, ADAM_B2, ADAM_EPS, ADAM_WD, ADAM_STEP = 0.001, 0.9, 0.999, 1e-08, 0.01, 10

NT = (((1,), (1,)), ((), ()))
NN = (((1,), (0,)), ((), ()))
TN = (((0,), (0,)), ((), ()))


def _cparams(sem):
    return pltpu.CompilerParams(dimension_semantics=sem, vmem_limit_bytes=VMEM_LIMIT)


def _tile(n, cap):
    if n <= cap:
        return n
    t = cap - cap % 64
    while t >= 64:
        if n % t == 0:
            return t
        t -= 64
    raise ValueError((n, cap))


def _dot(a, b, dims):
    return lax.dot_general(a.astype(BF16), b.astype(BF16), dims, preferred_element_type=F32)


def _slot(d):
    return (d % 2) * 4 + d // 2


class Exchange:
    def __init__(self, ins, out_shapes, aliases, n_sems, start, finish):
        self.ins, self.out_shapes, self.aliases, self.n_sems = list(ins), list(out_shapes), dict(aliases), n_sems
        self.start, self.finish = start, finish


def merge_exchanges(xs):
    ins, outs, aliases, bases, n = [], [], {}, [], 0
    for x in xs:
        bases.append((len(ins), len(outs), n))
        aliases.update({len(ins) + i: len(outs) + o for i, o in x.aliases.items()})
        ins += x.ins
        outs += x.out_shapes
        n += x.n_sems

    def run(which):
        def f(ci, co, ss, rs, base):
            for x, (i0, o0, s0) in zip(xs, bases):
                getattr(x, which)(ci[i0:i0 + len(x.ins)], co[o0:o0 + len(x.out_shapes)], ss, rs, base + s0)
        return f

    return Exchange(ins, outs, aliases, n, run('start'), run('finish'))


def _call(name, body, grid, ins, in_specs, out_shape, out_specs, scratch, sems, carry=None):
    if not carry:
        return pl.pallas_call(body, grid=grid, in_specs=in_specs, out_specs=out_specs, out_shape=out_shape,
                              scratch_shapes=scratch, compiler_params=_cparams(sems), name=name)(*ins)
    x = merge_exchanges(carry)
    n_in, n_ci, n_out, n_co, n_sc = len(ins), len(x.ins), len(out_shape), len(x.out_shapes), len(scratch)

    def wrapped(*refs):
        p = [0]

        def take(k):
            p[0] += k
            return refs[p[0] - k:p[0]]

        a, ci, o, co, sc = take(n_in), take(n_ci), take(n_out), take(n_co), take(n_sc)
        ss, rs = take(2)
        first = pl.program_id(0) == 0
        last = pl.program_id(0) == grid[0] - 1
        for d in range(1, len(grid)):
            first = jnp.logical_and(first, pl.program_id(d) == 0)
            last = jnp.logical_and(last, pl.program_id(d) == grid[d] - 1)

        @pl.when(first)
        def _():
            x.start(ci, co, ss, rs, 0)

        body(*a, *o, *sc)

        @pl.when(last)
        def _():
            x.finish(ci, co, ss, rs, 0)

    hbm = pl.BlockSpec(memory_space=pl.ANY)
    res = pl.pallas_call(
        wrapped, grid=grid, in_specs=list(in_specs) + [hbm] * n_ci, out_specs=list(out_specs) + [hbm] * n_co,
        out_shape=list(out_shape) + x.out_shapes,
        input_output_aliases={n_in + i: n_out + o for i, o in x.aliases.items()},
        scratch_shapes=list(scratch) + [pltpu.SemaphoreType.DMA((x.n_sems,)), pltpu.SemaphoreType.DMA((x.n_sems,))],
        compiler_params=_cparams(("arbitrary",) * len(grid)), name=name)(*ins, *x.ins)
    return list(res[:n_out]) + [list(res[n_out:])]


def _mm_core(name, grid, ins, in_specs, out_shape, out_specs, dims, acc_shape, epi, carry=None):
    nk = grid[2]
    n_extra = len(ins) - 2

    def body(*refs):
        a_ref, b_ref = refs[0], refs[1]
        ex = refs[2:2 + n_extra]
        outs = refs[2 + n_extra:-1]
        acc = refs[-1]
        k = pl.program_id(2)

        @pl.when(k == 0)
        def _():
            acc[...] = jnp.zeros_like(acc)

        acc[...] += _dot(a_ref[...], b_ref[...], dims)

        @pl.when(k == nk - 1)
        def _():
            epi(acc[...], ex, outs)

    return _call(name, body, grid, ins, in_specs, out_shape, out_specs, [pltpu.VMEM(acc_shape, F32)],
                 ("parallel", "parallel", "arbitrary"), carry)


def _split(res, n, carry):
    own = res[0] if n == 1 else list(res[:n])
    return (own, res[n]) if carry else own


def _epi_store(dtype):
    def epi(acc, ex, outs):
        outs[0][...] = acc.astype(dtype)
    return epi


def _epi_bias(acc, ex, outs):
    outs[0][...] = acc + ex[0][...]


def _epi_relu2(acc, ex, outs):
    r = jnp.maximum(acc, 0.0)
    outs[0][...] = (r * r).astype(BF16)
    outs[1][...] = r.astype(BF16)


def _epi_mul2r(acc, ex, outs):
    outs[0][...] = (acc * (2.0 * ex[0][...].astype(F32))).astype(BF16)


def _epi_resid_gate(nctx, tm):
    def epi(acc, ex, outs):
        rows = pl.program_id(0) * tm + lax.broadcasted_iota(jnp.int32, (tm, 1), 0)
        g = jnp.where(rows < nctx, ex[1][0:1, :], ex[1][1:2, :])
        outs[0][...] = ex[0][...] + g * acc
        outs[1][...] = acc
    return epi


def mm_nn(name, a, w, epi, outs, extras=(), extra_kinds=(), tm_cap=1100, tn_cap=512, tk_cap=2048, carry=None):
    M, K = a.shape
    if w.ndim == 3:
        ns = w.shape[2]
        N = NDEV * ns
        tn = _tile(ns, tn_cap)
        nper = ns // tn
    else:
        N = w.shape[1]
        tn = _tile(N, tn_cap)
    tm = _tile(M, tm_cap)
    tk = _tile(K, tk_cap)
    grid = (M // tm, N // tn, K // tk)
    a_spec = pl.BlockSpec((tm, tk), lambda i, j, k: (i, k))
    if w.ndim == 3:
        b_spec = pl.BlockSpec((None, tk, tn), lambda i, j, k: (j // nper, k, j % nper))
    else:
        b_spec = pl.BlockSpec((tk, tn), lambda i, j, k: (k, j))
    ex_specs = []
    for e, kind in zip(extras, extra_kinds):
        if kind == 'mn':
            ex_specs.append(pl.BlockSpec((tm, tn), lambda i, j, k: (i, j)))
        else:
            ex_specs.append(pl.BlockSpec((e.shape[0], tn), lambda i, j, k: (0, j)))
    out_shape = [jax.ShapeDtypeStruct((M, N), dt) for dt in outs]
    out_specs = [pl.BlockSpec((tm, tn), lambda i, j, k: (i, j)) for _ in outs]
    res = _mm_core(name, grid, (a, w, *extras), [a_spec, b_spec, *ex_specs], out_shape, out_specs, NN, (tm, tn), epi,
                   carry)
    return (list(res[:len(outs)]), res[len(outs)]) if carry else res


def mm_nt(name, a, w, epi, out_dtype, extras=(), tm_cap=1100, to_cap=512, tc_cap=2048, carry=None):
    M, N = a.shape
    tm = _tile(M, tm_cap)
    if w.ndim == 3:
        Kw, ns = w.shape[1], w.shape[2]
        tc = _tile(ns, tc_cap)
        cper = ns // tc
    else:
        Kw = w.shape[0]
        tc = _tile(N, tc_cap)
    to = _tile(Kw, to_cap)
    grid = (M // tm, Kw // to, N // tc)
    a_spec = pl.BlockSpec((tm, tc), lambda i, j, k: (i, k))
    if w.ndim == 3:
        b_spec = pl.BlockSpec((None, to, tc), lambda i, j, k: (k // cper, j, k % cper))
    else:
        b_spec = pl.BlockSpec((to, tc), lambda i, j, k: (j, k))
    ex_specs = [pl.BlockSpec((tm, to), lambda i, j, k: (i, j)) for _ in extras]
    out_shape = [jax.ShapeDtypeStruct((M, Kw), out_dtype)]
    out_specs = [pl.BlockSpec((tm, to), lambda i, j, k: (i, j))]
    return _split(_mm_core(name, grid, (a, w, *extras), [a_spec, b_spec, *ex_specs], out_shape, out_specs, NT, (tm, to),
                           epi, carry), 1, carry)


def mm_tn(name, a, b, shard_axis, to_cap=1024, tn_cap=512, tc_cap=1100, carry=None):
    M, Ka = a.shape
    N = b.shape[1]
    tc = _tile(M, tc_cap)
    if shard_axis is None:
        to, tn = _tile(Ka, to_cap), _tile(N, tn_cap)
        shape = (Ka, N)
        oblk = (to, tn)
        omap = lambda i, j, k: (i, j)
    elif shard_axis == 1:
        ns = N // NDEV
        to, tn = _tile(Ka, to_cap), _tile(ns, tn_cap)
        per = ns // tn
        shape = (NDEV, Ka, ns)
        oblk = (None, to, tn)
        omap = lambda i, j, k: (_slot(j // per), i, j % per)
    else:
        rs = Ka // NDEV
        to, tn = _tile(rs, to_cap), _tile(N, tn_cap)
        per = rs // to
        shape = (NDEV, rs, N)
        oblk = (None, to, tn)
        omap = lambda i, j, k: (_slot(i // per), i % per, j)
    grid = (Ka // to, N // tn, M // tc)
    a_spec = pl.BlockSpec((tc, to), lambda i, j, k: (k, i))
    b_spec = pl.BlockSpec((tc, tn), lambda i, j, k: (k, j))
    out_shape = [jax.ShapeDtypeStruct(shape, F32)]
    out_specs = [pl.BlockSpec(oblk, omap)]
    return _split(_mm_core(name, grid, (a, b), [a_spec, b_spec], out_shape, out_specs, TN, (to, tn), _epi_store(F32),
                           carry), 1, carry)


def _row_spec(D):
    return pl.BlockSpec((ROW_TILE, D), lambda i: (i, 0))


def _const_spec(r, D):
    return pl.BlockSpec((r, D), lambda i: (0, 0))


def _grp(ref, is_ctx):
    return jnp.where(is_ctx, ref[0:1, :], ref[1:2, :])


def norm_mod(name, x, nw, sh, sc, nctx):
    R, D = x.shape
    assert R % ROW_TILE == 0 and nctx % ROW_TILE == 0

    def body(x_ref, nw_ref, sh_ref, sc_ref, o_ref):
        is_ctx = pl.program_id(0) * ROW_TILE < nctx
        xv = x_ref[...]
        rstd = lax.rsqrt(jnp.mean(xv * xv, axis=-1, keepdims=True) + EPS)
        n = xv * rstd * nw_ref[...]
        o_ref[...] = (n * (1.0 + _grp(sc_ref, is_ctx)) + _grp(sh_ref, is_ctx)).astype(BF16)

    return pl.pallas_call(
        body, grid=(R // ROW_TILE,),
        in_specs=[_row_spec(D), _const_spec(1, D), _const_spec(2, D), _const_spec(2, D)],
        out_specs=_row_spec(D), out_shape=jax.ShapeDtypeStruct((R, D), BF16),
        compiler_params=_cparams(("parallel",)), name=name)(x, nw, sh, sc)


def norm_bwd(name, x, dh, dres, nw, sc, nctx):
    R, D = x.shape
    assert R % ROW_TILE == 0 and nctx % ROW_TILE == 0

    def body(x_ref, dh_ref, dres_ref, nw_ref, sc_ref, dx_ref, part_ref):
        i = pl.program_id(0)
        is_ctx = i * ROW_TILE < nctx

        @pl.when(i == 0)
        def _():
            part_ref[...] = jnp.zeros_like(part_ref)

        xv = x_ref[...]
        dhv = dh_ref[...]
        w = nw_ref[...]
        rstd = lax.rsqrt(jnp.mean(xv * xv, axis=-1, keepdims=True) + EPS)
        xhat = xv * rstd
        n = xhat * w
        dn = dhv * (1.0 + _grp(sc_ref, is_ctx))
        dxhat = dn * w
        dx_ref[...] = dres_ref[...] + rstd * (dxhat - xhat * jnp.mean(dxhat * xhat, axis=-1, keepdims=True))
        s_sh = jnp.sum(dhv, axis=0, keepdims=True)
        s_sc = jnp.sum(dhv * n, axis=0, keepdims=True)
        s_nw = jnp.sum(dn * xhat, axis=0, keepdims=True)
        zero = jnp.zeros_like(s_sh)
        part_ref[0:1, :] += jnp.where(is_ctx, s_sh, zero)
        part_ref[1:2, :] += jnp.where(is_ctx, zero, s_sh)
        part_ref[2:3, :] += jnp.where(is_ctx, s_sc, zero)
        part_ref[3:4, :] += jnp.where(is_ctx, zero, s_sc)
        part_ref[4:5, :] += s_nw

    return pl.pallas_call(
        body, grid=(R // ROW_TILE,),
        in_specs=[_row_spec(D), _row_spec(D), _row_spec(D), _const_spec(1, D), _const_spec(2, D)],
        out_specs=[_row_spec(D), _const_spec(8, D)],
        out_shape=[jax.ShapeDtypeStruct((R, D), F32), jax.ShapeDtypeStruct((8, D), F32)],
        compiler_params=_cparams(("arbitrary",)), name=name)(x, dh, dres, nw, sc)


def gate_bwd(name, dx, y, g, nctx):
    R, D = dx.shape
    assert R % ROW_TILE == 0 and nctx % ROW_TILE == 0

    def body(dx_ref, y_ref, g_ref, dy_ref, part_ref):
        i = pl.program_id(0)
        is_ctx = i * ROW_TILE < nctx

        @pl.when(i == 0)
        def _():
            part_ref[...] = jnp.zeros_like(part_ref)

        dxv = dx_ref[...]
        dy_ref[...] = (dxv * _grp(g_ref, is_ctx)).astype(BF16)
        s = jnp.sum(dxv * y_ref[...], axis=0, keepdims=True)
        zero = jnp.zeros_like(s)
        part_ref[0:1, :] += jnp.where(is_ctx, s, zero)
        part_ref[1:2, :] += jnp.where(is_ctx, zero, s)

    return pl.pallas_call(
        body, grid=(R // ROW_TILE,),
        in_specs=[_row_spec(D), _row_spec(D), _const_spec(2, D)],
        out_specs=[_row_spec(D), _const_spec(8, D)],
        out_shape=[jax.ShapeDtypeStruct((R, D), BF16), jax.ShapeDtypeStruct((8, D), F32)],
        compiler_params=_cparams(("arbitrary",)), name=name)(dx, y, g)


def final_loss(name, x, fw, tgt):
    S, D = x.shape

    def body(x_ref, fw_ref, t_ref, dx_ref, loss_ref, dfw_ref):
        i = pl.program_id(0)

        @pl.when(i == 0)
        def _():
            loss_ref[...] = jnp.zeros_like(loss_ref)
            dfw_ref[...] = jnp.zeros_like(dfw_ref)

        xv = x_ref[...]
        w = fw_ref[...]
        rstd = lax.rsqrt(jnp.mean(xv * xv, axis=-1, keepdims=True) + EPS)
        xhat = xv * rstd
        e = xhat * w - t_ref[...]
        loss_ref[...] += 0.5 * jnp.sum(jnp.mean(e * e, axis=-1, keepdims=True))
        dout = e * (1.0 / D)
        dfw_ref[0:1, :] += jnp.sum(dout * xhat, axis=0, keepdims=True)
        dxhat = dout * w
        dx_ref[...] = rstd * (dxhat - xhat * jnp.mean(dxhat * xhat, axis=-1, keepdims=True))

    return pl.pallas_call(
        body, grid=(S // ROW_TILE,),
        in_specs=[_row_spec(D), _const_spec(1, D), _row_spec(D)],
        out_specs=[_row_spec(D), pl.BlockSpec((8, 128), lambda i: (0, 0)), _const_spec(8, D)],
        out_shape=[jax.ShapeDtypeStruct((S, D), F32), jax.ShapeDtypeStruct((8, 128), F32),
                   jax.ShapeDtypeStruct((8, D), F32)],
        compiler_params=_cparams(("arbitrary",)), name=name)(x, fw, tgt)


def _rope(x, cos, sa, sb):
    return x * cos + pltpu.roll(x, 96, 1) * sa + pltpu.roll(x, 32, 1) * sb


def _rope_t(dy, cos, sa, sb):
    return dy * cos + pltpu.roll(dy * sa, 32, 1) + pltpu.roll(dy * sb, 96, 1)


_EVEN_KINDS = ['qa'] * 8 + ['ka'] * 2 + ['v'] * 2 + ['qb'] * 8 + ['kb'] * 2 + ['v'] * 2
_EVEN_DSRC = ([('q', j) for j in range(8)] + [('k', 0), ('k', 1), ('v', 0), ('v', 1)]
              + [('q', 8 + j) for j in range(8)] + [('k', 2), ('k', 3), ('v', 2), ('v', 3)])


def _cols(j):
    return slice(j * HEAD, (j + 1) * HEAD)


def prep_even(name, qkv, qn, kn, cos, sa, sb):
    T, W = qkv.shape

    def body(x_ref, qn_ref, kn_ref, cos_ref, sa_ref, sb_ref, o_ref):
        cos_, sa_, sb_ = cos_ref[...], sa_ref[...], sb_ref[...]
        for j, kind in enumerate(_EVEN_KINDS):
            x = x_ref[:, _cols(j)]
            if kind in ('qa', 'ka'):
                rstd = lax.rsqrt(jnp.mean(x * x, axis=-1, keepdims=True) + EPS)
                x = x * rstd * (qn_ref[...] if kind == 'qa' else kn_ref[...])
            if kind != 'v':
                x = _rope(x, cos_, sa_, sb_)
            o_ref[:, _cols(j)] = x.astype(BF16)

    blk = pl.BlockSpec((ROW_TILE, W), lambda i: (i, 0))
    tab = pl.BlockSpec((ROW_TILE, HEAD), lambda i: (i, 0))
    one = pl.BlockSpec((1, HEAD), lambda i: (0, 0))
    return pl.pallas_call(
        body, grid=(T // ROW_TILE,), in_specs=[blk, one, one, tab, tab, tab], out_specs=blk,
        out_shape=jax.ShapeDtypeStruct(qkv.shape, BF16),
        compiler_params=_cparams(("parallel",)), name=name)(qkv, qn, kn, cos, sa, sb)


def prep_even_bwd(name, qkv, dq, dk, dv, qn, kn, cos, sa, sb):
    T, W = qkv.shape

    def body(x_ref, dq_ref, dk_ref, dv_ref, qn_ref, kn_ref, cos_ref, sa_ref, sb_ref, o_ref, part_ref):
        @pl.when(pl.program_id(0) == 0)
        def _():
            part_ref[...] = jnp.zeros_like(part_ref)

        cos_, sa_, sb_ = cos_ref[...], sa_ref[...], sb_ref[...]
        src = {'q': dq_ref, 'k': dk_ref, 'v': dv_ref}
        sums = {'qa': None, 'ka': None}
        for j, kind in enumerate(_EVEN_KINDS):
            which, blk_j = _EVEN_DSRC[j]
            d = src[which][:, _cols(blk_j)]
            if kind != 'v':
                d = _rope_t(d, cos_, sa_, sb_)
            if kind in ('qa', 'ka'):
                x = x_ref[:, _cols(j)]
                rstd = lax.rsqrt(jnp.mean(x * x, axis=-1, keepdims=True) + EPS)
                xhat = x * rstd
                s = jnp.sum(d * xhat, axis=0, keepdims=True)
                sums[kind] = s if sums[kind] is None else sums[kind] + s
                dxhat = d * (qn_ref[...] if kind == 'qa' else kn_ref[...])
                d = rstd * (dxhat - xhat * jnp.mean(dxhat * xhat, axis=-1, keepdims=True))
            o_ref[:, _cols(j)] = d.astype(BF16)
        part_ref[0:1, :] += sums['qa']
        part_ref[1:2, :] += sums['ka']

    def rows(w):
        return pl.BlockSpec((ROW_TILE, w), lambda i: (i, 0))

    one = pl.BlockSpec((1, HEAD), lambda i: (0, 0))
    return pl.pallas_call(
        body, grid=(T // ROW_TILE,),
        in_specs=[rows(W), rows(dq.shape[1]), rows(dk.shape[1]), rows(dv.shape[1]), one, one,
                  rows(HEAD), rows(HEAD), rows(HEAD)],
        out_specs=[rows(W), pl.BlockSpec((8, HEAD), lambda i: (0, 0))],
        out_shape=[jax.ShapeDtypeStruct(qkv.shape, BF16), jax.ShapeDtypeStruct((8, HEAD), F32)],
        compiler_params=_cparams(("arbitrary",)), name=name)(qkv, dq, dk, dv, qn, kn, cos, sa, sb)


def _even_maps():
    qmap = lambda h, qb: (qb, jnp.where(h < 8, h, h + 4))
    kmap = lambda h, qb: (0, jnp.where(h < 8, 8 + h // 4, 18 + h // 4))
    vmap = lambda h, qb: (0, jnp.where(h < 8, 10 + h // 4, 20 + h // 4))
    return qmap, kmap, vmap


def _softmax_parts(parts, extra=None):
    m = parts[0].max(axis=-1, keepdims=True)
    for p in parts[1:]:
        m = jnp.maximum(m, p.max(axis=-1, keepdims=True))
    if extra is not None:
        m = jnp.maximum(m, extra)
    es = [jnp.exp(p - m) for p in parts]
    l = es[0].sum(axis=-1, keepdims=True)
    for e in es[1:]:
        l = l + e.sum(axis=-1, keepdims=True)
    ex = None
    if extra is not None:
        ex = jnp.exp(extra - m)
        l = l + ex
    inv = 1.0 / l
    return [e * inv for e in es], (None if ex is None else ex * inv)


def _win_scores(q, k_ref, qb, tq, nctx, S):
    L = tq + 2 * WINDOW
    nqc = nctx // tq
    qlat = (qb - nqc) * tq
    start = pl.multiple_of(jnp.clip(qlat - WINDOW, 0, S - L), 128)
    kc = k_ref[0:nctx, :]
    kw = k_ref[pl.ds(nctx + start, L), :]
    s_c = _dot(q, kc, NT) * SCALE
    s_w = _dot(q, kw, NT) * SCALE
    qpos = qlat + lax.broadcasted_iota(jnp.int32, (tq, 1), 0)
    kpos = start + lax.broadcasted_iota(jnp.int32, (1, L), 1)
    valid = jnp.logical_and(jnp.abs(kpos - qpos) <= WINDOW, qb >= nqc)
    return s_c, jnp.where(valid, s_w, NEG), start, L


def _softmax_raw(raw):
    m = raw.max(axis=-1, keepdims=True)
    e = jnp.exp2((raw - m) * (SCALE * np.log2(np.e)))
    return e * (1.0 / e.sum(axis=-1, keepdims=True))


def _glob_keys(qb, tq, nctx, T):
    is_ctx = qb < nctx // tq
    return [(is_ctx, slice(0, nctx)), (jnp.logical_not(is_ctx), slice(0, T))]


def attn_even_fwd(name, qkvh, sink, nctx, tq, carry=None):
    T = qkvh.shape[0]
    S = T - nctx
    qmap, kmap, vmap = _even_maps()

    def body(sink_ref, q_ref, k_ref, v_ref, o_ref):
        h, qb = pl.program_id(0), pl.program_id(1)
        q = q_ref[...]

        for pred, keys in _glob_keys(qb, tq, nctx, T):
            @pl.when(jnp.logical_and(h < 8, pred))
            def _():
                p = _softmax_raw(_dot(q, k_ref[keys, :], NT))
                o_ref[...] = _dot(p, v_ref[keys, :], NN).astype(BF16)

        @pl.when(h >= 8)
        def _():
            s_c, s_w, start, L = _win_scores(q, k_ref, qb, tq, nctx, S)
            sk = jnp.full((tq, 1), sink_ref[jnp.maximum(h - 8, 0)], F32)
            (p_c, p_w), _ = _softmax_parts([s_c, s_w], sk)
            o = _dot(p_c, v_ref[0:nctx, :], NN) + _dot(p_w, v_ref[pl.ds(nctx + start, L), :], NN)
            o_ref[...] = o.astype(BF16)

    res = _call(name, body, (16, T // tq), (sink, qkvh, qkvh, qkvh),
                [pl.BlockSpec(memory_space=pltpu.SMEM), pl.BlockSpec((tq, HEAD), qmap),
                 pl.BlockSpec((T, HEAD), kmap), pl.BlockSpec((T, HEAD), vmap)],
                [jax.ShapeDtypeStruct((T, 16 * HEAD), BF16)], [pl.BlockSpec((tq, HEAD), lambda h, qb: (qb, h))],
                [], ("parallel", "arbitrary"), carry)
    return _split(res, 1, carry)


def attn_even_bwd(name, qkvh, sink, do, nctx, tq, carry=None):
    T = qkvh.shape[0]
    S = T - nctx
    qmap, kmap, vmap = _even_maps()

    def body(sink_ref, q_ref, k_ref, v_ref, do_ref, dq_ref, dk_ref, dv_ref, ds_ref):
        h, qb = pl.program_id(0), pl.program_id(1)
        q = q_ref[...]
        dov = do_ref[...]

        @pl.when(jnp.logical_and(h % 4 == 0, qb == 0))
        def _():
            dk_ref[...] = jnp.zeros_like(dk_ref)
            dv_ref[...] = jnp.zeros_like(dv_ref)

        @pl.when(qb == 0)
        def _():
            ds_ref[...] = jnp.zeros_like(ds_ref)

        for pred, keys in _glob_keys(qb, tq, nctx, T):
            @pl.when(jnp.logical_and(h < 8, pred))
            def _():
                p = _softmax_raw(_dot(q, k_ref[keys, :], NT))
                dp = _dot(dov, v_ref[keys, :], NT)
                row = jnp.sum(p * dp, axis=-1, keepdims=True)
                dsb = (p * (dp - row) * SCALE).astype(BF16)
                dq_ref[...] = _dot(dsb, k_ref[keys, :], NN)
                dk_ref[keys, :] += _dot(dsb, q, TN)
                dv_ref[keys, :] += _dot(p, dov, TN)

        @pl.when(h >= 8)
        def _():
            s_c, s_w, start, L = _win_scores(q, k_ref, qb, tq, nctx, S)
            sk = jnp.full((tq, 1), sink_ref[jnp.maximum(h - 8, 0)], F32)
            (p_c, p_w), p_s = _softmax_parts([s_c, s_w], sk)
            win = pl.ds(nctx + start, L)
            dp_c = _dot(dov, v_ref[0:nctx, :], NT)
            dp_w = _dot(dov, v_ref[win, :], NT)
            row = jnp.sum(p_c * dp_c, axis=-1, keepdims=True) + jnp.sum(p_w * dp_w, axis=-1, keepdims=True)
            ds_c = (p_c * (dp_c - row) * SCALE).astype(BF16)
            ds_w = (p_w * (dp_w - row) * SCALE).astype(BF16)
            dq_ref[...] = _dot(ds_c, k_ref[0:nctx, :], NN) + _dot(ds_w, k_ref[win, :], NN)
            dk_ref[0:nctx, :] += _dot(ds_c, q, TN)
            dk_ref[win, :] += _dot(ds_w, q, TN)
            dv_ref[0:nctx, :] += _dot(p_c, dov, TN)
            dv_ref[win, :] += _dot(p_w, dov, TN)
            ds_ref[...] += jnp.sum(-(p_s * row))

    kv_out = pl.BlockSpec((T, HEAD), lambda h, qb: (0, h // 4))
    res = _call(name, body, (16, T // tq), (sink, qkvh, qkvh, qkvh, do),
                [pl.BlockSpec(memory_space=pltpu.SMEM), pl.BlockSpec((tq, HEAD), qmap),
                 pl.BlockSpec((T, HEAD), kmap), pl.BlockSpec((T, HEAD), vmap),
                 pl.BlockSpec((tq, HEAD), lambda h, qb: (qb, h))],
                [jax.ShapeDtypeStruct((T, 16 * HEAD), F32), jax.ShapeDtypeStruct((T, 4 * HEAD), F32),
                 jax.ShapeDtypeStruct((T, 4 * HEAD), F32), jax.ShapeDtypeStruct((16, 8, 128), F32)],
                [pl.BlockSpec((tq, HEAD), lambda h, qb: (qb, h)), kv_out, kv_out,
                 pl.BlockSpec((None, 8, 128), lambda h, qb: (h, 0, 0))],
                [], ("arbitrary", "arbitrary"), carry)
    return _split(res, 4, carry)


def _na_row_start(r, rows):
    return jnp.clip(r - NA_KH // 2, 0, rows - NA_KH)


NA_ROWS_PER_STEP = 8


def _na_scores(q, k_ref, bias_ref, r, rows, nctx):
    LW = NA_KH * GRID_W
    rs = _na_row_start(r, rows)
    win = pl.ds(pl.multiple_of(nctx + rs * GRID_W, GRID_W), LW)
    s_c = _dot(q, k_ref[0:nctx, :], NT) * SCALE
    s_w = _dot(q, k_ref[win, :], NT) * SCALE + bias_ref[r - rs]
    return s_c, s_w, win


def _na_specs(T, nctx, rb):
    LW = NA_KH * GRID_W
    qoff = nctx // (rb * GRID_W)
    assert nctx % (rb * GRID_W) == 0
    q_spec = pl.BlockSpec((rb * GRID_W, HEAD), lambda h, g: (g + qoff, h))
    k_spec = pl.BlockSpec((T, HEAD), lambda h, g: (0, 16 + h))
    v_spec = pl.BlockSpec((T, HEAD), lambda h, g: (0, 32 + h))
    b_spec = pl.BlockSpec((None, NA_KH, GRID_W, LW), lambda h, g: (h, 0, 0, 0))
    row_spec = pl.BlockSpec((rb * GRID_W, HEAD), lambda h, g: (g, h))
    return q_spec, k_spec, v_spec, b_spec, row_spec


def attn_odd_fwd(name, qkv, bias8, nctx, carry=None):
    T = qkv.shape[0]
    S = T - nctx
    rows = S // GRID_W
    rb = min(NA_ROWS_PER_STEP, nctx // GRID_W)
    q_spec, k_spec, v_spec, b_spec, row_spec = _na_specs(T, nctx, rb)

    def body(q_ref, k_ref, v_ref, b_ref, o_ref):
        g = pl.program_id(1)
        for j in range(rb):
            qr = slice(j * GRID_W, (j + 1) * GRID_W)
            s_c, s_w, win = _na_scores(q_ref[qr, :], k_ref, b_ref, g * rb + j, rows, nctx)
            (p_c, p_w), _ = _softmax_parts([s_c, s_w])
            o_ref[qr, :] = (_dot(p_c, v_ref[0:nctx, :], NN) + _dot(p_w, v_ref[win, :], NN)).astype(BF16)

    res = _call(name, body, (16, rows // rb), (qkv, qkv, qkv, bias8), [q_spec, k_spec, v_spec, b_spec],
                [jax.ShapeDtypeStruct((S, 16 * HEAD), BF16)], [row_spec], [], ("parallel", "arbitrary"), carry)
    return _split(res, 1, carry)


def attn_odd_bwd(name, qkv, bias8, do, nctx, carry=None):
    T = qkv.shape[0]
    S = T - nctx
    rows = S // GRID_W
    rb = min(NA_ROWS_PER_STEP, nctx // GRID_W)
    q_spec, k_spec, v_spec, b_spec, row_spec = _na_specs(T, nctx, rb)

    def body(q_ref, k_ref, v_ref, b_ref, do_ref, dq_ref, dk_ref, dv_ref, db_ref):
        g = pl.program_id(1)

        @pl.when(g == 0)
        def _():
            dk_ref[...] = jnp.zeros_like(dk_ref)
            dv_ref[...] = jnp.zeros_like(dv_ref)
            db_ref[...] = jnp.zeros_like(db_ref)

        for j in range(rb):
            qr = slice(j * GRID_W, (j + 1) * GRID_W)
            r = g * rb + j
            q = q_ref[qr, :]
            dov = do_ref[qr, :]
            s_c, s_w, win = _na_scores(q, k_ref, b_ref, r, rows, nctx)
            (p_c, p_w), _ = _softmax_parts([s_c, s_w])
            dp_c = _dot(dov, v_ref[0:nctx, :], NT)
            dp_w = _dot(dov, v_ref[win, :], NT)
            row = jnp.sum(p_c * dp_c, axis=-1, keepdims=True) + jnp.sum(p_w * dp_w, axis=-1, keepdims=True)
            dsw = p_w * (dp_w - row)
            db_ref[r - _na_row_start(r, rows)] += dsw
            ds_c = (p_c * (dp_c - row) * SCALE).astype(BF16)
            ds_w = (dsw * SCALE).astype(BF16)
            dq_ref[qr, :] = _dot(ds_c, k_ref[0:nctx, :], NN) + _dot(ds_w, k_ref[win, :], NN)
            dk_ref[0:nctx, :] += _dot(ds_c, q, TN)
            dk_ref[win, :] += _dot(ds_w, q, TN)
            dv_ref[0:nctx, :] += _dot(p_c, dov, TN)
            dv_ref[win, :] += _dot(p_w, dov, TN)

    kv_out = pl.BlockSpec((T, HEAD), lambda h, g: (0, h))
    res = _call(name, body, (16, rows // rb), (qkv, qkv, qkv, bias8, do), [q_spec, k_spec, v_spec, b_spec, row_spec],
                [jax.ShapeDtypeStruct((S, 16 * HEAD), F32), jax.ShapeDtypeStruct((T, 16 * HEAD), F32),
                 jax.ShapeDtypeStruct((T, 16 * HEAD), F32), jax.ShapeDtypeStruct(bias8.shape, F32)],
                [row_spec, kv_out, kv_out, b_spec], [], ("arbitrary", "arbitrary"), carry)
    return _split(res, 4, carry)


def _na_onehots():
    o = np.arange(NA_KH)[:, None]
    i = np.arange(NA_KH)[None, :]
    a = i - o + NA_KH - 1
    A = (a[..., None] == np.arange(2 * NA_KH - 1)).astype(np.float32)
    qc = np.arange(GRID_W)[:, None]
    kc = np.arange(GRID_W)[None, :]
    b = np.clip(kc - qc + NA_KW - 1, 0, 2 * NA_KW - 2)
    cs = np.clip(qc - NA_KW // 2, 0, GRID_W - NA_KW)
    valid = (kc >= cs) & (kc < cs + NA_KW)
    B = ((b[..., None] == np.arange(2 * NA_KW - 1)) & valid[..., None]).astype(np.float32)
    return A, B, valid


def na_bias_table(rpb):
    A, B, valid = _na_onehots()
    hp = lax.Precision.HIGHEST
    t = jnp.einsum('hab,oia->hoib', rpb, jnp.asarray(A), precision=hp)
    bias = jnp.einsum('hoib,qkb->hoqik', t, jnp.asarray(B), precision=hp)
    bias = jnp.where(jnp.asarray(valid)[None, None, :, None, :], bias, NEG)
    return bias.reshape(rpb.shape[0], NA_KH, GRID_W, NA_KH * GRID_W)


def na_bias_grad(name, dbias8):
    A, B, _ = _na_onehots()
    H = dbias8.shape[0]
    nb, na = 2 * NA_KW - 1, 2 * NA_KH - 1
    d = dbias8.reshape(H, NA_KH, GRID_W, NA_KH, GRID_W).transpose(0, 1, 3, 2, 4)
    d = d.reshape(H * NA_KH * NA_KH, GRID_W * GRID_W)
    Bp = np.zeros((GRID_W * GRID_W, 128), np.float32)
    Bp[:, :nb] = B.reshape(GRID_W * GRID_W, nb)
    Ap = np.zeros((16, NA_KH * NA_KH), np.float32)
    Ap[:na] = A.reshape(NA_KH * NA_KH, na).T
    rows_per_head = NA_KH * NA_KH

    def split3(x):
        hi = x.astype(BF16)
        r1 = x - hi.astype(F32)
        mid = r1.astype(BF16)
        return hi, mid, (r1 - mid.astype(F32)).astype(BF16)

    def body(d_ref, b_ref, a_ref, o_ref):
        bm, am = b_ref[...], a_ref[...]
        g = sum(lax.dot_general(p, bm, NN, preferred_element_type=F32) for p in split3(d_ref[...]))
        o_ref[...] = sum(lax.dot_general(am, p, NN, preferred_element_type=F32) for p in split3(g))

    out = pl.pallas_call(
        body, grid=(H,),
        in_specs=[pl.BlockSpec((rows_per_head, GRID_W * GRID_W), lambda h: (h, 0)),
                  pl.BlockSpec((GRID_W * GRID_W, 128), lambda h: (0, 0)),
                  pl.BlockSpec((16, rows_per_head), lambda h: (0, 0))],
        out_specs=pl.BlockSpec((None, 16, 128), lambda h: (h, 0, 0)),
        out_shape=jax.ShapeDtypeStruct((H, 16, 128), F32),
        compiler_params=_cparams(("parallel",)), name=name)(d, jnp.asarray(Bp, BF16), jnp.asarray(Ap, BF16))
    return out[:, :na, :nb]


def _vmem_call(name, fn, out_shape, *arrays):
    def body(*refs):
        n = len(arrays)
        res = fn(*[r[...] for r in refs[:n]])
        if not isinstance(res, (tuple, list)):
            res = (res,)
        for o, v in zip(refs[n:], res):
            o[...] = v
    return pl.pallas_call(body, out_shape=out_shape, name=name,
                          compiler_params=pltpu.CompilerParams(vmem_limit_bytes=VMEM_LIMIT))(*arrays)


def _silu(v):
    return v / (1.0 + jnp.exp(-v))


def _adamw_math(w, g, m, v):
    m2 = ADAM_B1 * m + (1.0 - ADAM_B1) * g
    v2 = ADAM_B2 * v + (1.0 - ADAM_B2) * (g * g)
    m_hat = m2 / (1.0 - ADAM_B1 ** ADAM_STEP)
    v_hat = v2 / (1.0 - ADAM_B2 ** ADAM_STEP)
    delta = -ADAM_LR * (m_hat / (jnp.sqrt(v_hat) + ADAM_EPS) + ADAM_WD * w)
    return delta, m2, v2


def _ew_tile(R, C):
    return _tile(R, max(64, (262144 // C) // 64 * 64))


def adamw_rows(name, w, g, m, v, extra_g=None):
    R, C = w.shape
    tr = _ew_tile(R, C)
    extra_g = list(extra_g or [])
    ne = len(extra_g)

    def body(*refs):
        w_ref, g_ref, m_ref, v_ref = refs[:4]
        gs = g_ref[...]
        for e in refs[4:4 + ne]:
            gs = gs + e[...].astype(F32)
        go, do, mo, vo = refs[4 + ne:]
        d, m2, v2 = _adamw_math(w_ref[...], gs, m_ref[...], v_ref[...])
        go[...] = gs
        do[...] = d
        mo[...] = m2
        vo[...] = v2

    spec = pl.BlockSpec((tr, C), lambda i: (i, 0))
    return pl.pallas_call(
        body, grid=(R // tr,), in_specs=[spec] * (4 + ne), out_specs=[spec] * 4,
        out_shape=[jax.ShapeDtypeStruct((R, C), F32)] * 4,
        compiler_params=_cparams(("parallel",)), name=name)(w, g, m, v, *extra_g)


def rs_chip_sum(name, g8, sib4, half):
    _, R, C = g8.shape
    tr = _ew_tile(R, C)

    def body(s_ref, g_ref, b_ref, o_ref):
        o_ref[...] = (g_ref[...] + b_ref[...]).astype(BF16)

    blk = (None, tr, C)
    grid_spec = pltpu.PrefetchScalarGridSpec(
        num_scalar_prefetch=1, grid=(4, R // tr),
        in_specs=[pl.BlockSpec(blk, lambda q, i, s: (s[0] + q, i, 0)), pl.BlockSpec(blk, lambda q, i, s: (q, i, 0))],
        out_specs=pl.BlockSpec(blk, lambda q, i, s: (q, i, 0)))
    return pl.pallas_call(body, grid_spec=grid_spec, out_shape=jax.ShapeDtypeStruct((4, R, C), BF16),
                          compiler_params=_cparams(("parallel", "parallel")), name=name)(half, g8, sib4)


def adamw_rs(name, w, g8, sib4, rem3, m, v, idx):
    R, C = w.shape
    tr = _ew_tile(R, C)

    def body(s_ref, w_ref, g_ref, sb_ref, r0_ref, r1_ref, r2_ref, m_ref, v_ref, go, do, mo, vo):
        gs = g_ref[...] + sb_ref[...]
        for r_ref in (r0_ref, r1_ref, r2_ref):
            gs = gs + r_ref[...].astype(F32)
        d, m2, v2 = _adamw_math(w_ref[...], gs, m_ref[...], v_ref[...])
        go[...] = gs
        do[...] = d
        mo[...] = m2
        vo[...] = v2

    flat = pl.BlockSpec((tr, C), lambda i, s: (i, 0))
    blk = (None, tr, C)

    def rem(k):
        return pl.BlockSpec(blk, lambda i, s: (k, i, 0))

    grid_spec = pltpu.PrefetchScalarGridSpec(
        num_scalar_prefetch=1, grid=(R // tr,),
        in_specs=[flat, pl.BlockSpec(blk, lambda i, s: (s[0], i, 0)), pl.BlockSpec(blk, lambda i, s: (s[1], i, 0)),
                  rem(0), rem(1), rem(2), flat, flat],
        out_specs=[flat] * 4)
    return pl.pallas_call(body, grid_spec=grid_spec, out_shape=[jax.ShapeDtypeStruct((R, C), F32)] * 4,
                          compiler_params=_cparams(("parallel",)), name=name)(idx, w, g8, sib4, rem3, rem3, rem3, m, v)


def _me():
    x, y, c = lax.axis_index("x"), lax.axis_index("y"), lax.axis_index("c")
    return x, y, c


def _flip(v, bit):
    return 1 - v if bit else v


def ag_small(name, x, with_sum=False):
    R, C = x.shape

    def body(x_ref, out_ref, *rest):
        if with_sum:
            sum_ref, send_sems, recv_sems, lsem = rest
        else:
            send_sems, recv_sems, lsem = rest
        mx, my, mc = _me()
        me = 4 * mx + 2 * my + mc
        local = pltpu.make_async_copy(x_ref, out_ref.at[me], lsem)
        local.start()
        sends = []
        for k in range(1, NDEV):
            peer = (_flip(mx, k & 4), _flip(my, k & 2), _flip(mc, k & 1))
            cp = pltpu.make_async_remote_copy(src_ref=x_ref, dst_ref=out_ref.at[me], send_sem=send_sems.at[k - 1],
                                              recv_sem=recv_sems.at[k - 1], device_id=peer, device_id_type=MESH)
            cp.start()
            sends.append(cp)
        for k in range(1, NDEV):
            px, py, pc = _flip(mx, k & 4), _flip(my, k & 2), _flip(mc, k & 1)
            pltpu.make_async_remote_copy(src_ref=x_ref, dst_ref=out_ref.at[4 * px + 2 * py + pc],
                                         send_sem=send_sems.at[k - 1], recv_sem=recv_sems.at[k - 1],
                                         device_id=(px, py, pc), device_id_type=MESH).wait_recv()
        for cp in sends:
            cp.wait_send()
        local.wait()
        if with_sum:
            acc = out_ref[0]
            for d in range(1, NDEV):
                acc = acc + out_ref[d]
            sum_ref[...] = acc

    out_shape = [jax.ShapeDtypeStruct((NDEV, R, C), F32)]
    if with_sum:
        out_shape.append(jax.ShapeDtypeStruct((R, C), F32))
    vm = pl.BlockSpec(memory_space=pltpu.VMEM)
    res = pl.pallas_call(
        body, out_shape=out_shape, in_specs=[vm], out_specs=[vm] * len(out_shape),
        scratch_shapes=[pltpu.SemaphoreType.DMA((NDEV - 1,)), pltpu.SemaphoreType.DMA((NDEV - 1,)),
                        pltpu.SemaphoreType.DMA],
        compiler_params=pltpu.CompilerParams(vmem_limit_bytes=VMEM_LIMIT), name=name)(x)
    return res if with_sum else res[0]


def ag_big(name, shards):
    n = len(shards)

    def body(*refs):
        ins, outs = refs[:n], refs[n:2 * n]
        send_sems, recv_sems, lsems = refs[2 * n:]
        mx, my, mc = _me()
        me = (mx, my, mc)
        sibling = (mx, my, 1 - mc)
        chips = [(1 - mx, my), (mx, 1 - my), (1 - mx, 1 - my)]

        def idx(p):
            return 4 * p[0] + 2 * p[1] + p[2]

        def copy(t, k, block, to, src=None):
            dst = outs[t].at[idx(block)]
            return pltpu.make_async_remote_copy(
                src_ref=dst if src is None else src, dst_ref=dst, send_sem=send_sems.at[7 * t + k],
                recv_sem=recv_sems.at[7 * t + k], device_id=to, device_id_type=MESH)

        started = []
        locals_ = []
        for t in range(n):
            mine = pltpu.make_async_copy(ins[t], outs[t].at[idx(me)], lsems.at[t])
            mine.start()
            locals_.append(mine)
            first = [copy(t, 0, me, sibling, src=ins[t])]
            first += [copy(t, 1 + j, me, (*chip, mc), src=ins[t]) for j, chip in enumerate(chips)]
            for cp in first:
                cp.start()
            started += first
        for t in range(n):
            for j, chip in enumerate(chips):
                copy(t, 1 + j, (*chip, mc), me).wait_recv()
                fwd = copy(t, 4 + j, (*chip, mc), sibling)
                fwd.start()
                started.append(fwd)
        for t in range(n):
            copy(t, 0, sibling, me).wait_recv()
            for j, chip in enumerate(chips):
                copy(t, 4 + j, (*chip, 1 - mc), me).wait_recv()
        for cp in started:
            cp.wait_send()
        for mine in locals_:
            mine.wait()

    anyspec = pl.BlockSpec(memory_space=pl.ANY)
    return pl.pallas_call(
        body, out_shape=[jax.ShapeDtypeStruct((NDEV,) + s.shape, s.dtype) for s in shards],
        in_specs=[anyspec] * n, out_specs=[anyspec] * n,
        scratch_shapes=[pltpu.SemaphoreType.DMA((7 * n,)), pltpu.SemaphoreType.DMA((7 * n,)),
                        pltpu.SemaphoreType.DMA((n,))],
        name=name)(*shards)


def _idx(p):
    return 4 * p[0] + 2 * p[1] + p[2]


def _remote(src, dst, ss, rs, k, to):
    return pltpu.make_async_remote_copy(src_ref=src, dst_ref=dst, send_sem=ss.at[k], recv_sem=rs.at[k],
                                        device_id=to, device_id_type=MESH)


def ex_ag_chips(shards):
    n = len(shards)

    def copies(ci, co, ss, rs, base):
        mx, my, mc = _me()
        me = (mx, my, mc)
        peers = [(mx, my, 1 - mc), (1 - mx, my, mc), (mx, 1 - my, mc), (1 - mx, 1 - my, mc)]
        sends, recvs, local = [], [], []
        for t in range(n):
            b = base + 5 * t
            for k, peer in enumerate(peers):
                sends.append(_remote(ci[t], co[t].at[_idx(me)], ss, rs, b + k, peer))
                recvs.append(_remote(ci[t], co[t].at[_idx(peer)], ss, rs, b + k, peer))
            local.append(pltpu.make_async_copy(ci[t], co[t].at[_idx(me)], ss.at[b + 4]))
        return sends, recvs, local

    def start(ci, co, ss, rs, base):
        sends, _, local = copies(ci, co, ss, rs, base)
        for cp in local + sends:
            cp.start()

    def finish(ci, co, ss, rs, base):
        sends, recvs, local = copies(ci, co, ss, rs, base)
        for cp in recvs:
            cp.wait_recv()
        for cp in sends:
            cp.wait_send()
        for cp in local:
            cp.wait()

    outs = [jax.ShapeDtypeStruct((NDEV,) + s.shape, s.dtype) for s in shards]
    return Exchange(shards, outs, {}, 5 * n, start, finish)


def ex_ag_sibling(bufs):
    n = len(bufs)

    def copies(co, ss, rs, base):
        mx, my, mc = _me()
        sibling = (mx, my, 1 - mc)
        chips = [(1 - mx, my), (mx, 1 - my), (1 - mx, 1 - my)]
        sends, recvs = [], []
        for t in range(n):
            for j, chip in enumerate(chips):
                mine, theirs = co[t].at[_idx((*chip, mc))], co[t].at[_idx((*chip, 1 - mc))]
                sends.append(_remote(mine, mine, ss, rs, base + 3 * t + j, sibling))
                recvs.append(_remote(mine, theirs, ss, rs, base + 3 * t + j, sibling))
        return sends, recvs

    def start(ci, co, ss, rs, base):
        for cp in copies(co, ss, rs, base)[0]:
            cp.start()

    def finish(ci, co, ss, rs, base):
        sends, recvs = copies(co, ss, rs, base)
        for cp in recvs:
            cp.wait_recv()
        for cp in sends:
            cp.wait_send()

    outs = [jax.ShapeDtypeStruct(b.shape, b.dtype) for b in bufs]
    return Exchange(bufs, outs, {t: t for t in range(n)}, 3 * n, start, finish)


def ex_rs_sibling(grads):
    n = len(grads)

    def copies(ci, co, ss, rs, base):
        mx, my, mc = _me()
        return [_remote(ci[t].at[pl.ds((1 - mc) * 4, 4)], co[t], ss, rs, base + t, (mx, my, 1 - mc)) for t in range(n)]

    def start(ci, co, ss, rs, base):
        for cp in copies(ci, co, ss, rs, base):
            cp.start()

    def finish(ci, co, ss, rs, base):
        for cp in copies(ci, co, ss, rs, base):
            cp.wait()

    outs = [jax.ShapeDtypeStruct((4,) + g.shape[1:], g.dtype) for g in grads]
    return Exchange(grads, outs, {}, n, start, finish)


def ex_rs_chips(parts):
    n = len(parts)

    def copies(ci, co, ss, rs, base):
        mx, my, mc = _me()
        cps = []
        for t in range(n):
            for k in range(1, 4):
                px, py = _flip(mx, k & 2), _flip(my, k & 1)
                cps.append(_remote(ci[t].at[2 * px + py], co[t].at[k - 1], ss, rs, base + 3 * t + k - 1, (px, py, mc)))
        return cps

    def start(ci, co, ss, rs, base):
        for cp in copies(ci, co, ss, rs, base):
            cp.start()

    def finish(ci, co, ss, rs, base):
        for cp in copies(ci, co, ss, rs, base):
            cp.wait()

    outs = [jax.ShapeDtypeStruct((3,) + p.shape[1:], p.dtype) for p in parts]
    return Exchange(parts, outs, {}, 3 * n, start, finish)


def run_exchanges(name, xs):
    x = merge_exchanges(xs)
    n_ci, n_co = len(x.ins), len(x.out_shapes)

    def body(*refs):
        ci, co = refs[:n_ci], refs[n_ci:n_ci + n_co]
        ss, rs = refs[n_ci + n_co:]
        x.start(ci, co, ss, rs, 0)
        x.finish(ci, co, ss, rs, 0)

    hbm = pl.BlockSpec(memory_space=pl.ANY)
    return pl.pallas_call(
        body, out_shape=x.out_shapes, in_specs=[hbm] * n_ci, out_specs=[hbm] * n_co, input_output_aliases=x.aliases,
        scratch_shapes=[pltpu.SemaphoreType.DMA((x.n_sems,)), pltpu.SemaphoreType.DMA((x.n_sems,))], name=name)(*x.ins)


def _rope_tables(S, nctx):
    t = jnp.arange(S)
    row = (t // GRID_W).astype(F32)
    col = (t % GRID_W).astype(F32)
    pairs = HEAD // 4
    inv = ROPE_THETA ** (-jnp.arange(pairs, dtype=F32) / pairs)
    ang_r = row[:, None] * inv
    ang_c = col[:, None] * inv
    ang = jnp.concatenate([ang_r, ang_r, ang_c, ang_c], axis=-1)
    cos = jnp.concatenate([jnp.ones((nctx, HEAD), F32), jnp.cos(ang)], axis=0)
    sin = jnp.concatenate([jnp.zeros((nctx, HEAD), F32), jnp.sin(ang)], axis=0)
    lane = jnp.arange(HEAD)[None, :]
    first = (lane & 32) == 0
    return cos, jnp.where(first, -sin, 0.0), jnp.where(first, 0.0, sin)


def _pad_rows(v, rows):
    v = v.reshape(-1).astype(F32)
    return jnp.pad(v, (0, rows * 128 - v.shape[0])).reshape(rows, 128)


def _rows8(n):
    return -(-n // 1024) * 8


def kernel(x, c, ctx, c_ctx, ada_w, ada_b, norm_w, mlp_w1, mlp_w2, ev_w_in, ev_w_out, ev_q_norm, ev_k_norm, ev_sink, od_w_in, od_w_out, od_rpb, final_norm_w, loss_target, m_c_ctx, m_ada_w, m_ada_b, m_norm_w, m_mlp_w1, m_mlp_w2, m_ev_w_in, m_ev_w_out, m_ev_q_norm, m_ev_k_norm, m_ev_sink, m_od_w_in, m_od_w_out, m_od_rpb, m_final_norm_w, v_c_ctx, v_ada_w, v_ada_b, v_norm_w, v_mlp_w1, v_mlp_w2, v_ev_w_in, v_ev_w_out, v_ev_q_norm, v_ev_k_norm, v_ev_sink, v_od_w_in, v_od_w_out, v_od_rpb, v_final_norm_w):
    S, D = x.shape[1], x.shape[2]
    NC = ctx.shape[1]
    T = NC + S
    assert NC == ROW_TILE and S % GRID_W == 0
    ada_cols = ada_w.shape[2]
    nw_cols = norm_w.shape[2]
    me = 4 * lax.axis_index("x") + 2 * lax.axis_index("y") + lax.axis_index("c")

    pack1 = jnp.concatenate([_pad_rows(c, _rows8(D)), _pad_rows(norm_w, _rows8(4 * nw_cols))], axis=0)
    g1 = ag_small("ag_c_normw", pack1)
    c_all = g1[:, :D // 128].reshape(NDEV, D)
    nw_rows = _rows8(D)
    nw = g1[:, nw_rows:nw_rows + 4 * nw_cols // 128].reshape(NDEV, 2, 2, nw_cols)
    nw = nw.transpose(1, 2, 0, 3).reshape(2, 2, D)
    cin = jnp.concatenate([c_all, jnp.broadcast_to(c_ctx[None], (NDEV, D))], axis=0)
    act = _vmem_call("silu_c", lambda v: _silu(v).astype(BF16), jax.ShapeDtypeStruct((2 * NDEV, D), BF16), cin)
    ada_b_loc = lax.dynamic_slice_in_dim(ada_b, me * ada_cols, ada_cols, axis=1)
    mods = [mm_nn(f"mod{i}", act, ada_w[i], _epi_bias, [F32], extras=(ada_b_loc[i:i + 1],), extra_kinds=('n',))[0]
            for i in range(2)]
    gm = ag_small("ag_mod", jnp.concatenate(mods, axis=1))
    gm = gm.reshape(NDEV, 2 * NDEV, 2, ada_cols).transpose(2, 1, 0, 3).reshape(2, 2 * NDEV, NDEV * ada_cols)
    mod_lat = lax.dynamic_index_in_dim(gm, me, axis=1, keepdims=False)
    mod_ctx = gm[:, NDEV]
    mod2 = jnp.stack([mod_ctx, mod_lat], axis=1).reshape(2, 2, 6, D)

    def chunk(i, j):
        return mod2[i, :, j, :]

    def b16(w):
        return w.astype(BF16)

    w_in_e, w_out_e = ag_big("ag_weights_l0_attn", [b16(ev_w_in[0]), b16(ev_w_out[0])])
    w_out_e = w_out_e.reshape(-1, D)

    cos, sa, sb = _rope_tables(S, NC)
    bias8 = na_bias_table(od_rpb[0])
    sink = ev_sink[0]
    TQ_F, TQ_B = 256, 128

    X0 = jnp.concatenate([ctx[0], x[0]], axis=0)
    h_a = norm_mod("l0_norm1", X0, nw[0, 0][None], chunk(0, 0), chunk(0, 1), NC)
    qkv0 = mm_nn("l0_qkv", h_a, w_in_e, _epi_store(F32), [F32])[0]
    qkvh0 = prep_even("l0_prep", qkv0, ev_q_norm, ev_k_norm, cos, sa, sb)
    o0, mlp0_half = attn_even_fwd("l0_attn", qkvh0, sink, NC, TQ_F,
                                  carry=[ex_ag_chips([b16(mlp_w1[0]), b16(mlp_w2[0])])])
    tm0 = _tile(T, 1100)
    (X1, y0), (w1_0, w2_0) = mm_nn("l0_out", o0, w_out_e, _epi_resid_gate(NC, tm0), [F32, F32],
                                   extras=(X0, chunk(0, 2)), extra_kinds=('mn', 'n'),
                                   carry=[ex_ag_sibling(mlp0_half)])
    h_b = norm_mod("l0_norm2", X1, nw[0, 1][None], chunk(0, 3), chunk(0, 4), NC)
    (a0, r0), od_half = mm_nn("l0_up", h_b, w1_0, _epi_relu2, [BF16, BF16],
                              carry=[ex_ag_chips([b16(od_w_in[0]), b16(od_w_out[0])])])
    (X2, z0), (w1_1_half, w_in_o, w_out_o) = mm_nn(
        "l0_down", a0, w2_0.reshape(-1, D), _epi_resid_gate(NC, tm0), [F32, F32], extras=(X1, chunk(0, 5)),
        extra_kinds=('mn', 'n'), carry=[ex_ag_chips([b16(mlp_w1[1])]), ex_ag_sibling(od_half)])
    w_out_o = w_out_o.reshape(-1, D)

    h_c = norm_mod("l1_norm1", X2, nw[1, 0][None], chunk(1, 0), chunk(1, 1), NC)
    (qkv1,), (w2_1_half, w1_1) = mm_nn("l1_qkv", h_c, w_in_o, _epi_store(BF16), [BF16],
                                        carry=[ex_ag_chips([b16(mlp_w2[1])]), ex_ag_sibling([w1_1_half])])
    o1, (w2_1,) = attn_odd_fwd("l1_attn", qkv1, bias8, NC, carry=[ex_ag_sibling([w2_1_half])])
    X2l = X2[NC:]
    tm1 = _tile(S, 1100)
    X3, y1 = mm_nn("l1_out", o1, w_out_o, _epi_resid_gate(0, tm1), [F32, F32], extras=(X2l, chunk(1, 2)),
                   extra_kinds=('mn', 'n'))
    h_d = norm_mod("l1_norm2", X3, nw[1, 1][None], chunk(1, 3), chunk(1, 4), 0)
    a1, r1 = mm_nn("l1_up", h_d, w1_1, _epi_relu2, [BF16, BF16])
    X4, z1 = mm_nn("l1_down", a1, w2_1.reshape(-1, D), _epi_resid_gate(0, tm1), [F32, F32], extras=(X3, chunk(1, 5)),
                   extra_kinds=('mn', 'n'))
    dX4, loss_p, dfw_p = final_loss("final_loss", X4, final_norm_w[None], loss_target[0])
    w_in = [w_in_e, w_in_o]
    w_out = [w_out_e, w_out_o]
    w1 = [w1_0, w1_1]
    w2 = [w2_0.reshape(-1, D), w2_1.reshape(-1, D)]

    mc4 = (lax.axis_index("c") * 4).astype(jnp.int32)
    my_chip = (2 * lax.axis_index("x") + lax.axis_index("y")).astype(jnp.int32)

    def chip_sum(tag, g8, sib4):
        return rs_chip_sum(f"rs_chip_sum_{tag}", g8, sib4, mc4[None])

    dz1, pg2_1 = gate_bwd("l1_gate2_bwd", dX4, z1, chunk(1, 5), 0)
    du1 = mm_nt("l1_down_dx", dz1, w2[1], _epi_mul2r, BF16, extras=(r1,))
    g_w2_1 = mm_tn("l1_down_dw", a1, dz1, 0)
    dh_d, (sib_w2_1,) = mm_nt("l1_up_dx", du1, w1[1], _epi_store(F32), F32, carry=[ex_rs_sibling([g_w2_1])])
    g_w1_1, (rem_w2_1,) = mm_tn("l1_up_dw", h_d, du1, 1,
                                carry=[ex_rs_chips([chip_sum("w2_1", g_w2_1, sib_w2_1)])])
    dX3, pn2_1 = norm_bwd("l1_norm2_bwd", X3, dh_d, dX4, nw[1, 1][None], chunk(1, 4), 0)
    dy1, pg1_1 = gate_bwd("l1_gate1_bwd", dX3, y1, chunk(1, 2), 0)
    do1 = mm_nt("l1_out_dx", dy1, w_out[1], _epi_store(BF16), BF16)
    g_wout_1, (sib_w1_1,) = mm_tn("l1_out_dw", o1, dy1, 0, carry=[ex_rs_sibling([g_w1_1])])
    (dq1, dk1, dv1, dbias8), (rem_w1_1, sib_wout_1) = attn_odd_bwd(
        "l1_attn_bwd", qkv1, bias8, do1, NC,
        carry=[ex_rs_chips([chip_sum("w1_1", g_w1_1, sib_w1_1)]), ex_rs_sibling([g_wout_1])])
    dqkv1 = jnp.concatenate([jnp.pad(dq1, ((NC, 0), (0, 0))), dk1, dv1], axis=1).astype(BF16)
    dh_c, (rem_wout_1,) = mm_nt("l1_qkv_dx", dqkv1, w_in[1], _epi_store(F32), F32,
                                carry=[ex_rs_chips([chip_sum("wout_1", g_wout_1, sib_wout_1)])])
    g_win_1 = mm_tn("l1_qkv_dw", h_c, dqkv1, 1)
    dX2, pn1_1 = norm_bwd("l1_norm1_bwd", X2, dh_c, jnp.pad(dX3, ((NC, 0), (0, 0))), nw[1, 0][None], chunk(1, 1), NC)
    d_rpb = na_bias_grad("rpb_grad", dbias8)

    dz0, pg2_0 = gate_bwd("l0_gate2_bwd", dX2, z0, chunk(0, 5), NC)
    du0, (sib_win_1,) = mm_nt("l0_down_dx", dz0, w2[0], _epi_mul2r, BF16, extras=(r0,),
                              carry=[ex_rs_sibling([g_win_1])])
    g_w2_0, (rem_win_1,) = mm_tn("l0_down_dw", a0, dz0, 0,
                                 carry=[ex_rs_chips([chip_sum("win_1", g_win_1, sib_win_1)])])
    dh_b, (sib_w2_0,) = mm_nt("l0_up_dx", du0, w1[0], _epi_store(F32), F32, carry=[ex_rs_sibling([g_w2_0])])
    g_w1_0, (rem_w2_0,) = mm_tn("l0_up_dw", h_b, du0, 1,
                                carry=[ex_rs_chips([chip_sum("w2_0", g_w2_0, sib_w2_0)])])
    dX1, pn2_0 = norm_bwd("l0_norm2_bwd", X1, dh_b, dX2, nw[0, 1][None], chunk(0, 4), NC)
    dy0, pg1_0 = gate_bwd("l0_gate1_bwd", dX1, y0, chunk(0, 2), NC)
    do0 = mm_nt("l0_out_dx", dy0, w_out[0], _epi_store(BF16), BF16)
    g_wout_0, (sib_w1_0,) = mm_tn("l0_out_dw", o0, dy0, 0, carry=[ex_rs_sibling([g_w1_0])])
    (dq0, dk0, dv0, dsink_p), (rem_w1_0, sib_wout_0) = attn_even_bwd(
        "l0_attn_bwd", qkvh0, sink, do0, NC, TQ_B,
        carry=[ex_rs_chips([chip_sum("w1_0", g_w1_0, sib_w1_0)]), ex_rs_sibling([g_wout_0])])
    dqkv0, pqk = prep_even_bwd("l0_prep_bwd", qkv0, dq0, dk0, dv0, ev_q_norm, ev_k_norm, cos, sa, sb)
    dh_a, (rem_wout_0,) = mm_nt("l0_qkv_dx", dqkv0, w_in[0], _epi_store(F32), F32,
                                carry=[ex_rs_chips([chip_sum("wout_0", g_wout_0, sib_wout_0)])])
    g_win_0 = mm_tn("l0_qkv_dw", h_a, dqkv0, 1)
    dX0, pn1_0 = norm_bwd("l0_norm1_bwd", X0, dh_a, dX1, nw[0, 0][None], chunk(0, 1), NC)
    grad_x = dX0[NC:][None]
    (sib_win_0,) = run_exchanges("rs_sibling_last", [ex_rs_sibling([g_win_0])])
    (rem_win_0,) = run_exchanges("rs_chips_last", [ex_rs_chips([chip_sum("win_0", g_win_0, sib_win_0)])])

    def dmod(grp, pn1, pg1, pn2, pg2):
        return jnp.concatenate([pn1[grp], pn1[2 + grp], pg1[grp], pn2[grp], pn2[2 + grp], pg2[grp]])

    dmod_lat = jnp.stack([dmod(1, pn1_0, pg1_0, pn2_0, pg2_0), dmod(1, pn1_1, pg1_1, pn2_1, pg2_1)])
    dmod_ctx = jnp.stack([dmod(0, pn1_0, pg1_0, pn2_0, pg2_0), dmod(0, pn1_1, pg1_1, pn2_1, pg2_1)])
    dnw_p = jnp.stack([pn1_0[4], pn2_0[4], pn1_1[4], pn2_1[4]])
    pieces = [dmod_lat, dmod_ctx, dnw_p, pqk[0], pqk[1], dsink_p[8:, 0, 0], d_rpb, dfw_p[0], loss_p[0, 0]]
    sizes = [int(np.prod(p.shape)) for p in pieces]
    rows = [_rows8(s) for s in sizes]
    pack2 = jnp.concatenate([_pad_rows(p, r) for p, r in zip(pieces, rows)], axis=0)
    g2, s2 = ag_small("ag_small_grads", pack2, with_sum=True)
    offs = np.concatenate([[0], np.cumsum(rows)])

    def piece(arr, i, shape):
        return arr[..., offs[i]:offs[i + 1], :].reshape(arr.shape[:-2] + (-1,))[..., :sizes[i]].reshape(
            arr.shape[:-2] + shape)

    dmod_all = piece(g2, 0, (2, 6 * D))
    dmodc_sum = piece(s2, 1, (2, 6 * D))
    dnw_sum = piece(s2, 2, (2, 2, D))
    g_qn = piece(s2, 3, ev_q_norm.shape)
    g_kn = piece(s2, 4, ev_k_norm.shape)
    g_sink = piece(s2, 5, ev_sink.shape)
    g_rpb = piece(s2, 6, od_rpb.shape)
    g_fw = piece(s2, 7, final_norm_w.shape)
    loss = piece(s2, 8, ())

    dm16 = jnp.concatenate([dmod_all.transpose(1, 0, 2), dmodc_sum[:, None, :],
                            jnp.zeros((2, NDEV - 1, 6 * D), F32)], axis=1)
    dm16_loc = lax.dynamic_slice_in_dim(dm16.reshape(2, 2 * NDEV, NDEV, ada_cols), me, 1, axis=2)[:, :, 0, :]
    g_ada_b = _vmem_call("ada_b_grad", lambda v: jnp.sum(v, axis=1),
                         jax.ShapeDtypeStruct((2, 6 * D), F32), dm16)
    g_ada_w = []
    dact_p = None
    for i in range(2):
        dmb = dm16_loc[i].astype(BF16)
        g_ada_w.append(mm_tn(f"ada_w_grad{i}", act, dmb, None))
        part = mm_nt(f"ada_dact{i}", dmb, ada_w[i], _epi_store(F32), F32)
        dact_p = part if dact_p is None else dact_p + part
    _, dact = ag_small("ag_cctx", dact_p, with_sum=True)

    def cctx_grad(da, cc):
        sg = 1.0 / (1.0 + jnp.exp(-cc))
        return da[NDEV:NDEV + 1] * (sg * (1.0 + cc * (1.0 - sg)))

    g_cctx = _vmem_call("cctx_grad", cctx_grad, jax.ShapeDtypeStruct((1, D), F32), dact, c_ctx[None])[0]

    grads = [g_win_0, g_wout_0, g_w1_0, g_w2_0, g_win_1, g_wout_1, g_w1_1, g_w2_1]
    sib = [sib_win_0, sib_wout_0, sib_w1_0, sib_w2_0, sib_win_1, sib_wout_1, sib_w1_1, sib_w2_1]
    rem = [rem_win_0, rem_wout_0, rem_w1_0, rem_w2_0, rem_win_1, rem_wout_1, rem_w1_1, rem_w2_1]
    own_idx = jnp.stack([mc4 + my_chip, my_chip])
    big = {}
    names = ['ev_w_in', 'ev_w_out', 'mlp_w1_0', 'mlp_w2_0', 'od_w_in', 'od_w_out', 'mlp_w1_1', 'mlp_w2_1']
    wts = [ev_w_in[0], ev_w_out[0], mlp_w1[0], mlp_w2[0], od_w_in[0], od_w_out[0], mlp_w1[1], mlp_w2[1]]
    ms = [m_ev_w_in[0], m_ev_w_out[0], m_mlp_w1[0], m_mlp_w2[0], m_od_w_in[0], m_od_w_out[0], m_mlp_w1[1], m_mlp_w2[1]]
    vs = [v_ev_w_in[0], v_ev_w_out[0], v_mlp_w1[0], v_mlp_w2[0], v_od_w_in[0], v_od_w_out[0], v_mlp_w1[1], v_mlp_w2[1]]
    for t in range(8):
        big[names[t]] = adamw_rs(f"adamw_{names[t]}", wts[t], grads[t], sib[t], rem[t], ms[t], vs[t], own_idx)

    def stack2(a, b):
        return tuple(jnp.stack([u, v_]) for u, v_ in zip(big[a], big[b]))

    def one(a):
        return tuple(u[None] for u in big[a])

    r_mlp_w1, r_mlp_w2 = stack2('mlp_w1_0', 'mlp_w1_1'), stack2('mlp_w2_0', 'mlp_w2_1')
    r_ev_w_in, r_ev_w_out, r_od_w_in, r_od_w_out = one('ev_w_in'), one('ev_w_out'), one('od_w_in'), one('od_w_out')

    g_ada = jnp.stack(g_ada_w)
    r_ada_w = adamw_rows("adamw_ada_w", ada_w.reshape(2 * D, ada_cols), g_ada.reshape(2 * D, ada_cols),
                         m_ada_w.reshape(2 * D, ada_cols), v_ada_w.reshape(2 * D, ada_cols))
    r_ada_w = tuple(u.reshape(2, D, ada_cols) for u in r_ada_w)

    g_nw_loc = lax.dynamic_slice_in_dim(dnw_sum, me * nw_cols, nw_cols, axis=2)
    small = [(c_ctx, g_cctx, m_c_ctx, v_c_ctx), (ada_b, g_ada_b, m_ada_b, v_ada_b),
             (norm_w, g_nw_loc, m_norm_w, v_norm_w), (ev_q_norm, g_qn, m_ev_q_norm, v_ev_q_norm),
             (ev_k_norm, g_kn, m_ev_k_norm, v_ev_k_norm), (ev_sink, g_sink, m_ev_sink, v_ev_sink),
             (od_rpb, g_rpb, m_od_rpb, v_od_rpb), (final_norm_w, g_fw, m_final_norm_w, v_final_norm_w)]
    srows = [_rows8(int(np.prod(w.shape))) for w, _, _, _ in small]
    packs = [jnp.concatenate([_pad_rows(tup[k], r) for tup, r in zip(small, srows)], axis=0) for k in range(4)]
    sres = adamw_rows("adamw_small", *packs)
    soffs = np.concatenate([[0], np.cumsum(srows)])

    def unpack(arr, i):
        w = small[i][0]
        return arr[soffs[i]:soffs[i + 1]].reshape(-1)[:int(np.prod(w.shape))].reshape(w.shape)

    sm = [[unpack(sres[k], i) for i in range(len(small))] for k in range(4)]

    def outs(k):
        big_k = {'ada_w': r_ada_w[k], 'mlp_w1': r_mlp_w1[k], 'mlp_w2': r_mlp_w2[k], 'ev_w_in': r_ev_w_in[k],
                 'ev_w_out': r_ev_w_out[k], 'od_w_in': r_od_w_in[k], 'od_w_out': r_od_w_out[k]}
        return (sm[k][0], big_k['ada_w'], sm[k][1], sm[k][2], big_k['mlp_w1'], big_k['mlp_w2'], big_k['ev_w_in'],
                big_k['ev_w_out'], sm[k][3], sm[k][4], sm[k][5], big_k['od_w_in'], big_k['od_w_out'], sm[k][6],
                sm[k][7])

    return (loss, grad_x, *outs(0), *outs(1), *outs(2), *outs(3))
```

```python
import numpy as np
import jax
import jax.numpy as jnp
from jax import lax
from jax.experimental import pallas as pl
from jax.experimental.pallas import tpu as pltpu

F32 = jnp.float32
BF16 = jnp.bfloat16
MESH = pl.DeviceIdType.MESH

NDEV = 8
HEAD = 128
GRID_W = 64
NA_KH, NA_KW = 8, 16
WINDOW = 128
ROPE_THETA = 10000.0
EPS = 1e-6
NEG = -1e30
SCALE = HEAD ** -0.5
ROW_TILE = 256
VMEM_LIMIT = 56 * 1024 * 1024

ADAM_LR, ADAM_B1, ADAM_B2, ADAM_EPS, ADAM_WD, ADAM_STEP = 0.001, 0.9, 0.999, 1e-08, 0.01, 10

NT = (((1,), (1,)), ((), ()))
NN = (((1,), (0,)), ((), ()))
TN = (((0,), (0,)), ((), ()))


def _cparams(sem):
    return pltpu.CompilerParams(dimension_semantics=sem, vmem_limit_bytes=VMEM_LIMIT)


def _tile(n, cap):
    if n <= cap:
        return n
    t = cap - cap % 64
    while t >= 64:
        if n % t == 0:
            return t
        t -= 64
    raise ValueError((n, cap))


def _dot(a, b, dims):
    return lax.dot_general(a.astype(BF16), b.astype(BF16), dims, preferred_element_type=F32)


def _slot(d):
    return (d % 2) * 4 + d // 2


class Exchange:
    def __init__(self, ins, out_shapes, aliases, n_sems, start, finish):
        self.ins, self.out_shapes, self.aliases, self.n_sems = list(ins), list(out_shapes), dict(aliases), n_sems
        self.start, self.finish = start, finish


def merge_exchanges(xs):
    ins, outs, aliases, bases, n = [], [], {}, [], 0
    for x in xs:
        bases.append((len(ins), len(outs), n))
        aliases.update({len(ins) + i: len(outs) + o for i, o in x.aliases.items()})
        ins += x.ins
        outs += x.out_shapes
        n += x.n_sems

    def run(which):
        def f(ci, co, ss, rs, base):
            for x, (i0, o0, s0) in zip(xs, bases):
                getattr(x, which)(ci[i0:i0 + len(x.ins)], co[o0:o0 + len(x.out_shapes)], ss, rs, base + s0)
        return f

    return Exchange(ins, outs, aliases, n, run('start'), run('finish'))


def _call(name, body, grid, ins, in_specs, out_shape, out_specs, scratch, sems, carry=None):
    if not carry:
        return pl.pallas_call(body, grid=grid, in_specs=in_specs, out_specs=out_specs, out_shape=out_shape,
                              scratch_shapes=scratch, compiler_params=_cparams(sems), name=name)(*ins)
    x = merge_exchanges(carry)
    n_in, n_ci, n_out, n_co, n_sc = len(ins), len(x.ins), len(out_shape), len(x.out_shapes), len(scratch)

    def wrapped(*refs):
        p = [0]

        def take(k):
            p[0] += k
            return refs[p[0] - k:p[0]]

        a, ci, o, co, sc = take(n_in), take(n_ci), take(n_out), take(n_co), take(n_sc)
        ss, rs = take(2)
        first = pl.program_id(0) == 0
        last = pl.program_id(0) == grid[0] - 1
        for d in range(1, len(grid)):
            first = jnp.logical_and(first, pl.program_id(d) == 0)
            last = jnp.logical_and(last, pl.program_id(d) == grid[d] - 1)

        @pl.when(first)
        def _():
            x.start(ci, co, ss, rs, 0)

        body(*a, *o, *sc)

        @pl.when(last)
        def _():
            x.finish(ci, co, ss, rs, 0)

    hbm = pl.BlockSpec(memory_space=pl.ANY)
    res = pl.pallas_call(
        wrapped, grid=grid, in_specs=list(in_specs) + [hbm] * n_ci, out_specs=list(out_specs) + [hbm] * n_co,
        out_shape=list(out_shape) + x.out_shapes,
        input_output_aliases={n_in + i: n_out + o for i, o in x.aliases.items()},
        scratch_shapes=list(scratch) + [pltpu.SemaphoreType.DMA((x.n_sems,)), pltpu.SemaphoreType.DMA((x.n_sems,))],
        compiler_params=_cparams(("arbitrary",) * len(grid)), name=name)(*ins, *x.ins)
    return list(res[:n_out]) + [list(res[n_out:])]


def _mm_core(name, grid, ins, in_specs, out_shape, out_specs, dims, acc_shape, epi, carry=None):
    nk = grid[2]
    n_extra = len(ins) - 2

    def body(*refs):
        a_ref, b_ref = refs[0], refs[1]
        ex = refs[2:2 + n_extra]
        outs = refs[2 + n_extra:-1]
        acc = refs[-1]
        k = pl.program_id(2)

        @pl.when(k == 0)
        def _():
            acc[...] = jnp.zeros_like(acc)

        acc[...] += _dot(a_ref[...], b_ref[...], dims)

        @pl.when(k == nk - 1)
        def _():
            epi(acc[...], ex, outs)

    return _call(name, body, grid, ins, in_specs, out_shape, out_specs, [pltpu.VMEM(acc_shape, F32)],
                 ("parallel", "parallel", "arbitrary"), carry)


def _split(res, n, carry):
    own = res[0] if n == 1 else list(res[:n])
    return (own, res[n]) if carry else own


def _epi_store(dtype):
    def epi(acc, ex, outs):
        outs[0][...] = acc.astype(dtype)
    return epi


def _epi_bias(acc, ex, outs):
    outs[0][...] = acc + ex[0][...]


def _epi_relu2(acc, ex, outs):
    r = jnp.maximum(acc, 0.0)
    outs[0][...] = (r * r).astype(BF16)
    outs[1][...] = r.astype(BF16)


def _epi_mul2r(acc, ex, outs):
    outs[0][...] = (acc * (2.0 * ex[0][...].astype(F32))).astype(BF16)


def _epi_resid_gate(nctx, tm):
    def epi(acc, ex, outs):
        rows = pl.program_id(0) * tm + lax.broadcasted_iota(jnp.int32, (tm, 1), 0)
        g = jnp.where(rows < nctx, ex[1][0:1, :], ex[1][1:2, :])
        outs[0][...] = ex[0][...] + g * acc
        outs[1][...] = acc
    return epi


def mm_nn(name, a, w, epi, outs, extras=(), extra_kinds=(), tm_cap=1100, tn_cap=512, tk_cap=2048, carry=None):
    M, K = a.shape
    if w.ndim == 3:
        ns = w.shape[2]
        N = NDEV * ns
        tn = _tile(ns, tn_cap)
        nper = ns // tn
    else:
        N = w.shape[1]
        tn = _tile(N, tn_cap)
    tm = _tile(M, tm_cap)
    tk = _tile(K, tk_cap)
    grid = (M // tm, N // tn, K // tk)
    a_spec = pl.BlockSpec((tm, tk), lambda i, j, k: (i, k))
    if w.ndim == 3:
        b_spec = pl.BlockSpec((None, tk, tn), lambda i, j, k: (j // nper, k, j % nper))
    else:
        b_spec = pl.BlockSpec((tk, tn), lambda i, j, k: (k, j))
    ex_specs = []
    for e, kind in zip(extras, extra_kinds):
        if kind == 'mn':
            ex_specs.append(pl.BlockSpec((tm, tn), lambda i, j, k: (i, j)))
        else:
            ex_specs.append(pl.BlockSpec((e.shape[0], tn), lambda i, j, k: (0, j)))
    out_shape = [jax.ShapeDtypeStruct((M, N), dt) for dt in outs]
    out_specs = [pl.BlockSpec((tm, tn), lambda i, j, k: (i, j)) for _ in outs]
    res = _mm_core(name, grid, (a, w, *extras), [a_spec, b_spec, *ex_specs], out_shape, out_specs, NN, (tm, tn), epi,
                   carry)
    return (list(res[:len(outs)]), res[len(outs)]) if carry else res


def mm_nt(name, a, w, epi, out_dtype, extras=(), tm_cap=1100, to_cap=1024, tc_cap=2048, carry=None):
    M, N = a.shape
    tm = _tile(M, tm_cap)
    if w.ndim == 3:
        Kw, ns = w.shape[1], w.shape[2]
        tc = _tile(ns, tc_cap)
        cper = ns // tc
    else:
        Kw = w.shape[0]
        tc = _tile(N, tc_cap)
    to = _tile(Kw, to_cap)
    grid = (M // tm, Kw // to, N // tc)
    a_spec = pl.BlockSpec((tm, tc), lambda i, j, k: (i, k))
    if w.ndim == 3:
        b_spec = pl.BlockSpec((None, to, tc), lambda i, j, k: (k // cper, j, k % cper))
    else:
        b_spec = pl.BlockSpec((to, tc), lambda i, j, k: (j, k))
    ex_specs = [pl.BlockSpec((tm, to), lambda i, j, k: (i, j)) for _ in extras]
    out_shape = [jax.ShapeDtypeStruct((M, Kw), out_dtype)]
    out_specs = [pl.BlockSpec((tm, to), lambda i, j, k: (i, j))]
    return _split(_mm_core(name, grid, (a, w, *extras), [a_spec, b_spec, *ex_specs], out_shape, out_specs, NT, (tm, to),
                           epi, carry), 1, carry)


def mm_tn(name, a, b, shard_axis, to_cap=1024, tn_cap=1024, tc_cap=1100, carry=None):
    M, Ka = a.shape
    N = b.shape[1]
    tc = _tile(M, tc_cap)
    if shard_axis is None:
        to, tn = _tile(Ka, to_cap), _tile(N, tn_cap)
        shape = (Ka, N)
        oblk = (to, tn)
        omap = lambda i, j, k: (i, j)
    elif shard_axis == 1:
        ns = N // NDEV
        to, tn = _tile(Ka, to_cap), _tile(ns, tn_cap)
        per = ns // tn
        shape = (NDEV, Ka, ns)
        oblk = (None, to, tn)
        omap = lambda i, j, k: (_slot(j // per), i, j % per)
    else:
        rs = Ka // NDEV
        to, tn = _tile(rs, to_cap), _tile(N, tn_cap)
        per = rs // to
        shape = (NDEV, rs, N)
        oblk = (None, to, tn)
        omap = lambda i, j, k: (_slot(i // per), i % per, j)
    grid = (Ka // to, N // tn, M // tc)
    a_spec = pl.BlockSpec((tc, to), lambda i, j, k: (k, i))
    b_spec = pl.BlockSpec((tc, tn), lambda i, j, k: (k, j))
    out_shape = [jax.ShapeDtypeStruct(shape, F32)]
    out_specs = [pl.BlockSpec(oblk, omap)]
    return _split(_mm_core(name, grid, (a, b), [a_spec, b_spec], out_shape, out_specs, TN, (to, tn), _epi_store(F32),
                           carry), 1, carry)


def _row_spec(D):
    return pl.BlockSpec((ROW_TILE, D), lambda i: (i, 0))


def _const_spec(r, D):
    return pl.BlockSpec((r, D), lambda i: (0, 0))


def _grp(ref, is_ctx):
    return jnp.where(is_ctx, ref[0:1, :], ref[1:2, :])


def norm_mod(name, x, nw, sh, sc, nctx):
    R, D = x.shape
    assert R % ROW_TILE == 0 and nctx % ROW_TILE == 0

    def body(x_ref, nw_ref, sh_ref, sc_ref, o_ref):
        is_ctx = pl.program_id(0) * ROW_TILE < nctx
        xv = x_ref[...]
        rstd = lax.rsqrt(jnp.mean(xv * xv, axis=-1, keepdims=True) + EPS)
        n = xv * rstd * nw_ref[...]
        o_ref[...] = (n * (1.0 + _grp(sc_ref, is_ctx)) + _grp(sh_ref, is_ctx)).astype(BF16)

    return pl.pallas_call(
        body, grid=(R // ROW_TILE,),
        in_specs=[_row_spec(D), _const_spec(1, D), _const_spec(2, D), _const_spec(2, D)],
        out_specs=_row_spec(D), out_shape=jax.ShapeDtypeStruct((R, D), BF16),
        compiler_params=_cparams(("parallel",)), name=name)(x, nw, sh, sc)


def norm_bwd(name, x, dh, dres, nw, sc, nctx):
    R, D = x.shape
    assert R % ROW_TILE == 0 and nctx % ROW_TILE == 0

    def body(x_ref, dh_ref, dres_ref, nw_ref, sc_ref, dx_ref, part_ref):
        i = pl.program_id(0)
        is_ctx = i * ROW_TILE < nctx

        @pl.when(i == 0)
        def _():
            part_ref[...] = jnp.zeros_like(part_ref)

        xv = x_ref[...]
        dhv = dh_ref[...]
        w = nw_ref[...]
        rstd = lax.rsqrt(jnp.mean(xv * xv, axis=-1, keepdims=True) + EPS)
        xhat = xv * rstd
        n = xhat * w
        dn = dhv * (1.0 + _grp(sc_ref, is_ctx))
        dxhat = dn * w
        dx_ref[...] = dres_ref[...] + rstd * (dxhat - xhat * jnp.mean(dxhat * xhat, axis=-1, keepdims=True))
        s_sh = jnp.sum(dhv, axis=0, keepdims=True)
        s_sc = jnp.sum(dhv * n, axis=0, keepdims=True)
        s_nw = jnp.sum(dn * xhat, axis=0, keepdims=True)
        zero = jnp.zeros_like(s_sh)
        part_ref[0:1, :] += jnp.where(is_ctx, s_sh, zero)
        part_ref[1:2, :] += jnp.where(is_ctx, zero, s_sh)
        part_ref[2:3, :] += jnp.where(is_ctx, s_sc, zero)
        part_ref[3:4, :] += jnp.where(is_ctx, zero, s_sc)
        part_ref[4:5, :] += s_nw

    return pl.pallas_call(
        body, grid=(R // ROW_TILE,),
        in_specs=[_row_spec(D), _row_spec(D), _row_spec(D), _const_spec(1, D), _const_spec(2, D)],
        out_specs=[_row_spec(D), _const_spec(8, D)],
        out_shape=[jax.ShapeDtypeStruct((R, D), F32), jax.ShapeDtypeStruct((8, D), F32)],
        compiler_params=_cparams(("arbitrary",)), name=name)(x, dh, dres, nw, sc)


def gate_bwd(name, dx, y, g, nctx):
    R, D = dx.shape
    assert R % ROW_TILE == 0 and nctx % ROW_TILE == 0

    def body(dx_ref, y_ref, g_ref, dy_ref, part_ref):
        i = pl.program_id(0)
        is_ctx = i * ROW_TILE < nctx

        @pl.when(i == 0)
        def _():
            part_ref[...] = jnp.zeros_like(part_ref)

        dxv = dx_ref[...]
        dy_ref[...] = (dxv * _grp(g_ref, is_ctx)).astype(BF16)
        s = jnp.sum(dxv * y_ref[...], axis=0, keepdims=True)
        zero = jnp.zeros_like(s)
        part_ref[0:1, :] += jnp.where(is_ctx, s, zero)
        part_ref[1:2, :] += jnp.where(is_ctx, zero, s)

    return pl.pallas_call(
        body, grid=(R // ROW_TILE,),
        in_specs=[_row_spec(D), _row_spec(D), _const_spec(2, D)],
        out_specs=[_row_spec(D), _const_spec(8, D)],
        out_shape=[jax.ShapeDtypeStruct((R, D), BF16), jax.ShapeDtypeStruct((8, D), F32)],
        compiler_params=_cparams(("arbitrary",)), name=name)(dx, y, g)


def final_loss(name, x, fw, tgt):
    S, D = x.shape

    def body(x_ref, fw_ref, t_ref, dx_ref, loss_ref, dfw_ref):
        i = pl.program_id(0)

        @pl.when(i == 0)
        def _():
            loss_ref[...] = jnp.zeros_like(loss_ref)
            dfw_ref[...] = jnp.zeros_like(dfw_ref)

        xv = x_ref[...]
        w = fw_ref[...]
        rstd = lax.rsqrt(jnp.mean(xv * xv, axis=-1, keepdims=True) + EPS)
        xhat = xv * rstd
        e = xhat * w - t_ref[...]
        loss_ref[...] += 0.5 * jnp.sum(jnp.mean(e * e, axis=-1, keepdims=True))
        dout = e * (1.0 / D)
        dfw_ref[0:1, :] += jnp.sum(dout * xhat, axis=0, keepdims=True)
        dxhat = dout * w
        dx_ref[...] = rstd * (dxhat - xhat * jnp.mean(dxhat * xhat, axis=-1, keepdims=True))

    return pl.pallas_call(
        body, grid=(S // ROW_TILE,),
        in_specs=[_row_spec(D), _const_spec(1, D), _row_spec(D)],
        out_specs=[_row_spec(D), pl.BlockSpec((8, 128), lambda i: (0, 0)), _const_spec(8, D)],
        out_shape=[jax.ShapeDtypeStruct((S, D), F32), jax.ShapeDtypeStruct((8, 128), F32),
                   jax.ShapeDtypeStruct((8, D), F32)],
        compiler_params=_cparams(("arbitrary",)), name=name)(x, fw, tgt)


def _rope(x, cos, sa, sb):
    return x * cos + pltpu.roll(x, 96, 1) * sa + pltpu.roll(x, 32, 1) * sb


def _rope_t(dy, cos, sa, sb):
    return dy * cos + pltpu.roll(dy * sa, 32, 1) + pltpu.roll(dy * sb, 96, 1)


_EVEN_KINDS = ['qa'] * 8 + ['ka'] * 2 + ['v'] * 2 + ['qb'] * 8 + ['kb'] * 2 + ['v'] * 2
_EVEN_DSRC = ([('q', j) for j in range(8)] + [('k', 0), ('k', 1), ('v', 0), ('v', 1)]
              + [('q', 8 + j) for j in range(8)] + [('k', 2), ('k', 3), ('v', 2), ('v', 3)])


def _cols(j):
    return slice(j * HEAD, (j + 1) * HEAD)


def prep_even(name, qkv, qn, kn, cos, sa, sb):
    T, W = qkv.shape

    def body(x_ref, qn_ref, kn_ref, cos_ref, sa_ref, sb_ref, o_ref):
        cos_, sa_, sb_ = cos_ref[...], sa_ref[...], sb_ref[...]
        for j, kind in enumerate(_EVEN_KINDS):
            x = x_ref[:, _cols(j)]
            if kind in ('qa', 'ka'):
                rstd = lax.rsqrt(jnp.mean(x * x, axis=-1, keepdims=True) + EPS)
                x = x * rstd * (qn_ref[...] if kind == 'qa' else kn_ref[...])
            if kind != 'v':
                x = _rope(x, cos_, sa_, sb_)
            o_ref[:, _cols(j)] = x.astype(BF16)

    blk = pl.BlockSpec((ROW_TILE, W), lambda i: (i, 0))
    tab = pl.BlockSpec((ROW_TILE, HEAD), lambda i: (i, 0))
    one = pl.BlockSpec((1, HEAD), lambda i: (0, 0))
    return pl.pallas_call(
        body, grid=(T // ROW_TILE,), in_specs=[blk, one, one, tab, tab, tab], out_specs=blk,
        out_shape=jax.ShapeDtypeStruct(qkv.shape, BF16),
        compiler_params=_cparams(("parallel",)), name=name)(qkv, qn, kn, cos, sa, sb)


def prep_even_bwd(name, qkv, dq, dk, dv, qn, kn, cos, sa, sb):
    T, W = qkv.shape

    def body(x_ref, dq_ref, dk_ref, dv_ref, qn_ref, kn_ref, cos_ref, sa_ref, sb_ref, o_ref, part_ref):
        @pl.when(pl.program_id(0) == 0)
        def _():
            part_ref[...] = jnp.zeros_like(part_ref)

        cos_, sa_, sb_ = cos_ref[...], sa_ref[...], sb_ref[...]
        src = {'q': dq_ref, 'k': dk_ref, 'v': dv_ref}
        sums = {'qa': None, 'ka': None}
        for j, kind in enumerate(_EVEN_KINDS):
            which, blk_j = _EVEN_DSRC[j]
            d = src[which][:, _cols(blk_j)]
            if kind != 'v':
                d = _rope_t(d, cos_, sa_, sb_)
            if kind in ('qa', 'ka'):
                x = x_ref[:, _cols(j)]
                rstd = lax.rsqrt(jnp.mean(x * x, axis=-1, keepdims=True) + EPS)
                xhat = x * rstd
                s = jnp.sum(d * xhat, axis=0, keepdims=True)
                sums[kind] = s if sums[kind] is None else sums[kind] + s
                dxhat = d * (qn_ref[...] if kind == 'qa' else kn_ref[...])
                d = rstd * (dxhat - xhat * jnp.mean(dxhat * xhat, axis=-1, keepdims=True))
            o_ref[:, _cols(j)] = d.astype(BF16)
        part_ref[0:1, :] += sums['qa']
        part_ref[1:2, :] += sums['ka']

    def rows(w):
        return pl.BlockSpec((ROW_TILE, w), lambda i: (i, 0))

    one = pl.BlockSpec((1, HEAD), lambda i: (0, 0))
    return pl.pallas_call(
        body, grid=(T // ROW_TILE,),
        in_specs=[rows(W), rows(dq.shape[1]), rows(dk.shape[1]), rows(dv.shape[1]), one, one,
                  rows(HEAD), rows(HEAD), rows(HEAD)],
        out_specs=[rows(W), pl.BlockSpec((8, HEAD), lambda i: (0, 0))],
        out_shape=[jax.ShapeDtypeStruct(qkv.shape, BF16), jax.ShapeDtypeStruct((8, HEAD), F32)],
        compiler_params=_cparams(("arbitrary",)), name=name)(qkv, dq, dk, dv, qn, kn, cos, sa, sb)


def _even_maps():
    qmap = lambda h, qb: (qb, jnp.where(h < 8, h, h + 4))
    kmap = lambda h, qb: (0, jnp.where(h < 8, 8 + h // 4, 18 + h // 4))
    vmap = lambda h, qb: (0, jnp.where(h < 8, 10 + h // 4, 20 + h // 4))
    return qmap, kmap, vmap


def _softmax_parts(parts, extra=None):
    m = parts[0].max(axis=-1, keepdims=True)
    for p in parts[1:]:
        m = jnp.maximum(m, p.max(axis=-1, keepdims=True))
    if extra is not None:
        m = jnp.maximum(m, extra)
    es = [jnp.exp(p - m) for p in parts]
    l = es[0].sum(axis=-1, keepdims=True)
    for e in es[1:]:
        l = l + e.sum(axis=-1, keepdims=True)
    ex = None
    if extra is not None:
        ex = jnp.exp(extra - m)
        l = l + ex
    inv = 1.0 / l
    return [e * inv for e in es], (None if ex is None else ex * inv)


def _win_scores(q, k_ref, qb, tq, nctx, S):
    L = tq + 2 * WINDOW
    nqc = nctx // tq
    qlat = (qb - nqc) * tq
    start = pl.multiple_of(jnp.clip(qlat - WINDOW, 0, S - L), 128)
    kc = k_ref[0:nctx, :]
    kw = k_ref[pl.ds(nctx + start, L), :]
    s_c = _dot(q, kc, NT) * SCALE
    s_w = _dot(q, kw, NT) * SCALE
    qpos = qlat + lax.broadcasted_iota(jnp.int32, (tq, 1), 0)
    kpos = start + lax.broadcasted_iota(jnp.int32, (1, L), 1)
    valid = jnp.logical_and(jnp.abs(kpos - qpos) <= WINDOW, qb >= nqc)
    return s_c, jnp.where(valid, s_w, NEG), start, L


def _softmax_raw(raw):
    m = raw.max(axis=-1, keepdims=True)
    e = jnp.exp2((raw - m) * (SCALE * np.log2(np.e)))
    return e * (1.0 / e.sum(axis=-1, keepdims=True))


def _glob_keys(qb, tq, nctx, T):
    is_ctx = qb < nctx // tq
    return [(is_ctx, slice(0, nctx)), (jnp.logical_not(is_ctx), slice(0, T))]


def attn_even_fwd(name, qkvh, sink, nctx, tq, carry=None):
    T = qkvh.shape[0]
    S = T - nctx
    qmap, kmap, vmap = _even_maps()

    def body(sink_ref, q_ref, k_ref, v_ref, o_ref):
        h, qb = pl.program_id(0), pl.program_id(1)
        q = q_ref[...]

        for pred, keys in _glob_keys(qb, tq, nctx, T):
            @pl.when(jnp.logical_and(h < 8, pred))
            def _():
                p = _softmax_raw(_dot(q, k_ref[keys, :], NT))
                o_ref[...] = _dot(p, v_ref[keys, :], NN).astype(BF16)

        @pl.when(h >= 8)
        def _():
            s_c, s_w, start, L = _win_scores(q, k_ref, qb, tq, nctx, S)
            sk = jnp.full((tq, 1), sink_ref[jnp.maximum(h - 8, 0)], F32)
            (p_c, p_w), _ = _softmax_parts([s_c, s_w], sk)
            o = _dot(p_c, v_ref[0:nctx, :], NN) + _dot(p_w, v_ref[pl.ds(nctx + start, L), :], NN)
            o_ref[...] = o.astype(BF16)

    res = _call(name, body, (16, T // tq), (sink, qkvh, qkvh, qkvh),
                [pl.BlockSpec(memory_space=pltpu.SMEM), pl.BlockSpec((tq, HEAD), qmap),
                 pl.BlockSpec((T, HEAD), kmap), pl.BlockSpec((T, HEAD), vmap)],
                [jax.ShapeDtypeStruct((T, 16 * HEAD), BF16)], [pl.BlockSpec((tq, HEAD), lambda h, qb: (qb, h))],
                [], ("parallel", "arbitrary"), carry)
    return _split(res, 1, carry)


def attn_even_bwd(name, qkvh, sink, do, nctx, tq, carry=None):
    T = qkvh.shape[0]
    S = T - nctx
    qmap, kmap, vmap = _even_maps()

    def body(sink_ref, q_ref, k_ref, v_ref, do_ref, dq_ref, dk_ref, dv_ref, ds_ref):
        h, qb = pl.program_id(0), pl.program_id(1)
        q = q_ref[...]
        dov = do_ref[...]

        @pl.when(jnp.logical_and(h % 4 == 0, qb == 0))
        def _():
            dk_ref[...] = jnp.zeros_like(dk_ref)
            dv_ref[...] = jnp.zeros_like(dv_ref)

        @pl.when(qb == 0)
        def _():
            ds_ref[...] = jnp.zeros_like(ds_ref)

        for pred, keys in _glob_keys(qb, tq, nctx, T):
            @pl.when(jnp.logical_and(h < 8, pred))
            def _():
                p = _softmax_raw(_dot(q, k_ref[keys, :], NT))
                dp = _dot(dov, v_ref[keys, :], NT)
                row = jnp.sum(p * dp, axis=-1, keepdims=True)
                dsb = (p * (dp - row) * SCALE).astype(BF16)
                dq_ref[...] = _dot(dsb, k_ref[keys, :], NN)
                dk_ref[keys, :] += _dot(dsb, q, TN)
                dv_ref[keys, :] += _dot(p, dov, TN)

        @pl.when(h >= 8)
        def _():
            s_c, s_w, start, L = _win_scores(q, k_ref, qb, tq, nctx, S)
            sk = jnp.full((tq, 1), sink_ref[jnp.maximum(h - 8, 0)], F32)
            (p_c, p_w), p_s = _softmax_parts([s_c, s_w], sk)
            win = pl.ds(nctx + start, L)
            dp_c = _dot(dov, v_ref[0:nctx, :], NT)
            dp_w = _dot(dov, v_ref[win, :], NT)
            row = jnp.sum(p_c * dp_c, axis=-1, keepdims=True) + jnp.sum(p_w * dp_w, axis=-1, keepdims=True)
            ds_c = (p_c * (dp_c - row) * SCALE).astype(BF16)
            ds_w = (p_w * (dp_w - row) * SCALE).astype(BF16)
            dq_ref[...] = _dot(ds_c, k_ref[0:nctx, :], NN) + _dot(ds_w, k_ref[win, :], NN)
            dk_ref[0:nctx, :] += _dot(ds_c, q, TN)
            dk_ref[win, :] += _dot(ds_w, q, TN)
            dv_ref[0:nctx, :] += _dot(p_c, dov, TN)
            dv_ref[win, :] += _dot(p_w, dov, TN)
            ds_ref[...] += jnp.sum(-(p_s * row))

    kv_out = pl.BlockSpec((T, HEAD), lambda h, qb: (0, h // 4))
    res = _call(name, body, (16, T // tq), (sink, qkvh, qkvh, qkvh, do),
                [pl.BlockSpec(memory_space=pltpu.SMEM), pl.BlockSpec((tq, HEAD), qmap),
                 pl.BlockSpec((T, HEAD), kmap), pl.BlockSpec((T, HEAD), vmap),
                 pl.BlockSpec((tq, HEAD), lambda h, qb: (qb, h))],
                [jax.ShapeDtypeStruct((T, 16 * HEAD), F32), jax.ShapeDtypeStruct((T, 4 * HEAD), F32),
                 jax.ShapeDtypeStruct((T, 4 * HEAD), F32), jax.ShapeDtypeStruct((16, 8, 128), F32)],
                [pl.BlockSpec((tq, HEAD), lambda h, qb: (qb, h)), kv_out, kv_out,
                 pl.BlockSpec((None, 8, 128), lambda h, qb: (h, 0, 0))],
                [], ("arbitrary", "arbitrary"), carry)
    return _split(res, 4, carry)


def _na_row_start(r, rows):
    return jnp.clip(r - NA_KH // 2, 0, rows - NA_KH)


NA_ROWS_PER_STEP = 8


def _na_scores(q, k_ref, bias_ref, r, rows, nctx):
    LW = NA_KH * GRID_W
    rs = _na_row_start(r, rows)
    win = pl.ds(pl.multiple_of(nctx + rs * GRID_W, GRID_W), LW)
    s_c = _dot(q, k_ref[0:nctx, :], NT) * SCALE
    s_w = _dot(q, k_ref[win, :], NT) * SCALE + bias_ref[r - rs]
    return s_c, s_w, win


def _na_specs(T, nctx, rb):
    LW = NA_KH * GRID_W
    qoff = nctx // (rb * GRID_W)
    assert nctx % (rb * GRID_W) == 0
    q_spec = pl.BlockSpec((rb * GRID_W, HEAD), lambda h, g: (g + qoff, h))
    k_spec = pl.BlockSpec((T, HEAD), lambda h, g: (0, 16 + h))
    v_spec = pl.BlockSpec((T, HEAD), lambda h, g: (0, 32 + h))
    b_spec = pl.BlockSpec((None, NA_KH, GRID_W, LW), lambda h, g: (h, 0, 0, 0))
    row_spec = pl.BlockSpec((rb * GRID_W, HEAD), lambda h, g: (g, h))
    return q_spec, k_spec, v_spec, b_spec, row_spec


def attn_odd_fwd(name, qkv, bias8, nctx, carry=None):
    T = qkv.shape[0]
    S = T - nctx
    rows = S // GRID_W
    rb = min(NA_ROWS_PER_STEP, nctx // GRID_W)
    q_spec, k_spec, v_spec, b_spec, row_spec = _na_specs(T, nctx, rb)

    def body(q_ref, k_ref, v_ref, b_ref, o_ref):
        g = pl.program_id(1)
        for j in range(rb):
            qr = slice(j * GRID_W, (j + 1) * GRID_W)
            s_c, s_w, win = _na_scores(q_ref[qr, :], k_ref, b_ref, g * rb + j, rows, nctx)
            (p_c, p_w), _ = _softmax_parts([s_c, s_w])
            o_ref[qr, :] = (_dot(p_c, v_ref[0:nctx, :], NN) + _dot(p_w, v_ref[win, :], NN)).astype(BF16)

    res = _call(name, body, (16, rows // rb), (qkv, qkv, qkv, bias8), [q_spec, k_spec, v_spec, b_spec],
                [jax.ShapeDtypeStruct((S, 16 * HEAD), BF16)], [row_spec], [], ("parallel", "arbitrary"), carry)
    return _split(res, 1, carry)


def attn_odd_bwd(name, qkv, bias8, do, nctx, carry=None):
    T = qkv.shape[0]
    S = T - nctx
    rows = S // GRID_W
    rb = min(NA_ROWS_PER_STEP, nctx // GRID_W)
    q_spec, k_spec, v_spec, b_spec, row_spec = _na_specs(T, nctx, rb)

    def body(q_ref, k_ref, v_ref, b_ref, do_ref, dq_ref, dk_ref, dv_ref, db_ref):
        g = pl.program_id(1)

        @pl.when(g == 0)
        def _():
            dk_ref[...] = jnp.zeros_like(dk_ref)
            dv_ref[...] = jnp.zeros_like(dv_ref)
            db_ref[...] = jnp.zeros_like(db_ref)

        for j in range(rb):
            qr = slice(j * GRID_W, (j + 1) * GRID_W)
            r = g * rb + j
            q = q_ref[qr, :]
            dov = do_ref[qr, :]
            s_c, s_w, win = _na_scores(q, k_ref, b_ref, r, rows, nctx)
            (p_c, p_w), _ = _softmax_parts([s_c, s_w])
            dp_c = _dot(dov, v_ref[0:nctx, :], NT)
            dp_w = _dot(dov, v_ref[win, :], NT)
            row = jnp.sum(p_c * dp_c, axis=-1, keepdims=True) + jnp.sum(p_w * dp_w, axis=-1, keepdims=True)
            dsw = p_w * (dp_w - row)
            db_ref[r - _na_row_start(r, rows)] += dsw
            ds_c = (p_c * (dp_c - row) * SCALE).astype(BF16)
            ds_w = (dsw * SCALE).astype(BF16)
            dq_ref[qr, :] = _dot(ds_c, k_ref[0:nctx, :], NN) + _dot(ds_w, k_ref[win, :], NN)
            dk_ref[0:nctx, :] += _dot(ds_c, q, TN)
            dk_ref[win, :] += _dot(ds_w, q, TN)
            dv_ref[0:nctx, :] += _dot(p_c, dov, TN)
            dv_ref[win, :] += _dot(p_w, dov, TN)

    kv_out = pl.BlockSpec((T, HEAD), lambda h, g: (0, h))
    res = _call(name, body, (16, rows // rb), (qkv, qkv, qkv, bias8, do), [q_spec, k_spec, v_spec, b_spec, row_spec],
                [jax.ShapeDtypeStruct((S, 16 * HEAD), F32), jax.ShapeDtypeStruct((T, 16 * HEAD), F32),
                 jax.ShapeDtypeStruct((T, 16 * HEAD), F32), jax.ShapeDtypeStruct(bias8.shape, F32)],
                [row_spec, kv_out, kv_out, b_spec], [], ("arbitrary", "arbitrary"), carry)
    return _split(res, 4, carry)


def _na_onehots():
    o = np.arange(NA_KH)[:, None]
    i = np.arange(NA_KH)[None, :]
    a = i - o + NA_KH - 1
    A = (a[..., None] == np.arange(2 * NA_KH - 1)).astype(np.float32)
    qc = np.arange(GRID_W)[:, None]
    kc = np.arange(GRID_W)[None, :]
    b = np.clip(kc - qc + NA_KW - 1, 0, 2 * NA_KW - 2)
    cs = np.clip(qc - NA_KW // 2, 0, GRID_W - NA_KW)
    valid = (kc >= cs) & (kc < cs + NA_KW)
    B = ((b[..., None] == np.arange(2 * NA_KW - 1)) & valid[..., None]).astype(np.float32)
    return A, B, valid


def na_bias_table(rpb):
    A, B, valid = _na_onehots()
    hp = lax.Precision.HIGHEST
    t = jnp.einsum('hab,oia->hoib', rpb, jnp.asarray(A), precision=hp)
    bias = jnp.einsum('hoib,qkb->hoqik', t, jnp.asarray(B), precision=hp)
    bias = jnp.where(jnp.asarray(valid)[None, None, :, None, :], bias, NEG)
    return bias.reshape(rpb.shape[0], NA_KH, GRID_W, NA_KH * GRID_W)


def na_bias_grad(name, dbias8):
    A, B, _ = _na_onehots()
    H = dbias8.shape[0]
    nb, na = 2 * NA_KW - 1, 2 * NA_KH - 1
    d = dbias8.reshape(H, NA_KH, GRID_W, NA_KH, GRID_W).transpose(0, 1, 3, 2, 4)
    d = d.reshape(H * NA_KH * NA_KH, GRID_W * GRID_W)
    Bp = np.zeros((GRID_W * GRID_W, 128), np.float32)
    Bp[:, :nb] = B.reshape(GRID_W * GRID_W, nb)
    Ap = np.zeros((16, NA_KH * NA_KH), np.float32)
    Ap[:na] = A.reshape(NA_KH * NA_KH, na).T
    rows_per_head = NA_KH * NA_KH

    def split3(x):
        hi = x.astype(BF16)
        r1 = x - hi.astype(F32)
        mid = r1.astype(BF16)
        return hi, mid, (r1 - mid.astype(F32)).astype(BF16)

    def body(d_ref, b_ref, a_ref, o_ref):
        bm, am = b_ref[...], a_ref[...]
        g = sum(lax.dot_general(p, bm, NN, preferred_element_type=F32) for p in split3(d_ref[...]))
        o_ref[...] = sum(lax.dot_general(am, p, NN, preferred_element_type=F32) for p in split3(g))

    out = pl.pallas_call(
        body, grid=(H,),
        in_specs=[pl.BlockSpec((rows_per_head, GRID_W * GRID_W), lambda h: (h, 0)),
                  pl.BlockSpec((GRID_W * GRID_W, 128), lambda h: (0, 0)),
                  pl.BlockSpec((16, rows_per_head), lambda h: (0, 0))],
        out_specs=pl.BlockSpec((None, 16, 128), lambda h: (h, 0, 0)),
        out_shape=jax.ShapeDtypeStruct((H, 16, 128), F32),
        compiler_params=_cparams(("parallel",)), name=name)(d, jnp.asarray(Bp, BF16), jnp.asarray(Ap, BF16))
    return out[:, :na, :nb]


def _vmem_call(name, fn, out_shape, *arrays):
    def body(*refs):
        n = len(arrays)
        res = fn(*[r[...] for r in refs[:n]])
        if not isinstance(res, (tuple, list)):
            res = (res,)
        for o, v in zip(refs[n:], res):
            o[...] = v
    return pl.pallas_call(body, out_shape=out_shape, name=name,
                          compiler_params=pltpu.CompilerParams(vmem_limit_bytes=VMEM_LIMIT))(*arrays)


def _silu(v):
    return v / (1.0 + jnp.exp(-v))


def _adamw_math(w, g, m, v):
    m2 = ADAM_B1 * m + (1.0 - ADAM_B1) * g
    v2 = ADAM_B2 * v + (1.0 - ADAM_B2) * (g * g)
    m_hat = m2 / (1.0 - ADAM_B1 ** ADAM_STEP)
    v_hat = v2 / (1.0 - ADAM_B2 ** ADAM_STEP)
    delta = -ADAM_LR * (m_hat / (jnp.sqrt(v_hat) + ADAM_EPS) + ADAM_WD * w)
    return delta, m2, v2


def _ew_tile(R, C):
    return _tile(R, max(64, (262144 // C) // 64 * 64))


def adamw_rows(name, w, g, m, v, extra_g=None):
    R, C = w.shape
    tr = _ew_tile(R, C)
    extra_g = list(extra_g or [])
    ne = len(extra_g)

    def body(*refs):
        w_ref, g_ref, m_ref, v_ref = refs[:4]
        gs = g_ref[...]
        for e in refs[4:4 + ne]:
            gs = gs + e[...].astype(F32)
        go, do, mo, vo = refs[4 + ne:]
        d, m2, v2 = _adamw_math(w_ref[...], gs, m_ref[...], v_ref[...])
        go[...] = gs
        do[...] = d
        mo[...] = m2
        vo[...] = v2

    spec = pl.BlockSpec((tr, C), lambda i: (i, 0))
    return pl.pallas_call(
        body, grid=(R // tr,), in_specs=[spec] * (4 + ne), out_specs=[spec] * 4,
        out_shape=[jax.ShapeDtypeStruct((R, C), F32)] * 4,
        compiler_params=_cparams(("parallel",)), name=name)(w, g, m, v, *extra_g)


def rs_chip_sum(name, g8, sib4, half):
    _, R, C = g8.shape
    tr = _ew_tile(R, C)

    def body(s_ref, g_ref, b_ref, o_ref):
        o_ref[...] = (g_ref[...] + b_ref[...]).astype(BF16)

    blk = (None, tr, C)
    grid_spec = pltpu.PrefetchScalarGridSpec(
        num_scalar_prefetch=1, grid=(4, R // tr),
        in_specs=[pl.BlockSpec(blk, lambda q, i, s: (s[0] + q, i, 0)), pl.BlockSpec(blk, lambda q, i, s: (q, i, 0))],
        out_specs=pl.BlockSpec(blk, lambda q, i, s: (q, i, 0)))
    return pl.pallas_call(body, grid_spec=grid_spec, out_shape=jax.ShapeDtypeStruct((4, R, C), BF16),
                          compiler_params=_cparams(("parallel", "parallel")), name=name)(half, g8, sib4)


def adamw_rs(name, w, g8, sib4, rem3, m, v, idx):
    R, C = w.shape
    tr = _ew_tile(R, C)

    def body(s_ref, w_ref, g_ref, sb_ref, r0_ref, r1_ref, r2_ref, m_ref, v_ref, go, do, mo, vo):
        gs = g_ref[...] + sb_ref[...]
        for r_ref in (r0_ref, r1_ref, r2_ref):
            gs = gs + r_ref[...].astype(F32)
        d, m2, v2 = _adamw_math(w_ref[...], gs, m_ref[...], v_ref[...])
        go[...] = gs
        do[...] = d
        mo[...] = m2
        vo[...] = v2

    flat = pl.BlockSpec((tr, C), lambda i, s: (i, 0))
    blk = (None, tr, C)

    def rem(k):
        return pl.BlockSpec(blk, lambda i, s: (k, i, 0))

    grid_spec = pltpu.PrefetchScalarGridSpec(
        num_scalar_prefetch=1, grid=(R // tr,),
        in_specs=[flat, pl.BlockSpec(blk, lambda i, s: (s[0], i, 0)), pl.BlockSpec(blk, lambda i, s: (s[1], i, 0)),
                  rem(0), rem(1), rem(2), flat, flat],
        out_specs=[flat] * 4)
    return pl.pallas_call(body, grid_spec=grid_spec, out_shape=[jax.ShapeDtypeStruct((R, C), F32)] * 4,
                          compiler_params=_cparams(("parallel",)), name=name)(idx, w, g8, sib4, rem3, rem3, rem3, m, v)


def _me():
    x, y, c = lax.axis_index("x"), lax.axis_index("y"), lax.axis_index("c")
    return x, y, c


def _flip(v, bit):
    return 1 - v if bit else v


def ag_small(name, x, with_sum=False):
    R, C = x.shape

    def body(x_ref, out_ref, *rest):
        if with_sum:
            sum_ref, send_sems, recv_sems, lsem = rest
        else:
            send_sems, recv_sems, lsem = rest
        mx, my, mc = _me()
        me = 4 * mx + 2 * my + mc
        local = pltpu.make_async_copy(x_ref, out_ref.at[me], lsem)
        local.start()
        sends = []
        for k in range(1, NDEV):
            peer = (_flip(mx, k & 4), _flip(my, k & 2), _flip(mc, k & 1))
            cp = pltpu.make_async_remote_copy(src_ref=x_ref, dst_ref=out_ref.at[me], send_sem=send_sems.at[k - 1],
                                              recv_sem=recv_sems.at[k - 1], device_id=peer, device_id_type=MESH)
            cp.start()
            sends.append(cp)
        for k in range(1, NDEV):
            px, py, pc = _flip(mx, k & 4), _flip(my, k & 2), _flip(mc, k & 1)
            pltpu.make_async_remote_copy(src_ref=x_ref, dst_ref=out_ref.at[4 * px + 2 * py + pc],
                                         send_sem=send_sems.at[k - 1], recv_sem=recv_sems.at[k - 1],
                                         device_id=(px, py, pc), device_id_type=MESH).wait_recv()
        for cp in sends:
            cp.wait_send()
        local.wait()
        if with_sum:
            acc = out_ref[0]
            for d in range(1, NDEV):
                acc = acc + out_ref[d]
            sum_ref[...] = acc

    out_shape = [jax.ShapeDtypeStruct((NDEV, R, C), F32)]
    if with_sum:
        out_shape.append(jax.ShapeDtypeStruct((R, C), F32))
    vm = pl.BlockSpec(memory_space=pltpu.VMEM)
    res = pl.pallas_call(
        body, out_shape=out_shape, in_specs=[vm], out_specs=[vm] * len(out_shape),
        scratch_shapes=[pltpu.SemaphoreType.DMA((NDEV - 1,)), pltpu.SemaphoreType.DMA((NDEV - 1,)),
                        pltpu.SemaphoreType.DMA],
        compiler_params=pltpu.CompilerParams(vmem_limit_bytes=VMEM_LIMIT), name=name)(x)
    return res if with_sum else res[0]


def ag_big(name, shards):
    n = len(shards)

    def body(*refs):
        ins, outs = refs[:n], refs[n:2 * n]
        send_sems, recv_sems, lsems = refs[2 * n:]
        mx, my, mc = _me()
        me = (mx, my, mc)
        sibling = (mx, my, 1 - mc)
        chips = [(1 - mx, my), (mx, 1 - my), (1 - mx, 1 - my)]

        def idx(p):
            return 4 * p[0] + 2 * p[1] + p[2]

        def copy(t, k, block, to, src=None):
            dst = outs[t].at[idx(block)]
            return pltpu.make_async_remote_copy(
                src_ref=dst if src is None else src, dst_ref=dst, send_sem=send_sems.at[7 * t + k],
                recv_sem=recv_sems.at[7 * t + k], device_id=to, device_id_type=MESH)

        started = []
        locals_ = []
        for t in range(n):
            mine = pltpu.make_async_copy(ins[t], outs[t].at[idx(me)], lsems.at[t])
            mine.start()
            locals_.append(mine)
            first = [copy(t, 0, me, sibling, src=ins[t])]
            first += [copy(t, 1 + j, me, (*chip, mc), src=ins[t]) for j, chip in enumerate(chips)]
            for cp in first:
                cp.start()
            started += first
        for t in range(n):
            for j, chip in enumerate(chips):
                copy(t, 1 + j, (*chip, mc), me).wait_recv()
                fwd = copy(t, 4 + j, (*chip, mc), sibling)
                fwd.start()
                started.append(fwd)
        for t in range(n):
            copy(t, 0, sibling, me).wait_recv()
            for j, chip in enumerate(chips):
                copy(t, 4 + j, (*chip, 1 - mc), me).wait_recv()
        for cp in started:
            cp.wait_send()
        for mine in locals_:
            mine.wait()

    anyspec = pl.BlockSpec(memory_space=pl.ANY)
    return pl.pallas_call(
        body, out_shape=[jax.ShapeDtypeStruct((NDEV,) + s.shape, s.dtype) for s in shards],
        in_specs=[anyspec] * n, out_specs=[anyspec] * n,
        scratch_shapes=[pltpu.SemaphoreType.DMA((7 * n,)), pltpu.SemaphoreType.DMA((7 * n,)),
                        pltpu.SemaphoreType.DMA((n,))],
        name=name)(*shards)


def _idx(p):
    return 4 * p[0] + 2 * p[1] + p[2]


def _remote(src, dst, ss, rs, k, to):
    return pltpu.make_async_remote_copy(src_ref=src, dst_ref=dst, send_sem=ss.at[k], recv_sem=rs.at[k],
                                        device_id=to, device_id_type=MESH)


def ex_ag_chips(shards):
    n = len(shards)

    def copies(ci, co, ss, rs, base):
        mx, my, mc = _me()
        me = (mx, my, mc)
        peers = [(mx, my, 1 - mc), (1 - mx, my, mc), (mx, 1 - my, mc), (1 - mx, 1 - my, mc)]
        sends, recvs, local = [], [], []
        for t in range(n):
            b = base + 5 * t
            for k, peer in enumerate(peers):
                sends.append(_remote(ci[t], co[t].at[_idx(me)], ss, rs, b + k, peer))
                recvs.append(_remote(ci[t], co[t].at[_idx(peer)], ss, rs, b + k, peer))
            local.append(pltpu.make_async_copy(ci[t], co[t].at[_idx(me)], ss.at[b + 4]))
        return sends, recvs, local

    def start(ci, co, ss, rs, base):
        sends, _, local = copies(ci, co, ss, rs, base)
        for cp in local + sends:
            cp.start()

    def finish(ci, co, ss, rs, base):
        sends, recvs, local = copies(ci, co, ss, rs, base)
        for cp in recvs:
            cp.wait_recv()
        for cp in sends:
            cp.wait_send()
        for cp in local:
            cp.wait()

    outs = [jax.ShapeDtypeStruct((NDEV,) + s.shape, s.dtype) for s in shards]
    return Exchange(shards, outs, {}, 5 * n, start, finish)


def ex_ag_sibling(bufs):
    n = len(bufs)

    def copies(co, ss, rs, base):
        mx, my, mc = _me()
        sibling = (mx, my, 1 - mc)
        chips = [(1 - mx, my), (mx, 1 - my), (1 - mx, 1 - my)]
        sends, recvs = [], []
        for t in range(n):
            for j, chip in enumerate(chips):
                mine, theirs = co[t].at[_idx((*chip, mc))], co[t].at[_idx((*chip, 1 - mc))]
                sends.append(_remote(mine, mine, ss, rs, base + 3 * t + j, sibling))
                recvs.append(_remote(mine, theirs, ss, rs, base + 3 * t + j, sibling))
        return sends, recvs

    def start(ci, co, ss, rs, base):
        for cp in copies(co, ss, rs, base)[0]:
            cp.start()

    def finish(ci, co, ss, rs, base):
        sends, recvs = copies(co, ss, rs, base)
        for cp in recvs:
            cp.wait_recv()
        for cp in sends:
            cp.wait_send()

    outs = [jax.ShapeDtypeStruct(b.shape, b.dtype) for b in bufs]
    return Exchange(bufs, outs, {t: t for t in range(n)}, 3 * n, start, finish)


def ex_rs_sibling(grads):
    n = len(grads)

    def copies(ci, co, ss, rs, base):
        mx, my, mc = _me()
        return [_remote(ci[t].at[pl.ds((1 - mc) * 4, 4)], co[t], ss, rs, base + t, (mx, my, 1 - mc)) for t in range(n)]

    def start(ci, co, ss, rs, base):
        for cp in copies(ci, co, ss, rs, base):
            cp.start()

    def finish(ci, co, ss, rs, base):
        for cp in copies(ci, co, ss, rs, base):
            cp.wait()

    outs = [jax.ShapeDtypeStruct((4,) + g.shape[1:], g.dtype) for g in grads]
    return Exchange(grads, outs, {}, n, start, finish)


def ex_rs_chips(parts):
    n = len(parts)

    def copies(ci, co, ss, rs, base):
        mx, my, mc = _me()
        cps = []
        for t in range(n):
            for k in range(1, 4):
                px, py = _flip(mx, k & 2), _flip(my, k & 1)
                cps.append(_remote(ci[t].at[2 * px + py], co[t].at[k - 1], ss, rs, base + 3 * t + k - 1, (px, py, mc)))
        return cps

    def start(ci, co, ss, rs, base):
        for cp in copies(ci, co, ss, rs, base):
            cp.start()

    def finish(ci, co, ss, rs, base):
        for cp in copies(ci, co, ss, rs, base):
            cp.wait()

    outs = [jax.ShapeDtypeStruct((3,) + p.shape[1:], p.dtype) for p in parts]
    return Exchange(parts, outs, {}, 3 * n, start, finish)


def run_exchanges(name, xs):
    x = merge_exchanges(xs)
    n_ci, n_co = len(x.ins), len(x.out_shapes)

    def body(*refs):
        ci, co = refs[:n_ci], refs[n_ci:n_ci + n_co]
        ss, rs = refs[n_ci + n_co:]
        x.start(ci, co, ss, rs, 0)
        x.finish(ci, co, ss, rs, 0)

    hbm = pl.BlockSpec(memory_space=pl.ANY)
    return pl.pallas_call(
        body, out_shape=x.out_shapes, in_specs=[hbm] * n_ci, out_specs=[hbm] * n_co, input_output_aliases=x.aliases,
        scratch_shapes=[pltpu.SemaphoreType.DMA((x.n_sems,)), pltpu.SemaphoreType.DMA((x.n_sems,))], name=name)(*x.ins)


def _rope_tables(S, nctx):
    t = jnp.arange(S)
    row = (t // GRID_W).astype(F32)
    col = (t % GRID_W).astype(F32)
    pairs = HEAD // 4
    inv = ROPE_THETA ** (-jnp.arange(pairs, dtype=F32) / pairs)
    ang_r = row[:, None] * inv
    ang_c = col[:, None] * inv
    ang = jnp.concatenate([ang_r, ang_r, ang_c, ang_c], axis=-1)
    cos = jnp.concatenate([jnp.ones((nctx, HEAD), F32), jnp.cos(ang)], axis=0)
    sin = jnp.concatenate([jnp.zeros((nctx, HEAD), F32), jnp.sin(ang)], axis=0)
    lane = jnp.arange(HEAD)[None, :]
    first = (lane & 32) == 0
    return cos, jnp.where(first, -sin, 0.0), jnp.where(first, 0.0, sin)


def _pad_rows(v, rows):
    v = v.reshape(-1).astype(F32)
    return jnp.pad(v, (0, rows * 128 - v.shape[0])).reshape(rows, 128)


def _rows8(n):
    return -(-n // 1024) * 8


def kernel(x, c, ctx, c_ctx, ada_w, ada_b, norm_w, mlp_w1, mlp_w2, ev_w_in, ev_w_out, ev_q_norm, ev_k_norm, ev_sink, od_w_in, od_w_out, od_rpb, final_norm_w, loss_target, m_c_ctx, m_ada_w, m_ada_b, m_norm_w, m_mlp_w1, m_mlp_w2, m_ev_w_in, m_ev_w_out, m_ev_q_norm, m_ev_k_norm, m_ev_sink, m_od_w_in, m_od_w_out, m_od_rpb, m_final_norm_w, v_c_ctx, v_ada_w, v_ada_b, v_norm_w, v_mlp_w1, v_mlp_w2, v_ev_w_in, v_ev_w_out, v_ev_q_norm, v_ev_k_norm, v_ev_sink, v_od_w_in, v_od_w_out, v_od_rpb, v_final_norm_w):
    S, D = x.shape[1], x.shape[2]
    NC = ctx.shape[1]
    T = NC + S
    assert NC == ROW_TILE and S % GRID_W == 0
    ada_cols = ada_w.shape[2]
    nw_cols = norm_w.shape[2]
    me = 4 * lax.axis_index("x") + 2 * lax.axis_index("y") + lax.axis_index("c")

    pack1 = jnp.concatenate([_pad_rows(c, _rows8(D)), _pad_rows(norm_w, _rows8(4 * nw_cols))], axis=0)
    g1 = ag_small("ag_c_normw", pack1)
    c_all = g1[:, :D // 128].reshape(NDEV, D)
    nw_rows = _rows8(D)
    nw = g1[:, nw_rows:nw_rows + 4 * nw_cols // 128].reshape(NDEV, 2, 2, nw_cols)
    nw = nw.transpose(1, 2, 0, 3).reshape(2, 2, D)
    cin = jnp.concatenate([c_all, jnp.broadcast_to(c_ctx[None], (NDEV, D))], axis=0)
    act = _vmem_call("silu_c", lambda v: _silu(v).astype(BF16), jax.ShapeDtypeStruct((2 * NDEV, D), BF16), cin)
    ada_b_loc = lax.dynamic_slice_in_dim(ada_b, me * ada_cols, ada_cols, axis=1)
    mods = [mm_nn(f"mod{i}", act, ada_w[i], _epi_bias, [F32], extras=(ada_b_loc[i:i + 1],), extra_kinds=('n',))[0]
            for i in range(2)]
    gm = ag_small("ag_mod", jnp.concatenate(mods, axis=1))
    gm = gm.reshape(NDEV, 2 * NDEV, 2, ada_cols).transpose(2, 1, 0, 3).reshape(2, 2 * NDEV, NDEV * ada_cols)
    mod_lat = lax.dynamic_index_in_dim(gm, me, axis=1, keepdims=False)
    mod_ctx = gm[:, NDEV]
    mod2 = jnp.stack([mod_ctx, mod_lat], axis=1).reshape(2, 2, 6, D)

    def chunk(i, j):
        return mod2[i, :, j, :]

    def b16(w):
        return w.astype(BF16)

    w_in_e, w_out_e = ag_big("ag_weights_l0_attn", [b16(ev_w_in[0]), b16(ev_w_out[0])])
    w_out_e = w_out_e.reshape(-1, D)

    cos, sa, sb = _rope_tables(S, NC)
    bias8 = na_bias_table(od_rpb[0])
    sink = ev_sink[0]
    TQ_F, TQ_B = 256, 256

    X0 = jnp.concatenate([ctx[0], x[0]], axis=0)
    h_a = norm_mod("l0_norm1", X0, nw[0, 0][None], chunk(0, 0), chunk(0, 1), NC)
    qkv0 = mm_nn("l0_qkv", h_a, w_in_e, _epi_store(F32), [F32])[0]
    qkvh0 = prep_even("l0_prep", qkv0, ev_q_norm, ev_k_norm, cos, sa, sb)
    o0, mlp0_half = attn_even_fwd("l0_attn", qkvh0, sink, NC, TQ_F,
                                  carry=[ex_ag_chips([b16(mlp_w1[0]), b16(mlp_w2[0])])])
    tm0 = _tile(T, 1100)
    (X1, y0), (w1_0, w2_0) = mm_nn("l0_out", o0, w_out_e, _epi_resid_gate(NC, tm0), [F32, F32],
                                   extras=(X0, chunk(0, 2)), extra_kinds=('mn', 'n'),
                                   carry=[ex_ag_sibling(mlp0_half)])
    h_b = norm_mod("l0_norm2", X1, nw[0, 1][None], chunk(0, 3), chunk(0, 4), NC)
    (a0, r0), od_half = mm_nn("l0_up", h_b, w1_0, _epi_relu2, [BF16, BF16],
                              carry=[ex_ag_chips([b16(od_w_in[0]), b16(od_w_out[0])])])
    (X2, z0), (w1_1_half, w_in_o, w_out_o) = mm_nn(
        "l0_down", a0, w2_0.reshape(-1, D), _epi_resid_gate(NC, tm0), [F32, F32], extras=(X1, chunk(0, 5)),
        extra_kinds=('mn', 'n'), carry=[ex_ag_chips([b16(mlp_w1[1])]), ex_ag_sibling(od_half)])
    w_out_o = w_out_o.reshape(-1, D)

    h_c = norm_mod("l1_norm1", X2, nw[1, 0][None], chunk(1, 0), chunk(1, 1), NC)
    (qkv1,), (w2_1_half, w1_1) = mm_nn("l1_qkv", h_c, w_in_o, _epi_store(BF16), [BF16],
                                        carry=[ex_ag_chips([b16(mlp_w2[1])]), ex_ag_sibling([w1_1_half])])
    o1, (w2_1,) = attn_odd_fwd("l1_attn", qkv1, bias8, NC, carry=[ex_ag_sibling([w2_1_half])])
    X2l = X2[NC:]
    tm1 = _tile(S, 1100)
    X3, y1 = mm_nn("l1_out", o1, w_out_o, _epi_resid_gate(0, tm1), [F32, F32], extras=(X2l, chunk(1, 2)),
                   extra_kinds=('mn', 'n'))
    h_d = norm_mod("l1_norm2", X3, nw[1, 1][None], chunk(1, 3), chunk(1, 4), 0)
    a1, r1 = mm_nn("l1_up", h_d, w1_1, _epi_relu2, [BF16, BF16])
    X4, z1 = mm_nn("l1_down", a1, w2_1.reshape(-1, D), _epi_resid_gate(0, tm1), [F32, F32], extras=(X3, chunk(1, 5)),
                   extra_kinds=('mn', 'n'))
    dX4, loss_p, dfw_p = final_loss("final_loss", X4, final_norm_w[None], loss_target[0])
    w_in = [w_in_e, w_in_o]
    w_out = [w_out_e, w_out_o]
    w1 = [w1_0, w1_1]
    w2 = [w2_0.reshape(-1, D), w2_1.reshape(-1, D)]

    mc4 = (lax.axis_index("c") * 4).astype(jnp.int32)
    my_chip = (2 * lax.axis_index("x") + lax.axis_index("y")).astype(jnp.int32)

    def chip_sum(tag, g8, sib4):
        return rs_chip_sum(f"rs_chip_sum_{tag}", g8, sib4, mc4[None])

    dz1, pg2_1 = gate_bwd("l1_gate2_bwd", dX4, z1, chunk(1, 5), 0)
    du1 = mm_nt("l1_down_dx", dz1, w2[1], _epi_mul2r, BF16, extras=(r1,))
    g_w2_1 = mm_tn("l1_down_dw", a1, dz1, 0)
    dh_d, (sib_w2_1,) = mm_nt("l1_up_dx", du1, w1[1], _epi_store(F32), F32, carry=[ex_rs_sibling([g_w2_1])])
    g_w1_1, (rem_w2_1,) = mm_tn("l1_up_dw", h_d, du1, 1,
                                carry=[ex_rs_chips([chip_sum("w2_1", g_w2_1, sib_w2_1)])])
    dX3, pn2_1 = norm_bwd("l1_norm2_bwd", X3, dh_d, dX4, nw[1, 1][None], chunk(1, 4), 0)
    dy1, pg1_1 = gate_bwd("l1_gate1_bwd", dX3, y1, chunk(1, 2), 0)
    do1 = mm_nt("l1_out_dx", dy1, w_out[1], _epi_store(BF16), BF16)
    g_wout_1, (sib_w1_1,) = mm_tn("l1_out_dw", o1, dy1, 0, carry=[ex_rs_sibling([g_w1_1])])
    (dq1, dk1, dv1, dbias8), (rem_w1_1, sib_wout_1) = attn_odd_bwd(
        "l1_attn_bwd", qkv1, bias8, do1, NC,
        carry=[ex_rs_chips([chip_sum("w1_1", g_w1_1, sib_w1_1)]), ex_rs_sibling([g_wout_1])])
    dqkv1 = jnp.concatenate([jnp.pad(dq1, ((NC, 0), (0, 0))), dk1, dv1], axis=1).astype(BF16)
    dh_c, (rem_wout_1,) = mm_nt("l1_qkv_dx", dqkv1, w_in[1], _epi_store(F32), F32,
                                carry=[ex_rs_chips([chip_sum("wout_1", g_wout_1, sib_wout_1)])])
    g_win_1 = mm_tn("l1_qkv_dw", h_c, dqkv1, 1)
    dX2, pn1_1 = norm_bwd("l1_norm1_bwd", X2, dh_c, jnp.pad(dX3, ((NC, 0), (0, 0))), nw[1, 0][None], chunk(1, 1), NC)
    d_rpb = na_bias_grad("rpb_grad", dbias8)

    dz0, pg2_0 = gate_bwd("l0_gate2_bwd", dX2, z0, chunk(0, 5), NC)
    du0, (sib_win_1,) = mm_nt("l0_down_dx", dz0, w2[0], _epi_mul2r, BF16, extras=(r0,),
                              carry=[ex_rs_sibling([g_win_1])])
    g_w2_0, (rem_win_1,) = mm_tn("l0_down_dw", a0, dz0, 0,
                                 carry=[ex_rs_chips([chip_sum("win_1", g_win_1, sib_win_1)])])
    dh_b, (sib_w2_0,) = mm_nt("l0_up_dx", du0, w1[0], _epi_store(F32), F32, carry=[ex_rs_sibling([g_w2_0])])
    g_w1_0, (rem_w2_0,) = mm_tn("l0_up_dw", h_b, du0, 1,
                                carry=[ex_rs_chips([chip_sum("w2_0", g_w2_0, sib_w2_0)])])
    dX1, pn2_0 = norm_bwd("l0_norm2_bwd", X1, dh_b, dX2, nw[0, 1][None], chunk(0, 4), NC)
    dy0, pg1_0 = gate_bwd("l0_gate1_bwd", dX1, y0, chunk(0, 2), NC)
    do0 = mm_nt("l0_out_dx", dy0, w_out[0], _epi_store(BF16), BF16)
    g_wout_0, (sib_w1_0,) = mm_tn("l0_out_dw", o0, dy0, 0, carry=[ex_rs_sibling([g_w1_0])])
    (dq0, dk0, dv0, dsink_p), (rem_w1_0, sib_wout_0) = attn_even_bwd(
        "l0_attn_bwd", qkvh0, sink, do0, NC, TQ_B,
        carry=[ex_rs_chips([chip_sum("w1_0", g_w1_0, sib_w1_0)]), ex_rs_sibling([g_wout_0])])
    dqkv0, pqk = prep_even_bwd("l0_prep_bwd", qkv0, dq0, dk0, dv0, ev_q_norm, ev_k_norm, cos, sa, sb)
    dh_a, (rem_wout_0,) = mm_nt("l0_qkv_dx", dqkv0, w_in[0], _epi_store(F32), F32,
                                carry=[ex_rs_chips([chip_sum("wout_0", g_wout_0, sib_wout_0)])])
    g_win_0 = mm_tn("l0_qkv_dw", h_a, dqkv0, 1)
    dX0, pn1_0 = norm_bwd("l0_norm1_bwd", X0, dh_a, dX1, nw[0, 0][None], chunk(0, 1), NC)
    grad_x = dX0[NC:][None]
    (sib_win_0,) = run_exchanges("rs_sibling_last", [ex_rs_sibling([g_win_0])])
    (rem_win_0,) = run_exchanges("rs_chips_last", [ex_rs_chips([chip_sum("win_0", g_win_0, sib_win_0)])])

    def dmod(grp, pn1, pg1, pn2, pg2):
        return jnp.concatenate([pn1[grp], pn1[2 + grp], pg1[grp], pn2[grp], pn2[2 + grp], pg2[grp]])

    dmod_lat = jnp.stack([dmod(1, pn1_0, pg1_0, pn2_0, pg2_0), dmod(1, pn1_1, pg1_1, pn2_1, pg2_1)])
    dmod_ctx = jnp.stack([dmod(0, pn1_0, pg1_0, pn2_0, pg2_0), dmod(0, pn1_1, pg1_1, pn2_1, pg2_1)])
    dnw_p = jnp.stack([pn1_0[4], pn2_0[4], pn1_1[4], pn2_1[4]])
    pieces = [dmod_lat, dmod_ctx, dnw_p, pqk[0], pqk[1], dsink_p[8:, 0, 0], d_rpb, dfw_p[0], loss_p[0, 0]]
    sizes = [int(np.prod(p.shape)) for p in pieces]
    rows = [_rows8(s) for s in sizes]
    pack2 = jnp.concatenate([_pad_rows(p, r) for p, r in zip(pieces, rows)], axis=0)
    g2, s2 = ag_small("ag_small_grads", pack2, with_sum=True)
    offs = np.concatenate([[0], np.cumsum(rows)])

    def piece(arr, i, shape):
        return arr[..., offs[i]:offs[i + 1], :].reshape(arr.shape[:-2] + (-1,))[..., :sizes[i]].reshape(
            arr.shape[:-2] + shape)

    dmod_all = piece(g2, 0, (2, 6 * D))
    dmodc_sum = piece(s2, 1, (2, 6 * D))
    dnw_sum = piece(s2, 2, (2, 2, D))
    g_qn = piece(s2, 3, ev_q_norm.shape)
    g_kn = piece(s2, 4, ev_k_norm.shape)
    g_sink = piece(s2, 5, ev_sink.shape)
    g_rpb = piece(s2, 6, od_rpb.shape)
    g_fw = piece(s2, 7, final_norm_w.shape)
    loss = piece(s2, 8, ())

    dm16 = jnp.concatenate([dmod_all.transpose(1, 0, 2), dmodc_sum[:, None, :],
                            jnp.zeros((2, NDEV - 1, 6 * D), F32)], axis=1)
    dm16_loc = lax.dynamic_slice_in_dim(dm16.reshape(2, 2 * NDEV, NDEV, ada_cols), me, 1, axis=2)[:, :, 0, :]
    g_ada_b = _vmem_call("ada_b_grad", lambda v: jnp.sum(v, axis=1),
                         jax.ShapeDtypeStruct((2, 6 * D), F32), dm16)
    g_ada_w = []
    dact_p = None
    for i in range(2):
        dmb = dm16_loc[i].astype(BF16)
        g_ada_w.append(mm_tn(f"ada_w_grad{i}", act, dmb, None))
        part = mm_nt(f"ada_dact{i}", dmb, ada_w[i], _epi_store(F32), F32)
        dact_p = part if dact_p is None else dact_p + part
    _, dact = ag_small("ag_cctx", dact_p, with_sum=True)

    def cctx_grad(da, cc):
        sg = 1.0 / (1.0 + jnp.exp(-cc))
        return da[NDEV:NDEV + 1] * (sg * (1.0 + cc * (1.0 - sg)))

    g_cctx = _vmem_call("cctx_grad", cctx_grad, jax.ShapeDtypeStruct((1, D), F32), dact, c_ctx[None])[0]

    grads = [g_win_0, g_wout_0, g_w1_0, g_w2_0, g_win_1, g_wout_1, g_w1_1, g_w2_1]
    sib = [sib_win_0, sib_wout_0, sib_w1_0, sib_w2_0, sib_win_1, sib_wout_1, sib_w1_1, sib_w2_1]
    rem = [rem_win_0, rem_wout_0, rem_w1_0, rem_w2_0, rem_win_1, rem_wout_1, rem_w1_1, rem_w2_1]
    own_idx = jnp.stack([mc4 + my_chip, my_chip])
    big = {}
    names = ['ev_w_in', 'ev_w_out', 'mlp_w1_0', 'mlp_w2_0', 'od_w_in', 'od_w_out', 'mlp_w1_1', 'mlp_w2_1']
    wts = [ev_w_in[0], ev_w_out[0], mlp_w1[0], mlp_w2[0], od_w_in[0], od_w_out[0], mlp_w1[1], mlp_w2[1]]
    ms = [m_ev_w_in[0], m_ev_w_out[0], m_mlp_w1[0], m_mlp_w2[0], m_od_w_in[0], m_od_w_out[0], m_mlp_w1[1], m_mlp_w2[1]]
    vs = [v_ev_w_in[0], v_ev_w_out[0], v_mlp_w1[0], v_mlp_w2[0], v_od_w_in[0], v_od_w_out[0], v_mlp_w1[1], v_mlp_w2[1]]
    for t in range(8):
        big[names[t]] = adamw_rs(f"adamw_{names[t]}", wts[t], grads[t], sib[t], rem[t], ms[t], vs[t], own_idx)

    def stack2(a, b):
        return tuple(jnp.stack([u, v_]) for u, v_ in zip(big[a], big[b]))

    def one(a):
        return tuple(u[None] for u in big[a])

    r_mlp_w1, r_mlp_w2 = stack2('mlp_w1_0', 'mlp_w1_1'), stack2('mlp_w2_0', 'mlp_w2_1')
    r_ev_w_in, r_ev_w_out, r_od_w_in, r_od_w_out = one('ev_w_in'), one('ev_w_out'), one('od_w_in'), one('od_w_out')

    g_ada = jnp.stack(g_ada_w)
    r_ada_w = adamw_rows("adamw_ada_w", ada_w.reshape(2 * D, ada_cols), g_ada.reshape(2 * D, ada_cols),
                         m_ada_w.reshape(2 * D, ada_cols), v_ada_w.reshape(2 * D, ada_cols))
    r_ada_w = tuple(u.reshape(2, D, ada_cols) for u in r_ada_w)

    g_nw_loc = lax.dynamic_slice_in_dim(dnw_sum, me * nw_cols, nw_cols, axis=2)
    small = [(c_ctx, g_cctx, m_c_ctx, v_c_ctx), (ada_b, g_ada_b, m_ada_b, v_ada_b),
             (norm_w, g_nw_loc, m_norm_w, v_norm_w), (ev_q_norm, g_qn, m_ev_q_norm, v_ev_q_norm),
             (ev_k_norm, g_kn, m_ev_k_norm, v_ev_k_norm), (ev_sink, g_sink, m_ev_sink, v_ev_sink),
             (od_rpb, g_rpb, m_od_rpb, v_od_rpb), (final_norm_w, g_fw, m_final_norm_w, v_final_norm_w)]
    srows = [_rows8(int(np.prod(w.shape))) for w, _, _, _ in small]
    packs = [jnp.concatenate([_pad_rows(tup[k], r) for tup, r in zip(small, srows)], axis=0) for k in range(4)]
    sres = adamw_rows("adamw_small", *packs)
    soffs = np.concatenate([[0], np.cumsum(srows)])

    def unpack(arr, i):
        w = small[i][0]
        return arr[soffs[i]:soffs[i + 1]].reshape(-1)[:int(np.prod(w.shape))].reshape(w.shape)

    sm = [[unpack(sres[k], i) for i in range(len(small))] for k in range(4)]

    def outs(k):
        big_k = {'ada_w': r_ada_w[k], 'mlp_w1': r_mlp_w1[k], 'mlp_w2': r_mlp_w2[k], 'ev_w_in': r_ev_w_in[k],
                 'ev_w_out': r_ev_w_out[k], 'od_w_in': r_od_w_in[k], 'od_w_out': r_od_w_out[k]}
        return (sm[k][0], big_k['ada_w'], sm[k][1], sm[k][2], big_k['mlp_w1'], big_k['mlp_w2'], big_k['ev_w_in'],
                big_k['ev_w_out'], sm[k][3], sm[k][4], sm[k][5], big_k['od_w_in'], big_k['od_w_out'], sm[k][6],
                sm[k][7])

    return (loss, grad_x, *outs(0), *outs(1), *outs(2), *outs(3))
```

```python
import numpy as np
import jax
import jax.numpy as jnp
from jax import lax
from jax.experimental import pallas as pl
from jax.experimental.pallas import tpu as pltpu

F32 = jnp.float32
BF16 = jnp.bfloat16
MESH = pl.DeviceIdType.MESH

NDEV = 8
HEAD = 128
GRID_W = 64
NA_KH, NA_KW = 8, 16
WINDOW = 128
ROPE_THETA = 10000.0
EPS = 1e-6
NEG = -1e30
SCALE = HEAD ** -0.5
ROW_TILE = 256
VMEM_LIMIT = 56 * 1024 * 1024

ADAM_LR, ADAM_B1, ADAM_B2, ADAM_EPS, ADAM_WD, ADAM_STEP = 0.001, 0.9, 0.999, 1e-08, 0.01, 10

NT = (((1,), (1,)), ((), ()))
NN = (((1,), (0,)), ((), ()))
TN = (((0,), (0,)), ((), ()))


def _cparams(sem):
    return pltpu.CompilerParams(dimension_semantics=sem, vmem_limit_bytes=VMEM_LIMIT)


def _tile(n, cap):
    if n <= cap:
        return n
    t = cap - cap % 64
    while t >= 64:
        if n % t == 0:
            return t
        t -= 64
    raise ValueError((n, cap))


def _dot(a, b, dims):
    return lax.dot_general(a.astype(BF16), b.astype(BF16), dims, preferred_element_type=F32)


def _slot(d):
    return (d % 2) * 4 + d // 2


class Exchange:
    def __init__(self, ins, out_shapes, aliases, n_sems, start, finish):
        self.ins, self.out_shapes, self.aliases, self.n_sems = list(ins), list(out_shapes), dict(aliases), n_sems
        self.start, self.finish = start, finish


def merge_exchanges(xs):
    ins, outs, aliases, bases, n = [], [], {}, [], 0
    for x in xs:
        bases.append((len(ins), len(outs), n))
        aliases.update({len(ins) + i: len(outs) + o for i, o in x.aliases.items()})
        ins += x.ins
        outs += x.out_shapes
        n += x.n_sems

    def run(which):
        def f(ci, co, ss, rs, base):
            for x, (i0, o0, s0) in zip(xs, bases):
                getattr(x, which)(ci[i0:i0 + len(x.ins)], co[o0:o0 + len(x.out_shapes)], ss, rs, base + s0)
        return f

    return Exchange(ins, outs, aliases, n, run('start'), run('finish'))


def _call(name, body, grid, ins, in_specs, out_shape, out_specs, scratch, sems, carry=None):
    if not carry:
        return pl.pallas_call(body, grid=grid, in_specs=in_specs, out_specs=out_specs, out_shape=out_shape,
                              scratch_shapes=scratch, compiler_params=_cparams(sems), name=name)(*ins)
    x = merge_exchanges(carry)
    n_in, n_ci, n_out, n_co, n_sc = len(ins), len(x.ins), len(out_shape), len(x.out_shapes), len(scratch)

    def wrapped(*refs):
        p = [0]

        def take(k):
            p[0] += k
            return refs[p[0] - k:p[0]]

        a, ci, o, co, sc = take(n_in), take(n_ci), take(n_out), take(n_co), take(n_sc)
        ss, rs = take(2)
        first = pl.program_id(0) == 0
        last = pl.program_id(0) == grid[0] - 1
        for d in range(1, len(grid)):
            first = jnp.logical_and(first, pl.program_id(d) == 0)
            last = jnp.logical_and(last, pl.program_id(d) == grid[d] - 1)

        @pl.when(first)
        def _():
            x.start(ci, co, ss, rs, 0)

        body(*a, *o, *sc)

        @pl.when(last)
        def _():
            x.finish(ci, co, ss, rs, 0)

    hbm = pl.BlockSpec(memory_space=pl.ANY)
    res = pl.pallas_call(
        wrapped, grid=grid, in_specs=list(in_specs) + [hbm] * n_ci, out_specs=list(out_specs) + [hbm] * n_co,
        out_shape=list(out_shape) + x.out_shapes,
        input_output_aliases={n_in + i: n_out + o for i, o in x.aliases.items()},
        scratch_shapes=list(scratch) + [pltpu.SemaphoreType.DMA((x.n_sems,)), pltpu.SemaphoreType.DMA((x.n_sems,))],
        compiler_params=_cparams(("arbitrary",) * len(grid)), name=name)(*ins, *x.ins)
    return list(res[:n_out]) + [list(res[n_out:])]


def _mm_core(name, grid, ins, in_specs, out_shape, out_specs, dims, acc_shape, epi, carry=None):
    nk = grid[2]
    n_extra = len(ins) - 2

    def body(*refs):
        a_ref, b_ref = refs[0], refs[1]
        ex = refs[2:2 + n_extra]
        outs = refs[2 + n_extra:-1]
        acc = refs[-1]
        k = pl.program_id(2)

        @pl.when(k == 0)
        def _():
            acc[...] = jnp.zeros_like(acc)

        acc[...] += _dot(a_ref[...], b_ref[...], dims)

        @pl.when(k == nk - 1)
        def _():
            epi(acc[...], ex, outs)

    return _call(name, body, grid, ins, in_specs, out_shape, out_specs, [pltpu.VMEM(acc_shape, F32)],
                 ("parallel", "parallel", "arbitrary"), carry)


def _split(res, n, carry):
    own = res[0] if n == 1 else list(res[:n])
    return (own, res[n]) if carry else own


def _epi_store(dtype):
    def epi(acc, ex, outs):
        outs[0][...] = acc.astype(dtype)
    return epi


def _epi_bias(acc, ex, outs):
    outs[0][...] = acc + ex[0][...]


def _epi_relu2(acc, ex, outs):
    r = jnp.maximum(acc, 0.0)
    outs[0][...] = (r * r).astype(BF16)
    outs[1][...] = r.astype(BF16)


def _epi_mul2r(acc, ex, outs):
    outs[0][...] = (acc * (2.0 * ex[0][...].astype(F32))).astype(BF16)


def _epi_resid_gate(nctx, tm):
    def epi(acc, ex, outs):
        rows = pl.program_id(0) * tm + lax.broadcasted_iota(jnp.int32, (tm, 1), 0)
        g = jnp.where(rows < nctx, ex[1][0:1, :], ex[1][1:2, :])
        outs[0][...] = ex[0][...] + g * acc
        outs[1][...] = acc
    return epi


def mm_nn(name, a, w, epi, outs, extras=(), extra_kinds=(), tm_cap=1100, tn_cap=512, tk_cap=2048, carry=None):
    M, K = a.shape
    if w.ndim == 3:
        ns = w.shape[2]
        N = NDEV * ns
        tn = _tile(ns, tn_cap)
        nper = ns // tn
    else:
        N = w.shape[1]
        tn = _tile(N, tn_cap)
    tm = _tile(M, tm_cap)
    tk = _tile(K, tk_cap)
    grid = (M // tm, N // tn, K // tk)
    a_spec = pl.BlockSpec((tm, tk), lambda i, j, k: (i, k))
    if w.ndim == 3:
        b_spec = pl.BlockSpec((None, tk, tn), lambda i, j, k: (j // nper, k, j % nper))
    else:
        b_spec = pl.BlockSpec((tk, tn), lambda i, j, k: (k, j))
    ex_specs = []
    for e, kind in zip(extras, extra_kinds):
        if kind == 'mn':
            ex_specs.append(pl.BlockSpec((tm, tn), lambda i, j, k: (i, j)))
        else:
            ex_specs.append(pl.BlockSpec((e.shape[0], tn), lambda i, j, k: (0, j)))
    out_shape = [jax.ShapeDtypeStruct((M, N), dt) for dt in outs]
    out_specs = [pl.BlockSpec((tm, tn), lambda i, j, k: (i, j)) for _ in outs]
    res = _mm_core(name, grid, (a, w, *extras), [a_spec, b_spec, *ex_specs], out_shape, out_specs, NN, (tm, tn), epi,
                   carry)
    return (list(res[:len(outs)]), res[len(outs)]) if carry else res


def mm_nt(name, a, w, epi, out_dtype, extras=(), tm_cap=1100, to_cap=1024, tc_cap=2048, carry=None):
    M, N = a.shape
    tm = _tile(M, tm_cap)
    if w.ndim == 3:
        Kw, ns = w.shape[1], w.shape[2]
        tc = _tile(ns, tc_cap)
        cper = ns // tc
    else:
        Kw = w.shape[0]
        tc = _tile(N, tc_cap)
    to = _tile(Kw, to_cap)
    grid = (M // tm, Kw // to, N // tc)
    a_spec = pl.BlockSpec((tm, tc), lambda i, j, k: (i, k))
    if w.ndim == 3:
        b_spec = pl.BlockSpec((None, to, tc), lambda i, j, k: (k // cper, j, k % cper))
    else:
        b_spec = pl.BlockSpec((to, tc), lambda i, j, k: (j, k))
    ex_specs = [pl.BlockSpec((tm, to), lambda i, j, k: (i, j)) for _ in extras]
    out_shape = [jax.ShapeDtypeStruct((M, Kw), out_dtype)]
    out_specs = [pl.BlockSpec((tm, to), lambda i, j, k: (i, j))]
    return _split(_mm_core(name, grid, (a, w, *extras), [a_spec, b_spec, *ex_specs], out_shape, out_specs, NT, (tm, to),
                           epi, carry), 1, carry)


def mm_tn(name, a, b, shard_axis, to_cap=1024, tn_cap=1024, tc_cap=1100, carry=None):
    M, Ka = a.shape
    N = b.shape[1]
    tc = _tile(M, tc_cap)
    if shard_axis is None:
        to, tn = _tile(Ka, to_cap), _tile(N, tn_cap)
        shape = (Ka, N)
        oblk = (to, tn)
        omap = lambda i, j, k: (i, j)
    elif shard_axis == 1:
        ns = N // NDEV
        to, tn = _tile(Ka, to_cap), _tile(ns, tn_cap)
        per = ns // tn
        shape = (NDEV, Ka, ns)
        oblk = (None, to, tn)
        omap = lambda i, j, k: (_slot(j // per), i, j % per)
    else:
        rs = Ka // NDEV
        to, tn = _tile(rs, to_cap), _tile(N, tn_cap)
        per = rs // to
        shape = (NDEV, rs, N)
        oblk = (None, to, tn)
        omap = lambda i, j, k: (_slot(i // per), i % per, j)
    grid = (Ka // to, N // tn, M // tc)
    a_spec = pl.BlockSpec((tc, to), lambda i, j, k: (k, i))
    b_spec = pl.BlockSpec((tc, tn), lambda i, j, k: (k, j))
    out_shape = [jax.ShapeDtypeStruct(shape, F32)]
    out_specs = [pl.BlockSpec(oblk, omap)]
    return _split(_mm_core(name, grid, (a, b), [a_spec, b_spec], out_shape, out_specs, TN, (to, tn), _epi_store(F32),
                           carry), 1, carry)


def _row_spec(D):
    return pl.BlockSpec((ROW_TILE, D), lambda i: (i, 0))


def _const_spec(r, D):
    return pl.BlockSpec((r, D), lambda i: (0, 0))


def _grp(ref, is_ctx):
    return jnp.where(is_ctx, ref[0:1, :], ref[1:2, :])


def norm_mod(name, x, nw, sh, sc, nctx):
    R, D = x.shape
    assert R % ROW_TILE == 0 and nctx % ROW_TILE == 0

    def body(x_ref, nw_ref, sh_ref, sc_ref, o_ref):
        is_ctx = pl.program_id(0) * ROW_TILE < nctx
        xv = x_ref[...]
        rstd = lax.rsqrt(jnp.mean(xv * xv, axis=-1, keepdims=True) + EPS)
        n = xv * rstd * nw_ref[...]
        o_ref[...] = (n * (1.0 + _grp(sc_ref, is_ctx)) + _grp(sh_ref, is_ctx)).astype(BF16)

    return pl.pallas_call(
        body, grid=(R // ROW_TILE,),
        in_specs=[_row_spec(D), _const_spec(1, D), _const_spec(2, D), _const_spec(2, D)],
        out_specs=_row_spec(D), out_shape=jax.ShapeDtypeStruct((R, D), BF16),
        compiler_params=_cparams(("parallel",)), name=name)(x, nw, sh, sc)


def norm_bwd(name, x, dh, dres, nw, sc, nctx, dres_skip=0, out_skip=0):
    R, D = x.shape
    assert R % ROW_TILE == 0 and nctx % ROW_TILE == 0 and dres_skip % ROW_TILE == 0 and out_skip % ROW_TILE == 0
    res_tiles, out_tiles = dres_skip // ROW_TILE, out_skip // ROW_TILE

    def body(x_ref, dh_ref, dres_ref, nw_ref, sc_ref, dx_ref, part_ref):
        i = pl.program_id(0)
        is_ctx = i * ROW_TILE < nctx

        @pl.when(i == 0)
        def _():
            part_ref[...] = jnp.zeros_like(part_ref)

        xv = x_ref[...]
        dhv = dh_ref[...]
        w = nw_ref[...]
        rstd = lax.rsqrt(jnp.mean(xv * xv, axis=-1, keepdims=True) + EPS)
        xhat = xv * rstd
        n = xhat * w
        dn = dhv * (1.0 + _grp(sc_ref, is_ctx))
        dxhat = dn * w
        dres = dres_ref[...]
        if res_tiles:
            dres = jnp.where(i < res_tiles, 0.0, dres)
        dx_ref[...] = dres + rstd * (dxhat - xhat * jnp.mean(dxhat * xhat, axis=-1, keepdims=True))
        s_sh = jnp.sum(dhv, axis=0, keepdims=True)
        s_sc = jnp.sum(dhv * n, axis=0, keepdims=True)
        s_nw = jnp.sum(dn * xhat, axis=0, keepdims=True)
        zero = jnp.zeros_like(s_sh)
        part_ref[0:1, :] += jnp.where(is_ctx, s_sh, zero)
        part_ref[1:2, :] += jnp.where(is_ctx, zero, s_sh)
        part_ref[2:3, :] += jnp.where(is_ctx, s_sc, zero)
        part_ref[3:4, :] += jnp.where(is_ctx, zero, s_sc)
        part_ref[4:5, :] += s_nw

    return pl.pallas_call(
        body, grid=(R // ROW_TILE,),
        in_specs=[_row_spec(D), _row_spec(D),
                  pl.BlockSpec((ROW_TILE, D), lambda i: (jnp.maximum(i - res_tiles, 0), 0)),
                  _const_spec(1, D), _const_spec(2, D)],
        out_specs=[pl.BlockSpec((ROW_TILE, D), lambda i: (jnp.maximum(i - out_tiles, 0), 0)), _const_spec(8, D)],
        out_shape=[jax.ShapeDtypeStruct((R - out_skip, D), F32), jax.ShapeDtypeStruct((8, D), F32)],
        compiler_params=_cparams(("arbitrary",)), name=name)(x, dh, dres, nw, sc)


def gate_bwd(name, dx, y, g, nctx):
    R, D = dx.shape
    assert R % ROW_TILE == 0 and nctx % ROW_TILE == 0

    def body(dx_ref, y_ref, g_ref, dy_ref, part_ref):
        i = pl.program_id(0)
        is_ctx = i * ROW_TILE < nctx

        @pl.when(i == 0)
        def _():
            part_ref[...] = jnp.zeros_like(part_ref)

        dxv = dx_ref[...]
        dy_ref[...] = (dxv * _grp(g_ref, is_ctx)).astype(BF16)
        s = jnp.sum(dxv * y_ref[...], axis=0, keepdims=True)
        zero = jnp.zeros_like(s)
        part_ref[0:1, :] += jnp.where(is_ctx, s, zero)
        part_ref[1:2, :] += jnp.where(is_ctx, zero, s)

    return pl.pallas_call(
        body, grid=(R // ROW_TILE,),
        in_specs=[_row_spec(D), _row_spec(D), _const_spec(2, D)],
        out_specs=[_row_spec(D), _const_spec(8, D)],
        out_shape=[jax.ShapeDtypeStruct((R, D), BF16), jax.ShapeDtypeStruct((8, D), F32)],
        compiler_params=_cparams(("arbitrary",)), name=name)(dx, y, g)


def final_loss(name, x, fw, tgt):
    S, D = x.shape

    def body(x_ref, fw_ref, t_ref, dx_ref, loss_ref, dfw_ref):
        i = pl.program_id(0)

        @pl.when(i == 0)
        def _():
            loss_ref[...] = jnp.zeros_like(loss_ref)
            dfw_ref[...] = jnp.zeros_like(dfw_ref)

        xv = x_ref[...]
        w = fw_ref[...]
        rstd = lax.rsqrt(jnp.mean(xv * xv, axis=-1, keepdims=True) + EPS)
        xhat = xv * rstd
        e = xhat * w - t_ref[...]
        loss_ref[...] += 0.5 * jnp.sum(jnp.mean(e * e, axis=-1, keepdims=True))
        dout = e * (1.0 / D)
        dfw_ref[0:1, :] += jnp.sum(dout * xhat, axis=0, keepdims=True)
        dxhat = dout * w
        dx_ref[...] = rstd * (dxhat - xhat * jnp.mean(dxhat * xhat, axis=-1, keepdims=True))

    return pl.pallas_call(
        body, grid=(S // ROW_TILE,),
        in_specs=[_row_spec(D), _const_spec(1, D), _row_spec(D)],
        out_specs=[_row_spec(D), pl.BlockSpec((8, 128), lambda i: (0, 0)), _const_spec(8, D)],
        out_shape=[jax.ShapeDtypeStruct((S, D), F32), jax.ShapeDtypeStruct((8, 128), F32),
                   jax.ShapeDtypeStruct((8, D), F32)],
        compiler_params=_cparams(("arbitrary",)), name=name)(x, fw, tgt)


def _rope(x, cos, sa, sb):
    return x * cos + pltpu.roll(x, 96, 1) * sa + pltpu.roll(x, 32, 1) * sb


def _rope_t(dy, cos, sa, sb):
    return dy * cos + pltpu.roll(dy * sa, 32, 1) + pltpu.roll(dy * sb, 96, 1)


_EVEN_KINDS = ['qa'] * 8 + ['ka'] * 2 + ['v'] * 2 + ['qb'] * 8 + ['kb'] * 2 + ['v'] * 2
_EVEN_DSRC = ([('q', j) for j in range(8)] + [('k', 0), ('k', 1), ('v', 0), ('v', 1)]
              + [('q', 8 + j) for j in range(8)] + [('k', 2), ('k', 3), ('v', 2), ('v', 3)])


def _cols(j):
    return slice(j * HEAD, (j + 1) * HEAD)


def prep_even(name, qkv, qn, kn, cos, sa, sb):
    T, W = qkv.shape

    def body(x_ref, qn_ref, kn_ref, cos_ref, sa_ref, sb_ref, o_ref):
        cos_, sa_, sb_ = cos_ref[...], sa_ref[...], sb_ref[...]
        for j, kind in enumerate(_EVEN_KINDS):
            x = x_ref[:, _cols(j)]
            if kind in ('qa', 'ka'):
                rstd = lax.rsqrt(jnp.mean(x * x, axis=-1, keepdims=True) + EPS)
                x = x * rstd * (qn_ref[...] if kind == 'qa' else kn_ref[...])
            if kind != 'v':
                x = _rope(x, cos_, sa_, sb_)
            o_ref[:, _cols(j)] = x.astype(BF16)

    blk = pl.BlockSpec((ROW_TILE, W), lambda i: (i, 0))
    tab = pl.BlockSpec((ROW_TILE, HEAD), lambda i: (i, 0))
    one = pl.BlockSpec((1, HEAD), lambda i: (0, 0))
    return pl.pallas_call(
        body, grid=(T // ROW_TILE,), in_specs=[blk, one, one, tab, tab, tab], out_specs=blk,
        out_shape=jax.ShapeDtypeStruct(qkv.shape, BF16),
        compiler_params=_cparams(("parallel",)), name=name)(qkv, qn, kn, cos, sa, sb)


def prep_even_bwd(name, qkv, dq, dk, dv, qn, kn, cos, sa, sb):
    T, W = qkv.shape

    def body(x_ref, dq_ref, dk_ref, dv_ref, qn_ref, kn_ref, cos_ref, sa_ref, sb_ref, o_ref, part_ref):
        @pl.when(pl.program_id(0) == 0)
        def _():
            part_ref[...] = jnp.zeros_like(part_ref)

        cos_, sa_, sb_ = cos_ref[...], sa_ref[...], sb_ref[...]
        src = {'q': dq_ref, 'k': dk_ref, 'v': dv_ref}
        sums = {'qa': None, 'ka': None}
        for j, kind in enumerate(_EVEN_KINDS):
            which, blk_j = _EVEN_DSRC[j]
            d = src[which][:, _cols(blk_j)]
            if kind != 'v':
                d = _rope_t(d, cos_, sa_, sb_)
            if kind in ('qa', 'ka'):
                x = x_ref[:, _cols(j)]
                rstd = lax.rsqrt(jnp.mean(x * x, axis=-1, keepdims=True) + EPS)
                xhat = x * rstd
                s = jnp.sum(d * xhat, axis=0, keepdims=True)
                sums[kind] = s if sums[kind] is None else sums[kind] + s
                dxhat = d * (qn_ref[...] if kind == 'qa' else kn_ref[...])
                d = rstd * (dxhat - xhat * jnp.mean(dxhat * xhat, axis=-1, keepdims=True))
            o_ref[:, _cols(j)] = d.astype(BF16)
        part_ref[0:1, :] += sums['qa']
        part_ref[1:2, :] += sums['ka']

    def rows(w):
        return pl.BlockSpec((ROW_TILE, w), lambda i: (i, 0))

    one = pl.BlockSpec((1, HEAD), lambda i: (0, 0))
    return pl.pallas_call(
        body, grid=(T // ROW_TILE,),
        in_specs=[rows(W), rows(dq.shape[1]), rows(dk.shape[1]), rows(dv.shape[1]), one, one,
                  rows(HEAD), rows(HEAD), rows(HEAD)],
        out_specs=[rows(W), pl.BlockSpec((8, HEAD), lambda i: (0, 0))],
        out_shape=[jax.ShapeDtypeStruct(qkv.shape, BF16), jax.ShapeDtypeStruct((8, HEAD), F32)],
        compiler_params=_cparams(("arbitrary",)), name=name)(qkv, dq, dk, dv, qn, kn, cos, sa, sb)


def _even_maps():
    qmap = lambda h, qb: (qb, jnp.where(h < 8, h, h + 4))
    kmap = lambda h, qb: (0, jnp.where(h < 8, 8 + h // 4, 18 + h // 4))
    vmap = lambda h, qb: (0, jnp.where(h < 8, 10 + h // 4, 20 + h // 4))
    return qmap, kmap, vmap


def _softmax_parts(parts, extra=None):
    m = parts[0].max(axis=-1, keepdims=True)
    for p in parts[1:]:
        m = jnp.maximum(m, p.max(axis=-1, keepdims=True))
    if extra is not None:
        m = jnp.maximum(m, extra)
    es = [jnp.exp(p - m) for p in parts]
    l = es[0].sum(axis=-1, keepdims=True)
    for e in es[1:]:
        l = l + e.sum(axis=-1, keepdims=True)
    ex = None
    if extra is not None:
        ex = jnp.exp(extra - m)
        l = l + ex
    inv = 1.0 / l
    return [e * inv for e in es], (None if ex is None else ex * inv)


def _win_scores(q, k_ref, qb, tq, nctx, S):
    L = tq + 2 * WINDOW
    nqc = nctx // tq
    qlat = (qb - nqc) * tq
    start = pl.multiple_of(jnp.clip(qlat - WINDOW, 0, S - L), 128)
    kc = k_ref[0:nctx, :]
    kw = k_ref[pl.ds(nctx + start, L), :]
    s_c = _dot(q, kc, NT) * SCALE
    s_w = _dot(q, kw, NT) * SCALE
    qpos = qlat + lax.broadcasted_iota(jnp.int32, (tq, 1), 0)
    kpos = start + lax.broadcasted_iota(jnp.int32, (1, L), 1)
    valid = jnp.logical_and(jnp.abs(kpos - qpos) <= WINDOW, qb >= nqc)
    return s_c, jnp.where(valid, s_w, NEG), start, L


def _softmax_raw(raw):
    m = raw.max(axis=-1, keepdims=True)
    e = jnp.exp2((raw - m) * (SCALE * np.log2(np.e)))
    return e * (1.0 / e.sum(axis=-1, keepdims=True))


def _glob_keys(qb, tq, nctx, T):
    is_ctx = qb < nctx // tq
    return [(is_ctx, slice(0, nctx)), (jnp.logical_not(is_ctx), slice(0, T))]


def attn_even_fwd(name, qkvh, sink, nctx, tq, carry=None):
    T = qkvh.shape[0]
    S = T - nctx
    qmap, kmap, vmap = _even_maps()

    def body(sink_ref, q_ref, k_ref, v_ref, o_ref):
        h, qb = pl.program_id(0), pl.program_id(1)
        q = q_ref[...]

        for pred, keys in _glob_keys(qb, tq, nctx, T):
            @pl.when(jnp.logical_and(h < 8, pred))
            def _():
                p = _softmax_raw(_dot(q, k_ref[keys, :], NT))
                o_ref[...] = _dot(p, v_ref[keys, :], NN).astype(BF16)

        @pl.when(h >= 8)
        def _():
            s_c, s_w, start, L = _win_scores(q, k_ref, qb, tq, nctx, S)
            sk = jnp.full((tq, 1), sink_ref[jnp.maximum(h - 8, 0)], F32)
            (p_c, p_w), _ = _softmax_parts([s_c, s_w], sk)
            o = _dot(p_c, v_ref[0:nctx, :], NN) + _dot(p_w, v_ref[pl.ds(nctx + start, L), :], NN)
            o_ref[...] = o.astype(BF16)

    res = _call(name, body, (16, T // tq), (sink, qkvh, qkvh, qkvh),
                [pl.BlockSpec(memory_space=pltpu.SMEM), pl.BlockSpec((tq, HEAD), qmap),
                 pl.BlockSpec((T, HEAD), kmap), pl.BlockSpec((T, HEAD), vmap)],
                [jax.ShapeDtypeStruct((T, 16 * HEAD), BF16)], [pl.BlockSpec((tq, HEAD), lambda h, qb: (qb, h))],
                [], ("parallel", "arbitrary"), carry)
    return _split(res, 1, carry)


def attn_even_bwd(name, qkvh, sink, do, nctx, tq, carry=None):
    T = qkvh.shape[0]
    S = T - nctx
    qmap, kmap, vmap = _even_maps()

    def body(sink_ref, q_ref, k_ref, v_ref, do_ref, dq_ref, dk_ref, dv_ref, ds_ref):
        h, qb = pl.program_id(0), pl.program_id(1)
        q = q_ref[...]
        dov = do_ref[...]

        @pl.when(jnp.logical_and(h % 4 == 0, qb == 0))
        def _():
            dk_ref[...] = jnp.zeros_like(dk_ref)
            dv_ref[...] = jnp.zeros_like(dv_ref)

        @pl.when(qb == 0)
        def _():
            ds_ref[...] = jnp.zeros_like(ds_ref)

        for pred, keys in _glob_keys(qb, tq, nctx, T):
            @pl.when(jnp.logical_and(h < 8, pred))
            def _():
                p = _softmax_raw(_dot(q, k_ref[keys, :], NT))
                dp = _dot(dov, v_ref[keys, :], NT)
                row = jnp.sum(p * dp, axis=-1, keepdims=True)
                dsb = (p * (dp - row) * SCALE).astype(BF16)
                dq_ref[...] = _dot(dsb, k_ref[keys, :], NN)
                dk_ref[keys, :] += _dot(dsb, q, TN)
                dv_ref[keys, :] += _dot(p, dov, TN)

        @pl.when(h >= 8)
        def _():
            s_c, s_w, start, L = _win_scores(q, k_ref, qb, tq, nctx, S)
            sk = jnp.full((tq, 1), sink_ref[jnp.maximum(h - 8, 0)], F32)
            (p_c, p_w), p_s = _softmax_parts([s_c, s_w], sk)
            win = pl.ds(nctx + start, L)
            dp_c = _dot(dov, v_ref[0:nctx, :], NT)
            dp_w = _dot(dov, v_ref[win, :], NT)
            row = jnp.sum(p_c * dp_c, axis=-1, keepdims=True) + jnp.sum(p_w * dp_w, axis=-1, keepdims=True)
            ds_c = (p_c * (dp_c - row) * SCALE).astype(BF16)
            ds_w = (p_w * (dp_w - row) * SCALE).astype(BF16)
            dq_ref[...] = _dot(ds_c, k_ref[0:nctx, :], NN) + _dot(ds_w, k_ref[win, :], NN)
            dk_ref[0:nctx, :] += _dot(ds_c, q, TN)
            dk_ref[win, :] += _dot(ds_w, q, TN)
            dv_ref[0:nctx, :] += _dot(p_c, dov, TN)
            dv_ref[win, :] += _dot(p_w, dov, TN)
            ds_ref[...] += jnp.sum(-(p_s * row))

    kv_out = pl.BlockSpec((T, HEAD), lambda h, qb: (0, h // 4))
    res = _call(name, body, (16, T // tq), (sink, qkvh, qkvh, qkvh, do),
                [pl.BlockSpec(memory_space=pltpu.SMEM), pl.BlockSpec((tq, HEAD), qmap),
                 pl.BlockSpec((T, HEAD), kmap), pl.BlockSpec((T, HEAD), vmap),
                 pl.BlockSpec((tq, HEAD), lambda h, qb: (qb, h))],
                [jax.ShapeDtypeStruct((T, 16 * HEAD), F32), jax.ShapeDtypeStruct((T, 4 * HEAD), F32),
                 jax.ShapeDtypeStruct((T, 4 * HEAD), F32), jax.ShapeDtypeStruct((16, 8, 128), F32)],
                [pl.BlockSpec((tq, HEAD), lambda h, qb: (qb, h)), kv_out, kv_out,
                 pl.BlockSpec((None, 8, 128), lambda h, qb: (h, 0, 0))],
                [], ("arbitrary", "arbitrary"), carry)
    return _split(res, 4, carry)


NA_GROUP = 4
NA_SPAN = NA_KH + NA_GROUP - 1
_NA_PLAN = [[(j, 0) for j in range(NA_GROUP)],
            [(NA_KH // 2, j) for j in range(NA_GROUP)],
            [(NA_KH // 2 + j, NA_GROUP - 1) for j in range(NA_GROUP)]]


def _na_group(g, n_groups, rows):
    last = g == n_groups - 1
    kind = jnp.where(g == 0, 0, jnp.where(last, 2, 1))
    first_row = jnp.where(g == 0, 0, jnp.where(last, rows - NA_SPAN, NA_GROUP * g - NA_KH // 2))
    return kind, first_row


def na_span_bias(bias8):
    LW, LS = NA_KH * GRID_W, NA_SPAN * GRID_W
    kinds = []
    for plan in _NA_PLAN:
        strips = [jnp.pad(bias8[:, off], ((0, 0), (0, 0), (s * GRID_W, LS - LW - s * GRID_W)), constant_values=NEG)
                  for off, s in plan]
        kinds.append(jnp.concatenate(strips, axis=1))
    return jnp.stack(kinds, axis=1)


def na_span_bias_grad(db):
    LW = NA_KH * GRID_W
    out = [None] * NA_KH
    for kind, plan in enumerate(_NA_PLAN):
        for j, (off, s) in enumerate(plan):
            piece = db[:, kind, j * GRID_W:(j + 1) * GRID_W, s * GRID_W:s * GRID_W + LW]
            out[off] = piece if out[off] is None else out[off] + piece
    return jnp.stack(out, axis=1)


def _na_specs(T, nctx, n_groups, rows):
    LS = NA_SPAN * GRID_W
    tq = NA_GROUP * GRID_W
    assert nctx % tq == 0 and n_groups >= 3
    q_spec = pl.BlockSpec((tq, HEAD), lambda h, g: (g + nctx // tq, h))
    k_spec = pl.BlockSpec((T, HEAD), lambda h, g: (0, 16 + h))
    v_spec = pl.BlockSpec((T, HEAD), lambda h, g: (0, 32 + h))
    b_spec = pl.BlockSpec((None, None, tq, LS), lambda h, g: (h, _na_group(g, n_groups, rows)[0], 0, 0))
    row_spec = pl.BlockSpec((tq, HEAD), lambda h, g: (g, h))
    return q_spec, k_spec, v_spec, b_spec, row_spec


def _na_scores(q, k_ref, b_ref, g, n_groups, rows, nctx):
    first_row = _na_group(g, n_groups, rows)[1]
    win = pl.ds(pl.multiple_of(nctx + first_row * GRID_W, GRID_W), NA_SPAN * GRID_W)
    s_c = _dot(q, k_ref[0:nctx, :], NT) * SCALE
    s_w = _dot(q, k_ref[win, :], NT) * SCALE + b_ref[...]
    return s_c, s_w, win


def attn_odd_fwd(name, qkv, bias_s, nctx, carry=None):
    T = qkv.shape[0]
    S = T - nctx
    rows = S // GRID_W
    n_groups = rows // NA_GROUP
    q_spec, k_spec, v_spec, b_spec, row_spec = _na_specs(T, nctx, n_groups, rows)

    def body(q_ref, k_ref, v_ref, b_ref, o_ref):
        s_c, s_w, win = _na_scores(q_ref[...], k_ref, b_ref, pl.program_id(1), n_groups, rows, nctx)
        (p_c, p_w), _ = _softmax_parts([s_c, s_w])
        o_ref[...] = (_dot(p_c, v_ref[0:nctx, :], NN) + _dot(p_w, v_ref[win, :], NN)).astype(BF16)

    res = _call(name, body, (16, n_groups), (qkv, qkv, qkv, bias_s), [q_spec, k_spec, v_spec, b_spec],
                [jax.ShapeDtypeStruct((S, 16 * HEAD), BF16)], [row_spec], [], ("parallel", "arbitrary"), carry)
    return _split(res, 1, carry)


def attn_odd_bwd(name, qkv, bias_s, do, nctx, carry=None):
    T = qkv.shape[0]
    S = T - nctx
    rows = S // GRID_W
    n_groups = rows // NA_GROUP
    q_spec, k_spec, v_spec, b_spec, row_spec = _na_specs(T, nctx, n_groups, rows)

    def body(q_ref, k_ref, v_ref, b_ref, do_ref, dq_ref, dk_ref, dv_ref, db_ref):
        g = pl.program_id(1)
        q = q_ref[...]
        dov = do_ref[...]

        @pl.when(g == 0)
        def _():
            dk_ref[...] = jnp.zeros_like(dk_ref)
            dv_ref[...] = jnp.zeros_like(dv_ref)

        s_c, s_w, win = _na_scores(q, k_ref, b_ref, g, n_groups, rows, nctx)
        (p_c, p_w), _ = _softmax_parts([s_c, s_w])
        dp_c = _dot(dov, v_ref[0:nctx, :], NT)
        dp_w = _dot(dov, v_ref[win, :], NT)
        row = jnp.sum(p_c * dp_c, axis=-1, keepdims=True) + jnp.sum(p_w * dp_w, axis=-1, keepdims=True)
        dsw = p_w * (dp_w - row)
        first_visit = jnp.logical_or(g <= 1, g == n_groups - 1)

        @pl.when(first_visit)
        def _():
            db_ref[...] = dsw

        @pl.when(jnp.logical_not(first_visit))
        def _():
            db_ref[...] += dsw

        ds_c = (p_c * (dp_c - row) * SCALE).astype(BF16)
        ds_w = (dsw * SCALE).astype(BF16)
        dq_ref[...] = _dot(ds_c, k_ref[0:nctx, :], NN) + _dot(ds_w, k_ref[win, :], NN)
        dk_ref[0:nctx, :] += _dot(ds_c, q, TN)
        dk_ref[win, :] += _dot(ds_w, q, TN)
        dv_ref[0:nctx, :] += _dot(p_c, dov, TN)
        dv_ref[win, :] += _dot(p_w, dov, TN)

    kv_out = pl.BlockSpec((T, HEAD), lambda h, g: (0, h))
    res = _call(name, body, (16, n_groups), (qkv, qkv, qkv, bias_s, do), [q_spec, k_spec, v_spec, b_spec, row_spec],
                [jax.ShapeDtypeStruct((S, 16 * HEAD), F32), jax.ShapeDtypeStruct((T, 16 * HEAD), F32),
                 jax.ShapeDtypeStruct((T, 16 * HEAD), F32), jax.ShapeDtypeStruct(bias_s.shape, F32)],
                [row_spec, kv_out, kv_out, b_spec], [], ("arbitrary", "arbitrary"), carry)
    return _split(res, 4, carry)


def _na_onehots():
    o = np.arange(NA_KH)[:, None]
    i = np.arange(NA_KH)[None, :]
    a = i - o + NA_KH - 1
    A = (a[..., None] == np.arange(2 * NA_KH - 1)).astype(np.float32)
    qc = np.arange(GRID_W)[:, None]
    kc = np.arange(GRID_W)[None, :]
    b = np.clip(kc - qc + NA_KW - 1, 0, 2 * NA_KW - 2)
    cs = np.clip(qc - NA_KW // 2, 0, GRID_W - NA_KW)
    valid = (kc >= cs) & (kc < cs + NA_KW)
    B = ((b[..., None] == np.arange(2 * NA_KW - 1)) & valid[..., None]).astype(np.float32)
    return A, B, valid


def na_bias_table(rpb):
    A, B, valid = _na_onehots()
    hp = lax.Precision.HIGHEST
    t = jnp.einsum('hab,oia->hoib', rpb, jnp.asarray(A), precision=hp)
    bias = jnp.einsum('hoib,qkb->hoqik', t, jnp.asarray(B), precision=hp)
    bias = jnp.where(jnp.asarray(valid)[None, None, :, None, :], bias, NEG)
    return bias.reshape(rpb.shape[0], NA_KH, GRID_W, NA_KH * GRID_W)


def na_bias_grad(name, dbias8):
    A, B, _ = _na_onehots()
    H = dbias8.shape[0]
    nb, na = 2 * NA_KW - 1, 2 * NA_KH - 1
    d = dbias8.reshape(H, NA_KH, GRID_W, NA_KH, GRID_W).transpose(0, 1, 3, 2, 4)
    d = d.reshape(H * NA_KH * NA_KH, GRID_W * GRID_W)
    Bp = np.zeros((GRID_W * GRID_W, 128), np.float32)
    Bp[:, :nb] = B.reshape(GRID_W * GRID_W, nb)
    Ap = np.zeros((16, NA_KH * NA_KH), np.float32)
    Ap[:na] = A.reshape(NA_KH * NA_KH, na).T
    rows_per_head = NA_KH * NA_KH

    def split3(x):
        hi = x.astype(BF16)
        r1 = x - hi.astype(F32)
        mid = r1.astype(BF16)
        return hi, mid, (r1 - mid.astype(F32)).astype(BF16)

    def body(d_ref, b_ref, a_ref, o_ref):
        bm, am = b_ref[...], a_ref[...]
        g = sum(lax.dot_general(p, bm, NN, preferred_element_type=F32) for p in split3(d_ref[...]))
        o_ref[...] = sum(lax.dot_general(am, p, NN, preferred_element_type=F32) for p in split3(g))

    out = pl.pallas_call(
        body, grid=(H,),
        in_specs=[pl.BlockSpec((rows_per_head, GRID_W * GRID_W), lambda h: (h, 0)),
                  pl.BlockSpec((GRID_W * GRID_W, 128), lambda h: (0, 0)),
                  pl.BlockSpec((16, rows_per_head), lambda h: (0, 0))],
        out_specs=pl.BlockSpec((None, 16, 128), lambda h: (h, 0, 0)),
        out_shape=jax.ShapeDtypeStruct((H, 16, 128), F32),
        compiler_params=_cparams(("parallel",)), name=name)(d, jnp.asarray(Bp, BF16), jnp.asarray(Ap, BF16))
    return out[:, :na, :nb]


def _vmem_call(name, fn, out_shape, *arrays):
    def body(*refs):
        n = len(arrays)
        res = fn(*[r[...] for r in refs[:n]])
        if not isinstance(res, (tuple, list)):
            res = (res,)
        for o, v in zip(refs[n:], res):
            o[...] = v
    return pl.pallas_call(body, out_shape=out_shape, name=name,
                          compiler_params=pltpu.CompilerParams(vmem_limit_bytes=VMEM_LIMIT))(*arrays)


def _silu(v):
    return v / (1.0 + jnp.exp(-v))


def _adamw_math(w, g, m, v):
    m2 = ADAM_B1 * m + (1.0 - ADAM_B1) * g
    v2 = ADAM_B2 * v + (1.0 - ADAM_B2) * (g * g)
    m_hat = m2 / (1.0 - ADAM_B1 ** ADAM_STEP)
    v_hat = v2 / (1.0 - ADAM_B2 ** ADAM_STEP)
    delta = -ADAM_LR * (m_hat / (jnp.sqrt(v_hat) + ADAM_EPS) + ADAM_WD * w)
    return delta, m2, v2


def _ew_tile(R, C):
    return _tile(R, max(64, (262144 // C) // 64 * 64))


def adamw_rows(name, w, g, m, v, extra_g=None):
    R, C = w.shape
    tr = _ew_tile(R, C)
    extra_g = list(extra_g or [])
    ne = len(extra_g)

    def body(*refs):
        w_ref, g_ref, m_ref, v_ref = refs[:4]
        gs = g_ref[...]
        for e in refs[4:4 + ne]:
            gs = gs + e[...].astype(F32)
        go, do, mo, vo = refs[4 + ne:]
        d, m2, v2 = _adamw_math(w_ref[...], gs, m_ref[...], v_ref[...])
        go[...] = gs
        do[...] = d
        mo[...] = m2
        vo[...] = v2

    spec = pl.BlockSpec((tr, C), lambda i: (i, 0))
    return pl.pallas_call(
        body, grid=(R // tr,), in_specs=[spec] * (4 + ne), out_specs=[spec] * 4,
        out_shape=[jax.ShapeDtypeStruct((R, C), F32)] * 4,
        compiler_params=_cparams(("parallel",)), name=name)(w, g, m, v, *extra_g)


def rs_chip_sum(name, g8, sib4, half):
    _, R, C = g8.shape
    tr = _ew_tile(R, C)

    def body(s_ref, g_ref, b_ref, o_ref):
        o_ref[...] = (g_ref[...] + b_ref[...]).astype(BF16)

    blk = (None, tr, C)
    grid_spec = pltpu.PrefetchScalarGridSpec(
        num_scalar_prefetch=1, grid=(4, R // tr),
        in_specs=[pl.BlockSpec(blk, lambda q, i, s: (s[0] + q, i, 0)), pl.BlockSpec(blk, lambda q, i, s: (q, i, 0))],
        out_specs=pl.BlockSpec(blk, lambda q, i, s: (q, i, 0)))
    return pl.pallas_call(body, grid_spec=grid_spec, out_shape=jax.ShapeDtypeStruct((4, R, C), BF16),
                          compiler_params=_cparams(("parallel", "parallel")), name=name)(half, g8, sib4)


def adamw_rs(name, w, g8, sib4, rem3, m, v, idx, layer, prev=None):
    L, R, C = w.shape
    tr = _ew_tile(R, C)

    def body(s_ref, w_ref, g_ref, sb_ref, r0_ref, r1_ref, r2_ref, m_ref, v_ref, *rest):
        go, do, mo, vo = rest[-4:]
        gs = g_ref[...] + sb_ref[...]
        for r_ref in (r0_ref, r1_ref, r2_ref):
            gs = gs + r_ref[...].astype(F32)
        d, m2, v2 = _adamw_math(w_ref[...], gs, m_ref[...], v_ref[...])
        go[...] = gs
        do[...] = d
        mo[...] = m2
        vo[...] = v2

    blk = (None, tr, C)
    mine = pl.BlockSpec(blk, lambda i, s: (layer, i, 0))

    def rem(k):
        return pl.BlockSpec(blk, lambda i, s: (k, i, 0))

    prev = list(prev or [])
    grid_spec = pltpu.PrefetchScalarGridSpec(
        num_scalar_prefetch=1, grid=(R // tr,),
        in_specs=[mine, pl.BlockSpec(blk, lambda i, s: (s[0], i, 0)), pl.BlockSpec(blk, lambda i, s: (s[1], i, 0)),
                  rem(0), rem(1), rem(2), mine, mine] + [pl.BlockSpec(memory_space=pl.ANY)] * len(prev),
        out_specs=[mine] * 4)
    return pl.pallas_call(body, grid_spec=grid_spec, out_shape=[jax.ShapeDtypeStruct((L, R, C), F32)] * 4,
                          input_output_aliases={9 + k: k for k in range(len(prev))},
                          compiler_params=_cparams(("parallel",)), name=name)(
                              idx, w, g8, sib4, rem3, rem3, rem3, m, v, *prev)


def _me():
    x, y, c = lax.axis_index("x"), lax.axis_index("y"), lax.axis_index("c")
    return x, y, c


def _flip(v, bit):
    return 1 - v if bit else v


def ag_small(name, x, with_sum=False):
    R, C = x.shape

    def body(x_ref, out_ref, *rest):
        if with_sum:
            sum_ref, send_sems, recv_sems, lsem = rest
        else:
            send_sems, recv_sems, lsem = rest
        mx, my, mc = _me()
        me = 4 * mx + 2 * my + mc
        local = pltpu.make_async_copy(x_ref, out_ref.at[me], lsem)
        local.start()
        sends = []
        for k in range(1, NDEV):
            peer = (_flip(mx, k & 4), _flip(my, k & 2), _flip(mc, k & 1))
            cp = pltpu.make_async_remote_copy(src_ref=x_ref, dst_ref=out_ref.at[me], send_sem=send_sems.at[k - 1],
                                              recv_sem=recv_sems.at[k - 1], device_id=peer, device_id_type=MESH)
            cp.start()
            sends.append(cp)
        for k in range(1, NDEV):
            px, py, pc = _flip(mx, k & 4), _flip(my, k & 2), _flip(mc, k & 1)
            pltpu.make_async_remote_copy(src_ref=x_ref, dst_ref=out_ref.at[4 * px + 2 * py + pc],
                                         send_sem=send_sems.at[k - 1], recv_sem=recv_sems.at[k - 1],
                                         device_id=(px, py, pc), device_id_type=MESH).wait_recv()
        for cp in sends:
            cp.wait_send()
        local.wait()
        if with_sum:
            acc = out_ref[0]
            for d in range(1, NDEV):
                acc = acc + out_ref[d]
            sum_ref[...] = acc

    out_shape = [jax.ShapeDtypeStruct((NDEV, R, C), F32)]
    if with_sum:
        out_shape.append(jax.ShapeDtypeStruct((R, C), F32))
    vm = pl.BlockSpec(memory_space=pltpu.VMEM)
    res = pl.pallas_call(
        body, out_shape=out_shape, in_specs=[vm], out_specs=[vm] * len(out_shape),
        scratch_shapes=[pltpu.SemaphoreType.DMA((NDEV - 1,)), pltpu.SemaphoreType.DMA((NDEV - 1,)),
                        pltpu.SemaphoreType.DMA],
        compiler_params=pltpu.CompilerParams(vmem_limit_bytes=VMEM_LIMIT), name=name)(x)
    return res if with_sum else res[0]


def ag_big(name, shards):
    n = len(shards)

    def body(*refs):
        ins, outs = refs[:n], refs[n:2 * n]
        send_sems, recv_sems, lsems = refs[2 * n:]
        mx, my, mc = _me()
        me = (mx, my, mc)
        sibling = (mx, my, 1 - mc)
        chips = [(1 - mx, my), (mx, 1 - my), (1 - mx, 1 - my)]

        def idx(p):
            return 4 * p[0] + 2 * p[1] + p[2]

        def copy(t, k, block, to, src=None):
            dst = outs[t].at[idx(block)]
            return pltpu.make_async_remote_copy(
                src_ref=dst if src is None else src, dst_ref=dst, send_sem=send_sems.at[7 * t + k],
                recv_sem=recv_sems.at[7 * t + k], device_id=to, device_id_type=MESH)

        started = []
        locals_ = []
        for t in range(n):
            mine = pltpu.make_async_copy(ins[t], outs[t].at[idx(me)], lsems.at[t])
            mine.start()
            locals_.append(mine)
            first = [copy(t, 0, me, sibling, src=ins[t])]
            first += [copy(t, 1 + j, me, (*chip, mc), src=ins[t]) for j, chip in enumerate(chips)]
            for cp in first:
                cp.start()
            started += first
        for t in range(n):
            for j, chip in enumerate(chips):
                copy(t, 1 + j, (*chip, mc), me).wait_recv()
                fwd = copy(t, 4 + j, (*chip, mc), sibling)
                fwd.start()
                started.append(fwd)
        for t in range(n):
            copy(t, 0, sibling, me).wait_recv()
            for j, chip in enumerate(chips):
                copy(t, 4 + j, (*chip, 1 - mc), me).wait_recv()
        for cp in started:
            cp.wait_send()
        for mine in locals_:
            mine.wait()

    anyspec = pl.BlockSpec(memory_space=pl.ANY)
    return pl.pallas_call(
        body, out_shape=[jax.ShapeDtypeStruct((NDEV,) + s.shape, s.dtype) for s in shards],
        in_specs=[anyspec] * n, out_specs=[anyspec] * n,
        scratch_shapes=[pltpu.SemaphoreType.DMA((7 * n,)), pltpu.SemaphoreType.DMA((7 * n,)),
                        pltpu.SemaphoreType.DMA((n,))],
        name=name)(*shards)


def _idx(p):
    return 4 * p[0] + 2 * p[1] + p[2]


def _remote(src, dst, ss, rs, k, to):
    return pltpu.make_async_remote_copy(src_ref=src, dst_ref=dst, send_sem=ss.at[k], recv_sem=rs.at[k],
                                        device_id=to, device_id_type=MESH)


def ex_ag_chips(shards):
    n = len(shards)

    def copies(ci, co, ss, rs, base):
        mx, my, mc = _me()
        me = (mx, my, mc)
        peers = [(mx, my, 1 - mc), (1 - mx, my, mc), (mx, 1 - my, mc), (1 - mx, 1 - my, mc)]
        sends, recvs, local = [], [], []
        for t in range(n):
            b = base + 5 * t
            for k, peer in enumerate(peers):
                sends.append(_remote(ci[t], co[t].at[_idx(me)], ss, rs, b + k, peer))
                recvs.append(_remote(ci[t], co[t].at[_idx(peer)], ss, rs, b + k, peer))
            local.append(pltpu.make_async_copy(ci[t], co[t].at[_idx(me)], ss.at[b + 4]))
        return sends, recvs, local

    def start(ci, co, ss, rs, base):
        sends, _, local = copies(ci, co, ss, rs, base)
        for cp in local + sends:
            cp.start()

    def finish(ci, co, ss, rs, base):
        sends, recvs, local = copies(ci, co, ss, rs, base)
        for cp in recvs:
            cp.wait_recv()
        for cp in sends:
            cp.wait_send()
        for cp in local:
            cp.wait()

    outs = [jax.ShapeDtypeStruct((NDEV,) + s.shape, s.dtype) for s in shards]
    return Exchange(shards, outs, {}, 5 * n, start, finish)


def ex_ag_sibling(bufs):
    n = len(bufs)

    def copies(co, ss, rs, base):
        mx, my, mc = _me()
        sibling = (mx, my, 1 - mc)
        chips = [(1 - mx, my), (mx, 1 - my), (1 - mx, 1 - my)]
        sends, recvs = [], []
        for t in range(n):
            for j, chip in enumerate(chips):
                mine, theirs = co[t].at[_idx((*chip, mc))], co[t].at[_idx((*chip, 1 - mc))]
                sends.append(_remote(mine, mine, ss, rs, base + 3 * t + j, sibling))
                recvs.append(_remote(mine, theirs, ss, rs, base + 3 * t + j, sibling))
        return sends, recvs

    def start(ci, co, ss, rs, base):
        for cp in copies(co, ss, rs, base)[0]:
            cp.start()

    def finish(ci, co, ss, rs, base):
        sends, recvs = copies(co, ss, rs, base)
        for cp in recvs:
            cp.wait_recv()
        for cp in sends:
            cp.wait_send()

    outs = [jax.ShapeDtypeStruct(b.shape, b.dtype) for b in bufs]
    return Exchange(bufs, outs, {t: t for t in range(n)}, 3 * n, start, finish)


def ex_rs_sibling(grads):
    n = len(grads)

    def copies(ci, co, ss, rs, base):
        mx, my, mc = _me()
        return [_remote(ci[t].at[pl.ds((1 - mc) * 4, 4)], co[t], ss, rs, base + t, (mx, my, 1 - mc)) for t in range(n)]

    def start(ci, co, ss, rs, base):
        for cp in copies(ci, co, ss, rs, base):
            cp.start()

    def finish(ci, co, ss, rs, base):
        for cp in copies(ci, co, ss, rs, base):
            cp.wait()

    outs = [jax.ShapeDtypeStruct((4,) + g.shape[1:], g.dtype) for g in grads]
    return Exchange(grads, outs, {}, n, start, finish)


def ex_rs_chips(parts):
    n = len(parts)

    def copies(ci, co, ss, rs, base):
        mx, my, mc = _me()
        cps = []
        for t in range(n):
            for k in range(1, 4):
                px, py = _flip(mx, k & 2), _flip(my, k & 1)
                cps.append(_remote(ci[t].at[2 * px + py], co[t].at[k - 1], ss, rs, base + 3 * t + k - 1, (px, py, mc)))
        return cps

    def start(ci, co, ss, rs, base):
        for cp in copies(ci, co, ss, rs, base):
            cp.start()

    def finish(ci, co, ss, rs, base):
        for cp in copies(ci, co, ss, rs, base):
            cp.wait()

    outs = [jax.ShapeDtypeStruct((3,) + p.shape[1:], p.dtype) for p in parts]
    return Exchange(parts, outs, {}, 3 * n, start, finish)


def run_exchanges(name, xs):
    x = merge_exchanges(xs)
    n_ci, n_co = len(x.ins), len(x.out_shapes)

    def body(*refs):
        ci, co = refs[:n_ci], refs[n_ci:n_ci + n_co]
        ss, rs = refs[n_ci + n_co:]
        x.start(ci, co, ss, rs, 0)
        x.finish(ci, co, ss, rs, 0)

    hbm = pl.BlockSpec(memory_space=pl.ANY)
    return pl.pallas_call(
        body, out_shape=x.out_shapes, in_specs=[hbm] * n_ci, out_specs=[hbm] * n_co, input_output_aliases=x.aliases,
        scratch_shapes=[pltpu.SemaphoreType.DMA((x.n_sems,)), pltpu.SemaphoreType.DMA((x.n_sems,))], name=name)(*x.ins)


def _rope_tables(S, nctx):
    t = jnp.arange(S)
    row = (t // GRID_W).astype(F32)
    col = (t % GRID_W).astype(F32)
    pairs = HEAD // 4
    inv = ROPE_THETA ** (-jnp.arange(pairs, dtype=F32) / pairs)
    ang_r = row[:, None] * inv
    ang_c = col[:, None] * inv
    ang = jnp.concatenate([ang_r, ang_r, ang_c, ang_c], axis=-1)
    cos = jnp.concatenate([jnp.ones((nctx, HEAD), F32), jnp.cos(ang)], axis=0)
    sin = jnp.concatenate([jnp.zeros((nctx, HEAD), F32), jnp.sin(ang)], axis=0)
    lane = jnp.arange(HEAD)[None, :]
    first = (lane & 32) == 0
    return cos, jnp.where(first, -sin, 0.0), jnp.where(first, 0.0, sin)


def _pad_rows(v, rows):
    v = v.reshape(-1).astype(F32)
    return jnp.pad(v, (0, rows * 128 - v.shape[0])).reshape(rows, 128)


def _rows8(n):
    return -(-n // 1024) * 8


def kernel(x, c, ctx, c_ctx, ada_w, ada_b, norm_w, mlp_w1, mlp_w2, ev_w_in, ev_w_out, ev_q_norm, ev_k_norm, ev_sink, od_w_in, od_w_out, od_rpb, final_norm_w, loss_target, m_c_ctx, m_ada_w, m_ada_b, m_norm_w, m_mlp_w1, m_mlp_w2, m_ev_w_in, m_ev_w_out, m_ev_q_norm, m_ev_k_norm, m_ev_sink, m_od_w_in, m_od_w_out, m_od_rpb, m_final_norm_w, v_c_ctx, v_ada_w, v_ada_b, v_norm_w, v_mlp_w1, v_mlp_w2, v_ev_w_in, v_ev_w_out, v_ev_q_norm, v_ev_k_norm, v_ev_sink, v_od_w_in, v_od_w_out, v_od_rpb, v_final_norm_w):
    S, D = x.shape[1], x.shape[2]
    NC = ctx.shape[1]
    T = NC + S
    assert NC == ROW_TILE and S % GRID_W == 0
    ada_cols = ada_w.shape[2]
    nw_cols = norm_w.shape[2]
    me = 4 * lax.axis_index("x") + 2 * lax.axis_index("y") + lax.axis_index("c")

    pack1 = jnp.concatenate([_pad_rows(c, _rows8(D)), _pad_rows(norm_w, _rows8(4 * nw_cols))], axis=0)
    g1 = ag_small("ag_c_normw", pack1)
    c_all = g1[:, :D // 128].reshape(NDEV, D)
    nw_rows = _rows8(D)
    nw = g1[:, nw_rows:nw_rows + 4 * nw_cols // 128].reshape(NDEV, 2, 2, nw_cols)
    nw = nw.transpose(1, 2, 0, 3).reshape(2, 2, D)
    cin = jnp.concatenate([c_all, jnp.broadcast_to(c_ctx[None], (NDEV, D))], axis=0)
    act = _vmem_call("silu_c", lambda v: _silu(v).astype(BF16), jax.ShapeDtypeStruct((2 * NDEV, D), BF16), cin)
    ada_b_loc = lax.dynamic_slice_in_dim(ada_b, me * ada_cols, ada_cols, axis=1)
    mods = [mm_nn(f"mod{i}", act, ada_w[i], _epi_bias, [F32], extras=(ada_b_loc[i:i + 1],), extra_kinds=('n',))[0]
            for i in range(2)]
    gm = ag_small("ag_mod", jnp.concatenate(mods, axis=1))
    gm = gm.reshape(NDEV, 2 * NDEV, 2, ada_cols).transpose(2, 1, 0, 3).reshape(2, 2 * NDEV, NDEV * ada_cols)
    mod_lat = lax.dynamic_index_in_dim(gm, me, axis=1, keepdims=False)
    mod_ctx = gm[:, NDEV]
    mod2 = jnp.stack([mod_ctx, mod_lat], axis=1).reshape(2, 2, 6, D)

    def chunk(i, j):
        return mod2[i, :, j, :]

    def b16(w):
        return w.astype(BF16)

    w_in_e, w_out_e = ag_big("ag_weights_l0_attn", [b16(ev_w_in[0]), b16(ev_w_out[0])])
    w_out_e = w_out_e.reshape(-1, D)

    cos, sa, sb = _rope_tables(S, NC)
    bias8 = na_span_bias(na_bias_table(od_rpb[0]))
    sink = ev_sink[0]
    TQ_F, TQ_B = 256, 256

    X0 = jnp.concatenate([ctx[0], x[0]], axis=0)
    h_a = norm_mod("l0_norm1", X0, nw[0, 0][None], chunk(0, 0), chunk(0, 1), NC)
    qkv0 = mm_nn("l0_qkv", h_a, w_in_e, _epi_store(F32), [F32])[0]
    qkvh0 = prep_even("l0_prep", qkv0, ev_q_norm, ev_k_norm, cos, sa, sb)
    o0, mlp0_half = attn_even_fwd("l0_attn", qkvh0, sink, NC, TQ_F,
                                  carry=[ex_ag_chips([b16(mlp_w1[0]), b16(mlp_w2[0])])])
    tm0 = _tile(T, 1100)
    (X1, y0), (w1_0, w2_0) = mm_nn("l0_out", o0, w_out_e, _epi_resid_gate(NC, tm0), [F32, F32],
                                   extras=(X0, chunk(0, 2)), extra_kinds=('mn', 'n'),
                                   carry=[ex_ag_sibling(mlp0_half)])
    h_b = norm_mod("l0_norm2", X1, nw[0, 1][None], chunk(0, 3), chunk(0, 4), NC)
    (a0, r0), od_half = mm_nn("l0_up", h_b, w1_0, _epi_relu2, [BF16, BF16],
                              carry=[ex_ag_chips([b16(od_w_in[0]), b16(od_w_out[0])])])
    (X2, z0), (w1_1_half, w_in_o, w_out_o) = mm_nn(
        "l0_down", a0, w2_0.reshape(-1, D), _epi_resid_gate(NC, tm0), [F32, F32], extras=(X1, chunk(0, 5)),
        extra_kinds=('mn', 'n'), carry=[ex_ag_chips([b16(mlp_w1[1])]), ex_ag_sibling(od_half)])
    w_out_o = w_out_o.reshape(-1, D)

    h_c = norm_mod("l1_norm1", X2, nw[1, 0][None], chunk(1, 0), chunk(1, 1), NC)
    (qkv1,), (w2_1_half, w1_1) = mm_nn("l1_qkv", h_c, w_in_o, _epi_store(BF16), [BF16],
                                        carry=[ex_ag_chips([b16(mlp_w2[1])]), ex_ag_sibling([w1_1_half])])
    o1, (w2_1,) = attn_odd_fwd("l1_attn", qkv1, bias8, NC, carry=[ex_ag_sibling([w2_1_half])])
    X2l = X2[NC:]
    tm1 = _tile(S, 1100)
    X3, y1 = mm_nn("l1_out", o1, w_out_o, _epi_resid_gate(0, tm1), [F32, F32], extras=(X2l, chunk(1, 2)),
                   extra_kinds=('mn', 'n'))
    h_d = norm_mod("l1_norm2", X3, nw[1, 1][None], chunk(1, 3), chunk(1, 4), 0)
    a1, r1 = mm_nn("l1_up", h_d, w1_1, _epi_relu2, [BF16, BF16])
    X4, z1 = mm_nn("l1_down", a1, w2_1.reshape(-1, D), _epi_resid_gate(0, tm1), [F32, F32], extras=(X3, chunk(1, 5)),
                   extra_kinds=('mn', 'n'))
    dX4, loss_p, dfw_p = final_loss("final_loss", X4, final_norm_w[None], loss_target[0])
    w_in = [w_in_e, w_in_o]
    w_out = [w_out_e, w_out_o]
    w1 = [w1_0, w1_1]
    w2 = [w2_0.reshape(-1, D), w2_1.reshape(-1, D)]

    mc4 = (lax.axis_index("c") * 4).astype(jnp.int32)
    my_chip = (2 * lax.axis_index("x") + lax.axis_index("y")).astype(jnp.int32)

    def chip_sum(tag, g8, sib4):
        return rs_chip_sum(f"rs_chip_sum_{tag}", g8, sib4, mc4[None])

    dz1, pg2_1 = gate_bwd("l1_gate2_bwd", dX4, z1, chunk(1, 5), 0)
    du1 = mm_nt("l1_down_dx", dz1, w2[1], _epi_mul2r, BF16, extras=(r1,))
    g_w2_1 = mm_tn("l1_down_dw", a1, dz1, 0)
    dh_d, (sib_w2_1,) = mm_nt("l1_up_dx", du1, w1[1], _epi_store(F32), F32, carry=[ex_rs_sibling([g_w2_1])])
    g_w1_1, (rem_w2_1,) = mm_tn("l1_up_dw", h_d, du1, 1,
                                carry=[ex_rs_chips([chip_sum("w2_1", g_w2_1, sib_w2_1)])])
    dX3, pn2_1 = norm_bwd("l1_norm2_bwd", X3, dh_d, dX4, nw[1, 1][None], chunk(1, 4), 0)
    dy1, pg1_1 = gate_bwd("l1_gate1_bwd", dX3, y1, chunk(1, 2), 0)
    do1 = mm_nt("l1_out_dx", dy1, w_out[1], _epi_store(BF16), BF16)
    g_wout_1, (sib_w1_1,) = mm_tn("l1_out_dw", o1, dy1, 0, carry=[ex_rs_sibling([g_w1_1])])
    (dq1, dk1, dv1, dbias8), (rem_w1_1, sib_wout_1) = attn_odd_bwd(
        "l1_attn_bwd", qkv1, bias8, do1, NC,
        carry=[ex_rs_chips([chip_sum("w1_1", g_w1_1, sib_w1_1)]), ex_rs_sibling([g_wout_1])])
    dqkv1 = jnp.concatenate([jnp.pad(dq1, ((NC, 0), (0, 0))), dk1, dv1], axis=1).astype(BF16)
    dh_c, (rem_wout_1,) = mm_nt("l1_qkv_dx", dqkv1, w_in[1], _epi_store(F32), F32,
                                carry=[ex_rs_chips([chip_sum("wout_1", g_wout_1, sib_wout_1)])])
    g_win_1 = mm_tn("l1_qkv_dw", h_c, dqkv1, 1)
    dX2, pn1_1 = norm_bwd("l1_norm1_bwd", X2, dh_c, dX3, nw[1, 0][None], chunk(1, 1), NC, dres_skip=NC)
    d_rpb = na_bias_grad("rpb_grad", na_span_bias_grad(dbias8))

    dz0, pg2_0 = gate_bwd("l0_gate2_bwd", dX2, z0, chunk(0, 5), NC)
    du0, (sib_win_1,) = mm_nt("l0_down_dx", dz0, w2[0], _epi_mul2r, BF16, extras=(r0,),
                              carry=[ex_rs_sibling([g_win_1])])
    g_w2_0, (rem_win_1,) = mm_tn("l0_down_dw", a0, dz0, 0,
                                 carry=[ex_rs_chips([chip_sum("win_1", g_win_1, sib_win_1)])])
    dh_b, (sib_w2_0,) = mm_nt("l0_up_dx", du0, w1[0], _epi_store(F32), F32, carry=[ex_rs_sibling([g_w2_0])])
    g_w1_0, (rem_w2_0,) = mm_tn("l0_up_dw", h_b, du0, 1,
                                carry=[ex_rs_chips([chip_sum("w2_0", g_w2_0, sib_w2_0)])])
    dX1, pn2_0 = norm_bwd("l0_norm2_bwd", X1, dh_b, dX2, nw[0, 1][None], chunk(0, 4), NC)
    dy0, pg1_0 = gate_bwd("l0_gate1_bwd", dX1, y0, chunk(0, 2), NC)
    do0 = mm_nt("l0_out_dx", dy0, w_out[0], _epi_store(BF16), BF16)
    g_wout_0, (sib_w1_0,) = mm_tn("l0_out_dw", o0, dy0, 0, carry=[ex_rs_sibling([g_w1_0])])
    (dq0, dk0, dv0, dsink_p), (rem_w1_0, sib_wout_0) = attn_even_bwd(
        "l0_attn_bwd", qkvh0, sink, do0, NC, TQ_B,
        carry=[ex_rs_chips([chip_sum("w1_0", g_w1_0, sib_w1_0)]), ex_rs_sibling([g_wout_0])])
    dqkv0, pqk = prep_even_bwd("l0_prep_bwd", qkv0, dq0, dk0, dv0, ev_q_norm, ev_k_norm, cos, sa, sb)
    dh_a, (rem_wout_0,) = mm_nt("l0_qkv_dx", dqkv0, w_in[0], _epi_store(F32), F32,
                                carry=[ex_rs_chips([chip_sum("wout_0", g_wout_0, sib_wout_0)])])
    g_win_0 = mm_tn("l0_qkv_dw", h_a, dqkv0, 1)
    dx_lat, pn1_0 = norm_bwd("l0_norm1_bwd", X0, dh_a, dX1, nw[0, 0][None], chunk(0, 1), NC, out_skip=NC)
    grad_x = dx_lat[None]
    (sib_win_0,) = run_exchanges("rs_sibling_last", [ex_rs_sibling([g_win_0])])
    (rem_win_0,) = run_exchanges("rs_chips_last", [ex_rs_chips([chip_sum("win_0", g_win_0, sib_win_0)])])

    def dmod(grp, pn1, pg1, pn2, pg2):
        return jnp.concatenate([pn1[grp], pn1[2 + grp], pg1[grp], pn2[grp], pn2[2 + grp], pg2[grp]])

    dmod_lat = jnp.stack([dmod(1, pn1_0, pg1_0, pn2_0, pg2_0), dmod(1, pn1_1, pg1_1, pn2_1, pg2_1)])
    dmod_ctx = jnp.stack([dmod(0, pn1_0, pg1_0, pn2_0, pg2_0), dmod(0, pn1_1, pg1_1, pn2_1, pg2_1)])
    dnw_p = jnp.stack([pn1_0[4], pn2_0[4], pn1_1[4], pn2_1[4]])
    pieces = [dmod_lat, dmod_ctx, dnw_p, pqk[0], pqk[1], dsink_p[8:, 0, 0], d_rpb, dfw_p[0], loss_p[0, 0]]
    sizes = [int(np.prod(p.shape)) for p in pieces]
    rows = [_rows8(s) for s in sizes]
    pack2 = jnp.concatenate([_pad_rows(p, r) for p, r in zip(pieces, rows)], axis=0)
    g2, s2 = ag_small("ag_small_grads", pack2, with_sum=True)
    offs = np.concatenate([[0], np.cumsum(rows)])

    def piece(arr, i, shape):
        return arr[..., offs[i]:offs[i + 1], :].reshape(arr.shape[:-2] + (-1,))[..., :sizes[i]].reshape(
            arr.shape[:-2] + shape)

    dmod_all = piece(g2, 0, (2, 6 * D))
    dmodc_sum = piece(s2, 1, (2, 6 * D))
    dnw_sum = piece(s2, 2, (2, 2, D))
    g_qn = piece(s2, 3, ev_q_norm.shape)
    g_kn = piece(s2, 4, ev_k_norm.shape)
    g_sink = piece(s2, 5, ev_sink.shape)
    g_rpb = piece(s2, 6, od_rpb.shape)
    g_fw = piece(s2, 7, final_norm_w.shape)
    loss = piece(s2, 8, ())

    dm16 = jnp.concatenate([dmod_all.transpose(1, 0, 2), dmodc_sum[:, None, :],
                            jnp.zeros((2, NDEV - 1, 6 * D), F32)], axis=1)
    dm16_loc = lax.dynamic_slice_in_dim(dm16.reshape(2, 2 * NDEV, NDEV, ada_cols), me, 1, axis=2)[:, :, 0, :]
    g_ada_b = _vmem_call("ada_b_grad", lambda v: jnp.sum(v, axis=1),
                         jax.ShapeDtypeStruct((2, 6 * D), F32), dm16)
    g_ada_w = []
    dact_p = None
    for i in range(2):
        dmb = dm16_loc[i].astype(BF16)
        g_ada_w.append(mm_tn(f"ada_w_grad{i}", act, dmb, None))
        part = mm_nt(f"ada_dact{i}", dmb, ada_w[i], _epi_store(F32), F32)
        dact_p = part if dact_p is None else dact_p + part
    _, dact = ag_small("ag_cctx", dact_p, with_sum=True)

    def cctx_grad(da, cc):
        sg = 1.0 / (1.0 + jnp.exp(-cc))
        return da[NDEV:NDEV + 1] * (sg * (1.0 + cc * (1.0 - sg)))

    g_cctx = _vmem_call("cctx_grad", cctx_grad, jax.ShapeDtypeStruct((1, D), F32), dact, c_ctx[None])[0]

    grads = [g_win_0, g_wout_0, g_w1_0, g_w2_0, g_win_1, g_wout_1, g_w1_1, g_w2_1]
    sib = [sib_win_0, sib_wout_0, sib_w1_0, sib_w2_0, sib_win_1, sib_wout_1, sib_w1_1, sib_w2_1]
    rem = [rem_win_0, rem_wout_0, rem_w1_0, rem_w2_0, rem_win_1, rem_wout_1, rem_w1_1, rem_w2_1]
    own_idx = jnp.stack([mc4 + my_chip, my_chip])

    def big(tag, w, m, v, ts):
        res = None
        for l, t in enumerate(ts):
            res = adamw_rs(f"adamw_{tag}_{l}", w, grads[t], sib[t], rem[t], m, v, own_idx, l, res)
        return tuple(res)

    r_ev_w_in = big('ev_w_in', ev_w_in, m_ev_w_in, v_ev_w_in, [0])
    r_ev_w_out = big('ev_w_out', ev_w_out, m_ev_w_out, v_ev_w_out, [1])
    r_mlp_w1 = big('mlp_w1', mlp_w1, m_mlp_w1, v_mlp_w1, [2, 6])
    r_mlp_w2 = big('mlp_w2', mlp_w2, m_mlp_w2, v_mlp_w2, [3, 7])
    r_od_w_in = big('od_w_in', od_w_in, m_od_w_in, v_od_w_in, [4])
    r_od_w_out = big('od_w_out', od_w_out, m_od_w_out, v_od_w_out, [5])

    g_ada = jnp.stack(g_ada_w)
    r_ada_w = adamw_rows("adamw_ada_w", ada_w.reshape(2 * D, ada_cols), g_ada.reshape(2 * D, ada_cols),
                         m_ada_w.reshape(2 * D, ada_cols), v_ada_w.reshape(2 * D, ada_cols))
    r_ada_w = tuple(u.reshape(2, D, ada_cols) for u in r_ada_w)

    g_nw_loc = lax.dynamic_slice_in_dim(dnw_sum, me * nw_cols, nw_cols, axis=2)
    small = [(c_ctx, g_cctx, m_c_ctx, v_c_ctx), (ada_b, g_ada_b, m_ada_b, v_ada_b),
             (norm_w, g_nw_loc, m_norm_w, v_norm_w), (ev_q_norm, g_qn, m_ev_q_norm, v_ev_q_norm),
             (ev_k_norm, g_kn, m_ev_k_norm, v_ev_k_norm), (ev_sink, g_sink, m_ev_sink, v_ev_sink),
             (od_rpb, g_rpb, m_od_rpb, v_od_rpb), (final_norm_w, g_fw, m_final_norm_w, v_final_norm_w)]
    srows = [_rows8(int(np.prod(w.shape))) for w, _, _, _ in small]
    packs = [jnp.concatenate([_pad_rows(tup[k], r) for tup, r in zip(small, srows)], axis=0) for k in range(4)]
    sres = adamw_rows("adamw_small", *packs)
    soffs = np.concatenate([[0], np.cumsum(srows)])

    def unpack(arr, i):
        w = small[i][0]
        return arr[soffs[i]:soffs[i + 1]].reshape(-1)[:int(np.prod(w.shape))].reshape(w.shape)

    sm = [[unpack(sres[k], i) for i in range(len(small))] for k in range(4)]

    def outs(k):
        big_k = {'ada_w': r_ada_w[k], 'mlp_w1': r_mlp_w1[k], 'mlp_w2': r_mlp_w2[k], 'ev_w_in': r_ev_w_in[k],
                 'ev_w_out': r_ev_w_out[k], 'od_w_in': r_od_w_in[k], 'od_w_out': r_od_w_out[k]}
        return (sm[k][0], big_k['ada_w'], sm[k][1], sm[k][2], big_k['mlp_w1'], big_k['mlp_w2'], big_k['ev_w_in'],
                big_k['ev_w_out'], sm[k][3], sm[k][4], sm[k][5], big_k['od_w_in'], big_k['od_w_out'], sm[k][6],
                sm[k][7])

    return (loss, grad_x, *outs(0), *outs(1), *outs(2), *outs(3))
```

```python
import numpy as np
import jax
import jax.numpy as jnp
from jax import lax
from jax.experimental import pallas as pl
from jax.experimental.pallas import tpu as pltpu

F32 = jnp.float32
BF16 = jnp.bfloat16
MESH = pl.DeviceIdType.MESH

NDEV = 8
HEAD = 128
GRID_W = 64
NA_KH, NA_KW = 8, 16
WINDOW = 128
ROPE_THETA = 10000.0
EPS = 1e-6
NEG = -1e30
SCALE = HEAD ** -0.5
ROW_TILE = 256
VMEM_LIMIT = 56 * 1024 * 1024

ADAM_LR, ADAM_B1, ADAM_B2, ADAM_EPS, ADAM_WD, ADAM_STEP = 0.001, 0.9, 0.999, 1e-08, 0.01, 10

NT = (((1,), (1,)), ((), ()))
NN = (((1,), (0,)), ((), ()))
TN = (((0,), (0,)), ((), ()))


def _cparams(sem):
    return pltpu.CompilerParams(dimension_semantics=sem, vmem_limit_bytes=VMEM_LIMIT)


def _tile(n, cap):
    if n <= cap:
        return n
    t = cap - cap % 64
    while t >= 64:
        if n % t == 0:
            return t
        t -= 64
    raise ValueError((n, cap))


def _dot(a, b, dims):
    return lax.dot_general(a.astype(BF16), b.astype(BF16), dims, preferred_element_type=F32)


def _slot(d):
    return (d % 2) * 4 + d // 2


class Exchange:
    def __init__(self, ins, out_shapes, aliases, n_sems, start, finish):
        self.ins, self.out_shapes, self.aliases, self.n_sems = list(ins), list(out_shapes), dict(aliases), n_sems
        self.start, self.finish = start, finish


def merge_exchanges(xs):
    ins, outs, aliases, bases, n = [], [], {}, [], 0
    for x in xs:
        bases.append((len(ins), len(outs), n))
        aliases.update({len(ins) + i: len(outs) + o for i, o in x.aliases.items()})
        ins += x.ins
        outs += x.out_shapes
        n += x.n_sems

    def run(which):
        def f(ci, co, ss, rs, base):
            for x, (i0, o0, s0) in zip(xs, bases):
                getattr(x, which)(ci[i0:i0 + len(x.ins)], co[o0:o0 + len(x.out_shapes)], ss, rs, base + s0)
        return f

    return Exchange(ins, outs, aliases, n, run('start'), run('finish'))


def _call(name, body, grid, ins, in_specs, out_shape, out_specs, scratch, sems, carry=None):
    if not carry:
        return pl.pallas_call(body, grid=grid, in_specs=in_specs, out_specs=out_specs, out_shape=out_shape,
                              scratch_shapes=scratch, compiler_params=_cparams(sems), name=name)(*ins)
    x = merge_exchanges(carry)
    n_in, n_ci, n_out, n_co, n_sc = len(ins), len(x.ins), len(out_shape), len(x.out_shapes), len(scratch)

    def wrapped(*refs):
        p = [0]

        def take(k):
            p[0] += k
            return refs[p[0] - k:p[0]]

        a, ci, o, co, sc = take(n_in), take(n_ci), take(n_out), take(n_co), take(n_sc)
        ss, rs = take(2)
        first = pl.program_id(0) == 0
        last = pl.program_id(0) == grid[0] - 1
        for d in range(1, len(grid)):
            first = jnp.logical_and(first, pl.program_id(d) == 0)
            last = jnp.logical_and(last, pl.program_id(d) == grid[d] - 1)

        @pl.when(first)
        def _():
            x.start(ci, co, ss, rs, 0)

        body(*a, *o, *sc)

        @pl.when(last)
        def _():
            x.finish(ci, co, ss, rs, 0)

    hbm = pl.BlockSpec(memory_space=pl.ANY)
    res = pl.pallas_call(
        wrapped, grid=grid, in_specs=list(in_specs) + [hbm] * n_ci, out_specs=list(out_specs) + [hbm] * n_co,
        out_shape=list(out_shape) + x.out_shapes,
        input_output_aliases={n_in + i: n_out + o for i, o in x.aliases.items()},
        scratch_shapes=list(scratch) + [pltpu.SemaphoreType.DMA((x.n_sems,)), pltpu.SemaphoreType.DMA((x.n_sems,))],
        compiler_params=_cparams(("arbitrary",) * len(grid)), name=name)(*ins, *x.ins)
    return list(res[:n_out]) + [list(res[n_out:])]


def _mm_core(name, grid, ins, in_specs, out_shape, out_specs, dims, acc_shape, epi, carry=None):
    nk = grid[2]
    n_extra = len(ins) - 2

    def body(*refs):
        a_ref, b_ref = refs[0], refs[1]
        ex = refs[2:2 + n_extra]
        outs = refs[2 + n_extra:-1]
        acc = refs[-1]
        k = pl.program_id(2)

        @pl.when(k == 0)
        def _():
            acc[...] = jnp.zeros_like(acc)

        acc[...] += _dot(a_ref[...], b_ref[...], dims)

        @pl.when(k == nk - 1)
        def _():
            epi(acc[...], ex, outs)

    return _call(name, body, grid, ins, in_specs, out_shape, out_specs, [pltpu.VMEM(acc_shape, F32)],
                 ("parallel", "parallel", "arbitrary"), carry)


def _split(res, n, carry):
    own = res[0] if n == 1 else list(res[:n])
    return (own, res[n]) if carry else own


def _epi_store(dtype):
    def epi(acc, ex, outs):
        outs[0][...] = acc.astype(dtype)
    return epi


def _epi_bias(acc, ex, outs):
    outs[0][...] = acc + ex[0][...]


def _epi_relu2(acc, ex, outs):
    r = jnp.maximum(acc, 0.0)
    outs[0][...] = (r * r).astype(BF16)
    outs[1][...] = r.astype(BF16)


def _epi_mul2r(acc, ex, outs):
    outs[0][...] = (acc * (2.0 * ex[0][...].astype(F32))).astype(BF16)


def _epi_resid_gate(nctx, tm):
    def epi(acc, ex, outs):
        rows = pl.program_id(0) * tm + lax.broadcasted_iota(jnp.int32, (tm, 1), 0)
        g = jnp.where(rows < nctx, ex[1][0:1, :], ex[1][1:2, :])
        outs[0][...] = ex[0][...] + g * acc
        outs[1][...] = acc
    return epi


def mm_nn(name, a, w, epi, outs, extras=(), extra_kinds=(), tm_cap=1100, tn_cap=512, tk_cap=2048, carry=None):
    M, K = a.shape
    if w.ndim == 3:
        ns = w.shape[2]
        N = NDEV * ns
        tn = _tile(ns, tn_cap)
        nper = ns // tn
    else:
        N = w.shape[1]
        tn = _tile(N, tn_cap)
    tm = _tile(M, tm_cap)
    tk = _tile(K, tk_cap)
    grid = (M // tm, N // tn, K // tk)
    a_spec = pl.BlockSpec((tm, tk), lambda i, j, k: (i, k))
    if w.ndim == 3:
        b_spec = pl.BlockSpec((None, tk, tn), lambda i, j, k: (j // nper, k, j % nper))
    else:
        b_spec = pl.BlockSpec((tk, tn), lambda i, j, k: (k, j))
    ex_specs = []
    for e, kind in zip(extras, extra_kinds):
        if kind == 'mn':
            ex_specs.append(pl.BlockSpec((tm, tn), lambda i, j, k: (i, j)))
        else:
            ex_specs.append(pl.BlockSpec((e.shape[0], tn), lambda i, j, k: (0, j)))
    out_shape = [jax.ShapeDtypeStruct((M, N), dt) for dt in outs]
    out_specs = [pl.BlockSpec((tm, tn), lambda i, j, k: (i, j)) for _ in outs]
    res = _mm_core(name, grid, (a, w, *extras), [a_spec, b_spec, *ex_specs], out_shape, out_specs, NN, (tm, tn), epi,
                   carry)
    return (list(res[:len(outs)]), res[len(outs)]) if carry else res


def mm_nt(name, a, w, epi, out_dtype, extras=(), tm_cap=1100, to_cap=1024, tc_cap=2048, carry=None):
    M, N = a.shape
    tm = _tile(M, tm_cap)
    if w.ndim == 3:
        Kw, ns = w.shape[1], w.shape[2]
        tc = _tile(ns, tc_cap)
        cper = ns // tc
    else:
        Kw = w.shape[0]
        tc = _tile(N, tc_cap)
    to = _tile(Kw, to_cap)
    grid = (M // tm, Kw // to, N // tc)
    a_spec = pl.BlockSpec((tm, tc), lambda i, j, k: (i, k))
    if w.ndim == 3:
        b_spec = pl.BlockSpec((None, to, tc), lambda i, j, k: (k // cper, j, k % cper))
    else:
        b_spec = pl.BlockSpec((to, tc), lambda i, j, k: (j, k))
    ex_specs = [pl.BlockSpec((tm, to), lambda i, j, k: (i, j)) for _ in extras]
    out_shape = [jax.ShapeDtypeStruct((M, Kw), out_dtype)]
    out_specs = [pl.BlockSpec((tm, to), lambda i, j, k: (i, j))]
    return _split(_mm_core(name, grid, (a, w, *extras), [a_spec, b_spec, *ex_specs], out_shape, out_specs, NT, (tm, to),
                           epi, carry), 1, carry)


def mm_tn(name, a, b, shard_axis, to_cap=1024, tn_cap=1024, tc_cap=1100, carry=None):
    M, Ka = a.shape
    N = b.shape[1]
    tc = _tile(M, tc_cap)
    if shard_axis is None:
        to, tn = _tile(Ka, to_cap), _tile(N, tn_cap)
        shape = (Ka, N)
        oblk = (to, tn)
        omap = lambda i, j, k: (i, j)
    elif shard_axis == 1:
        ns = N // NDEV
        to, tn = _tile(Ka, to_cap), _tile(ns, tn_cap)
        per = ns // tn
        shape = (NDEV, Ka, ns)
        oblk = (None, to, tn)
        omap = lambda i, j, k: (_slot(j // per), i, j % per)
    else:
        rs = Ka // NDEV
        to, tn = _tile(rs, to_cap), _tile(N, tn_cap)
        per = rs // to
        shape = (NDEV, rs, N)
        oblk = (None, to, tn)
        omap = lambda i, j, k: (_slot(i // per), i % per, j)
    grid = (Ka // to, N // tn, M // tc)
    a_spec = pl.BlockSpec((tc, to), lambda i, j, k: (k, i))
    b_spec = pl.BlockSpec((tc, tn), lambda i, j, k: (k, j))
    out_shape = [jax.ShapeDtypeStruct(shape, F32)]
    out_specs = [pl.BlockSpec(oblk, omap)]
    return _split(_mm_core(name, grid, (a, b), [a_spec, b_spec], out_shape, out_specs, TN, (to, tn), _epi_store(F32),
                           carry), 1, carry)


def _row_spec(D):
    return pl.BlockSpec((ROW_TILE, D), lambda i: (i, 0))


def _const_spec(r, D):
    return pl.BlockSpec((r, D), lambda i: (0, 0))


def _grp(ref, is_ctx):
    return jnp.where(is_ctx, ref[0:1, :], ref[1:2, :])


def norm_mod(name, x, nw, sh, sc, nctx):
    R, D = x.shape
    assert R % ROW_TILE == 0 and nctx % ROW_TILE == 0

    def body(x_ref, nw_ref, sh_ref, sc_ref, o_ref):
        is_ctx = pl.program_id(0) * ROW_TILE < nctx
        xv = x_ref[...]
        rstd = lax.rsqrt(jnp.mean(xv * xv, axis=-1, keepdims=True) + EPS)
        n = xv * rstd * nw_ref[...]
        o_ref[...] = (n * (1.0 + _grp(sc_ref, is_ctx)) + _grp(sh_ref, is_ctx)).astype(BF16)

    return pl.pallas_call(
        body, grid=(R // ROW_TILE,),
        in_specs=[_row_spec(D), _const_spec(1, D), _const_spec(2, D), _const_spec(2, D)],
        out_specs=_row_spec(D), out_shape=jax.ShapeDtypeStruct((R, D), BF16),
        compiler_params=_cparams(("parallel",)), name=name)(x, nw, sh, sc)


def norm_bwd(name, x, dh, dres, nw, sc, nctx, dres_skip=0, out_skip=0, carry=None):
    R, D = x.shape
    assert R % ROW_TILE == 0 and nctx % ROW_TILE == 0 and dres_skip % ROW_TILE == 0 and out_skip % ROW_TILE == 0
    res_tiles, out_tiles = dres_skip // ROW_TILE, out_skip // ROW_TILE

    def body(x_ref, dh_ref, dres_ref, nw_ref, sc_ref, dx_ref, part_ref):
        i = pl.program_id(0)
        is_ctx = i * ROW_TILE < nctx

        @pl.when(i == 0)
        def _():
            part_ref[...] = jnp.zeros_like(part_ref)

        xv = x_ref[...]
        dhv = dh_ref[...]
        w = nw_ref[...]
        rstd = lax.rsqrt(jnp.mean(xv * xv, axis=-1, keepdims=True) + EPS)
        xhat = xv * rstd
        n = xhat * w
        dn = dhv * (1.0 + _grp(sc_ref, is_ctx))
        dxhat = dn * w
        dres = dres_ref[...]
        if res_tiles:
            dres = jnp.where(i < res_tiles, 0.0, dres)
        dx_ref[...] = dres + rstd * (dxhat - xhat * jnp.mean(dxhat * xhat, axis=-1, keepdims=True))
        s_sh = jnp.sum(dhv, axis=0, keepdims=True)
        s_sc = jnp.sum(dhv * n, axis=0, keepdims=True)
        s_nw = jnp.sum(dn * xhat, axis=0, keepdims=True)
        zero = jnp.zeros_like(s_sh)
        part_ref[0:1, :] += jnp.where(is_ctx, s_sh, zero)
        part_ref[1:2, :] += jnp.where(is_ctx, zero, s_sh)
        part_ref[2:3, :] += jnp.where(is_ctx, s_sc, zero)
        part_ref[3:4, :] += jnp.where(is_ctx, zero, s_sc)
        part_ref[4:5, :] += s_nw

    res = _call(name, body, (R // ROW_TILE,), (x, dh, dres, nw, sc),
                [_row_spec(D), _row_spec(D),
                 pl.BlockSpec((ROW_TILE, D), lambda i: (jnp.maximum(i - res_tiles, 0), 0)),
                 _const_spec(1, D), _const_spec(2, D)],
                [jax.ShapeDtypeStruct((R - out_skip, D), F32), jax.ShapeDtypeStruct((8, D), F32)],
                [pl.BlockSpec((ROW_TILE, D), lambda i: (jnp.maximum(i - out_tiles, 0), 0)), _const_spec(8, D)],
                [], ("arbitrary",), carry)
    return _split(res, 2, carry)


def gate_bwd(name, dx, y, g, nctx):
    R, D = dx.shape
    assert R % ROW_TILE == 0 and nctx % ROW_TILE == 0

    def body(dx_ref, y_ref, g_ref, dy_ref, part_ref):
        i = pl.program_id(0)
        is_ctx = i * ROW_TILE < nctx

        @pl.when(i == 0)
        def _():
            part_ref[...] = jnp.zeros_like(part_ref)

        dxv = dx_ref[...]
        dy_ref[...] = (dxv * _grp(g_ref, is_ctx)).astype(BF16)
        s = jnp.sum(dxv * y_ref[...], axis=0, keepdims=True)
        zero = jnp.zeros_like(s)
        part_ref[0:1, :] += jnp.where(is_ctx, s, zero)
        part_ref[1:2, :] += jnp.where(is_ctx, zero, s)

    return pl.pallas_call(
        body, grid=(R // ROW_TILE,),
        in_specs=[_row_spec(D), _row_spec(D), _const_spec(2, D)],
        out_specs=[_row_spec(D), _const_spec(8, D)],
        out_shape=[jax.ShapeDtypeStruct((R, D), BF16), jax.ShapeDtypeStruct((8, D), F32)],
        compiler_params=_cparams(("arbitrary",)), name=name)(dx, y, g)


def final_loss(name, x, fw, tgt):
    S, D = x.shape

    def body(x_ref, fw_ref, t_ref, dx_ref, loss_ref, dfw_ref):
        i = pl.program_id(0)

        @pl.when(i == 0)
        def _():
            loss_ref[...] = jnp.zeros_like(loss_ref)
            dfw_ref[...] = jnp.zeros_like(dfw_ref)

        xv = x_ref[...]
        w = fw_ref[...]
        rstd = lax.rsqrt(jnp.mean(xv * xv, axis=-1, keepdims=True) + EPS)
        xhat = xv * rstd
        e = xhat * w - t_ref[...]
        loss_ref[...] += 0.5 * jnp.sum(jnp.mean(e * e, axis=-1, keepdims=True))
        dout = e * (1.0 / D)
        dfw_ref[0:1, :] += jnp.sum(dout * xhat, axis=0, keepdims=True)
        dxhat = dout * w
        dx_ref[...] = rstd * (dxhat - xhat * jnp.mean(dxhat * xhat, axis=-1, keepdims=True))

    return pl.pallas_call(
        body, grid=(S // ROW_TILE,),
        in_specs=[_row_spec(D), _const_spec(1, D), _row_spec(D)],
        out_specs=[_row_spec(D), pl.BlockSpec((8, 128), lambda i: (0, 0)), _const_spec(8, D)],
        out_shape=[jax.ShapeDtypeStruct((S, D), F32), jax.ShapeDtypeStruct((8, 128), F32),
                   jax.ShapeDtypeStruct((8, D), F32)],
        compiler_params=_cparams(("arbitrary",)), name=name)(x, fw, tgt)


def _rope(x, cos, sa, sb):
    return x * cos + pltpu.roll(x, 96, 1) * sa + pltpu.roll(x, 32, 1) * sb


def _rope_t(dy, cos, sa, sb):
    return dy * cos + pltpu.roll(dy * sa, 32, 1) + pltpu.roll(dy * sb, 96, 1)


_EVEN_KINDS = ['qa'] * 8 + ['ka'] * 2 + ['v'] * 2 + ['qb'] * 8 + ['kb'] * 2 + ['v'] * 2
_EVEN_DSRC = ([('q', j) for j in range(8)] + [('k', 0), ('k', 1), ('v', 0), ('v', 1)]
              + [('q', 8 + j) for j in range(8)] + [('k', 2), ('k', 3), ('v', 2), ('v', 3)])


def _cols(j):
    return slice(j * HEAD, (j + 1) * HEAD)


def prep_even(name, qkv, qn, kn, cos, sa, sb):
    T, W = qkv.shape

    def body(x_ref, qn_ref, kn_ref, cos_ref, sa_ref, sb_ref, o_ref):
        cos_, sa_, sb_ = cos_ref[...], sa_ref[...], sb_ref[...]
        for j, kind in enumerate(_EVEN_KINDS):
            x = x_ref[:, _cols(j)]
            if kind in ('qa', 'ka'):
                rstd = lax.rsqrt(jnp.mean(x * x, axis=-1, keepdims=True) + EPS)
                x = x * rstd * (qn_ref[...] if kind == 'qa' else kn_ref[...])
            if kind != 'v':
                x = _rope(x, cos_, sa_, sb_)
            o_ref[:, _cols(j)] = x.astype(BF16)

    blk = pl.BlockSpec((ROW_TILE, W), lambda i: (i, 0))
    tab = pl.BlockSpec((ROW_TILE, HEAD), lambda i: (i, 0))
    one = pl.BlockSpec((1, HEAD), lambda i: (0, 0))
    return pl.pallas_call(
        body, grid=(T // ROW_TILE,), in_specs=[blk, one, one, tab, tab, tab], out_specs=blk,
        out_shape=jax.ShapeDtypeStruct(qkv.shape, BF16),
        compiler_params=_cparams(("parallel",)), name=name)(qkv, qn, kn, cos, sa, sb)


def prep_even_bwd(name, qkv, dq, dk, dv, qn, kn, cos, sa, sb):
    T, W = qkv.shape

    def body(x_ref, dq_ref, dk_ref, dv_ref, qn_ref, kn_ref, cos_ref, sa_ref, sb_ref, o_ref, part_ref):
        @pl.when(pl.program_id(0) == 0)
        def _():
            part_ref[...] = jnp.zeros_like(part_ref)

        cos_, sa_, sb_ = cos_ref[...], sa_ref[...], sb_ref[...]
        src = {'q': dq_ref, 'k': dk_ref, 'v': dv_ref}
        sums = {'qa': None, 'ka': None}
        for j, kind in enumerate(_EVEN_KINDS):
            which, blk_j = _EVEN_DSRC[j]
            d = src[which][:, _cols(blk_j)]
            if kind != 'v':
                d = _rope_t(d, cos_, sa_, sb_)
            if kind in ('qa', 'ka'):
                x = x_ref[:, _cols(j)]
                rstd = lax.rsqrt(jnp.mean(x * x, axis=-1, keepdims=True) + EPS)
                xhat = x * rstd
                s = jnp.sum(d * xhat, axis=0, keepdims=True)
                sums[kind] = s if sums[kind] is None else sums[kind] + s
                dxhat = d * (qn_ref[...] if kind == 'qa' else kn_ref[...])
                d = rstd * (dxhat - xhat * jnp.mean(dxhat * xhat, axis=-1, keepdims=True))
            o_ref[:, _cols(j)] = d.astype(BF16)
        part_ref[0:1, :] += sums['qa']
        part_ref[1:2, :] += sums['ka']

    def rows(w):
        return pl.BlockSpec((ROW_TILE, w), lambda i: (i, 0))

    one = pl.BlockSpec((1, HEAD), lambda i: (0, 0))
    return pl.pallas_call(
        body, grid=(T // ROW_TILE,),
        in_specs=[rows(W), rows(dq.shape[1]), rows(dk.shape[1]), rows(dv.shape[1]), one, one,
                  rows(HEAD), rows(HEAD), rows(HEAD)],
        out_specs=[rows(W), pl.BlockSpec((8, HEAD), lambda i: (0, 0))],
        out_shape=[jax.ShapeDtypeStruct(qkv.shape, BF16), jax.ShapeDtypeStruct((8, HEAD), F32)],
        compiler_params=_cparams(("arbitrary",)), name=name)(qkv, dq, dk, dv, qn, kn, cos, sa, sb)


def _even_maps():
    qmap = lambda h, qb: (qb, jnp.where(h < 8, h, h + 4))
    kmap = lambda h, qb: (0, jnp.where(h < 8, 8 + h // 4, 18 + h // 4))
    vmap = lambda h, qb: (0, jnp.where(h < 8, 10 + h // 4, 20 + h // 4))
    return qmap, kmap, vmap


def _softmax_parts(parts, extra=None):
    m = parts[0].max(axis=-1, keepdims=True)
    for p in parts[1:]:
        m = jnp.maximum(m, p.max(axis=-1, keepdims=True))
    if extra is not None:
        m = jnp.maximum(m, extra)
    es = [jnp.exp(p - m) for p in parts]
    l = es[0].sum(axis=-1, keepdims=True)
    for e in es[1:]:
        l = l + e.sum(axis=-1, keepdims=True)
    ex = None
    if extra is not None:
        ex = jnp.exp(extra - m)
        l = l + ex
    inv = 1.0 / l
    return [e * inv for e in es], (None if ex is None else ex * inv)


def _win_scores(q, k_ref, qb, tq, nctx, S):
    L = tq + 2 * WINDOW
    nqc = nctx // tq
    qlat = (qb - nqc) * tq
    start = pl.multiple_of(jnp.clip(qlat - WINDOW, 0, S - L), 128)
    kc = k_ref[0:nctx, :]
    kw = k_ref[pl.ds(nctx + start, L), :]
    s_c = _dot(q, kc, NT) * SCALE
    s_w = _dot(q, kw, NT) * SCALE
    qpos = qlat + lax.broadcasted_iota(jnp.int32, (tq, 1), 0)
    kpos = start + lax.broadcasted_iota(jnp.int32, (1, L), 1)
    valid = jnp.logical_and(jnp.abs(kpos - qpos) <= WINDOW, qb >= nqc)
    return s_c, jnp.where(valid, s_w, NEG), start, L


def _softmax_raw(raw):
    m = raw.max(axis=-1, keepdims=True)
    e = jnp.exp2((raw - m) * (SCALE * np.log2(np.e)))
    return e * (1.0 / e.sum(axis=-1, keepdims=True))


def _glob_keys(qb, tq, nctx, T):
    is_ctx = qb < nctx // tq
    return [(is_ctx, slice(0, nctx)), (jnp.logical_not(is_ctx), slice(0, T))]


def attn_even_fwd(name, qkvh, sink, nctx, tq, carry=None):
    T = qkvh.shape[0]
    S = T - nctx
    qmap, kmap, vmap = _even_maps()

    def body(sink_ref, q_ref, k_ref, v_ref, o_ref):
        h, qb = pl.program_id(0), pl.program_id(1)
        q = q_ref[...]

        for pred, keys in _glob_keys(qb, tq, nctx, T):
            @pl.when(jnp.logical_and(h < 8, pred))
            def _():
                p = _softmax_raw(_dot(q, k_ref[keys, :], NT))
                o_ref[...] = _dot(p, v_ref[keys, :], NN).astype(BF16)

        @pl.when(h >= 8)
        def _():
            s_c, s_w, start, L = _win_scores(q, k_ref, qb, tq, nctx, S)
            sk = jnp.full((tq, 1), sink_ref[jnp.maximum(h - 8, 0)], F32)
            (p_c, p_w), _ = _softmax_parts([s_c, s_w], sk)
            o = _dot(p_c, v_ref[0:nctx, :], NN) + _dot(p_w, v_ref[pl.ds(nctx + start, L), :], NN)
            o_ref[...] = o.astype(BF16)

    res = _call(name, body, (16, T // tq), (sink, qkvh, qkvh, qkvh),
                [pl.BlockSpec(memory_space=pltpu.SMEM), pl.BlockSpec((tq, HEAD), qmap),
                 pl.BlockSpec((T, HEAD), kmap), pl.BlockSpec((T, HEAD), vmap)],
                [jax.ShapeDtypeStruct((T, 16 * HEAD), BF16)], [pl.BlockSpec((tq, HEAD), lambda h, qb: (qb, h))],
                [], ("parallel", "arbitrary"), carry)
    return _split(res, 1, carry)


def attn_even_bwd(name, qkvh, sink, do, nctx, tq, carry=None):
    T = qkvh.shape[0]
    S = T - nctx
    qmap, kmap, vmap = _even_maps()

    def body(sink_ref, q_ref, k_ref, v_ref, do_ref, dq_ref, dk_ref, dv_ref, ds_ref):
        h, qb = pl.program_id(0), pl.program_id(1)
        q = q_ref[...]
        dov = do_ref[...]

        @pl.when(jnp.logical_and(h % 4 == 0, qb == 0))
        def _():
            dk_ref[...] = jnp.zeros_like(dk_ref)
            dv_ref[...] = jnp.zeros_like(dv_ref)

        @pl.when(qb == 0)
        def _():
            ds_ref[...] = jnp.zeros_like(ds_ref)

        for pred, keys in _glob_keys(qb, tq, nctx, T):
            @pl.when(jnp.logical_and(h < 8, pred))
            def _():
                p = _softmax_raw(_dot(q, k_ref[keys, :], NT))
                dp = _dot(dov, v_ref[keys, :], NT)
                row = jnp.sum(p * dp, axis=-1, keepdims=True)
                dsb = (p * (dp - row) * SCALE).astype(BF16)
                dq_ref[...] = _dot(dsb, k_ref[keys, :], NN)
                dk_ref[keys, :] += _dot(dsb, q, TN)
                dv_ref[keys, :] += _dot(p, dov, TN)

        @pl.when(h >= 8)
        def _():
            s_c, s_w, start, L = _win_scores(q, k_ref, qb, tq, nctx, S)
            sk = jnp.full((tq, 1), sink_ref[jnp.maximum(h - 8, 0)], F32)
            (p_c, p_w), p_s = _softmax_parts([s_c, s_w], sk)
            win = pl.ds(nctx + start, L)
            dp_c = _dot(dov, v_ref[0:nctx, :], NT)
            dp_w = _dot(dov, v_ref[win, :], NT)
            row = jnp.sum(p_c * dp_c, axis=-1, keepdims=True) + jnp.sum(p_w * dp_w, axis=-1, keepdims=True)
            ds_c = (p_c * (dp_c - row) * SCALE).astype(BF16)
            ds_w = (p_w * (dp_w - row) * SCALE).astype(BF16)
            dq_ref[...] = _dot(ds_c, k_ref[0:nctx, :], NN) + _dot(ds_w, k_ref[win, :], NN)
            dk_ref[0:nctx, :] += _dot(ds_c, q, TN)
            dk_ref[win, :] += _dot(ds_w, q, TN)
            dv_ref[0:nctx, :] += _dot(p_c, dov, TN)
            dv_ref[win, :] += _dot(p_w, dov, TN)
            ds_ref[...] += jnp.sum(-(p_s * row))

    kv_out = pl.BlockSpec((T, HEAD), lambda h, qb: (0, h // 4))
    res = _call(name, body, (16, T // tq), (sink, qkvh, qkvh, qkvh, do),
                [pl.BlockSpec(memory_space=pltpu.SMEM), pl.BlockSpec((tq, HEAD), qmap),
                 pl.BlockSpec((T, HEAD), kmap), pl.BlockSpec((T, HEAD), vmap),
                 pl.BlockSpec((tq, HEAD), lambda h, qb: (qb, h))],
                [jax.ShapeDtypeStruct((T, 16 * HEAD), F32), jax.ShapeDtypeStruct((T, 4 * HEAD), F32),
                 jax.ShapeDtypeStruct((T, 4 * HEAD), F32), jax.ShapeDtypeStruct((16, 8, 128), F32)],
                [pl.BlockSpec((tq, HEAD), lambda h, qb: (qb, h)), kv_out, kv_out,
                 pl.BlockSpec((None, 8, 128), lambda h, qb: (h, 0, 0))],
                [], ("arbitrary", "arbitrary"), carry)
    return _split(res, 4, carry)


NA_GROUP = 4
NA_SPAN = NA_KH + NA_GROUP - 1
_NA_PLAN = [[(j, 0) for j in range(NA_GROUP)],
            [(NA_KH // 2, j) for j in range(NA_GROUP)],
            [(NA_KH // 2 + j, NA_GROUP - 1) for j in range(NA_GROUP)]]


def _na_group(g, n_groups, rows):
    last = g == n_groups - 1
    kind = jnp.where(g == 0, 0, jnp.where(last, 2, 1))
    first_row = jnp.where(g == 0, 0, jnp.where(last, rows - NA_SPAN, NA_GROUP * g - NA_KH // 2))
    return kind, first_row


def na_span_bias(bias8):
    LW, LS = NA_KH * GRID_W, NA_SPAN * GRID_W
    kinds = []
    for plan in _NA_PLAN:
        strips = [jnp.pad(bias8[:, off], ((0, 0), (0, 0), (s * GRID_W, LS - LW - s * GRID_W)), constant_values=NEG)
                  for off, s in plan]
        kinds.append(jnp.concatenate(strips, axis=1))
    return jnp.stack(kinds, axis=1)


def na_span_bias_grad(db):
    LW = NA_KH * GRID_W
    out = [None] * NA_KH
    for kind, plan in enumerate(_NA_PLAN):
        for j, (off, s) in enumerate(plan):
            piece = db[:, kind, j * GRID_W:(j + 1) * GRID_W, s * GRID_W:s * GRID_W + LW]
            out[off] = piece if out[off] is None else out[off] + piece
    return jnp.stack(out, axis=1)


def _na_specs(T, nctx, n_groups, rows):
    LS = NA_SPAN * GRID_W
    tq = NA_GROUP * GRID_W
    assert nctx % tq == 0 and n_groups >= 3
    q_spec = pl.BlockSpec((tq, HEAD), lambda h, g: (g + nctx // tq, h))
    k_spec = pl.BlockSpec((T, HEAD), lambda h, g: (0, 16 + h))
    v_spec = pl.BlockSpec((T, HEAD), lambda h, g: (0, 32 + h))
    b_spec = pl.BlockSpec((None, None, tq, LS), lambda h, g: (h, _na_group(g, n_groups, rows)[0], 0, 0))
    row_spec = pl.BlockSpec((tq, HEAD), lambda h, g: (g, h))
    return q_spec, k_spec, v_spec, b_spec, row_spec


def _na_scores(q, k_ref, b_ref, g, n_groups, rows, nctx):
    first_row = _na_group(g, n_groups, rows)[1]
    win = pl.ds(pl.multiple_of(nctx + first_row * GRID_W, GRID_W), NA_SPAN * GRID_W)
    s_c = _dot(q, k_ref[0:nctx, :], NT) * SCALE
    s_w = _dot(q, k_ref[win, :], NT) * SCALE + b_ref[...]
    return s_c, s_w, win


def attn_odd_fwd(name, qkv, bias_s, nctx, carry=None):
    T = qkv.shape[0]
    S = T - nctx
    rows = S // GRID_W
    n_groups = rows // NA_GROUP
    q_spec, k_spec, v_spec, b_spec, row_spec = _na_specs(T, nctx, n_groups, rows)

    def body(q_ref, k_ref, v_ref, b_ref, o_ref):
        s_c, s_w, win = _na_scores(q_ref[...], k_ref, b_ref, pl.program_id(1), n_groups, rows, nctx)
        (p_c, p_w), _ = _softmax_parts([s_c, s_w])
        o_ref[...] = (_dot(p_c, v_ref[0:nctx, :], NN) + _dot(p_w, v_ref[win, :], NN)).astype(BF16)

    res = _call(name, body, (16, n_groups), (qkv, qkv, qkv, bias_s), [q_spec, k_spec, v_spec, b_spec],
                [jax.ShapeDtypeStruct((S, 16 * HEAD), BF16)], [row_spec], [], ("parallel", "arbitrary"), carry)
    return _split(res, 1, carry)


def attn_odd_bwd(name, qkv, bias_s, do, nctx, carry=None):
    T = qkv.shape[0]
    S = T - nctx
    rows = S // GRID_W
    n_groups = rows // NA_GROUP
    q_spec, k_spec, v_spec, b_spec, row_spec = _na_specs(T, nctx, n_groups, rows)

    def body(q_ref, k_ref, v_ref, b_ref, do_ref, dq_ref, dk_ref, dv_ref, db_ref):
        g = pl.program_id(1)
        q = q_ref[...]
        dov = do_ref[...]

        @pl.when(g == 0)
        def _():
            dk_ref[...] = jnp.zeros_like(dk_ref)
            dv_ref[...] = jnp.zeros_like(dv_ref)

        s_c, s_w, win = _na_scores(q, k_ref, b_ref, g, n_groups, rows, nctx)
        (p_c, p_w), _ = _softmax_parts([s_c, s_w])
        dp_c = _dot(dov, v_ref[0:nctx, :], NT)
        dp_w = _dot(dov, v_ref[win, :], NT)
        row = jnp.sum(p_c * dp_c, axis=-1, keepdims=True) + jnp.sum(p_w * dp_w, axis=-1, keepdims=True)
        dsw = p_w * (dp_w - row)
        first_visit = jnp.logical_or(g <= 1, g == n_groups - 1)

        @pl.when(first_visit)
        def _():
            db_ref[...] = dsw

        @pl.when(jnp.logical_not(first_visit))
        def _():
            db_ref[...] += dsw

        ds_c = (p_c * (dp_c - row) * SCALE).astype(BF16)
        ds_w = (dsw * SCALE).astype(BF16)
        dq_ref[...] = _dot(ds_c, k_ref[0:nctx, :], NN) + _dot(ds_w, k_ref[win, :], NN)
        dk_ref[0:nctx, :] += _dot(ds_c, q, TN)
        dk_ref[win, :] += _dot(ds_w, q, TN)
        dv_ref[0:nctx, :] += _dot(p_c, dov, TN)
        dv_ref[win, :] += _dot(p_w, dov, TN)

    kv_out = pl.BlockSpec((T, HEAD), lambda h, g: (0, h))
    res = _call(name, body, (16, n_groups), (qkv, qkv, qkv, bias_s, do), [q_spec, k_spec, v_spec, b_spec, row_spec],
                [jax.ShapeDtypeStruct((S, 16 * HEAD), F32), jax.ShapeDtypeStruct((T, 16 * HEAD), F32),
                 jax.ShapeDtypeStruct((T, 16 * HEAD), F32), jax.ShapeDtypeStruct(bias_s.shape, F32)],
                [row_spec, kv_out, kv_out, b_spec], [], ("arbitrary", "arbitrary"), carry)
    return _split(res, 4, carry)


def _na_onehots():
    o = np.arange(NA_KH)[:, None]
    i = np.arange(NA_KH)[None, :]
    a = i - o + NA_KH - 1
    A = (a[..., None] == np.arange(2 * NA_KH - 1)).astype(np.float32)
    qc = np.arange(GRID_W)[:, None]
    kc = np.arange(GRID_W)[None, :]
    b = np.clip(kc - qc + NA_KW - 1, 0, 2 * NA_KW - 2)
    cs = np.clip(qc - NA_KW // 2, 0, GRID_W - NA_KW)
    valid = (kc >= cs) & (kc < cs + NA_KW)
    B = ((b[..., None] == np.arange(2 * NA_KW - 1)) & valid[..., None]).astype(np.float32)
    return A, B, valid


def na_bias_table(rpb):
    A, B, valid = _na_onehots()
    hp = lax.Precision.HIGHEST
    t = jnp.einsum('hab,oia->hoib', rpb, jnp.asarray(A), precision=hp)
    bias = jnp.einsum('hoib,qkb->hoqik', t, jnp.asarray(B), precision=hp)
    bias = jnp.where(jnp.asarray(valid)[None, None, :, None, :], bias, NEG)
    return bias.reshape(rpb.shape[0], NA_KH, GRID_W, NA_KH * GRID_W)


def na_bias_grad(name, dbias8):
    A, B, _ = _na_onehots()
    H = dbias8.shape[0]
    nb, na = 2 * NA_KW - 1, 2 * NA_KH - 1
    d = dbias8.reshape(H, NA_KH, GRID_W, NA_KH, GRID_W).transpose(0, 1, 3, 2, 4)
    d = d.reshape(H * NA_KH * NA_KH, GRID_W * GRID_W)
    Bp = np.zeros((GRID_W * GRID_W, 128), np.float32)
    Bp[:, :nb] = B.reshape(GRID_W * GRID_W, nb)
    Ap = np.zeros((16, NA_KH * NA_KH), np.float32)
    Ap[:na] = A.reshape(NA_KH * NA_KH, na).T
    rows_per_head = NA_KH * NA_KH

    def split3(x):
        hi = x.astype(BF16)
        r1 = x - hi.astype(F32)
        mid = r1.astype(BF16)
        return hi, mid, (r1 - mid.astype(F32)).astype(BF16)

    def body(d_ref, b_ref, a_ref, o_ref):
        bm, am = b_ref[...], a_ref[...]
        g = sum(lax.dot_general(p, bm, NN, preferred_element_type=F32) for p in split3(d_ref[...]))
        o_ref[...] = sum(lax.dot_general(am, p, NN, preferred_element_type=F32) for p in split3(g))

    out = pl.pallas_call(
        body, grid=(H,),
        in_specs=[pl.BlockSpec((rows_per_head, GRID_W * GRID_W), lambda h: (h, 0)),
                  pl.BlockSpec((GRID_W * GRID_W, 128), lambda h: (0, 0)),
                  pl.BlockSpec((16, rows_per_head), lambda h: (0, 0))],
        out_specs=pl.BlockSpec((None, 16, 128), lambda h: (h, 0, 0)),
        out_shape=jax.ShapeDtypeStruct((H, 16, 128), F32),
        compiler_params=_cparams(("parallel",)), name=name)(d, jnp.asarray(Bp, BF16), jnp.asarray(Ap, BF16))
    return out[:, :na, :nb]


def _vmem_call(name, fn, out_shape, *arrays):
    def body(*refs):
        n = len(arrays)
        res = fn(*[r[...] for r in refs[:n]])
        if not isinstance(res, (tuple, list)):
            res = (res,)
        for o, v in zip(refs[n:], res):
            o[...] = v
    return pl.pallas_call(body, out_shape=out_shape, name=name,
                          compiler_params=pltpu.CompilerParams(vmem_limit_bytes=VMEM_LIMIT))(*arrays)


def _silu(v):
    return v / (1.0 + jnp.exp(-v))


def _adamw_math(w, g, m, v):
    m2 = ADAM_B1 * m + (1.0 - ADAM_B1) * g
    v2 = ADAM_B2 * v + (1.0 - ADAM_B2) * (g * g)
    m_hat = m2 / (1.0 - ADAM_B1 ** ADAM_STEP)
    v_hat = v2 / (1.0 - ADAM_B2 ** ADAM_STEP)
    delta = -ADAM_LR * (m_hat / (jnp.sqrt(v_hat) + ADAM_EPS) + ADAM_WD * w)
    return delta, m2, v2


def _ew_tile(R, C):
    return _tile(R, max(64, (262144 // C) // 64 * 64))


def adamw_rows(name, w, g, m, v, extra_g=None):
    R, C = w.shape
    tr = _ew_tile(R, C)
    extra_g = list(extra_g or [])
    ne = len(extra_g)

    def body(*refs):
        w_ref, g_ref, m_ref, v_ref = refs[:4]
        gs = g_ref[...]
        for e in refs[4:4 + ne]:
            gs = gs + e[...].astype(F32)
        go, do, mo, vo = refs[4 + ne:]
        d, m2, v2 = _adamw_math(w_ref[...], gs, m_ref[...], v_ref[...])
        go[...] = gs
        do[...] = d
        mo[...] = m2
        vo[...] = v2

    spec = pl.BlockSpec((tr, C), lambda i: (i, 0))
    return pl.pallas_call(
        body, grid=(R // tr,), in_specs=[spec] * (4 + ne), out_specs=[spec] * 4,
        out_shape=[jax.ShapeDtypeStruct((R, C), F32)] * 4,
        compiler_params=_cparams(("parallel",)), name=name)(w, g, m, v, *extra_g)


def rs_chip_sum(name, g8, sib4, half):
    _, R, C = g8.shape
    tr = _ew_tile(R, C)

    def body(s_ref, g_ref, b_ref, o_ref):
        o_ref[...] = (g_ref[...] + b_ref[...]).astype(BF16)

    blk = (None, tr, C)
    grid_spec = pltpu.PrefetchScalarGridSpec(
        num_scalar_prefetch=1, grid=(4, R // tr),
        in_specs=[pl.BlockSpec(blk, lambda q, i, s: (s[0] + q, i, 0)), pl.BlockSpec(blk, lambda q, i, s: (q, i, 0))],
        out_specs=pl.BlockSpec(blk, lambda q, i, s: (q, i, 0)))
    return pl.pallas_call(body, grid_spec=grid_spec, out_shape=jax.ShapeDtypeStruct((4, R, C), BF16),
                          compiler_params=_cparams(("parallel", "parallel")), name=name)(half, g8, sib4)


def adamw_rs(name, w, g8, sib4, rem3, m, v, idx, layer, prev=None):
    L, R, C = w.shape
    tr = _ew_tile(R, C)

    def body(s_ref, w_ref, g_ref, sb_ref, r0_ref, r1_ref, r2_ref, m_ref, v_ref, *rest):
        go, do, mo, vo = rest[-4:]
        gs = g_ref[...] + sb_ref[...]
        for r_ref in (r0_ref, r1_ref, r2_ref):
            gs = gs + r_ref[...].astype(F32)
        d, m2, v2 = _adamw_math(w_ref[...], gs, m_ref[...], v_ref[...])
        go[...] = gs
        do[...] = d
        mo[...] = m2
        vo[...] = v2

    blk = (None, tr, C)
    mine = pl.BlockSpec(blk, lambda i, s: (layer, i, 0))

    def rem(k):
        return pl.BlockSpec(blk, lambda i, s: (k, i, 0))

    prev = list(prev or [])
    grid_spec = pltpu.PrefetchScalarGridSpec(
        num_scalar_prefetch=1, grid=(R // tr,),
        in_specs=[mine, pl.BlockSpec(blk, lambda i, s: (s[0], i, 0)), pl.BlockSpec(blk, lambda i, s: (s[1], i, 0)),
                  rem(0), rem(1), rem(2), mine, mine] + [pl.BlockSpec(memory_space=pl.ANY)] * len(prev),
        out_specs=[mine] * 4)
    return pl.pallas_call(body, grid_spec=grid_spec, out_shape=[jax.ShapeDtypeStruct((L, R, C), F32)] * 4,
                          input_output_aliases={9 + k: k for k in range(len(prev))},
                          compiler_params=_cparams(("parallel",)), name=name)(
                              idx, w, g8, sib4, rem3, rem3, rem3, m, v, *prev)


def _me():
    x, y, c = lax.axis_index("x"), lax.axis_index("y"), lax.axis_index("c")
    return x, y, c


def _flip(v, bit):
    return 1 - v if bit else v


def ag_small(name, x, with_sum=False):
    R, C = x.shape

    def body(x_ref, out_ref, *rest):
        if with_sum:
            sum_ref, send_sems, recv_sems, lsem = rest
        else:
            send_sems, recv_sems, lsem = rest
        mx, my, mc = _me()
        me = 4 * mx + 2 * my + mc
        local = pltpu.make_async_copy(x_ref, out_ref.at[me], lsem)
        local.start()
        sends = []
        for k in range(1, NDEV):
            peer = (_flip(mx, k & 4), _flip(my, k & 2), _flip(mc, k & 1))
            cp = pltpu.make_async_remote_copy(src_ref=x_ref, dst_ref=out_ref.at[me], send_sem=send_sems.at[k - 1],
                                              recv_sem=recv_sems.at[k - 1], device_id=peer, device_id_type=MESH)
            cp.start()
            sends.append(cp)
        for k in range(1, NDEV):
            px, py, pc = _flip(mx, k & 4), _flip(my, k & 2), _flip(mc, k & 1)
            pltpu.make_async_remote_copy(src_ref=x_ref, dst_ref=out_ref.at[4 * px + 2 * py + pc],
                                         send_sem=send_sems.at[k - 1], recv_sem=recv_sems.at[k - 1],
                                         device_id=(px, py, pc), device_id_type=MESH).wait_recv()
        for cp in sends:
            cp.wait_send()
        local.wait()
        if with_sum:
            acc = out_ref[0]
            for d in range(1, NDEV):
                acc = acc + out_ref[d]
            sum_ref[...] = acc

    out_shape = [jax.ShapeDtypeStruct((NDEV, R, C), F32)]
    if with_sum:
        out_shape.append(jax.ShapeDtypeStruct((R, C), F32))
    vm = pl.BlockSpec(memory_space=pltpu.VMEM)
    res = pl.pallas_call(
        body, out_shape=out_shape, in_specs=[vm], out_specs=[vm] * len(out_shape),
        scratch_shapes=[pltpu.SemaphoreType.DMA((NDEV - 1,)), pltpu.SemaphoreType.DMA((NDEV - 1,)),
                        pltpu.SemaphoreType.DMA],
        compiler_params=pltpu.CompilerParams(vmem_limit_bytes=VMEM_LIMIT), name=name)(x)
    return res if with_sum else res[0]


def ag_big(name, shards):
    n = len(shards)

    def body(*refs):
        ins, outs = refs[:n], refs[n:2 * n]
        send_sems, recv_sems, lsems = refs[2 * n:]
        mx, my, mc = _me()
        me = (mx, my, mc)
        sibling = (mx, my, 1 - mc)
        chips = [(1 - mx, my), (mx, 1 - my), (1 - mx, 1 - my)]

        def idx(p):
            return 4 * p[0] + 2 * p[1] + p[2]

        def copy(t, k, block, to, src=None):
            dst = outs[t].at[idx(block)]
            return pltpu.make_async_remote_copy(
                src_ref=dst if src is None else src, dst_ref=dst, send_sem=send_sems.at[7 * t + k],
                recv_sem=recv_sems.at[7 * t + k], device_id=to, device_id_type=MESH)

        started = []
        locals_ = []
        for t in range(n):
            mine = pltpu.make_async_copy(ins[t], outs[t].at[idx(me)], lsems.at[t])
            mine.start()
            locals_.append(mine)
            first = [copy(t, 0, me, sibling, src=ins[t])]
            first += [copy(t, 1 + j, me, (*chip, mc), src=ins[t]) for j, chip in enumerate(chips)]
            for cp in first:
                cp.start()
            started += first
        for t in range(n):
            for j, chip in enumerate(chips):
                copy(t, 1 + j, (*chip, mc), me).wait_recv()
                fwd = copy(t, 4 + j, (*chip, mc), sibling)
                fwd.start()
                started.append(fwd)
        for t in range(n):
            copy(t, 0, sibling, me).wait_recv()
            for j, chip in enumerate(chips):
                copy(t, 4 + j, (*chip, 1 - mc), me).wait_recv()
        for cp in started:
            cp.wait_send()
        for mine in locals_:
            mine.wait()

    anyspec = pl.BlockSpec(memory_space=pl.ANY)
    return pl.pallas_call(
        body, out_shape=[jax.ShapeDtypeStruct((NDEV,) + s.shape, s.dtype) for s in shards],
        in_specs=[anyspec] * n, out_specs=[anyspec] * n,
        scratch_shapes=[pltpu.SemaphoreType.DMA((7 * n,)), pltpu.SemaphoreType.DMA((7 * n,)),
                        pltpu.SemaphoreType.DMA((n,))],
        name=name)(*shards)


def _idx(p):
    return 4 * p[0] + 2 * p[1] + p[2]


def _remote(src, dst, ss, rs, k, to):
    return pltpu.make_async_remote_copy(src_ref=src, dst_ref=dst, send_sem=ss.at[k], recv_sem=rs.at[k],
                                        device_id=to, device_id_type=MESH)


def ex_ag_chips(shards):
    n = len(shards)

    def copies(ci, co, ss, rs, base):
        mx, my, mc = _me()
        me = (mx, my, mc)
        peers = [(mx, my, 1 - mc), (1 - mx, my, mc), (mx, 1 - my, mc), (1 - mx, 1 - my, mc)]
        sends, recvs, local = [], [], []
        for t in range(n):
            b = base + 5 * t
            for k, peer in enumerate(peers):
                sends.append(_remote(ci[t], co[t].at[_idx(me)], ss, rs, b + k, peer))
                recvs.append(_remote(ci[t], co[t].at[_idx(peer)], ss, rs, b + k, peer))
            local.append(pltpu.make_async_copy(ci[t], co[t].at[_idx(me)], ss.at[b + 4]))
        return sends, recvs, local

    def start(ci, co, ss, rs, base):
        sends, _, local = copies(ci, co, ss, rs, base)
        for cp in local + sends:
            cp.start()

    def finish(ci, co, ss, rs, base):
        sends, recvs, local = copies(ci, co, ss, rs, base)
        for cp in recvs:
            cp.wait_recv()
        for cp in sends:
            cp.wait_send()
        for cp in local:
            cp.wait()

    outs = [jax.ShapeDtypeStruct((NDEV,) + s.shape, s.dtype) for s in shards]
    return Exchange(shards, outs, {}, 5 * n, start, finish)


def ex_ag_sibling(bufs):
    n = len(bufs)

    def copies(co, ss, rs, base):
        mx, my, mc = _me()
        sibling = (mx, my, 1 - mc)
        chips = [(1 - mx, my), (mx, 1 - my), (1 - mx, 1 - my)]
        sends, recvs = [], []
        for t in range(n):
            for j, chip in enumerate(chips):
                mine, theirs = co[t].at[_idx((*chip, mc))], co[t].at[_idx((*chip, 1 - mc))]
                sends.append(_remote(mine, mine, ss, rs, base + 3 * t + j, sibling))
                recvs.append(_remote(mine, theirs, ss, rs, base + 3 * t + j, sibling))
        return sends, recvs

    def start(ci, co, ss, rs, base):
        for cp in copies(co, ss, rs, base)[0]:
            cp.start()

    def finish(ci, co, ss, rs, base):
        sends, recvs = copies(co, ss, rs, base)
        for cp in recvs:
            cp.wait_recv()
        for cp in sends:
            cp.wait_send()

    outs = [jax.ShapeDtypeStruct(b.shape, b.dtype) for b in bufs]
    return Exchange(bufs, outs, {t: t for t in range(n)}, 3 * n, start, finish)


def ex_rs_sibling(grads):
    n = len(grads)

    def copies(ci, co, ss, rs, base):
        mx, my, mc = _me()
        return [_remote(ci[t].at[pl.ds((1 - mc) * 4, 4)], co[t], ss, rs, base + t, (mx, my, 1 - mc)) for t in range(n)]

    def start(ci, co, ss, rs, base):
        for cp in copies(ci, co, ss, rs, base):
            cp.start()

    def finish(ci, co, ss, rs, base):
        for cp in copies(ci, co, ss, rs, base):
            cp.wait()

    outs = [jax.ShapeDtypeStruct((4,) + g.shape[1:], g.dtype) for g in grads]
    return Exchange(grads, outs, {}, n, start, finish)


def ex_rs_chips(parts):
    n = len(parts)

    def copies(ci, co, ss, rs, base):
        mx, my, mc = _me()
        cps = []
        for t in range(n):
            for k in range(1, 4):
                px, py = _flip(mx, k & 2), _flip(my, k & 1)
                cps.append(_remote(ci[t].at[2 * px + py], co[t].at[k - 1], ss, rs, base + 3 * t + k - 1, (px, py, mc)))
        return cps

    def start(ci, co, ss, rs, base):
        for cp in copies(ci, co, ss, rs, base):
            cp.start()

    def finish(ci, co, ss, rs, base):
        for cp in copies(ci, co, ss, rs, base):
            cp.wait()

    outs = [jax.ShapeDtypeStruct((3,) + p.shape[1:], p.dtype) for p in parts]
    return Exchange(parts, outs, {}, 3 * n, start, finish)


def run_exchanges(name, xs):
    x = merge_exchanges(xs)
    n_ci, n_co = len(x.ins), len(x.out_shapes)

    def body(*refs):
        ci, co = refs[:n_ci], refs[n_ci:n_ci + n_co]
        ss, rs = refs[n_ci + n_co:]
        x.start(ci, co, ss, rs, 0)
        x.finish(ci, co, ss, rs, 0)

    hbm = pl.BlockSpec(memory_space=pl.ANY)
    return pl.pallas_call(
        body, out_shape=x.out_shapes, in_specs=[hbm] * n_ci, out_specs=[hbm] * n_co, input_output_aliases=x.aliases,
        scratch_shapes=[pltpu.SemaphoreType.DMA((x.n_sems,)), pltpu.SemaphoreType.DMA((x.n_sems,))], name=name)(*x.ins)


def _rope_tables(S, nctx):
    t = jnp.arange(S)
    row = (t // GRID_W).astype(F32)
    col = (t % GRID_W).astype(F32)
    pairs = HEAD // 4
    inv = ROPE_THETA ** (-jnp.arange(pairs, dtype=F32) / pairs)
    ang_r = row[:, None] * inv
    ang_c = col[:, None] * inv
    ang = jnp.concatenate([ang_r, ang_r, ang_c, ang_c], axis=-1)
    cos = jnp.concatenate([jnp.ones((nctx, HEAD), F32), jnp.cos(ang)], axis=0)
    sin = jnp.concatenate([jnp.zeros((nctx, HEAD), F32), jnp.sin(ang)], axis=0)
    lane = jnp.arange(HEAD)[None, :]
    first = (lane & 32) == 0
    return cos, jnp.where(first, -sin, 0.0), jnp.where(first, 0.0, sin)


def _pad_rows(v, rows):
    v = v.reshape(-1).astype(F32)
    return jnp.pad(v, (0, rows * 128 - v.shape[0])).reshape(rows, 128)


def _rows8(n):
    return -(-n // 1024) * 8


def kernel(x, c, ctx, c_ctx, ada_w, ada_b, norm_w, mlp_w1, mlp_w2, ev_w_in, ev_w_out, ev_q_norm, ev_k_norm, ev_sink, od_w_in, od_w_out, od_rpb, final_norm_w, loss_target, m_c_ctx, m_ada_w, m_ada_b, m_norm_w, m_mlp_w1, m_mlp_w2, m_ev_w_in, m_ev_w_out, m_ev_q_norm, m_ev_k_norm, m_ev_sink, m_od_w_in, m_od_w_out, m_od_rpb, m_final_norm_w, v_c_ctx, v_ada_w, v_ada_b, v_norm_w, v_mlp_w1, v_mlp_w2, v_ev_w_in, v_ev_w_out, v_ev_q_norm, v_ev_k_norm, v_ev_sink, v_od_w_in, v_od_w_out, v_od_rpb, v_final_norm_w):
    S, D = x.shape[1], x.shape[2]
    NC = ctx.shape[1]
    T = NC + S
    assert NC == ROW_TILE and S % GRID_W == 0
    ada_cols = ada_w.shape[2]
    nw_cols = norm_w.shape[2]
    me = 4 * lax.axis_index("x") + 2 * lax.axis_index("y") + lax.axis_index("c")

    pack1 = jnp.concatenate([_pad_rows(c, _rows8(D)), _pad_rows(norm_w, _rows8(4 * nw_cols))], axis=0)
    g1 = ag_small("ag_c_normw", pack1)
    c_all = g1[:, :D // 128].reshape(NDEV, D)
    nw_rows = _rows8(D)
    nw = g1[:, nw_rows:nw_rows + 4 * nw_cols // 128].reshape(NDEV, 2, 2, nw_cols)
    nw = nw.transpose(1, 2, 0, 3).reshape(2, 2, D)
    cin = jnp.concatenate([c_all, jnp.broadcast_to(c_ctx[None], (NDEV, D))], axis=0)
    act = _vmem_call("silu_c", lambda v: _silu(v).astype(BF16), jax.ShapeDtypeStruct((2 * NDEV, D), BF16), cin)
    ada_b_loc = lax.dynamic_slice_in_dim(ada_b, me * ada_cols, ada_cols, axis=1)
    mods = [mm_nn(f"mod{i}", act, ada_w[i], _epi_bias, [F32], extras=(ada_b_loc[i:i + 1],), extra_kinds=('n',))[0]
            for i in range(2)]
    gm = ag_small("ag_mod", jnp.concatenate(mods, axis=1))
    gm = gm.reshape(NDEV, 2 * NDEV, 2, ada_cols).transpose(2, 1, 0, 3).reshape(2, 2 * NDEV, NDEV * ada_cols)
    mod_lat = lax.dynamic_index_in_dim(gm, me, axis=1, keepdims=False)
    mod_ctx = gm[:, NDEV]
    mod2 = jnp.stack([mod_ctx, mod_lat], axis=1).reshape(2, 2, 6, D)

    def chunk(i, j):
        return mod2[i, :, j, :]

    def b16(w):
        return w.astype(BF16)

    w_in_e, w_out_e = ag_big("ag_weights_l0_attn", [b16(ev_w_in[0]), b16(ev_w_out[0])])
    w_out_e = w_out_e.reshape(-1, D)

    cos, sa, sb = _rope_tables(S, NC)
    bias8 = na_span_bias(na_bias_table(od_rpb[0]))
    sink = ev_sink[0]
    TQ_F, TQ_B = 256, 256

    X0 = jnp.concatenate([ctx[0], x[0]], axis=0)
    h_a = norm_mod("l0_norm1", X0, nw[0, 0][None], chunk(0, 0), chunk(0, 1), NC)
    qkv0 = mm_nn("l0_qkv", h_a, w_in_e, _epi_store(F32), [F32])[0]
    qkvh0 = prep_even("l0_prep", qkv0, ev_q_norm, ev_k_norm, cos, sa, sb)
    o0, mlp0_half = attn_even_fwd("l0_attn", qkvh0, sink, NC, TQ_F,
                                  carry=[ex_ag_chips([b16(mlp_w1[0]), b16(mlp_w2[0])])])
    tm0 = _tile(T, 1100)
    (X1, y0), (w1_0, w2_0) = mm_nn("l0_out", o0, w_out_e, _epi_resid_gate(NC, tm0), [F32, F32],
                                   extras=(X0, chunk(0, 2)), extra_kinds=('mn', 'n'),
                                   carry=[ex_ag_sibling(mlp0_half)])
    h_b = norm_mod("l0_norm2", X1, nw[0, 1][None], chunk(0, 3), chunk(0, 4), NC)
    (a0, r0), od_half = mm_nn("l0_up", h_b, w1_0, _epi_relu2, [BF16, BF16], tn_cap=1024,
                              carry=[ex_ag_chips([b16(od_w_in[0]), b16(od_w_out[0])])])
    (X2, z0), (w1_1_half, w_in_o, w_out_o) = mm_nn(
        "l0_down", a0, w2_0.reshape(-1, D), _epi_resid_gate(NC, tm0), [F32, F32], extras=(X1, chunk(0, 5)),
        extra_kinds=('mn', 'n'), carry=[ex_ag_chips([b16(mlp_w1[1])]), ex_ag_sibling(od_half)])
    w_out_o = w_out_o.reshape(-1, D)

    h_c = norm_mod("l1_norm1", X2, nw[1, 0][None], chunk(1, 0), chunk(1, 1), NC)
    (qkv1,), (w2_1_half, w1_1) = mm_nn("l1_qkv", h_c, w_in_o, _epi_store(BF16), [BF16], tn_cap=768,
                                        carry=[ex_ag_chips([b16(mlp_w2[1])]), ex_ag_sibling([w1_1_half])])
    o1, (w2_1,) = attn_odd_fwd("l1_attn", qkv1, bias8, NC, carry=[ex_ag_sibling([w2_1_half])])
    X2l = X2[NC:]
    tm1 = _tile(S, 1100)
    X3, y1 = mm_nn("l1_out", o1, w_out_o, _epi_resid_gate(0, tm1), [F32, F32], extras=(X2l, chunk(1, 2)),
                   extra_kinds=('mn', 'n'))
    h_d = norm_mod("l1_norm2", X3, nw[1, 1][None], chunk(1, 3), chunk(1, 4), 0)
    a1, r1 = mm_nn("l1_up", h_d, w1_1, _epi_relu2, [BF16, BF16], tn_cap=1024)
    X4, z1 = mm_nn("l1_down", a1, w2_1.reshape(-1, D), _epi_resid_gate(0, tm1), [F32, F32], extras=(X3, chunk(1, 5)),
                   extra_kinds=('mn', 'n'))
    dX4, loss_p, dfw_p = final_loss("final_loss", X4, final_norm_w[None], loss_target[0])
    w_in = [w_in_e, w_in_o]
    w_out = [w_out_e, w_out_o]
    w1 = [w1_0, w1_1]
    w2 = [w2_0.reshape(-1, D), w2_1.reshape(-1, D)]

    mc4 = (lax.axis_index("c") * 4).astype(jnp.int32)
    my_chip = (2 * lax.axis_index("x") + lax.axis_index("y")).astype(jnp.int32)

    def chip_sum(tag, g8, sib4):
        return rs_chip_sum(f"rs_chip_sum_{tag}", g8, sib4, mc4[None])

    dz1, pg2_1 = gate_bwd("l1_gate2_bwd", dX4, z1, chunk(1, 5), 0)
    du1 = mm_nt("l1_down_dx", dz1, w2[1], _epi_mul2r, BF16, extras=(r1,))
    g_w2_1 = mm_tn("l1_down_dw", a1, dz1, 0)
    dh_d, (sib_w2_1,) = mm_nt("l1_up_dx", du1, w1[1], _epi_store(F32), F32, carry=[ex_rs_sibling([g_w2_1])])
    g_w1_1, (rem_w2_1,) = mm_tn("l1_up_dw", h_d, du1, 1,
                                carry=[ex_rs_chips([chip_sum("w2_1", g_w2_1, sib_w2_1)])])
    dX3, pn2_1 = norm_bwd("l1_norm2_bwd", X3, dh_d, dX4, nw[1, 1][None], chunk(1, 4), 0)
    dy1, pg1_1 = gate_bwd("l1_gate1_bwd", dX3, y1, chunk(1, 2), 0)
    do1 = mm_nt("l1_out_dx", dy1, w_out[1], _epi_store(BF16), BF16)
    g_wout_1, (sib_w1_1,) = mm_tn("l1_out_dw", o1, dy1, 0, carry=[ex_rs_sibling([g_w1_1])])
    (dq1, dk1, dv1, dbias8), (rem_w1_1, sib_wout_1) = attn_odd_bwd(
        "l1_attn_bwd", qkv1, bias8, do1, NC,
        carry=[ex_rs_chips([chip_sum("w1_1", g_w1_1, sib_w1_1)]), ex_rs_sibling([g_wout_1])])
    dqkv1 = jnp.concatenate([jnp.pad(dq1, ((NC, 0), (0, 0))), dk1, dv1], axis=1).astype(BF16)
    dh_c, (rem_wout_1,) = mm_nt("l1_qkv_dx", dqkv1, w_in[1], _epi_store(F32), F32,
                                carry=[ex_rs_chips([chip_sum("wout_1", g_wout_1, sib_wout_1)])])
    g_win_1 = mm_tn("l1_qkv_dw", h_c, dqkv1, 1)
    dX2, pn1_1 = norm_bwd("l1_norm1_bwd", X2, dh_c, dX3, nw[1, 0][None], chunk(1, 1), NC, dres_skip=NC)
    d_rpb = na_bias_grad("rpb_grad", na_span_bias_grad(dbias8))

    dz0, pg2_0 = gate_bwd("l0_gate2_bwd", dX2, z0, chunk(0, 5), NC)
    du0, (sib_win_1,) = mm_nt("l0_down_dx", dz0, w2[0], _epi_mul2r, BF16, extras=(r0,),
                              carry=[ex_rs_sibling([g_win_1])])
    g_w2_0, (rem_win_1,) = mm_tn("l0_down_dw", a0, dz0, 0,
                                 carry=[ex_rs_chips([chip_sum("win_1", g_win_1, sib_win_1)])])
    dh_b, (sib_w2_0,) = mm_nt("l0_up_dx", du0, w1[0], _epi_store(F32), F32, carry=[ex_rs_sibling([g_w2_0])])
    g_w1_0, (rem_w2_0,) = mm_tn("l0_up_dw", h_b, du0, 1,
                                carry=[ex_rs_chips([chip_sum("w2_0", g_w2_0, sib_w2_0)])])
    dX1, pn2_0 = norm_bwd("l0_norm2_bwd", X1, dh_b, dX2, nw[0, 1][None], chunk(0, 4), NC)
    dy0, pg1_0 = gate_bwd("l0_gate1_bwd", dX1, y0, chunk(0, 2), NC)
    do0 = mm_nt("l0_out_dx", dy0, w_out[0], _epi_store(BF16), BF16)
    g_wout_0, (sib_w1_0,) = mm_tn("l0_out_dw", o0, dy0, 0, carry=[ex_rs_sibling([g_w1_0])])
    (dq0, dk0, dv0, dsink_p), (rem_w1_0, sib_wout_0) = attn_even_bwd(
        "l0_attn_bwd", qkvh0, sink, do0, NC, TQ_B,
        carry=[ex_rs_chips([chip_sum("w1_0", g_w1_0, sib_w1_0)]), ex_rs_sibling([g_wout_0])])
    dqkv0, pqk = prep_even_bwd("l0_prep_bwd", qkv0, dq0, dk0, dv0, ev_q_norm, ev_k_norm, cos, sa, sb)
    g_win_0, (rem_wout_0,) = mm_tn("l0_qkv_dw", h_a, dqkv0, 1,
                                   carry=[ex_rs_chips([chip_sum("wout_0", g_wout_0, sib_wout_0)])])
    dh_a, (sib_win_0,) = mm_nt("l0_qkv_dx", dqkv0, w_in[0], _epi_store(F32), F32, carry=[ex_rs_sibling([g_win_0])])
    (dx_lat, pn1_0), (rem_win_0,) = norm_bwd(
        "l0_norm1_bwd", X0, dh_a, dX1, nw[0, 0][None], chunk(0, 1), NC, out_skip=NC,
        carry=[ex_rs_chips([chip_sum("win_0", g_win_0, sib_win_0)])])
    grad_x = dx_lat[None]

    def dmod(grp, pn1, pg1, pn2, pg2):
        return jnp.concatenate([pn1[grp], pn1[2 + grp], pg1[grp], pn2[grp], pn2[2 + grp], pg2[grp]])

    dmod_lat = jnp.stack([dmod(1, pn1_0, pg1_0, pn2_0, pg2_0), dmod(1, pn1_1, pg1_1, pn2_1, pg2_1)])
    dmod_ctx = jnp.stack([dmod(0, pn1_0, pg1_0, pn2_0, pg2_0), dmod(0, pn1_1, pg1_1, pn2_1, pg2_1)])
    dnw_p = jnp.stack([pn1_0[4], pn2_0[4], pn1_1[4], pn2_1[4]])
    pieces = [dmod_lat, dmod_ctx, dnw_p, pqk[0], pqk[1], dsink_p[8:, 0, 0], d_rpb, dfw_p[0], loss_p[0, 0]]
    sizes = [int(np.prod(p.shape)) for p in pieces]
    rows = [_rows8(s) for s in sizes]
    pack2 = jnp.concatenate([_pad_rows(p, r) for p, r in zip(pieces, rows)], axis=0)
    g2, s2 = ag_small("ag_small_grads", pack2, with_sum=True)
    offs = np.concatenate([[0], np.cumsum(rows)])

    def piece(arr, i, shape):
        return arr[..., offs[i]:offs[i + 1], :].reshape(arr.shape[:-2] + (-1,))[..., :sizes[i]].reshape(
            arr.shape[:-2] + shape)

    dmod_all = piece(g2, 0, (2, 6 * D))
    dmodc_sum = piece(s2, 1, (2, 6 * D))
    dnw_sum = piece(s2, 2, (2, 2, D))
    g_qn = piece(s2, 3, ev_q_norm.shape)
    g_kn = piece(s2, 4, ev_k_norm.shape)
    g_sink = piece(s2, 5, ev_sink.shape)
    g_rpb = piece(s2, 6, od_rpb.shape)
    g_fw = piece(s2, 7, final_norm_w.shape)
    loss = piece(s2, 8, ())

    dm16 = jnp.concatenate([dmod_all.transpose(1, 0, 2), dmodc_sum[:, None, :],
                            jnp.zeros((2, NDEV - 1, 6 * D), F32)], axis=1)
    dm16_loc = lax.dynamic_slice_in_dim(dm16.reshape(2, 2 * NDEV, NDEV, ada_cols), me, 1, axis=2)[:, :, 0, :]
    g_ada_b = _vmem_call("ada_b_grad", lambda v: jnp.sum(v, axis=1),
                         jax.ShapeDtypeStruct((2, 6 * D), F32), dm16)
    g_ada_w = []
    dact_p = None
    for i in range(2):
        dmb = dm16_loc[i].astype(BF16)
        g_ada_w.append(mm_tn(f"ada_w_grad{i}", act, dmb, None))
        part = mm_nt(f"ada_dact{i}", dmb, ada_w[i], _epi_store(F32), F32)
        dact_p = part if dact_p is None else dact_p + part
    _, dact = ag_small("ag_cctx", dact_p, with_sum=True)

    def cctx_grad(da, cc):
        sg = 1.0 / (1.0 + jnp.exp(-cc))
        return da[NDEV:NDEV + 1] * (sg * (1.0 + cc * (1.0 - sg)))

    g_cctx = _vmem_call("cctx_grad", cctx_grad, jax.ShapeDtypeStruct((1, D), F32), dact, c_ctx[None])[0]

    grads = [g_win_0, g_wout_0, g_w1_0, g_w2_0, g_win_1, g_wout_1, g_w1_1, g_w2_1]
    sib = [sib_win_0, sib_wout_0, sib_w1_0, sib_w2_0, sib_win_1, sib_wout_1, sib_w1_1, sib_w2_1]
    rem = [rem_win_0, rem_wout_0, rem_w1_0, rem_w2_0, rem_win_1, rem_wout_1, rem_w1_1, rem_w2_1]
    own_idx = jnp.stack([mc4 + my_chip, my_chip])

    def big(tag, w, m, v, ts):
        res = None
        for l, t in enumerate(ts):
            res = adamw_rs(f"adamw_{tag}_{l}", w, grads[t], sib[t], rem[t], m, v, own_idx, l, res)
        return tuple(res)

    r_ev_w_in = big('ev_w_in', ev_w_in, m_ev_w_in, v_ev_w_in, [0])
    r_ev_w_out = big('ev_w_out', ev_w_out, m_ev_w_out, v_ev_w_out, [1])
    r_mlp_w1 = big('mlp_w1', mlp_w1, m_mlp_w1, v_mlp_w1, [2, 6])
    r_mlp_w2 = big('mlp_w2', mlp_w2, m_mlp_w2, v_mlp_w2, [3, 7])
    r_od_w_in = big('od_w_in', od_w_in, m_od_w_in, v_od_w_in, [4])
    r_od_w_out = big('od_w_out', od_w_out, m_od_w_out, v_od_w_out, [5])

    g_ada = jnp.stack(g_ada_w)
    r_ada_w = adamw_rows("adamw_ada_w", ada_w.reshape(2 * D, ada_cols), g_ada.reshape(2 * D, ada_cols),
                         m_ada_w.reshape(2 * D, ada_cols), v_ada_w.reshape(2 * D, ada_cols))
    r_ada_w = tuple(u.reshape(2, D, ada_cols) for u in r_ada_w)

    g_nw_loc = lax.dynamic_slice_in_dim(dnw_sum, me * nw_cols, nw_cols, axis=2)
    small = [(c_ctx, g_cctx, m_c_ctx, v_c_ctx), (ada_b, g_ada_b, m_ada_b, v_ada_b),
             (norm_w, g_nw_loc, m_norm_w, v_norm_w), (ev_q_norm, g_qn, m_ev_q_norm, v_ev_q_norm),
             (ev_k_norm, g_kn, m_ev_k_norm, v_ev_k_norm), (ev_sink, g_sink, m_ev_sink, v_ev_sink),
             (od_rpb, g_rpb, m_od_rpb, v_od_rpb), (final_norm_w, g_fw, m_final_norm_w, v_final_norm_w)]
    srows = [_rows8(int(np.prod(w.shape))) for w, _, _, _ in small]
    packs = [jnp.concatenate([_pad_rows(tup[k], r) for tup, r in zip(small, srows)], axis=0) for k in range(4)]
    sres = adamw_rows("adamw_small", *packs)
    soffs = np.concatenate([[0], np.cumsum(srows)])

    def unpack(arr, i):
        w = small[i][0]
        return arr[soffs[i]:soffs[i + 1]].reshape(-1)[:int(np.prod(w.shape))].reshape(w.shape)

    sm = [[unpack(sres[k], i) for i in range(len(small))] for k in range(4)]

    def outs(k):
        big_k = {'ada_w': r_ada_w[k], 'mlp_w1': r_mlp_w1[k], 'mlp_w2': r_mlp_w2[k], 'ev_w_in': r_ev_w_in[k],
                 'ev_w_out': r_ev_w_out[k], 'od_w_in': r_od_w_in[k], 'od_w_out': r_od_w_out[k]}
        return (sm[k][0], big_k['ada_w'], sm[k][1], sm[k][2], big_k['mlp_w1'], big_k['mlp_w2'], big_k['ev_w_in'],
                big_k['ev_w_out'], sm[k][3], sm[k][4], sm[k][5], big_k['od_w_in'], big_k['od_w_out'], sm[k][6],
                sm[k][7])

    return (loss, grad_x, *outs(0), *outs(1), *outs(2), *outs(3))
```

```python
import numpy as np
import jax
import jax.numpy as jnp
from jax import lax
from jax.experimental import pallas as pl
from jax.experimental.pallas import tpu as pltpu

F32 = jnp.float32
BF16 = jnp.bfloat16
MESH = pl.DeviceIdType.MESH

NDEV = 8
HEAD = 128
GRID_W = 64
NA_KH, NA_KW = 8, 16
WINDOW = 128
ROPE_THETA = 10000.0
EPS = 1e-6
NEG = -1e30
SCALE = HEAD ** -0.5
ROW_TILE = 256
VMEM_LIMIT = 56 * 1024 * 1024

ADAM_LR, ADAM_B1, ADAM_B2, ADAM_EPS, ADAM_WD, ADAM_STEP = 0.001, 0.9, 0.999, 1e-08, 0.01, 10

NT = (((1,), (1,)), ((), ()))
NN = (((1,), (0,)), ((), ()))
TN = (((0,), (0,)), ((), ()))


def _cparams(sem):
    return pltpu.CompilerParams(dimension_semantics=sem, vmem_limit_bytes=VMEM_LIMIT)


def _tile(n, cap):
    if n <= cap:
        return n
    t = cap - cap % 64
    while t >= 64:
        if n % t == 0:
            return t
        t -= 64
    raise ValueError((n, cap))


def _dot(a, b, dims):
    return lax.dot_general(a.astype(BF16), b.astype(BF16), dims, preferred_element_type=F32)


def _slot(d):
    return (d % 2) * 4 + d // 2


class Exchange:
    def __init__(self, ins, out_shapes, aliases, n_sems, start, finish):
        self.ins, self.out_shapes, self.aliases, self.n_sems = list(ins), list(out_shapes), dict(aliases), n_sems
        self.start, self.finish = start, finish


def merge_exchanges(xs):
    ins, outs, aliases, bases, n = [], [], {}, [], 0
    for x in xs:
        bases.append((len(ins), len(outs), n))
        aliases.update({len(ins) + i: len(outs) + o for i, o in x.aliases.items()})
        ins += x.ins
        outs += x.out_shapes
        n += x.n_sems

    def run(which):
        def f(ci, co, ss, rs, base):
            for x, (i0, o0, s0) in zip(xs, bases):
                getattr(x, which)(ci[i0:i0 + len(x.ins)], co[o0:o0 + len(x.out_shapes)], ss, rs, base + s0)
        return f

    return Exchange(ins, outs, aliases, n, run('start'), run('finish'))


def _call(name, body, grid, ins, in_specs, out_shape, out_specs, scratch, sems, carry=None):
    if not carry:
        return pl.pallas_call(body, grid=grid, in_specs=in_specs, out_specs=out_specs, out_shape=out_shape,
                              scratch_shapes=scratch, compiler_params=_cparams(sems), name=name)(*ins)
    x = merge_exchanges(carry)
    n_in, n_ci, n_out, n_co, n_sc = len(ins), len(x.ins), len(out_shape), len(x.out_shapes), len(scratch)

    def wrapped(*refs):
        p = [0]

        def take(k):
            p[0] += k
            return refs[p[0] - k:p[0]]

        a, ci, o, co, sc = take(n_in), take(n_ci), take(n_out), take(n_co), take(n_sc)
        ss, rs = take(2)
        first = pl.program_id(0) == 0
        last = pl.program_id(0) == grid[0] - 1
        for d in range(1, len(grid)):
            first = jnp.logical_and(first, pl.program_id(d) == 0)
            last = jnp.logical_and(last, pl.program_id(d) == grid[d] - 1)

        @pl.when(first)
        def _():
            x.start(ci, co, ss, rs, 0)

        body(*a, *o, *sc)

        @pl.when(last)
        def _():
            x.finish(ci, co, ss, rs, 0)

    hbm = pl.BlockSpec(memory_space=pl.ANY)
    res = pl.pallas_call(
        wrapped, grid=grid, in_specs=list(in_specs) + [hbm] * n_ci, out_specs=list(out_specs) + [hbm] * n_co,
        out_shape=list(out_shape) + x.out_shapes,
        input_output_aliases={n_in + i: n_out + o for i, o in x.aliases.items()},
        scratch_shapes=list(scratch) + [pltpu.SemaphoreType.DMA((x.n_sems,)), pltpu.SemaphoreType.DMA((x.n_sems,))],
        compiler_params=_cparams(("arbitrary",) * len(grid)), name=name)(*ins, *x.ins)
    return list(res[:n_out]) + [list(res[n_out:])]


def _mm_core(name, grid, ins, in_specs, out_shape, out_specs, dims, acc_shape, epi, carry=None):
    nk = grid[2]
    n_extra = len(ins) - 2

    def body_single(*refs):
        epi(_dot(refs[0][...], refs[1][...], dims), refs[2:2 + n_extra], refs[2 + n_extra:])

    def body(*refs):
        a_ref, b_ref = refs[0], refs[1]
        ex = refs[2:2 + n_extra]
        outs = refs[2 + n_extra:-1]
        acc = refs[-1]
        k = pl.program_id(2)

        @pl.when(k == 0)
        def _():
            acc[...] = _dot(a_ref[...], b_ref[...], dims)

        @pl.when(jnp.logical_and(k > 0, k < nk - 1))
        def _():
            acc[...] += _dot(a_ref[...], b_ref[...], dims)

        @pl.when(k == nk - 1)
        def _():
            epi(acc[...] + _dot(a_ref[...], b_ref[...], dims), ex, outs)

    if nk == 1:
        return _call(name, body_single, grid, ins, in_specs, out_shape, out_specs, [],
                     ("parallel", "parallel", "arbitrary"), carry)
    return _call(name, body, grid, ins, in_specs, out_shape, out_specs, [pltpu.VMEM(acc_shape, F32)],
                 ("parallel", "parallel", "arbitrary"), carry)


def _split(res, n, carry):
    own = res[0] if n == 1 else list(res[:n])
    return (own, res[n]) if carry else own


def _epi_store(dtype):
    def epi(acc, ex, outs):
        outs[0][...] = acc.astype(dtype)
    return epi


def _epi_bias(acc, ex, outs):
    outs[0][...] = acc + ex[0][...]


def _epi_relu2(acc, ex, outs):
    r = jnp.maximum(acc, 0.0)
    outs[0][...] = (r * r).astype(BF16)
    outs[1][...] = r.astype(BF16)


def _epi_mul2r(acc, ex, outs):
    outs[0][...] = (acc * (2.0 * ex[0][...].astype(F32))).astype(BF16)


def _epi_resid_gate(nctx, tm):
    def epi(acc, ex, outs):
        rows = pl.program_id(0) * tm + lax.broadcasted_iota(jnp.int32, (tm, 1), 0)
        g = jnp.where(rows < nctx, ex[1][0:1, :], ex[1][1:2, :])
        outs[0][...] = ex[0][...] + g * acc
        outs[1][...] = acc
    return epi


def mm_nn(name, a, w, epi, outs, extras=(), extra_kinds=(), tm_cap=1100, tn_cap=512, tk_cap=2048, carry=None):
    M, K = a.shape
    if w.ndim == 3:
        ns = w.shape[2]
        N = NDEV * ns
        tn = _tile(ns, tn_cap)
        nper = ns // tn
    else:
        N = w.shape[1]
        tn = _tile(N, tn_cap)
    tm = _tile(M, tm_cap)
    tk = _tile(K, tk_cap)
    grid = (M // tm, N // tn, K // tk)
    a_spec = pl.BlockSpec((tm, tk), lambda i, j, k: (i, k))
    if w.ndim == 3:
        b_spec = pl.BlockSpec((None, tk, tn), lambda i, j, k: (j // nper, k, j % nper))
    else:
        b_spec = pl.BlockSpec((tk, tn), lambda i, j, k: (k, j))
    ex_specs = []
    for e, kind in zip(extras, extra_kinds):
        if kind == 'mn':
            ex_specs.append(pl.BlockSpec((tm, tn), lambda i, j, k: (i, j)))
        else:
            ex_specs.append(pl.BlockSpec((e.shape[0], tn), lambda i, j, k: (0, j)))
    out_shape = [jax.ShapeDtypeStruct((M, N), dt) for dt in outs]
    out_specs = [pl.BlockSpec((tm, tn), lambda i, j, k: (i, j)) for _ in outs]
    res = _mm_core(name, grid, (a, w, *extras), [a_spec, b_spec, *ex_specs], out_shape, out_specs, NN, (tm, tn), epi,
                   carry)
    return (list(res[:len(outs)]), res[len(outs)]) if carry else res


def mm_nt(name, a, w, epi, out_dtype, extras=(), tm_cap=1100, to_cap=1024, tc_cap=2048, carry=None):
    M, N = a.shape
    tm = _tile(M, tm_cap)
    if w.ndim == 3:
        Kw, ns = w.shape[1], w.shape[2]
        tc = _tile(ns, tc_cap)
        cper = ns // tc
    else:
        Kw = w.shape[0]
        tc = _tile(N, tc_cap)
    to = _tile(Kw, to_cap)
    grid = (M // tm, Kw // to, N // tc)
    a_spec = pl.BlockSpec((tm, tc), lambda i, j, k: (i, k))
    if w.ndim == 3:
        b_spec = pl.BlockSpec((None, to, tc), lambda i, j, k: (k // cper, j, k % cper))
    else:
        b_spec = pl.BlockSpec((to, tc), lambda i, j, k: (j, k))
    ex_specs = [pl.BlockSpec((tm, to), lambda i, j, k: (i, j)) for _ in extras]
    out_shape = [jax.ShapeDtypeStruct((M, Kw), out_dtype)]
    out_specs = [pl.BlockSpec((tm, to), lambda i, j, k: (i, j))]
    return _split(_mm_core(name, grid, (a, w, *extras), [a_spec, b_spec, *ex_specs], out_shape, out_specs, NT, (tm, to),
                           epi, carry), 1, carry)


def mm_tn(name, a, b, shard_axis, to_cap=1024, tn_cap=1024, tc_cap=2200, carry=None):
    M, Ka = a.shape
    N = b.shape[1]
    tc = _tile(M, tc_cap)
    if shard_axis is None:
        to, tn = _tile(Ka, to_cap), _tile(N, tn_cap)
        shape = (Ka, N)
        oblk = (to, tn)
        omap = lambda i, j, k: (i, j)
    elif shard_axis == 1:
        ns = N // NDEV
        to, tn = _tile(Ka, to_cap), _tile(ns, tn_cap)
        per = ns // tn
        shape = (NDEV, Ka, ns)
        oblk = (None, to, tn)
        omap = lambda i, j, k: (_slot(j // per), i, j % per)
    else:
        rs = Ka // NDEV
        to, tn = _tile(rs, to_cap), _tile(N, tn_cap)
        per = rs // to
        shape = (NDEV, rs, N)
        oblk = (None, to, tn)
        omap = lambda i, j, k: (_slot(i // per), i % per, j)
    grid = (Ka // to, N // tn, M // tc)
    a_spec = pl.BlockSpec((tc, to), lambda i, j, k: (k, i))
    b_spec = pl.BlockSpec((tc, tn), lambda i, j, k: (k, j))
    out_shape = [jax.ShapeDtypeStruct(shape, F32)]
    out_specs = [pl.BlockSpec(oblk, omap)]
    return _split(_mm_core(name, grid, (a, b), [a_spec, b_spec], out_shape, out_specs, TN, (to, tn), _epi_store(F32),
                           carry), 1, carry)


def _row_spec(D):
    return pl.BlockSpec((ROW_TILE, D), lambda i: (i, 0))


def _const_spec(r, D):
    return pl.BlockSpec((r, D), lambda i: (0, 0))


def _grp(ref, is_ctx):
    return jnp.where(is_ctx, ref[0:1, :], ref[1:2, :])


def norm_mod(name, x, nw, sh, sc, nctx):
    R, D = x.shape
    assert R % ROW_TILE == 0 and nctx % ROW_TILE == 0

    def body(x_ref, nw_ref, sh_ref, sc_ref, o_ref):
        is_ctx = pl.program_id(0) * ROW_TILE < nctx
        xv = x_ref[...]
        rstd = lax.rsqrt(jnp.mean(xv * xv, axis=-1, keepdims=True) + EPS)
        n = xv * rstd * nw_ref[...]
        o_ref[...] = (n * (1.0 + _grp(sc_ref, is_ctx)) + _grp(sh_ref, is_ctx)).astype(BF16)

    return pl.pallas_call(
        body, grid=(R // ROW_TILE,),
        in_specs=[_row_spec(D), _const_spec(1, D), _const_spec(2, D), _const_spec(2, D)],
        out_specs=_row_spec(D), out_shape=jax.ShapeDtypeStruct((R, D), BF16),
        compiler_params=_cparams(("parallel",)), name=name)(x, nw, sh, sc)


def norm_bwd(name, x, dh, dres, nw, sc, nctx, dres_skip=0, out_skip=0, carry=None):
    R, D = x.shape
    assert R % ROW_TILE == 0 and nctx % ROW_TILE == 0 and dres_skip % ROW_TILE == 0 and out_skip % ROW_TILE == 0
    res_tiles, out_tiles = dres_skip // ROW_TILE, out_skip // ROW_TILE

    def body(x_ref, dh_ref, dres_ref, nw_ref, sc_ref, dx_ref, part_ref):
        i = pl.program_id(0)
        is_ctx = i * ROW_TILE < nctx

        @pl.when(i == 0)
        def _():
            part_ref[...] = jnp.zeros_like(part_ref)

        xv = x_ref[...]
        dhv = dh_ref[...]
        w = nw_ref[...]
        rstd = lax.rsqrt(jnp.mean(xv * xv, axis=-1, keepdims=True) + EPS)
        xhat = xv * rstd
        n = xhat * w
        dn = dhv * (1.0 + _grp(sc_ref, is_ctx))
        dxhat = dn * w
        dres = dres_ref[...]
        if res_tiles:
            dres = jnp.where(i < res_tiles, 0.0, dres)
        dx_ref[...] = dres + rstd * (dxhat - xhat * jnp.mean(dxhat * xhat, axis=-1, keepdims=True))
        s_sh = jnp.sum(dhv, axis=0, keepdims=True)
        s_sc = jnp.sum(dhv * n, axis=0, keepdims=True)
        s_nw = jnp.sum(dn * xhat, axis=0, keepdims=True)
        zero = jnp.zeros_like(s_sh)
        part_ref[0:1, :] += jnp.where(is_ctx, s_sh, zero)
        part_ref[1:2, :] += jnp.where(is_ctx, zero, s_sh)
        part_ref[2:3, :] += jnp.where(is_ctx, s_sc, zero)
        part_ref[3:4, :] += jnp.where(is_ctx, zero, s_sc)
        part_ref[4:5, :] += s_nw

    res = _call(name, body, (R // ROW_TILE,), (x, dh, dres, nw, sc),
                [_row_spec(D), _row_spec(D),
                 pl.BlockSpec((ROW_TILE, D), lambda i: (jnp.maximum(i - res_tiles, 0), 0)),
                 _const_spec(1, D), _const_spec(2, D)],
                [jax.ShapeDtypeStruct((R - out_skip, D), F32), jax.ShapeDtypeStruct((8, D), F32)],
                [pl.BlockSpec((ROW_TILE, D), lambda i: (jnp.maximum(i - out_tiles, 0), 0)), _const_spec(8, D)],
                [], ("arbitrary",), carry)
    return _split(res, 2, carry)


def gate_bwd(name, dx, y, g, nctx):
    R, D = dx.shape
    assert R % ROW_TILE == 0 and nctx % ROW_TILE == 0

    def body(dx_ref, y_ref, g_ref, dy_ref, part_ref):
        i = pl.program_id(0)
        is_ctx = i * ROW_TILE < nctx

        @pl.when(i == 0)
        def _():
            part_ref[...] = jnp.zeros_like(part_ref)

        dxv = dx_ref[...]
        dy_ref[...] = (dxv * _grp(g_ref, is_ctx)).astype(BF16)
        s = jnp.sum(dxv * y_ref[...], axis=0, keepdims=True)
        zero = jnp.zeros_like(s)
        part_ref[0:1, :] += jnp.where(is_ctx, s, zero)
        part_ref[1:2, :] += jnp.where(is_ctx, zero, s)

    return pl.pallas_call(
        body, grid=(R // ROW_TILE,),
        in_specs=[_row_spec(D), _row_spec(D), _const_spec(2, D)],
        out_specs=[_row_spec(D), _const_spec(8, D)],
        out_shape=[jax.ShapeDtypeStruct((R, D), BF16), jax.ShapeDtypeStruct((8, D), F32)],
        compiler_params=_cparams(("arbitrary",)), name=name)(dx, y, g)


def final_loss(name, x, fw, tgt):
    S, D = x.shape

    def body(x_ref, fw_ref, t_ref, dx_ref, loss_ref, dfw_ref):
        i = pl.program_id(0)

        @pl.when(i == 0)
        def _():
            loss_ref[...] = jnp.zeros_like(loss_ref)
            dfw_ref[...] = jnp.zeros_like(dfw_ref)

        xv = x_ref[...]
        w = fw_ref[...]
        rstd = lax.rsqrt(jnp.mean(xv * xv, axis=-1, keepdims=True) + EPS)
        xhat = xv * rstd
        e = xhat * w - t_ref[...]
        loss_ref[...] += 0.5 * jnp.sum(jnp.mean(e * e, axis=-1, keepdims=True))
        dout = e * (1.0 / D)
        dfw_ref[0:1, :] += jnp.sum(dout * xhat, axis=0, keepdims=True)
        dxhat = dout * w
        dx_ref[...] = rstd * (dxhat - xhat * jnp.mean(dxhat * xhat, axis=-1, keepdims=True))

    return pl.pallas_call(
        body, grid=(S // ROW_TILE,),
        in_specs=[_row_spec(D), _const_spec(1, D), _row_spec(D)],
        out_specs=[_row_spec(D), pl.BlockSpec((8, 128), lambda i: (0, 0)), _const_spec(8, D)],
        out_shape=[jax.ShapeDtypeStruct((S, D), F32), jax.ShapeDtypeStruct((8, 128), F32),
                   jax.ShapeDtypeStruct((8, D), F32)],
        compiler_params=_cparams(("arbitrary",)), name=name)(x, fw, tgt)


def _rope(x, cos, sa, sb):
    return x * cos + pltpu.roll(x, 96, 1) * sa + pltpu.roll(x, 32, 1) * sb


def _rope_t(dy, cos, sa, sb):
    return dy * cos + pltpu.roll(dy * sa, 32, 1) + pltpu.roll(dy * sb, 96, 1)


_EVEN_KINDS = ['qa'] * 8 + ['ka'] * 2 + ['v'] * 2 + ['qb'] * 8 + ['kb'] * 2 + ['v'] * 2
_EVEN_DSRC = ([('q', j) for j in range(8)] + [('k', 0), ('k', 1), ('v', 0), ('v', 1)]
              + [('q', 8 + j) for j in range(8)] + [('k', 2), ('k', 3), ('v', 2), ('v', 3)])


def _cols(j):
    return slice(j * HEAD, (j + 1) * HEAD)


def prep_even(name, qkv, qn, kn, cos, sa, sb):
    T, W = qkv.shape

    def body(x_ref, qn_ref, kn_ref, cos_ref, sa_ref, sb_ref, o_ref):
        cos_, sa_, sb_ = cos_ref[...], sa_ref[...], sb_ref[...]
        for j, kind in enumerate(_EVEN_KINDS):
            x = x_ref[:, _cols(j)]
            if kind in ('qa', 'ka'):
                rstd = lax.rsqrt(jnp.mean(x * x, axis=-1, keepdims=True) + EPS)
                x = x * rstd * (qn_ref[...] if kind == 'qa' else kn_ref[...])
            if kind != 'v':
                x = _rope(x, cos_, sa_, sb_)
            o_ref[:, _cols(j)] = x.astype(BF16)

    blk = pl.BlockSpec((ROW_TILE, W), lambda i: (i, 0))
    tab = pl.BlockSpec((ROW_TILE, HEAD), lambda i: (i, 0))
    one = pl.BlockSpec((1, HEAD), lambda i: (0, 0))
    return pl.pallas_call(
        body, grid=(T // ROW_TILE,), in_specs=[blk, one, one, tab, tab, tab], out_specs=blk,
        out_shape=jax.ShapeDtypeStruct(qkv.shape, BF16),
        compiler_params=_cparams(("parallel",)), name=name)(qkv, qn, kn, cos, sa, sb)


def prep_even_bwd(name, qkv, dq, dk, dv, qn, kn, cos, sa, sb):
    T, W = qkv.shape

    def body(x_ref, dq_ref, dk_ref, dv_ref, qn_ref, kn_ref, cos_ref, sa_ref, sb_ref, o_ref, part_ref):
        @pl.when(pl.program_id(0) == 0)
        def _():
            part_ref[...] = jnp.zeros_like(part_ref)

        cos_, sa_, sb_ = cos_ref[...], sa_ref[...], sb_ref[...]
        src = {'q': dq_ref, 'k': dk_ref, 'v': dv_ref}
        sums = {'qa': None, 'ka': None}
        for j, kind in enumerate(_EVEN_KINDS):
            which, blk_j = _EVEN_DSRC[j]
            d = src[which][:, _cols(blk_j)]
            if kind != 'v':
                d = _rope_t(d, cos_, sa_, sb_)
            if kind in ('qa', 'ka'):
                x = x_ref[:, _cols(j)]
                rstd = lax.rsqrt(jnp.mean(x * x, axis=-1, keepdims=True) + EPS)
                xhat = x * rstd
                s = jnp.sum(d * xhat, axis=0, keepdims=True)
                sums[kind] = s if sums[kind] is None else sums[kind] + s
                dxhat = d * (qn_ref[...] if kind == 'qa' else kn_ref[...])
                d = rstd * (dxhat - xhat * jnp.mean(dxhat * xhat, axis=-1, keepdims=True))
            o_ref[:, _cols(j)] = d.astype(BF16)
        part_ref[0:1, :] += sums['qa']
        part_ref[1:2, :] += sums['ka']

    def rows(w):
        return pl.BlockSpec((ROW_TILE, w), lambda i: (i, 0))

    one = pl.BlockSpec((1, HEAD), lambda i: (0, 0))
    return pl.pallas_call(
        body, grid=(T // ROW_TILE,),
        in_specs=[rows(W), rows(dq.shape[1]), rows(dk.shape[1]), rows(dv.shape[1]), one, one,
                  rows(HEAD), rows(HEAD), rows(HEAD)],
        out_specs=[rows(W), pl.BlockSpec((8, HEAD), lambda i: (0, 0))],
        out_shape=[jax.ShapeDtypeStruct(qkv.shape, BF16), jax.ShapeDtypeStruct((8, HEAD), F32)],
        compiler_params=_cparams(("arbitrary",)), name=name)(qkv, dq, dk, dv, qn, kn, cos, sa, sb)


def _even_maps():
    qmap = lambda h, qb: (qb, jnp.where(h < 8, h, h + 4))
    kmap = lambda h, qb: (0, jnp.where(h < 8, 8 + h // 4, 18 + h // 4))
    vmap = lambda h, qb: (0, jnp.where(h < 8, 10 + h // 4, 20 + h // 4))
    return qmap, kmap, vmap


def _softmax_parts(parts, extra=None):
    m = parts[0].max(axis=-1, keepdims=True)
    for p in parts[1:]:
        m = jnp.maximum(m, p.max(axis=-1, keepdims=True))
    if extra is not None:
        m = jnp.maximum(m, extra)
    es = [jnp.exp(p - m) for p in parts]
    l = es[0].sum(axis=-1, keepdims=True)
    for e in es[1:]:
        l = l + e.sum(axis=-1, keepdims=True)
    ex = None
    if extra is not None:
        ex = jnp.exp(extra - m)
        l = l + ex
    inv = 1.0 / l
    return [e * inv for e in es], (None if ex is None else ex * inv)


def _win_scores(q, k_ref, qb, tq, nctx, S):
    L = tq + 2 * WINDOW
    nqc = nctx // tq
    qlat = (qb - nqc) * tq
    start = pl.multiple_of(jnp.clip(qlat - WINDOW, 0, S - L), 128)
    kc = k_ref[0:nctx, :]
    kw = k_ref[pl.ds(nctx + start, L), :]
    s_c = _dot(q, kc, NT) * SCALE
    s_w = _dot(q, kw, NT) * SCALE
    qpos = qlat + lax.broadcasted_iota(jnp.int32, (tq, 1), 0)
    kpos = start + lax.broadcasted_iota(jnp.int32, (1, L), 1)
    valid = jnp.logical_and(jnp.abs(kpos - qpos) <= WINDOW, qb >= nqc)
    return s_c, jnp.where(valid, s_w, NEG), start, L


def _softmax_raw(raw):
    m = raw.max(axis=-1, keepdims=True)
    e = jnp.exp2((raw - m) * (SCALE * np.log2(np.e)))
    return e * (1.0 / e.sum(axis=-1, keepdims=True))


def _glob_keys(qb, tq, nctx, T):
    is_ctx = qb < nctx // tq
    return [(is_ctx, slice(0, nctx)), (jnp.logical_not(is_ctx), slice(0, T))]


def attn_even_fwd(name, qkvh, sink, nctx, tq, carry=None):
    T = qkvh.shape[0]
    S = T - nctx
    qmap, kmap, vmap = _even_maps()

    def body(sink_ref, q_ref, k_ref, v_ref, o_ref):
        h, qb = pl.program_id(0), pl.program_id(1)
        q = q_ref[...]

        for pred, keys in _glob_keys(qb, tq, nctx, T):
            @pl.when(jnp.logical_and(h < 8, pred))
            def _():
                p = _softmax_raw(_dot(q, k_ref[keys, :], NT))
                o_ref[...] = _dot(p, v_ref[keys, :], NN).astype(BF16)

        @pl.when(h >= 8)
        def _():
            s_c, s_w, start, L = _win_scores(q, k_ref, qb, tq, nctx, S)
            sk = jnp.full((tq, 1), sink_ref[jnp.maximum(h - 8, 0)], F32)
            (p_c, p_w), _ = _softmax_parts([s_c, s_w], sk)
            o = _dot(p_c, v_ref[0:nctx, :], NN) + _dot(p_w, v_ref[pl.ds(nctx + start, L), :], NN)
            o_ref[...] = o.astype(BF16)

    res = _call(name, body, (16, T // tq), (sink, qkvh, qkvh, qkvh),
                [pl.BlockSpec(memory_space=pltpu.SMEM), pl.BlockSpec((tq, HEAD), qmap),
                 pl.BlockSpec((T, HEAD), kmap), pl.BlockSpec((T, HEAD), vmap)],
                [jax.ShapeDtypeStruct((T, 16 * HEAD), BF16)], [pl.BlockSpec((tq, HEAD), lambda h, qb: (qb, h))],
                [], ("parallel", "arbitrary"), carry)
    return _split(res, 1, carry)


def attn_even_bwd(name, qkvh, sink, do, nctx, tq, carry=None):
    T = qkvh.shape[0]
    S = T - nctx
    qmap, kmap, vmap = _even_maps()

    def body(sink_ref, q_ref, k_ref, v_ref, do_ref, dq_ref, dk_ref, dv_ref, ds_ref):
        h, qb = pl.program_id(0), pl.program_id(1)
        q = q_ref[...]
        dov = do_ref[...]

        @pl.when(jnp.logical_and(h % 4 == 0, qb == 0))
        def _():
            dk_ref[...] = jnp.zeros_like(dk_ref)
            dv_ref[...] = jnp.zeros_like(dv_ref)

        @pl.when(qb == 0)
        def _():
            ds_ref[...] = jnp.zeros_like(ds_ref)

        for pred, keys in _glob_keys(qb, tq, nctx, T):
            @pl.when(jnp.logical_and(h < 8, pred))
            def _():
                p = _softmax_raw(_dot(q, k_ref[keys, :], NT))
                dp = _dot(dov, v_ref[keys, :], NT)
                row = jnp.sum(p * dp, axis=-1, keepdims=True)
                dsb = (p * (dp - row) * SCALE).astype(BF16)
                dq_ref[...] = _dot(dsb, k_ref[keys, :], NN)
                dk_ref[keys, :] += _dot(dsb, q, TN)
                dv_ref[keys, :] += _dot(p, dov, TN)

        @pl.when(h >= 8)
        def _():
            s_c, s_w, start, L = _win_scores(q, k_ref, qb, tq, nctx, S)
            sk = jnp.full((tq, 1), sink_ref[jnp.maximum(h - 8, 0)], F32)
            (p_c, p_w), p_s = _softmax_parts([s_c, s_w], sk)
            win = pl.ds(nctx + start, L)
            dp_c = _dot(dov, v_ref[0:nctx, :], NT)
            dp_w = _dot(dov, v_ref[win, :], NT)
            row = jnp.sum(p_c * dp_c, axis=-1, keepdims=True) + jnp.sum(p_w * dp_w, axis=-1, keepdims=True)
            ds_c = (p_c * (dp_c - row) * SCALE).astype(BF16)
            ds_w = (p_w * (dp_w - row) * SCALE).astype(BF16)
            dq_ref[...] = _dot(ds_c, k_ref[0:nctx, :], NN) + _dot(ds_w, k_ref[win, :], NN)
            dk_ref[0:nctx, :] += _dot(ds_c, q, TN)
            dk_ref[win, :] += _dot(ds_w, q, TN)
            dv_ref[0:nctx, :] += _dot(p_c, dov, TN)
            dv_ref[win, :] += _dot(p_w, dov, TN)
            ds_ref[...] += jnp.sum(-(p_s * row))

    kv_out = pl.BlockSpec((T, HEAD), lambda h, qb: (0, h // 4))
    res = _call(name, body, (16, T // tq), (sink, qkvh, qkvh, qkvh, do),
                [pl.BlockSpec(memory_space=pltpu.SMEM), pl.BlockSpec((tq, HEAD), qmap),
                 pl.BlockSpec((T, HEAD), kmap), pl.BlockSpec((T, HEAD), vmap),
                 pl.BlockSpec((tq, HEAD), lambda h, qb: (qb, h))],
                [jax.ShapeDtypeStruct((T, 16 * HEAD), F32), jax.ShapeDtypeStruct((T, 4 * HEAD), F32),
                 jax.ShapeDtypeStruct((T, 4 * HEAD), F32), jax.ShapeDtypeStruct((16, 8, 128), F32)],
                [pl.BlockSpec((tq, HEAD), lambda h, qb: (qb, h)), kv_out, kv_out,
                 pl.BlockSpec((None, 8, 128), lambda h, qb: (h, 0, 0))],
                [], ("arbitrary", "arbitrary"), carry)
    return _split(res, 4, carry)


NA_GROUP = 4
NA_SPAN = NA_KH + NA_GROUP - 1
_NA_PLAN = [[(j, 0) for j in range(NA_GROUP)],
            [(NA_KH // 2, j) for j in range(NA_GROUP)],
            [(NA_KH // 2 + j, NA_GROUP - 1) for j in range(NA_GROUP)]]


def _na_group(g, n_groups, rows):
    last = g == n_groups - 1
    kind = jnp.where(g == 0, 0, jnp.where(last, 2, 1))
    first_row = jnp.where(g == 0, 0, jnp.where(last, rows - NA_SPAN, NA_GROUP * g - NA_KH // 2))
    return kind, first_row


def na_span_bias(bias8):
    LW, LS = NA_KH * GRID_W, NA_SPAN * GRID_W
    kinds = []
    for plan in _NA_PLAN:
        strips = [jnp.pad(bias8[:, off], ((0, 0), (0, 0), (s * GRID_W, LS - LW - s * GRID_W)), constant_values=NEG)
                  for off, s in plan]
        kinds.append(jnp.concatenate(strips, axis=1))
    return jnp.stack(kinds, axis=1)


def na_span_bias_grad(db):
    LW = NA_KH * GRID_W
    out = [None] * NA_KH
    for kind, plan in enumerate(_NA_PLAN):
        for j, (off, s) in enumerate(plan):
            piece = db[:, kind, j * GRID_W:(j + 1) * GRID_W, s * GRID_W:s * GRID_W + LW]
            out[off] = piece if out[off] is None else out[off] + piece
    return jnp.stack(out, axis=1)


def _na_specs(T, nctx, n_groups, rows):
    LS = NA_SPAN * GRID_W
    tq = NA_GROUP * GRID_W
    assert nctx % tq == 0 and n_groups >= 3
    q_spec = pl.BlockSpec((tq, HEAD), lambda h, g: (g + nctx // tq, h))
    k_spec = pl.BlockSpec((T, HEAD), lambda h, g: (0, 16 + h))
    v_spec = pl.BlockSpec((T, HEAD), lambda h, g: (0, 32 + h))
    b_spec = pl.BlockSpec((None, None, tq, LS), lambda h, g: (h, _na_group(g, n_groups, rows)[0], 0, 0))
    row_spec = pl.BlockSpec((tq, HEAD), lambda h, g: (g, h))
    return q_spec, k_spec, v_spec, b_spec, row_spec


def _na_scores(q, k_ref, b_ref, g, n_groups, rows, nctx):
    first_row = _na_group(g, n_groups, rows)[1]
    win = pl.ds(pl.multiple_of(nctx + first_row * GRID_W, GRID_W), NA_SPAN * GRID_W)
    s_c = _dot(q, k_ref[0:nctx, :], NT) * SCALE
    s_w = _dot(q, k_ref[win, :], NT) * SCALE + b_ref[...]
    return s_c, s_w, win


def attn_odd_fwd(name, qkv, bias_s, nctx, carry=None):
    T = qkv.shape[0]
    S = T - nctx
    rows = S // GRID_W
    n_groups = rows // NA_GROUP
    q_spec, k_spec, v_spec, b_spec, row_spec = _na_specs(T, nctx, n_groups, rows)

    def body(q_ref, k_ref, v_ref, b_ref, o_ref):
        s_c, s_w, win = _na_scores(q_ref[...], k_ref, b_ref, pl.program_id(1), n_groups, rows, nctx)
        (p_c, p_w), _ = _softmax_parts([s_c, s_w])
        o_ref[...] = (_dot(p_c, v_ref[0:nctx, :], NN) + _dot(p_w, v_ref[win, :], NN)).astype(BF16)

    res = _call(name, body, (16, n_groups), (qkv, qkv, qkv, bias_s), [q_spec, k_spec, v_spec, b_spec],
                [jax.ShapeDtypeStruct((S, 16 * HEAD), BF16)], [row_spec], [], ("parallel", "arbitrary"), carry)
    return _split(res, 1, carry)


def attn_odd_bwd(name, qkv, bias_s, do, nctx, carry=None):
    T = qkv.shape[0]
    S = T - nctx
    rows = S // GRID_W
    n_groups = rows // NA_GROUP
    q_spec, k_spec, v_spec, b_spec, row_spec = _na_specs(T, nctx, n_groups, rows)

    def body(q_ref, k_ref, v_ref, b_ref, do_ref, dq_ref, dk_ref, dv_ref, db_ref):
        g = pl.program_id(1)
        q = q_ref[...]
        dov = do_ref[...]

        @pl.when(g == 0)
        def _():
            dk_ref[...] = jnp.zeros_like(dk_ref)
            dv_ref[...] = jnp.zeros_like(dv_ref)

        s_c, s_w, win = _na_scores(q, k_ref, b_ref, g, n_groups, rows, nctx)
        (p_c, p_w), _ = _softmax_parts([s_c, s_w])
        dp_c = _dot(dov, v_ref[0:nctx, :], NT)
        dp_w = _dot(dov, v_ref[win, :], NT)
        row = jnp.sum(p_c * dp_c, axis=-1, keepdims=True) + jnp.sum(p_w * dp_w, axis=-1, keepdims=True)
        dsw = p_w * (dp_w - row)
        first_visit = jnp.logical_or(g <= 1, g == n_groups - 1)

        @pl.when(first_visit)
        def _():
            db_ref[...] = dsw

        @pl.when(jnp.logical_not(first_visit))
        def _():
            db_ref[...] += dsw

        ds_c = (p_c * (dp_c - row) * SCALE).astype(BF16)
        ds_w = (dsw * SCALE).astype(BF16)
        dq_ref[...] = _dot(ds_c, k_ref[0:nctx, :], NN) + _dot(ds_w, k_ref[win, :], NN)
        dk_ref[0:nctx, :] += _dot(ds_c, q, TN)
        dk_ref[win, :] += _dot(ds_w, q, TN)
        dv_ref[0:nctx, :] += _dot(p_c, dov, TN)
        dv_ref[win, :] += _dot(p_w, dov, TN)

    kv_out = pl.BlockSpec((T, HEAD), lambda h, g: (0, h))
    res = _call(name, body, (16, n_groups), (qkv, qkv, qkv, bias_s, do), [q_spec, k_spec, v_spec, b_spec, row_spec],
                [jax.ShapeDtypeStruct((S, 16 * HEAD), F32), jax.ShapeDtypeStruct((T, 16 * HEAD), F32),
                 jax.ShapeDtypeStruct((T, 16 * HEAD), F32), jax.ShapeDtypeStruct(bias_s.shape, F32)],
                [row_spec, kv_out, kv_out, b_spec], [], ("arbitrary", "arbitrary"), carry)
    return _split(res, 4, carry)


def _na_onehots():
    o = np.arange(NA_KH)[:, None]
    i = np.arange(NA_KH)[None, :]
    a = i - o + NA_KH - 1
    A = (a[..., None] == np.arange(2 * NA_KH - 1)).astype(np.float32)
    qc = np.arange(GRID_W)[:, None]
    kc = np.arange(GRID_W)[None, :]
    b = np.clip(kc - qc + NA_KW - 1, 0, 2 * NA_KW - 2)
    cs = np.clip(qc - NA_KW // 2, 0, GRID_W - NA_KW)
    valid = (kc >= cs) & (kc < cs + NA_KW)
    B = ((b[..., None] == np.arange(2 * NA_KW - 1)) & valid[..., None]).astype(np.float32)
    return A, B, valid


def na_bias_table(rpb):
    A, B, valid = _na_onehots()
    hp = lax.Precision.HIGHEST
    t = jnp.einsum('hab,oia->hoib', rpb, jnp.asarray(A), precision=hp)
    bias = jnp.einsum('hoib,qkb->hoqik', t, jnp.asarray(B), precision=hp)
    bias = jnp.where(jnp.asarray(valid)[None, None, :, None, :], bias, NEG)
    return bias.reshape(rpb.shape[0], NA_KH, GRID_W, NA_KH * GRID_W)


def na_bias_grad(name, dbias8):
    A, B, _ = _na_onehots()
    H = dbias8.shape[0]
    nb, na = 2 * NA_KW - 1, 2 * NA_KH - 1
    d = dbias8.reshape(H, NA_KH, GRID_W, NA_KH, GRID_W).transpose(0, 1, 3, 2, 4)
    d = d.reshape(H * NA_KH * NA_KH, GRID_W * GRID_W)
    Bp = np.zeros((GRID_W * GRID_W, 128), np.float32)
    Bp[:, :nb] = B.reshape(GRID_W * GRID_W, nb)
    Ap = np.zeros((16, NA_KH * NA_KH), np.float32)
    Ap[:na] = A.reshape(NA_KH * NA_KH, na).T
    rows_per_head = NA_KH * NA_KH

    def split3(x):
        hi = x.astype(BF16)
        r1 = x - hi.astype(F32)
        mid = r1.astype(BF16)
        return hi, mid, (r1 - mid.astype(F32)).astype(BF16)

    def body(d_ref, b_ref, a_ref, o_ref):
        bm, am = b_ref[...], a_ref[...]
        g = sum(lax.dot_general(p, bm, NN, preferred_element_type=F32) for p in split3(d_ref[...]))
        o_ref[...] = sum(lax.dot_general(am, p, NN, preferred_element_type=F32) for p in split3(g))

    out = pl.pallas_call(
        body, grid=(H,),
        in_specs=[pl.BlockSpec((rows_per_head, GRID_W * GRID_W), lambda h: (h, 0)),
                  pl.BlockSpec((GRID_W * GRID_W, 128), lambda h: (0, 0)),
                  pl.BlockSpec((16, rows_per_head), lambda h: (0, 0))],
        out_specs=pl.BlockSpec((None, 16, 128), lambda h: (h, 0, 0)),
        out_shape=jax.ShapeDtypeStruct((H, 16, 128), F32),
        compiler_params=_cparams(("parallel",)), name=name)(d, jnp.asarray(Bp, BF16), jnp.asarray(Ap, BF16))
    return out[:, :na, :nb]


def _vmem_call(name, fn, out_shape, *arrays):
    def body(*refs):
        n = len(arrays)
        res = fn(*[r[...] for r in refs[:n]])
        if not isinstance(res, (tuple, list)):
            res = (res,)
        for o, v in zip(refs[n:], res):
            o[...] = v
    return pl.pallas_call(body, out_shape=out_shape, name=name,
                          compiler_params=pltpu.CompilerParams(vmem_limit_bytes=VMEM_LIMIT))(*arrays)


def _silu(v):
    return v / (1.0 + jnp.exp(-v))


def _adamw_math(w, g, m, v):
    m2 = ADAM_B1 * m + (1.0 - ADAM_B1) * g
    v2 = ADAM_B2 * v + (1.0 - ADAM_B2) * (g * g)
    m_hat = m2 / (1.0 - ADAM_B1 ** ADAM_STEP)
    v_hat = v2 / (1.0 - ADAM_B2 ** ADAM_STEP)
    delta = -ADAM_LR * (m_hat / (jnp.sqrt(v_hat) + ADAM_EPS) + ADAM_WD * w)
    return delta, m2, v2


def _ew_tile(R, C):
    return _tile(R, max(64, (262144 // C) // 64 * 64))


def adamw_rows(name, w, g, m, v, extra_g=None):
    R, C = w.shape
    tr = _ew_tile(R, C)
    extra_g = list(extra_g or [])
    ne = len(extra_g)

    def body(*refs):
        w_ref, g_ref, m_ref, v_ref = refs[:4]
        gs = g_ref[...]
        for e in refs[4:4 + ne]:
            gs = gs + e[...].astype(F32)
        go, do, mo, vo = refs[4 + ne:]
        d, m2, v2 = _adamw_math(w_ref[...], gs, m_ref[...], v_ref[...])
        go[...] = gs
        do[...] = d
        mo[...] = m2
        vo[...] = v2

    spec = pl.BlockSpec((tr, C), lambda i: (i, 0))
    return pl.pallas_call(
        body, grid=(R // tr,), in_specs=[spec] * (4 + ne), out_specs=[spec] * 4,
        out_shape=[jax.ShapeDtypeStruct((R, C), F32)] * 4,
        compiler_params=_cparams(("parallel",)), name=name)(w, g, m, v, *extra_g)


def rs_chip_sum(name, g8, sib4, half):
    _, R, C = g8.shape
    tr = _ew_tile(R, C)

    def body(s_ref, g_ref, b_ref, o_ref):
        o_ref[...] = (g_ref[...] + b_ref[...]).astype(BF16)

    blk = (None, tr, C)
    grid_spec = pltpu.PrefetchScalarGridSpec(
        num_scalar_prefetch=1, grid=(4, R // tr),
        in_specs=[pl.BlockSpec(blk, lambda q, i, s: (s[0] + q, i, 0)), pl.BlockSpec(blk, lambda q, i, s: (q, i, 0))],
        out_specs=pl.BlockSpec(blk, lambda q, i, s: (q, i, 0)))
    return pl.pallas_call(body, grid_spec=grid_spec, out_shape=jax.ShapeDtypeStruct((4, R, C), BF16),
                          compiler_params=_cparams(("parallel", "parallel")), name=name)(half, g8, sib4)


def adamw_rs(name, w, g8, sib4, rem3, m, v, idx, layer, prev=None):
    L, R, C = w.shape
    tr = _ew_tile(R, C)

    def body(s_ref, w_ref, g_ref, sb_ref, r0_ref, r1_ref, r2_ref, m_ref, v_ref, *rest):
        go, do, mo, vo = rest[-4:]
        gs = g_ref[...] + sb_ref[...]
        for r_ref in (r0_ref, r1_ref, r2_ref):
            gs = gs + r_ref[...].astype(F32)
        d, m2, v2 = _adamw_math(w_ref[...], gs, m_ref[...], v_ref[...])
        go[...] = gs
        do[...] = d
        mo[...] = m2
        vo[...] = v2

    blk = (None, tr, C)
    mine = pl.BlockSpec(blk, lambda i, s: (layer, i, 0))

    def rem(k):
        return pl.BlockSpec(blk, lambda i, s: (k, i, 0))

    prev = list(prev or [])
    grid_spec = pltpu.PrefetchScalarGridSpec(
        num_scalar_prefetch=1, grid=(R // tr,),
        in_specs=[mine, pl.BlockSpec(blk, lambda i, s: (s[0], i, 0)), pl.BlockSpec(blk, lambda i, s: (s[1], i, 0)),
                  rem(0), rem(1), rem(2), mine, mine] + [pl.BlockSpec(memory_space=pl.ANY)] * len(prev),
        out_specs=[mine] * 4)
    return pl.pallas_call(body, grid_spec=grid_spec, out_shape=[jax.ShapeDtypeStruct((L, R, C), F32)] * 4,
                          input_output_aliases={9 + k: k for k in range(len(prev))},
                          compiler_params=_cparams(("parallel",)), name=name)(
                              idx, w, g8, sib4, rem3, rem3, rem3, m, v, *prev)


def _me():
    x, y, c = lax.axis_index("x"), lax.axis_index("y"), lax.axis_index("c")
    return x, y, c


def _flip(v, bit):
    return 1 - v if bit else v


def ag_small(name, x, with_sum=False):
    R, C = x.shape

    def body(x_ref, out_ref, *rest):
        if with_sum:
            sum_ref, send_sems, recv_sems, lsem = rest
        else:
            send_sems, recv_sems, lsem = rest
        mx, my, mc = _me()
        me = 4 * mx + 2 * my + mc
        local = pltpu.make_async_copy(x_ref, out_ref.at[me], lsem)
        local.start()
        sends = []
        for k in range(1, NDEV):
            peer = (_flip(mx, k & 4), _flip(my, k & 2), _flip(mc, k & 1))
            cp = pltpu.make_async_remote_copy(src_ref=x_ref, dst_ref=out_ref.at[me], send_sem=send_sems.at[k - 1],
                                              recv_sem=recv_sems.at[k - 1], device_id=peer, device_id_type=MESH)
            cp.start()
            sends.append(cp)
        for k in range(1, NDEV):
            px, py, pc = _flip(mx, k & 4), _flip(my, k & 2), _flip(mc, k & 1)
            pltpu.make_async_remote_copy(src_ref=x_ref, dst_ref=out_ref.at[4 * px + 2 * py + pc],
                                         send_sem=send_sems.at[k - 1], recv_sem=recv_sems.at[k - 1],
                                         device_id=(px, py, pc), device_id_type=MESH).wait_recv()
        for cp in sends:
            cp.wait_send()
        local.wait()
        if with_sum:
            acc = out_ref[0]
            for d in range(1, NDEV):
                acc = acc + out_ref[d]
            sum_ref[...] = acc

    out_shape = [jax.ShapeDtypeStruct((NDEV, R, C), F32)]
    if with_sum:
        out_shape.append(jax.ShapeDtypeStruct((R, C), F32))
    vm = pl.BlockSpec(memory_space=pltpu.VMEM)
    res = pl.pallas_call(
        body, out_shape=out_shape, in_specs=[vm], out_specs=[vm] * len(out_shape),
        scratch_shapes=[pltpu.SemaphoreType.DMA((NDEV - 1,)), pltpu.SemaphoreType.DMA((NDEV - 1,)),
                        pltpu.SemaphoreType.DMA],
        compiler_params=pltpu.CompilerParams(vmem_limit_bytes=VMEM_LIMIT), name=name)(x)
    return res if with_sum else res[0]


def ag_big(name, shards):
    n = len(shards)

    def body(*refs):
        ins, outs = refs[:n], refs[n:2 * n]
        send_sems, recv_sems, lsems = refs[2 * n:]
        mx, my, mc = _me()
        me = (mx, my, mc)
        sibling = (mx, my, 1 - mc)
        chips = [(1 - mx, my), (mx, 1 - my), (1 - mx, 1 - my)]

        def idx(p):
            return 4 * p[0] + 2 * p[1] + p[2]

        def copy(t, k, block, to, src=None):
            dst = outs[t].at[idx(block)]
            return pltpu.make_async_remote_copy(
                src_ref=dst if src is None else src, dst_ref=dst, send_sem=send_sems.at[7 * t + k],
                recv_sem=recv_sems.at[7 * t + k], device_id=to, device_id_type=MESH)

        started = []
        locals_ = []
        for t in range(n):
            mine = pltpu.make_async_copy(ins[t], outs[t].at[idx(me)], lsems.at[t])
            mine.start()
            locals_.append(mine)
            first = [copy(t, 0, me, sibling, src=ins[t])]
            first += [copy(t, 1 + j, me, (*chip, mc), src=ins[t]) for j, chip in enumerate(chips)]
            for cp in first:
                cp.start()
            started += first
        for t in range(n):
            for j, chip in enumerate(chips):
                copy(t, 1 + j, (*chip, mc), me).wait_recv()
                fwd = copy(t, 4 + j, (*chip, mc), sibling)
                fwd.start()
                started.append(fwd)
        for t in range(n):
            copy(t, 0, sibling, me).wait_recv()
            for j, chip in enumerate(chips):
                copy(t, 4 + j, (*chip, 1 - mc), me).wait_recv()
        for cp in started:
            cp.wait_send()
        for mine in locals_:
            mine.wait()

    anyspec = pl.BlockSpec(memory_space=pl.ANY)
    return pl.pallas_call(
        body, out_shape=[jax.ShapeDtypeStruct((NDEV,) + s.shape, s.dtype) for s in shards],
        in_specs=[anyspec] * n, out_specs=[anyspec] * n,
        scratch_shapes=[pltpu.SemaphoreType.DMA((7 * n,)), pltpu.SemaphoreType.DMA((7 * n,)),
                        pltpu.SemaphoreType.DMA((n,))],
        name=name)(*shards)


def _idx(p):
    return 4 * p[0] + 2 * p[1] + p[2]


def _remote(src, dst, ss, rs, k, to):
    return pltpu.make_async_remote_copy(src_ref=src, dst_ref=dst, send_sem=ss.at[k], recv_sem=rs.at[k],
                                        device_id=to, device_id_type=MESH)


def ex_ag_chips(shards):
    n = len(shards)

    def copies(ci, co, ss, rs, base):
        mx, my, mc = _me()
        me = (mx, my, mc)
        peers = [(mx, my, 1 - mc), (1 - mx, my, mc), (mx, 1 - my, mc), (1 - mx, 1 - my, mc)]
        sends, recvs, local = [], [], []
        for t in range(n):
            b = base + 5 * t
            for k, peer in enumerate(peers):
                sends.append(_remote(ci[t], co[t].at[_idx(me)], ss, rs, b + k, peer))
                recvs.append(_remote(ci[t], co[t].at[_idx(peer)], ss, rs, b + k, peer))
            local.append(pltpu.make_async_copy(ci[t], co[t].at[_idx(me)], ss.at[b + 4]))
        return sends, recvs, local

    def start(ci, co, ss, rs, base):
        sends, _, local = copies(ci, co, ss, rs, base)
        for cp in local + sends:
            cp.start()

    def finish(ci, co, ss, rs, base):
        sends, recvs, local = copies(ci, co, ss, rs, base)
        for cp in recvs:
            cp.wait_recv()
        for cp in sends:
            cp.wait_send()
        for cp in local:
            cp.wait()

    outs = [jax.ShapeDtypeStruct((NDEV,) + s.shape, s.dtype) for s in shards]
    return Exchange(shards, outs, {}, 5 * n, start, finish)


def ex_ag_sibling(bufs):
    n = len(bufs)

    def copies(co, ss, rs, base):
        mx, my, mc = _me()
        sibling = (mx, my, 1 - mc)
        chips = [(1 - mx, my), (mx, 1 - my), (1 - mx, 1 - my)]
        sends, recvs = [], []
        for t in range(n):
            for j, chip in enumerate(chips):
                mine, theirs = co[t].at[_idx((*chip, mc))], co[t].at[_idx((*chip, 1 - mc))]
                sends.append(_remote(mine, mine, ss, rs, base + 3 * t + j, sibling))
                recvs.append(_remote(mine, theirs, ss, rs, base + 3 * t + j, sibling))
        return sends, recvs

    def start(ci, co, ss, rs, base):
        for cp in copies(co, ss, rs, base)[0]:
            cp.start()

    def finish(ci, co, ss, rs, base):
        sends, recvs = copies(co, ss, rs, base)
        for cp in recvs:
            cp.wait_recv()
        for cp in sends:
            cp.wait_send()

    outs = [jax.ShapeDtypeStruct(b.shape, b.dtype) for b in bufs]
    return Exchange(bufs, outs, {t: t for t in range(n)}, 3 * n, start, finish)


def ex_rs_sibling(grads):
    n = len(grads)

    def copies(ci, co, ss, rs, base):
        mx, my, mc = _me()
        return [_remote(ci[t].at[pl.ds((1 - mc) * 4, 4)], co[t], ss, rs, base + t, (mx, my, 1 - mc)) for t in range(n)]

    def start(ci, co, ss, rs, base):
        for cp in copies(ci, co, ss, rs, base):
            cp.start()

    def finish(ci, co, ss, rs, base):
        for cp in copies(ci, co, ss, rs, base):
            cp.wait()

    outs = [jax.ShapeDtypeStruct((4,) + g.shape[1:], g.dtype) for g in grads]
    return Exchange(grads, outs, {}, n, start, finish)


def ex_rs_chips(parts):
    n = len(parts)

    def copies(ci, co, ss, rs, base):
        mx, my, mc = _me()
        cps = []
        for t in range(n):
            for k in range(1, 4):
                px, py = _flip(mx, k & 2), _flip(my, k & 1)
                cps.append(_remote(ci[t].at[2 * px + py], co[t].at[k - 1], ss, rs, base + 3 * t + k - 1, (px, py, mc)))
        return cps

    def start(ci, co, ss, rs, base):
        for cp in copies(ci, co, ss, rs, base):
            cp.start()

    def finish(ci, co, ss, rs, base):
        for cp in copies(ci, co, ss, rs, base):
            cp.wait()

    outs = [jax.ShapeDtypeStruct((3,) + p.shape[1:], p.dtype) for p in parts]
    return Exchange(parts, outs, {}, 3 * n, start, finish)


def run_exchanges(name, xs):
    x = merge_exchanges(xs)
    n_ci, n_co = len(x.ins), len(x.out_shapes)

    def body(*refs):
        ci, co = refs[:n_ci], refs[n_ci:n_ci + n_co]
        ss, rs = refs[n_ci + n_co:]
        x.start(ci, co, ss, rs, 0)
        x.finish(ci, co, ss, rs, 0)

    hbm = pl.BlockSpec(memory_space=pl.ANY)
    return pl.pallas_call(
        body, out_shape=x.out_shapes, in_specs=[hbm] * n_ci, out_specs=[hbm] * n_co, input_output_aliases=x.aliases,
        scratch_shapes=[pltpu.SemaphoreType.DMA((x.n_sems,)), pltpu.SemaphoreType.DMA((x.n_sems,))], name=name)(*x.ins)


def _rope_tables(S, nctx):
    t = jnp.arange(S)
    row = (t // GRID_W).astype(F32)
    col = (t % GRID_W).astype(F32)
    pairs = HEAD // 4
    inv = ROPE_THETA ** (-jnp.arange(pairs, dtype=F32) / pairs)
    ang_r = row[:, None] * inv
    ang_c = col[:, None] * inv
    ang = jnp.concatenate([ang_r, ang_r, ang_c, ang_c], axis=-1)
    cos = jnp.concatenate([jnp.ones((nctx, HEAD), F32), jnp.cos(ang)], axis=0)
    sin = jnp.concatenate([jnp.zeros((nctx, HEAD), F32), jnp.sin(ang)], axis=0)
    lane = jnp.arange(HEAD)[None, :]
    first = (lane & 32) == 0
    return cos, jnp.where(first, -sin, 0.0), jnp.where(first, 0.0, sin)


def _pad_rows(v, rows):
    v = v.reshape(-1).astype(F32)
    return jnp.pad(v, (0, rows * 128 - v.shape[0])).reshape(rows, 128)


def _rows8(n):
    return -(-n // 1024) * 8


def kernel(x, c, ctx, c_ctx, ada_w, ada_b, norm_w, mlp_w1, mlp_w2, ev_w_in, ev_w_out, ev_q_norm, ev_k_norm, ev_sink, od_w_in, od_w_out, od_rpb, final_norm_w, loss_target, m_c_ctx, m_ada_w, m_ada_b, m_norm_w, m_mlp_w1, m_mlp_w2, m_ev_w_in, m_ev_w_out, m_ev_q_norm, m_ev_k_norm, m_ev_sink, m_od_w_in, m_od_w_out, m_od_rpb, m_final_norm_w, v_c_ctx, v_ada_w, v_ada_b, v_norm_w, v_mlp_w1, v_mlp_w2, v_ev_w_in, v_ev_w_out, v_ev_q_norm, v_ev_k_norm, v_ev_sink, v_od_w_in, v_od_w_out, v_od_rpb, v_final_norm_w):
    S, D = x.shape[1], x.shape[2]
    NC = ctx.shape[1]
    T = NC + S
    assert NC == ROW_TILE and S % GRID_W == 0
    ada_cols = ada_w.shape[2]
    nw_cols = norm_w.shape[2]
    me = 4 * lax.axis_index("x") + 2 * lax.axis_index("y") + lax.axis_index("c")

    pack1 = jnp.concatenate([_pad_rows(c, _rows8(D)), _pad_rows(norm_w, _rows8(4 * nw_cols))], axis=0)
    g1 = ag_small("ag_c_normw", pack1)
    c_all = g1[:, :D // 128].reshape(NDEV, D)
    nw_rows = _rows8(D)
    nw = g1[:, nw_rows:nw_rows + 4 * nw_cols // 128].reshape(NDEV, 2, 2, nw_cols)
    nw = nw.transpose(1, 2, 0, 3).reshape(2, 2, D)
    cin = jnp.concatenate([c_all, jnp.broadcast_to(c_ctx[None], (NDEV, D))], axis=0)
    act = _vmem_call("silu_c", lambda v: _silu(v).astype(BF16), jax.ShapeDtypeStruct((2 * NDEV, D), BF16), cin)
    ada_b_loc = lax.dynamic_slice_in_dim(ada_b, me * ada_cols, ada_cols, axis=1)
    mods = [mm_nn(f"mod{i}", act, ada_w[i], _epi_bias, [F32], extras=(ada_b_loc[i:i + 1],), extra_kinds=('n',))[0]
            for i in range(2)]
    gm = ag_small("ag_mod", jnp.concatenate(mods, axis=1))
    gm = gm.reshape(NDEV, 2 * NDEV, 2, ada_cols).transpose(2, 1, 0, 3).reshape(2, 2 * NDEV, NDEV * ada_cols)
    mod_lat = lax.dynamic_index_in_dim(gm, me, axis=1, keepdims=False)
    mod_ctx = gm[:, NDEV]
    mod2 = jnp.stack([mod_ctx, mod_lat], axis=1).reshape(2, 2, 6, D)

    def chunk(i, j):
        return mod2[i, :, j, :]

    def b16(w):
        return w.astype(BF16)

    (w_in_e,) = ag_big("ag_weights_l0_qkv", [b16(ev_w_in[0])])

    cos, sa, sb = _rope_tables(S, NC)
    bias8 = na_span_bias(na_bias_table(od_rpb[0]))
    sink = ev_sink[0]
    TQ_F, TQ_B = 256, 256

    X0 = jnp.concatenate([ctx[0], x[0]], axis=0)
    h_a = norm_mod("l0_norm1", X0, nw[0, 0][None], chunk(0, 0), chunk(0, 1), NC)
    (qkv0,), (w_out_e_half,) = mm_nn("l0_qkv", h_a, w_in_e, _epi_store(F32), [F32],
                                     carry=[ex_ag_chips([b16(ev_w_out[0])])])
    qkvh0 = prep_even("l0_prep", qkv0, ev_q_norm, ev_k_norm, cos, sa, sb)
    o0, (w1_0_half, w2_0_half, w_out_o_half, w_out_e) = attn_even_fwd(
        "l0_attn", qkvh0, sink, NC, TQ_F,
        carry=[ex_ag_chips([b16(mlp_w1[0]), b16(mlp_w2[0]), b16(od_w_out[0])]), ex_ag_sibling([w_out_e_half])])
    w_out_e = w_out_e.reshape(-1, D)
    tm0 = _tile(T, 1100)
    (X1, y0), (w1_0, w2_0) = mm_nn("l0_out", o0, w_out_e, _epi_resid_gate(NC, tm0), [F32, F32],
                                   extras=(X0, chunk(0, 2)), extra_kinds=('mn', 'n'),
                                   carry=[ex_ag_sibling([w1_0_half, w2_0_half])])
    h_b = norm_mod("l0_norm2", X1, nw[0, 1][None], chunk(0, 3), chunk(0, 4), NC)
    (a0, r0), (w_in_o_half, w_out_o) = mm_nn(
        "l0_up", h_b, w1_0, _epi_relu2, [BF16, BF16], tn_cap=1024,
        carry=[ex_ag_chips([b16(od_w_in[0])]), ex_ag_sibling([w_out_o_half])])
    (X2, z0), (w1_1_half, w_in_o) = mm_nn(
        "l0_down", a0, w2_0.reshape(-1, D), _epi_resid_gate(NC, tm0), [F32, F32], extras=(X1, chunk(0, 5)),
        extra_kinds=('mn', 'n'), carry=[ex_ag_chips([b16(mlp_w1[1])]), ex_ag_sibling([w_in_o_half])])
    w_out_o = w_out_o.reshape(-1, D)

    h_c = norm_mod("l1_norm1", X2, nw[1, 0][None], chunk(1, 0), chunk(1, 1), NC)
    (qkv1,), (w1_1,) = mm_nn("l1_qkv", h_c, w_in_o, _epi_store(BF16), [BF16], tn_cap=768,
                             carry=[ex_ag_sibling([w1_1_half])])
    o1, (w2_1_half,) = attn_odd_fwd("l1_attn", qkv1, bias8, NC, carry=[ex_ag_chips([b16(mlp_w2[1])])])
    X2l = X2[NC:]
    tm1 = _tile(S, 1100)
    (X3, y1), (w2_1,) = mm_nn("l1_out", o1, w_out_o, _epi_resid_gate(0, tm1), [F32, F32],
                              extras=(X2l, chunk(1, 2)), extra_kinds=('mn', 'n'),
                              carry=[ex_ag_sibling([w2_1_half])])
    h_d = norm_mod("l1_norm2", X3, nw[1, 1][None], chunk(1, 3), chunk(1, 4), 0)
    a1, r1 = mm_nn("l1_up", h_d, w1_1, _epi_relu2, [BF16, BF16], tn_cap=1024)
    X4, z1 = mm_nn("l1_down", a1, w2_1.reshape(-1, D), _epi_resid_gate(0, tm1), [F32, F32], extras=(X3, chunk(1, 5)),
                   extra_kinds=('mn', 'n'))
    dX4, loss_p, dfw_p = final_loss("final_loss", X4, final_norm_w[None], loss_target[0])
    w_in = [w_in_e, w_in_o]
    w_out = [w_out_e, w_out_o]
    w1 = [w1_0, w1_1]
    w2 = [w2_0.reshape(-1, D), w2_1.reshape(-1, D)]

    mc4 = (lax.axis_index("c") * 4).astype(jnp.int32)
    my_chip = (2 * lax.axis_index("x") + lax.axis_index("y")).astype(jnp.int32)

    def chip_sum(tag, g8, sib4):
        return rs_chip_sum(f"rs_chip_sum_{tag}", g8, sib4, mc4[None])

    dz1, pg2_1 = gate_bwd("l1_gate2_bwd", dX4, z1, chunk(1, 5), 0)
    du1 = mm_nt("l1_down_dx", dz1, w2[1], _epi_mul2r, BF16, extras=(r1,))
    g_w1_1 = mm_tn("l1_up_dw", h_d, du1, 1)
    g_w2_1, (sib_w1_1,) = mm_tn("l1_down_dw", a1, dz1, 0, carry=[ex_rs_sibling([g_w1_1])])
    dh_d, (rem_w1_1, sib_w2_1) = mm_nt(
        "l1_up_dx", du1, w1[1], _epi_store(F32), F32,
        carry=[ex_rs_chips([chip_sum("w1_1", g_w1_1, sib_w1_1)]), ex_rs_sibling([g_w2_1])])
    dX3, pn2_1 = norm_bwd("l1_norm2_bwd", X3, dh_d, dX4, nw[1, 1][None], chunk(1, 4), 0)
    dy1, pg1_1 = gate_bwd("l1_gate1_bwd", dX3, y1, chunk(1, 2), 0)
    do1 = mm_nt("l1_out_dx", dy1, w_out[1], _epi_store(BF16), BF16)
    g_wout_1 = mm_tn("l1_out_dw", o1, dy1, 0)
    (dq1, dk1, dv1, dbias8), (rem_w2_1, sib_wout_1) = attn_odd_bwd(
        "l1_attn_bwd", qkv1, bias8, do1, NC,
        carry=[ex_rs_chips([chip_sum("w2_1", g_w2_1, sib_w2_1)]), ex_rs_sibling([g_wout_1])])
    dqkv1 = jnp.concatenate([jnp.pad(dq1, ((NC, 0), (0, 0))), dk1, dv1], axis=1).astype(BF16)
    dh_c, (rem_wout_1,) = mm_nt("l1_qkv_dx", dqkv1, w_in[1], _epi_store(F32), F32,
                                carry=[ex_rs_chips([chip_sum("wout_1", g_wout_1, sib_wout_1)])])
    g_win_1 = mm_tn("l1_qkv_dw", h_c, dqkv1, 1)
    dX2, pn1_1 = norm_bwd("l1_norm1_bwd", X2, dh_c, dX3, nw[1, 0][None], chunk(1, 1), NC, dres_skip=NC)
    d_rpb = na_bias_grad("rpb_grad", na_span_bias_grad(dbias8))

    dz0, pg2_0 = gate_bwd("l0_gate2_bwd", dX2, z0, chunk(0, 5), NC)
    du0, (sib_win_1,) = mm_nt("l0_down_dx", dz0, w2[0], _epi_mul2r, BF16, extras=(r0,),
                              carry=[ex_rs_sibling([g_win_1])])
    g_w1_0, (rem_win_1,) = mm_tn("l0_up_dw", h_b, du0, 1,
                                 carry=[ex_rs_chips([chip_sum("win_1", g_win_1, sib_win_1)])])
    g_w2_0, (sib_w1_0,) = mm_tn("l0_down_dw", a0, dz0, 0, carry=[ex_rs_sibling([g_w1_0])])
    dh_b, (rem_w1_0, sib_w2_0) = mm_nt(
        "l0_up_dx", du0, w1[0], _epi_store(F32), F32,
        carry=[ex_rs_chips([chip_sum("w1_0", g_w1_0, sib_w1_0)]), ex_rs_sibling([g_w2_0])])
    dX1, pn2_0 = norm_bwd("l0_norm2_bwd", X1, dh_b, dX2, nw[0, 1][None], chunk(0, 4), NC)
    dy0, pg1_0 = gate_bwd("l0_gate1_bwd", dX1, y0, chunk(0, 2), NC)
    do0 = mm_nt("l0_out_dx", dy0, w_out[0], _epi_store(BF16), BF16)
    g_wout_0 = mm_tn("l0_out_dw", o0, dy0, 0)
    (dq0, dk0, dv0, dsink_p), (rem_w2_0, sib_wout_0) = attn_even_bwd(
        "l0_attn_bwd", qkvh0, sink, do0, NC, TQ_B,
        carry=[ex_rs_chips([chip_sum("w2_0", g_w2_0, sib_w2_0)]), ex_rs_sibling([g_wout_0])])
    dqkv0, pqk = prep_even_bwd("l0_prep_bwd", qkv0, dq0, dk0, dv0, ev_q_norm, ev_k_norm, cos, sa, sb)
    g_win_0, (rem_wout_0,) = mm_tn("l0_qkv_dw", h_a, dqkv0, 1,
                                   carry=[ex_rs_chips([chip_sum("wout_0", g_wout_0, sib_wout_0)])])
    dh_a, (sib_win_0,) = mm_nt("l0_qkv_dx", dqkv0, w_in[0], _epi_store(F32), F32, carry=[ex_rs_sibling([g_win_0])])
    (dx_lat, pn1_0), (rem_win_0,) = norm_bwd(
        "l0_norm1_bwd", X0, dh_a, dX1, nw[0, 0][None], chunk(0, 1), NC, out_skip=NC,
        carry=[ex_rs_chips([chip_sum("win_0", g_win_0, sib_win_0)])])
    grad_x = dx_lat[None]

    def dmod(grp, pn1, pg1, pn2, pg2):
        return jnp.concatenate([pn1[grp], pn1[2 + grp], pg1[grp], pn2[grp], pn2[2 + grp], pg2[grp]])

    dmod_lat = jnp.stack([dmod(1, pn1_0, pg1_0, pn2_0, pg2_0), dmod(1, pn1_1, pg1_1, pn2_1, pg2_1)])
    dmod_ctx = jnp.stack([dmod(0, pn1_0, pg1_0, pn2_0, pg2_0), dmod(0, pn1_1, pg1_1, pn2_1, pg2_1)])
    dnw_p = jnp.stack([pn1_0[4], pn2_0[4], pn1_1[4], pn2_1[4]])
    pieces = [dmod_lat, dmod_ctx, dnw_p, pqk[0], pqk[1], dsink_p[8:, 0, 0], d_rpb, dfw_p[0], loss_p[0, 0]]
    sizes = [int(np.prod(p.shape)) for p in pieces]
    rows = [_rows8(s) for s in sizes]
    pack2 = jnp.concatenate([_pad_rows(p, r) for p, r in zip(pieces, rows)], axis=0)
    g2, s2 = ag_small("ag_small_grads", pack2, with_sum=True)
    offs = np.concatenate([[0], np.cumsum(rows)])

    def piece(arr, i, shape):
        return arr[..., offs[i]:offs[i + 1], :].reshape(arr.shape[:-2] + (-1,))[..., :sizes[i]].reshape(
            arr.shape[:-2] + shape)

    dmod_all = piece(g2, 0, (2, 6 * D))
    dmodc_sum = piece(s2, 1, (2, 6 * D))
    dnw_sum = piece(s2, 2, (2, 2, D))
    g_qn = piece(s2, 3, ev_q_norm.shape)
    g_kn = piece(s2, 4, ev_k_norm.shape)
    g_sink = piece(s2, 5, ev_sink.shape)
    g_rpb = piece(s2, 6, od_rpb.shape)
    g_fw = piece(s2, 7, final_norm_w.shape)
    loss = piece(s2, 8, ())

    dm16 = jnp.concatenate([dmod_all.transpose(1, 0, 2), dmodc_sum[:, None, :],
                            jnp.zeros((2, NDEV - 1, 6 * D), F32)], axis=1)
    dm16_loc = lax.dynamic_slice_in_dim(dm16.reshape(2, 2 * NDEV, NDEV, ada_cols), me, 1, axis=2)[:, :, 0, :]
    g_ada_b = _vmem_call("ada_b_grad", lambda v: jnp.sum(v, axis=1),
                         jax.ShapeDtypeStruct((2, 6 * D), F32), dm16)
    g_ada_w = []
    dact_p = None
    for i in range(2):
        dmb = dm16_loc[i].astype(BF16)
        g_ada_w.append(mm_tn(f"ada_w_grad{i}", act, dmb, None))
        part = mm_nt(f"ada_dact{i}", dmb, ada_w[i], _epi_store(F32), F32)
        dact_p = part if dact_p is None else dact_p + part
    _, dact = ag_small("ag_cctx", dact_p, with_sum=True)

    def cctx_grad(da, cc):
        sg = 1.0 / (1.0 + jnp.exp(-cc))
        return da[NDEV:NDEV + 1] * (sg * (1.0 + cc * (1.0 - sg)))

    g_cctx = _vmem_call("cctx_grad", cctx_grad, jax.ShapeDtypeStruct((1, D), F32), dact, c_ctx[None])[0]

    grads = [g_win_0, g_wout_0, g_w1_0, g_w2_0, g_win_1, g_wout_1, g_w1_1, g_w2_1]
    sib = [sib_win_0, sib_wout_0, sib_w1_0, sib_w2_0, sib_win_1, sib_wout_1, sib_w1_1, sib_w2_1]
    rem = [rem_win_0, rem_wout_0, rem_w1_0, rem_w2_0, rem_win_1, rem_wout_1, rem_w1_1, rem_w2_1]
    own_idx = jnp.stack([mc4 + my_chip, my_chip])

    def big(tag, w, m, v, ts):
        res = None
        for l, t in enumerate(ts):
            res = adamw_rs(f"adamw_{tag}_{l}", w, grads[t], sib[t], rem[t], m, v, own_idx, l, res)
        return tuple(res)

    r_ev_w_in = big('ev_w_in', ev_w_in, m_ev_w_in, v_ev_w_in, [0])
    r_ev_w_out = big('ev_w_out', ev_w_out, m_ev_w_out, v_ev_w_out, [1])
    r_mlp_w1 = big('mlp_w1', mlp_w1, m_mlp_w1, v_mlp_w1, [2, 6])
    r_mlp_w2 = big('mlp_w2', mlp_w2, m_mlp_w2, v_mlp_w2, [3, 7])
    r_od_w_in = big('od_w_in', od_w_in, m_od_w_in, v_od_w_in, [4])
    r_od_w_out = big('od_w_out', od_w_out, m_od_w_out, v_od_w_out, [5])

    g_ada = jnp.stack(g_ada_w)
    r_ada_w = adamw_rows("adamw_ada_w", ada_w.reshape(2 * D, ada_cols), g_ada.reshape(2 * D, ada_cols),
                         m_ada_w.reshape(2 * D, ada_cols), v_ada_w.reshape(2 * D, ada_cols))
    r_ada_w = tuple(u.reshape(2, D, ada_cols) for u in r_ada_w)

    g_nw_loc = lax.dynamic_slice_in_dim(dnw_sum, me * nw_cols, nw_cols, axis=2)
    small = [(c_ctx, g_cctx, m_c_ctx, v_c_ctx), (ada_b, g_ada_b, m_ada_b, v_ada_b),
             (norm_w, g_nw_loc, m_norm_w, v_norm_w), (ev_q_norm, g_qn, m_ev_q_norm, v_ev_q_norm),
             (ev_k_norm, g_kn, m_ev_k_norm, v_ev_k_norm), (ev_sink, g_sink, m_ev_sink, v_ev_sink),
             (od_rpb, g_rpb, m_od_rpb, v_od_rpb), (final_norm_w, g_fw, m_final_norm_w, v_final_norm_w)]
    srows = [_rows8(int(np.prod(w.shape))) for w, _, _, _ in small]
    packs = [jnp.concatenate([_pad_rows(tup[k], r) for tup, r in zip(small, srows)], axis=0) for k in range(4)]
    sres = adamw_rows("adamw_small", *packs)
    soffs = np.concatenate([[0], np.cumsum(srows)])

    def unpack(arr, i):
        w = small[i][0]
        return arr[soffs[i]:soffs[i + 1]].reshape(-1)[:int(np.prod(w.shape))].reshape(w.shape)

    sm = [[unpack(sres[k], i) for i in range(len(small))] for k in range(4)]

    def outs(k):
        big_k = {'ada_w': r_ada_w[k], 'mlp_w1': r_mlp_w1[k], 'mlp_w2': r_mlp_w2[k], 'ev_w_in': r_ev_w_in[k],
                 'ev_w_out': r_ev_w_out[k], 'od_w_in': r_od_w_in[k], 'od_w_out': r_od_w_out[k]}
        return (sm[k][0], big_k['ada_w'], sm[k][1], sm[k][2], big_k['mlp_w1'], big_k['mlp_w2'], big_k['ev_w_in'],
                big_k['ev_w_out'], sm[k][3], sm[k][4], sm[k][5], big_k['od_w_in'], big_k['od_w_out'], sm[k][6],
                sm[k][7])

    return (loss, grad_x, *outs(0), *outs(1), *outs(2), *outs(3))
```

```python
import numpy as np
import jax
import jax.numpy as jnp
from jax import lax
from jax.experimental import pallas as pl
from jax.experimental.pallas import tpu as pltpu

F32 = jnp.float32
BF16 = jnp.bfloat16
MESH = pl.DeviceIdType.MESH

NDEV = 8
HEAD = 128
GRID_W = 64
NA_KH, NA_KW = 8, 16
WINDOW = 128
ROPE_THETA = 10000.0
EPS = 1e-6
NEG = -1e30
SCALE = HEAD ** -0.5
ROW_TILE = 256
VMEM_LIMIT = 56 * 1024 * 1024

ADAM_LR, ADAM_B1, ADAM_B2, ADAM_EPS, ADAM_WD, ADAM_STEP = 0.001, 0.9, 0.999, 1e-08, 0.01, 10

NT = (((1,), (1,)), ((), ()))
NN = (((1,), (0,)), ((), ()))
TN = (((0,), (0,)), ((), ()))


def _cparams(sem):
    return pltpu.CompilerParams(dimension_semantics=sem, vmem_limit_bytes=VMEM_LIMIT)


def _tile(n, cap):
    if n <= cap:
        return n
    t = cap - cap % 64
    while t >= 64:
        if n % t == 0:
            return t
        t -= 64
    raise ValueError((n, cap))


def _dot(a, b, dims):
    return lax.dot_general(a.astype(BF16), b.astype(BF16), dims, preferred_element_type=F32)


def _slot(d):
    return (d % 2) * 4 + d // 2


class Exchange:
    def __init__(self, ins, out_shapes, aliases, n_sems, start, finish):
        self.ins, self.out_shapes, self.aliases, self.n_sems = list(ins), list(out_shapes), dict(aliases), n_sems
        self.start, self.finish = start, finish


def merge_exchanges(xs):
    ins, outs, aliases, bases, n = [], [], {}, [], 0
    for x in xs:
        bases.append((len(ins), len(outs), n))
        aliases.update({len(ins) + i: len(outs) + o for i, o in x.aliases.items()})
        ins += x.ins
        outs += x.out_shapes
        n += x.n_sems

    def run(which):
        def f(ci, co, ss, rs, base):
            for x, (i0, o0, s0) in zip(xs, bases):
                getattr(x, which)(ci[i0:i0 + len(x.ins)], co[o0:o0 + len(x.out_shapes)], ss, rs, base + s0)
        return f

    return Exchange(ins, outs, aliases, n, run('start'), run('finish'))


def _call(name, body, grid, ins, in_specs, out_shape, out_specs, scratch, sems, carry=None):
    if not carry:
        return pl.pallas_call(body, grid=grid, in_specs=in_specs, out_specs=out_specs, out_shape=out_shape,
                              scratch_shapes=scratch, compiler_params=_cparams(sems), name=name)(*ins)
    x = merge_exchanges(carry)
    n_in, n_ci, n_out, n_co, n_sc = len(ins), len(x.ins), len(out_shape), len(x.out_shapes), len(scratch)

    def wrapped(*refs):
        p = [0]

        def take(k):
            p[0] += k
            return refs[p[0] - k:p[0]]

        a, ci, o, co, sc = take(n_in), take(n_ci), take(n_out), take(n_co), take(n_sc)
        ss, rs = take(2)
        first = pl.program_id(0) == 0
        last = pl.program_id(0) == grid[0] - 1
        for d in range(1, len(grid)):
            first = jnp.logical_and(first, pl.program_id(d) == 0)
            last = jnp.logical_and(last, pl.program_id(d) == grid[d] - 1)

        @pl.when(first)
        def _():
            x.start(ci, co, ss, rs, 0)

        body(*a, *o, *sc)

        @pl.when(last)
        def _():
            x.finish(ci, co, ss, rs, 0)

    hbm = pl.BlockSpec(memory_space=pl.ANY)
    res = pl.pallas_call(
        wrapped, grid=grid, in_specs=list(in_specs) + [hbm] * n_ci, out_specs=list(out_specs) + [hbm] * n_co,
        out_shape=list(out_shape) + x.out_shapes,
        input_output_aliases={n_in + i: n_out + o for i, o in x.aliases.items()},
        scratch_shapes=list(scratch) + [pltpu.SemaphoreType.DMA((x.n_sems,)), pltpu.SemaphoreType.DMA((x.n_sems,))],
        compiler_params=_cparams(("arbitrary",) * len(grid)), name=name)(*ins, *x.ins)
    return list(res[:n_out]) + [list(res[n_out:])]


def _mm_core(name, grid, ins, in_specs, out_shape, out_specs, dims, acc_shape, epi, carry=None):
    nk = grid[2]
    n_extra = len(ins) - 2

    def body_single(*refs):
        epi(_dot(refs[0][...], refs[1][...], dims), refs[2:2 + n_extra], refs[2 + n_extra:])

    def body(*refs):
        a_ref, b_ref = refs[0], refs[1]
        ex = refs[2:2 + n_extra]
        outs = refs[2 + n_extra:-1]
        acc = refs[-1]
        k = pl.program_id(2)

        @pl.when(k == 0)
        def _():
            acc[...] = _dot(a_ref[...], b_ref[...], dims)

        @pl.when(jnp.logical_and(k > 0, k < nk - 1))
        def _():
            acc[...] += _dot(a_ref[...], b_ref[...], dims)

        @pl.when(k == nk - 1)
        def _():
            epi(acc[...] + _dot(a_ref[...], b_ref[...], dims), ex, outs)

    if nk == 1:
        return _call(name, body_single, grid, ins, in_specs, out_shape, out_specs, [],
                     ("parallel", "parallel", "arbitrary"), carry)
    return _call(name, body, grid, ins, in_specs, out_shape, out_specs, [pltpu.VMEM(acc_shape, F32)],
                 ("parallel", "parallel", "arbitrary"), carry)


def _split(res, n, carry):
    own = res[0] if n == 1 else list(res[:n])
    return (own, res[n]) if carry else own


def _epi_store(dtype):
    def epi(acc, ex, outs):
        outs[0][...] = acc.astype(dtype)
    return epi


def _epi_bias(acc, ex, outs):
    outs[0][...] = acc + ex[0][...]


def _epi_relu2(acc, ex, outs):
    r = jnp.maximum(acc, 0.0)
    outs[0][...] = (r * r).astype(BF16)
    outs[1][...] = r.astype(BF16)


def _epi_mul2r(acc, ex, outs):
    outs[0][...] = (acc * (2.0 * ex[0][...].astype(F32))).astype(BF16)


def _epi_resid_gate(nctx, tm):
    def epi(acc, ex, outs):
        rows = pl.program_id(0) * tm + lax.broadcasted_iota(jnp.int32, (tm, 1), 0)
        g = jnp.where(rows < nctx, ex[1][0:1, :], ex[1][1:2, :])
        outs[0][...] = ex[0][...] + g * acc
        outs[1][...] = acc
    return epi


def mm_nn(name, a, w, epi, outs, extras=(), extra_kinds=(), tm_cap=1100, tn_cap=512, tk_cap=2048, carry=None):
    M, K = a.shape
    if w.ndim == 3:
        ns = w.shape[2]
        N = NDEV * ns
        tn = _tile(ns, tn_cap)
        nper = ns // tn
    else:
        N = w.shape[1]
        tn = _tile(N, tn_cap)
    tm = _tile(M, tm_cap)
    tk = _tile(K, tk_cap)
    grid = (M // tm, N // tn, K // tk)
    a_spec = pl.BlockSpec((tm, tk), lambda i, j, k: (i, k))
    if w.ndim == 3:
        b_spec = pl.BlockSpec((None, tk, tn), lambda i, j, k: (j // nper, k, j % nper))
    else:
        b_spec = pl.BlockSpec((tk, tn), lambda i, j, k: (k, j))
    ex_specs = []
    for e, kind in zip(extras, extra_kinds):
        if kind == 'mn':
            ex_specs.append(pl.BlockSpec((tm, tn), lambda i, j, k: (i, j)))
        else:
            ex_specs.append(pl.BlockSpec((e.shape[0], tn), lambda i, j, k: (0, j)))
    out_shape = [jax.ShapeDtypeStruct((M, N), dt) for dt in outs]
    out_specs = [pl.BlockSpec((tm, tn), lambda i, j, k: (i, j)) for _ in outs]
    res = _mm_core(name, grid, (a, w, *extras), [a_spec, b_spec, *ex_specs], out_shape, out_specs, NN, (tm, tn), epi,
                   carry)
    return (list(res[:len(outs)]), res[len(outs)]) if carry else res


def mm_nt(name, a, w, epi, out_dtype, extras=(), tm_cap=1100, to_cap=1024, tc_cap=2048, carry=None):
    M, N = a.shape
    tm = _tile(M, tm_cap)
    if w.ndim == 3:
        Kw, ns = w.shape[1], w.shape[2]
        tc = _tile(ns, tc_cap)
        cper = ns // tc
    else:
        Kw = w.shape[0]
        tc = _tile(N, tc_cap)
    to = _tile(Kw, to_cap)
    grid = (M // tm, Kw // to, N // tc)
    a_spec = pl.BlockSpec((tm, tc), lambda i, j, k: (i, k))
    if w.ndim == 3:
        b_spec = pl.BlockSpec((None, to, tc), lambda i, j, k: (k // cper, j, k % cper))
    else:
        b_spec = pl.BlockSpec((to, tc), lambda i, j, k: (j, k))
    ex_specs = [pl.BlockSpec((tm, to), lambda i, j, k: (i, j)) for _ in extras]
    out_shape = [jax.ShapeDtypeStruct((M, Kw), out_dtype)]
    out_specs = [pl.BlockSpec((tm, to), lambda i, j, k: (i, j))]
    return _split(_mm_core(name, grid, (a, w, *extras), [a_spec, b_spec, *ex_specs], out_shape, out_specs, NT, (tm, to),
                           epi, carry), 1, carry)


def mm_tn(name, a, b, shard_axis, to_cap=1024, tn_cap=1024, tc_cap=2200, carry=None):
    M, Ka = a.shape
    N = b.shape[1]
    tc = _tile(M, tc_cap)
    if shard_axis is None:
        to, tn = _tile(Ka, to_cap), _tile(N, tn_cap)
        shape = (Ka, N)
        oblk = (to, tn)
        omap = lambda i, j, k: (i, j)
    elif shard_axis == 1:
        ns = N // NDEV
        to, tn = _tile(Ka, to_cap), _tile(ns, tn_cap)
        per = ns // tn
        shape = (NDEV, Ka, ns)
        oblk = (None, to, tn)
        omap = lambda i, j, k: (_slot(j // per), i, j % per)
    else:
        rs = Ka // NDEV
        to, tn = _tile(rs, to_cap), _tile(N, tn_cap)
        per = rs // to
        shape = (NDEV, rs, N)
        oblk = (None, to, tn)
        omap = lambda i, j, k: (_slot(i // per), i % per, j)
    grid = (Ka // to, N // tn, M // tc)
    a_spec = pl.BlockSpec((tc, to), lambda i, j, k: (k, i))
    b_spec = pl.BlockSpec((tc, tn), lambda i, j, k: (k, j))
    out_shape = [jax.ShapeDtypeStruct(shape, F32)]
    out_specs = [pl.BlockSpec(oblk, omap)]
    return _split(_mm_core(name, grid, (a, b), [a_spec, b_spec], out_shape, out_specs, TN, (to, tn), _epi_store(F32),
                           carry), 1, carry)


def _row_spec(D):
    return pl.BlockSpec((ROW_TILE, D), lambda i: (i, 0))


def _const_spec(r, D):
    return pl.BlockSpec((r, D), lambda i: (0, 0))


def _grp(ref, is_ctx):
    return jnp.where(is_ctx, ref[0:1, :], ref[1:2, :])


def norm_mod(name, x, nw, sh, sc, nctx):
    R, D = x.shape
    assert R % ROW_TILE == 0 and nctx % ROW_TILE == 0

    def body(x_ref, nw_ref, sh_ref, sc_ref, o_ref):
        is_ctx = pl.program_id(0) * ROW_TILE < nctx
        xv = x_ref[...]
        rstd = lax.rsqrt(jnp.mean(xv * xv, axis=-1, keepdims=True) + EPS)
        n = xv * rstd * nw_ref[...]
        o_ref[...] = (n * (1.0 + _grp(sc_ref, is_ctx)) + _grp(sh_ref, is_ctx)).astype(BF16)

    return pl.pallas_call(
        body, grid=(R // ROW_TILE,),
        in_specs=[_row_spec(D), _const_spec(1, D), _const_spec(2, D), _const_spec(2, D)],
        out_specs=_row_spec(D), out_shape=jax.ShapeDtypeStruct((R, D), BF16),
        compiler_params=_cparams(("parallel",)), name=name)(x, nw, sh, sc)


def norm_bwd(name, x, dh, dres, nw, sc, nctx, dres_skip=0, out_skip=0, carry=None):
    R, D = x.shape
    assert R % ROW_TILE == 0 and nctx % ROW_TILE == 0 and dres_skip % ROW_TILE == 0 and out_skip % ROW_TILE == 0
    res_tiles, out_tiles = dres_skip // ROW_TILE, out_skip // ROW_TILE

    def body(x_ref, dh_ref, dres_ref, nw_ref, sc_ref, dx_ref, part_ref):
        i = pl.program_id(0)
        is_ctx = i * ROW_TILE < nctx

        @pl.when(i == 0)
        def _():
            part_ref[...] = jnp.zeros_like(part_ref)

        xv = x_ref[...]
        dhv = dh_ref[...]
        w = nw_ref[...]
        rstd = lax.rsqrt(jnp.mean(xv * xv, axis=-1, keepdims=True) + EPS)
        xhat = xv * rstd
        n = xhat * w
        dn = dhv * (1.0 + _grp(sc_ref, is_ctx))
        dxhat = dn * w
        dres = dres_ref[...]
        if res_tiles:
            dres = jnp.where(i < res_tiles, 0.0, dres)
        dx_ref[...] = dres + rstd * (dxhat - xhat * jnp.mean(dxhat * xhat, axis=-1, keepdims=True))
        s_sh = jnp.sum(dhv, axis=0, keepdims=True)
        s_sc = jnp.sum(dhv * n, axis=0, keepdims=True)
        s_nw = jnp.sum(dn * xhat, axis=0, keepdims=True)
        zero = jnp.zeros_like(s_sh)
        part_ref[0:1, :] += jnp.where(is_ctx, s_sh, zero)
        part_ref[1:2, :] += jnp.where(is_ctx, zero, s_sh)
        part_ref[2:3, :] += jnp.where(is_ctx, s_sc, zero)
        part_ref[3:4, :] += jnp.where(is_ctx, zero, s_sc)
        part_ref[4:5, :] += s_nw

    res = _call(name, body, (R // ROW_TILE,), (x, dh, dres, nw, sc),
                [_row_spec(D), _row_spec(D),
                 pl.BlockSpec((ROW_TILE, D), lambda i: (jnp.maximum(i - res_tiles, 0), 0)),
                 _const_spec(1, D), _const_spec(2, D)],
                [jax.ShapeDtypeStruct((R - out_skip, D), F32), jax.ShapeDtypeStruct((8, D), F32)],
                [pl.BlockSpec((ROW_TILE, D), lambda i: (jnp.maximum(i - out_tiles, 0), 0)), _const_spec(8, D)],
                [], ("arbitrary",), carry)
    return _split(res, 2, carry)


def gate_bwd(name, dx, y, g, nctx):
    R, D = dx.shape
    assert R % ROW_TILE == 0 and nctx % ROW_TILE == 0

    def body(dx_ref, y_ref, g_ref, dy_ref, part_ref):
        i = pl.program_id(0)
        is_ctx = i * ROW_TILE < nctx

        @pl.when(i == 0)
        def _():
            part_ref[...] = jnp.zeros_like(part_ref)

        dxv = dx_ref[...]
        dy_ref[...] = (dxv * _grp(g_ref, is_ctx)).astype(BF16)
        s = jnp.sum(dxv * y_ref[...], axis=0, keepdims=True)
        zero = jnp.zeros_like(s)
        part_ref[0:1, :] += jnp.where(is_ctx, s, zero)
        part_ref[1:2, :] += jnp.where(is_ctx, zero, s)

    return pl.pallas_call(
        body, grid=(R // ROW_TILE,),
        in_specs=[_row_spec(D), _row_spec(D), _const_spec(2, D)],
        out_specs=[_row_spec(D), _const_spec(8, D)],
        out_shape=[jax.ShapeDtypeStruct((R, D), BF16), jax.ShapeDtypeStruct((8, D), F32)],
        compiler_params=_cparams(("arbitrary",)), name=name)(dx, y, g)


def final_loss(name, x, fw, tgt):
    S, D = x.shape

    def body(x_ref, fw_ref, t_ref, dx_ref, loss_ref, dfw_ref):
        i = pl.program_id(0)

        @pl.when(i == 0)
        def _():
            loss_ref[...] = jnp.zeros_like(loss_ref)
            dfw_ref[...] = jnp.zeros_like(dfw_ref)

        xv = x_ref[...]
        w = fw_ref[...]
        rstd = lax.rsqrt(jnp.mean(xv * xv, axis=-1, keepdims=True) + EPS)
        xhat = xv * rstd
        e = xhat * w - t_ref[...]
        loss_ref[...] += 0.5 * jnp.sum(jnp.mean(e * e, axis=-1, keepdims=True))
        dout = e * (1.0 / D)
        dfw_ref[0:1, :] += jnp.sum(dout * xhat, axis=0, keepdims=True)
        dxhat = dout * w
        dx_ref[...] = rstd * (dxhat - xhat * jnp.mean(dxhat * xhat, axis=-1, keepdims=True))

    return pl.pallas_call(
        body, grid=(S // ROW_TILE,),
        in_specs=[_row_spec(D), _const_spec(1, D), _row_spec(D)],
        out_specs=[_row_spec(D), pl.BlockSpec((8, 128), lambda i: (0, 0)), _const_spec(8, D)],
        out_shape=[jax.ShapeDtypeStruct((S, D), F32), jax.ShapeDtypeStruct((8, 128), F32),
                   jax.ShapeDtypeStruct((8, D), F32)],
        compiler_params=_cparams(("arbitrary",)), name=name)(x, fw, tgt)


def _rope(x, cos, sa, sb):
    return x * cos + pltpu.roll(x, 96, 1) * sa + pltpu.roll(x, 32, 1) * sb


def _rope_t(dy, cos, sa, sb):
    return dy * cos + pltpu.roll(dy * sa, 32, 1) + pltpu.roll(dy * sb, 96, 1)


_EVEN_KINDS = ['qa'] * 8 + ['ka'] * 2 + ['v'] * 2 + ['qb'] * 8 + ['kb'] * 2 + ['v'] * 2
_EVEN_DSRC = ([('q', j) for j in range(8)] + [('k', 0), ('k', 1), ('v', 0), ('v', 1)]
              + [('q', 8 + j) for j in range(8)] + [('k', 2), ('k', 3), ('v', 2), ('v', 3)])


def _cols(j):
    return slice(j * HEAD, (j + 1) * HEAD)


def prep_even(name, qkv, qn, kn, cos, sa, sb):
    T, W = qkv.shape

    def body(x_ref, qn_ref, kn_ref, cos_ref, sa_ref, sb_ref, o_ref):
        cos_, sa_, sb_ = cos_ref[...], sa_ref[...], sb_ref[...]
        for j, kind in enumerate(_EVEN_KINDS):
            x = x_ref[:, _cols(j)]
            if kind in ('qa', 'ka'):
                rstd = lax.rsqrt(jnp.mean(x * x, axis=-1, keepdims=True) + EPS)
                x = x * rstd * (qn_ref[...] if kind == 'qa' else kn_ref[...])
            if kind != 'v':
                x = _rope(x, cos_, sa_, sb_)
            o_ref[:, _cols(j)] = x.astype(BF16)

    blk = pl.BlockSpec((ROW_TILE, W), lambda i: (i, 0))
    tab = pl.BlockSpec((ROW_TILE, HEAD), lambda i: (i, 0))
    one = pl.BlockSpec((1, HEAD), lambda i: (0, 0))
    return pl.pallas_call(
        body, grid=(T // ROW_TILE,), in_specs=[blk, one, one, tab, tab, tab], out_specs=blk,
        out_shape=jax.ShapeDtypeStruct(qkv.shape, BF16),
        compiler_params=_cparams(("parallel",)), name=name)(qkv, qn, kn, cos, sa, sb)


def prep_even_bwd(name, qkv, dq, dk, dv, qn, kn, cos, sa, sb):
    T, W = qkv.shape

    def body(x_ref, dq_ref, dk_ref, dv_ref, qn_ref, kn_ref, cos_ref, sa_ref, sb_ref, o_ref, part_ref):
        @pl.when(pl.program_id(0) == 0)
        def _():
            part_ref[...] = jnp.zeros_like(part_ref)

        cos_, sa_, sb_ = cos_ref[...], sa_ref[...], sb_ref[...]
        src = {'q': dq_ref, 'k': dk_ref, 'v': dv_ref}
        sums = {'qa': None, 'ka': None}
        for j, kind in enumerate(_EVEN_KINDS):
            which, blk_j = _EVEN_DSRC[j]
            d = src[which][:, _cols(blk_j)]
            if kind != 'v':
                d = _rope_t(d, cos_, sa_, sb_)
            if kind in ('qa', 'ka'):
                x = x_ref[:, _cols(j)]
                rstd = lax.rsqrt(jnp.mean(x * x, axis=-1, keepdims=True) + EPS)
                xhat = x * rstd
                s = jnp.sum(d * xhat, axis=0, keepdims=True)
                sums[kind] = s if sums[kind] is None else sums[kind] + s
                dxhat = d * (qn_ref[...] if kind == 'qa' else kn_ref[...])
                d = rstd * (dxhat - xhat * jnp.mean(dxhat * xhat, axis=-1, keepdims=True))
            o_ref[:, _cols(j)] = d.astype(BF16)
        part_ref[0:1, :] += sums['qa']
        part_ref[1:2, :] += sums['ka']

    def rows(w):
        return pl.BlockSpec((ROW_TILE, w), lambda i: (i, 0))

    one = pl.BlockSpec((1, HEAD), lambda i: (0, 0))
    return pl.pallas_call(
        body, grid=(T // ROW_TILE,),
        in_specs=[rows(W), rows(dq.shape[1]), rows(dk.shape[1]), rows(dv.shape[1]), one, one,
                  rows(HEAD), rows(HEAD), rows(HEAD)],
        out_specs=[rows(W), pl.BlockSpec((8, HEAD), lambda i: (0, 0))],
        out_shape=[jax.ShapeDtypeStruct(qkv.shape, BF16), jax.ShapeDtypeStruct((8, HEAD), F32)],
        compiler_params=_cparams(("arbitrary",)), name=name)(qkv, dq, dk, dv, qn, kn, cos, sa, sb)


def _even_maps():
    qmap = lambda h, qb: (qb, jnp.where(h < 8, h, h + 4))
    kmap = lambda h, qb: (0, jnp.where(h < 8, 8 + h // 4, 18 + h // 4))
    vmap = lambda h, qb: (0, jnp.where(h < 8, 10 + h // 4, 20 + h // 4))
    return qmap, kmap, vmap


def _softmax_parts(parts, extra=None):
    m = parts[0].max(axis=-1, keepdims=True)
    for p in parts[1:]:
        m = jnp.maximum(m, p.max(axis=-1, keepdims=True))
    if extra is not None:
        m = jnp.maximum(m, extra)
    es = [jnp.exp(p - m) for p in parts]
    l = es[0].sum(axis=-1, keepdims=True)
    for e in es[1:]:
        l = l + e.sum(axis=-1, keepdims=True)
    ex = None
    if extra is not None:
        ex = jnp.exp(extra - m)
        l = l + ex
    inv = 1.0 / l
    return [e * inv for e in es], (None if ex is None else ex * inv)


def _win_scores(q, k_ref, qb, tq, nctx, S):
    L = tq + 2 * WINDOW
    nqc = nctx // tq
    qlat = (qb - nqc) * tq
    start = pl.multiple_of(jnp.clip(qlat - WINDOW, 0, S - L), 128)
    kc = k_ref[0:nctx, :]
    kw = k_ref[pl.ds(nctx + start, L), :]
    s_c = _dot(q, kc, NT) * SCALE
    s_w = _dot(q, kw, NT) * SCALE
    qpos = qlat + lax.broadcasted_iota(jnp.int32, (tq, 1), 0)
    kpos = start + lax.broadcasted_iota(jnp.int32, (1, L), 1)
    valid = jnp.logical_and(jnp.abs(kpos - qpos) <= WINDOW, qb >= nqc)
    return s_c, jnp.where(valid, s_w, NEG), start, L


def _softmax_raw(raw):
    m = raw.max(axis=-1, keepdims=True)
    e = jnp.exp2((raw - m) * (SCALE * np.log2(np.e)))
    return e * (1.0 / e.sum(axis=-1, keepdims=True))


def _softmax_raw_t(raw):
    m = raw.max(axis=0, keepdims=True)
    e = jnp.exp2((raw - m) * (SCALE * np.log2(np.e)))
    return e * (1.0 / e.sum(axis=0, keepdims=True))


def _glob_keys(qb, tq, nctx, T):
    is_ctx = qb < nctx // tq
    return [(is_ctx, slice(0, nctx)), (jnp.logical_not(is_ctx), slice(0, T))]


def attn_even_fwd(name, qkvh, sink, nctx, tq, carry=None):
    T = qkvh.shape[0]
    S = T - nctx
    qmap, kmap, vmap = _even_maps()

    def body(sink_ref, q_ref, k_ref, v_ref, o_ref):
        h, qb = pl.program_id(0), pl.program_id(1)
        q = q_ref[...]

        for pred, keys in _glob_keys(qb, tq, nctx, T):
            @pl.when(jnp.logical_and(h < 8, pred))
            def _():
                p = _softmax_raw(_dot(q, k_ref[keys, :], NT))
                o_ref[...] = _dot(p, v_ref[keys, :], NN).astype(BF16)

        @pl.when(h >= 8)
        def _():
            s_c, s_w, start, L = _win_scores(q, k_ref, qb, tq, nctx, S)
            sk = jnp.full((tq, 1), sink_ref[jnp.maximum(h - 8, 0)], F32)
            (p_c, p_w), _ = _softmax_parts([s_c, s_w], sk)
            o = _dot(p_c, v_ref[0:nctx, :], NN) + _dot(p_w, v_ref[pl.ds(nctx + start, L), :], NN)
            o_ref[...] = o.astype(BF16)

    res = _call(name, body, (16, T // tq), (sink, qkvh, qkvh, qkvh),
                [pl.BlockSpec(memory_space=pltpu.SMEM), pl.BlockSpec((tq, HEAD), qmap),
                 pl.BlockSpec((T, HEAD), kmap), pl.BlockSpec((T, HEAD), vmap)],
                [jax.ShapeDtypeStruct((T, 16 * HEAD), BF16)], [pl.BlockSpec((tq, HEAD), lambda h, qb: (qb, h))],
                [], ("parallel", "arbitrary"), carry)
    return _split(res, 1, carry)


def attn_even_bwd(name, qkvh, sink, do, nctx, tq, carry=None):
    T = qkvh.shape[0]
    S = T - nctx
    qmap, kmap, vmap = _even_maps()

    def body(sink_ref, q_ref, k_ref, v_ref, do_ref, dq_ref, dk_ref, dv_ref, ds_ref):
        h, qb = pl.program_id(0), pl.program_id(1)
        q = q_ref[...]
        dov = do_ref[...]

        @pl.when(jnp.logical_and(h % 4 == 0, qb == 0))
        def _():
            dk_ref[...] = jnp.zeros_like(dk_ref)
            dv_ref[...] = jnp.zeros_like(dv_ref)

        @pl.when(qb == 0)
        def _():
            ds_ref[...] = jnp.zeros_like(ds_ref)

        for pred, keys in _glob_keys(qb, tq, nctx, T):
            @pl.when(jnp.logical_and(h < 8, pred))
            def _():
                p = _softmax_raw_t(_dot(k_ref[keys, :], q, NT))
                dp = _dot(v_ref[keys, :], dov, NT)
                row = jnp.sum(p * dp, axis=0, keepdims=True)
                dsb = (p * (dp - row) * SCALE).astype(BF16)
                dq_ref[...] = _dot(dsb, k_ref[keys, :], TN)
                dk_ref[keys, :] += _dot(dsb, q, NN)
                dv_ref[keys, :] += _dot(p, dov, NN)

        @pl.when(h >= 8)
        def _():
            s_c, s_w, start, L = _win_scores(q, k_ref, qb, tq, nctx, S)
            sk = jnp.full((tq, 1), sink_ref[jnp.maximum(h - 8, 0)], F32)
            (p_c, p_w), p_s = _softmax_parts([s_c, s_w], sk)
            win = pl.ds(nctx + start, L)
            dp_c = _dot(dov, v_ref[0:nctx, :], NT)
            dp_w = _dot(dov, v_ref[win, :], NT)
            row = jnp.sum(p_c * dp_c, axis=-1, keepdims=True) + jnp.sum(p_w * dp_w, axis=-1, keepdims=True)
            ds_c = (p_c * (dp_c - row) * SCALE).astype(BF16)
            ds_w = (p_w * (dp_w - row) * SCALE).astype(BF16)
            dq_ref[...] = _dot(ds_c, k_ref[0:nctx, :], NN) + _dot(ds_w, k_ref[win, :], NN)
            dk_ref[0:nctx, :] += _dot(ds_c, q, TN)
            dk_ref[win, :] += _dot(ds_w, q, TN)
            dv_ref[0:nctx, :] += _dot(p_c, dov, TN)
            dv_ref[win, :] += _dot(p_w, dov, TN)
            ds_ref[...] += jnp.sum(-(p_s * row))

    kv_out = pl.BlockSpec((T, HEAD), lambda h, qb: (0, h // 4))
    res = _call(name, body, (16, T // tq), (sink, qkvh, qkvh, qkvh, do),
                [pl.BlockSpec(memory_space=pltpu.SMEM), pl.BlockSpec((tq, HEAD), qmap),
                 pl.BlockSpec((T, HEAD), kmap), pl.BlockSpec((T, HEAD), vmap),
                 pl.BlockSpec((tq, HEAD), lambda h, qb: (qb, h))],
                [jax.ShapeDtypeStruct((T, 16 * HEAD), F32), jax.ShapeDtypeStruct((T, 4 * HEAD), F32),
                 jax.ShapeDtypeStruct((T, 4 * HEAD), F32), jax.ShapeDtypeStruct((16, 8, 128), F32)],
                [pl.BlockSpec((tq, HEAD), lambda h, qb: (qb, h)), kv_out, kv_out,
                 pl.BlockSpec((None, 8, 128), lambda h, qb: (h, 0, 0))],
                [], ("arbitrary", "arbitrary"), carry)
    return _split(res, 4, carry)


NA_GROUP = 4
NA_SPAN = NA_KH + NA_GROUP - 1
_NA_PLAN = [[(j, 0) for j in range(NA_GROUP)],
            [(NA_KH // 2, j) for j in range(NA_GROUP)],
            [(NA_KH // 2 + j, NA_GROUP - 1) for j in range(NA_GROUP)]]


def _na_group(g, n_groups, rows):
    last = g == n_groups - 1
    kind = jnp.where(g == 0, 0, jnp.where(last, 2, 1))
    first_row = jnp.where(g == 0, 0, jnp.where(last, rows - NA_SPAN, NA_GROUP * g - NA_KH // 2))
    return kind, first_row


def na_span_bias(bias8):
    LW, LS = NA_KH * GRID_W, NA_SPAN * GRID_W
    kinds = []
    for plan in _NA_PLAN:
        strips = [jnp.pad(bias8[:, off], ((0, 0), (0, 0), (s * GRID_W, LS - LW - s * GRID_W)), constant_values=NEG)
                  for off, s in plan]
        kinds.append(jnp.concatenate(strips, axis=1))
    return jnp.stack(kinds, axis=1)


def na_span_bias_grad(db):
    LW = NA_KH * GRID_W
    out = [None] * NA_KH
    for kind, plan in enumerate(_NA_PLAN):
        for j, (off, s) in enumerate(plan):
            piece = db[:, kind, j * GRID_W:(j + 1) * GRID_W, s * GRID_W:s * GRID_W + LW]
            out[off] = piece if out[off] is None else out[off] + piece
    return jnp.stack(out, axis=1)


def _na_specs(T, nctx, n_groups, rows):
    LS = NA_SPAN * GRID_W
    tq = NA_GROUP * GRID_W
    assert nctx % tq == 0 and n_groups >= 3
    q_spec = pl.BlockSpec((tq, HEAD), lambda h, g: (g + nctx // tq, h))
    k_spec = pl.BlockSpec((T, HEAD), lambda h, g: (0, 16 + h))
    v_spec = pl.BlockSpec((T, HEAD), lambda h, g: (0, 32 + h))
    b_spec = pl.BlockSpec((None, None, tq, LS), lambda h, g: (h, _na_group(g, n_groups, rows)[0], 0, 0))
    row_spec = pl.BlockSpec((tq, HEAD), lambda h, g: (g, h))
    return q_spec, k_spec, v_spec, b_spec, row_spec


def _na_scores(q, k_ref, b_ref, g, n_groups, rows, nctx):
    first_row = _na_group(g, n_groups, rows)[1]
    win = pl.ds(pl.multiple_of(nctx + first_row * GRID_W, GRID_W), NA_SPAN * GRID_W)
    s_c = _dot(q, k_ref[0:nctx, :], NT) * SCALE
    s_w = _dot(q, k_ref[win, :], NT) * SCALE + b_ref[...]
    return s_c, s_w, win


def attn_odd_fwd(name, qkv, bias_s, nctx, carry=None):
    T = qkv.shape[0]
    S = T - nctx
    rows = S // GRID_W
    n_groups = rows // NA_GROUP
    q_spec, k_spec, v_spec, b_spec, row_spec = _na_specs(T, nctx, n_groups, rows)

    def body(q_ref, k_ref, v_ref, b_ref, o_ref):
        s_c, s_w, win = _na_scores(q_ref[...], k_ref, b_ref, pl.program_id(1), n_groups, rows, nctx)
        (p_c, p_w), _ = _softmax_parts([s_c, s_w])
        o_ref[...] = (_dot(p_c, v_ref[0:nctx, :], NN) + _dot(p_w, v_ref[win, :], NN)).astype(BF16)

    res = _call(name, body, (16, n_groups), (qkv, qkv, qkv, bias_s), [q_spec, k_spec, v_spec, b_spec],
                [jax.ShapeDtypeStruct((S, 16 * HEAD), BF16)], [row_spec], [], ("parallel", "arbitrary"), carry)
    return _split(res, 1, carry)


def attn_odd_bwd(name, qkv, bias_s, do, nctx, carry=None):
    T = qkv.shape[0]
    S = T - nctx
    rows = S // GRID_W
    n_groups = rows // NA_GROUP
    q_spec, k_spec, v_spec, b_spec, row_spec = _na_specs(T, nctx, n_groups, rows)

    def body(q_ref, k_ref, v_ref, b_ref, do_ref, dq_ref, dk_ref, dv_ref, db_ref):
        g = pl.program_id(1)
        q = q_ref[...]
        dov = do_ref[...]

        @pl.when(g == 0)
        def _():
            dk_ref[...] = jnp.zeros_like(dk_ref)
            dv_ref[...] = jnp.zeros_like(dv_ref)

        s_c, s_w, win = _na_scores(q, k_ref, b_ref, g, n_groups, rows, nctx)
        (p_c, p_w), _ = _softmax_parts([s_c, s_w])
        dp_c = _dot(dov, v_ref[0:nctx, :], NT)
        dp_w = _dot(dov, v_ref[win, :], NT)
        row = jnp.sum(p_c * dp_c, axis=-1, keepdims=True) + jnp.sum(p_w * dp_w, axis=-1, keepdims=True)
        dsw = p_w * (dp_w - row)
        first_visit = jnp.logical_or(g <= 1, g == n_groups - 1)

        @pl.when(first_visit)
        def _():
            db_ref[...] = dsw

        @pl.when(jnp.logical_not(first_visit))
        def _():
            db_ref[...] += dsw

        ds_c = (p_c * (dp_c - row) * SCALE).astype(BF16)
        ds_w = (dsw * SCALE).astype(BF16)
        dq_ref[...] = _dot(ds_c, k_ref[0:nctx, :], NN) + _dot(ds_w, k_ref[win, :], NN)
        dk_ref[0:nctx, :] += _dot(ds_c, q, TN)
        dk_ref[win, :] += _dot(ds_w, q, TN)
        dv_ref[0:nctx, :] += _dot(p_c, dov, TN)
        dv_ref[win, :] += _dot(p_w, dov, TN)

    kv_out = pl.BlockSpec((T, HEAD), lambda h, g: (0, h))
    res = _call(name, body, (16, n_groups), (qkv, qkv, qkv, bias_s, do), [q_spec, k_spec, v_spec, b_spec, row_spec],
                [jax.ShapeDtypeStruct((S, 16 * HEAD), F32), jax.ShapeDtypeStruct((T, 16 * HEAD), F32),
                 jax.ShapeDtypeStruct((T, 16 * HEAD), F32), jax.ShapeDtypeStruct(bias_s.shape, F32)],
                [row_spec, kv_out, kv_out, b_spec], [], ("arbitrary", "arbitrary"), carry)
    return _split(res, 4, carry)


def _na_onehots():
    o = np.arange(NA_KH)[:, None]
    i = np.arange(NA_KH)[None, :]
    a = i - o + NA_KH - 1
    A = (a[..., None] == np.arange(2 * NA_KH - 1)).astype(np.float32)
    qc = np.arange(GRID_W)[:, None]
    kc = np.arange(GRID_W)[None, :]
    b = np.clip(kc - qc + NA_KW - 1, 0, 2 * NA_KW - 2)
    cs = np.clip(qc - NA_KW // 2, 0, GRID_W - NA_KW)
    valid = (kc >= cs) & (kc < cs + NA_KW)
    B = ((b[..., None] == np.arange(2 * NA_KW - 1)) & valid[..., None]).astype(np.float32)
    return A, B, valid


def na_bias_table(rpb):
    A, B, valid = _na_onehots()
    hp = lax.Precision.HIGHEST
    t = jnp.einsum('hab,oia->hoib', rpb, jnp.asarray(A), precision=hp)
    bias = jnp.einsum('hoib,qkb->hoqik', t, jnp.asarray(B), precision=hp)
    bias = jnp.where(jnp.asarray(valid)[None, None, :, None, :], bias, NEG)
    return bias.reshape(rpb.shape[0], NA_KH, GRID_W, NA_KH * GRID_W)


def na_bias_grad(name, dbias8):
    A, B, _ = _na_onehots()
    H = dbias8.shape[0]
    nb, na = 2 * NA_KW - 1, 2 * NA_KH - 1
    d = dbias8.reshape(H, NA_KH, GRID_W, NA_KH, GRID_W).transpose(0, 1, 3, 2, 4)
    d = d.reshape(H * NA_KH * NA_KH, GRID_W * GRID_W)
    Bp = np.zeros((GRID_W * GRID_W, 128), np.float32)
    Bp[:, :nb] = B.reshape(GRID_W * GRID_W, nb)
    Ap = np.zeros((16, NA_KH * NA_KH), np.float32)
    Ap[:na] = A.reshape(NA_KH * NA_KH, na).T
    rows_per_head = NA_KH * NA_KH

    def split3(x):
        hi = x.astype(BF16)
        r1 = x - hi.astype(F32)
        mid = r1.astype(BF16)
        return hi, mid, (r1 - mid.astype(F32)).astype(BF16)

    def body(d_ref, b_ref, a_ref, o_ref):
        bm, am = b_ref[...], a_ref[...]
        g = sum(lax.dot_general(p, bm, NN, preferred_element_type=F32) for p in split3(d_ref[...]))
        o_ref[...] = sum(lax.dot_general(am, p, NN, preferred_element_type=F32) for p in split3(g))

    out = pl.pallas_call(
        body, grid=(H,),
        in_specs=[pl.BlockSpec((rows_per_head, GRID_W * GRID_W), lambda h: (h, 0)),
                  pl.BlockSpec((GRID_W * GRID_W, 128), lambda h: (0, 0)),
                  pl.BlockSpec((16, rows_per_head), lambda h: (0, 0))],
        out_specs=pl.BlockSpec((None, 16, 128), lambda h: (h, 0, 0)),
        out_shape=jax.ShapeDtypeStruct((H, 16, 128), F32),
        compiler_params=_cparams(("parallel",)), name=name)(d, jnp.asarray(Bp, BF16), jnp.asarray(Ap, BF16))
    return out[:, :na, :nb]


def _vmem_call(name, fn, out_shape, *arrays):
    def body(*refs):
        n = len(arrays)
        res = fn(*[r[...] for r in refs[:n]])
        if not isinstance(res, (tuple, list)):
            res = (res,)
        for o, v in zip(refs[n:], res):
            o[...] = v
    return pl.pallas_call(body, out_shape=out_shape, name=name,
                          compiler_params=pltpu.CompilerParams(vmem_limit_bytes=VMEM_LIMIT))(*arrays)


def _silu(v):
    return v / (1.0 + jnp.exp(-v))


def _adamw_math(w, g, m, v):
    m2 = ADAM_B1 * m + (1.0 - ADAM_B1) * g
    v2 = ADAM_B2 * v + (1.0 - ADAM_B2) * (g * g)
    m_hat = m2 / (1.0 - ADAM_B1 ** ADAM_STEP)
    v_hat = v2 / (1.0 - ADAM_B2 ** ADAM_STEP)
    delta = -ADAM_LR * (m_hat / (jnp.sqrt(v_hat) + ADAM_EPS) + ADAM_WD * w)
    return delta, m2, v2


def _ew_tile(R, C):
    return _tile(R, max(64, (262144 // C) // 64 * 64))


def adamw_rows(name, w, g, m, v, extra_g=None):
    R, C = w.shape
    tr = _ew_tile(R, C)
    extra_g = list(extra_g or [])
    ne = len(extra_g)

    def body(*refs):
        w_ref, g_ref, m_ref, v_ref = refs[:4]
        gs = g_ref[...]
        for e in refs[4:4 + ne]:
            gs = gs + e[...].astype(F32)
        go, do, mo, vo = refs[4 + ne:]
        d, m2, v2 = _adamw_math(w_ref[...], gs, m_ref[...], v_ref[...])
        go[...] = gs
        do[...] = d
        mo[...] = m2
        vo[...] = v2

    spec = pl.BlockSpec((tr, C), lambda i: (i, 0))
    return pl.pallas_call(
        body, grid=(R // tr,), in_specs=[spec] * (4 + ne), out_specs=[spec] * 4,
        out_shape=[jax.ShapeDtypeStruct((R, C), F32)] * 4,
        compiler_params=_cparams(("parallel",)), name=name)(w, g, m, v, *extra_g)


def rs_chip_sum(name, g8, sib4, where):
    _, R, C = g8.shape
    tr = _ew_tile(R, C)

    def body(s_ref, g_ref, b_ref, o_ref):
        o_ref[...] = (g_ref[...] + b_ref[...]).astype(BF16)

    def chip(q, s):
        return (s[1] + 1 + q) % 4

    blk = (None, tr, C)
    grid_spec = pltpu.PrefetchScalarGridSpec(
        num_scalar_prefetch=1, grid=(3, R // tr),
        in_specs=[pl.BlockSpec(blk, lambda q, i, s: (s[0] + chip(q, s), i, 0)),
                  pl.BlockSpec(blk, lambda q, i, s: (chip(q, s), i, 0))],
        out_specs=pl.BlockSpec(blk, lambda q, i, s: (chip(q, s), i, 0)))
    return pl.pallas_call(body, grid_spec=grid_spec, out_shape=jax.ShapeDtypeStruct((4, R, C), BF16),
                          compiler_params=_cparams(("parallel", "parallel")), name=name)(where, g8, sib4)


def adamw_rs(name, w, g8, sib4, rem3, m, v, idx, layer, prev=None):
    L, R, C = w.shape
    tr = _ew_tile(R, C)

    def body(s_ref, w_ref, g_ref, sb_ref, r0_ref, r1_ref, r2_ref, m_ref, v_ref, *rest):
        go, do, mo, vo = rest[-4:]
        gs = g_ref[...] + sb_ref[...]
        for r_ref in (r0_ref, r1_ref, r2_ref):
            gs = gs + r_ref[...].astype(F32)
        d, m2, v2 = _adamw_math(w_ref[...], gs, m_ref[...], v_ref[...])
        go[...] = gs
        do[...] = d
        mo[...] = m2
        vo[...] = v2

    blk = (None, tr, C)
    mine = pl.BlockSpec(blk, lambda i, s: (layer, i, 0))

    def rem(k):
        return pl.BlockSpec(blk, lambda i, s: (k, i, 0))

    prev = list(prev or [])
    grid_spec = pltpu.PrefetchScalarGridSpec(
        num_scalar_prefetch=1, grid=(R // tr,),
        in_specs=[mine, pl.BlockSpec(blk, lambda i, s: (s[0], i, 0)), pl.BlockSpec(blk, lambda i, s: (s[1], i, 0)),
                  rem(0), rem(1), rem(2), mine, mine] + [pl.BlockSpec(memory_space=pl.ANY)] * len(prev),
        out_specs=[mine] * 4)
    return pl.pallas_call(body, grid_spec=grid_spec, out_shape=[jax.ShapeDtypeStruct((L, R, C), F32)] * 4,
                          input_output_aliases={9 + k: k for k in range(len(prev))},
                          compiler_params=_cparams(("parallel",)), name=name)(
                              idx, w, g8, sib4, rem3, rem3, rem3, m, v, *prev)


def _me():
    x, y, c = lax.axis_index("x"), lax.axis_index("y"), lax.axis_index("c")
    return x, y, c


def _flip(v, bit):
    return 1 - v if bit else v


def ag_small(name, x, with_sum=False):
    R, C = x.shape

    def body(x_ref, out_ref, *rest):
        if with_sum:
            sum_ref, send_sems, recv_sems, lsem = rest
        else:
            send_sems, recv_sems, lsem = rest
        mx, my, mc = _me()
        me = 4 * mx + 2 * my + mc
        local = pltpu.make_async_copy(x_ref, out_ref.at[me], lsem)
        local.start()
        sends = []
        for k in range(1, NDEV):
            peer = (_flip(mx, k & 4), _flip(my, k & 2), _flip(mc, k & 1))
            cp = pltpu.make_async_remote_copy(src_ref=x_ref, dst_ref=out_ref.at[me], send_sem=send_sems.at[k - 1],
                                              recv_sem=recv_sems.at[k - 1], device_id=peer, device_id_type=MESH)
            cp.start()
            sends.append(cp)
        for k in range(1, NDEV):
            px, py, pc = _flip(mx, k & 4), _flip(my, k & 2), _flip(mc, k & 1)
            pltpu.make_async_remote_copy(src_ref=x_ref, dst_ref=out_ref.at[4 * px + 2 * py + pc],
                                         send_sem=send_sems.at[k - 1], recv_sem=recv_sems.at[k - 1],
                                         device_id=(px, py, pc), device_id_type=MESH).wait_recv()
        for cp in sends:
            cp.wait_send()
        local.wait()
        if with_sum:
            acc = out_ref[0]
            for d in range(1, NDEV):
                acc = acc + out_ref[d]
            sum_ref[...] = acc

    out_shape = [jax.ShapeDtypeStruct((NDEV, R, C), F32)]
    if with_sum:
        out_shape.append(jax.ShapeDtypeStruct((R, C), F32))
    vm = pl.BlockSpec(memory_space=pltpu.VMEM)
    res = pl.pallas_call(
        body, out_shape=out_shape, in_specs=[vm], out_specs=[vm] * len(out_shape),
        scratch_shapes=[pltpu.SemaphoreType.DMA((NDEV - 1,)), pltpu.SemaphoreType.DMA((NDEV - 1,)),
                        pltpu.SemaphoreType.DMA],
        compiler_params=pltpu.CompilerParams(vmem_limit_bytes=VMEM_LIMIT), name=name)(x)
    return res if with_sum else res[0]


def ag_big(name, shards):
    n = len(shards)

    def body(*refs):
        ins, outs = refs[:n], refs[n:2 * n]
        send_sems, recv_sems, lsems = refs[2 * n:]
        mx, my, mc = _me()
        me = (mx, my, mc)
        sibling = (mx, my, 1 - mc)
        chips = [(1 - mx, my), (mx, 1 - my), (1 - mx, 1 - my)]

        def idx(p):
            return 4 * p[0] + 2 * p[1] + p[2]

        def copy(t, k, block, to, src=None):
            dst = outs[t].at[idx(block)]
            return pltpu.make_async_remote_copy(
                src_ref=dst if src is None else src, dst_ref=dst, send_sem=send_sems.at[7 * t + k],
                recv_sem=recv_sems.at[7 * t + k], device_id=to, device_id_type=MESH)

        started = []
        locals_ = []
        for t in range(n):
            mine = pltpu.make_async_copy(ins[t], outs[t].at[idx(me)], lsems.at[t])
            mine.start()
            locals_.append(mine)
            first = [copy(t, 0, me, sibling, src=ins[t])]
            first += [copy(t, 1 + j, me, (*chip, mc), src=ins[t]) for j, chip in enumerate(chips)]
            for cp in first:
                cp.start()
            started += first
        for t in range(n):
            for j, chip in enumerate(chips):
                copy(t, 1 + j, (*chip, mc), me).wait_recv()
                fwd = copy(t, 4 + j, (*chip, mc), sibling)
                fwd.start()
                started.append(fwd)
        for t in range(n):
            copy(t, 0, sibling, me).wait_recv()
            for j, chip in enumerate(chips):
                copy(t, 4 + j, (*chip, 1 - mc), me).wait_recv()
        for cp in started:
            cp.wait_send()
        for mine in locals_:
            mine.wait()

    anyspec = pl.BlockSpec(memory_space=pl.ANY)
    return pl.pallas_call(
        body, out_shape=[jax.ShapeDtypeStruct((NDEV,) + s.shape, s.dtype) for s in shards],
        in_specs=[anyspec] * n, out_specs=[anyspec] * n,
        scratch_shapes=[pltpu.SemaphoreType.DMA((7 * n,)), pltpu.SemaphoreType.DMA((7 * n,)),
                        pltpu.SemaphoreType.DMA((n,))],
        name=name)(*shards)


def _idx(p):
    return 4 * p[0] + 2 * p[1] + p[2]


def _remote(src, dst, ss, rs, k, to):
    return pltpu.make_async_remote_copy(src_ref=src, dst_ref=dst, send_sem=ss.at[k], recv_sem=rs.at[k],
                                        device_id=to, device_id_type=MESH)


def ex_ag_chips(shards):
    n = len(shards)

    def copies(ci, co, ss, rs, base):
        mx, my, mc = _me()
        me = (mx, my, mc)
        peers = [(mx, my, 1 - mc), (1 - mx, my, mc), (mx, 1 - my, mc), (1 - mx, 1 - my, mc)]
        sends, recvs, local = [], [], []
        for t in range(n):
            b = base + 5 * t
            for k, peer in enumerate(peers):
                sends.append(_remote(ci[t], co[t].at[_idx(me)], ss, rs, b + k, peer))
                recvs.append(_remote(ci[t], co[t].at[_idx(peer)], ss, rs, b + k, peer))
            local.append(pltpu.make_async_copy(ci[t], co[t].at[_idx(me)], ss.at[b + 4]))
        return sends, recvs, local

    def start(ci, co, ss, rs, base):
        sends, _, local = copies(ci, co, ss, rs, base)
        for cp in local + sends:
            cp.start()

    def finish(ci, co, ss, rs, base):
        sends, recvs, local = copies(ci, co, ss, rs, base)
        for cp in recvs:
            cp.wait_recv()
        for cp in sends:
            cp.wait_send()
        for cp in local:
            cp.wait()

    outs = [jax.ShapeDtypeStruct((NDEV,) + s.shape, s.dtype) for s in shards]
    return Exchange(shards, outs, {}, 5 * n, start, finish)


def ex_ag_sibling(bufs):
    n = len(bufs)

    def copies(co, ss, rs, base):
        mx, my, mc = _me()
        sibling = (mx, my, 1 - mc)
        chips = [(1 - mx, my), (mx, 1 - my), (1 - mx, 1 - my)]
        sends, recvs = [], []
        for t in range(n):
            for j, chip in enumerate(chips):
                mine, theirs = co[t].at[_idx((*chip, mc))], co[t].at[_idx((*chip, 1 - mc))]
                sends.append(_remote(mine, mine, ss, rs, base + 3 * t + j, sibling))
                recvs.append(_remote(mine, theirs, ss, rs, base + 3 * t + j, sibling))
        return sends, recvs

    def start(ci, co, ss, rs, base):
        for cp in copies(co, ss, rs, base)[0]:
            cp.start()

    def finish(ci, co, ss, rs, base):
        sends, recvs = copies(co, ss, rs, base)
        for cp in recvs:
            cp.wait_recv()
        for cp in sends:
            cp.wait_send()

    outs = [jax.ShapeDtypeStruct(b.shape, b.dtype) for b in bufs]
    return Exchange(bufs, outs, {t: t for t in range(n)}, 3 * n, start, finish)


def ex_rs_sibling(grads):
    n = len(grads)

    def copies(ci, co, ss, rs, base):
        mx, my, mc = _me()
        return [_remote(ci[t].at[pl.ds((1 - mc) * 4, 4)], co[t], ss, rs, base + t, (mx, my, 1 - mc)) for t in range(n)]

    def start(ci, co, ss, rs, base):
        for cp in copies(ci, co, ss, rs, base):
            cp.start()

    def finish(ci, co, ss, rs, base):
        for cp in copies(ci, co, ss, rs, base):
            cp.wait()

    outs = [jax.ShapeDtypeStruct((4,) + g.shape[1:], g.dtype) for g in grads]
    return Exchange(grads, outs, {}, n, start, finish)


def ex_rs_chips(parts):
    n = len(parts)

    def copies(ci, co, ss, rs, base):
        mx, my, mc = _me()
        cps = []
        for t in range(n):
            for k in range(1, 4):
                px, py = _flip(mx, k & 2), _flip(my, k & 1)
                cps.append(_remote(ci[t].at[2 * px + py], co[t].at[k - 1], ss, rs, base + 3 * t + k - 1, (px, py, mc)))
        return cps

    def start(ci, co, ss, rs, base):
        for cp in copies(ci, co, ss, rs, base):
            cp.start()

    def finish(ci, co, ss, rs, base):
        for cp in copies(ci, co, ss, rs, base):
            cp.wait()

    outs = [jax.ShapeDtypeStruct((3,) + p.shape[1:], p.dtype) for p in parts]
    return Exchange(parts, outs, {}, 3 * n, start, finish)


def run_exchanges(name, xs):
    x = merge_exchanges(xs)
    n_ci, n_co = len(x.ins), len(x.out_shapes)

    def body(*refs):
        ci, co = refs[:n_ci], refs[n_ci:n_ci + n_co]
        ss, rs = refs[n_ci + n_co:]
        x.start(ci, co, ss, rs, 0)
        x.finish(ci, co, ss, rs, 0)

    hbm = pl.BlockSpec(memory_space=pl.ANY)
    return pl.pallas_call(
        body, out_shape=x.out_shapes, in_specs=[hbm] * n_ci, out_specs=[hbm] * n_co, input_output_aliases=x.aliases,
        scratch_shapes=[pltpu.SemaphoreType.DMA((x.n_sems,)), pltpu.SemaphoreType.DMA((x.n_sems,))], name=name)(*x.ins)


def _rope_tables(S, nctx):
    t = jnp.arange(S)
    row = (t // GRID_W).astype(F32)
    col = (t % GRID_W).astype(F32)
    pairs = HEAD // 4
    inv = ROPE_THETA ** (-jnp.arange(pairs, dtype=F32) / pairs)
    ang_r = row[:, None] * inv
    ang_c = col[:, None] * inv
    ang = jnp.concatenate([ang_r, ang_r, ang_c, ang_c], axis=-1)
    cos = jnp.concatenate([jnp.ones((nctx, HEAD), F32), jnp.cos(ang)], axis=0)
    sin = jnp.concatenate([jnp.zeros((nctx, HEAD), F32), jnp.sin(ang)], axis=0)
    lane = jnp.arange(HEAD)[None, :]
    first = (lane & 32) == 0
    return cos, jnp.where(first, -sin, 0.0), jnp.where(first, 0.0, sin)


def _pad_rows(v, rows):
    v = v.reshape(-1).astype(F32)
    return jnp.pad(v, (0, rows * 128 - v.shape[0])).reshape(rows, 128)


def _rows8(n):
    return -(-n // 1024) * 8


def kernel(x, c, ctx, c_ctx, ada_w, ada_b, norm_w, mlp_w1, mlp_w2, ev_w_in, ev_w_out, ev_q_norm, ev_k_norm, ev_sink, od_w_in, od_w_out, od_rpb, final_norm_w, loss_target, m_c_ctx, m_ada_w, m_ada_b, m_norm_w, m_mlp_w1, m_mlp_w2, m_ev_w_in, m_ev_w_out, m_ev_q_norm, m_ev_k_norm, m_ev_sink, m_od_w_in, m_od_w_out, m_od_rpb, m_final_norm_w, v_c_ctx, v_ada_w, v_ada_b, v_norm_w, v_mlp_w1, v_mlp_w2, v_ev_w_in, v_ev_w_out, v_ev_q_norm, v_ev_k_norm, v_ev_sink, v_od_w_in, v_od_w_out, v_od_rpb, v_final_norm_w):
    S, D = x.shape[1], x.shape[2]
    NC = ctx.shape[1]
    T = NC + S
    assert NC == ROW_TILE and S % GRID_W == 0
    ada_cols = ada_w.shape[2]
    nw_cols = norm_w.shape[2]
    me = 4 * lax.axis_index("x") + 2 * lax.axis_index("y") + lax.axis_index("c")

    pack1 = jnp.concatenate([_pad_rows(c, _rows8(D)), _pad_rows(norm_w, _rows8(4 * nw_cols))], axis=0)
    g1 = ag_small("ag_c_normw", pack1)
    c_all = g1[:, :D // 128].reshape(NDEV, D)
    nw_rows = _rows8(D)
    nw = g1[:, nw_rows:nw_rows + 4 * nw_cols // 128].reshape(NDEV, 2, 2, nw_cols)
    nw = nw.transpose(1, 2, 0, 3).reshape(2, 2, D)
    cin = jnp.concatenate([c_all, jnp.broadcast_to(c_ctx[None], (NDEV, D))], axis=0)
    act = _vmem_call("silu_c", lambda v: _silu(v).astype(BF16), jax.ShapeDtypeStruct((2 * NDEV, D), BF16), cin)
    ada_b_loc = lax.dynamic_slice_in_dim(ada_b, me * ada_cols, ada_cols, axis=1)
    mods = [mm_nn(f"mod{i}", act, ada_w[i], _epi_bias, [F32], extras=(ada_b_loc[i:i + 1],), extra_kinds=('n',))[0]
            for i in range(2)]
    gm = ag_small("ag_mod", jnp.concatenate(mods, axis=1))
    gm = gm.reshape(NDEV, 2 * NDEV, 2, ada_cols).transpose(2, 1, 0, 3).reshape(2, 2 * NDEV, NDEV * ada_cols)
    mod_lat = lax.dynamic_index_in_dim(gm, me, axis=1, keepdims=False)
    mod_ctx = gm[:, NDEV]
    mod2 = jnp.stack([mod_ctx, mod_lat], axis=1).reshape(2, 2, 6, D)

    def chunk(i, j):
        return mod2[i, :, j, :]

    def b16(w):
        return w.astype(BF16)

    (w_in_e,) = ag_big("ag_weights_l0_qkv", [b16(ev_w_in[0])])

    cos, sa, sb = _rope_tables(S, NC)
    bias8 = na_span_bias(na_bias_table(od_rpb[0]))
    sink = ev_sink[0]
    TQ_F, TQ_B = 256, 256

    X0 = jnp.concatenate([ctx[0], x[0]], axis=0)
    h_a = norm_mod("l0_norm1", X0, nw[0, 0][None], chunk(0, 0), chunk(0, 1), NC)
    (qkv0,), (w_out_e_half,) = mm_nn("l0_qkv", h_a, w_in_e, _epi_store(F32), [F32],
                                     carry=[ex_ag_chips([b16(ev_w_out[0])])])
    qkvh0 = prep_even("l0_prep", qkv0, ev_q_norm, ev_k_norm, cos, sa, sb)
    o0, (w1_0_half, w2_0_half, w_out_o_half, w_out_e) = attn_even_fwd(
        "l0_attn", qkvh0, sink, NC, TQ_F,
        carry=[ex_ag_chips([b16(mlp_w1[0]), b16(mlp_w2[0]), b16(od_w_out[0])]), ex_ag_sibling([w_out_e_half])])
    w_out_e = w_out_e.reshape(-1, D)
    tm0 = _tile(T, 1100)
    (X1, y0), (w1_0, w2_0) = mm_nn("l0_out", o0, w_out_e, _epi_resid_gate(NC, tm0), [F32, F32],
                                   extras=(X0, chunk(0, 2)), extra_kinds=('mn', 'n'),
                                   carry=[ex_ag_sibling([w1_0_half, w2_0_half])])
    h_b = norm_mod("l0_norm2", X1, nw[0, 1][None], chunk(0, 3), chunk(0, 4), NC)
    (a0, r0), (w_in_o_half, w_out_o) = mm_nn(
        "l0_up", h_b, w1_0, _epi_relu2, [BF16, BF16], tn_cap=1024,
        carry=[ex_ag_chips([b16(od_w_in[0])]), ex_ag_sibling([w_out_o_half])])
    (X2, z0), (w1_1_half, w_in_o) = mm_nn(
        "l0_down", a0, w2_0.reshape(-1, D), _epi_resid_gate(NC, tm0), [F32, F32], extras=(X1, chunk(0, 5)),
        extra_kinds=('mn', 'n'), tn_cap=1024, carry=[ex_ag_chips([b16(mlp_w1[1])]), ex_ag_sibling([w_in_o_half])])
    w_out_o = w_out_o.reshape(-1, D)

    h_c = norm_mod("l1_norm1", X2, nw[1, 0][None], chunk(1, 0), chunk(1, 1), NC)
    (qkv1,), (w1_1,) = mm_nn("l1_qkv", h_c, w_in_o, _epi_store(BF16), [BF16], tn_cap=768,
                             carry=[ex_ag_sibling([w1_1_half])])
    o1, (w2_1_half,) = attn_odd_fwd("l1_attn", qkv1, bias8, NC, carry=[ex_ag_chips([b16(mlp_w2[1])])])
    X2l = X2[NC:]
    tm1 = _tile(S, 1100)
    (X3, y1), (w2_1,) = mm_nn("l1_out", o1, w_out_o, _epi_resid_gate(0, tm1), [F32, F32],
                              extras=(X2l, chunk(1, 2)), extra_kinds=('mn', 'n'),
                              carry=[ex_ag_sibling([w2_1_half])])
    h_d = norm_mod("l1_norm2", X3, nw[1, 1][None], chunk(1, 3), chunk(1, 4), 0)
    a1, r1 = mm_nn("l1_up", h_d, w1_1, _epi_relu2, [BF16, BF16], tn_cap=1024)
    X4, z1 = mm_nn("l1_down", a1, w2_1.reshape(-1, D), _epi_resid_gate(0, tm1), [F32, F32], extras=(X3, chunk(1, 5)),
                   extra_kinds=('mn', 'n'), tn_cap=1024)
    dX4, loss_p, dfw_p = final_loss("final_loss", X4, final_norm_w[None], loss_target[0])
    w_in = [w_in_e, w_in_o]
    w_out = [w_out_e, w_out_o]
    w1 = [w1_0, w1_1]
    w2 = [w2_0.reshape(-1, D), w2_1.reshape(-1, D)]

    mc4 = (lax.axis_index("c") * 4).astype(jnp.int32)
    my_chip = (2 * lax.axis_index("x") + lax.axis_index("y")).astype(jnp.int32)

    def chip_sum(tag, g8, sib4):
        return rs_chip_sum(f"rs_chip_sum_{tag}", g8, sib4, jnp.stack([mc4, my_chip]))

    dz1, pg2_1 = gate_bwd("l1_gate2_bwd", dX4, z1, chunk(1, 5), 0)
    du1 = mm_nt("l1_down_dx", dz1, w2[1], _epi_mul2r, BF16, extras=(r1,))
    g_w1_1 = mm_tn("l1_up_dw", h_d, du1, 1)
    g_w2_1, (sib_w1_1,) = mm_tn("l1_down_dw", a1, dz1, 0, carry=[ex_rs_sibling([g_w1_1])])
    dh_d, (rem_w1_1, sib_w2_1) = mm_nt(
        "l1_up_dx", du1, w1[1], _epi_store(F32), F32,
        carry=[ex_rs_chips([chip_sum("w1_1", g_w1_1, sib_w1_1)]), ex_rs_sibling([g_w2_1])])
    dX3, pn2_1 = norm_bwd("l1_norm2_bwd", X3, dh_d, dX4, nw[1, 1][None], chunk(1, 4), 0)
    dy1, pg1_1 = gate_bwd("l1_gate1_bwd", dX3, y1, chunk(1, 2), 0)
    do1 = mm_nt("l1_out_dx", dy1, w_out[1], _epi_store(BF16), BF16)
    g_wout_1 = mm_tn("l1_out_dw", o1, dy1, 0)
    (dq1, dk1, dv1, dbias8), (rem_w2_1, sib_wout_1) = attn_odd_bwd(
        "l1_attn_bwd", qkv1, bias8, do1, NC,
        carry=[ex_rs_chips([chip_sum("w2_1", g_w2_1, sib_w2_1)]), ex_rs_sibling([g_wout_1])])
    dqkv1 = jnp.concatenate([jnp.pad(dq1, ((NC, 0), (0, 0))), dk1, dv1], axis=1).astype(BF16)
    dh_c, (rem_wout_1,) = mm_nt("l1_qkv_dx", dqkv1, w_in[1], _epi_store(F32), F32,
                                carry=[ex_rs_chips([chip_sum("wout_1", g_wout_1, sib_wout_1)])])
    g_win_1 = mm_tn("l1_qkv_dw", h_c, dqkv1, 1)
    dX2, pn1_1 = norm_bwd("l1_norm1_bwd", X2, dh_c, dX3, nw[1, 0][None], chunk(1, 1), NC, dres_skip=NC)
    d_rpb = na_bias_grad("rpb_grad", na_span_bias_grad(dbias8))

    dz0, pg2_0 = gate_bwd("l0_gate2_bwd", dX2, z0, chunk(0, 5), NC)
    du0, (sib_win_1,) = mm_nt("l0_down_dx", dz0, w2[0], _epi_mul2r, BF16, extras=(r0,),
                              carry=[ex_rs_sibling([g_win_1])])
    g_w1_0, (rem_win_1,) = mm_tn("l0_up_dw", h_b, du0, 1,
                                 carry=[ex_rs_chips([chip_sum("win_1", g_win_1, sib_win_1)])])
    g_w2_0, (sib_w1_0,) = mm_tn("l0_down_dw", a0, dz0, 0, carry=[ex_rs_sibling([g_w1_0])])
    dh_b, (rem_w1_0, sib_w2_0) = mm_nt(
        "l0_up_dx", du0, w1[0], _epi_store(F32), F32,
        carry=[ex_rs_chips([chip_sum("w1_0", g_w1_0, sib_w1_0)]), ex_rs_sibling([g_w2_0])])
    dX1, pn2_0 = norm_bwd("l0_norm2_bwd", X1, dh_b, dX2, nw[0, 1][None], chunk(0, 4), NC)
    dy0, pg1_0 = gate_bwd("l0_gate1_bwd", dX1, y0, chunk(0, 2), NC)
    do0 = mm_nt("l0_out_dx", dy0, w_out[0], _epi_store(BF16), BF16)
    g_wout_0 = mm_tn("l0_out_dw", o0, dy0, 0)
    (dq0, dk0, dv0, dsink_p), (rem_w2_0, sib_wout_0) = attn_even_bwd(
        "l0_attn_bwd", qkvh0, sink, do0, NC, TQ_B,
        carry=[ex_rs_chips([chip_sum("w2_0", g_w2_0, sib_w2_0)]), ex_rs_sibling([g_wout_0])])
    dqkv0, pqk = prep_even_bwd("l0_prep_bwd", qkv0, dq0, dk0, dv0, ev_q_norm, ev_k_norm, cos, sa, sb)
    g_win_0, (rem_wout_0,) = mm_tn("l0_qkv_dw", h_a, dqkv0, 1,
                                   carry=[ex_rs_chips([chip_sum("wout_0", g_wout_0, sib_wout_0)])])
    dh_a, (sib_win_0,) = mm_nt("l0_qkv_dx", dqkv0, w_in[0], _epi_store(F32), F32, carry=[ex_rs_sibling([g_win_0])])
    (dx_lat, pn1_0), (rem_win_0,) = norm_bwd(
        "l0_norm1_bwd", X0, dh_a, dX1, nw[0, 0][None], chunk(0, 1), NC, out_skip=NC,
        carry=[ex_rs_chips([chip_sum("win_0", g_win_0, sib_win_0)])])
    grad_x = dx_lat[None]

    def dmod(grp, pn1, pg1, pn2, pg2):
        return jnp.concatenate([pn1[grp], pn1[2 + grp], pg1[grp], pn2[grp], pn2[2 + grp], pg2[grp]])

    dmod_lat = jnp.stack([dmod(1, pn1_0, pg1_0, pn2_0, pg2_0), dmod(1, pn1_1, pg1_1, pn2_1, pg2_1)])
    dmod_ctx = jnp.stack([dmod(0, pn1_0, pg1_0, pn2_0, pg2_0), dmod(0, pn1_1, pg1_1, pn2_1, pg2_1)])
    dnw_p = jnp.stack([pn1_0[4], pn2_0[4], pn1_1[4], pn2_1[4]])
    pieces = [dmod_lat, dmod_ctx, dnw_p, pqk[0], pqk[1], dsink_p[8:, 0, 0], d_rpb, dfw_p[0], loss_p[0, 0]]
    sizes = [int(np.prod(p.shape)) for p in pieces]
    rows = [_rows8(s) for s in sizes]
    pack2 = jnp.concatenate([_pad_rows(p, r) for p, r in zip(pieces, rows)], axis=0)
    g2, s2 = ag_small("ag_small_grads", pack2, with_sum=True)
    offs = np.concatenate([[0], np.cumsum(rows)])

    def piece(arr, i, shape):
        return arr[..., offs[i]:offs[i + 1], :].reshape(arr.shape[:-2] + (-1,))[..., :sizes[i]].reshape(
            arr.shape[:-2] + shape)

    dmod_all = piece(g2, 0, (2, 6 * D))
    dmodc_sum = piece(s2, 1, (2, 6 * D))
    dnw_sum = piece(s2, 2, (2, 2, D))
    g_qn = piece(s2, 3, ev_q_norm.shape)
    g_kn = piece(s2, 4, ev_k_norm.shape)
    g_sink = piece(s2, 5, ev_sink.shape)
    g_rpb = piece(s2, 6, od_rpb.shape)
    g_fw = piece(s2, 7, final_norm_w.shape)
    loss = piece(s2, 8, ())

    dm16 = jnp.concatenate([dmod_all.transpose(1, 0, 2), dmodc_sum[:, None, :],
                            jnp.zeros((2, NDEV - 1, 6 * D), F32)], axis=1)
    dm16_loc = lax.dynamic_slice_in_dim(dm16.reshape(2, 2 * NDEV, NDEV, ada_cols), me, 1, axis=2)[:, :, 0, :]
    g_ada_b = _vmem_call("ada_b_grad", lambda v: jnp.sum(v, axis=1),
                         jax.ShapeDtypeStruct((2, 6 * D), F32), dm16)
    g_ada_w = []
    dact_p = None
    for i in range(2):
        dmb = dm16_loc[i].astype(BF16)
        g_ada_w.append(mm_tn(f"ada_w_grad{i}", act, dmb, None))
        part = mm_nt(f"ada_dact{i}", dmb, ada_w[i], _epi_store(F32), F32)
        dact_p = part if dact_p is None else dact_p + part
    _, dact = ag_small("ag_cctx", dact_p, with_sum=True)

    def cctx_grad(da, cc):
        sg = 1.0 / (1.0 + jnp.exp(-cc))
        return da[NDEV:NDEV + 1] * (sg * (1.0 + cc * (1.0 - sg)))

    g_cctx = _vmem_call("cctx_grad", cctx_grad, jax.ShapeDtypeStruct((1, D), F32), dact, c_ctx[None])[0]

    grads = [g_win_0, g_wout_0, g_w1_0, g_w2_0, g_win_1, g_wout_1, g_w1_1, g_w2_1]
    sib = [sib_win_0, sib_wout_0, sib_w1_0, sib_w2_0, sib_win_1, sib_wout_1, sib_w1_1, sib_w2_1]
    rem = [rem_win_0, rem_wout_0, rem_w1_0, rem_w2_0, rem_win_1, rem_wout_1, rem_w1_1, rem_w2_1]
    own_idx = jnp.stack([mc4 + my_chip, my_chip])

    def big(tag, w, m, v, ts):
        res = None
        for l, t in enumerate(ts):
            res = adamw_rs(f"adamw_{tag}_{l}", w, grads[t], sib[t], rem[t], m, v, own_idx, l, res)
        return tuple(res)

    r_ev_w_in = big('ev_w_in', ev_w_in, m_ev_w_in, v_ev_w_in, [0])
    r_ev_w_out = big('ev_w_out', ev_w_out, m_ev_w_out, v_ev_w_out, [1])
    r_mlp_w1 = big('mlp_w1', mlp_w1, m_mlp_w1, v_mlp_w1, [2, 6])
    r_mlp_w2 = big('mlp_w2', mlp_w2, m_mlp_w2, v_mlp_w2, [3, 7])
    r_od_w_in = big('od_w_in', od_w_in, m_od_w_in, v_od_w_in, [4])
    r_od_w_out = big('od_w_out', od_w_out, m_od_w_out, v_od_w_out, [5])

    g_ada = jnp.stack(g_ada_w)
    r_ada_w = adamw_rows("adamw_ada_w", ada_w.reshape(2 * D, ada_cols), g_ada.reshape(2 * D, ada_cols),
                         m_ada_w.reshape(2 * D, ada_cols), v_ada_w.reshape(2 * D, ada_cols))
    r_ada_w = tuple(u.reshape(2, D, ada_cols) for u in r_ada_w)

    g_nw_loc = lax.dynamic_slice_in_dim(dnw_sum, me * nw_cols, nw_cols, axis=2)
    small = [(c_ctx, g_cctx, m_c_ctx, v_c_ctx), (ada_b, g_ada_b, m_ada_b, v_ada_b),
             (norm_w, g_nw_loc, m_norm_w, v_norm_w), (ev_q_norm, g_qn, m_ev_q_norm, v_ev_q_norm),
             (ev_k_norm, g_kn, m_ev_k_norm, v_ev_k_norm), (ev_sink, g_sink, m_ev_sink, v_ev_sink),
             (od_rpb, g_rpb, m_od_rpb, v_od_rpb), (final_norm_w, g_fw, m_final_norm_w, v_final_norm_w)]
    srows = [_rows8(int(np.prod(w.shape))) for w, _, _, _ in small]
    packs = [jnp.concatenate([_pad_rows(tup[k], r) for tup, r in zip(small, srows)], axis=0) for k in range(4)]
    sres = adamw_rows("adamw_small", *packs)
    soffs = np.concatenate([[0], np.cumsum(srows)])

    def unpack(arr, i):
        w = small[i][0]
        return arr[soffs[i]:soffs[i + 1]].reshape(-1)[:int(np.prod(w.shape))].reshape(w.shape)

    sm = [[unpack(sres[k], i) for i in range(len(small))] for k in range(4)]

    def outs(k):
        big_k = {'ada_w': r_ada_w[k], 'mlp_w1': r_mlp_w1[k], 'mlp_w2': r_mlp_w2[k], 'ev_w_in': r_ev_w_in[k],
                 'ev_w_out': r_ev_w_out[k], 'od_w_in': r_od_w_in[k], 'od_w_out': r_od_w_out[k]}
        return (sm[k][0], big_k['ada_w'], sm[k][1], sm[k][2], big_k['mlp_w1'], big_k['mlp_w2'], big_k['ev_w_in'],
                big_k['ev_w_out'], sm[k][3], sm[k][4], sm[k][5], big_k['od_w_in'], big_k['od_w_out'], sm[k][6],
                sm[k][7])

    return (loss, grad_x, *outs(0), *outs(1), *outs(2), *outs(3))
```

```python
import numpy as np
import jax
import jax.numpy as jnp
from jax import lax
from jax.experimental import pallas as pl
from jax.experimental.pallas import tpu as pltpu

F32 = jnp.float32
BF16 = jnp.bfloat16
MESH = pl.DeviceIdType.MESH

NDEV = 8
HEAD = 128
GRID_W = 64
NA_KH, NA_KW = 8, 16
WINDOW = 128
ROPE_THETA = 10000.0
EPS = 1e-6
NEG = -1e30
SCALE = HEAD ** -0.5
ROW_TILE = 256
VMEM_LIMIT = 56 * 1024 * 1024

ADAM_LR, ADAM_B1, ADAM_B2, ADAM_EPS, ADAM_WD, ADAM_STEP = 0.001, 0.9, 0.999, 1e-08, 0.01, 10

NT = (((1,), (1,)), ((), ()))
NN = (((1,), (0,)), ((), ()))
TN = (((0,), (0,)), ((), ()))


def _cparams(sem):
    return pltpu.CompilerParams(dimension_semantics=sem, vmem_limit_bytes=VMEM_LIMIT)


def _tile(n, cap):
    if n <= cap:
        return n
    t = cap - cap % 64
    while t >= 64:
        if n % t == 0:
            return t
        t -= 64
    raise ValueError((n, cap))


def _dot(a, b, dims):
    return lax.dot_general(a.astype(BF16), b.astype(BF16), dims, preferred_element_type=F32)


def _slot(d):
    return (d % 2) * 4 + d // 2


class Exchange:
    def __init__(self, ins, out_shapes, aliases, n_sems, start, finish):
        self.ins, self.out_shapes, self.aliases, self.n_sems = list(ins), list(out_shapes), dict(aliases), n_sems
        self.start, self.finish = start, finish


def merge_exchanges(xs):
    ins, outs, aliases, bases, n = [], [], {}, [], 0
    for x in xs:
        bases.append((len(ins), len(outs), n))
        aliases.update({len(ins) + i: len(outs) + o for i, o in x.aliases.items()})
        ins += x.ins
        outs += x.out_shapes
        n += x.n_sems

    def run(which):
        def f(ci, co, ss, rs, base):
            for x, (i0, o0, s0) in zip(xs, bases):
                getattr(x, which)(ci[i0:i0 + len(x.ins)], co[o0:o0 + len(x.out_shapes)], ss, rs, base + s0)
        return f

    return Exchange(ins, outs, aliases, n, run('start'), run('finish'))


def _call(name, body, grid, ins, in_specs, out_shape, out_specs, scratch, sems, carry=None):
    if not carry:
        return pl.pallas_call(body, grid=grid, in_specs=in_specs, out_specs=out_specs, out_shape=out_shape,
                              scratch_shapes=scratch, compiler_params=_cparams(sems), name=name)(*ins)
    x = merge_exchanges(carry)
    n_in, n_ci, n_out, n_co, n_sc = len(ins), len(x.ins), len(out_shape), len(x.out_shapes), len(scratch)

    def wrapped(*refs):
        p = [0]

        def take(k):
            p[0] += k
            return refs[p[0] - k:p[0]]

        a, ci, o, co, sc = take(n_in), take(n_ci), take(n_out), take(n_co), take(n_sc)
        ss, rs = take(2)
        first = pl.program_id(0) == 0
        last = pl.program_id(0) == grid[0] - 1
        for d in range(1, len(grid)):
            first = jnp.logical_and(first, pl.program_id(d) == 0)
            last = jnp.logical_and(last, pl.program_id(d) == grid[d] - 1)

        @pl.when(first)
        def _():
            x.start(ci, co, ss, rs, 0)

        body(*a, *o, *sc)

        @pl.when(last)
        def _():
            x.finish(ci, co, ss, rs, 0)

    hbm = pl.BlockSpec(memory_space=pl.ANY)
    res = pl.pallas_call(
        wrapped, grid=grid, in_specs=list(in_specs) + [hbm] * n_ci, out_specs=list(out_specs) + [hbm] * n_co,
        out_shape=list(out_shape) + x.out_shapes,
        input_output_aliases={n_in + i: n_out + o for i, o in x.aliases.items()},
        scratch_shapes=list(scratch) + [pltpu.SemaphoreType.DMA((x.n_sems,)), pltpu.SemaphoreType.DMA((x.n_sems,))],
        compiler_params=_cparams(("arbitrary",) * len(grid)), name=name)(*ins, *x.ins)
    return list(res[:n_out]) + [list(res[n_out:])]


def _mm_core(name, grid, ins, in_specs, out_shape, out_specs, dims, acc_shape, epi, carry=None):
    nk = grid[2]
    n_extra = len(ins) - 2

    def body_single(*refs):
        epi(_dot(refs[0][...], refs[1][...], dims), refs[2:2 + n_extra], refs[2 + n_extra:])

    def body(*refs):
        a_ref, b_ref = refs[0], refs[1]
        ex = refs[2:2 + n_extra]
        outs = refs[2 + n_extra:-1]
        acc = refs[-1]
        k = pl.program_id(2)

        @pl.when(k == 0)
        def _():
            acc[...] = _dot(a_ref[...], b_ref[...], dims)

        @pl.when(jnp.logical_and(k > 0, k < nk - 1))
        def _():
            acc[...] += _dot(a_ref[...], b_ref[...], dims)

        @pl.when(k == nk - 1)
        def _():
            epi(acc[...] + _dot(a_ref[...], b_ref[...], dims), ex, outs)

    if nk == 1:
        return _call(name, body_single, grid, ins, in_specs, out_shape, out_specs, [],
                     ("parallel", "parallel", "arbitrary"), carry)
    return _call(name, body, grid, ins, in_specs, out_shape, out_specs, [pltpu.VMEM(acc_shape, F32)],
                 ("parallel", "parallel", "arbitrary"), carry)


def _split(res, n, carry):
    own = res[0] if n == 1 else list(res[:n])
    return (own, res[n]) if carry else own


def _epi_store(dtype):
    def epi(acc, ex, outs):
        outs[0][...] = acc.astype(dtype)
    return epi


def _epi_bias(acc, ex, outs):
    outs[0][...] = acc + ex[0][...]


def _epi_relu2(acc, ex, outs):
    r = jnp.maximum(acc, 0.0)
    outs[0][...] = (r * r).astype(BF16)
    outs[1][...] = r.astype(BF16)


def _epi_mul2r(acc, ex, outs):
    outs[0][...] = (acc * (2.0 * ex[0][...].astype(F32))).astype(BF16)


def _epi_resid_gate(nctx, tm):
    def epi(acc, ex, outs):
        rows = pl.program_id(0) * tm + lax.broadcasted_iota(jnp.int32, (tm, 1), 0)
        g = jnp.where(rows < nctx, ex[1][0:1, :], ex[1][1:2, :])
        outs[0][...] = ex[0][...] + g * acc
        outs[1][...] = acc
    return epi


def mm_nn(name, a, w, epi, outs, extras=(), extra_kinds=(), tm_cap=1100, tn_cap=512, tk_cap=2048, carry=None):
    M, K = a.shape
    if w.ndim == 3:
        ns = w.shape[2]
        N = NDEV * ns
        tn = _tile(ns, tn_cap)
        nper = ns // tn
    else:
        N = w.shape[1]
        tn = _tile(N, tn_cap)
    tm = _tile(M, tm_cap)
    tk = _tile(K, tk_cap)
    grid = (M // tm, N // tn, K // tk)
    a_spec = pl.BlockSpec((tm, tk), lambda i, j, k: (i, k))
    if w.ndim == 3:
        b_spec = pl.BlockSpec((None, tk, tn), lambda i, j, k: (j // nper, k, j % nper))
    else:
        b_spec = pl.BlockSpec((tk, tn), lambda i, j, k: (k, j))
    ex_specs = []
    for e, kind in zip(extras, extra_kinds):
        if kind == 'mn':
            ex_specs.append(pl.BlockSpec((tm, tn), lambda i, j, k: (i, j)))
        else:
            ex_specs.append(pl.BlockSpec((e.shape[0], tn), lambda i, j, k: (0, j)))
    out_shape = [jax.ShapeDtypeStruct((M, N), dt) for dt in outs]
    out_specs = [pl.BlockSpec((tm, tn), lambda i, j, k: (i, j)) for _ in outs]
    res = _mm_core(name, grid, (a, w, *extras), [a_spec, b_spec, *ex_specs], out_shape, out_specs, NN, (tm, tn), epi,
                   carry)
    return (list(res[:len(outs)]), res[len(outs)]) if carry else res


def mm_nt(name, a, w, epi, out_dtype, extras=(), tm_cap=1100, to_cap=1024, tc_cap=2048, carry=None):
    M, N = a.shape
    tm = _tile(M, tm_cap)
    if w.ndim == 3:
        Kw, ns = w.shape[1], w.shape[2]
        tc = _tile(ns, tc_cap)
        cper = ns // tc
    else:
        Kw = w.shape[0]
        tc = _tile(N, tc_cap)
    to = _tile(Kw, to_cap)
    grid = (M // tm, Kw // to, N // tc)
    a_spec = pl.BlockSpec((tm, tc), lambda i, j, k: (i, k))
    if w.ndim == 3:
        b_spec = pl.BlockSpec((None, to, tc), lambda i, j, k: (k // cper, j, k % cper))
    else:
        b_spec = pl.BlockSpec((to, tc), lambda i, j, k: (j, k))
    ex_specs = [pl.BlockSpec((tm, to), lambda i, j, k: (i, j)) for _ in extras]
    out_shape = [jax.ShapeDtypeStruct((M, Kw), out_dtype)]
    out_specs = [pl.BlockSpec((tm, to), lambda i, j, k: (i, j))]
    return _split(_mm_core(name, grid, (a, w, *extras), [a_spec, b_spec, *ex_specs], out_shape, out_specs, NT, (tm, to),
                           epi, carry), 1, carry)


def mm_tn(name, a, b, shard_axis, to_cap=1024, tn_cap=1024, tc_cap=2200, carry=None):
    M, Ka = a.shape
    N = b.shape[1]
    tc = _tile(M, tc_cap)
    if shard_axis is None:
        to, tn = _tile(Ka, to_cap), _tile(N, tn_cap)
        shape = (Ka, N)
        oblk = (to, tn)
        omap = lambda i, j, k: (i, j)
    elif shard_axis == 1:
        ns = N // NDEV
        to, tn = _tile(Ka, to_cap), _tile(ns, tn_cap)
        per = ns // tn
        shape = (NDEV, Ka, ns)
        oblk = (None, to, tn)
        omap = lambda i, j, k: (_slot(j // per), i, j % per)
    else:
        rs = Ka // NDEV
        to, tn = _tile(rs, to_cap), _tile(N, tn_cap)
        per = rs // to
        shape = (NDEV, rs, N)
        oblk = (None, to, tn)
        omap = lambda i, j, k: (_slot(i // per), i % per, j)
    grid = (Ka // to, N // tn, M // tc)
    a_spec = pl.BlockSpec((tc, to), lambda i, j, k: (k, i))
    b_spec = pl.BlockSpec((tc, tn), lambda i, j, k: (k, j))
    out_shape = [jax.ShapeDtypeStruct(shape, F32)]
    out_specs = [pl.BlockSpec(oblk, omap)]
    return _split(_mm_core(name, grid, (a, b), [a_spec, b_spec], out_shape, out_specs, TN, (to, tn), _epi_store(F32),
                           carry), 1, carry)


def _row_spec(D):
    return pl.BlockSpec((ROW_TILE, D), lambda i: (i, 0))


def _const_spec(r, D):
    return pl.BlockSpec((r, D), lambda i: (0, 0))


def _grp(ref, is_ctx):
    return jnp.where(is_ctx, ref[0:1, :], ref[1:2, :])


def norm_mod(name, x, nw, sh, sc, nctx):
    R, D = x.shape
    assert R % ROW_TILE == 0 and nctx % ROW_TILE == 0

    def body(x_ref, nw_ref, sh_ref, sc_ref, o_ref):
        is_ctx = pl.program_id(0) * ROW_TILE < nctx
        xv = x_ref[...]
        rstd = lax.rsqrt(jnp.mean(xv * xv, axis=-1, keepdims=True) + EPS)
        n = xv * rstd * nw_ref[...]
        o_ref[...] = (n * (1.0 + _grp(sc_ref, is_ctx)) + _grp(sh_ref, is_ctx)).astype(BF16)

    return pl.pallas_call(
        body, grid=(R // ROW_TILE,),
        in_specs=[_row_spec(D), _const_spec(1, D), _const_spec(2, D), _const_spec(2, D)],
        out_specs=_row_spec(D), out_shape=jax.ShapeDtypeStruct((R, D), BF16),
        compiler_params=_cparams(("parallel",)), name=name)(x, nw, sh, sc)


def _gate_rows(dxv, y_ref, g_ref, is_ctx, dy_ref, gpart_ref):
    dy_ref[...] = (dxv * _grp(g_ref, is_ctx)).astype(BF16)
    s = jnp.sum(dxv * y_ref[...], axis=0, keepdims=True)
    zero = jnp.zeros_like(s)
    gpart_ref[0:1, :] += jnp.where(is_ctx, s, zero)
    gpart_ref[1:2, :] += jnp.where(is_ctx, zero, s)


def norm_bwd(name, x, dh, dres, nw, sc, nctx, dres_skip=0, out_skip=0, carry=None, gate=None):
    R, D = x.shape
    assert R % ROW_TILE == 0 and nctx % ROW_TILE == 0 and dres_skip % ROW_TILE == 0 and out_skip % ROW_TILE == 0
    res_tiles, out_tiles = dres_skip // ROW_TILE, out_skip // ROW_TILE

    def body(x_ref, dh_ref, dres_ref, nw_ref, sc_ref, *rest):
        if gate is None:
            dx_ref, part_ref = rest
        else:
            y_ref, g_ref, dx_ref, part_ref, dy_ref, gpart_ref = rest
        i = pl.program_id(0)
        is_ctx = i * ROW_TILE < nctx

        @pl.when(i == 0)
        def _():
            part_ref[...] = jnp.zeros_like(part_ref)
            if gate is not None:
                gpart_ref[...] = jnp.zeros_like(gpart_ref)

        xv = x_ref[...]
        dhv = dh_ref[...]
        w = nw_ref[...]
        rstd = lax.rsqrt(jnp.mean(xv * xv, axis=-1, keepdims=True) + EPS)
        xhat = xv * rstd
        n = xhat * w
        dn = dhv * (1.0 + _grp(sc_ref, is_ctx))
        dxhat = dn * w
        dres = dres_ref[...]
        if res_tiles:
            dres = jnp.where(i < res_tiles, 0.0, dres)
        dxv = dres + rstd * (dxhat - xhat * jnp.mean(dxhat * xhat, axis=-1, keepdims=True))
        dx_ref[...] = dxv
        s_sh = jnp.sum(dhv, axis=0, keepdims=True)
        s_sc = jnp.sum(dhv * n, axis=0, keepdims=True)
        s_nw = jnp.sum(dn * xhat, axis=0, keepdims=True)
        zero = jnp.zeros_like(s_sh)
        part_ref[0:1, :] += jnp.where(is_ctx, s_sh, zero)
        part_ref[1:2, :] += jnp.where(is_ctx, zero, s_sh)
        part_ref[2:3, :] += jnp.where(is_ctx, s_sc, zero)
        part_ref[3:4, :] += jnp.where(is_ctx, zero, s_sc)
        part_ref[4:5, :] += s_nw
        if gate is not None:
            _gate_rows(dxv, y_ref, g_ref, is_ctx, dy_ref, gpart_ref)

    ins = [x, dh, dres, nw, sc]
    in_specs = [_row_spec(D), _row_spec(D), pl.BlockSpec((ROW_TILE, D), lambda i: (jnp.maximum(i - res_tiles, 0), 0)),
                _const_spec(1, D), _const_spec(2, D)]
    out_shape = [jax.ShapeDtypeStruct((R - out_skip, D), F32), jax.ShapeDtypeStruct((8, D), F32)]
    out_specs = [pl.BlockSpec((ROW_TILE, D), lambda i: (jnp.maximum(i - out_tiles, 0), 0)), _const_spec(8, D)]
    if gate is not None:
        assert out_skip == 0
        ins += list(gate)
        in_specs += [_row_spec(D), _const_spec(2, D)]
        out_shape += [jax.ShapeDtypeStruct((R, D), BF16), jax.ShapeDtypeStruct((8, D), F32)]
        out_specs += [_row_spec(D), _const_spec(8, D)]
    res = _call(name, body, (R // ROW_TILE,), ins, in_specs, out_shape, out_specs, [], ("arbitrary",), carry)
    return _split(res, len(out_shape), carry)


def final_loss(name, x, fw, tgt, y, g):
    S, D = x.shape

    def body(x_ref, fw_ref, t_ref, y_ref, g_ref, dx_ref, loss_ref, dfw_ref, dy_ref, gpart_ref):
        i = pl.program_id(0)

        @pl.when(i == 0)
        def _():
            loss_ref[...] = jnp.zeros_like(loss_ref)
            dfw_ref[...] = jnp.zeros_like(dfw_ref)
            gpart_ref[...] = jnp.zeros_like(gpart_ref)

        xv = x_ref[...]
        w = fw_ref[...]
        rstd = lax.rsqrt(jnp.mean(xv * xv, axis=-1, keepdims=True) + EPS)
        xhat = xv * rstd
        e = xhat * w - t_ref[...]
        loss_ref[...] += 0.5 * jnp.sum(jnp.mean(e * e, axis=-1, keepdims=True))
        dout = e * (1.0 / D)
        dfw_ref[0:1, :] += jnp.sum(dout * xhat, axis=0, keepdims=True)
        dxhat = dout * w
        dxv = rstd * (dxhat - xhat * jnp.mean(dxhat * xhat, axis=-1, keepdims=True))
        dx_ref[...] = dxv
        _gate_rows(dxv, y_ref, g_ref, False, dy_ref, gpart_ref)

    return pl.pallas_call(
        body, grid=(S // ROW_TILE,),
        in_specs=[_row_spec(D), _const_spec(1, D), _row_spec(D), _row_spec(D), _const_spec(2, D)],
        out_specs=[_row_spec(D), pl.BlockSpec((8, 128), lambda i: (0, 0)), _const_spec(8, D), _row_spec(D),
                   _const_spec(8, D)],
        out_shape=[jax.ShapeDtypeStruct((S, D), F32), jax.ShapeDtypeStruct((8, 128), F32),
                   jax.ShapeDtypeStruct((8, D), F32), jax.ShapeDtypeStruct((S, D), BF16),
                   jax.ShapeDtypeStruct((8, D), F32)],
        compiler_params=_cparams(("arbitrary",)), name=name)(x, fw, tgt, y, g)


def _rope(x, cos, sa, sb):
    return x * cos + pltpu.roll(x, 96, 1) * sa + pltpu.roll(x, 32, 1) * sb


def _rope_t(dy, cos, sa, sb):
    return dy * cos + pltpu.roll(dy * sa, 32, 1) + pltpu.roll(dy * sb, 96, 1)


_EVEN_KINDS = ['qa'] * 8 + ['ka'] * 2 + ['v'] * 2 + ['qb'] * 8 + ['kb'] * 2 + ['v'] * 2
_EVEN_DSRC = ([('q', j) for j in range(8)] + [('k', 0), ('k', 1), ('v', 0), ('v', 1)]
              + [('q', 8 + j) for j in range(8)] + [('k', 2), ('k', 3), ('v', 2), ('v', 3)])


def _cols(j):
    return slice(j * HEAD, (j + 1) * HEAD)


def prep_even(name, qkv, qn, kn, cos, sa, sb):
    T, W = qkv.shape

    def body(x_ref, qn_ref, kn_ref, cos_ref, sa_ref, sb_ref, o_ref):
        cos_, sa_, sb_ = cos_ref[...], sa_ref[...], sb_ref[...]
        for j, kind in enumerate(_EVEN_KINDS):
            x = x_ref[:, _cols(j)]
            if kind in ('qa', 'ka'):
                rstd = lax.rsqrt(jnp.mean(x * x, axis=-1, keepdims=True) + EPS)
                x = x * rstd * (qn_ref[...] if kind == 'qa' else kn_ref[...])
            if kind != 'v':
                x = _rope(x, cos_, sa_, sb_)
            o_ref[:, _cols(j)] = x.astype(BF16)

    blk = pl.BlockSpec((ROW_TILE, W), lambda i: (i, 0))
    tab = pl.BlockSpec((ROW_TILE, HEAD), lambda i: (i, 0))
    one = pl.BlockSpec((1, HEAD), lambda i: (0, 0))
    return pl.pallas_call(
        body, grid=(T // ROW_TILE,), in_specs=[blk, one, one, tab, tab, tab], out_specs=blk,
        out_shape=jax.ShapeDtypeStruct(qkv.shape, BF16),
        compiler_params=_cparams(("parallel",)), name=name)(qkv, qn, kn, cos, sa, sb)


def prep_even_bwd(name, qkv, dq, dk, dv, qn, kn, cos, sa, sb):
    T, W = qkv.shape

    def body(x_ref, dq_ref, dk_ref, dv_ref, qn_ref, kn_ref, cos_ref, sa_ref, sb_ref, o_ref, part_ref):
        @pl.when(pl.program_id(0) == 0)
        def _():
            part_ref[...] = jnp.zeros_like(part_ref)

        cos_, sa_, sb_ = cos_ref[...], sa_ref[...], sb_ref[...]
        src = {'q': dq_ref, 'k': dk_ref, 'v': dv_ref}
        sums = {'qa': None, 'ka': None}
        for j, kind in enumerate(_EVEN_KINDS):
            which, blk_j = _EVEN_DSRC[j]
            d = src[which][:, _cols(blk_j)]
            if kind != 'v':
                d = _rope_t(d, cos_, sa_, sb_)
            if kind in ('qa', 'ka'):
                x = x_ref[:, _cols(j)]
                rstd = lax.rsqrt(jnp.mean(x * x, axis=-1, keepdims=True) + EPS)
                xhat = x * rstd
                s = jnp.sum(d * xhat, axis=0, keepdims=True)
                sums[kind] = s if sums[kind] is None else sums[kind] + s
                dxhat = d * (qn_ref[...] if kind == 'qa' else kn_ref[...])
                d = rstd * (dxhat - xhat * jnp.mean(dxhat * xhat, axis=-1, keepdims=True))
            o_ref[:, _cols(j)] = d.astype(BF16)
        part_ref[0:1, :] += sums['qa']
        part_ref[1:2, :] += sums['ka']

    def rows(w):
        return pl.BlockSpec((ROW_TILE, w), lambda i: (i, 0))

    one = pl.BlockSpec((1, HEAD), lambda i: (0, 0))
    return pl.pallas_call(
        body, grid=(T // ROW_TILE,),
        in_specs=[rows(W), rows(dq.shape[1]), rows(dk.shape[1]), rows(dv.shape[1]), one, one,
                  rows(HEAD), rows(HEAD), rows(HEAD)],
        out_specs=[rows(W), pl.BlockSpec((8, HEAD), lambda i: (0, 0))],
        out_shape=[jax.ShapeDtypeStruct(qkv.shape, BF16), jax.ShapeDtypeStruct((8, HEAD), F32)],
        compiler_params=_cparams(("arbitrary",)), name=name)(qkv, dq, dk, dv, qn, kn, cos, sa, sb)


def _even_maps():
    qmap = lambda h, qb: (qb, jnp.where(h < 8, h, h + 4))
    kmap = lambda h, qb: (0, jnp.where(h < 8, 8 + h // 4, 18 + h // 4))
    vmap = lambda h, qb: (0, jnp.where(h < 8, 10 + h // 4, 20 + h // 4))
    return qmap, kmap, vmap


def _softmax_parts(parts, extra=None):
    m = parts[0].max(axis=-1, keepdims=True)
    for p in parts[1:]:
        m = jnp.maximum(m, p.max(axis=-1, keepdims=True))
    if extra is not None:
        m = jnp.maximum(m, extra)
    es = [jnp.exp(p - m) for p in parts]
    l = es[0].sum(axis=-1, keepdims=True)
    for e in es[1:]:
        l = l + e.sum(axis=-1, keepdims=True)
    ex = None
    if extra is not None:
        ex = jnp.exp(extra - m)
        l = l + ex
    inv = 1.0 / l
    return [e * inv for e in es], (None if ex is None else ex * inv)


def _win_scores(q, k_ref, qb, tq, nctx, S):
    L = tq + 2 * WINDOW
    nqc = nctx // tq
    qlat = (qb - nqc) * tq
    start = pl.multiple_of(jnp.clip(qlat - WINDOW, 0, S - L), 128)
    kc = k_ref[0:nctx, :]
    kw = k_ref[pl.ds(nctx + start, L), :]
    s_c = _dot(q, kc, NT) * SCALE
    s_w = _dot(q, kw, NT) * SCALE
    qpos = qlat + lax.broadcasted_iota(jnp.int32, (tq, 1), 0)
    kpos = start + lax.broadcasted_iota(jnp.int32, (1, L), 1)
    valid = jnp.logical_and(jnp.abs(kpos - qpos) <= WINDOW, qb >= nqc)
    return s_c, jnp.where(valid, s_w, NEG), start, L


def _softmax_raw(raw):
    m = raw.max(axis=-1, keepdims=True)
    e = jnp.exp2((raw - m) * (SCALE * np.log2(np.e)))
    return e * (1.0 / e.sum(axis=-1, keepdims=True))


def _glob_keys(qb, tq, nctx, T):
    is_ctx = qb < nctx // tq
    return [(is_ctx, slice(0, nctx)), (jnp.logical_not(is_ctx), slice(0, T))]


def attn_even_fwd(name, qkvh, sink, nctx, tq, carry=None):
    T = qkvh.shape[0]
    S = T - nctx
    qmap, kmap, vmap = _even_maps()

    def body(sink_ref, q_ref, k_ref, v_ref, o_ref):
        h, qb = pl.program_id(0), pl.program_id(1)
        q = q_ref[...]

        for pred, keys in _glob_keys(qb, tq, nctx, T):
            @pl.when(jnp.logical_and(h < 8, pred))
            def _():
                p = _softmax_raw(_dot(q, k_ref[keys, :], NT))
                o_ref[...] = _dot(p, v_ref[keys, :], NN).astype(BF16)

        @pl.when(h >= 8)
        def _():
            s_c, s_w, start, L = _win_scores(q, k_ref, qb, tq, nctx, S)
            sk = jnp.full((tq, 1), sink_ref[jnp.maximum(h - 8, 0)], F32)
            (p_c, p_w), _ = _softmax_parts([s_c, s_w], sk)
            o = _dot(p_c, v_ref[0:nctx, :], NN) + _dot(p_w, v_ref[pl.ds(nctx + start, L), :], NN)
            o_ref[...] = o.astype(BF16)

    res = _call(name, body, (16, T // tq), (sink, qkvh, qkvh, qkvh),
                [pl.BlockSpec(memory_space=pltpu.SMEM), pl.BlockSpec((tq, HEAD), qmap),
                 pl.BlockSpec((T, HEAD), kmap), pl.BlockSpec((T, HEAD), vmap)],
                [jax.ShapeDtypeStruct((T, 16 * HEAD), BF16)], [pl.BlockSpec((tq, HEAD), lambda h, qb: (qb, h))],
                [], ("parallel", "arbitrary"), carry)
    return _split(res, 1, carry)


def attn_even_bwd(name, qkvh, sink, do, nctx, tq, carry=None):
    T = qkvh.shape[0]
    S = T - nctx
    qmap, kmap, vmap = _even_maps()

    def body(sink_ref, q_ref, k_ref, v_ref, do_ref, dq_ref, dk_ref, dv_ref, ds_ref):
        h, qb = pl.program_id(0), pl.program_id(1)
        q = q_ref[...]
        dov = do_ref[...]

        @pl.when(jnp.logical_and(h % 4 == 0, qb == 0))
        def _():
            dk_ref[...] = jnp.zeros_like(dk_ref)
            dv_ref[...] = jnp.zeros_like(dv_ref)

        @pl.when(qb == 0)
        def _():
            ds_ref[...] = jnp.zeros_like(ds_ref)

        for pred, keys in _glob_keys(qb, tq, nctx, T):
            @pl.when(jnp.logical_and(h < 8, pred))
            def _():
                p = _softmax_raw(_dot(q, k_ref[keys, :], NT))
                dp = _dot(dov, v_ref[keys, :], NT)
                row = jnp.sum(p * dp, axis=-1, keepdims=True)
                dsb = (p * (dp - row) * SCALE).astype(BF16)
                dq_ref[...] = _dot(dsb, k_ref[keys, :], NN)
                dk_ref[keys, :] += _dot(dsb, q, TN)
                dv_ref[keys, :] += _dot(p, dov, TN)

        @pl.when(h >= 8)
        def _():
            s_c, s_w, start, L = _win_scores(q, k_ref, qb, tq, nctx, S)
            sk = jnp.full((tq, 1), sink_ref[jnp.maximum(h - 8, 0)], F32)
            (p_c, p_w), p_s = _softmax_parts([s_c, s_w], sk)
            win = pl.ds(nctx + start, L)
            dp_c = _dot(dov, v_ref[0:nctx, :], NT)
            dp_w = _dot(dov, v_ref[win, :], NT)
            row = jnp.sum(p_c * dp_c, axis=-1, keepdims=True) + jnp.sum(p_w * dp_w, axis=-1, keepdims=True)
            ds_c = (p_c * (dp_c - row) * SCALE).astype(BF16)
            ds_w = (p_w * (dp_w - row) * SCALE).astype(BF16)
            dq_ref[...] = _dot(ds_c, k_ref[0:nctx, :], NN) + _dot(ds_w, k_ref[win, :], NN)
            dk_ref[0:nctx, :] += _dot(ds_c, q, TN)
            dk_ref[win, :] += _dot(ds_w, q, TN)
            dv_ref[0:nctx, :] += _dot(p_c, dov, TN)
            dv_ref[win, :] += _dot(p_w, dov, TN)
            ds_ref[...] += jnp.sum(-(p_s * row))

    kv_out = pl.BlockSpec((T, HEAD), lambda h, qb: (0, h // 4))
    res = _call(name, body, (16, T // tq), (sink, qkvh, qkvh, qkvh, do),
                [pl.BlockSpec(memory_space=pltpu.SMEM), pl.BlockSpec((tq, HEAD), qmap),
                 pl.BlockSpec((T, HEAD), kmap), pl.BlockSpec((T, HEAD), vmap),
                 pl.BlockSpec((tq, HEAD), lambda h, qb: (qb, h))],
                [jax.ShapeDtypeStruct((T, 16 * HEAD), F32), jax.ShapeDtypeStruct((T, 4 * HEAD), F32),
                 jax.ShapeDtypeStruct((T, 4 * HEAD), F32), jax.ShapeDtypeStruct((16, 8, 128), F32)],
                [pl.BlockSpec((tq, HEAD), lambda h, qb: (qb, h)), kv_out, kv_out,
                 pl.BlockSpec((None, 8, 128), lambda h, qb: (h, 0, 0))],
                [], ("arbitrary", "arbitrary"), carry)
    return _split(res, 4, carry)


NA_GROUP = 4
NA_SPAN = NA_KH + NA_GROUP - 1
_NA_PLAN = [[(j, 0) for j in range(NA_GROUP)],
            [(NA_KH // 2, j) for j in range(NA_GROUP)],
            [(NA_KH // 2 + j, NA_GROUP - 1) for j in range(NA_GROUP)]]


def _na_group(g, n_groups, rows):
    last = g == n_groups - 1
    kind = jnp.where(g == 0, 0, jnp.where(last, 2, 1))
    first_row = jnp.where(g == 0, 0, jnp.where(last, rows - NA_SPAN, NA_GROUP * g - NA_KH // 2))
    return kind, first_row


def na_span_bias(bias8):
    LW, LS = NA_KH * GRID_W, NA_SPAN * GRID_W
    kinds = []
    for plan in _NA_PLAN:
        strips = [jnp.pad(bias8[:, off], ((0, 0), (0, 0), (s * GRID_W, LS - LW - s * GRID_W)), constant_values=NEG)
                  for off, s in plan]
        kinds.append(jnp.concatenate(strips, axis=1))
    return jnp.stack(kinds, axis=1)


def na_span_bias_grad(db):
    LW = NA_KH * GRID_W
    out = [None] * NA_KH
    for kind, plan in enumerate(_NA_PLAN):
        for j, (off, s) in enumerate(plan):
            piece = db[:, kind, j * GRID_W:(j + 1) * GRID_W, s * GRID_W:s * GRID_W + LW]
            out[off] = piece if out[off] is None else out[off] + piece
    return jnp.stack(out, axis=1)


def _na_specs(T, nctx, n_groups, rows):
    LS = NA_SPAN * GRID_W
    tq = NA_GROUP * GRID_W
    assert nctx % tq == 0 and n_groups >= 3
    q_spec = pl.BlockSpec((tq, HEAD), lambda h, g: (g + nctx // tq, h))
    k_spec = pl.BlockSpec((T, HEAD), lambda h, g: (0, 16 + h))
    v_spec = pl.BlockSpec((T, HEAD), lambda h, g: (0, 32 + h))
    b_spec = pl.BlockSpec((None, None, tq, LS), lambda h, g: (h, _na_group(g, n_groups, rows)[0], 0, 0))
    row_spec = pl.BlockSpec((tq, HEAD), lambda h, g: (g, h))
    return q_spec, k_spec, v_spec, b_spec, row_spec


def _na_scores(q, k_ref, b_ref, g, n_groups, rows, nctx):
    first_row = _na_group(g, n_groups, rows)[1]
    win = pl.ds(pl.multiple_of(nctx + first_row * GRID_W, GRID_W), NA_SPAN * GRID_W)
    s_c = _dot(q, k_ref[0:nctx, :], NT) * SCALE
    s_w = _dot(q, k_ref[win, :], NT) * SCALE + b_ref[...]
    return s_c, s_w, win


def attn_odd_fwd(name, qkv, bias_s, nctx, carry=None):
    T = qkv.shape[0]
    S = T - nctx
    rows = S // GRID_W
    n_groups = rows // NA_GROUP
    q_spec, k_spec, v_spec, b_spec, row_spec = _na_specs(T, nctx, n_groups, rows)

    def body(q_ref, k_ref, v_ref, b_ref, o_ref):
        s_c, s_w, win = _na_scores(q_ref[...], k_ref, b_ref, pl.program_id(1), n_groups, rows, nctx)
        (p_c, p_w), _ = _softmax_parts([s_c, s_w])
        o_ref[...] = (_dot(p_c, v_ref[0:nctx, :], NN) + _dot(p_w, v_ref[win, :], NN)).astype(BF16)

    res = _call(name, body, (16, n_groups), (qkv, qkv, qkv, bias_s), [q_spec, k_spec, v_spec, b_spec],
                [jax.ShapeDtypeStruct((S, 16 * HEAD), BF16)], [row_spec], [], ("parallel", "arbitrary"), carry)
    return _split(res, 1, carry)


def attn_odd_bwd(name, qkv, bias_s, do, nctx, carry=None):
    T = qkv.shape[0]
    S = T - nctx
    rows = S // GRID_W
    n_groups = rows // NA_GROUP
    q_spec, k_spec, v_spec, b_spec, row_spec = _na_specs(T, nctx, n_groups, rows)

    def body(q_ref, k_ref, v_ref, b_ref, do_ref, dq_ref, dk_ref, dv_ref, db_ref):
        g = pl.program_id(1)
        q = q_ref[...]
        dov = do_ref[...]

        @pl.when(g == 0)
        def _():
            dk_ref[...] = jnp.zeros_like(dk_ref)
            dv_ref[...] = jnp.zeros_like(dv_ref)

        s_c, s_w, win = _na_scores(q, k_ref, b_ref, g, n_groups, rows, nctx)
        (p_c, p_w), _ = _softmax_parts([s_c, s_w])
        dp_c = _dot(dov, v_ref[0:nctx, :], NT)
        dp_w = _dot(dov, v_ref[win, :], NT)
        row = jnp.sum(p_c * dp_c, axis=-1, keepdims=True) + jnp.sum(p_w * dp_w, axis=-1, keepdims=True)
        dsw = p_w * (dp_w - row)
        first_visit = jnp.logical_or(g <= 1, g == n_groups - 1)

        @pl.when(first_visit)
        def _():
            db_ref[...] = dsw

        @pl.when(jnp.logical_not(first_visit))
        def _():
            db_ref[...] += dsw

        ds_c = (p_c * (dp_c - row) * SCALE).astype(BF16)
        ds_w = (dsw * SCALE).astype(BF16)
        dq_ref[...] = _dot(ds_c, k_ref[0:nctx, :], NN) + _dot(ds_w, k_ref[win, :], NN)
        dk_ref[0:nctx, :] += _dot(ds_c, q, TN)
        dk_ref[win, :] += _dot(ds_w, q, TN)
        dv_ref[0:nctx, :] += _dot(p_c, dov, TN)
        dv_ref[win, :] += _dot(p_w, dov, TN)

    kv_out = pl.BlockSpec((T, HEAD), lambda h, g: (0, h))
    res = _call(name, body, (16, n_groups), (qkv, qkv, qkv, bias_s, do), [q_spec, k_spec, v_spec, b_spec, row_spec],
                [jax.ShapeDtypeStruct((S, 16 * HEAD), F32), jax.ShapeDtypeStruct((T, 16 * HEAD), F32),
                 jax.ShapeDtypeStruct((T, 16 * HEAD), F32), jax.ShapeDtypeStruct(bias_s.shape, F32)],
                [row_spec, kv_out, kv_out, b_spec], [], ("arbitrary", "arbitrary"), carry)
    return _split(res, 4, carry)


def _na_onehots():
    o = np.arange(NA_KH)[:, None]
    i = np.arange(NA_KH)[None, :]
    a = i - o + NA_KH - 1
    A = (a[..., None] == np.arange(2 * NA_KH - 1)).astype(np.float32)
    qc = np.arange(GRID_W)[:, None]
    kc = np.arange(GRID_W)[None, :]
    b = np.clip(kc - qc + NA_KW - 1, 0, 2 * NA_KW - 2)
    cs = np.clip(qc - NA_KW // 2, 0, GRID_W - NA_KW)
    valid = (kc >= cs) & (kc < cs + NA_KW)
    B = ((b[..., None] == np.arange(2 * NA_KW - 1)) & valid[..., None]).astype(np.float32)
    return A, B, valid


def na_bias_table(rpb):
    A, B, valid = _na_onehots()
    hp = lax.Precision.HIGHEST
    t = jnp.einsum('hab,oia->hoib', rpb, jnp.asarray(A), precision=hp)
    bias = jnp.einsum('hoib,qkb->hoqik', t, jnp.asarray(B), precision=hp)
    bias = jnp.where(jnp.asarray(valid)[None, None, :, None, :], bias, NEG)
    return bias.reshape(rpb.shape[0], NA_KH, GRID_W, NA_KH * GRID_W)


def na_bias_grad(name, dbias8):
    A, B, _ = _na_onehots()
    H = dbias8.shape[0]
    nb, na = 2 * NA_KW - 1, 2 * NA_KH - 1
    d = dbias8.reshape(H, NA_KH, GRID_W, NA_KH, GRID_W).transpose(0, 1, 3, 2, 4)
    d = d.reshape(H * NA_KH * NA_KH, GRID_W * GRID_W)
    Bp = np.zeros((GRID_W * GRID_W, 128), np.float32)
    Bp[:, :nb] = B.reshape(GRID_W * GRID_W, nb)
    Ap = np.zeros((16, NA_KH * NA_KH), np.float32)
    Ap[:na] = A.reshape(NA_KH * NA_KH, na).T
    rows_per_head = NA_KH * NA_KH

    def split3(x):
        hi = x.astype(BF16)
        r1 = x - hi.astype(F32)
        mid = r1.astype(BF16)
        return hi, mid, (r1 - mid.astype(F32)).astype(BF16)

    def body(d_ref, b_ref, a_ref, o_ref):
        bm, am = b_ref[...], a_ref[...]
        g = sum(lax.dot_general(p, bm, NN, preferred_element_type=F32) for p in split3(d_ref[...]))
        o_ref[...] = sum(lax.dot_general(am, p, NN, preferred_element_type=F32) for p in split3(g))

    out = pl.pallas_call(
        body, grid=(H,),
        in_specs=[pl.BlockSpec((rows_per_head, GRID_W * GRID_W), lambda h: (h, 0)),
                  pl.BlockSpec((GRID_W * GRID_W, 128), lambda h: (0, 0)),
                  pl.BlockSpec((16, rows_per_head), lambda h: (0, 0))],
        out_specs=pl.BlockSpec((None, 16, 128), lambda h: (h, 0, 0)),
        out_shape=jax.ShapeDtypeStruct((H, 16, 128), F32),
        compiler_params=_cparams(("parallel",)), name=name)(d, jnp.asarray(Bp, BF16), jnp.asarray(Ap, BF16))
    return out[:, :na, :nb]


def _vmem_call(name, fn, out_shape, *arrays):
    def body(*refs):
        n = len(arrays)
        res = fn(*[r[...] for r in refs[:n]])
        if not isinstance(res, (tuple, list)):
            res = (res,)
        for o, v in zip(refs[n:], res):
            o[...] = v
    return pl.pallas_call(body, out_shape=out_shape, name=name,
                          compiler_params=pltpu.CompilerParams(vmem_limit_bytes=VMEM_LIMIT))(*arrays)


def _silu(v):
    return v / (1.0 + jnp.exp(-v))


def _adamw_math(w, g, m, v):
    m2 = ADAM_B1 * m + (1.0 - ADAM_B1) * g
    v2 = ADAM_B2 * v + (1.0 - ADAM_B2) * (g * g)
    m_hat = m2 / (1.0 - ADAM_B1 ** ADAM_STEP)
    v_hat = v2 / (1.0 - ADAM_B2 ** ADAM_STEP)
    delta = -ADAM_LR * (m_hat / (jnp.sqrt(v_hat) + ADAM_EPS) + ADAM_WD * w)
    return delta, m2, v2


def _ew_tile(R, C):
    return _tile(R, max(64, (262144 // C) // 64 * 64))


def adamw_rows(name, w, g, m, v, extra_g=None):
    R, C = w.shape
    tr = _ew_tile(R, C)
    extra_g = list(extra_g or [])
    ne = len(extra_g)

    def body(*refs):
        w_ref, g_ref, m_ref, v_ref = refs[:4]
        gs = g_ref[...]
        for e in refs[4:4 + ne]:
            gs = gs + e[...].astype(F32)
        go, do, mo, vo = refs[4 + ne:]
        d, m2, v2 = _adamw_math(w_ref[...], gs, m_ref[...], v_ref[...])
        go[...] = gs
        do[...] = d
        mo[...] = m2
        vo[...] = v2

    spec = pl.BlockSpec((tr, C), lambda i: (i, 0))
    return pl.pallas_call(
        body, grid=(R // tr,), in_specs=[spec] * (4 + ne), out_specs=[spec] * 4,
        out_shape=[jax.ShapeDtypeStruct((R, C), F32)] * 4,
        compiler_params=_cparams(("parallel",)), name=name)(w, g, m, v, *extra_g)


def rs_chip_sum(name, g8, sib4, where):
    _, R, C = g8.shape
    tr = _ew_tile(R, C)

    def body(s_ref, g_ref, b_ref, o_ref):
        o_ref[...] = (g_ref[...] + b_ref[...]).astype(BF16)

    def chip(q, s):
        return (s[1] + 1 + q) % 4

    blk = (None, tr, C)
    grid_spec = pltpu.PrefetchScalarGridSpec(
        num_scalar_prefetch=1, grid=(3, R // tr),
        in_specs=[pl.BlockSpec(blk, lambda q, i, s: (s[0] + chip(q, s), i, 0)),
                  pl.BlockSpec(blk, lambda q, i, s: (chip(q, s), i, 0))],
        out_specs=pl.BlockSpec(blk, lambda q, i, s: (chip(q, s), i, 0)))
    return pl.pallas_call(body, grid_spec=grid_spec, out_shape=jax.ShapeDtypeStruct((4, R, C), BF16),
                          compiler_params=_cparams(("parallel", "parallel")), name=name)(where, g8, sib4)


def adamw_rs(name, w, g8, sib4, rem3, m, v, idx, layer, prev=None):
    L, R, C = w.shape
    tr = _ew_tile(R, C)

    def body(s_ref, w_ref, g_ref, sb_ref, r0_ref, r1_ref, r2_ref, m_ref, v_ref, *rest):
        go, do, mo, vo = rest[-4:]
        gs = g_ref[...] + sb_ref[...]
        for r_ref in (r0_ref, r1_ref, r2_ref):
            gs = gs + r_ref[...].astype(F32)
        d, m2, v2 = _adamw_math(w_ref[...], gs, m_ref[...], v_ref[...])
        go[...] = gs
        do[...] = d
        mo[...] = m2
        vo[...] = v2

    blk = (None, tr, C)
    mine = pl.BlockSpec(blk, lambda i, s: (layer, i, 0))

    def rem(k):
        return pl.BlockSpec(blk, lambda i, s: (k, i, 0))

    prev = list(prev or [])
    grid_spec = pltpu.PrefetchScalarGridSpec(
        num_scalar_prefetch=1, grid=(R // tr,),
        in_specs=[mine, pl.BlockSpec(blk, lambda i, s: (s[0], i, 0)), pl.BlockSpec(blk, lambda i, s: (s[1], i, 0)),
                  rem(0), rem(1), rem(2), mine, mine] + [pl.BlockSpec(memory_space=pl.ANY)] * len(prev),
        out_specs=[mine] * 4)
    return pl.pallas_call(body, grid_spec=grid_spec, out_shape=[jax.ShapeDtypeStruct((L, R, C), F32)] * 4,
                          input_output_aliases={9 + k: k for k in range(len(prev))},
                          compiler_params=_cparams(("parallel",)), name=name)(
                              idx, w, g8, sib4, rem3, rem3, rem3, m, v, *prev)


def _me():
    x, y, c = lax.axis_index("x"), lax.axis_index("y"), lax.axis_index("c")
    return x, y, c


def _flip(v, bit):
    return 1 - v if bit else v


def ag_small(name, x, with_sum=False):
    R, C = x.shape

    def body(x_ref, out_ref, *rest):
        if with_sum:
            sum_ref, send_sems, recv_sems, lsem = rest
        else:
            send_sems, recv_sems, lsem = rest
        mx, my, mc = _me()
        me = 4 * mx + 2 * my + mc
        local = pltpu.make_async_copy(x_ref, out_ref.at[me], lsem)
        local.start()
        sends = []
        for k in range(1, NDEV):
            peer = (_flip(mx, k & 4), _flip(my, k & 2), _flip(mc, k & 1))
            cp = pltpu.make_async_remote_copy(src_ref=x_ref, dst_ref=out_ref.at[me], send_sem=send_sems.at[k - 1],
                                              recv_sem=recv_sems.at[k - 1], device_id=peer, device_id_type=MESH)
            cp.start()
            sends.append(cp)
        for k in range(1, NDEV):
            px, py, pc = _flip(mx, k & 4), _flip(my, k & 2), _flip(mc, k & 1)
            pltpu.make_async_remote_copy(src_ref=x_ref, dst_ref=out_ref.at[4 * px + 2 * py + pc],
                                         send_sem=send_sems.at[k - 1], recv_sem=recv_sems.at[k - 1],
                                         device_id=(px, py, pc), device_id_type=MESH).wait_recv()
        for cp in sends:
            cp.wait_send()
        local.wait()
        if with_sum:
            acc = out_ref[0]
            for d in range(1, NDEV):
                acc = acc + out_ref[d]
            sum_ref[...] = acc

    out_shape = [jax.ShapeDtypeStruct((NDEV, R, C), F32)]
    if with_sum:
        out_shape.append(jax.ShapeDtypeStruct((R, C), F32))
    vm = pl.BlockSpec(memory_space=pltpu.VMEM)
    res = pl.pallas_call(
        body, out_shape=out_shape, in_specs=[vm], out_specs=[vm] * len(out_shape),
        scratch_shapes=[pltpu.SemaphoreType.DMA((NDEV - 1,)), pltpu.SemaphoreType.DMA((NDEV - 1,)),
                        pltpu.SemaphoreType.DMA],
        compiler_params=pltpu.CompilerParams(vmem_limit_bytes=VMEM_LIMIT), name=name)(x)
    return res if with_sum else res[0]


def ag_big(name, shards):
    n = len(shards)

    def body(*refs):
        ins, outs = refs[:n], refs[n:2 * n]
        send_sems, recv_sems, lsems = refs[2 * n:]
        mx, my, mc = _me()
        me = (mx, my, mc)
        sibling = (mx, my, 1 - mc)
        chips = [(1 - mx, my), (mx, 1 - my), (1 - mx, 1 - my)]

        def idx(p):
            return 4 * p[0] + 2 * p[1] + p[2]

        def copy(t, k, block, to, src=None):
            dst = outs[t].at[idx(block)]
            return pltpu.make_async_remote_copy(
                src_ref=dst if src is None else src, dst_ref=dst, send_sem=send_sems.at[7 * t + k],
                recv_sem=recv_sems.at[7 * t + k], device_id=to, device_id_type=MESH)

        started = []
        locals_ = []
        for t in range(n):
            mine = pltpu.make_async_copy(ins[t], outs[t].at[idx(me)], lsems.at[t])
            mine.start()
            locals_.append(mine)
            first = [copy(t, 0, me, sibling, src=ins[t])]
            first += [copy(t, 1 + j, me, (*chip, mc), src=ins[t]) for j, chip in enumerate(chips)]
            for cp in first:
                cp.start()
            started += first
        for t in range(n):
            for j, chip in enumerate(chips):
                copy(t, 1 + j, (*chip, mc), me).wait_recv()
                fwd = copy(t, 4 + j, (*chip, mc), sibling)
                fwd.start()
                started.append(fwd)
        for t in range(n):
            copy(t, 0, sibling, me).wait_recv()
            for j, chip in enumerate(chips):
                copy(t, 4 + j, (*chip, 1 - mc), me).wait_recv()
        for cp in started:
            cp.wait_send()
        for mine in locals_:
            mine.wait()

    anyspec = pl.BlockSpec(memory_space=pl.ANY)
    return pl.pallas_call(
        body, out_shape=[jax.ShapeDtypeStruct((NDEV,) + s.shape, s.dtype) for s in shards],
        in_specs=[anyspec] * n, out_specs=[anyspec] * n,
        scratch_shapes=[pltpu.SemaphoreType.DMA((7 * n,)), pltpu.SemaphoreType.DMA((7 * n,)),
                        pltpu.SemaphoreType.DMA((n,))],
        name=name)(*shards)


def _idx(p):
    return 4 * p[0] + 2 * p[1] + p[2]


def _remote(src, dst, ss, rs, k, to):
    return pltpu.make_async_remote_copy(src_ref=src, dst_ref=dst, send_sem=ss.at[k], recv_sem=rs.at[k],
                                        device_id=to, device_id_type=MESH)


def ex_ag_chips(shards):
    n = len(shards)

    def copies(ci, co, ss, rs, base):
        mx, my, mc = _me()
        me = (mx, my, mc)
        peers = [(mx, my, 1 - mc), (1 - mx, my, mc), (mx, 1 - my, mc), (1 - mx, 1 - my, mc)]
        sends, recvs, local = [], [], []
        for t in range(n):
            b = base + 5 * t
            for k, peer in enumerate(peers):
                sends.append(_remote(ci[t], co[t].at[_idx(me)], ss, rs, b + k, peer))
                recvs.append(_remote(ci[t], co[t].at[_idx(peer)], ss, rs, b + k, peer))
            local.append(pltpu.make_async_copy(ci[t], co[t].at[_idx(me)], ss.at[b + 4]))
        return sends, recvs, local

    def start(ci, co, ss, rs, base):
        sends, _, local = copies(ci, co, ss, rs, base)
        for cp in local + sends:
            cp.start()

    def finish(ci, co, ss, rs, base):
        sends, recvs, local = copies(ci, co, ss, rs, base)
        for cp in recvs:
            cp.wait_recv()
        for cp in sends:
            cp.wait_send()
        for cp in local:
            cp.wait()

    outs = [jax.ShapeDtypeStruct((NDEV,) + s.shape, s.dtype) for s in shards]
    return Exchange(shards, outs, {}, 5 * n, start, finish)


def ex_ag_sibling(bufs):
    n = len(bufs)

    def copies(co, ss, rs, base):
        mx, my, mc = _me()
        sibling = (mx, my, 1 - mc)
        chips = [(1 - mx, my), (mx, 1 - my), (1 - mx, 1 - my)]
        sends, recvs = [], []
        for t in range(n):
            for j, chip in enumerate(chips):
                mine, theirs = co[t].at[_idx((*chip, mc))], co[t].at[_idx((*chip, 1 - mc))]
                sends.append(_remote(mine, mine, ss, rs, base + 3 * t + j, sibling))
                recvs.append(_remote(mine, theirs, ss, rs, base + 3 * t + j, sibling))
        return sends, recvs

    def start(ci, co, ss, rs, base):
        for cp in copies(co, ss, rs, base)[0]:
            cp.start()

    def finish(ci, co, ss, rs, base):
        sends, recvs = copies(co, ss, rs, base)
        for cp in recvs:
            cp.wait_recv()
        for cp in sends:
            cp.wait_send()

    outs = [jax.ShapeDtypeStruct(b.shape, b.dtype) for b in bufs]
    return Exchange(bufs, outs, {t: t for t in range(n)}, 3 * n, start, finish)


def ex_rs_sibling(grads):
    n = len(grads)

    def copies(ci, co, ss, rs, base):
        mx, my, mc = _me()
        return [_remote(ci[t].at[pl.ds((1 - mc) * 4, 4)], co[t], ss, rs, base + t, (mx, my, 1 - mc)) for t in range(n)]

    def start(ci, co, ss, rs, base):
        for cp in copies(ci, co, ss, rs, base):
            cp.start()

    def finish(ci, co, ss, rs, base):
        for cp in copies(ci, co, ss, rs, base):
            cp.wait()

    outs = [jax.ShapeDtypeStruct((4,) + g.shape[1:], g.dtype) for g in grads]
    return Exchange(grads, outs, {}, n, start, finish)


def ex_rs_chips(parts):
    n = len(parts)

    def copies(ci, co, ss, rs, base):
        mx, my, mc = _me()
        cps = []
        for t in range(n):
            for k in range(1, 4):
                px, py = _flip(mx, k & 2), _flip(my, k & 1)
                cps.append(_remote(ci[t].at[2 * px + py], co[t].at[k - 1], ss, rs, base + 3 * t + k - 1, (px, py, mc)))
        return cps

    def start(ci, co, ss, rs, base):
        for cp in copies(ci, co, ss, rs, base):
            cp.start()

    def finish(ci, co, ss, rs, base):
        for cp in copies(ci, co, ss, rs, base):
            cp.wait()

    outs = [jax.ShapeDtypeStruct((3,) + p.shape[1:], p.dtype) for p in parts]
    return Exchange(parts, outs, {}, 3 * n, start, finish)


def run_exchanges(name, xs):
    x = merge_exchanges(xs)
    n_ci, n_co = len(x.ins), len(x.out_shapes)

    def body(*refs):
        ci, co = refs[:n_ci], refs[n_ci:n_ci + n_co]
        ss, rs = refs[n_ci + n_co:]
        x.start(ci, co, ss, rs, 0)
        x.finish(ci, co, ss, rs, 0)

    hbm = pl.BlockSpec(memory_space=pl.ANY)
    return pl.pallas_call(
        body, out_shape=x.out_shapes, in_specs=[hbm] * n_ci, out_specs=[hbm] * n_co, input_output_aliases=x.aliases,
        scratch_shapes=[pltpu.SemaphoreType.DMA((x.n_sems,)), pltpu.SemaphoreType.DMA((x.n_sems,))], name=name)(*x.ins)


def _rope_tables(S, nctx):
    t = jnp.arange(S)
    row = (t // GRID_W).astype(F32)
    col = (t % GRID_W).astype(F32)
    pairs = HEAD // 4
    inv = ROPE_THETA ** (-jnp.arange(pairs, dtype=F32) / pairs)
    ang_r = row[:, None] * inv
    ang_c = col[:, None] * inv
    ang = jnp.concatenate([ang_r, ang_r, ang_c, ang_c], axis=-1)
    cos = jnp.concatenate([jnp.ones((nctx, HEAD), F32), jnp.cos(ang)], axis=0)
    sin = jnp.concatenate([jnp.zeros((nctx, HEAD), F32), jnp.sin(ang)], axis=0)
    lane = jnp.arange(HEAD)[None, :]
    first = (lane & 32) == 0
    return cos, jnp.where(first, -sin, 0.0), jnp.where(first, 0.0, sin)


def _pad_rows(v, rows):
    v = v.reshape(-1).astype(F32)
    return jnp.pad(v, (0, rows * 128 - v.shape[0])).reshape(rows, 128)


def _rows8(n):
    return -(-n // 1024) * 8


def kernel(x, c, ctx, c_ctx, ada_w, ada_b, norm_w, mlp_w1, mlp_w2, ev_w_in, ev_w_out, ev_q_norm, ev_k_norm, ev_sink, od_w_in, od_w_out, od_rpb, final_norm_w, loss_target, m_c_ctx, m_ada_w, m_ada_b, m_norm_w, m_mlp_w1, m_mlp_w2, m_ev_w_in, m_ev_w_out, m_ev_q_norm, m_ev_k_norm, m_ev_sink, m_od_w_in, m_od_w_out, m_od_rpb, m_final_norm_w, v_c_ctx, v_ada_w, v_ada_b, v_norm_w, v_mlp_w1, v_mlp_w2, v_ev_w_in, v_ev_w_out, v_ev_q_norm, v_ev_k_norm, v_ev_sink, v_od_w_in, v_od_w_out, v_od_rpb, v_final_norm_w):
    S, D = x.shape[1], x.shape[2]
    NC = ctx.shape[1]
    T = NC + S
    assert NC == ROW_TILE and S % GRID_W == 0
    ada_cols = ada_w.shape[2]
    nw_cols = norm_w.shape[2]
    me = 4 * lax.axis_index("x") + 2 * lax.axis_index("y") + lax.axis_index("c")

    pack1 = jnp.concatenate([_pad_rows(c, _rows8(D)), _pad_rows(norm_w, _rows8(4 * nw_cols))], axis=0)
    g1 = ag_small("ag_c_normw", pack1)
    c_all = g1[:, :D // 128].reshape(NDEV, D)
    nw_rows = _rows8(D)
    nw = g1[:, nw_rows:nw_rows + 4 * nw_cols // 128].reshape(NDEV, 2, 2, nw_cols)
    nw = nw.transpose(1, 2, 0, 3).reshape(2, 2, D)
    cin = jnp.concatenate([c_all, jnp.broadcast_to(c_ctx[None], (NDEV, D))], axis=0)
    act = _vmem_call("silu_c", lambda v: _silu(v).astype(BF16), jax.ShapeDtypeStruct((2 * NDEV, D), BF16), cin)
    ada_b_loc = lax.dynamic_slice_in_dim(ada_b, me * ada_cols, ada_cols, axis=1)
    mods = [mm_nn(f"mod{i}", act, ada_w[i], _epi_bias, [F32], extras=(ada_b_loc[i:i + 1],), extra_kinds=('n',))[0]
            for i in range(2)]
    gm = ag_small("ag_mod", jnp.concatenate(mods, axis=1))
    gm = gm.reshape(NDEV, 2 * NDEV, 2, ada_cols).transpose(2, 1, 0, 3).reshape(2, 2 * NDEV, NDEV * ada_cols)
    mod_lat = lax.dynamic_index_in_dim(gm, me, axis=1, keepdims=False)
    mod_ctx = gm[:, NDEV]
    mod2 = jnp.stack([mod_ctx, mod_lat], axis=1).reshape(2, 2, 6, D)

    def chunk(i, j):
        return mod2[i, :, j, :]

    def b16(w):
        return w.astype(BF16)

    (w_in_e,) = ag_big("ag_weights_l0_qkv", [b16(ev_w_in[0])])

    cos, sa, sb = _rope_tables(S, NC)
    bias8 = na_span_bias(na_bias_table(od_rpb[0]))
    sink = ev_sink[0]
    TQ_F, TQ_B = 256, 256

    X0 = jnp.concatenate([ctx[0], x[0]], axis=0)
    h_a = norm_mod("l0_norm1", X0, nw[0, 0][None], chunk(0, 0), chunk(0, 1), NC)
    (qkv0,), (w_out_e_half,) = mm_nn("l0_qkv", h_a, w_in_e, _epi_store(F32), [F32],
                                     carry=[ex_ag_chips([b16(ev_w_out[0])])])
    qkvh0 = prep_even("l0_prep", qkv0, ev_q_norm, ev_k_norm, cos, sa, sb)
    o0, (w1_0_half, w2_0_half, w_out_o_half, w_out_e) = attn_even_fwd(
        "l0_attn", qkvh0, sink, NC, TQ_F,
        carry=[ex_ag_chips([b16(mlp_w1[0]), b16(mlp_w2[0]), b16(od_w_out[0])]), ex_ag_sibling([w_out_e_half])])
    w_out_e = w_out_e.reshape(-1, D)
    tm0 = _tile(T, 1100)
    (X1, y0), (w1_0, w2_0) = mm_nn("l0_out", o0, w_out_e, _epi_resid_gate(NC, tm0), [F32, F32],
                                   extras=(X0, chunk(0, 2)), extra_kinds=('mn', 'n'),
                                   carry=[ex_ag_sibling([w1_0_half, w2_0_half])])
    h_b = norm_mod("l0_norm2", X1, nw[0, 1][None], chunk(0, 3), chunk(0, 4), NC)
    (a0, r0), (w_in_o_half, w_out_o) = mm_nn(
        "l0_up", h_b, w1_0, _epi_relu2, [BF16, BF16], tn_cap=1024,
        carry=[ex_ag_chips([b16(od_w_in[0])]), ex_ag_sibling([w_out_o_half])])
    (X2, z0), (w1_1_half, w_in_o) = mm_nn(
        "l0_down", a0, w2_0.reshape(-1, D), _epi_resid_gate(NC, tm0), [F32, F32], extras=(X1, chunk(0, 5)),
        extra_kinds=('mn', 'n'), tn_cap=1024, carry=[ex_ag_chips([b16(mlp_w1[1])]), ex_ag_sibling([w_in_o_half])])
    w_out_o = w_out_o.reshape(-1, D)

    h_c = norm_mod("l1_norm1", X2, nw[1, 0][None], chunk(1, 0), chunk(1, 1), NC)
    (qkv1,), (w1_1,) = mm_nn("l1_qkv", h_c, w_in_o, _epi_store(BF16), [BF16], tn_cap=768,
                             carry=[ex_ag_sibling([w1_1_half])])
    o1, (w2_1_half,) = attn_odd_fwd("l1_attn", qkv1, bias8, NC, carry=[ex_ag_chips([b16(mlp_w2[1])])])
    X2l = X2[NC:]
    tm1 = _tile(S, 1100)
    (X3, y1), (w2_1,) = mm_nn("l1_out", o1, w_out_o, _epi_resid_gate(0, tm1), [F32, F32],
                              extras=(X2l, chunk(1, 2)), extra_kinds=('mn', 'n'),
                              carry=[ex_ag_sibling([w2_1_half])])
    h_d = norm_mod("l1_norm2", X3, nw[1, 1][None], chunk(1, 3), chunk(1, 4), 0)
    a1, r1 = mm_nn("l1_up", h_d, w1_1, _epi_relu2, [BF16, BF16], tn_cap=1024)
    X4, z1 = mm_nn("l1_down", a1, w2_1.reshape(-1, D), _epi_resid_gate(0, tm1), [F32, F32], extras=(X3, chunk(1, 5)),
                   extra_kinds=('mn', 'n'), tn_cap=1024)
    dX4, loss_p, dfw_p, dz1, pg2_1 = final_loss("final_loss", X4, final_norm_w[None], loss_target[0], z1, chunk(1, 5))
    w_in = [w_in_e, w_in_o]
    w_out = [w_out_e, w_out_o]
    w1 = [w1_0, w1_1]
    w2 = [w2_0.reshape(-1, D), w2_1.reshape(-1, D)]

    mc4 = (lax.axis_index("c") * 4).astype(jnp.int32)
    my_chip = (2 * lax.axis_index("x") + lax.axis_index("y")).astype(jnp.int32)

    def chip_sum(tag, g8, sib4):
        return rs_chip_sum(f"rs_chip_sum_{tag}", g8, sib4, jnp.stack([mc4, my_chip]))

    du1 = mm_nt("l1_down_dx", dz1, w2[1], _epi_mul2r, BF16, extras=(r1,))
    g_w1_1 = mm_tn("l1_up_dw", h_d, du1, 1)
    g_w2_1, (sib_w1_1,) = mm_tn("l1_down_dw", a1, dz1, 0, carry=[ex_rs_sibling([g_w1_1])])
    dh_d, (rem_w1_1, sib_w2_1) = mm_nt(
        "l1_up_dx", du1, w1[1], _epi_store(F32), F32,
        carry=[ex_rs_chips([chip_sum("w1_1", g_w1_1, sib_w1_1)]), ex_rs_sibling([g_w2_1])])
    dX3, pn2_1, dy1, pg1_1 = norm_bwd("l1_norm2_bwd", X3, dh_d, dX4, nw[1, 1][None], chunk(1, 4), 0,
                                      gate=(y1, chunk(1, 2)))
    do1 = mm_nt("l1_out_dx", dy1, w_out[1], _epi_store(BF16), BF16)
    g_wout_1 = mm_tn("l1_out_dw", o1, dy1, 0)
    (dq1, dk1, dv1, dbias8), (rem_w2_1, sib_wout_1) = attn_odd_bwd(
        "l1_attn_bwd", qkv1, bias8, do1, NC,
        carry=[ex_rs_chips([chip_sum("w2_1", g_w2_1, sib_w2_1)]), ex_rs_sibling([g_wout_1])])
    dqkv1 = jnp.concatenate([jnp.pad(dq1, ((NC, 0), (0, 0))), dk1, dv1], axis=1).astype(BF16)
    dh_c, (rem_wout_1,) = mm_nt("l1_qkv_dx", dqkv1, w_in[1], _epi_store(F32), F32,
                                carry=[ex_rs_chips([chip_sum("wout_1", g_wout_1, sib_wout_1)])])
    g_win_1 = mm_tn("l1_qkv_dw", h_c, dqkv1, 1)
    dX2, pn1_1, dz0, pg2_0 = norm_bwd("l1_norm1_bwd", X2, dh_c, dX3, nw[1, 0][None], chunk(1, 1), NC, dres_skip=NC,
                                      gate=(z0, chunk(0, 5)))
    d_rpb = na_bias_grad("rpb_grad", na_span_bias_grad(dbias8))

    du0, (sib_win_1,) = mm_nt("l0_down_dx", dz0, w2[0], _epi_mul2r, BF16, extras=(r0,),
                              carry=[ex_rs_sibling([g_win_1])])
    g_w1_0, (rem_win_1,) = mm_tn("l0_up_dw", h_b, du0, 1,
                                 carry=[ex_rs_chips([chip_sum("win_1", g_win_1, sib_win_1)])])
    g_w2_0, (sib_w1_0,) = mm_tn("l0_down_dw", a0, dz0, 0, carry=[ex_rs_sibling([g_w1_0])])
    dh_b, (rem_w1_0, sib_w2_0) = mm_nt(
        "l0_up_dx", du0, w1[0], _epi_store(F32), F32,
        carry=[ex_rs_chips([chip_sum("w1_0", g_w1_0, sib_w1_0)]), ex_rs_sibling([g_w2_0])])
    dX1, pn2_0, dy0, pg1_0 = norm_bwd("l0_norm2_bwd", X1, dh_b, dX2, nw[0, 1][None], chunk(0, 4), NC,
                                      gate=(y0, chunk(0, 2)))
    do0 = mm_nt("l0_out_dx", dy0, w_out[0], _epi_store(BF16), BF16)
    g_wout_0 = mm_tn("l0_out_dw", o0, dy0, 0)
    (dq0, dk0, dv0, dsink_p), (rem_w2_0, sib_wout_0) = attn_even_bwd(
        "l0_attn_bwd", qkvh0, sink, do0, NC, TQ_B,
        carry=[ex_rs_chips([chip_sum("w2_0", g_w2_0, sib_w2_0)]), ex_rs_sibling([g_wout_0])])
    dqkv0, pqk = prep_even_bwd("l0_prep_bwd", qkv0, dq0, dk0, dv0, ev_q_norm, ev_k_norm, cos, sa, sb)
    g_win_0, (rem_wout_0,) = mm_tn("l0_qkv_dw", h_a, dqkv0, 1,
                                   carry=[ex_rs_chips([chip_sum("wout_0", g_wout_0, sib_wout_0)])])
    dh_a, (sib_win_0,) = mm_nt("l0_qkv_dx", dqkv0, w_in[0], _epi_store(F32), F32, carry=[ex_rs_sibling([g_win_0])])
    (dx_lat, pn1_0), (rem_win_0,) = norm_bwd(
        "l0_norm1_bwd", X0, dh_a, dX1, nw[0, 0][None], chunk(0, 1), NC, out_skip=NC,
        carry=[ex_rs_chips([chip_sum("win_0", g_win_0, sib_win_0)])])
    grad_x = dx_lat[None]

    def dmod(grp, pn1, pg1, pn2, pg2):
        return jnp.concatenate([pn1[grp], pn1[2 + grp], pg1[grp], pn2[grp], pn2[2 + grp], pg2[grp]])

    dmod_lat = jnp.stack([dmod(1, pn1_0, pg1_0, pn2_0, pg2_0), dmod(1, pn1_1, pg1_1, pn2_1, pg2_1)])
    dmod_ctx = jnp.stack([dmod(0, pn1_0, pg1_0, pn2_0, pg2_0), dmod(0, pn1_1, pg1_1, pn2_1, pg2_1)])
    dnw_p = jnp.stack([pn1_0[4], pn2_0[4], pn1_1[4], pn2_1[4]])
    pieces = [dmod_lat, dmod_ctx, dnw_p, pqk[0], pqk[1], dsink_p[8:, 0, 0], d_rpb, dfw_p[0], loss_p[0, 0]]
    sizes = [int(np.prod(p.shape)) for p in pieces]
    rows = [_rows8(s) for s in sizes]
    pack2 = jnp.concatenate([_pad_rows(p, r) for p, r in zip(pieces, rows)], axis=0)
    g2, s2 = ag_small("ag_small_grads", pack2, with_sum=True)
    offs = np.concatenate([[0], np.cumsum(rows)])

    def piece(arr, i, shape):
        return arr[..., offs[i]:offs[i + 1], :].reshape(arr.shape[:-2] + (-1,))[..., :sizes[i]].reshape(
            arr.shape[:-2] + shape)

    dmod_all = piece(g2, 0, (2, 6 * D))
    dmodc_sum = piece(s2, 1, (2, 6 * D))
    dnw_sum = piece(s2, 2, (2, 2, D))
    g_qn = piece(s2, 3, ev_q_norm.shape)
    g_kn = piece(s2, 4, ev_k_norm.shape)
    g_sink = piece(s2, 5, ev_sink.shape)
    g_rpb = piece(s2, 6, od_rpb.shape)
    g_fw = piece(s2, 7, final_norm_w.shape)
    loss = piece(s2, 8, ())

    dm16 = jnp.concatenate([dmod_all.transpose(1, 0, 2), dmodc_sum[:, None, :],
                            jnp.zeros((2, NDEV - 1, 6 * D), F32)], axis=1)
    dm16_loc = lax.dynamic_slice_in_dim(dm16.reshape(2, 2 * NDEV, NDEV, ada_cols), me, 1, axis=2)[:, :, 0, :]
    g_ada_b = _vmem_call("ada_b_grad", lambda v: jnp.sum(v, axis=1),
                         jax.ShapeDtypeStruct((2, 6 * D), F32), dm16)
    g_ada_w = []
    dact_p = None
    for i in range(2):
        dmb = dm16_loc[i].astype(BF16)
        g_ada_w.append(mm_tn(f"ada_w_grad{i}", act, dmb, None))
        part = mm_nt(f"ada_dact{i}", dmb, ada_w[i], _epi_store(F32), F32)
        dact_p = part if dact_p is None else dact_p + part
    _, dact = ag_small("ag_cctx", dact_p, with_sum=True)

    def cctx_grad(da, cc):
        sg = 1.0 / (1.0 + jnp.exp(-cc))
        return da[NDEV:NDEV + 1] * (sg * (1.0 + cc * (1.0 - sg)))

    g_cctx = _vmem_call("cctx_grad", cctx_grad, jax.ShapeDtypeStruct((1, D), F32), dact, c_ctx[None])[0]

    grads = [g_win_0, g_wout_0, g_w1_0, g_w2_0, g_win_1, g_wout_1, g_w1_1, g_w2_1]
    sib = [sib_win_0, sib_wout_0, sib_w1_0, sib_w2_0, sib_win_1, sib_wout_1, sib_w1_1, sib_w2_1]
    rem = [rem_win_0, rem_wout_0, rem_w1_0, rem_w2_0, rem_win_1, rem_wout_1, rem_w1_1, rem_w2_1]
    own_idx = jnp.stack([mc4 + my_chip, my_chip])

    def big(tag, w, m, v, ts):
        res = None
        for l, t in enumerate(ts):
            res = adamw_rs(f"adamw_{tag}_{l}", w, grads[t], sib[t], rem[t], m, v, own_idx, l, res)
        return tuple(res)

    r_ev_w_in = big('ev_w_in', ev_w_in, m_ev_w_in, v_ev_w_in, [0])
    r_ev_w_out = big('ev_w_out', ev_w_out, m_ev_w_out, v_ev_w_out, [1])
    r_mlp_w1 = big('mlp_w1', mlp_w1, m_mlp_w1, v_mlp_w1, [2, 6])
    r_mlp_w2 = big('mlp_w2', mlp_w2, m_mlp_w2, v_mlp_w2, [3, 7])
    r_od_w_in = big('od_w_in', od_w_in, m_od_w_in, v_od_w_in, [4])
    r_od_w_out = big('od_w_out', od_w_out, m_od_w_out, v_od_w_out, [5])

    g_ada = jnp.stack(g_ada_w)
    r_ada_w = adamw_rows("adamw_ada_w", ada_w.reshape(2 * D, ada_cols), g_ada.reshape(2 * D, ada_cols),
                         m_ada_w.reshape(2 * D, ada_cols), v_ada_w.reshape(2 * D, ada_cols))
    r_ada_w = tuple(u.reshape(2, D, ada_cols) for u in r_ada_w)

    g_nw_loc = lax.dynamic_slice_in_dim(dnw_sum, me * nw_cols, nw_cols, axis=2)
    small = [(c_ctx, g_cctx, m_c_ctx, v_c_ctx), (ada_b, g_ada_b, m_ada_b, v_ada_b),
             (norm_w, g_nw_loc, m_norm_w, v_norm_w), (ev_q_norm, g_qn, m_ev_q_norm, v_ev_q_norm),
             (ev_k_norm, g_kn, m_ev_k_norm, v_ev_k_norm), (ev_sink, g_sink, m_ev_sink, v_ev_sink),
             (od_rpb, g_rpb, m_od_rpb, v_od_rpb), (final_norm_w, g_fw, m_final_norm_w, v_final_norm_w)]
    srows = [_rows8(int(np.prod(w.shape))) for w, _, _, _ in small]
    packs = [jnp.concatenate([_pad_rows(tup[k], r) for tup, r in zip(small, srows)], axis=0) for k in range(4)]
    sres = adamw_rows("adamw_small", *packs)
    soffs = np.concatenate([[0], np.cumsum(srows)])

    def unpack(arr, i):
        w = small[i][0]
        return arr[soffs[i]:soffs[i + 1]].reshape(-1)[:int(np.prod(w.shape))].reshape(w.shape)

    sm = [[unpack(sres[k], i) for i in range(len(small))] for k in range(4)]

    def outs(k):
        big_k = {'ada_w': r_ada_w[k], 'mlp_w1': r_mlp_w1[k], 'mlp_w2': r_mlp_w2[k], 'ev_w_in': r_ev_w_in[k],
                 'ev_w_out': r_ev_w_out[k], 'od_w_in': r_od_w_in[k], 'od_w_out': r_od_w_out[k]}
        return (sm[k][0], big_k['ada_w'], sm[k][1], sm[k][2], big_k['mlp_w1'], big_k['mlp_w2'], big_k['ev_w_in'],
                big_k['ev_w_out'], sm[k][3], sm[k][4], sm[k][5], big_k['od_w_in'], big_k['od_w_out'], sm[k][6],
                sm[k][7])

    return (loss, grad_x, *outs(0), *outs(1), *outs(2), *outs(3))
```

```python
import numpy as np
import jax
import jax.numpy as jnp
from jax import lax
from jax.experimental import pallas as pl
from jax.experimental.pallas import tpu as pltpu

F32 = jnp.float32
BF16 = jnp.bfloat16
MESH = pl.DeviceIdType.MESH

NDEV = 8
HEAD = 128
GRID_W = 64
NA_KH, NA_KW = 8, 16
WINDOW = 128
ROPE_THETA = 10000.0
EPS = 1e-6
NEG = -1e30
SCALE = HEAD ** -0.5
ROW_TILE = 256
VMEM_LIMIT = 56 * 1024 * 1024

ADAM_LR, ADAM_B1, ADAM_B2, ADAM_EPS, ADAM_WD, ADAM_STEP = 0.001, 0.9, 0.999, 1e-08, 0.01, 10

NT = (((1,), (1,)), ((), ()))
NN = (((1,), (0,)), ((), ()))
TN = (((0,), (0,)), ((), ()))


def _cparams(sem):
    return pltpu.CompilerParams(dimension_semantics=sem, vmem_limit_bytes=VMEM_LIMIT)


def _tile(n, cap):
    if n <= cap:
        return n
    t = cap - cap % 64
    while t >= 64:
        if n % t == 0:
            return t
        t -= 64
    raise ValueError((n, cap))


def _dot(a, b, dims):
    return lax.dot_general(a.astype(BF16), b.astype(BF16), dims, preferred_element_type=F32)


def _slot(d):
    return (d % 2) * 4 + d // 2


class Exchange:
    def __init__(self, ins, out_shapes, aliases, n_sems, start, finish):
        self.ins, self.out_shapes, self.aliases, self.n_sems = list(ins), list(out_shapes), dict(aliases), n_sems
        self.start, self.finish = start, finish


def merge_exchanges(xs):
    ins, outs, aliases, bases, n = [], [], {}, [], 0
    for x in xs:
        bases.append((len(ins), len(outs), n))
        aliases.update({len(ins) + i: len(outs) + o for i, o in x.aliases.items()})
        ins += x.ins
        outs += x.out_shapes
        n += x.n_sems

    def run(which):
        def f(ci, co, ss, rs, base):
            for x, (i0, o0, s0) in zip(xs, bases):
                getattr(x, which)(ci[i0:i0 + len(x.ins)], co[o0:o0 + len(x.out_shapes)], ss, rs, base + s0)
        return f

    return Exchange(ins, outs, aliases, n, run('start'), run('finish'))


def _call(name, body, grid, ins, in_specs, out_shape, out_specs, scratch, sems, carry=None):
    if not carry:
        return pl.pallas_call(body, grid=grid, in_specs=in_specs, out_specs=out_specs, out_shape=out_shape,
                              scratch_shapes=scratch, compiler_params=_cparams(sems), name=name)(*ins)
    x = merge_exchanges(carry)
    n_in, n_ci, n_out, n_co, n_sc = len(ins), len(x.ins), len(out_shape), len(x.out_shapes), len(scratch)

    def wrapped(*refs):
        p = [0]

        def take(k):
            p[0] += k
            return refs[p[0] - k:p[0]]

        a, ci, o, co, sc = take(n_in), take(n_ci), take(n_out), take(n_co), take(n_sc)
        ss, rs = take(2)
        first = pl.program_id(0) == 0
        last = pl.program_id(0) == grid[0] - 1
        for d in range(1, len(grid)):
            first = jnp.logical_and(first, pl.program_id(d) == 0)
            last = jnp.logical_and(last, pl.program_id(d) == grid[d] - 1)

        @pl.when(first)
        def _():
            x.start(ci, co, ss, rs, 0)

        body(*a, *o, *sc)

        @pl.when(last)
        def _():
            x.finish(ci, co, ss, rs, 0)

    hbm = pl.BlockSpec(memory_space=pl.ANY)
    res = pl.pallas_call(
        wrapped, grid=grid, in_specs=list(in_specs) + [hbm] * n_ci, out_specs=list(out_specs) + [hbm] * n_co,
        out_shape=list(out_shape) + x.out_shapes,
        input_output_aliases={n_in + i: n_out + o for i, o in x.aliases.items()},
        scratch_shapes=list(scratch) + [pltpu.SemaphoreType.DMA((x.n_sems,)), pltpu.SemaphoreType.DMA((x.n_sems,))],
        compiler_params=_cparams(("arbitrary",) * len(grid)), name=name)(*ins, *x.ins)
    return list(res[:n_out]) + [list(res[n_out:])]


def _mm_core(name, grid, ins, in_specs, out_shape, out_specs, dims, acc_shape, epi, carry=None):
    nk = grid[2]
    n_extra = len(ins) - 2

    def body_single(*refs):
        epi(_dot(refs[0][...], refs[1][...], dims), refs[2:2 + n_extra], refs[2 + n_extra:])

    def body(*refs):
        a_ref, b_ref = refs[0], refs[1]
        ex = refs[2:2 + n_extra]
        outs = refs[2 + n_extra:-1]
        acc = refs[-1]
        k = pl.program_id(2)

        @pl.when(k == 0)
        def _():
            acc[...] = _dot(a_ref[...], b_ref[...], dims)

        @pl.when(jnp.logical_and(k > 0, k < nk - 1))
        def _():
            acc[...] += _dot(a_ref[...], b_ref[...], dims)

        @pl.when(k == nk - 1)
        def _():
            epi(acc[...] + _dot(a_ref[...], b_ref[...], dims), ex, outs)

    if nk == 1:
        return _call(name, body_single, grid, ins, in_specs, out_shape, out_specs, [],
                     ("parallel", "parallel", "arbitrary"), carry)
    return _call(name, body, grid, ins, in_specs, out_shape, out_specs, [pltpu.VMEM(acc_shape, F32)],
                 ("parallel", "parallel", "arbitrary"), carry)


def _split(res, n, carry):
    own = res[0] if n == 1 else list(res[:n])
    return (own, res[n]) if carry else own


def _epi_store(dtype):
    def epi(acc, ex, outs):
        outs[0][...] = acc.astype(dtype)
    return epi


def _epi_bias(acc, ex, outs):
    outs[0][...] = acc + ex[0][...]


def _epi_relu2(acc, ex, outs):
    r = jnp.maximum(acc, 0.0)
    outs[0][...] = (r * r).astype(BF16)
    outs[1][...] = r.astype(BF16)


def _epi_mul2r(acc, ex, outs):
    outs[0][...] = (acc * (2.0 * ex[0][...].astype(F32))).astype(BF16)


def _epi_resid_gate(nctx, tm):
    def epi(acc, ex, outs):
        rows = pl.program_id(0) * tm + lax.broadcasted_iota(jnp.int32, (tm, 1), 0)
        g = jnp.where(rows < nctx, ex[1][0:1, :], ex[1][1:2, :])
        outs[0][...] = ex[0][...] + g * acc
        outs[1][...] = acc
    return epi


def mm_nn(name, a, w, epi, outs, extras=(), extra_kinds=(), tm_cap=1100, tn_cap=512, tk_cap=2048, carry=None):
    M, K = a.shape
    if w.ndim == 3:
        ns = w.shape[2]
        N = NDEV * ns
        tn = _tile(ns, tn_cap)
        nper = ns // tn
    else:
        N = w.shape[1]
        tn = _tile(N, tn_cap)
    tm = _tile(M, tm_cap)
    tk = _tile(K, tk_cap)
    grid = (M // tm, N // tn, K // tk)
    a_spec = pl.BlockSpec((tm, tk), lambda i, j, k: (i, k))
    if w.ndim == 3:
        b_spec = pl.BlockSpec((None, tk, tn), lambda i, j, k: (j // nper, k, j % nper))
    else:
        b_spec = pl.BlockSpec((tk, tn), lambda i, j, k: (k, j))
    ex_specs = []
    for e, kind in zip(extras, extra_kinds):
        if kind == 'mn':
            ex_specs.append(pl.BlockSpec((tm, tn), lambda i, j, k: (i, j)))
        else:
            ex_specs.append(pl.BlockSpec((e.shape[0], tn), lambda i, j, k: (0, j)))
    out_shape = [jax.ShapeDtypeStruct((M, N), dt) for dt in outs]
    out_specs = [pl.BlockSpec((tm, tn), lambda i, j, k: (i, j)) for _ in outs]
    res = _mm_core(name, grid, (a, w, *extras), [a_spec, b_spec, *ex_specs], out_shape, out_specs, NN, (tm, tn), epi,
                   carry)
    return (list(res[:len(outs)]), res[len(outs)]) if carry else res


def mm_nt(name, a, w, epi, out_dtype, extras=(), tm_cap=1100, to_cap=1024, tc_cap=2048, carry=None):
    M, N = a.shape
    tm = _tile(M, tm_cap)
    if w.ndim == 3:
        Kw, ns = w.shape[1], w.shape[2]
        tc = _tile(ns, tc_cap)
        cper = ns // tc
    else:
        Kw = w.shape[0]
        tc = _tile(N, tc_cap)
    to = _tile(Kw, to_cap)
    grid = (M // tm, Kw // to, N // tc)
    a_spec = pl.BlockSpec((tm, tc), lambda i, j, k: (i, k))
    if w.ndim == 3:
        b_spec = pl.BlockSpec((None, to, tc), lambda i, j, k: (k // cper, j, k % cper))
    else:
        b_spec = pl.BlockSpec((to, tc), lambda i, j, k: (j, k))
    ex_specs = [pl.BlockSpec((tm, to), lambda i, j, k: (i, j)) for _ in extras]
    out_shape = [jax.ShapeDtypeStruct((M, Kw), out_dtype)]
    out_specs = [pl.BlockSpec((tm, to), lambda i, j, k: (i, j))]
    return _split(_mm_core(name, grid, (a, w, *extras), [a_spec, b_spec, *ex_specs], out_shape, out_specs, NT, (tm, to),
                           epi, carry), 1, carry)


def mm_tn(name, a, b, shard_axis, to_cap=1024, tn_cap=1024, tc_cap=2200, carry=None):
    M, Ka = a.shape
    N = b.shape[1]
    tc = _tile(M, tc_cap)
    if shard_axis is None:
        to, tn = _tile(Ka, to_cap), _tile(N, tn_cap)
        shape = (Ka, N)
        oblk = (to, tn)
        omap = lambda i, j, k: (i, j)
    elif shard_axis == 1:
        ns = N // NDEV
        to, tn = _tile(Ka, to_cap), _tile(ns, tn_cap)
        per = ns // tn
        shape = (NDEV, Ka, ns)
        oblk = (None, to, tn)
        omap = lambda i, j, k: (_slot(j // per), i, j % per)
    else:
        rs = Ka // NDEV
        to, tn = _tile(rs, to_cap), _tile(N, tn_cap)
        per = rs // to
        shape = (NDEV, rs, N)
        oblk = (None, to, tn)
        omap = lambda i, j, k: (_slot(i // per), i % per, j)
    grid = (Ka // to, N // tn, M // tc)
    a_spec = pl.BlockSpec((tc, to), lambda i, j, k: (k, i))
    b_spec = pl.BlockSpec((tc, tn), lambda i, j, k: (k, j))
    out_shape = [jax.ShapeDtypeStruct(shape, F32)]
    out_specs = [pl.BlockSpec(oblk, omap)]
    return _split(_mm_core(name, grid, (a, b), [a_spec, b_spec], out_shape, out_specs, TN, (to, tn), _epi_store(F32),
                           carry), 1, carry)


def _row_spec(D):
    return pl.BlockSpec((ROW_TILE, D), lambda i: (i, 0))


def _const_spec(r, D):
    return pl.BlockSpec((r, D), lambda i: (0, 0))


def _grp(ref, is_ctx):
    return jnp.where(is_ctx, ref[0:1, :], ref[1:2, :])


def norm_mod(name, x, nw, sh, sc, nctx):
    R, D = x.shape
    assert R % ROW_TILE == 0 and nctx % ROW_TILE == 0

    def body(x_ref, nw_ref, sh_ref, sc_ref, o_ref):
        is_ctx = pl.program_id(0) * ROW_TILE < nctx
        xv = x_ref[...]
        rstd = lax.rsqrt(jnp.mean(xv * xv, axis=-1, keepdims=True) + EPS)
        n = xv * rstd * nw_ref[...]
        o_ref[...] = (n * (1.0 + _grp(sc_ref, is_ctx)) + _grp(sh_ref, is_ctx)).astype(BF16)

    return pl.pallas_call(
        body, grid=(R // ROW_TILE,),
        in_specs=[_row_spec(D), _const_spec(1, D), _const_spec(2, D), _const_spec(2, D)],
        out_specs=_row_spec(D), out_shape=jax.ShapeDtypeStruct((R, D), BF16),
        compiler_params=_cparams(("parallel",)), name=name)(x, nw, sh, sc)


def _gate_rows(dxv, y_ref, g_ref, is_ctx, dy_ref, gpart_ref):
    dy_ref[...] = (dxv * _grp(g_ref, is_ctx)).astype(BF16)
    s = jnp.sum(dxv * y_ref[...], axis=0, keepdims=True)
    zero = jnp.zeros_like(s)
    gpart_ref[0:1, :] += jnp.where(is_ctx, s, zero)
    gpart_ref[1:2, :] += jnp.where(is_ctx, zero, s)


def norm_bwd(name, x, dh, dres, nw, sc, nctx, dres_skip=0, out_skip=0, carry=None, gate=None):
    R, D = x.shape
    assert R % ROW_TILE == 0 and nctx % ROW_TILE == 0 and dres_skip % ROW_TILE == 0 and out_skip % ROW_TILE == 0
    res_tiles, out_tiles = dres_skip // ROW_TILE, out_skip // ROW_TILE

    def body(x_ref, dh_ref, dres_ref, nw_ref, sc_ref, *rest):
        if gate is None:
            dx_ref, part_ref = rest
        else:
            y_ref, g_ref, dx_ref, part_ref, dy_ref, gpart_ref = rest
        i = pl.program_id(0)
        is_ctx = i * ROW_TILE < nctx

        @pl.when(i == 0)
        def _():
            part_ref[...] = jnp.zeros_like(part_ref)
            if gate is not None:
                gpart_ref[...] = jnp.zeros_like(gpart_ref)

        xv = x_ref[...]
        dhv = dh_ref[...]
        w = nw_ref[...]
        rstd = lax.rsqrt(jnp.mean(xv * xv, axis=-1, keepdims=True) + EPS)
        xhat = xv * rstd
        n = xhat * w
        dn = dhv * (1.0 + _grp(sc_ref, is_ctx))
        dxhat = dn * w
        dres = dres_ref[...]
        if res_tiles:
            dres = jnp.where(i < res_tiles, 0.0, dres)
        dxv = dres + rstd * (dxhat - xhat * jnp.mean(dxhat * xhat, axis=-1, keepdims=True))
        dx_ref[...] = dxv
        s_sh = jnp.sum(dhv, axis=0, keepdims=True)
        s_sc = jnp.sum(dhv * n, axis=0, keepdims=True)
        s_nw = jnp.sum(dn * xhat, axis=0, keepdims=True)
        zero = jnp.zeros_like(s_sh)
        part_ref[0:1, :] += jnp.where(is_ctx, s_sh, zero)
        part_ref[1:2, :] += jnp.where(is_ctx, zero, s_sh)
        part_ref[2:3, :] += jnp.where(is_ctx, s_sc, zero)
        part_ref[3:4, :] += jnp.where(is_ctx, zero, s_sc)
        part_ref[4:5, :] += s_nw
        if gate is not None:
            _gate_rows(dxv, y_ref, g_ref, is_ctx, dy_ref, gpart_ref)

    ins = [x, dh, dres, nw, sc]
    in_specs = [_row_spec(D), _row_spec(D), pl.BlockSpec((ROW_TILE, D), lambda i: (jnp.maximum(i - res_tiles, 0), 0)),
                _const_spec(1, D), _const_spec(2, D)]
    out_shape = [jax.ShapeDtypeStruct((R - out_skip, D), F32), jax.ShapeDtypeStruct((8, D), F32)]
    out_specs = [pl.BlockSpec((ROW_TILE, D), lambda i: (jnp.maximum(i - out_tiles, 0), 0)), _const_spec(8, D)]
    if gate is not None:
        assert out_skip == 0
        ins += list(gate)
        in_specs += [_row_spec(D), _const_spec(2, D)]
        out_shape += [jax.ShapeDtypeStruct((R, D), BF16), jax.ShapeDtypeStruct((8, D), F32)]
        out_specs += [_row_spec(D), _const_spec(8, D)]
    res = _call(name, body, (R // ROW_TILE,), ins, in_specs, out_shape, out_specs, [], ("arbitrary",), carry)
    return _split(res, len(out_shape), carry)


def final_loss(name, x, fw, tgt, y, g):
    S, D = x.shape

    def body(x_ref, fw_ref, t_ref, y_ref, g_ref, dx_ref, loss_ref, dfw_ref, dy_ref, gpart_ref):
        i = pl.program_id(0)

        @pl.when(i == 0)
        def _():
            loss_ref[...] = jnp.zeros_like(loss_ref)
            dfw_ref[...] = jnp.zeros_like(dfw_ref)
            gpart_ref[...] = jnp.zeros_like(gpart_ref)

        xv = x_ref[...]
        w = fw_ref[...]
        rstd = lax.rsqrt(jnp.mean(xv * xv, axis=-1, keepdims=True) + EPS)
        xhat = xv * rstd
        e = xhat * w - t_ref[...]
        loss_ref[...] += 0.5 * jnp.sum(jnp.mean(e * e, axis=-1, keepdims=True))
        dout = e * (1.0 / D)
        dfw_ref[0:1, :] += jnp.sum(dout * xhat, axis=0, keepdims=True)
        dxhat = dout * w
        dxv = rstd * (dxhat - xhat * jnp.mean(dxhat * xhat, axis=-1, keepdims=True))
        dx_ref[...] = dxv
        _gate_rows(dxv, y_ref, g_ref, False, dy_ref, gpart_ref)

    return pl.pallas_call(
        body, grid=(S // ROW_TILE,),
        in_specs=[_row_spec(D), _const_spec(1, D), _row_spec(D), _row_spec(D), _const_spec(2, D)],
        out_specs=[_row_spec(D), pl.BlockSpec((8, 128), lambda i: (0, 0)), _const_spec(8, D), _row_spec(D),
                   _const_spec(8, D)],
        out_shape=[jax.ShapeDtypeStruct((S, D), F32), jax.ShapeDtypeStruct((8, 128), F32),
                   jax.ShapeDtypeStruct((8, D), F32), jax.ShapeDtypeStruct((S, D), BF16),
                   jax.ShapeDtypeStruct((8, D), F32)],
        compiler_params=_cparams(("arbitrary",)), name=name)(x, fw, tgt, y, g)


def _rope(x, cos, sa, sb):
    return x * cos + pltpu.roll(x, 96, 1) * sa + pltpu.roll(x, 32, 1) * sb


def _rope_t(dy, cos, sa, sb):
    return dy * cos + pltpu.roll(dy * sa, 32, 1) + pltpu.roll(dy * sb, 96, 1)


_EVEN_KINDS = ['qa'] * 8 + ['ka'] * 2 + ['v'] * 2 + ['qb'] * 8 + ['kb'] * 2 + ['v'] * 2
_EVEN_DSRC = ([('q', j) for j in range(8)] + [('k', 0), ('k', 1), ('v', 0), ('v', 1)]
              + [('q', 8 + j) for j in range(8)] + [('k', 2), ('k', 3), ('v', 2), ('v', 3)])


def _cols(j):
    return slice(j * HEAD, (j + 1) * HEAD)


def prep_even(name, qkv, qn, kn, cos, sa, sb):
    T, W = qkv.shape

    def body(x_ref, qn_ref, kn_ref, cos_ref, sa_ref, sb_ref, o_ref):
        cos_, sa_, sb_ = cos_ref[...], sa_ref[...], sb_ref[...]
        for j, kind in enumerate(_EVEN_KINDS):
            x = x_ref[:, _cols(j)]
            if kind in ('qa', 'ka'):
                rstd = lax.rsqrt(jnp.mean(x * x, axis=-1, keepdims=True) + EPS)
                x = x * rstd * (qn_ref[...] if kind == 'qa' else kn_ref[...])
            if kind != 'v':
                x = _rope(x, cos_, sa_, sb_)
            o_ref[:, _cols(j)] = x.astype(BF16)

    blk = pl.BlockSpec((ROW_TILE, W), lambda i: (i, 0))
    tab = pl.BlockSpec((ROW_TILE, HEAD), lambda i: (i, 0))
    one = pl.BlockSpec((1, HEAD), lambda i: (0, 0))
    return pl.pallas_call(
        body, grid=(T // ROW_TILE,), in_specs=[blk, one, one, tab, tab, tab], out_specs=blk,
        out_shape=jax.ShapeDtypeStruct(qkv.shape, BF16),
        compiler_params=_cparams(("parallel",)), name=name)(qkv, qn, kn, cos, sa, sb)


def prep_even_bwd(name, qkv, dq, dk, dv, qn, kn, cos, sa, sb):
    T, W = qkv.shape

    def body(x_ref, dq_ref, dk_ref, dv_ref, qn_ref, kn_ref, cos_ref, sa_ref, sb_ref, o_ref, part_ref):
        @pl.when(pl.program_id(0) == 0)
        def _():
            part_ref[...] = jnp.zeros_like(part_ref)

        cos_, sa_, sb_ = cos_ref[...], sa_ref[...], sb_ref[...]
        src = {'q': dq_ref, 'k': dk_ref, 'v': dv_ref}
        sums = {'qa': None, 'ka': None}
        for j, kind in enumerate(_EVEN_KINDS):
            which, blk_j = _EVEN_DSRC[j]
            d = src[which][:, _cols(blk_j)]
            if kind != 'v':
                d = _rope_t(d, cos_, sa_, sb_)
            if kind in ('qa', 'ka'):
                x = x_ref[:, _cols(j)]
                rstd = lax.rsqrt(jnp.mean(x * x, axis=-1, keepdims=True) + EPS)
                xhat = x * rstd
                s = jnp.sum(d * xhat, axis=0, keepdims=True)
                sums[kind] = s if sums[kind] is None else sums[kind] + s
                dxhat = d * (qn_ref[...] if kind == 'qa' else kn_ref[...])
                d = rstd * (dxhat - xhat * jnp.mean(dxhat * xhat, axis=-1, keepdims=True))
            o_ref[:, _cols(j)] = d.astype(BF16)
        part_ref[0:1, :] += sums['qa']
        part_ref[1:2, :] += sums['ka']

    def rows(w):
        return pl.BlockSpec((ROW_TILE, w), lambda i: (i, 0))

    one = pl.BlockSpec((1, HEAD), lambda i: (0, 0))
    return pl.pallas_call(
        body, grid=(T // ROW_TILE,),
        in_specs=[rows(W), rows(dq.shape[1]), rows(dk.shape[1]), rows(dv.shape[1]), one, one,
                  rows(HEAD), rows(HEAD), rows(HEAD)],
        out_specs=[rows(W), pl.BlockSpec((8, HEAD), lambda i: (0, 0))],
        out_shape=[jax.ShapeDtypeStruct(qkv.shape, BF16), jax.ShapeDtypeStruct((8, HEAD), F32)],
        compiler_params=_cparams(("arbitrary",)), name=name)(qkv, dq, dk, dv, qn, kn, cos, sa, sb)


def _even_maps():
    qmap = lambda h, qb: (qb, jnp.where(h < 8, h, h + 4))
    kmap = lambda h, qb: (0, jnp.where(h < 8, 8 + h // 4, 18 + h // 4))
    vmap = lambda h, qb: (0, jnp.where(h < 8, 10 + h // 4, 20 + h // 4))
    return qmap, kmap, vmap


def _softmax_parts(parts, extra=None):
    m = parts[0].max(axis=-1, keepdims=True)
    for p in parts[1:]:
        m = jnp.maximum(m, p.max(axis=-1, keepdims=True))
    if extra is not None:
        m = jnp.maximum(m, extra)
    es = [jnp.exp(p - m) for p in parts]
    l = es[0].sum(axis=-1, keepdims=True)
    for e in es[1:]:
        l = l + e.sum(axis=-1, keepdims=True)
    ex = None
    if extra is not None:
        ex = jnp.exp(extra - m)
        l = l + ex
    inv = 1.0 / l
    return [e * inv for e in es], (None if ex is None else ex * inv)


def _win_scores(q, k_ref, qb, tq, nctx, S):
    L = tq + 2 * WINDOW
    nqc = nctx // tq
    qlat = (qb - nqc) * tq
    start = pl.multiple_of(jnp.clip(qlat - WINDOW, 0, S - L), 128)
    kc = k_ref[0:nctx, :]
    kw = k_ref[pl.ds(nctx + start, L), :]
    s_c = _dot(q, kc, NT) * SCALE
    s_w = _dot(q, kw, NT) * SCALE
    qpos = qlat + lax.broadcasted_iota(jnp.int32, (tq, 1), 0)
    kpos = start + lax.broadcasted_iota(jnp.int32, (1, L), 1)
    valid = jnp.logical_and(jnp.abs(kpos - qpos) <= WINDOW, qb >= nqc)
    return s_c, jnp.where(valid, s_w, NEG), start, L


def _softmax_raw(raw, stats=None):
    m = raw.max(axis=-1, keepdims=True) if stats is None else stats[0]
    e = jnp.exp2((raw - m) * (SCALE * np.log2(np.e)))
    inv = 1.0 / e.sum(axis=-1, keepdims=True) if stats is None else stats[1]
    return e * inv, (m, inv)


def _glob_keys(qb, tq, nctx, T):
    is_ctx = qb < nctx // tq
    return [(is_ctx, slice(0, nctx)), (jnp.logical_not(is_ctx), slice(0, T))]


def _stats_spec(tq):
    return pl.BlockSpec((None, None, tq, 2), lambda h, qb: (h, qb, 0, 0))


def attn_even_fwd(name, qkvh, sink, nctx, tq, carry=None):
    T = qkvh.shape[0]
    S = T - nctx
    qmap, kmap, vmap = _even_maps()

    def body(sink_ref, q_ref, k_ref, v_ref, o_ref, st_ref):
        h, qb = pl.program_id(0), pl.program_id(1)
        q = q_ref[...]

        for pred, keys in _glob_keys(qb, tq, nctx, T):
            @pl.when(jnp.logical_and(h < 8, pred))
            def _():
                p, (m, inv) = _softmax_raw(_dot(q, k_ref[keys, :], NT))
                st_ref[:, 0:1] = m
                st_ref[:, 1:2] = inv
                o_ref[...] = _dot(p, v_ref[keys, :], NN).astype(BF16)

        @pl.when(h >= 8)
        def _():
            s_c, s_w, start, L = _win_scores(q, k_ref, qb, tq, nctx, S)
            sk = jnp.full((tq, 1), sink_ref[jnp.maximum(h - 8, 0)], F32)
            (p_c, p_w), _ = _softmax_parts([s_c, s_w], sk)
            o = _dot(p_c, v_ref[0:nctx, :], NN) + _dot(p_w, v_ref[pl.ds(nctx + start, L), :], NN)
            o_ref[...] = o.astype(BF16)
            st_ref[...] = jnp.zeros_like(st_ref)

    res = _call(name, body, (16, T // tq), (sink, qkvh, qkvh, qkvh),
                [pl.BlockSpec(memory_space=pltpu.SMEM), pl.BlockSpec((tq, HEAD), qmap),
                 pl.BlockSpec((T, HEAD), kmap), pl.BlockSpec((T, HEAD), vmap)],
                [jax.ShapeDtypeStruct((T, 16 * HEAD), BF16), jax.ShapeDtypeStruct((16, T // tq, tq, 2), F32)],
                [pl.BlockSpec((tq, HEAD), lambda h, qb: (qb, h)), _stats_spec(tq)],
                [], ("parallel", "arbitrary"), carry)
    return _split(res, 2, carry)


def attn_even_bwd(name, qkvh, sink, do, stats, nctx, tq, carry=None):
    T = qkvh.shape[0]
    assert stats.shape == (16, T // tq, tq, 2)
    S = T - nctx
    qmap, kmap, vmap = _even_maps()

    def body(sink_ref, q_ref, k_ref, v_ref, do_ref, st_ref, dq_ref, dk_ref, dv_ref, ds_ref):
        h, qb = pl.program_id(0), pl.program_id(1)
        q = q_ref[...]
        dov = do_ref[...]

        @pl.when(jnp.logical_and(h % 4 == 0, qb == 0))
        def _():
            dk_ref[...] = jnp.zeros_like(dk_ref)
            dv_ref[...] = jnp.zeros_like(dv_ref)

        @pl.when(qb == 0)
        def _():
            ds_ref[...] = jnp.zeros_like(ds_ref)

        for pred, keys in _glob_keys(qb, tq, nctx, T):
            @pl.when(jnp.logical_and(h < 8, pred))
            def _():
                p, _ = _softmax_raw(_dot(q, k_ref[keys, :], NT), (st_ref[:, 0:1], st_ref[:, 1:2]))
                dp = _dot(dov, v_ref[keys, :], NT)
                row = jnp.sum(p * dp, axis=-1, keepdims=True)
                dsb = (p * (dp - row) * SCALE).astype(BF16)
                dq_ref[...] = _dot(dsb, k_ref[keys, :], NN)
                dk_ref[keys, :] += _dot(dsb, q, TN)
                dv_ref[keys, :] += _dot(p, dov, TN)

        @pl.when(h >= 8)
        def _():
            s_c, s_w, start, L = _win_scores(q, k_ref, qb, tq, nctx, S)
            sk = jnp.full((tq, 1), sink_ref[jnp.maximum(h - 8, 0)], F32)
            (p_c, p_w), p_s = _softmax_parts([s_c, s_w], sk)
            win = pl.ds(nctx + start, L)
            dp_c = _dot(dov, v_ref[0:nctx, :], NT)
            dp_w = _dot(dov, v_ref[win, :], NT)
            row = jnp.sum(p_c * dp_c, axis=-1, keepdims=True) + jnp.sum(p_w * dp_w, axis=-1, keepdims=True)
            ds_c = (p_c * (dp_c - row) * SCALE).astype(BF16)
            ds_w = (p_w * (dp_w - row) * SCALE).astype(BF16)
            dq_ref[...] = _dot(ds_c, k_ref[0:nctx, :], NN) + _dot(ds_w, k_ref[win, :], NN)
            dk_ref[0:nctx, :] += _dot(ds_c, q, TN)
            dk_ref[win, :] += _dot(ds_w, q, TN)
            dv_ref[0:nctx, :] += _dot(p_c, dov, TN)
            dv_ref[win, :] += _dot(p_w, dov, TN)
            ds_ref[...] += jnp.sum(-(p_s * row))

    kv_out = pl.BlockSpec((T, HEAD), lambda h, qb: (0, h // 4))
    res = _call(name, body, (16, T // tq), (sink, qkvh, qkvh, qkvh, do, stats),
                [pl.BlockSpec(memory_space=pltpu.SMEM), pl.BlockSpec((tq, HEAD), qmap),
                 pl.BlockSpec((T, HEAD), kmap), pl.BlockSpec((T, HEAD), vmap),
                 pl.BlockSpec((tq, HEAD), lambda h, qb: (qb, h)), _stats_spec(tq)],
                [jax.ShapeDtypeStruct((T, 16 * HEAD), F32), jax.ShapeDtypeStruct((T, 4 * HEAD), F32),
                 jax.ShapeDtypeStruct((T, 4 * HEAD), F32), jax.ShapeDtypeStruct((16, 8, 128), F32)],
                [pl.BlockSpec((tq, HEAD), lambda h, qb: (qb, h)), kv_out, kv_out,
                 pl.BlockSpec((None, 8, 128), lambda h, qb: (h, 0, 0))],
                [], ("arbitrary", "arbitrary"), carry)
    return _split(res, 4, carry)


NA_GROUP = 4
NA_SPAN = NA_KH + NA_GROUP - 1
_NA_PLAN = [[(j, 0) for j in range(NA_GROUP)],
            [(NA_KH // 2, j) for j in range(NA_GROUP)],
            [(NA_KH // 2 + j, NA_GROUP - 1) for j in range(NA_GROUP)]]


def _na_group(g, n_groups, rows):
    last = g == n_groups - 1
    kind = jnp.where(g == 0, 0, jnp.where(last, 2, 1))
    first_row = jnp.where(g == 0, 0, jnp.where(last, rows - NA_SPAN, NA_GROUP * g - NA_KH // 2))
    return kind, first_row


def na_span_bias(bias8):
    LW, LS = NA_KH * GRID_W, NA_SPAN * GRID_W
    kinds = []
    for plan in _NA_PLAN:
        strips = [jnp.pad(bias8[:, off], ((0, 0), (0, 0), (s * GRID_W, LS - LW - s * GRID_W)), constant_values=NEG)
                  for off, s in plan]
        kinds.append(jnp.concatenate(strips, axis=1))
    return jnp.stack(kinds, axis=1)


def na_span_bias_grad(db):
    LW = NA_KH * GRID_W
    out = [None] * NA_KH
    for kind, plan in enumerate(_NA_PLAN):
        for j, (off, s) in enumerate(plan):
            piece = db[:, kind, j * GRID_W:(j + 1) * GRID_W, s * GRID_W:s * GRID_W + LW]
            out[off] = piece if out[off] is None else out[off] + piece
    return jnp.stack(out, axis=1)


def _na_specs(T, nctx, n_groups, rows):
    LS = NA_SPAN * GRID_W
    tq = NA_GROUP * GRID_W
    assert nctx % tq == 0 and n_groups >= 3
    q_spec = pl.BlockSpec((tq, HEAD), lambda h, g: (g + nctx // tq, h))
    k_spec = pl.BlockSpec((T, HEAD), lambda h, g: (0, 16 + h))
    v_spec = pl.BlockSpec((T, HEAD), lambda h, g: (0, 32 + h))
    b_spec = pl.BlockSpec((None, None, tq, LS), lambda h, g: (h, _na_group(g, n_groups, rows)[0], 0, 0))
    row_spec = pl.BlockSpec((tq, HEAD), lambda h, g: (g, h))
    return q_spec, k_spec, v_spec, b_spec, row_spec


def _na_scores(q, k_ref, b_ref, g, n_groups, rows, nctx):
    first_row = _na_group(g, n_groups, rows)[1]
    win = pl.ds(pl.multiple_of(nctx + first_row * GRID_W, GRID_W), NA_SPAN * GRID_W)
    s_c = _dot(q, k_ref[0:nctx, :], NT) * SCALE
    s_w = _dot(q, k_ref[win, :], NT) * SCALE + b_ref[...]
    return s_c, s_w, win


def attn_odd_fwd(name, qkv, bias_s, nctx, carry=None):
    T = qkv.shape[0]
    S = T - nctx
    rows = S // GRID_W
    n_groups = rows // NA_GROUP
    q_spec, k_spec, v_spec, b_spec, row_spec = _na_specs(T, nctx, n_groups, rows)

    def body(q_ref, k_ref, v_ref, b_ref, o_ref):
        s_c, s_w, win = _na_scores(q_ref[...], k_ref, b_ref, pl.program_id(1), n_groups, rows, nctx)
        (p_c, p_w), _ = _softmax_parts([s_c, s_w])
        o_ref[...] = (_dot(p_c, v_ref[0:nctx, :], NN) + _dot(p_w, v_ref[win, :], NN)).astype(BF16)

    res = _call(name, body, (16, n_groups), (qkv, qkv, qkv, bias_s), [q_spec, k_spec, v_spec, b_spec],
                [jax.ShapeDtypeStruct((S, 16 * HEAD), BF16)], [row_spec], [], ("parallel", "arbitrary"), carry)
    return _split(res, 1, carry)


def attn_odd_bwd(name, qkv, bias_s, do, nctx, carry=None):
    T = qkv.shape[0]
    S = T - nctx
    rows = S // GRID_W
    n_groups = rows // NA_GROUP
    q_spec, k_spec, v_spec, b_spec, row_spec = _na_specs(T, nctx, n_groups, rows)

    def body(q_ref, k_ref, v_ref, b_ref, do_ref, dq_ref, dk_ref, dv_ref, db_ref):
        g = pl.program_id(1)
        q = q_ref[...]
        dov = do_ref[...]

        @pl.when(g == 0)
        def _():
            dk_ref[...] = jnp.zeros_like(dk_ref)
            dv_ref[...] = jnp.zeros_like(dv_ref)

        s_c, s_w, win = _na_scores(q, k_ref, b_ref, g, n_groups, rows, nctx)
        (p_c, p_w), _ = _softmax_parts([s_c, s_w])
        dp_c = _dot(dov, v_ref[0:nctx, :], NT)
        dp_w = _dot(dov, v_ref[win, :], NT)
        row = jnp.sum(p_c * dp_c, axis=-1, keepdims=True) + jnp.sum(p_w * dp_w, axis=-1, keepdims=True)
        dsw = p_w * (dp_w - row)
        first_visit = jnp.logical_or(g <= 1, g == n_groups - 1)

        @pl.when(first_visit)
        def _():
            db_ref[...] = dsw

        @pl.when(jnp.logical_not(first_visit))
        def _():
            db_ref[...] += dsw

        ds_c = (p_c * (dp_c - row) * SCALE).astype(BF16)
        ds_w = (dsw * SCALE).astype(BF16)
        dq_ref[...] = _dot(ds_c, k_ref[0:nctx, :], NN) + _dot(ds_w, k_ref[win, :], NN)
        dk_ref[0:nctx, :] += _dot(ds_c, q, TN)
        dk_ref[win, :] += _dot(ds_w, q, TN)
        dv_ref[0:nctx, :] += _dot(p_c, dov, TN)
        dv_ref[win, :] += _dot(p_w, dov, TN)

    kv_out = pl.BlockSpec((T, HEAD), lambda h, g: (0, h))
    res = _call(name, body, (16, n_groups), (qkv, qkv, qkv, bias_s, do), [q_spec, k_spec, v_spec, b_spec, row_spec],
                [jax.ShapeDtypeStruct((S, 16 * HEAD), F32), jax.ShapeDtypeStruct((T, 16 * HEAD), F32),
                 jax.ShapeDtypeStruct((T, 16 * HEAD), F32), jax.ShapeDtypeStruct(bias_s.shape, F32)],
                [row_spec, kv_out, kv_out, b_spec], [], ("arbitrary", "arbitrary"), carry)
    return _split(res, 4, carry)


def _na_onehots():
    o = np.arange(NA_KH)[:, None]
    i = np.arange(NA_KH)[None, :]
    a = i - o + NA_KH - 1
    A = (a[..., None] == np.arange(2 * NA_KH - 1)).astype(np.float32)
    qc = np.arange(GRID_W)[:, None]
    kc = np.arange(GRID_W)[None, :]
    b = np.clip(kc - qc + NA_KW - 1, 0, 2 * NA_KW - 2)
    cs = np.clip(qc - NA_KW // 2, 0, GRID_W - NA_KW)
    valid = (kc >= cs) & (kc < cs + NA_KW)
    B = ((b[..., None] == np.arange(2 * NA_KW - 1)) & valid[..., None]).astype(np.float32)
    return A, B, valid


def na_bias_table(rpb):
    A, B, valid = _na_onehots()
    hp = lax.Precision.HIGHEST
    t = jnp.einsum('hab,oia->hoib', rpb, jnp.asarray(A), precision=hp)
    bias = jnp.einsum('hoib,qkb->hoqik', t, jnp.asarray(B), precision=hp)
    bias = jnp.where(jnp.asarray(valid)[None, None, :, None, :], bias, NEG)
    return bias.reshape(rpb.shape[0], NA_KH, GRID_W, NA_KH * GRID_W)


def na_bias_grad(name, dbias8):
    A, B, _ = _na_onehots()
    H = dbias8.shape[0]
    nb, na = 2 * NA_KW - 1, 2 * NA_KH - 1
    d = dbias8.reshape(H, NA_KH, GRID_W, NA_KH, GRID_W).transpose(0, 1, 3, 2, 4)
    d = d.reshape(H * NA_KH * NA_KH, GRID_W * GRID_W)
    Bp = np.zeros((GRID_W * GRID_W, 128), np.float32)
    Bp[:, :nb] = B.reshape(GRID_W * GRID_W, nb)
    Ap = np.zeros((16, NA_KH * NA_KH), np.float32)
    Ap[:na] = A.reshape(NA_KH * NA_KH, na).T
    rows_per_head = NA_KH * NA_KH

    def split3(x):
        hi = x.astype(BF16)
        r1 = x - hi.astype(F32)
        mid = r1.astype(BF16)
        return hi, mid, (r1 - mid.astype(F32)).astype(BF16)

    def body(d_ref, b_ref, a_ref, o_ref):
        bm, am = b_ref[...], a_ref[...]
        g = sum(lax.dot_general(p, bm, NN, preferred_element_type=F32) for p in split3(d_ref[...]))
        o_ref[...] = sum(lax.dot_general(am, p, NN, preferred_element_type=F32) for p in split3(g))

    out = pl.pallas_call(
        body, grid=(H,),
        in_specs=[pl.BlockSpec((rows_per_head, GRID_W * GRID_W), lambda h: (h, 0)),
                  pl.BlockSpec((GRID_W * GRID_W, 128), lambda h: (0, 0)),
                  pl.BlockSpec((16, rows_per_head), lambda h: (0, 0))],
        out_specs=pl.BlockSpec((None, 16, 128), lambda h: (h, 0, 0)),
        out_shape=jax.ShapeDtypeStruct((H, 16, 128), F32),
        compiler_params=_cparams(("parallel",)), name=name)(d, jnp.asarray(Bp, BF16), jnp.asarray(Ap, BF16))
    return out[:, :na, :nb]


def _vmem_call(name, fn, out_shape, *arrays):
    def body(*refs):
        n = len(arrays)
        res = fn(*[r[...] for r in refs[:n]])
        if not isinstance(res, (tuple, list)):
            res = (res,)
        for o, v in zip(refs[n:], res):
            o[...] = v
    return pl.pallas_call(body, out_shape=out_shape, name=name,
                          compiler_params=pltpu.CompilerParams(vmem_limit_bytes=VMEM_LIMIT))(*arrays)


def _silu(v):
    return v / (1.0 + jnp.exp(-v))


def _adamw_math(w, g, m, v):
    m2 = ADAM_B1 * m + (1.0 - ADAM_B1) * g
    v2 = ADAM_B2 * v + (1.0 - ADAM_B2) * (g * g)
    m_hat = m2 / (1.0 - ADAM_B1 ** ADAM_STEP)
    v_hat = v2 / (1.0 - ADAM_B2 ** ADAM_STEP)
    delta = -ADAM_LR * (m_hat / (jnp.sqrt(v_hat) + ADAM_EPS) + ADAM_WD * w)
    return delta, m2, v2


def _ew_tile(R, C):
    return _tile(R, max(64, (262144 // C) // 64 * 64))


def adamw_rows(name, w, g, m, v, extra_g=None):
    R, C = w.shape
    tr = _ew_tile(R, C)
    extra_g = list(extra_g or [])
    ne = len(extra_g)

    def body(*refs):
        w_ref, g_ref, m_ref, v_ref = refs[:4]
        gs = g_ref[...]
        for e in refs[4:4 + ne]:
            gs = gs + e[...].astype(F32)
        go, do, mo, vo = refs[4 + ne:]
        d, m2, v2 = _adamw_math(w_ref[...], gs, m_ref[...], v_ref[...])
        go[...] = gs
        do[...] = d
        mo[...] = m2
        vo[...] = v2

    spec = pl.BlockSpec((tr, C), lambda i: (i, 0))
    return pl.pallas_call(
        body, grid=(R // tr,), in_specs=[spec] * (4 + ne), out_specs=[spec] * 4,
        out_shape=[jax.ShapeDtypeStruct((R, C), F32)] * 4,
        compiler_params=_cparams(("parallel",)), name=name)(w, g, m, v, *extra_g)


def rs_chip_sum(name, g8, sib4, where):
    _, R, C = g8.shape
    tr = _ew_tile(R, C)

    def body(s_ref, g_ref, b_ref, o_ref):
        o_ref[...] = (g_ref[...] + b_ref[...]).astype(BF16)

    def chip(q, s):
        return (s[1] + 1 + q) % 4

    blk = (None, tr, C)
    grid_spec = pltpu.PrefetchScalarGridSpec(
        num_scalar_prefetch=1, grid=(3, R // tr),
        in_specs=[pl.BlockSpec(blk, lambda q, i, s: (s[0] + chip(q, s), i, 0)),
                  pl.BlockSpec(blk, lambda q, i, s: (chip(q, s), i, 0))],
        out_specs=pl.BlockSpec(blk, lambda q, i, s: (chip(q, s), i, 0)))
    return pl.pallas_call(body, grid_spec=grid_spec, out_shape=jax.ShapeDtypeStruct((4, R, C), BF16),
                          compiler_params=_cparams(("parallel", "parallel")), name=name)(where, g8, sib4)


def adamw_rs(name, w, g8, sib4, rem3, m, v, idx, layer, prev=None):
    L, R, C = w.shape
    tr = _ew_tile(R, C)

    def body(s_ref, w_ref, g_ref, sb_ref, r0_ref, r1_ref, r2_ref, m_ref, v_ref, *rest):
        go, do, mo, vo = rest[-4:]
        gs = g_ref[...] + sb_ref[...]
        for r_ref in (r0_ref, r1_ref, r2_ref):
            gs = gs + r_ref[...].astype(F32)
        d, m2, v2 = _adamw_math(w_ref[...], gs, m_ref[...], v_ref[...])
        go[...] = gs
        do[...] = d
        mo[...] = m2
        vo[...] = v2

    blk = (None, tr, C)
    mine = pl.BlockSpec(blk, lambda i, s: (layer, i, 0))

    def rem(k):
        return pl.BlockSpec(blk, lambda i, s: (k, i, 0))

    prev = list(prev or [])
    grid_spec = pltpu.PrefetchScalarGridSpec(
        num_scalar_prefetch=1, grid=(R // tr,),
        in_specs=[mine, pl.BlockSpec(blk, lambda i, s: (s[0], i, 0)), pl.BlockSpec(blk, lambda i, s: (s[1], i, 0)),
                  rem(0), rem(1), rem(2), mine, mine] + [pl.BlockSpec(memory_space=pl.ANY)] * len(prev),
        out_specs=[mine] * 4)
    return pl.pallas_call(body, grid_spec=grid_spec, out_shape=[jax.ShapeDtypeStruct((L, R, C), F32)] * 4,
                          input_output_aliases={9 + k: k for k in range(len(prev))},
                          compiler_params=_cparams(("parallel",)), name=name)(
                              idx, w, g8, sib4, rem3, rem3, rem3, m, v, *prev)


def _me():
    x, y, c = lax.axis_index("x"), lax.axis_index("y"), lax.axis_index("c")
    return x, y, c


def _flip(v, bit):
    return 1 - v if bit else v


def ag_small(name, x, with_sum=False):
    R, C = x.shape

    def body(x_ref, out_ref, *rest):
        if with_sum:
            sum_ref, send_sems, recv_sems, lsem = rest
        else:
            send_sems, recv_sems, lsem = rest
        mx, my, mc = _me()
        me = 4 * mx + 2 * my + mc
        local = pltpu.make_async_copy(x_ref, out_ref.at[me], lsem)
        local.start()
        sends = []
        for k in range(1, NDEV):
            peer = (_flip(mx, k & 4), _flip(my, k & 2), _flip(mc, k & 1))
            cp = pltpu.make_async_remote_copy(src_ref=x_ref, dst_ref=out_ref.at[me], send_sem=send_sems.at[k - 1],
                                              recv_sem=recv_sems.at[k - 1], device_id=peer, device_id_type=MESH)
            cp.start()
            sends.append(cp)
        for k in range(1, NDEV):
            px, py, pc = _flip(mx, k & 4), _flip(my, k & 2), _flip(mc, k & 1)
            pltpu.make_async_remote_copy(src_ref=x_ref, dst_ref=out_ref.at[4 * px + 2 * py + pc],
                                         send_sem=send_sems.at[k - 1], recv_sem=recv_sems.at[k - 1],
                                         device_id=(px, py, pc), device_id_type=MESH).wait_recv()
        for cp in sends:
            cp.wait_send()
        local.wait()
        if with_sum:
            acc = out_ref[0]
            for d in range(1, NDEV):
                acc = acc + out_ref[d]
            sum_ref[...] = acc

    out_shape = [jax.ShapeDtypeStruct((NDEV, R, C), F32)]
    if with_sum:
        out_shape.append(jax.ShapeDtypeStruct((R, C), F32))
    vm = pl.BlockSpec(memory_space=pltpu.VMEM)
    res = pl.pallas_call(
        body, out_shape=out_shape, in_specs=[vm], out_specs=[vm] * len(out_shape),
        scratch_shapes=[pltpu.SemaphoreType.DMA((NDEV - 1,)), pltpu.SemaphoreType.DMA((NDEV - 1,)),
                        pltpu.SemaphoreType.DMA],
        compiler_params=pltpu.CompilerParams(vmem_limit_bytes=VMEM_LIMIT), name=name)(x)
    return res if with_sum else res[0]


def ag_big(name, shards):
    n = len(shards)

    def body(*refs):
        ins, outs = refs[:n], refs[n:2 * n]
        send_sems, recv_sems, lsems = refs[2 * n:]
        mx, my, mc = _me()
        me = (mx, my, mc)
        sibling = (mx, my, 1 - mc)
        chips = [(1 - mx, my), (mx, 1 - my), (1 - mx, 1 - my)]

        def idx(p):
            return 4 * p[0] + 2 * p[1] + p[2]

        def copy(t, k, block, to, src=None):
            dst = outs[t].at[idx(block)]
            return pltpu.make_async_remote_copy(
                src_ref=dst if src is None else src, dst_ref=dst, send_sem=send_sems.at[7 * t + k],
                recv_sem=recv_sems.at[7 * t + k], device_id=to, device_id_type=MESH)

        started = []
        locals_ = []
        for t in range(n):
            mine = pltpu.make_async_copy(ins[t], outs[t].at[idx(me)], lsems.at[t])
            mine.start()
            locals_.append(mine)
            first = [copy(t, 0, me, sibling, src=ins[t])]
            first += [copy(t, 1 + j, me, (*chip, mc), src=ins[t]) for j, chip in enumerate(chips)]
            for cp in first:
                cp.start()
            started += first
        for t in range(n):
            for j, chip in enumerate(chips):
                copy(t, 1 + j, (*chip, mc), me).wait_recv()
                fwd = copy(t, 4 + j, (*chip, mc), sibling)
                fwd.start()
                started.append(fwd)
        for t in range(n):
            copy(t, 0, sibling, me).wait_recv()
            for j, chip in enumerate(chips):
                copy(t, 4 + j, (*chip, 1 - mc), me).wait_recv()
        for cp in started:
            cp.wait_send()
        for mine in locals_:
            mine.wait()

    anyspec = pl.BlockSpec(memory_space=pl.ANY)
    return pl.pallas_call(
        body, out_shape=[jax.ShapeDtypeStruct((NDEV,) + s.shape, s.dtype) for s in shards],
        in_specs=[anyspec] * n, out_specs=[anyspec] * n,
        scratch_shapes=[pltpu.SemaphoreType.DMA((7 * n,)), pltpu.SemaphoreType.DMA((7 * n,)),
                        pltpu.SemaphoreType.DMA((n,))],
        name=name)(*shards)


def _idx(p):
    return 4 * p[0] + 2 * p[1] + p[2]


def _remote(src, dst, ss, rs, k, to):
    return pltpu.make_async_remote_copy(src_ref=src, dst_ref=dst, send_sem=ss.at[k], recv_sem=rs.at[k],
                                        device_id=to, device_id_type=MESH)


def ex_ag_chips(shards):
    n = len(shards)

    def copies(ci, co, ss, rs, base):
        mx, my, mc = _me()
        me = (mx, my, mc)
        peers = [(mx, my, 1 - mc), (1 - mx, my, mc), (mx, 1 - my, mc), (1 - mx, 1 - my, mc)]
        sends, recvs, local = [], [], []
        for t in range(n):
            b = base + 5 * t
            for k, peer in enumerate(peers):
                sends.append(_remote(ci[t], co[t].at[_idx(me)], ss, rs, b + k, peer))
                recvs.append(_remote(ci[t], co[t].at[_idx(peer)], ss, rs, b + k, peer))
            local.append(pltpu.make_async_copy(ci[t], co[t].at[_idx(me)], ss.at[b + 4]))
        return sends, recvs, local

    def start(ci, co, ss, rs, base):
        sends, _, local = copies(ci, co, ss, rs, base)
        for cp in local + sends:
            cp.start()

    def finish(ci, co, ss, rs, base):
        sends, recvs, local = copies(ci, co, ss, rs, base)
        for cp in recvs:
            cp.wait_recv()
        for cp in sends:
            cp.wait_send()
        for cp in local:
            cp.wait()

    outs = [jax.ShapeDtypeStruct((NDEV,) + s.shape, s.dtype) for s in shards]
    return Exchange(shards, outs, {}, 5 * n, start, finish)


def ex_ag_sibling(bufs):
    n = len(bufs)

    def copies(co, ss, rs, base):
        mx, my, mc = _me()
        sibling = (mx, my, 1 - mc)
        chips = [(1 - mx, my), (mx, 1 - my), (1 - mx, 1 - my)]
        sends, recvs = [], []
        for t in range(n):
            for j, chip in enumerate(chips):
                mine, theirs = co[t].at[_idx((*chip, mc))], co[t].at[_idx((*chip, 1 - mc))]
                sends.append(_remote(mine, mine, ss, rs, base + 3 * t + j, sibling))
                recvs.append(_remote(mine, theirs, ss, rs, base + 3 * t + j, sibling))
        return sends, recvs

    def start(ci, co, ss, rs, base):
        for cp in copies(co, ss, rs, base)[0]:
            cp.start()

    def finish(ci, co, ss, rs, base):
        sends, recvs = copies(co, ss, rs, base)
        for cp in recvs:
            cp.wait_recv()
        for cp in sends:
            cp.wait_send()

    outs = [jax.ShapeDtypeStruct(b.shape, b.dtype) for b in bufs]
    return Exchange(bufs, outs, {t: t for t in range(n)}, 3 * n, start, finish)


def ex_rs_sibling(grads):
    n = len(grads)

    def copies(ci, co, ss, rs, base):
        mx, my, mc = _me()
        return [_remote(ci[t].at[pl.ds((1 - mc) * 4, 4)], co[t], ss, rs, base + t, (mx, my, 1 - mc)) for t in range(n)]

    def start(ci, co, ss, rs, base):
        for cp in copies(ci, co, ss, rs, base):
            cp.start()

    def finish(ci, co, ss, rs, base):
        for cp in copies(ci, co, ss, rs, base):
            cp.wait()

    outs = [jax.ShapeDtypeStruct((4,) + g.shape[1:], g.dtype) for g in grads]
    return Exchange(grads, outs, {}, n, start, finish)


def ex_rs_chips(parts):
    n = len(parts)

    def copies(ci, co, ss, rs, base):
        mx, my, mc = _me()
        cps = []
        for t in range(n):
            for k in range(1, 4):
                px, py = _flip(mx, k & 2), _flip(my, k & 1)
                cps.append(_remote(ci[t].at[2 * px + py], co[t].at[k - 1], ss, rs, base + 3 * t + k - 1, (px, py, mc)))
        return cps

    def start(ci, co, ss, rs, base):
        for cp in copies(ci, co, ss, rs, base):
            cp.start()

    def finish(ci, co, ss, rs, base):
        for cp in copies(ci, co, ss, rs, base):
            cp.wait()

    outs = [jax.ShapeDtypeStruct((3,) + p.shape[1:], p.dtype) for p in parts]
    return Exchange(parts, outs, {}, 3 * n, start, finish)


def run_exchanges(name, xs):
    x = merge_exchanges(xs)
    n_ci, n_co = len(x.ins), len(x.out_shapes)

    def body(*refs):
        ci, co = refs[:n_ci], refs[n_ci:n_ci + n_co]
        ss, rs = refs[n_ci + n_co:]
        x.start(ci, co, ss, rs, 0)
        x.finish(ci, co, ss, rs, 0)

    hbm = pl.BlockSpec(memory_space=pl.ANY)
    return pl.pallas_call(
        body, out_shape=x.out_shapes, in_specs=[hbm] * n_ci, out_specs=[hbm] * n_co, input_output_aliases=x.aliases,
        scratch_shapes=[pltpu.SemaphoreType.DMA((x.n_sems,)), pltpu.SemaphoreType.DMA((x.n_sems,))], name=name)(*x.ins)


def _rope_tables(S, nctx):
    t = jnp.arange(S)
    row = (t // GRID_W).astype(F32)
    col = (t % GRID_W).astype(F32)
    pairs = HEAD // 4
    inv = ROPE_THETA ** (-jnp.arange(pairs, dtype=F32) / pairs)
    ang_r = row[:, None] * inv
    ang_c = col[:, None] * inv
    ang = jnp.concatenate([ang_r, ang_r, ang_c, ang_c], axis=-1)
    cos = jnp.concatenate([jnp.ones((nctx, HEAD), F32), jnp.cos(ang)], axis=0)
    sin = jnp.concatenate([jnp.zeros((nctx, HEAD), F32), jnp.sin(ang)], axis=0)
    lane = jnp.arange(HEAD)[None, :]
    first = (lane & 32) == 0
    return cos, jnp.where(first, -sin, 0.0), jnp.where(first, 0.0, sin)


def _pad_rows(v, rows):
    v = v.reshape(-1).astype(F32)
    return jnp.pad(v, (0, rows * 128 - v.shape[0])).reshape(rows, 128)


def _rows8(n):
    return -(-n // 1024) * 8


def kernel(x, c, ctx, c_ctx, ada_w, ada_b, norm_w, mlp_w1, mlp_w2, ev_w_in, ev_w_out, ev_q_norm, ev_k_norm, ev_sink, od_w_in, od_w_out, od_rpb, final_norm_w, loss_target, m_c_ctx, m_ada_w, m_ada_b, m_norm_w, m_mlp_w1, m_mlp_w2, m_ev_w_in, m_ev_w_out, m_ev_q_norm, m_ev_k_norm, m_ev_sink, m_od_w_in, m_od_w_out, m_od_rpb, m_final_norm_w, v_c_ctx, v_ada_w, v_ada_b, v_norm_w, v_mlp_w1, v_mlp_w2, v_ev_w_in, v_ev_w_out, v_ev_q_norm, v_ev_k_norm, v_ev_sink, v_od_w_in, v_od_w_out, v_od_rpb, v_final_norm_w):
    S, D = x.shape[1], x.shape[2]
    NC = ctx.shape[1]
    T = NC + S
    assert NC == ROW_TILE and S % GRID_W == 0
    ada_cols = ada_w.shape[2]
    nw_cols = norm_w.shape[2]
    me = 4 * lax.axis_index("x") + 2 * lax.axis_index("y") + lax.axis_index("c")

    pack1 = jnp.concatenate([_pad_rows(c, _rows8(D)), _pad_rows(norm_w, _rows8(4 * nw_cols))], axis=0)
    g1 = ag_small("ag_c_normw", pack1)
    c_all = g1[:, :D // 128].reshape(NDEV, D)
    nw_rows = _rows8(D)
    nw = g1[:, nw_rows:nw_rows + 4 * nw_cols // 128].reshape(NDEV, 2, 2, nw_cols)
    nw = nw.transpose(1, 2, 0, 3).reshape(2, 2, D)
    cin = jnp.concatenate([c_all, jnp.broadcast_to(c_ctx[None], (NDEV, D))], axis=0)
    act = _vmem_call("silu_c", lambda v: _silu(v).astype(BF16), jax.ShapeDtypeStruct((2 * NDEV, D), BF16), cin)
    ada_b_loc = lax.dynamic_slice_in_dim(ada_b, me * ada_cols, ada_cols, axis=1)
    mods = [mm_nn(f"mod{i}", act, ada_w[i], _epi_bias, [F32], extras=(ada_b_loc[i:i + 1],), extra_kinds=('n',))[0]
            for i in range(2)]
    gm = ag_small("ag_mod", jnp.concatenate(mods, axis=1))
    gm = gm.reshape(NDEV, 2 * NDEV, 2, ada_cols).transpose(2, 1, 0, 3).reshape(2, 2 * NDEV, NDEV * ada_cols)
    mod_lat = lax.dynamic_index_in_dim(gm, me, axis=1, keepdims=False)
    mod_ctx = gm[:, NDEV]
    mod2 = jnp.stack([mod_ctx, mod_lat], axis=1).reshape(2, 2, 6, D)

    def chunk(i, j):
        return mod2[i, :, j, :]

    def b16(w):
        return w.astype(BF16)

    (w_in_e,) = ag_big("ag_weights_l0_qkv", [b16(ev_w_in[0])])

    cos, sa, sb = _rope_tables(S, NC)
    bias8 = na_span_bias(na_bias_table(od_rpb[0]))
    sink = ev_sink[0]
    TQ_F, TQ_B = 256, 256

    X0 = jnp.concatenate([ctx[0], x[0]], axis=0)
    h_a = norm_mod("l0_norm1", X0, nw[0, 0][None], chunk(0, 0), chunk(0, 1), NC)
    (qkv0,), (w_out_e_half,) = mm_nn("l0_qkv", h_a, w_in_e, _epi_store(F32), [F32],
                                     carry=[ex_ag_chips([b16(ev_w_out[0])])])
    qkvh0 = prep_even("l0_prep", qkv0, ev_q_norm, ev_k_norm, cos, sa, sb)
    (o0, stats0), (w1_0_half, w2_0_half, w_out_o_half, w_out_e) = attn_even_fwd(
        "l0_attn", qkvh0, sink, NC, TQ_F,
        carry=[ex_ag_chips([b16(mlp_w1[0]), b16(mlp_w2[0]), b16(od_w_out[0])]), ex_ag_sibling([w_out_e_half])])
    w_out_e = w_out_e.reshape(-1, D)
    tm0 = _tile(T, 1100)
    (X1, y0), (w1_0, w2_0) = mm_nn("l0_out", o0, w_out_e, _epi_resid_gate(NC, tm0), [F32, F32],
                                   extras=(X0, chunk(0, 2)), extra_kinds=('mn', 'n'),
                                   carry=[ex_ag_sibling([w1_0_half, w2_0_half])])
    h_b = norm_mod("l0_norm2", X1, nw[0, 1][None], chunk(0, 3), chunk(0, 4), NC)
    (a0, r0), (w_in_o_half, w_out_o) = mm_nn(
        "l0_up", h_b, w1_0, _epi_relu2, [BF16, BF16], tn_cap=1024,
        carry=[ex_ag_chips([b16(od_w_in[0])]), ex_ag_sibling([w_out_o_half])])
    (X2, z0), (w1_1_half, w_in_o) = mm_nn(
        "l0_down", a0, w2_0.reshape(-1, D), _epi_resid_gate(NC, tm0), [F32, F32], extras=(X1, chunk(0, 5)),
        extra_kinds=('mn', 'n'), tn_cap=1024, carry=[ex_ag_chips([b16(mlp_w1[1])]), ex_ag_sibling([w_in_o_half])])
    w_out_o = w_out_o.reshape(-1, D)

    h_c = norm_mod("l1_norm1", X2, nw[1, 0][None], chunk(1, 0), chunk(1, 1), NC)
    (qkv1,), (w1_1,) = mm_nn("l1_qkv", h_c, w_in_o, _epi_store(BF16), [BF16], tn_cap=768,
                             carry=[ex_ag_sibling([w1_1_half])])
    o1, (w2_1_half,) = attn_odd_fwd("l1_attn", qkv1, bias8, NC, carry=[ex_ag_chips([b16(mlp_w2[1])])])
    X2l = X2[NC:]
    tm1 = _tile(S, 1100)
    (X3, y1), (w2_1,) = mm_nn("l1_out", o1, w_out_o, _epi_resid_gate(0, tm1), [F32, F32],
                              extras=(X2l, chunk(1, 2)), extra_kinds=('mn', 'n'),
                              carry=[ex_ag_sibling([w2_1_half])])
    h_d = norm_mod("l1_norm2", X3, nw[1, 1][None], chunk(1, 3), chunk(1, 4), 0)
    a1, r1 = mm_nn("l1_up", h_d, w1_1, _epi_relu2, [BF16, BF16], tn_cap=1024, tm_cap=2200)
    X4, z1 = mm_nn("l1_down", a1, w2_1.reshape(-1, D), _epi_resid_gate(0, tm1), [F32, F32], extras=(X3, chunk(1, 5)),
                   extra_kinds=('mn', 'n'), tn_cap=1024)
    dX4, loss_p, dfw_p, dz1, pg2_1 = final_loss("final_loss", X4, final_norm_w[None], loss_target[0], z1, chunk(1, 5))
    w_in = [w_in_e, w_in_o]
    w_out = [w_out_e, w_out_o]
    w1 = [w1_0, w1_1]
    w2 = [w2_0.reshape(-1, D), w2_1.reshape(-1, D)]

    mc4 = (lax.axis_index("c") * 4).astype(jnp.int32)
    my_chip = (2 * lax.axis_index("x") + lax.axis_index("y")).astype(jnp.int32)

    def chip_sum(tag, g8, sib4):
        return rs_chip_sum(f"rs_chip_sum_{tag}", g8, sib4, jnp.stack([mc4, my_chip]))

    du1 = mm_nt("l1_down_dx", dz1, w2[1], _epi_mul2r, BF16, extras=(r1,))
    g_w1_1 = mm_tn("l1_up_dw", h_d, du1, 1)
    g_w2_1, (sib_w1_1,) = mm_tn("l1_down_dw", a1, dz1, 0, carry=[ex_rs_sibling([g_w1_1])])
    dh_d, (rem_w1_1, sib_w2_1) = mm_nt(
        "l1_up_dx", du1, w1[1], _epi_store(F32), F32,
        carry=[ex_rs_chips([chip_sum("w1_1", g_w1_1, sib_w1_1)]), ex_rs_sibling([g_w2_1])])
    dX3, pn2_1, dy1, pg1_1 = norm_bwd("l1_norm2_bwd", X3, dh_d, dX4, nw[1, 1][None], chunk(1, 4), 0,
                                      gate=(y1, chunk(1, 2)))
    do1 = mm_nt("l1_out_dx", dy1, w_out[1], _epi_store(BF16), BF16)
    g_wout_1 = mm_tn("l1_out_dw", o1, dy1, 0)
    (dq1, dk1, dv1, dbias8), (rem_w2_1, sib_wout_1) = attn_odd_bwd(
        "l1_attn_bwd", qkv1, bias8, do1, NC,
        carry=[ex_rs_chips([chip_sum("w2_1", g_w2_1, sib_w2_1)]), ex_rs_sibling([g_wout_1])])
    dqkv1 = jnp.concatenate([jnp.pad(dq1, ((NC, 0), (0, 0))), dk1, dv1], axis=1).astype(BF16)
    dh_c, (rem_wout_1,) = mm_nt("l1_qkv_dx", dqkv1, w_in[1], _epi_store(F32), F32,
                                carry=[ex_rs_chips([chip_sum("wout_1", g_wout_1, sib_wout_1)])])
    g_win_1 = mm_tn("l1_qkv_dw", h_c, dqkv1, 1)
    dX2, pn1_1, dz0, pg2_0 = norm_bwd("l1_norm1_bwd", X2, dh_c, dX3, nw[1, 0][None], chunk(1, 1), NC, dres_skip=NC,
                                      gate=(z0, chunk(0, 5)))
    d_rpb = na_bias_grad("rpb_grad", na_span_bias_grad(dbias8))

    du0, (sib_win_1,) = mm_nt("l0_down_dx", dz0, w2[0], _epi_mul2r, BF16, extras=(r0,),
                              carry=[ex_rs_sibling([g_win_1])])
    g_w1_0, (rem_win_1,) = mm_tn("l0_up_dw", h_b, du0, 1,
                                 carry=[ex_rs_chips([chip_sum("win_1", g_win_1, sib_win_1)])])
    g_w2_0, (sib_w1_0,) = mm_tn("l0_down_dw", a0, dz0, 0, carry=[ex_rs_sibling([g_w1_0])])
    dh_b, (rem_w1_0, sib_w2_0) = mm_nt(
        "l0_up_dx", du0, w1[0], _epi_store(F32), F32,
        carry=[ex_rs_chips([chip_sum("w1_0", g_w1_0, sib_w1_0)]), ex_rs_sibling([g_w2_0])])
    dX1, pn2_0, dy0, pg1_0 = norm_bwd("l0_norm2_bwd", X1, dh_b, dX2, nw[0, 1][None], chunk(0, 4), NC,
                                      gate=(y0, chunk(0, 2)))
    do0 = mm_nt("l0_out_dx", dy0, w_out[0], _epi_store(BF16), BF16)
    g_wout_0 = mm_tn("l0_out_dw", o0, dy0, 0)
    (dq0, dk0, dv0, dsink_p), (rem_w2_0, sib_wout_0) = attn_even_bwd(
        "l0_attn_bwd", qkvh0, sink, do0, stats0, NC, TQ_B,
        carry=[ex_rs_chips([chip_sum("w2_0", g_w2_0, sib_w2_0)]), ex_rs_sibling([g_wout_0])])
    dqkv0, pqk = prep_even_bwd("l0_prep_bwd", qkv0, dq0, dk0, dv0, ev_q_norm, ev_k_norm, cos, sa, sb)
    g_win_0, (rem_wout_0,) = mm_tn("l0_qkv_dw", h_a, dqkv0, 1,
                                   carry=[ex_rs_chips([chip_sum("wout_0", g_wout_0, sib_wout_0)])])
    dh_a, (sib_win_0,) = mm_nt("l0_qkv_dx", dqkv0, w_in[0], _epi_store(F32), F32, carry=[ex_rs_sibling([g_win_0])])
    (dx_lat, pn1_0), (rem_win_0,) = norm_bwd(
        "l0_norm1_bwd", X0, dh_a, dX1, nw[0, 0][None], chunk(0, 1), NC, out_skip=NC,
        carry=[ex_rs_chips([chip_sum("win_0", g_win_0, sib_win_0)])])
    grad_x = dx_lat[None]

    def dmod(grp, pn1, pg1, pn2, pg2):
        return jnp.concatenate([pn1[grp], pn1[2 + grp], pg1[grp], pn2[grp], pn2[2 + grp], pg2[grp]])

    dmod_lat = jnp.stack([dmod(1, pn1_0, pg1_0, pn2_0, pg2_0), dmod(1, pn1_1, pg1_1, pn2_1, pg2_1)])
    dmod_ctx = jnp.stack([dmod(0, pn1_0, pg1_0, pn2_0, pg2_0), dmod(0, pn1_1, pg1_1, pn2_1, pg2_1)])
    dnw_p = jnp.stack([pn1_0[4], pn2_0[4], pn1_1[4], pn2_1[4]])
    pieces = [dmod_lat, dmod_ctx, dnw_p, pqk[0], pqk[1], dsink_p[8:, 0, 0], d_rpb, dfw_p[0], loss_p[0, 0]]
    sizes = [int(np.prod(p.shape)) for p in pieces]
    rows = [_rows8(s) for s in sizes]
    pack2 = jnp.concatenate([_pad_rows(p, r) for p, r in zip(pieces, rows)], axis=0)
    g2, s2 = ag_small("ag_small_grads", pack2, with_sum=True)
    offs = np.concatenate([[0], np.cumsum(rows)])

    def piece(arr, i, shape):
        return arr[..., offs[i]:offs[i + 1], :].reshape(arr.shape[:-2] + (-1,))[..., :sizes[i]].reshape(
            arr.shape[:-2] + shape)

    dmod_all = piece(g2, 0, (2, 6 * D))
    dmodc_sum = piece(s2, 1, (2, 6 * D))
    dnw_sum = piece(s2, 2, (2, 2, D))
    g_qn = piece(s2, 3, ev_q_norm.shape)
    g_kn = piece(s2, 4, ev_k_norm.shape)
    g_sink = piece(s2, 5, ev_sink.shape)
    g_rpb = piece(s2, 6, od_rpb.shape)
    g_fw = piece(s2, 7, final_norm_w.shape)
    loss = piece(s2, 8, ())

    dm16 = jnp.concatenate([dmod_all.transpose(1, 0, 2), dmodc_sum[:, None, :],
                            jnp.zeros((2, NDEV - 1, 6 * D), F32)], axis=1)
    dm16_loc = lax.dynamic_slice_in_dim(dm16.reshape(2, 2 * NDEV, NDEV, ada_cols), me, 1, axis=2)[:, :, 0, :]
    g_ada_b = _vmem_call("ada_b_grad", lambda v: jnp.sum(v, axis=1),
                         jax.ShapeDtypeStruct((2, 6 * D), F32), dm16)
    g_ada_w = []
    dact_p = None
    for i in range(2):
        dmb = dm16_loc[i].astype(BF16)
        g_ada_w.append(mm_tn(f"ada_w_grad{i}", act, dmb, None))
        part = mm_nt(f"ada_dact{i}", dmb, ada_w[i], _epi_store(F32), F32)
        dact_p = part if dact_p is None else dact_p + part
    _, dact = ag_small("ag_cctx", dact_p, with_sum=True)

    def cctx_grad(da, cc):
        sg = 1.0 / (1.0 + jnp.exp(-cc))
        return da[NDEV:NDEV + 1] * (sg * (1.0 + cc * (1.0 - sg)))

    g_cctx = _vmem_call("cctx_grad", cctx_grad, jax.ShapeDtypeStruct((1, D), F32), dact, c_ctx[None])[0]

    grads = [g_win_0, g_wout_0, g_w1_0, g_w2_0, g_win_1, g_wout_1, g_w1_1, g_w2_1]
    sib = [sib_win_0, sib_wout_0, sib_w1_0, sib_w2_0, sib_win_1, sib_wout_1, sib_w1_1, sib_w2_1]
    rem = [rem_win_0, rem_wout_0, rem_w1_0, rem_w2_0, rem_win_1, rem_wout_1, rem_w1_1, rem_w2_1]
    own_idx = jnp.stack([mc4 + my_chip, my_chip])

    def big(tag, w, m, v, ts):
        res = None
        for l, t in enumerate(ts):
            res = adamw_rs(f"adamw_{tag}_{l}", w, grads[t], sib[t], rem[t], m, v, own_idx, l, res)
        return tuple(res)

    r_ev_w_in = big('ev_w_in', ev_w_in, m_ev_w_in, v_ev_w_in, [0])
    r_ev_w_out = big('ev_w_out', ev_w_out, m_ev_w_out, v_ev_w_out, [1])
    r_mlp_w1 = big('mlp_w1', mlp_w1, m_mlp_w1, v_mlp_w1, [2, 6])
    r_mlp_w2 = big('mlp_w2', mlp_w2, m_mlp_w2, v_mlp_w2, [3, 7])
    r_od_w_in = big('od_w_in', od_w_in, m_od_w_in, v_od_w_in, [4])
    r_od_w_out = big('od_w_out', od_w_out, m_od_w_out, v_od_w_out, [5])

    g_ada = jnp.stack(g_ada_w)
    r_ada_w = adamw_rows("adamw_ada_w", ada_w.reshape(2 * D, ada_cols), g_ada.reshape(2 * D, ada_cols),
                         m_ada_w.reshape(2 * D, ada_cols), v_ada_w.reshape(2 * D, ada_cols))
    r_ada_w = tuple(u.reshape(2, D, ada_cols) for u in r_ada_w)

    g_nw_loc = lax.dynamic_slice_in_dim(dnw_sum, me * nw_cols, nw_cols, axis=2)
    small = [(c_ctx, g_cctx, m_c_ctx, v_c_ctx), (ada_b, g_ada_b, m_ada_b, v_ada_b),
             (norm_w, g_nw_loc, m_norm_w, v_norm_w), (ev_q_norm, g_qn, m_ev_q_norm, v_ev_q_norm),
             (ev_k_norm, g_kn, m_ev_k_norm, v_ev_k_norm), (ev_sink, g_sink, m_ev_sink, v_ev_sink),
             (od_rpb, g_rpb, m_od_rpb, v_od_rpb), (final_norm_w, g_fw, m_final_norm_w, v_final_norm_w)]
    srows = [_rows8(int(np.prod(w.shape))) for w, _, _, _ in small]
    packs = [jnp.concatenate([_pad_rows(tup[k], r) for tup, r in zip(small, srows)], axis=0) for k in range(4)]
    sres = adamw_rows("adamw_small", *packs)
    soffs = np.concatenate([[0], np.cumsum(srows)])

    def unpack(arr, i):
        w = small[i][0]
        return arr[soffs[i]:soffs[i + 1]].reshape(-1)[:int(np.prod(w.shape))].reshape(w.shape)

    sm = [[unpack(sres[k], i) for i in range(len(small))] for k in range(4)]

    def outs(k):
        big_k = {'ada_w': r_ada_w[k], 'mlp_w1': r_mlp_w1[k], 'mlp_w2': r_mlp_w2[k], 'ev_w_in': r_ev_w_in[k],
                 'ev_w_out': r_ev_w_out[k], 'od_w_in': r_od_w_in[k], 'od_w_out': r_od_w_out[k]}
        return (sm[k][0], big_k['ada_w'], sm[k][1], sm[k][2], big_k['mlp_w1'], big_k['mlp_w2'], big_k['ev_w_in'],
                big_k['ev_w_out'], sm[k][3], sm[k][4], sm[k][5], big_k['od_w_in'], big_k['od_w_out'], sm[k][6],
                sm[k][7])

    return (loss, grad_x, *outs(0), *outs(1), *outs(2), *outs(3))
```

```python
import numpy as np
import jax
import jax.numpy as jnp
from jax import lax
from jax.experimental import pallas as pl
from jax.experimental.pallas import tpu as pltpu

F32 = jnp.float32
BF16 = jnp.bfloat16
MESH = pl.DeviceIdType.MESH

NDEV = 8
HEAD = 128
GRID_W = 64
NA_KH, NA_KW = 8, 16
WINDOW = 128
ROPE_THETA = 10000.0
EPS = 1e-6
NEG = -1e30
SCALE = HEAD ** -0.5
ROW_TILE = 256
VMEM_LIMIT = 56 * 1024 * 1024

ADAM_LR, ADAM_B1, ADAM_B2, ADAM_EPS, ADAM_WD, ADAM_STEP = 0.001, 0.9, 0.999, 1e-08, 0.01, 10

NT = (((1,), (1,)), ((), ()))
NN = (((1,), (0,)), ((), ()))
TN = (((0,), (0,)), ((), ()))


def _cparams(sem):
    return pltpu.CompilerParams(dimension_semantics=sem, vmem_limit_bytes=VMEM_LIMIT)


def _tile(n, cap):
    if n <= cap:
        return n
    t = cap - cap % 64
    while t >= 64:
        if n % t == 0:
            return t
        t -= 64
    raise ValueError((n, cap))


def _dot(a, b, dims):
    return lax.dot_general(a.astype(BF16), b.astype(BF16), dims, preferred_element_type=F32)


def _slot(d):
    return (d % 2) * 4 + d // 2


class Exchange:
    def __init__(self, ins, out_shapes, aliases, n_sems, start, finish):
        self.ins, self.out_shapes, self.aliases, self.n_sems = list(ins), list(out_shapes), dict(aliases), n_sems
        self.start, self.finish = start, finish


def merge_exchanges(xs):
    ins, outs, aliases, bases, n = [], [], {}, [], 0
    for x in xs:
        bases.append((len(ins), len(outs), n))
        aliases.update({len(ins) + i: len(outs) + o for i, o in x.aliases.items()})
        ins += x.ins
        outs += x.out_shapes
        n += x.n_sems

    def run(which):
        def f(ci, co, ss, rs, base):
            for x, (i0, o0, s0) in zip(xs, bases):
                getattr(x, which)(ci[i0:i0 + len(x.ins)], co[o0:o0 + len(x.out_shapes)], ss, rs, base + s0)
        return f

    return Exchange(ins, outs, aliases, n, run('start'), run('finish'))


def _call(name, body, grid, ins, in_specs, out_shape, out_specs, scratch, sems, carry=None):
    if not carry:
        return pl.pallas_call(body, grid=grid, in_specs=in_specs, out_specs=out_specs, out_shape=out_shape,
                              scratch_shapes=scratch, compiler_params=_cparams(sems), name=name)(*ins)
    x = merge_exchanges(carry)
    n_in, n_ci, n_out, n_co, n_sc = len(ins), len(x.ins), len(out_shape), len(x.out_shapes), len(scratch)

    def wrapped(*refs):
        p = [0]

        def take(k):
            p[0] += k
            return refs[p[0] - k:p[0]]

        a, ci, o, co, sc = take(n_in), take(n_ci), take(n_out), take(n_co), take(n_sc)
        ss, rs = take(2)
        first = pl.program_id(0) == 0
        last = pl.program_id(0) == grid[0] - 1
        for d in range(1, len(grid)):
            first = jnp.logical_and(first, pl.program_id(d) == 0)
            last = jnp.logical_and(last, pl.program_id(d) == grid[d] - 1)

        @pl.when(first)
        def _():
            x.start(ci, co, ss, rs, 0)

        body(*a, *o, *sc)

        @pl.when(last)
        def _():
            x.finish(ci, co, ss, rs, 0)

    hbm = pl.BlockSpec(memory_space=pl.ANY)
    res = pl.pallas_call(
        wrapped, grid=grid, in_specs=list(in_specs) + [hbm] * n_ci, out_specs=list(out_specs) + [hbm] * n_co,
        out_shape=list(out_shape) + x.out_shapes,
        input_output_aliases={n_in + i: n_out + o for i, o in x.aliases.items()},
        scratch_shapes=list(scratch) + [pltpu.SemaphoreType.DMA((x.n_sems,)), pltpu.SemaphoreType.DMA((x.n_sems,))],
        compiler_params=_cparams(("arbitrary",) * len(grid)), name=name)(*ins, *x.ins)
    return list(res[:n_out]) + [list(res[n_out:])]


def _mm_core(name, grid, ins, in_specs, out_shape, out_specs, dims, acc_shape, epi, carry=None):
    nk = grid[2]
    n_extra = len(ins) - 2

    def body_single(*refs):
        epi(_dot(refs[0][...], refs[1][...], dims), refs[2:2 + n_extra], refs[2 + n_extra:])

    def body(*refs):
        a_ref, b_ref = refs[0], refs[1]
        ex = refs[2:2 + n_extra]
        outs = refs[2 + n_extra:-1]
        acc = refs[-1]
        k = pl.program_id(2)

        @pl.when(k == 0)
        def _():
            acc[...] = _dot(a_ref[...], b_ref[...], dims)

        @pl.when(jnp.logical_and(k > 0, k < nk - 1))
        def _():
            acc[...] += _dot(a_ref[...], b_ref[...], dims)

        @pl.when(k == nk - 1)
        def _():
            epi(acc[...] + _dot(a_ref[...], b_ref[...], dims), ex, outs)

    if nk == 1:
        return _call(name, body_single, grid, ins, in_specs, out_shape, out_specs, [],
                     ("parallel", "parallel", "arbitrary"), carry)
    return _call(name, body, grid, ins, in_specs, out_shape, out_specs, [pltpu.VMEM(acc_shape, F32)],
                 ("parallel", "parallel", "arbitrary"), carry)


def _split(res, n, carry):
    own = res[0] if n == 1 else list(res[:n])
    return (own, res[n]) if carry else own


def _epi_store(dtype):
    def epi(acc, ex, outs):
        outs[0][...] = acc.astype(dtype)
    return epi


def _epi_bias(acc, ex, outs):
    outs[0][...] = acc + ex[0][...]


def _epi_relu2(acc, ex, outs):
    r = jnp.maximum(acc, 0.0)
    outs[0][...] = (r * r).astype(BF16)
    outs[1][...] = r.astype(BF16)


def _epi_mul2r(acc, ex, outs):
    outs[0][...] = (acc * (2.0 * ex[0][...].astype(F32))).astype(BF16)


def _epi_resid_gate(nctx, tm):
    def epi(acc, ex, outs):
        rows = pl.program_id(0) * tm + lax.broadcasted_iota(jnp.int32, (tm, 1), 0)
        g = jnp.where(rows < nctx, ex[1][0:1, :], ex[1][1:2, :])
        outs[0][...] = ex[0][...] + g * acc
        outs[1][...] = acc
    return epi


def mm_nn(name, a, w, epi, outs, extras=(), extra_kinds=(), tm_cap=1100, tn_cap=512, tk_cap=2048, carry=None):
    M, K = a.shape
    if w.ndim == 3:
        ns = w.shape[2]
        N = NDEV * ns
        tn = _tile(ns, tn_cap)
        nper = ns // tn
    else:
        N = w.shape[1]
        tn = _tile(N, tn_cap)
    tm = _tile(M, tm_cap)
    tk = _tile(K, tk_cap)
    grid = (M // tm, N // tn, K // tk)
    a_spec = pl.BlockSpec((tm, tk), lambda i, j, k: (i, k))
    if w.ndim == 3:
        b_spec = pl.BlockSpec((None, tk, tn), lambda i, j, k: (j // nper, k, j % nper))
    else:
        b_spec = pl.BlockSpec((tk, tn), lambda i, j, k: (k, j))
    ex_specs = []
    for e, kind in zip(extras, extra_kinds):
        if kind == 'mn':
            ex_specs.append(pl.BlockSpec((tm, tn), lambda i, j, k: (i, j)))
        else:
            ex_specs.append(pl.BlockSpec((e.shape[0], tn), lambda i, j, k: (0, j)))
    out_shape = [jax.ShapeDtypeStruct((M, N), dt) for dt in outs]
    out_specs = [pl.BlockSpec((tm, tn), lambda i, j, k: (i, j)) for _ in outs]
    res = _mm_core(name, grid, (a, w, *extras), [a_spec, b_spec, *ex_specs], out_shape, out_specs, NN, (tm, tn), epi,
                   carry)
    return (list(res[:len(outs)]), res[len(outs)]) if carry else res


def mm_nt(name, a, w, epi, out_dtype, extras=(), tm_cap=1100, to_cap=1024, tc_cap=2048, carry=None):
    M, N = a.shape
    tm = _tile(M, tm_cap)
    if w.ndim == 3:
        Kw, ns = w.shape[1], w.shape[2]
        tc = _tile(ns, tc_cap)
        cper = ns // tc
    else:
        Kw = w.shape[0]
        tc = _tile(N, tc_cap)
    to = _tile(Kw, to_cap)
    grid = (M // tm, Kw // to, N // tc)
    a_spec = pl.BlockSpec((tm, tc), lambda i, j, k: (i, k))
    if w.ndim == 3:
        b_spec = pl.BlockSpec((None, to, tc), lambda i, j, k: (k // cper, j, k % cper))
    else:
        b_spec = pl.BlockSpec((to, tc), lambda i, j, k: (j, k))
    ex_specs = [pl.BlockSpec((tm, to), lambda i, j, k: (i, j)) for _ in extras]
    out_shape = [jax.ShapeDtypeStruct((M, Kw), out_dtype)]
    out_specs = [pl.BlockSpec((tm, to), lambda i, j, k: (i, j))]
    return _split(_mm_core(name, grid, (a, w, *extras), [a_spec, b_spec, *ex_specs], out_shape, out_specs, NT, (tm, to),
                           epi, carry), 1, carry)


def mm_tn(name, a, b, shard_axis, to_cap=1024, tn_cap=1024, tc_cap=2200, carry=None):
    M, Ka = a.shape
    N = b.shape[1]
    tc = _tile(M, tc_cap)
    if shard_axis is None:
        to, tn = _tile(Ka, to_cap), _tile(N, tn_cap)
        shape = (Ka, N)
        oblk = (to, tn)
        omap = lambda i, j, k: (i, j)
    elif shard_axis == 1:
        ns = N // NDEV
        to, tn = _tile(Ka, to_cap), _tile(ns, tn_cap)
        per = ns // tn
        shape = (NDEV, Ka, ns)
        oblk = (None, to, tn)
        omap = lambda i, j, k: (_slot(j // per), i, j % per)
    else:
        rs = Ka // NDEV
        to, tn = _tile(rs, to_cap), _tile(N, tn_cap)
        per = rs // to
        shape = (NDEV, rs, N)
        oblk = (None, to, tn)
        omap = lambda i, j, k: (_slot(i // per), i % per, j)
    grid = (Ka // to, N // tn, M // tc)
    a_spec = pl.BlockSpec((tc, to), lambda i, j, k: (k, i))
    b_spec = pl.BlockSpec((tc, tn), lambda i, j, k: (k, j))
    out_shape = [jax.ShapeDtypeStruct(shape, F32)]
    out_specs = [pl.BlockSpec(oblk, omap)]
    return _split(_mm_core(name, grid, (a, b), [a_spec, b_spec], out_shape, out_specs, TN, (to, tn), _epi_store(F32),
                           carry), 1, carry)


def _row_spec(D):
    return pl.BlockSpec((ROW_TILE, D), lambda i: (i, 0))


def _const_spec(r, D):
    return pl.BlockSpec((r, D), lambda i: (0, 0))


def _grp(ref, is_ctx):
    return jnp.where(is_ctx, ref[0:1, :], ref[1:2, :])


def norm_mod(name, x, nw, sh, sc, nctx):
    R, D = x.shape
    assert R % ROW_TILE == 0 and nctx % ROW_TILE == 0

    def body(x_ref, nw_ref, sh_ref, sc_ref, o_ref):
        is_ctx = pl.program_id(0) * ROW_TILE < nctx
        xv = x_ref[...]
        rstd = lax.rsqrt(jnp.mean(xv * xv, axis=-1, keepdims=True) + EPS)
        n = xv * rstd * nw_ref[...]
        o_ref[...] = (n * (1.0 + _grp(sc_ref, is_ctx)) + _grp(sh_ref, is_ctx)).astype(BF16)

    return pl.pallas_call(
        body, grid=(R // ROW_TILE,),
        in_specs=[_row_spec(D), _const_spec(1, D), _const_spec(2, D), _const_spec(2, D)],
        out_specs=_row_spec(D), out_shape=jax.ShapeDtypeStruct((R, D), BF16),
        compiler_params=_cparams(("parallel",)), name=name)(x, nw, sh, sc)


def _gate_rows(dxv, y_ref, g_ref, is_ctx, dy_ref, gpart_ref):
    dy_ref[...] = (dxv * _grp(g_ref, is_ctx)).astype(BF16)
    s = jnp.sum(dxv * y_ref[...], axis=0, keepdims=True)
    zero = jnp.zeros_like(s)
    gpart_ref[0:1, :] += jnp.where(is_ctx, s, zero)
    gpart_ref[1:2, :] += jnp.where(is_ctx, zero, s)


def norm_bwd(name, x, dh, dres, nw, sc, nctx, dres_skip=0, out_skip=0, carry=None, gate=None):
    R, D = x.shape
    assert R % ROW_TILE == 0 and nctx % ROW_TILE == 0 and dres_skip % ROW_TILE == 0 and out_skip % ROW_TILE == 0
    res_tiles, out_tiles = dres_skip // ROW_TILE, out_skip // ROW_TILE

    def body(x_ref, dh_ref, dres_ref, nw_ref, sc_ref, *rest):
        if gate is None:
            dx_ref, part_ref = rest
        else:
            y_ref, g_ref, dx_ref, part_ref, dy_ref, gpart_ref = rest
        i = pl.program_id(0)
        is_ctx = i * ROW_TILE < nctx

        @pl.when(i == 0)
        def _():
            part_ref[...] = jnp.zeros_like(part_ref)
            if gate is not None:
                gpart_ref[...] = jnp.zeros_like(gpart_ref)

        xv = x_ref[...]
        dhv = dh_ref[...]
        w = nw_ref[...]
        rstd = lax.rsqrt(jnp.mean(xv * xv, axis=-1, keepdims=True) + EPS)
        xhat = xv * rstd
        n = xhat * w
        dn = dhv * (1.0 + _grp(sc_ref, is_ctx))
        dxhat = dn * w
        dres = dres_ref[...]
        if res_tiles:
            dres = jnp.where(i < res_tiles, 0.0, dres)
        dxv = dres + rstd * (dxhat - xhat * jnp.mean(dxhat * xhat, axis=-1, keepdims=True))
        dx_ref[...] = dxv
        s_sh = jnp.sum(dhv, axis=0, keepdims=True)
        s_sc = jnp.sum(dhv * n, axis=0, keepdims=True)
        s_nw = jnp.sum(dn * xhat, axis=0, keepdims=True)
        zero = jnp.zeros_like(s_sh)
        part_ref[0:1, :] += jnp.where(is_ctx, s_sh, zero)
        part_ref[1:2, :] += jnp.where(is_ctx, zero, s_sh)
        part_ref[2:3, :] += jnp.where(is_ctx, s_sc, zero)
        part_ref[3:4, :] += jnp.where(is_ctx, zero, s_sc)
        part_ref[4:5, :] += s_nw
        if gate is not None:
            _gate_rows(dxv, y_ref, g_ref, is_ctx, dy_ref, gpart_ref)

    ins = [x, dh, dres, nw, sc]
    in_specs = [_row_spec(D), _row_spec(D), pl.BlockSpec((ROW_TILE, D), lambda i: (jnp.maximum(i - res_tiles, 0), 0)),
                _const_spec(1, D), _const_spec(2, D)]
    out_shape = [jax.ShapeDtypeStruct((R - out_skip, D), F32), jax.ShapeDtypeStruct((8, D), F32)]
    out_specs = [pl.BlockSpec((ROW_TILE, D), lambda i: (jnp.maximum(i - out_tiles, 0), 0)), _const_spec(8, D)]
    if gate is not None:
        assert out_skip == 0
        ins += list(gate)
        in_specs += [_row_spec(D), _const_spec(2, D)]
        out_shape += [jax.ShapeDtypeStruct((R, D), BF16), jax.ShapeDtypeStruct((8, D), F32)]
        out_specs += [_row_spec(D), _const_spec(8, D)]
    res = _call(name, body, (R // ROW_TILE,), ins, in_specs, out_shape, out_specs, [], ("arbitrary",), carry)
    return _split(res, len(out_shape), carry)


def final_loss(name, x, fw, tgt, y, g):
    S, D = x.shape

    def body(x_ref, fw_ref, t_ref, y_ref, g_ref, dx_ref, loss_ref, dfw_ref, dy_ref, gpart_ref):
        i = pl.program_id(0)

        @pl.when(i == 0)
        def _():
            loss_ref[...] = jnp.zeros_like(loss_ref)
            dfw_ref[...] = jnp.zeros_like(dfw_ref)
            gpart_ref[...] = jnp.zeros_like(gpart_ref)

        xv = x_ref[...]
        w = fw_ref[...]
        rstd = lax.rsqrt(jnp.mean(xv * xv, axis=-1, keepdims=True) + EPS)
        xhat = xv * rstd
        e = xhat * w - t_ref[...]
        loss_ref[...] += 0.5 * jnp.sum(jnp.mean(e * e, axis=-1, keepdims=True))
        dout = e * (1.0 / D)
        dfw_ref[0:1, :] += jnp.sum(dout * xhat, axis=0, keepdims=True)
        dxhat = dout * w
        dxv = rstd * (dxhat - xhat * jnp.mean(dxhat * xhat, axis=-1, keepdims=True))
        dx_ref[...] = dxv
        _gate_rows(dxv, y_ref, g_ref, False, dy_ref, gpart_ref)

    return pl.pallas_call(
        body, grid=(S // ROW_TILE,),
        in_specs=[_row_spec(D), _const_spec(1, D), _row_spec(D), _row_spec(D), _const_spec(2, D)],
        out_specs=[_row_spec(D), pl.BlockSpec((8, 128), lambda i: (0, 0)), _const_spec(8, D), _row_spec(D),
                   _const_spec(8, D)],
        out_shape=[jax.ShapeDtypeStruct((S, D), F32), jax.ShapeDtypeStruct((8, 128), F32),
                   jax.ShapeDtypeStruct((8, D), F32), jax.ShapeDtypeStruct((S, D), BF16),
                   jax.ShapeDtypeStruct((8, D), F32)],
        compiler_params=_cparams(("arbitrary",)), name=name)(x, fw, tgt, y, g)


def _rope(x, cos, sa, sb):
    return x * cos + pltpu.roll(x, 96, 1) * sa + pltpu.roll(x, 32, 1) * sb


def _rope_t(dy, cos, sa, sb):
    return dy * cos + pltpu.roll(dy * sa, 32, 1) + pltpu.roll(dy * sb, 96, 1)


_EVEN_KINDS = ['qa'] * 8 + ['ka'] * 2 + ['v'] * 2 + ['qb'] * 8 + ['kb'] * 2 + ['v'] * 2
_EVEN_DSRC = ([('q', j) for j in range(8)] + [('k', 0), ('k', 1), ('v', 0), ('v', 1)]
              + [('q', 8 + j) for j in range(8)] + [('k', 2), ('k', 3), ('v', 2), ('v', 3)])


def _cols(j):
    return slice(j * HEAD, (j + 1) * HEAD)


def prep_even(name, qkv, qn, kn, cos, sa, sb):
    T, W = qkv.shape

    def body(x_ref, qn_ref, kn_ref, cos_ref, sa_ref, sb_ref, o_ref):
        cos_, sa_, sb_ = cos_ref[...], sa_ref[...], sb_ref[...]
        for j, kind in enumerate(_EVEN_KINDS):
            x = x_ref[:, _cols(j)]
            if kind in ('qa', 'ka'):
                rstd = lax.rsqrt(jnp.mean(x * x, axis=-1, keepdims=True) + EPS)
                x = x * rstd * (qn_ref[...] if kind == 'qa' else kn_ref[...])
            if kind != 'v':
                x = _rope(x, cos_, sa_, sb_)
            o_ref[:, _cols(j)] = x.astype(BF16)

    blk = pl.BlockSpec((ROW_TILE, W), lambda i: (i, 0))
    tab = pl.BlockSpec((ROW_TILE, HEAD), lambda i: (i, 0))
    one = pl.BlockSpec((1, HEAD), lambda i: (0, 0))
    return pl.pallas_call(
        body, grid=(T // ROW_TILE,), in_specs=[blk, one, one, tab, tab, tab], out_specs=blk,
        out_shape=jax.ShapeDtypeStruct(qkv.shape, BF16),
        compiler_params=_cparams(("parallel",)), name=name)(qkv, qn, kn, cos, sa, sb)


def prep_even_bwd(name, qkv, dq, dk, dv, qn, kn, cos, sa, sb):
    T, W = qkv.shape

    def body(x_ref, dq_ref, dk_ref, dv_ref, qn_ref, kn_ref, cos_ref, sa_ref, sb_ref, o_ref, part_ref):
        @pl.when(pl.program_id(0) == 0)
        def _():
            part_ref[...] = jnp.zeros_like(part_ref)

        cos_, sa_, sb_ = cos_ref[...], sa_ref[...], sb_ref[...]
        src = {'q': dq_ref, 'k': dk_ref, 'v': dv_ref}
        sums = {'qa': None, 'ka': None}
        for j, kind in enumerate(_EVEN_KINDS):
            which, blk_j = _EVEN_DSRC[j]
            d = src[which][:, _cols(blk_j)]
            if kind != 'v':
                d = _rope_t(d, cos_, sa_, sb_)
            if kind in ('qa', 'ka'):
                x = x_ref[:, _cols(j)]
                rstd = lax.rsqrt(jnp.mean(x * x, axis=-1, keepdims=True) + EPS)
                xhat = x * rstd
                s = jnp.sum(d * xhat, axis=0, keepdims=True)
                sums[kind] = s if sums[kind] is None else sums[kind] + s
                dxhat = d * (qn_ref[...] if kind == 'qa' else kn_ref[...])
                d = rstd * (dxhat - xhat * jnp.mean(dxhat * xhat, axis=-1, keepdims=True))
            o_ref[:, _cols(j)] = d.astype(BF16)
        part_ref[0:1, :] += sums['qa']
        part_ref[1:2, :] += sums['ka']

    def rows(w):
        return pl.BlockSpec((ROW_TILE, w), lambda i: (i, 0))

    one = pl.BlockSpec((1, HEAD), lambda i: (0, 0))
    return pl.pallas_call(
        body, grid=(T // ROW_TILE,),
        in_specs=[rows(W), rows(dq.shape[1]), rows(dk.shape[1]), rows(dv.shape[1]), one, one,
                  rows(HEAD), rows(HEAD), rows(HEAD)],
        out_specs=[rows(W), pl.BlockSpec((8, HEAD), lambda i: (0, 0))],
        out_shape=[jax.ShapeDtypeStruct(qkv.shape, BF16), jax.ShapeDtypeStruct((8, HEAD), F32)],
        compiler_params=_cparams(("arbitrary",)), name=name)(qkv, dq, dk, dv, qn, kn, cos, sa, sb)


def _even_maps():
    qmap = lambda h, qb: (qb, jnp.where(h < 8, h, h + 4))
    kmap = lambda h, qb: (0, jnp.where(h < 8, 8 + h // 4, 18 + h // 4))
    vmap = lambda h, qb: (0, jnp.where(h < 8, 10 + h // 4, 20 + h // 4))
    return qmap, kmap, vmap


def _softmax_parts(parts, extra=None, stats=None):
    if stats is None:
        m = parts[0].max(axis=-1, keepdims=True)
        for p in parts[1:]:
            m = jnp.maximum(m, p.max(axis=-1, keepdims=True))
        if extra is not None:
            m = jnp.maximum(m, extra)
    else:
        m = stats[0]
    es = [jnp.exp(p - m) for p in parts]
    ex = None if extra is None else jnp.exp(extra - m)
    if stats is None:
        l = es[0].sum(axis=-1, keepdims=True)
        for e in es[1:]:
            l = l + e.sum(axis=-1, keepdims=True)
        if extra is not None:
            l = l + ex
        inv = 1.0 / l
    else:
        inv = stats[1]
    return [e * inv for e in es], (None if ex is None else ex * inv), (m, inv)


def _win_scores(q, k_ref, qb, tq, nctx, S):
    L = tq + 2 * WINDOW
    nqc = nctx // tq
    qlat = (qb - nqc) * tq
    start = pl.multiple_of(jnp.clip(qlat - WINDOW, 0, S - L), 128)
    kc = k_ref[0:nctx, :]
    kw = k_ref[pl.ds(nctx + start, L), :]
    s_c = _dot(q, kc, NT) * SCALE
    s_w = _dot(q, kw, NT) * SCALE
    qpos = qlat + lax.broadcasted_iota(jnp.int32, (tq, 1), 0)
    kpos = start + lax.broadcasted_iota(jnp.int32, (1, L), 1)
    valid = jnp.logical_and(jnp.abs(kpos - qpos) <= WINDOW, qb >= nqc)
    return s_c, jnp.where(valid, s_w, NEG), start, L


def _softmax_raw(raw, stats=None):
    m = raw.max(axis=-1, keepdims=True) if stats is None else stats[0]
    e = jnp.exp2((raw - m) * (SCALE * np.log2(np.e)))
    inv = 1.0 / e.sum(axis=-1, keepdims=True) if stats is None else stats[1]
    return e * inv, (m, inv)


def _glob_keys(qb, tq, nctx, T):
    is_ctx = qb < nctx // tq
    return [(is_ctx, slice(0, nctx)), (jnp.logical_not(is_ctx), slice(0, T))]


def _stats_spec(tq):
    return pl.BlockSpec((None, None, tq, 2), lambda h, qb: (h, qb, 0, 0))


def attn_even_fwd(name, qkvh, sink, nctx, tq, carry=None):
    T = qkvh.shape[0]
    S = T - nctx
    qmap, kmap, vmap = _even_maps()

    def body(sink_ref, q_ref, k_ref, v_ref, o_ref, st_ref):
        h, qb = pl.program_id(0), pl.program_id(1)
        q = q_ref[...]

        for pred, keys in _glob_keys(qb, tq, nctx, T):
            @pl.when(jnp.logical_and(h < 8, pred))
            def _():
                p, (m, inv) = _softmax_raw(_dot(q, k_ref[keys, :], NT))
                st_ref[:, 0:1] = m
                st_ref[:, 1:2] = inv
                o_ref[...] = _dot(p, v_ref[keys, :], NN).astype(BF16)

        @pl.when(h >= 8)
        def _():
            s_c, s_w, start, L = _win_scores(q, k_ref, qb, tq, nctx, S)
            sk = jnp.full((tq, 1), sink_ref[jnp.maximum(h - 8, 0)], F32)
            (p_c, p_w), _, (m, inv) = _softmax_parts([s_c, s_w], sk)
            st_ref[:, 0:1] = m
            st_ref[:, 1:2] = inv
            o = _dot(p_c, v_ref[0:nctx, :], NN) + _dot(p_w, v_ref[pl.ds(nctx + start, L), :], NN)
            o_ref[...] = o.astype(BF16)

    res = _call(name, body, (16, T // tq), (sink, qkvh, qkvh, qkvh),
                [pl.BlockSpec(memory_space=pltpu.SMEM), pl.BlockSpec((tq, HEAD), qmap),
                 pl.BlockSpec((T, HEAD), kmap), pl.BlockSpec((T, HEAD), vmap)],
                [jax.ShapeDtypeStruct((T, 16 * HEAD), BF16), jax.ShapeDtypeStruct((16, T // tq, tq, 2), F32)],
                [pl.BlockSpec((tq, HEAD), lambda h, qb: (qb, h)), _stats_spec(tq)],
                [], ("parallel", "arbitrary"), carry)
    return _split(res, 2, carry)


def attn_even_bwd(name, qkvh, sink, do, stats, nctx, tq, carry=None):
    T = qkvh.shape[0]
    assert stats.shape == (16, T // tq, tq, 2)
    S = T - nctx
    qmap, kmap, vmap = _even_maps()

    def body(sink_ref, q_ref, k_ref, v_ref, do_ref, st_ref, dq_ref, dk_ref, dv_ref, ds_ref):
        h, qb = pl.program_id(0), pl.program_id(1)
        q = q_ref[...]
        dov = do_ref[...]

        @pl.when(jnp.logical_and(h % 4 == 0, qb == 0))
        def _():
            dk_ref[...] = jnp.zeros_like(dk_ref)
            dv_ref[...] = jnp.zeros_like(dv_ref)

        @pl.when(qb == 0)
        def _():
            ds_ref[...] = jnp.zeros_like(ds_ref)

        for pred, keys in _glob_keys(qb, tq, nctx, T):
            @pl.when(jnp.logical_and(h < 8, pred))
            def _():
                p, _ = _softmax_raw(_dot(q, k_ref[keys, :], NT), (st_ref[:, 0:1], st_ref[:, 1:2]))
                dp = _dot(dov, v_ref[keys, :], NT)
                row = jnp.sum(p * dp, axis=-1, keepdims=True)
                dsb = (p * (dp - row) * SCALE).astype(BF16)
                dq_ref[...] = _dot(dsb, k_ref[keys, :], NN)
                dk_ref[keys, :] += _dot(dsb, q, TN)
                dv_ref[keys, :] += _dot(p, dov, TN)

        @pl.when(h >= 8)
        def _():
            s_c, s_w, start, L = _win_scores(q, k_ref, qb, tq, nctx, S)
            sk = jnp.full((tq, 1), sink_ref[jnp.maximum(h - 8, 0)], F32)
            (p_c, p_w), p_s, _ = _softmax_parts([s_c, s_w], sk, (st_ref[:, 0:1], st_ref[:, 1:2]))
            win = pl.ds(nctx + start, L)
            dp_c = _dot(dov, v_ref[0:nctx, :], NT)
            dp_w = _dot(dov, v_ref[win, :], NT)
            row = jnp.sum(p_c * dp_c, axis=-1, keepdims=True) + jnp.sum(p_w * dp_w, axis=-1, keepdims=True)
            ds_c = (p_c * (dp_c - row) * SCALE).astype(BF16)
            ds_w = (p_w * (dp_w - row) * SCALE).astype(BF16)
            dq_ref[...] = _dot(ds_c, k_ref[0:nctx, :], NN) + _dot(ds_w, k_ref[win, :], NN)
            dk_ref[0:nctx, :] += _dot(ds_c, q, TN)
            dk_ref[win, :] += _dot(ds_w, q, TN)
            dv_ref[0:nctx, :] += _dot(p_c, dov, TN)
            dv_ref[win, :] += _dot(p_w, dov, TN)
            ds_ref[...] += jnp.sum(-(p_s * row))

    kv_out = pl.BlockSpec((T, HEAD), lambda h, qb: (0, h // 4))
    res = _call(name, body, (16, T // tq), (sink, qkvh, qkvh, qkvh, do, stats),
                [pl.BlockSpec(memory_space=pltpu.SMEM), pl.BlockSpec((tq, HEAD), qmap),
                 pl.BlockSpec((T, HEAD), kmap), pl.BlockSpec((T, HEAD), vmap),
                 pl.BlockSpec((tq, HEAD), lambda h, qb: (qb, h)), _stats_spec(tq)],
                [jax.ShapeDtypeStruct((T, 16 * HEAD), F32), jax.ShapeDtypeStruct((T, 4 * HEAD), F32),
                 jax.ShapeDtypeStruct((T, 4 * HEAD), F32), jax.ShapeDtypeStruct((16, 8, 128), F32)],
                [pl.BlockSpec((tq, HEAD), lambda h, qb: (qb, h)), kv_out, kv_out,
                 pl.BlockSpec((None, 8, 128), lambda h, qb: (h, 0, 0))],
                [], ("arbitrary", "arbitrary"), carry)
    return _split(res, 4, carry)


NA_GROUP = 4
NA_SPAN = NA_KH + NA_GROUP - 1
_NA_PLAN = [[(j, 0) for j in range(NA_GROUP)],
            [(NA_KH // 2, j) for j in range(NA_GROUP)],
            [(NA_KH // 2 + j, NA_GROUP - 1) for j in range(NA_GROUP)]]


def _na_group(g, n_groups, rows):
    last = g == n_groups - 1
    kind = jnp.where(g == 0, 0, jnp.where(last, 2, 1))
    first_row = jnp.where(g == 0, 0, jnp.where(last, rows - NA_SPAN, NA_GROUP * g - NA_KH // 2))
    return kind, first_row


def na_span_bias(bias8):
    LW, LS = NA_KH * GRID_W, NA_SPAN * GRID_W
    kinds = []
    for plan in _NA_PLAN:
        strips = [jnp.pad(bias8[:, off], ((0, 0), (0, 0), (s * GRID_W, LS - LW - s * GRID_W)), constant_values=NEG)
                  for off, s in plan]
        kinds.append(jnp.concatenate(strips, axis=1))
    return jnp.stack(kinds, axis=1)


def na_span_bias_grad(db):
    LW = NA_KH * GRID_W
    out = [None] * NA_KH
    for kind, plan in enumerate(_NA_PLAN):
        for j, (off, s) in enumerate(plan):
            piece = db[:, kind, j * GRID_W:(j + 1) * GRID_W, s * GRID_W:s * GRID_W + LW]
            out[off] = piece if out[off] is None else out[off] + piece
    return jnp.stack(out, axis=1)


def _na_specs(T, nctx, n_groups, rows):
    LS = NA_SPAN * GRID_W
    tq = NA_GROUP * GRID_W
    assert nctx % tq == 0 and n_groups >= 3
    q_spec = pl.BlockSpec((tq, HEAD), lambda h, g: (g + nctx // tq, h))
    k_spec = pl.BlockSpec((T, HEAD), lambda h, g: (0, 16 + h))
    v_spec = pl.BlockSpec((T, HEAD), lambda h, g: (0, 32 + h))
    b_spec = pl.BlockSpec((None, None, tq, LS), lambda h, g: (h, _na_group(g, n_groups, rows)[0], 0, 0))
    row_spec = pl.BlockSpec((tq, HEAD), lambda h, g: (g, h))
    return q_spec, k_spec, v_spec, b_spec, row_spec


def _na_scores(q, k_ref, b_ref, g, n_groups, rows, nctx):
    first_row = _na_group(g, n_groups, rows)[1]
    win = pl.ds(pl.multiple_of(nctx + first_row * GRID_W, GRID_W), NA_SPAN * GRID_W)
    s_c = _dot(q, k_ref[0:nctx, :], NT) * SCALE
    s_w = _dot(q, k_ref[win, :], NT) * SCALE + b_ref[...]
    return s_c, s_w, win


def attn_odd_fwd(name, qkv, bias_s, nctx, carry=None):
    T = qkv.shape[0]
    S = T - nctx
    rows = S // GRID_W
    n_groups = rows // NA_GROUP
    q_spec, k_spec, v_spec, b_spec, row_spec = _na_specs(T, nctx, n_groups, rows)

    def body(q_ref, k_ref, v_ref, b_ref, o_ref, st_ref):
        s_c, s_w, win = _na_scores(q_ref[...], k_ref, b_ref, pl.program_id(1), n_groups, rows, nctx)
        (p_c, p_w), _, (m, inv) = _softmax_parts([s_c, s_w])
        st_ref[:, 0:1] = m
        st_ref[:, 1:2] = inv
        o_ref[...] = (_dot(p_c, v_ref[0:nctx, :], NN) + _dot(p_w, v_ref[win, :], NN)).astype(BF16)

    tq = NA_GROUP * GRID_W
    res = _call(name, body, (16, n_groups), (qkv, qkv, qkv, bias_s), [q_spec, k_spec, v_spec, b_spec],
                [jax.ShapeDtypeStruct((S, 16 * HEAD), BF16), jax.ShapeDtypeStruct((16, n_groups, tq, 2), F32)],
                [row_spec, _stats_spec(tq)], [], ("parallel", "arbitrary"), carry)
    return _split(res, 2, carry)


def attn_odd_bwd(name, qkv, bias_s, do, stats, nctx, carry=None):
    T = qkv.shape[0]
    S = T - nctx
    rows = S // GRID_W
    n_groups = rows // NA_GROUP
    q_spec, k_spec, v_spec, b_spec, row_spec = _na_specs(T, nctx, n_groups, rows)

    def body(q_ref, k_ref, v_ref, b_ref, do_ref, st_ref, dq_ref, dk_ref, dv_ref, db_ref):
        g = pl.program_id(1)
        q = q_ref[...]
        dov = do_ref[...]

        @pl.when(g == 0)
        def _():
            dk_ref[...] = jnp.zeros_like(dk_ref)
            dv_ref[...] = jnp.zeros_like(dv_ref)

        s_c, s_w, win = _na_scores(q, k_ref, b_ref, g, n_groups, rows, nctx)
        (p_c, p_w), _, _ = _softmax_parts([s_c, s_w], None, (st_ref[:, 0:1], st_ref[:, 1:2]))
        dp_c = _dot(dov, v_ref[0:nctx, :], NT)
        dp_w = _dot(dov, v_ref[win, :], NT)
        row = jnp.sum(p_c * dp_c, axis=-1, keepdims=True) + jnp.sum(p_w * dp_w, axis=-1, keepdims=True)
        dsw = p_w * (dp_w - row)
        first_visit = jnp.logical_or(g <= 1, g == n_groups - 1)

        @pl.when(first_visit)
        def _():
            db_ref[...] = dsw

        @pl.when(jnp.logical_not(first_visit))
        def _():
            db_ref[...] += dsw

        ds_c = (p_c * (dp_c - row) * SCALE).astype(BF16)
        ds_w = (dsw * SCALE).astype(BF16)
        dq_ref[...] = _dot(ds_c, k_ref[0:nctx, :], NN) + _dot(ds_w, k_ref[win, :], NN)
        dk_ref[0:nctx, :] += _dot(ds_c, q, TN)
        dk_ref[win, :] += _dot(ds_w, q, TN)
        dv_ref[0:nctx, :] += _dot(p_c, dov, TN)
        dv_ref[win, :] += _dot(p_w, dov, TN)

    kv_out = pl.BlockSpec((T, HEAD), lambda h, g: (0, h))
    res = _call(name, body, (16, n_groups), (qkv, qkv, qkv, bias_s, do, stats),
                [q_spec, k_spec, v_spec, b_spec, row_spec, _stats_spec(NA_GROUP * GRID_W)],
                [jax.ShapeDtypeStruct((S, 16 * HEAD), F32), jax.ShapeDtypeStruct((T, 16 * HEAD), F32),
                 jax.ShapeDtypeStruct((T, 16 * HEAD), F32), jax.ShapeDtypeStruct(bias_s.shape, F32)],
                [row_spec, kv_out, kv_out, b_spec], [], ("arbitrary", "arbitrary"), carry)
    return _split(res, 4, carry)


def _na_onehots():
    o = np.arange(NA_KH)[:, None]
    i = np.arange(NA_KH)[None, :]
    a = i - o + NA_KH - 1
    A = (a[..., None] == np.arange(2 * NA_KH - 1)).astype(np.float32)
    qc = np.arange(GRID_W)[:, None]
    kc = np.arange(GRID_W)[None, :]
    b = np.clip(kc - qc + NA_KW - 1, 0, 2 * NA_KW - 2)
    cs = np.clip(qc - NA_KW // 2, 0, GRID_W - NA_KW)
    valid = (kc >= cs) & (kc < cs + NA_KW)
    B = ((b[..., None] == np.arange(2 * NA_KW - 1)) & valid[..., None]).astype(np.float32)
    return A, B, valid


def na_bias_table(rpb):
    A, B, valid = _na_onehots()
    hp = lax.Precision.HIGHEST
    t = jnp.einsum('hab,oia->hoib', rpb, jnp.asarray(A), precision=hp)
    bias = jnp.einsum('hoib,qkb->hoqik', t, jnp.asarray(B), precision=hp)
    bias = jnp.where(jnp.asarray(valid)[None, None, :, None, :], bias, NEG)
    return bias.reshape(rpb.shape[0], NA_KH, GRID_W, NA_KH * GRID_W)


def na_bias_grad(name, dbias8):
    A, B, _ = _na_onehots()
    H = dbias8.shape[0]
    nb, na = 2 * NA_KW - 1, 2 * NA_KH - 1
    d = dbias8.reshape(H, NA_KH, GRID_W, NA_KH, GRID_W).transpose(0, 1, 3, 2, 4)
    d = d.reshape(H * NA_KH * NA_KH, GRID_W * GRID_W)
    Bp = np.zeros((GRID_W * GRID_W, 128), np.float32)
    Bp[:, :nb] = B.reshape(GRID_W * GRID_W, nb)
    Ap = np.zeros((16, NA_KH * NA_KH), np.float32)
    Ap[:na] = A.reshape(NA_KH * NA_KH, na).T
    rows_per_head = NA_KH * NA_KH

    def split3(x):
        hi = x.astype(BF16)
        r1 = x - hi.astype(F32)
        mid = r1.astype(BF16)
        return hi, mid, (r1 - mid.astype(F32)).astype(BF16)

    def body(d_ref, b_ref, a_ref, o_ref):
        bm, am = b_ref[...], a_ref[...]
        g = sum(lax.dot_general(p, bm, NN, preferred_element_type=F32) for p in split3(d_ref[...]))
        o_ref[...] = sum(lax.dot_general(am, p, NN, preferred_element_type=F32) for p in split3(g))

    out = pl.pallas_call(
        body, grid=(H,),
        in_specs=[pl.BlockSpec((rows_per_head, GRID_W * GRID_W), lambda h: (h, 0)),
                  pl.BlockSpec((GRID_W * GRID_W, 128), lambda h: (0, 0)),
                  pl.BlockSpec((16, rows_per_head), lambda h: (0, 0))],
        out_specs=pl.BlockSpec((None, 16, 128), lambda h: (h, 0, 0)),
        out_shape=jax.ShapeDtypeStruct((H, 16, 128), F32),
        compiler_params=_cparams(("parallel",)), name=name)(d, jnp.asarray(Bp, BF16), jnp.asarray(Ap, BF16))
    return out[:, :na, :nb]


def _vmem_call(name, fn, out_shape, *arrays):
    def body(*refs):
        n = len(arrays)
        res = fn(*[r[...] for r in refs[:n]])
        if not isinstance(res, (tuple, list)):
            res = (res,)
        for o, v in zip(refs[n:], res):
            o[...] = v
    return pl.pallas_call(body, out_shape=out_shape, name=name,
                          compiler_params=pltpu.CompilerParams(vmem_limit_bytes=VMEM_LIMIT))(*arrays)


def _silu(v):
    return v / (1.0 + jnp.exp(-v))


def _adamw_math(w, g, m, v):
    m2 = ADAM_B1 * m + (1.0 - ADAM_B1) * g
    v2 = ADAM_B2 * v + (1.0 - ADAM_B2) * (g * g)
    m_hat = m2 / (1.0 - ADAM_B1 ** ADAM_STEP)
    v_hat = v2 / (1.0 - ADAM_B2 ** ADAM_STEP)
    delta = -ADAM_LR * (m_hat / (jnp.sqrt(v_hat) + ADAM_EPS) + ADAM_WD * w)
    return delta, m2, v2


def _ew_tile(R, C):
    return _tile(R, max(64, (262144 // C) // 64 * 64))


def adamw_rows(name, w, g, m, v, extra_g=None):
    R, C = w.shape
    tr = _ew_tile(R, C)
    extra_g = list(extra_g or [])
    ne = len(extra_g)

    def body(*refs):
        w_ref, g_ref, m_ref, v_ref = refs[:4]
        gs = g_ref[...]
        for e in refs[4:4 + ne]:
            gs = gs + e[...].astype(F32)
        go, do, mo, vo = refs[4 + ne:]
        d, m2, v2 = _adamw_math(w_ref[...], gs, m_ref[...], v_ref[...])
        go[...] = gs
        do[...] = d
        mo[...] = m2
        vo[...] = v2

    spec = pl.BlockSpec((tr, C), lambda i: (i, 0))
    return pl.pallas_call(
        body, grid=(R // tr,), in_specs=[spec] * (4 + ne), out_specs=[spec] * 4,
        out_shape=[jax.ShapeDtypeStruct((R, C), F32)] * 4,
        compiler_params=_cparams(("parallel",)), name=name)(w, g, m, v, *extra_g)


def rs_chip_sum(name, g8, sib4, where):
    _, R, C = g8.shape
    tr = _ew_tile(R, C)

    def body(s_ref, g_ref, b_ref, o_ref):
        o_ref[...] = (g_ref[...] + b_ref[...]).astype(BF16)

    def chip(q, s):
        return (s[1] + 1 + q) % 4

    blk = (None, tr, C)
    grid_spec = pltpu.PrefetchScalarGridSpec(
        num_scalar_prefetch=1, grid=(3, R // tr),
        in_specs=[pl.BlockSpec(blk, lambda q, i, s: (s[0] + chip(q, s), i, 0)),
                  pl.BlockSpec(blk, lambda q, i, s: (chip(q, s), i, 0))],
        out_specs=pl.BlockSpec(blk, lambda q, i, s: (chip(q, s), i, 0)))
    return pl.pallas_call(body, grid_spec=grid_spec, out_shape=jax.ShapeDtypeStruct((4, R, C), BF16),
                          compiler_params=_cparams(("parallel", "parallel")), name=name)(where, g8, sib4)


def adamw_rs(name, w, g8, sib4, rem3, m, v, idx, layer, prev=None):
    L, R, C = w.shape
    tr = _ew_tile(R, C)

    def body(s_ref, w_ref, g_ref, sb_ref, r0_ref, r1_ref, r2_ref, m_ref, v_ref, *rest):
        go, do, mo, vo = rest[-4:]
        gs = g_ref[...] + sb_ref[...]
        for r_ref in (r0_ref, r1_ref, r2_ref):
            gs = gs + r_ref[...].astype(F32)
        d, m2, v2 = _adamw_math(w_ref[...], gs, m_ref[...], v_ref[...])
        go[...] = gs
        do[...] = d
        mo[...] = m2
        vo[...] = v2

    blk = (None, tr, C)
    mine = pl.BlockSpec(blk, lambda i, s: (layer, i, 0))

    def rem(k):
        return pl.BlockSpec(blk, lambda i, s: (k, i, 0))

    prev = list(prev or [])
    grid_spec = pltpu.PrefetchScalarGridSpec(
        num_scalar_prefetch=1, grid=(R // tr,),
        in_specs=[mine, pl.BlockSpec(blk, lambda i, s: (s[0], i, 0)), pl.BlockSpec(blk, lambda i, s: (s[1], i, 0)),
                  rem(0), rem(1), rem(2), mine, mine] + [pl.BlockSpec(memory_space=pl.ANY)] * len(prev),
        out_specs=[mine] * 4)
    return pl.pallas_call(body, grid_spec=grid_spec, out_shape=[jax.ShapeDtypeStruct((L, R, C), F32)] * 4,
                          input_output_aliases={9 + k: k for k in range(len(prev))},
                          compiler_params=_cparams(("parallel",)), name=name)(
                              idx, w, g8, sib4, rem3, rem3, rem3, m, v, *prev)


def _me():
    x, y, c = lax.axis_index("x"), lax.axis_index("y"), lax.axis_index("c")
    return x, y, c


def _flip(v, bit):
    return 1 - v if bit else v


def ag_small(name, x, with_sum=False):
    R, C = x.shape

    def body(x_ref, out_ref, *rest):
        if with_sum:
            sum_ref, send_sems, recv_sems, lsem = rest
        else:
            send_sems, recv_sems, lsem = rest
        mx, my, mc = _me()
        me = 4 * mx + 2 * my + mc
        local = pltpu.make_async_copy(x_ref, out_ref.at[me], lsem)
        local.start()
        sends = []
        for k in range(1, NDEV):
            peer = (_flip(mx, k & 4), _flip(my, k & 2), _flip(mc, k & 1))
            cp = pltpu.make_async_remote_copy(src_ref=x_ref, dst_ref=out_ref.at[me], send_sem=send_sems.at[k - 1],
                                              recv_sem=recv_sems.at[k - 1], device_id=peer, device_id_type=MESH)
            cp.start()
            sends.append(cp)
        for k in range(1, NDEV):
            px, py, pc = _flip(mx, k & 4), _flip(my, k & 2), _flip(mc, k & 1)
            pltpu.make_async_remote_copy(src_ref=x_ref, dst_ref=out_ref.at[4 * px + 2 * py + pc],
                                         send_sem=send_sems.at[k - 1], recv_sem=recv_sems.at[k - 1],
                                         device_id=(px, py, pc), device_id_type=MESH).wait_recv()
        for cp in sends:
            cp.wait_send()
        local.wait()
        if with_sum:
            acc = out_ref[0]
            for d in range(1, NDEV):
                acc = acc + out_ref[d]
            sum_ref[...] = acc

    out_shape = [jax.ShapeDtypeStruct((NDEV, R, C), F32)]
    if with_sum:
        out_shape.append(jax.ShapeDtypeStruct((R, C), F32))
    vm = pl.BlockSpec(memory_space=pltpu.VMEM)
    res = pl.pallas_call(
        body, out_shape=out_shape, in_specs=[vm], out_specs=[vm] * len(out_shape),
        scratch_shapes=[pltpu.SemaphoreType.DMA((NDEV - 1,)), pltpu.SemaphoreType.DMA((NDEV - 1,)),
                        pltpu.SemaphoreType.DMA],
        compiler_params=pltpu.CompilerParams(vmem_limit_bytes=VMEM_LIMIT), name=name)(x)
    return res if with_sum else res[0]


def ag_big(name, shards):
    n = len(shards)

    def body(*refs):
        ins, outs = refs[:n], refs[n:2 * n]
        send_sems, recv_sems, lsems = refs[2 * n:]
        mx, my, mc = _me()
        me = (mx, my, mc)
        sibling = (mx, my, 1 - mc)
        chips = [(1 - mx, my), (mx, 1 - my), (1 - mx, 1 - my)]

        def idx(p):
            return 4 * p[0] + 2 * p[1] + p[2]

        def copy(t, k, block, to, src=None):
            dst = outs[t].at[idx(block)]
            return pltpu.make_async_remote_copy(
                src_ref=dst if src is None else src, dst_ref=dst, send_sem=send_sems.at[7 * t + k],
                recv_sem=recv_sems.at[7 * t + k], device_id=to, device_id_type=MESH)

        started = []
        locals_ = []
        for t in range(n):
            mine = pltpu.make_async_copy(ins[t], outs[t].at[idx(me)], lsems.at[t])
            mine.start()
            locals_.append(mine)
            first = [copy(t, 0, me, sibling, src=ins[t])]
            first += [copy(t, 1 + j, me, (*chip, mc), src=ins[t]) for j, chip in enumerate(chips)]
            for cp in first:
                cp.start()
            started += first
        for t in range(n):
            for j, chip in enumerate(chips):
                copy(t, 1 + j, (*chip, mc), me).wait_recv()
                fwd = copy(t, 4 + j, (*chip, mc), sibling)
                fwd.start()
                started.append(fwd)
        for t in range(n):
            copy(t, 0, sibling, me).wait_recv()
            for j, chip in enumerate(chips):
                copy(t, 4 + j, (*chip, 1 - mc), me).wait_recv()
        for cp in started:
            cp.wait_send()
        for mine in locals_:
            mine.wait()

    anyspec = pl.BlockSpec(memory_space=pl.ANY)
    return pl.pallas_call(
        body, out_shape=[jax.ShapeDtypeStruct((NDEV,) + s.shape, s.dtype) for s in shards],
        in_specs=[anyspec] * n, out_specs=[anyspec] * n,
        scratch_shapes=[pltpu.SemaphoreType.DMA((7 * n,)), pltpu.SemaphoreType.DMA((7 * n,)),
                        pltpu.SemaphoreType.DMA((n,))],
        name=name)(*shards)


def _idx(p):
    return 4 * p[0] + 2 * p[1] + p[2]


def _remote(src, dst, ss, rs, k, to):
    return pltpu.make_async_remote_copy(src_ref=src, dst_ref=dst, send_sem=ss.at[k], recv_sem=rs.at[k],
                                        device_id=to, device_id_type=MESH)


def ex_ag_chips(shards):
    n = len(shards)

    def copies(ci, co, ss, rs, base):
        mx, my, mc = _me()
        me = (mx, my, mc)
        peers = [(mx, my, 1 - mc), (1 - mx, my, mc), (mx, 1 - my, mc), (1 - mx, 1 - my, mc)]
        sends, recvs, local = [], [], []
        for t in range(n):
            b = base + 5 * t
            for k, peer in enumerate(peers):
                sends.append(_remote(ci[t], co[t].at[_idx(me)], ss, rs, b + k, peer))
                recvs.append(_remote(ci[t], co[t].at[_idx(peer)], ss, rs, b + k, peer))
            local.append(pltpu.make_async_copy(ci[t], co[t].at[_idx(me)], ss.at[b + 4]))
        return sends, recvs, local

    def start(ci, co, ss, rs, base):
        sends, _, local = copies(ci, co, ss, rs, base)
        for cp in local + sends:
            cp.start()

    def finish(ci, co, ss, rs, base):
        sends, recvs, local = copies(ci, co, ss, rs, base)
        for cp in recvs:
            cp.wait_recv()
        for cp in sends:
            cp.wait_send()
        for cp in local:
            cp.wait()

    outs = [jax.ShapeDtypeStruct((NDEV,) + s.shape, s.dtype) for s in shards]
    return Exchange(shards, outs, {}, 5 * n, start, finish)


def ex_ag_sibling(bufs):
    n = len(bufs)

    def copies(co, ss, rs, base):
        mx, my, mc = _me()
        sibling = (mx, my, 1 - mc)
        chips = [(1 - mx, my), (mx, 1 - my), (1 - mx, 1 - my)]
        sends, recvs = [], []
        for t in range(n):
            for j, chip in enumerate(chips):
                mine, theirs = co[t].at[_idx((*chip, mc))], co[t].at[_idx((*chip, 1 - mc))]
                sends.append(_remote(mine, mine, ss, rs, base + 3 * t + j, sibling))
                recvs.append(_remote(mine, theirs, ss, rs, base + 3 * t + j, sibling))
        return sends, recvs

    def start(ci, co, ss, rs, base):
        for cp in copies(co, ss, rs, base)[0]:
            cp.start()

    def finish(ci, co, ss, rs, base):
        sends, recvs = copies(co, ss, rs, base)
        for cp in recvs:
            cp.wait_recv()
        for cp in sends:
            cp.wait_send()

    outs = [jax.ShapeDtypeStruct(b.shape, b.dtype) for b in bufs]
    return Exchange(bufs, outs, {t: t for t in range(n)}, 3 * n, start, finish)


def ex_rs_sibling(grads):
    n = len(grads)

    def copies(ci, co, ss, rs, base):
        mx, my, mc = _me()
        return [_remote(ci[t].at[pl.ds((1 - mc) * 4, 4)], co[t], ss, rs, base + t, (mx, my, 1 - mc)) for t in range(n)]

    def start(ci, co, ss, rs, base):
        for cp in copies(ci, co, ss, rs, base):
            cp.start()

    def finish(ci, co, ss, rs, base):
        for cp in copies(ci, co, ss, rs, base):
            cp.wait()

    outs = [jax.ShapeDtypeStruct((4,) + g.shape[1:], g.dtype) for g in grads]
    return Exchange(grads, outs, {}, n, start, finish)


def ex_rs_chips(parts):
    n = len(parts)

    def copies(ci, co, ss, rs, base):
        mx, my, mc = _me()
        cps = []
        for t in range(n):
            for k in range(1, 4):
                px, py = _flip(mx, k & 2), _flip(my, k & 1)
                cps.append(_remote(ci[t].at[2 * px + py], co[t].at[k - 1], ss, rs, base + 3 * t + k - 1, (px, py, mc)))
        return cps

    def start(ci, co, ss, rs, base):
        for cp in copies(ci, co, ss, rs, base):
            cp.start()

    def finish(ci, co, ss, rs, base):
        for cp in copies(ci, co, ss, rs, base):
            cp.wait()

    outs = [jax.ShapeDtypeStruct((3,) + p.shape[1:], p.dtype) for p in parts]
    return Exchange(parts, outs, {}, 3 * n, start, finish)


def run_exchanges(name, xs):
    x = merge_exchanges(xs)
    n_ci, n_co = len(x.ins), len(x.out_shapes)

    def body(*refs):
        ci, co = refs[:n_ci], refs[n_ci:n_ci + n_co]
        ss, rs = refs[n_ci + n_co:]
        x.start(ci, co, ss, rs, 0)
        x.finish(ci, co, ss, rs, 0)

    hbm = pl.BlockSpec(memory_space=pl.ANY)
    return pl.pallas_call(
        body, out_shape=x.out_shapes, in_specs=[hbm] * n_ci, out_specs=[hbm] * n_co, input_output_aliases=x.aliases,
        scratch_shapes=[pltpu.SemaphoreType.DMA((x.n_sems,)), pltpu.SemaphoreType.DMA((x.n_sems,))], name=name)(*x.ins)


def _rope_tables(S, nctx):
    t = jnp.arange(S)
    row = (t // GRID_W).astype(F32)
    col = (t % GRID_W).astype(F32)
    pairs = HEAD // 4
    inv = ROPE_THETA ** (-jnp.arange(pairs, dtype=F32) / pairs)
    ang_r = row[:, None] * inv
    ang_c = col[:, None] * inv
    ang = jnp.concatenate([ang_r, ang_r, ang_c, ang_c], axis=-1)
    cos = jnp.concatenate([jnp.ones((nctx, HEAD), F32), jnp.cos(ang)], axis=0)
    sin = jnp.concatenate([jnp.zeros((nctx, HEAD), F32), jnp.sin(ang)], axis=0)
    lane = jnp.arange(HEAD)[None, :]
    first = (lane & 32) == 0
    return cos, jnp.where(first, -sin, 0.0), jnp.where(first, 0.0, sin)


def _pad_rows(v, rows):
    v = v.reshape(-1).astype(F32)
    return jnp.pad(v, (0, rows * 128 - v.shape[0])).reshape(rows, 128)


def _rows8(n):
    return -(-n // 1024) * 8


def kernel(x, c, ctx, c_ctx, ada_w, ada_b, norm_w, mlp_w1, mlp_w2, ev_w_in, ev_w_out, ev_q_norm, ev_k_norm, ev_sink, od_w_in, od_w_out, od_rpb, final_norm_w, loss_target, m_c_ctx, m_ada_w, m_ada_b, m_norm_w, m_mlp_w1, m_mlp_w2, m_ev_w_in, m_ev_w_out, m_ev_q_norm, m_ev_k_norm, m_ev_sink, m_od_w_in, m_od_w_out, m_od_rpb, m_final_norm_w, v_c_ctx, v_ada_w, v_ada_b, v_norm_w, v_mlp_w1, v_mlp_w2, v_ev_w_in, v_ev_w_out, v_ev_q_norm, v_ev_k_norm, v_ev_sink, v_od_w_in, v_od_w_out, v_od_rpb, v_final_norm_w):
    S, D = x.shape[1], x.shape[2]
    NC = ctx.shape[1]
    T = NC + S
    assert NC == ROW_TILE and S % GRID_W == 0
    ada_cols = ada_w.shape[2]
    nw_cols = norm_w.shape[2]
    me = 4 * lax.axis_index("x") + 2 * lax.axis_index("y") + lax.axis_index("c")

    pack1 = jnp.concatenate([_pad_rows(c, _rows8(D)), _pad_rows(norm_w, _rows8(4 * nw_cols))], axis=0)
    g1 = ag_small("ag_c_normw", pack1)
    c_all = g1[:, :D // 128].reshape(NDEV, D)
    nw_rows = _rows8(D)
    nw = g1[:, nw_rows:nw_rows + 4 * nw_cols // 128].reshape(NDEV, 2, 2, nw_cols)
    nw = nw.transpose(1, 2, 0, 3).reshape(2, 2, D)
    cin = jnp.concatenate([c_all, jnp.broadcast_to(c_ctx[None], (NDEV, D))], axis=0)
    act = _vmem_call("silu_c", lambda v: _silu(v).astype(BF16), jax.ShapeDtypeStruct((2 * NDEV, D), BF16), cin)
    ada_b_loc = lax.dynamic_slice_in_dim(ada_b, me * ada_cols, ada_cols, axis=1)
    mods = [mm_nn(f"mod{i}", act, ada_w[i], _epi_bias, [F32], extras=(ada_b_loc[i:i + 1],), extra_kinds=('n',))[0]
            for i in range(2)]
    gm = ag_small("ag_mod", jnp.concatenate(mods, axis=1))
    gm = gm.reshape(NDEV, 2 * NDEV, 2, ada_cols).transpose(2, 1, 0, 3).reshape(2, 2 * NDEV, NDEV * ada_cols)
    mod_lat = lax.dynamic_index_in_dim(gm, me, axis=1, keepdims=False)
    mod_ctx = gm[:, NDEV]
    mod2 = jnp.stack([mod_ctx, mod_lat], axis=1).reshape(2, 2, 6, D)

    def chunk(i, j):
        return mod2[i, :, j, :]

    def b16(w):
        return w.astype(BF16)

    (w_in_e,) = ag_big("ag_weights_l0_qkv", [b16(ev_w_in[0])])

    cos, sa, sb = _rope_tables(S, NC)
    bias8 = na_span_bias(na_bias_table(od_rpb[0]))
    sink = ev_sink[0]
    TQ_F, TQ_B = 256, 256

    X0 = jnp.concatenate([ctx[0], x[0]], axis=0)
    h_a = norm_mod("l0_norm1", X0, nw[0, 0][None], chunk(0, 0), chunk(0, 1), NC)
    (qkv0,), (w_out_e_half,) = mm_nn("l0_qkv", h_a, w_in_e, _epi_store(F32), [F32],
                                     carry=[ex_ag_chips([b16(ev_w_out[0])])])
    qkvh0 = prep_even("l0_prep", qkv0, ev_q_norm, ev_k_norm, cos, sa, sb)
    (o0, stats0), (w1_0_half, w2_0_half, w_out_o_half, w_out_e) = attn_even_fwd(
        "l0_attn", qkvh0, sink, NC, TQ_F,
        carry=[ex_ag_chips([b16(mlp_w1[0]), b16(mlp_w2[0]), b16(od_w_out[0])]), ex_ag_sibling([w_out_e_half])])
    w_out_e = w_out_e.reshape(-1, D)
    tm0 = _tile(T, 1100)
    (X1, y0), (w1_0, w2_0) = mm_nn("l0_out", o0, w_out_e, _epi_resid_gate(NC, tm0), [F32, F32],
                                   extras=(X0, chunk(0, 2)), extra_kinds=('mn', 'n'),
                                   carry=[ex_ag_sibling([w1_0_half, w2_0_half])])
    h_b = norm_mod("l0_norm2", X1, nw[0, 1][None], chunk(0, 3), chunk(0, 4), NC)
    (a0, r0), (w_in_o_half, w_out_o) = mm_nn(
        "l0_up", h_b, w1_0, _epi_relu2, [BF16, BF16], tn_cap=1024,
        carry=[ex_ag_chips([b16(od_w_in[0])]), ex_ag_sibling([w_out_o_half])])
    (X2, z0), (w1_1_half, w_in_o) = mm_nn(
        "l0_down", a0, w2_0.reshape(-1, D), _epi_resid_gate(NC, tm0), [F32, F32], extras=(X1, chunk(0, 5)),
        extra_kinds=('mn', 'n'), tn_cap=1024, carry=[ex_ag_chips([b16(mlp_w1[1])]), ex_ag_sibling([w_in_o_half])])
    w_out_o = w_out_o.reshape(-1, D)

    h_c = norm_mod("l1_norm1", X2, nw[1, 0][None], chunk(1, 0), chunk(1, 1), NC)
    (qkv1,), (w1_1,) = mm_nn("l1_qkv", h_c, w_in_o, _epi_store(BF16), [BF16], tn_cap=768,
                             carry=[ex_ag_sibling([w1_1_half])])
    (o1, stats1), (w2_1_half,) = attn_odd_fwd("l1_attn", qkv1, bias8, NC, carry=[ex_ag_chips([b16(mlp_w2[1])])])
    X2l = X2[NC:]
    tm1 = _tile(S, 1100)
    (X3, y1), (w2_1,) = mm_nn("l1_out", o1, w_out_o, _epi_resid_gate(0, tm1), [F32, F32],
                              extras=(X2l, chunk(1, 2)), extra_kinds=('mn', 'n'),
                              carry=[ex_ag_sibling([w2_1_half])])
    h_d = norm_mod("l1_norm2", X3, nw[1, 1][None], chunk(1, 3), chunk(1, 4), 0)
    a1, r1 = mm_nn("l1_up", h_d, w1_1, _epi_relu2, [BF16, BF16], tn_cap=1024)
    X4, z1 = mm_nn("l1_down", a1, w2_1.reshape(-1, D), _epi_resid_gate(0, tm1), [F32, F32], extras=(X3, chunk(1, 5)),
                   extra_kinds=('mn', 'n'), tn_cap=1024)
    dX4, loss_p, dfw_p, dz1, pg2_1 = final_loss("final_loss", X4, final_norm_w[None], loss_target[0], z1, chunk(1, 5))
    w_in = [w_in_e, w_in_o]
    w_out = [w_out_e, w_out_o]
    w1 = [w1_0, w1_1]
    w2 = [w2_0.reshape(-1, D), w2_1.reshape(-1, D)]

    mc4 = (lax.axis_index("c") * 4).astype(jnp.int32)
    my_chip = (2 * lax.axis_index("x") + lax.axis_index("y")).astype(jnp.int32)

    def chip_sum(tag, g8, sib4):
        return rs_chip_sum(f"rs_chip_sum_{tag}", g8, sib4, jnp.stack([mc4, my_chip]))

    du1 = mm_nt("l1_down_dx", dz1, w2[1], _epi_mul2r, BF16, extras=(r1,))
    g_w1_1 = mm_tn("l1_up_dw", h_d, du1, 1)
    g_w2_1, (sib_w1_1,) = mm_tn("l1_down_dw", a1, dz1, 0, carry=[ex_rs_sibling([g_w1_1])])
    dh_d, (rem_w1_1, sib_w2_1) = mm_nt(
        "l1_up_dx", du1, w1[1], _epi_store(F32), F32, to_cap=2048,
        carry=[ex_rs_chips([chip_sum("w1_1", g_w1_1, sib_w1_1)]), ex_rs_sibling([g_w2_1])])
    dX3, pn2_1, dy1, pg1_1 = norm_bwd("l1_norm2_bwd", X3, dh_d, dX4, nw[1, 1][None], chunk(1, 4), 0,
                                      gate=(y1, chunk(1, 2)))
    do1 = mm_nt("l1_out_dx", dy1, w_out[1], _epi_store(BF16), BF16)
    g_wout_1 = mm_tn("l1_out_dw", o1, dy1, 0)
    (dq1, dk1, dv1, dbias8), (rem_w2_1, sib_wout_1) = attn_odd_bwd(
        "l1_attn_bwd", qkv1, bias8, do1, stats1, NC,
        carry=[ex_rs_chips([chip_sum("w2_1", g_w2_1, sib_w2_1)]), ex_rs_sibling([g_wout_1])])
    dqkv1 = jnp.concatenate([jnp.pad(dq1, ((NC, 0), (0, 0))), dk1, dv1], axis=1).astype(BF16)
    dh_c, (rem_wout_1,) = mm_nt("l1_qkv_dx", dqkv1, w_in[1], _epi_store(F32), F32,
                                carry=[ex_rs_chips([chip_sum("wout_1", g_wout_1, sib_wout_1)])])
    g_win_1 = mm_tn("l1_qkv_dw", h_c, dqkv1, 1)
    dX2, pn1_1, dz0, pg2_0 = norm_bwd("l1_norm1_bwd", X2, dh_c, dX3, nw[1, 0][None], chunk(1, 1), NC, dres_skip=NC,
                                      gate=(z0, chunk(0, 5)))
    d_rpb = na_bias_grad("rpb_grad", na_span_bias_grad(dbias8))

    du0, (sib_win_1,) = mm_nt("l0_down_dx", dz0, w2[0], _epi_mul2r, BF16, extras=(r0,),
                              carry=[ex_rs_sibling([g_win_1])])
    g_w1_0, (rem_win_1,) = mm_tn("l0_up_dw", h_b, du0, 1,
                                 carry=[ex_rs_chips([chip_sum("win_1", g_win_1, sib_win_1)])])
    g_w2_0, (sib_w1_0,) = mm_tn("l0_down_dw", a0, dz0, 0, carry=[ex_rs_sibling([g_w1_0])])
    dh_b, (rem_w1_0, sib_w2_0) = mm_nt(
        "l0_up_dx", du0, w1[0], _epi_store(F32), F32,
        carry=[ex_rs_chips([chip_sum("w1_0", g_w1_0, sib_w1_0)]), ex_rs_sibling([g_w2_0])])
    dX1, pn2_0, dy0, pg1_0 = norm_bwd("l0_norm2_bwd", X1, dh_b, dX2, nw[0, 1][None], chunk(0, 4), NC,
                                      gate=(y0, chunk(0, 2)))
    do0 = mm_nt("l0_out_dx", dy0, w_out[0], _epi_store(BF16), BF16)
    g_wout_0 = mm_tn("l0_out_dw", o0, dy0, 0)
    (dq0, dk0, dv0, dsink_p), (rem_w2_0, sib_wout_0) = attn_even_bwd(
        "l0_attn_bwd", qkvh0, sink, do0, stats0, NC, TQ_B,
        carry=[ex_rs_chips([chip_sum("w2_0", g_w2_0, sib_w2_0)]), ex_rs_sibling([g_wout_0])])
    dqkv0, pqk = prep_even_bwd("l0_prep_bwd", qkv0, dq0, dk0, dv0, ev_q_norm, ev_k_norm, cos, sa, sb)
    g_win_0, (rem_wout_0,) = mm_tn("l0_qkv_dw", h_a, dqkv0, 1,
                                   carry=[ex_rs_chips([chip_sum("wout_0", g_wout_0, sib_wout_0)])])
    dh_a, (sib_win_0,) = mm_nt("l0_qkv_dx", dqkv0, w_in[0], _epi_store(F32), F32, carry=[ex_rs_sibling([g_win_0])])
    (dx_lat, pn1_0), (rem_win_0,) = norm_bwd(
        "l0_norm1_bwd", X0, dh_a, dX1, nw[0, 0][None], chunk(0, 1), NC, out_skip=NC,
        carry=[ex_rs_chips([chip_sum("win_0", g_win_0, sib_win_0)])])
    grad_x = dx_lat[None]

    def dmod(grp, pn1, pg1, pn2, pg2):
        return jnp.concatenate([pn1[grp], pn1[2 + grp], pg1[grp], pn2[grp], pn2[2 + grp], pg2[grp]])

    dmod_lat = jnp.stack([dmod(1, pn1_0, pg1_0, pn2_0, pg2_0), dmod(1, pn1_1, pg1_1, pn2_1, pg2_1)])
    dmod_ctx = jnp.stack([dmod(0, pn1_0, pg1_0, pn2_0, pg2_0), dmod(0, pn1_1, pg1_1, pn2_1, pg2_1)])
    dnw_p = jnp.stack([pn1_0[4], pn2_0[4], pn1_1[4], pn2_1[4]])
    pieces = [dmod_lat, dmod_ctx, dnw_p, pqk[0], pqk[1], dsink_p[8:, 0, 0], d_rpb, dfw_p[0], loss_p[0, 0]]
    sizes = [int(np.prod(p.shape)) for p in pieces]
    rows = [_rows8(s) for s in sizes]
    pack2 = jnp.concatenate([_pad_rows(p, r) for p, r in zip(pieces, rows)], axis=0)
    g2, s2 = ag_small("ag_small_grads", pack2, with_sum=True)
    offs = np.concatenate([[0], np.cumsum(rows)])

    def piece(arr, i, shape):
        return arr[..., offs[i]:offs[i + 1], :].reshape(arr.shape[:-2] + (-1,))[..., :sizes[i]].reshape(
            arr.shape[:-2] + shape)

    dmod_all = piece(g2, 0, (2, 6 * D))
    dmodc_sum = piece(s2, 1, (2, 6 * D))
    dnw_sum = piece(s2, 2, (2, 2, D))
    g_qn = piece(s2, 3, ev_q_norm.shape)
    g_kn = piece(s2, 4, ev_k_norm.shape)
    g_sink = piece(s2, 5, ev_sink.shape)
    g_rpb = piece(s2, 6, od_rpb.shape)
    g_fw = piece(s2, 7, final_norm_w.shape)
    loss = piece(s2, 8, ())

    dm16 = jnp.concatenate([dmod_all.transpose(1, 0, 2), dmodc_sum[:, None, :],
                            jnp.zeros((2, NDEV - 1, 6 * D), F32)], axis=1)
    dm16_loc = lax.dynamic_slice_in_dim(dm16.reshape(2, 2 * NDEV, NDEV, ada_cols), me, 1, axis=2)[:, :, 0, :]
    g_ada_b = _vmem_call("ada_b_grad", lambda v: jnp.sum(v, axis=1),
                         jax.ShapeDtypeStruct((2, 6 * D), F32), dm16)
    g_ada_w = []
    dact_p = None
    for i in range(2):
        dmb = dm16_loc[i].astype(BF16)
        g_ada_w.append(mm_tn(f"ada_w_grad{i}", act, dmb, None))
        part = mm_nt(f"ada_dact{i}", dmb, ada_w[i], _epi_store(F32), F32)
        dact_p = part if dact_p is None else dact_p + part
    _, dact = ag_small("ag_cctx", dact_p, with_sum=True)

    def cctx_grad(da, cc):
        sg = 1.0 / (1.0 + jnp.exp(-cc))
        return da[NDEV:NDEV + 1] * (sg * (1.0 + cc * (1.0 - sg)))

    g_cctx = _vmem_call("cctx_grad", cctx_grad, jax.ShapeDtypeStruct((1, D), F32), dact, c_ctx[None])[0]

    grads = [g_win_0, g_wout_0, g_w1_0, g_w2_0, g_win_1, g_wout_1, g_w1_1, g_w2_1]
    sib = [sib_win_0, sib_wout_0, sib_w1_0, sib_w2_0, sib_win_1, sib_wout_1, sib_w1_1, sib_w2_1]
    rem = [rem_win_0, rem_wout_0, rem_w1_0, rem_w2_0, rem_win_1, rem_wout_1, rem_w1_1, rem_w2_1]
    own_idx = jnp.stack([mc4 + my_chip, my_chip])

    def big(tag, w, m, v, ts):
        res = None
        for l, t in enumerate(ts):
            res = adamw_rs(f"adamw_{tag}_{l}", w, grads[t], sib[t], rem[t], m, v, own_idx, l, res)
        return tuple(res)

    r_ev_w_in = big('ev_w_in', ev_w_in, m_ev_w_in, v_ev_w_in, [0])
    r_ev_w_out = big('ev_w_out', ev_w_out, m_ev_w_out, v_ev_w_out, [1])
    r_mlp_w1 = big('mlp_w1', mlp_w1, m_mlp_w1, v_mlp_w1, [2, 6])
    r_mlp_w2 = big('mlp_w2', mlp_w2, m_mlp_w2, v_mlp_w2, [3, 7])
    r_od_w_in = big('od_w_in', od_w_in, m_od_w_in, v_od_w_in, [4])
    r_od_w_out = big('od_w_out', od_w_out, m_od_w_out, v_od_w_out, [5])

    g_ada = jnp.stack(g_ada_w)
    r_ada_w = adamw_rows("adamw_ada_w", ada_w.reshape(2 * D, ada_cols), g_ada.reshape(2 * D, ada_cols),
                         m_ada_w.reshape(2 * D, ada_cols), v_ada_w.reshape(2 * D, ada_cols))
    r_ada_w = tuple(u.reshape(2, D, ada_cols) for u in r_ada_w)

    g_nw_loc = lax.dynamic_slice_in_dim(dnw_sum, me * nw_cols, nw_cols, axis=2)
    small = [(c_ctx, g_cctx, m_c_ctx, v_c_ctx), (ada_b, g_ada_b, m_ada_b, v_ada_b),
             (norm_w, g_nw_loc, m_norm_w, v_norm_w), (ev_q_norm, g_qn, m_ev_q_norm, v_ev_q_norm),
             (ev_k_norm, g_kn, m_ev_k_norm, v_ev_k_norm), (ev_sink, g_sink, m_ev_sink, v_ev_sink),
             (od_rpb, g_rpb, m_od_rpb, v_od_rpb), (final_norm_w, g_fw, m_final_norm_w, v_final_norm_w)]
    srows = [_rows8(int(np.prod(w.shape))) for w, _, _, _ in small]
    packs = [jnp.concatenate([_pad_rows(tup[k], r) for tup, r in zip(small, srows)], axis=0) for k in range(4)]
    sres = adamw_rows("adamw_small", *packs)
    soffs = np.concatenate([[0], np.cumsum(srows)])

    def unpack(arr, i):
        w = small[i][0]
        return arr[soffs[i]:soffs[i + 1]].reshape(-1)[:int(np.prod(w.shape))].reshape(w.shape)

    sm = [[unpack(sres[k], i) for i in range(len(small))] for k in range(4)]

    def outs(k):
        big_k = {'ada_w': r_ada_w[k], 'mlp_w1': r_mlp_w1[k], 'mlp_w2': r_mlp_w2[k], 'ev_w_in': r_ev_w_in[k],
                 'ev_w_out': r_ev_w_out[k], 'od_w_in': r_od_w_in[k], 'od_w_out': r_od_w_out[k]}
        return (sm[k][0], big_k['ada_w'], sm[k][1], sm[k][2], big_k['mlp_w1'], big_k['mlp_w2'], big_k['ev_w_in'],
                big_k['ev_w_out'], sm[k][3], sm[k][4], sm[k][5], big_k['od_w_in'], big_k['od_w_out'], sm[k][6],
                sm[k][7])

    return (loss, grad_x, *outs(0), *outs(1), *outs(2), *outs(3))
```

```python
import numpy as np
import jax
import jax.numpy as jnp
from jax import lax
from jax.experimental import pallas as pl
from jax.experimental.pallas import tpu as pltpu

F32 = jnp.float32
BF16 = jnp.bfloat16
MESH = pl.DeviceIdType.MESH

NDEV = 8
HEAD = 128
GRID_W = 64
NA_KH, NA_KW = 8, 16
WINDOW = 128
ROPE_THETA = 10000.0
EPS = 1e-6
NEG = -1e30
SCALE = HEAD ** -0.5
ROW_TILE = 256
VMEM_LIMIT = 56 * 1024 * 1024

ADAM_LR, ADAM_B1, ADAM_B2, ADAM_EPS, ADAM_WD, ADAM_STEP = 0.001, 0.9, 0.999, 1e-08, 0.01, 10

NT = (((1,), (1,)), ((), ()))
NN = (((1,), (0,)), ((), ()))
TN = (((0,), (0,)), ((), ()))


def _cparams(sem):
    return pltpu.CompilerParams(dimension_semantics=sem, vmem_limit_bytes=VMEM_LIMIT)


def _tile(n, cap):
    if n <= cap:
        return n
    t = cap - cap % 64
    while t >= 64:
        if n % t == 0:
            return t
        t -= 64
    raise ValueError((n, cap))


def _dot(a, b, dims):
    return lax.dot_general(a.astype(BF16), b.astype(BF16), dims, preferred_element_type=F32)


def _slot(d):
    return (d % 2) * 4 + d // 2


class Exchange:
    def __init__(self, ins, out_shapes, aliases, n_sems, start, finish):
        self.ins, self.out_shapes, self.aliases, self.n_sems = list(ins), list(out_shapes), dict(aliases), n_sems
        self.start, self.finish = start, finish


def merge_exchanges(xs):
    ins, outs, aliases, bases, n = [], [], {}, [], 0
    for x in xs:
        bases.append((len(ins), len(outs), n))
        aliases.update({len(ins) + i: len(outs) + o for i, o in x.aliases.items()})
        ins += x.ins
        outs += x.out_shapes
        n += x.n_sems

    def run(which):
        def f(ci, co, ss, rs, base):
            for x, (i0, o0, s0) in zip(xs, bases):
                getattr(x, which)(ci[i0:i0 + len(x.ins)], co[o0:o0 + len(x.out_shapes)], ss, rs, base + s0)
        return f

    return Exchange(ins, outs, aliases, n, run('start'), run('finish'))


def _call(name, body, grid, ins, in_specs, out_shape, out_specs, scratch, sems, carry=None):
    if not carry:
        return pl.pallas_call(body, grid=grid, in_specs=in_specs, out_specs=out_specs, out_shape=out_shape,
                              scratch_shapes=scratch, compiler_params=_cparams(sems), name=name)(*ins)
    x = merge_exchanges(carry)
    n_in, n_ci, n_out, n_co, n_sc = len(ins), len(x.ins), len(out_shape), len(x.out_shapes), len(scratch)

    def wrapped(*refs):
        p = [0]

        def take(k):
            p[0] += k
            return refs[p[0] - k:p[0]]

        a, ci, o, co, sc = take(n_in), take(n_ci), take(n_out), take(n_co), take(n_sc)
        ss, rs = take(2)
        first = pl.program_id(0) == 0
        last = pl.program_id(0) == grid[0] - 1
        for d in range(1, len(grid)):
            first = jnp.logical_and(first, pl.program_id(d) == 0)
            last = jnp.logical_and(last, pl.program_id(d) == grid[d] - 1)

        @pl.when(first)
        def _():
            x.start(ci, co, ss, rs, 0)

        body(*a, *o, *sc)

        @pl.when(last)
        def _():
            x.finish(ci, co, ss, rs, 0)

    hbm = pl.BlockSpec(memory_space=pl.ANY)
    res = pl.pallas_call(
        wrapped, grid=grid, in_specs=list(in_specs) + [hbm] * n_ci, out_specs=list(out_specs) + [hbm] * n_co,
        out_shape=list(out_shape) + x.out_shapes,
        input_output_aliases={n_in + i: n_out + o for i, o in x.aliases.items()},
        scratch_shapes=list(scratch) + [pltpu.SemaphoreType.DMA((x.n_sems,)), pltpu.SemaphoreType.DMA((x.n_sems,))],
        compiler_params=_cparams(("arbitrary",) * len(grid)), name=name)(*ins, *x.ins)
    return list(res[:n_out]) + [list(res[n_out:])]


def _mm_core(name, grid, ins, in_specs, out_shape, out_specs, dims, acc_shape, epi, carry=None):
    nk = grid[2]
    n_extra = len(ins) - 2

    def body_single(*refs):
        epi(_dot(refs[0][...], refs[1][...], dims), refs[2:2 + n_extra], refs[2 + n_extra:])

    def body(*refs):
        a_ref, b_ref = refs[0], refs[1]
        ex = refs[2:2 + n_extra]
        outs = refs[2 + n_extra:-1]
        acc = refs[-1]
        k = pl.program_id(2)

        @pl.when(k == 0)
        def _():
            acc[...] = _dot(a_ref[...], b_ref[...], dims)

        @pl.when(jnp.logical_and(k > 0, k < nk - 1))
        def _():
            acc[...] += _dot(a_ref[...], b_ref[...], dims)

        @pl.when(k == nk - 1)
        def _():
            epi(acc[...] + _dot(a_ref[...], b_ref[...], dims), ex, outs)

    if nk == 1:
        return _call(name, body_single, grid, ins, in_specs, out_shape, out_specs, [],
                     ("parallel", "parallel", "arbitrary"), carry)
    return _call(name, body, grid, ins, in_specs, out_shape, out_specs, [pltpu.VMEM(acc_shape, F32)],
                 ("parallel", "parallel", "arbitrary"), carry)


def _split(res, n, carry):
    own = res[0] if n == 1 else list(res[:n])
    return (own, res[n]) if carry else own


def _epi_store(dtype):
    def epi(acc, ex, outs):
        outs[0][...] = acc.astype(dtype)
    return epi


def _epi_bias(acc, ex, outs):
    outs[0][...] = acc + ex[0][...]


def _epi_relu2(acc, ex, outs):
    r = jnp.maximum(acc, 0.0)
    outs[0][...] = (r * r).astype(BF16)
    outs[1][...] = r.astype(BF16)


def _epi_mul2r(acc, ex, outs):
    outs[0][...] = (acc * (2.0 * ex[0][...].astype(F32))).astype(BF16)


def _epi_resid_gate(nctx, tm):
    def epi(acc, ex, outs):
        rows = pl.program_id(0) * tm + lax.broadcasted_iota(jnp.int32, (tm, 1), 0)
        g = jnp.where(rows < nctx, ex[1][0:1, :], ex[1][1:2, :])
        outs[0][...] = ex[0][...] + g * acc
        outs[1][...] = acc
    return epi


def mm_nn(name, a, w, epi, outs, extras=(), extra_kinds=(), tm_cap=1100, tn_cap=512, tk_cap=2048, carry=None):
    M, K = a.shape
    if w.ndim == 3:
        ns = w.shape[2]
        N = NDEV * ns
        tn = _tile(ns, tn_cap)
        nper = ns // tn
    else:
        N = w.shape[1]
        tn = _tile(N, tn_cap)
    tm = _tile(M, tm_cap)
    tk = _tile(K, tk_cap)
    grid = (M // tm, N // tn, K // tk)
    a_spec = pl.BlockSpec((tm, tk), lambda i, j, k: (i, k))
    if w.ndim == 3:
        b_spec = pl.BlockSpec((None, tk, tn), lambda i, j, k: (j // nper, k, j % nper))
    else:
        b_spec = pl.BlockSpec((tk, tn), lambda i, j, k: (k, j))
    ex_specs = []
    for e, kind in zip(extras, extra_kinds):
        if kind == 'mn':
            ex_specs.append(pl.BlockSpec((tm, tn), lambda i, j, k: (i, j)))
        else:
            ex_specs.append(pl.BlockSpec((e.shape[0], tn), lambda i, j, k: (0, j)))
    out_shape = [jax.ShapeDtypeStruct((M, N), dt) for dt in outs]
    out_specs = [pl.BlockSpec((tm, tn), lambda i, j, k: (i, j)) for _ in outs]
    res = _mm_core(name, grid, (a, w, *extras), [a_spec, b_spec, *ex_specs], out_shape, out_specs, NN, (tm, tn), epi,
                   carry)
    return (list(res[:len(outs)]), res[len(outs)]) if carry else res


def mm_nt(name, a, w, epi, out_dtype, extras=(), tm_cap=1100, to_cap=1024, tc_cap=2048, carry=None):
    M, N = a.shape
    tm = _tile(M, tm_cap)
    if w.ndim == 3:
        Kw, ns = w.shape[1], w.shape[2]
        tc = _tile(ns, tc_cap)
        cper = ns // tc
    else:
        Kw = w.shape[0]
        tc = _tile(N, tc_cap)
    to = _tile(Kw, to_cap)
    grid = (M // tm, Kw // to, N // tc)
    a_spec = pl.BlockSpec((tm, tc), lambda i, j, k: (i, k))
    if w.ndim == 3:
        b_spec = pl.BlockSpec((None, to, tc), lambda i, j, k: (k // cper, j, k % cper))
    else:
        b_spec = pl.BlockSpec((to, tc), lambda i, j, k: (j, k))
    ex_specs = [pl.BlockSpec((tm, to), lambda i, j, k: (i, j)) for _ in extras]
    out_shape = [jax.ShapeDtypeStruct((M, Kw), out_dtype)]
    out_specs = [pl.BlockSpec((tm, to), lambda i, j, k: (i, j))]
    return _split(_mm_core(name, grid, (a, w, *extras), [a_spec, b_spec, *ex_specs], out_shape, out_specs, NT, (tm, to),
                           epi, carry), 1, carry)


def mm_tn(name, a, b, shard_axis, to_cap=1024, tn_cap=1024, tc_cap=2200, carry=None):
    M, Ka = a.shape
    N = b.shape[1]
    tc = _tile(M, tc_cap)
    if shard_axis is None:
        to, tn = _tile(Ka, to_cap), _tile(N, tn_cap)
        shape = (Ka, N)
        oblk = (to, tn)
        omap = lambda i, j, k: (i, j)
    elif shard_axis == 1:
        ns = N // NDEV
        to, tn = _tile(Ka, to_cap), _tile(ns, tn_cap)
        per = ns // tn
        shape = (NDEV, Ka, ns)
        oblk = (None, to, tn)
        omap = lambda i, j, k: (_slot(j // per), i, j % per)
    else:
        rs = Ka // NDEV
        to, tn = _tile(rs, to_cap), _tile(N, tn_cap)
        per = rs // to
        shape = (NDEV, rs, N)
        oblk = (None, to, tn)
        omap = lambda i, j, k: (_slot(i // per), i % per, j)
    grid = (Ka // to, N // tn, M // tc)
    a_spec = pl.BlockSpec((tc, to), lambda i, j, k: (k, i))
    b_spec = pl.BlockSpec((tc, tn), lambda i, j, k: (k, j))
    out_shape = [jax.ShapeDtypeStruct(shape, F32)]
    out_specs = [pl.BlockSpec(oblk, omap)]
    return _split(_mm_core(name, grid, (a, b), [a_spec, b_spec], out_shape, out_specs, TN, (to, tn), _epi_store(F32),
                           carry), 1, carry)


def _row_spec(D):
    return pl.BlockSpec((ROW_TILE, D), lambda i: (i, 0))


def _const_spec(r, D):
    return pl.BlockSpec((r, D), lambda i: (0, 0))


def _grp(ref, is_ctx):
    return jnp.where(is_ctx, ref[0:1, :], ref[1:2, :])


def norm_mod(name, x, nw, sh, sc, nctx):
    R, D = x.shape
    assert R % ROW_TILE == 0 and nctx % ROW_TILE == 0

    def body(x_ref, nw_ref, sh_ref, sc_ref, o_ref):
        is_ctx = pl.program_id(0) * ROW_TILE < nctx
        xv = x_ref[...]
        rstd = lax.rsqrt(jnp.mean(xv * xv, axis=-1, keepdims=True) + EPS)
        n = xv * rstd * nw_ref[...]
        o_ref[...] = (n * (1.0 + _grp(sc_ref, is_ctx)) + _grp(sh_ref, is_ctx)).astype(BF16)

    return pl.pallas_call(
        body, grid=(R // ROW_TILE,),
        in_specs=[_row_spec(D), _const_spec(1, D), _const_spec(2, D), _const_spec(2, D)],
        out_specs=_row_spec(D), out_shape=jax.ShapeDtypeStruct((R, D), BF16),
        compiler_params=_cparams(("parallel",)), name=name)(x, nw, sh, sc)


def _gate_rows(dxv, y_ref, g_ref, is_ctx, dy_ref, gpart_ref):
    dy_ref[...] = (dxv * _grp(g_ref, is_ctx)).astype(BF16)
    s = jnp.sum(dxv * y_ref[...], axis=0, keepdims=True)
    zero = jnp.zeros_like(s)
    gpart_ref[0:1, :] += jnp.where(is_ctx, s, zero)
    gpart_ref[1:2, :] += jnp.where(is_ctx, zero, s)


def norm_bwd(name, x, dh, dres, nw, sc, nctx, dres_skip=0, out_skip=0, carry=None, gate=None):
    R, D = x.shape
    assert R % ROW_TILE == 0 and nctx % ROW_TILE == 0 and dres_skip % ROW_TILE == 0 and out_skip % ROW_TILE == 0
    res_tiles, out_tiles = dres_skip // ROW_TILE, out_skip // ROW_TILE

    def body(x_ref, dh_ref, dres_ref, nw_ref, sc_ref, *rest):
        if gate is None:
            dx_ref, part_ref = rest
        else:
            y_ref, g_ref, dx_ref, part_ref, dy_ref, gpart_ref = rest
        i = pl.program_id(0)
        is_ctx = i * ROW_TILE < nctx

        @pl.when(i == 0)
        def _():
            part_ref[...] = jnp.zeros_like(part_ref)
            if gate is not None:
                gpart_ref[...] = jnp.zeros_like(gpart_ref)

        xv = x_ref[...]
        dhv = dh_ref[...]
        w = nw_ref[...]
        rstd = lax.rsqrt(jnp.mean(xv * xv, axis=-1, keepdims=True) + EPS)
        xhat = xv * rstd
        n = xhat * w
        dn = dhv * (1.0 + _grp(sc_ref, is_ctx))
        dxhat = dn * w
        dres = dres_ref[...]
        if res_tiles:
            dres = jnp.where(i < res_tiles, 0.0, dres)
        dxv = dres + rstd * (dxhat - xhat * jnp.mean(dxhat * xhat, axis=-1, keepdims=True))
        dx_ref[...] = dxv
        s_sh = jnp.sum(dhv, axis=0, keepdims=True)
        s_sc = jnp.sum(dhv * n, axis=0, keepdims=True)
        s_nw = jnp.sum(dn * xhat, axis=0, keepdims=True)
        zero = jnp.zeros_like(s_sh)
        part_ref[0:1, :] += jnp.where(is_ctx, s_sh, zero)
        part_ref[1:2, :] += jnp.where(is_ctx, zero, s_sh)
        part_ref[2:3, :] += jnp.where(is_ctx, s_sc, zero)
        part_ref[3:4, :] += jnp.where(is_ctx, zero, s_sc)
        part_ref[4:5, :] += s_nw
        if gate is not None:
            _gate_rows(dxv, y_ref, g_ref, is_ctx, dy_ref, gpart_ref)

    ins = [x, dh, dres, nw, sc]
    in_specs = [_row_spec(D), _row_spec(D), pl.BlockSpec((ROW_TILE, D), lambda i: (jnp.maximum(i - res_tiles, 0), 0)),
                _const_spec(1, D), _const_spec(2, D)]
    out_shape = [jax.ShapeDtypeStruct((R - out_skip, D), F32), jax.ShapeDtypeStruct((8, D), F32)]
    out_specs = [pl.BlockSpec((ROW_TILE, D), lambda i: (jnp.maximum(i - out_tiles, 0), 0)), _const_spec(8, D)]
    if gate is not None:
        assert out_skip == 0
        ins += list(gate)
        in_specs += [_row_spec(D), _const_spec(2, D)]
        out_shape += [jax.ShapeDtypeStruct((R, D), BF16), jax.ShapeDtypeStruct((8, D), F32)]
        out_specs += [_row_spec(D), _const_spec(8, D)]
    res = _call(name, body, (R // ROW_TILE,), ins, in_specs, out_shape, out_specs, [], ("arbitrary",), carry)
    return _split(res, len(out_shape), carry)


def final_loss(name, x, fw, tgt, y, g):
    S, D = x.shape

    def body(x_ref, fw_ref, t_ref, y_ref, g_ref, dx_ref, loss_ref, dfw_ref, dy_ref, gpart_ref):
        i = pl.program_id(0)

        @pl.when(i == 0)
        def _():
            loss_ref[...] = jnp.zeros_like(loss_ref)
            dfw_ref[...] = jnp.zeros_like(dfw_ref)
            gpart_ref[...] = jnp.zeros_like(gpart_ref)

        xv = x_ref[...]
        w = fw_ref[...]
        rstd = lax.rsqrt(jnp.mean(xv * xv, axis=-1, keepdims=True) + EPS)
        xhat = xv * rstd
        e = xhat * w - t_ref[...]
        loss_ref[...] += 0.5 * jnp.sum(jnp.mean(e * e, axis=-1, keepdims=True))
        dout = e * (1.0 / D)
        dfw_ref[0:1, :] += jnp.sum(dout * xhat, axis=0, keepdims=True)
        dxhat = dout * w
        dxv = rstd * (dxhat - xhat * jnp.mean(dxhat * xhat, axis=-1, keepdims=True))
        dx_ref[...] = dxv
        _gate_rows(dxv, y_ref, g_ref, False, dy_ref, gpart_ref)

    return pl.pallas_call(
        body, grid=(S // ROW_TILE,),
        in_specs=[_row_spec(D), _const_spec(1, D), _row_spec(D), _row_spec(D), _const_spec(2, D)],
        out_specs=[_row_spec(D), pl.BlockSpec((8, 128), lambda i: (0, 0)), _const_spec(8, D), _row_spec(D),
                   _const_spec(8, D)],
        out_shape=[jax.ShapeDtypeStruct((S, D), F32), jax.ShapeDtypeStruct((8, 128), F32),
                   jax.ShapeDtypeStruct((8, D), F32), jax.ShapeDtypeStruct((S, D), BF16),
                   jax.ShapeDtypeStruct((8, D), F32)],
        compiler_params=_cparams(("arbitrary",)), name=name)(x, fw, tgt, y, g)


def _rope(x, cos, sa, sb):
    return x * cos + pltpu.roll(x, 96, 1) * sa + pltpu.roll(x, 32, 1) * sb


def _rope_t(dy, cos, sa, sb):
    return dy * cos + pltpu.roll(dy * sa, 32, 1) + pltpu.roll(dy * sb, 96, 1)


_EVEN_KINDS = ['qa'] * 8 + ['ka'] * 2 + ['v'] * 2 + ['qb'] * 8 + ['kb'] * 2 + ['v'] * 2
_EVEN_DSRC = ([('q', j) for j in range(8)] + [('k', 0), ('k', 1), ('v', 0), ('v', 1)]
              + [('q', 8 + j) for j in range(8)] + [('k', 2), ('k', 3), ('v', 2), ('v', 3)])


def _cols(j):
    return slice(j * HEAD, (j + 1) * HEAD)


def prep_even(name, qkv, qn, kn, cos, sa, sb):
    T, W = qkv.shape

    def body(x_ref, qn_ref, kn_ref, cos_ref, sa_ref, sb_ref, o_ref):
        cos_, sa_, sb_ = cos_ref[...], sa_ref[...], sb_ref[...]
        for j, kind in enumerate(_EVEN_KINDS):
            x = x_ref[:, _cols(j)]
            if kind in ('qa', 'ka'):
                rstd = lax.rsqrt(jnp.mean(x * x, axis=-1, keepdims=True) + EPS)
                x = x * rstd * (qn_ref[...] if kind == 'qa' else kn_ref[...])
            if kind != 'v':
                x = _rope(x, cos_, sa_, sb_)
            o_ref[:, _cols(j)] = x.astype(BF16)

    blk = pl.BlockSpec((ROW_TILE, W), lambda i: (i, 0))
    tab = pl.BlockSpec((ROW_TILE, HEAD), lambda i: (i, 0))
    one = pl.BlockSpec((1, HEAD), lambda i: (0, 0))
    return pl.pallas_call(
        body, grid=(T // ROW_TILE,), in_specs=[blk, one, one, tab, tab, tab], out_specs=blk,
        out_shape=jax.ShapeDtypeStruct(qkv.shape, BF16),
        compiler_params=_cparams(("parallel",)), name=name)(qkv, qn, kn, cos, sa, sb)


def prep_even_bwd(name, qkv, dq, dk, dv, qn, kn, cos, sa, sb):
    T, W = qkv.shape

    def body(x_ref, dq_ref, dk_ref, dv_ref, qn_ref, kn_ref, cos_ref, sa_ref, sb_ref, o_ref, part_ref):
        @pl.when(pl.program_id(0) == 0)
        def _():
            part_ref[...] = jnp.zeros_like(part_ref)

        cos_, sa_, sb_ = cos_ref[...], sa_ref[...], sb_ref[...]
        src = {'q': dq_ref, 'k': dk_ref, 'v': dv_ref}
        sums = {'qa': None, 'ka': None}
        for j, kind in enumerate(_EVEN_KINDS):
            which, blk_j = _EVEN_DSRC[j]
            d = src[which][:, _cols(blk_j)]
            if kind != 'v':
                d = _rope_t(d, cos_, sa_, sb_)
            if kind in ('qa', 'ka'):
                x = x_ref[:, _cols(j)]
                rstd = lax.rsqrt(jnp.mean(x * x, axis=-1, keepdims=True) + EPS)
                xhat = x * rstd
                s = jnp.sum(d * xhat, axis=0, keepdims=True)
                sums[kind] = s if sums[kind] is None else sums[kind] + s
                dxhat = d * (qn_ref[...] if kind == 'qa' else kn_ref[...])
                d = rstd * (dxhat - xhat * jnp.mean(dxhat * xhat, axis=-1, keepdims=True))
            o_ref[:, _cols(j)] = d.astype(BF16)
        part_ref[0:1, :] += sums['qa']
        part_ref[1:2, :] += sums['ka']

    def rows(w):
        return pl.BlockSpec((ROW_TILE, w), lambda i: (i, 0))

    one = pl.BlockSpec((1, HEAD), lambda i: (0, 0))
    return pl.pallas_call(
        body, grid=(T // ROW_TILE,),
        in_specs=[rows(W), rows(dq.shape[1]), rows(dk.shape[1]), rows(dv.shape[1]), one, one,
                  rows(HEAD), rows(HEAD), rows(HEAD)],
        out_specs=[rows(W), pl.BlockSpec((8, HEAD), lambda i: (0, 0))],
        out_shape=[jax.ShapeDtypeStruct(qkv.shape, BF16), jax.ShapeDtypeStruct((8, HEAD), F32)],
        compiler_params=_cparams(("arbitrary",)), name=name)(qkv, dq, dk, dv, qn, kn, cos, sa, sb)


def _even_maps():
    qmap = lambda h, qb: (qb, jnp.where(h < 8, h, h + 4))
    kmap = lambda h, qb: (0, jnp.where(h < 8, 8 + h // 4, 18 + h // 4))
    vmap = lambda h, qb: (0, jnp.where(h < 8, 10 + h // 4, 20 + h // 4))
    return qmap, kmap, vmap


def _softmax_parts(parts, extra=None, stats=None):
    if stats is None:
        m = parts[0].max(axis=-1, keepdims=True)
        for p in parts[1:]:
            m = jnp.maximum(m, p.max(axis=-1, keepdims=True))
        if extra is not None:
            m = jnp.maximum(m, extra)
    else:
        m = stats[0]
    es = [jnp.exp(p - m) for p in parts]
    ex = None if extra is None else jnp.exp(extra - m)
    if stats is None:
        l = es[0].sum(axis=-1, keepdims=True)
        for e in es[1:]:
            l = l + e.sum(axis=-1, keepdims=True)
        if extra is not None:
            l = l + ex
        inv = 1.0 / l
    else:
        inv = stats[1]
    return [e * inv for e in es], (None if ex is None else ex * inv), (m, inv)


def _win_scores(q, k_ref, qb, tq, nctx, S):
    L = tq + 2 * WINDOW
    nqc = nctx // tq
    qlat = (qb - nqc) * tq
    start = pl.multiple_of(jnp.clip(qlat - WINDOW, 0, S - L), 128)
    kc = k_ref[0:nctx, :]
    kw = k_ref[pl.ds(nctx + start, L), :]
    s_c = _dot(q, kc, NT) * SCALE
    s_w = _dot(q, kw, NT) * SCALE
    qpos = qlat + lax.broadcasted_iota(jnp.int32, (tq, 1), 0)
    kpos = start + lax.broadcasted_iota(jnp.int32, (1, L), 1)
    valid = jnp.logical_and(jnp.abs(kpos - qpos) <= WINDOW, qb >= nqc)
    return s_c, jnp.where(valid, s_w, NEG), start, L


def _softmax_raw(raw, stats=None):
    m = raw.max(axis=-1, keepdims=True) if stats is None else stats[0]
    e = jnp.exp2((raw - m) * (SCALE * np.log2(np.e)))
    inv = 1.0 / e.sum(axis=-1, keepdims=True) if stats is None else stats[1]
    return e * inv, (m, inv)


def _glob_keys(qb, tq, nctx, T):
    is_ctx = qb < nctx // tq
    return [(is_ctx, slice(0, nctx)), (jnp.logical_not(is_ctx), slice(0, T))]


def _stats_spec(tq):
    return pl.BlockSpec((None, None, tq, 2), lambda h, qb: (h, qb, 0, 0))


def attn_even_fwd(name, qkvh, sink, nctx, tq, carry=None):
    T = qkvh.shape[0]
    S = T - nctx
    qmap, kmap, vmap = _even_maps()

    def body(sink_ref, q_ref, k_ref, v_ref, o_ref, st_ref):
        h, qb = pl.program_id(0), pl.program_id(1)
        q = q_ref[...]

        for pred, keys in _glob_keys(qb, tq, nctx, T):
            @pl.when(jnp.logical_and(h < 8, pred))
            def _():
                p, (m, inv) = _softmax_raw(_dot(q, k_ref[keys, :], NT))
                st_ref[:, 0:1] = m
                st_ref[:, 1:2] = inv
                o_ref[...] = _dot(p, v_ref[keys, :], NN).astype(BF16)

        @pl.when(h >= 8)
        def _():
            s_c, s_w, start, L = _win_scores(q, k_ref, qb, tq, nctx, S)
            sk = jnp.full((tq, 1), sink_ref[jnp.maximum(h - 8, 0)], F32)
            (p_c, p_w), _, (m, inv) = _softmax_parts([s_c, s_w], sk)
            st_ref[:, 0:1] = m
            st_ref[:, 1:2] = inv
            o = _dot(p_c, v_ref[0:nctx, :], NN) + _dot(p_w, v_ref[pl.ds(nctx + start, L), :], NN)
            o_ref[...] = o.astype(BF16)

    res = _call(name, body, (16, T // tq), (sink, qkvh, qkvh, qkvh),
                [pl.BlockSpec(memory_space=pltpu.SMEM), pl.BlockSpec((tq, HEAD), qmap),
                 pl.BlockSpec((T, HEAD), kmap), pl.BlockSpec((T, HEAD), vmap)],
                [jax.ShapeDtypeStruct((T, 16 * HEAD), BF16), jax.ShapeDtypeStruct((16, T // tq, tq, 2), F32)],
                [pl.BlockSpec((tq, HEAD), lambda h, qb: (qb, h)), _stats_spec(tq)],
                [], ("parallel", "arbitrary"), carry)
    return _split(res, 2, carry)


def attn_even_bwd(name, qkvh, sink, do, stats, nctx, tq, carry=None):
    T = qkvh.shape[0]
    assert stats.shape == (16, T // tq, tq, 2)
    S = T - nctx
    qmap, kmap, vmap = _even_maps()

    def body(sink_ref, q_ref, k_ref, v_ref, do_ref, st_ref, dq_ref, dk_ref, dv_ref, ds_ref):
        h, qb = pl.program_id(0), pl.program_id(1)
        q = q_ref[...]
        dov = do_ref[...]

        @pl.when(jnp.logical_and(h % 4 == 0, qb == 0))
        def _():
            dk_ref[...] = jnp.zeros_like(dk_ref)
            dv_ref[...] = jnp.zeros_like(dv_ref)

        @pl.when(qb == 0)
        def _():
            ds_ref[...] = jnp.zeros_like(ds_ref)

        for pred, keys in _glob_keys(qb, tq, nctx, T):
            @pl.when(jnp.logical_and(h < 8, pred))
            def _():
                p, _ = _softmax_raw(_dot(q, k_ref[keys, :], NT), (st_ref[:, 0:1], st_ref[:, 1:2]))
                dp = _dot(dov, v_ref[keys, :], NT)
                row = jnp.sum(p * dp, axis=-1, keepdims=True)
                dsb = (p * (dp - row) * SCALE).astype(BF16)
                dq_ref[...] = _dot(dsb, k_ref[keys, :], NN)
                dk_ref[keys, :] += _dot(dsb, q, TN)
                dv_ref[keys, :] += _dot(p, dov, TN)

        @pl.when(h >= 8)
        def _():
            s_c, s_w, start, L = _win_scores(q, k_ref, qb, tq, nctx, S)
            sk = jnp.full((tq, 1), sink_ref[jnp.maximum(h - 8, 0)], F32)
            (p_c, p_w), p_s, _ = _softmax_parts([s_c, s_w], sk, (st_ref[:, 0:1], st_ref[:, 1:2]))
            win = pl.ds(nctx + start, L)
            dp_c = _dot(dov, v_ref[0:nctx, :], NT)
            dp_w = _dot(dov, v_ref[win, :], NT)
            row = jnp.sum(p_c * dp_c, axis=-1, keepdims=True) + jnp.sum(p_w * dp_w, axis=-1, keepdims=True)
            ds_c = (p_c * (dp_c - row) * SCALE).astype(BF16)
            ds_w = (p_w * (dp_w - row) * SCALE).astype(BF16)
            dq_ref[...] = _dot(ds_c, k_ref[0:nctx, :], NN) + _dot(ds_w, k_ref[win, :], NN)
            dk_ref[0:nctx, :] += _dot(ds_c, q, TN)
            dk_ref[win, :] += _dot(ds_w, q, TN)
            dv_ref[0:nctx, :] += _dot(p_c, dov, TN)
            dv_ref[win, :] += _dot(p_w, dov, TN)
            ds_ref[...] += jnp.sum(-(p_s * row))

    kv_out = pl.BlockSpec((T, HEAD), lambda h, qb: (0, h // 4))
    res = _call(name, body, (16, T // tq), (sink, qkvh, qkvh, qkvh, do, stats),
                [pl.BlockSpec(memory_space=pltpu.SMEM), pl.BlockSpec((tq, HEAD), qmap),
                 pl.BlockSpec((T, HEAD), kmap), pl.BlockSpec((T, HEAD), vmap),
                 pl.BlockSpec((tq, HEAD), lambda h, qb: (qb, h)), _stats_spec(tq)],
                [jax.ShapeDtypeStruct((T, 16 * HEAD), F32), jax.ShapeDtypeStruct((T, 4 * HEAD), F32),
                 jax.ShapeDtypeStruct((T, 4 * HEAD), F32), jax.ShapeDtypeStruct((16, 8, 128), F32)],
                [pl.BlockSpec((tq, HEAD), lambda h, qb: (qb, h)), kv_out, kv_out,
                 pl.BlockSpec((None, 8, 128), lambda h, qb: (h, 0, 0))],
                [], ("arbitrary", "arbitrary"), carry)
    return _split(res, 4, carry)


NA_GROUP = 4
NA_SPAN = NA_KH + NA_GROUP - 1
_NA_PLAN = [[(j, 0) for j in range(NA_GROUP)],
            [(NA_KH // 2, j) for j in range(NA_GROUP)],
            [(NA_KH // 2 + j, NA_GROUP - 1) for j in range(NA_GROUP)]]


def _na_group(g, n_groups, rows):
    last = g == n_groups - 1
    kind = jnp.where(g == 0, 0, jnp.where(last, 2, 1))
    first_row = jnp.where(g == 0, 0, jnp.where(last, rows - NA_SPAN, NA_GROUP * g - NA_KH // 2))
    return kind, first_row


def na_span_bias(bias8):
    LW, LS = NA_KH * GRID_W, NA_SPAN * GRID_W
    kinds = []
    for plan in _NA_PLAN:
        strips = [jnp.pad(bias8[:, off], ((0, 0), (0, 0), (s * GRID_W, LS - LW - s * GRID_W)), constant_values=NEG)
                  for off, s in plan]
        kinds.append(jnp.concatenate(strips, axis=1))
    return jnp.stack(kinds, axis=1)


def na_span_bias_grad(db):
    LW = NA_KH * GRID_W
    out = [None] * NA_KH
    for kind, plan in enumerate(_NA_PLAN):
        for j, (off, s) in enumerate(plan):
            piece = db[:, kind, j * GRID_W:(j + 1) * GRID_W, s * GRID_W:s * GRID_W + LW]
            out[off] = piece if out[off] is None else out[off] + piece
    return jnp.stack(out, axis=1)


def _na_specs(T, nctx, n_groups, rows):
    LS = NA_SPAN * GRID_W
    tq = NA_GROUP * GRID_W
    assert nctx % tq == 0 and n_groups >= 3
    q_spec = pl.BlockSpec((tq, HEAD), lambda h, g: (g + nctx // tq, h))
    k_spec = pl.BlockSpec((T, HEAD), lambda h, g: (0, 16 + h))
    v_spec = pl.BlockSpec((T, HEAD), lambda h, g: (0, 32 + h))
    b_spec = pl.BlockSpec((None, None, tq, LS), lambda h, g: (h, _na_group(g, n_groups, rows)[0], 0, 0))
    row_spec = pl.BlockSpec((tq, HEAD), lambda h, g: (g, h))
    return q_spec, k_spec, v_spec, b_spec, row_spec


def _na_scores(q, k_ref, b_ref, g, n_groups, rows, nctx):
    first_row = _na_group(g, n_groups, rows)[1]
    win = pl.ds(pl.multiple_of(nctx + first_row * GRID_W, GRID_W), NA_SPAN * GRID_W)
    s_c = _dot(q, k_ref[0:nctx, :], NT) * SCALE
    s_w = _dot(q, k_ref[win, :], NT) * SCALE + b_ref[...]
    return s_c, s_w, win


def attn_odd_fwd(name, qkv, bias_s, nctx, carry=None):
    T = qkv.shape[0]
    S = T - nctx
    rows = S // GRID_W
    n_groups = rows // NA_GROUP
    q_spec, k_spec, v_spec, b_spec, row_spec = _na_specs(T, nctx, n_groups, rows)

    def body(q_ref, k_ref, v_ref, b_ref, o_ref, st_ref):
        s_c, s_w, win = _na_scores(q_ref[...], k_ref, b_ref, pl.program_id(1), n_groups, rows, nctx)
        (p_c, p_w), _, (m, inv) = _softmax_parts([s_c, s_w])
        st_ref[:, 0:1] = m
        st_ref[:, 1:2] = inv
        o_ref[...] = (_dot(p_c, v_ref[0:nctx, :], NN) + _dot(p_w, v_ref[win, :], NN)).astype(BF16)

    tq = NA_GROUP * GRID_W
    res = _call(name, body, (16, n_groups), (qkv, qkv, qkv, bias_s), [q_spec, k_spec, v_spec, b_spec],
                [jax.ShapeDtypeStruct((S, 16 * HEAD), BF16), jax.ShapeDtypeStruct((16, n_groups, tq, 2), F32)],
                [row_spec, _stats_spec(tq)], [], ("parallel", "arbitrary"), carry)
    return _split(res, 2, carry)


def attn_odd_bwd(name, qkv, bias_s, do, stats, nctx, carry=None):
    T = qkv.shape[0]
    S = T - nctx
    rows = S // GRID_W
    n_groups = rows // NA_GROUP
    q_spec, k_spec, v_spec, b_spec, row_spec = _na_specs(T, nctx, n_groups, rows)

    def body(q_ref, k_ref, v_ref, b_ref, do_ref, st_ref, dq_ref, dk_ref, dv_ref, db_ref):
        g = pl.program_id(1)
        q = q_ref[...]
        dov = do_ref[...]

        @pl.when(g == 0)
        def _():
            dk_ref[...] = jnp.zeros_like(dk_ref)
            dv_ref[...] = jnp.zeros_like(dv_ref)

        s_c, s_w, win = _na_scores(q, k_ref, b_ref, g, n_groups, rows, nctx)
        (p_c, p_w), _, _ = _softmax_parts([s_c, s_w], None, (st_ref[:, 0:1], st_ref[:, 1:2]))
        dp_c = _dot(dov, v_ref[0:nctx, :], NT)
        dp_w = _dot(dov, v_ref[win, :], NT)
        row = jnp.sum(p_c * dp_c, axis=-1, keepdims=True) + jnp.sum(p_w * dp_w, axis=-1, keepdims=True)
        dsw = p_w * (dp_w - row)
        first_visit = jnp.logical_or(g <= 1, g == n_groups - 1)

        @pl.when(first_visit)
        def _():
            db_ref[...] = dsw

        @pl.when(jnp.logical_not(first_visit))
        def _():
            db_ref[...] += dsw

        ds_c = (p_c * (dp_c - row) * SCALE).astype(BF16)
        ds_w = (dsw * SCALE).astype(BF16)
        dq_ref[...] = _dot(ds_c, k_ref[0:nctx, :], NN) + _dot(ds_w, k_ref[win, :], NN)
        dk_ref[0:nctx, :] += _dot(ds_c, q, TN)
        dk_ref[win, :] += _dot(ds_w, q, TN)
        dv_ref[0:nctx, :] += _dot(p_c, dov, TN)
        dv_ref[win, :] += _dot(p_w, dov, TN)

    kv_out = pl.BlockSpec((T, HEAD), lambda h, g: (0, h))
    res = _call(name, body, (16, n_groups), (qkv, qkv, qkv, bias_s, do, stats),
                [q_spec, k_spec, v_spec, b_spec, row_spec, _stats_spec(NA_GROUP * GRID_W)],
                [jax.ShapeDtypeStruct((S, 16 * HEAD), F32), jax.ShapeDtypeStruct((T, 16 * HEAD), F32),
                 jax.ShapeDtypeStruct((T, 16 * HEAD), F32), jax.ShapeDtypeStruct(bias_s.shape, F32)],
                [row_spec, kv_out, kv_out, b_spec], [], ("arbitrary", "arbitrary"), carry)
    return _split(res, 4, carry)


def _na_onehots():
    o = np.arange(NA_KH)[:, None]
    i = np.arange(NA_KH)[None, :]
    a = i - o + NA_KH - 1
    A = (a[..., None] == np.arange(2 * NA_KH - 1)).astype(np.float32)
    qc = np.arange(GRID_W)[:, None]
    kc = np.arange(GRID_W)[None, :]
    b = np.clip(kc - qc + NA_KW - 1, 0, 2 * NA_KW - 2)
    cs = np.clip(qc - NA_KW // 2, 0, GRID_W - NA_KW)
    valid = (kc >= cs) & (kc < cs + NA_KW)
    B = ((b[..., None] == np.arange(2 * NA_KW - 1)) & valid[..., None]).astype(np.float32)
    return A, B, valid


def na_bias_table(rpb):
    A, B, valid = _na_onehots()
    hp = lax.Precision.HIGHEST
    t = jnp.einsum('hab,oia->hoib', rpb, jnp.asarray(A), precision=hp)
    bias = jnp.einsum('hoib,qkb->hoqik', t, jnp.asarray(B), precision=hp)
    bias = jnp.where(jnp.asarray(valid)[None, None, :, None, :], bias, NEG)
    return bias.reshape(rpb.shape[0], NA_KH, GRID_W, NA_KH * GRID_W)


def na_bias_grad(name, dbias8):
    A, B, _ = _na_onehots()
    H = dbias8.shape[0]
    nb, na = 2 * NA_KW - 1, 2 * NA_KH - 1
    d = dbias8.reshape(H, NA_KH, GRID_W, NA_KH, GRID_W).transpose(0, 1, 3, 2, 4)
    d = d.reshape(H * NA_KH * NA_KH, GRID_W * GRID_W)
    Bp = np.zeros((GRID_W * GRID_W, 128), np.float32)
    Bp[:, :nb] = B.reshape(GRID_W * GRID_W, nb)
    Ap = np.zeros((16, NA_KH * NA_KH), np.float32)
    Ap[:na] = A.reshape(NA_KH * NA_KH, na).T
    rows_per_head = NA_KH * NA_KH

    def split3(x):
        hi = x.astype(BF16)
        r1 = x - hi.astype(F32)
        mid = r1.astype(BF16)
        return hi, mid, (r1 - mid.astype(F32)).astype(BF16)

    def body(d_ref, b_ref, a_ref, o_ref):
        bm, am = b_ref[...], a_ref[...]
        g = sum(lax.dot_general(p, bm, NN, preferred_element_type=F32) for p in split3(d_ref[...]))
        o_ref[...] = sum(lax.dot_general(am, p, NN, preferred_element_type=F32) for p in split3(g))

    out = pl.pallas_call(
        body, grid=(H,),
        in_specs=[pl.BlockSpec((rows_per_head, GRID_W * GRID_W), lambda h: (h, 0)),
                  pl.BlockSpec((GRID_W * GRID_W, 128), lambda h: (0, 0)),
                  pl.BlockSpec((16, rows_per_head), lambda h: (0, 0))],
        out_specs=pl.BlockSpec((None, 16, 128), lambda h: (h, 0, 0)),
        out_shape=jax.ShapeDtypeStruct((H, 16, 128), F32),
        compiler_params=_cparams(("parallel",)), name=name)(d, jnp.asarray(Bp, BF16), jnp.asarray(Ap, BF16))
    return out[:, :na, :nb]


def _vmem_call(name, fn, out_shape, *arrays):
    def body(*refs):
        n = len(arrays)
        res = fn(*[r[...] for r in refs[:n]])
        if not isinstance(res, (tuple, list)):
            res = (res,)
        for o, v in zip(refs[n:], res):
            o[...] = v
    return pl.pallas_call(body, out_shape=out_shape, name=name,
                          compiler_params=pltpu.CompilerParams(vmem_limit_bytes=VMEM_LIMIT))(*arrays)


def _silu(v):
    return v / (1.0 + jnp.exp(-v))


def _adamw_math(w, g, m, v):
    m2 = ADAM_B1 * m + (1.0 - ADAM_B1) * g
    v2 = ADAM_B2 * v + (1.0 - ADAM_B2) * (g * g)
    m_hat = m2 / (1.0 - ADAM_B1 ** ADAM_STEP)
    v_hat = v2 / (1.0 - ADAM_B2 ** ADAM_STEP)
    delta = -ADAM_LR * (m_hat / (jnp.sqrt(v_hat) + ADAM_EPS) + ADAM_WD * w)
    return delta, m2, v2


def _ew_tile(R, C):
    return _tile(R, max(64, (262144 // C) // 64 * 64))


def adamw_rows(name, w, g, m, v, extra_g=None):
    R, C = w.shape
    tr = _ew_tile(R, C)
    extra_g = list(extra_g or [])
    ne = len(extra_g)

    def body(*refs):
        w_ref, g_ref, m_ref, v_ref = refs[:4]
        gs = g_ref[...]
        for e in refs[4:4 + ne]:
            gs = gs + e[...].astype(F32)
        go, do, mo, vo = refs[4 + ne:]
        d, m2, v2 = _adamw_math(w_ref[...], gs, m_ref[...], v_ref[...])
        go[...] = gs
        do[...] = d
        mo[...] = m2
        vo[...] = v2

    spec = pl.BlockSpec((tr, C), lambda i: (i, 0))
    return pl.pallas_call(
        body, grid=(R // tr,), in_specs=[spec] * (4 + ne), out_specs=[spec] * 4,
        out_shape=[jax.ShapeDtypeStruct((R, C), F32)] * 4,
        compiler_params=_cparams(("parallel",)), name=name)(w, g, m, v, *extra_g)


def rs_chip_sum(name, g8, sib4, where):
    _, R, C = g8.shape
    tr = _ew_tile(R, C)

    def body(s_ref, g_ref, b_ref, o_ref):
        o_ref[...] = (g_ref[...] + b_ref[...]).astype(BF16)

    def chip(q, s):
        return (s[1] + 1 + q) % 4

    blk = (None, tr, C)
    grid_spec = pltpu.PrefetchScalarGridSpec(
        num_scalar_prefetch=1, grid=(3, R // tr),
        in_specs=[pl.BlockSpec(blk, lambda q, i, s: (s[0] + chip(q, s), i, 0)),
                  pl.BlockSpec(blk, lambda q, i, s: (chip(q, s), i, 0))],
        out_specs=pl.BlockSpec(blk, lambda q, i, s: (chip(q, s), i, 0)))
    return pl.pallas_call(body, grid_spec=grid_spec, out_shape=jax.ShapeDtypeStruct((4, R, C), BF16),
                          compiler_params=_cparams(("parallel", "parallel")), name=name)(where, g8, sib4)


def adamw_rs(name, w, g8, sib4, rem3, m, v, idx, layer, prev=None):
    L, R, C = w.shape
    tr = _ew_tile(R, C)

    def body(s_ref, w_ref, g_ref, sb_ref, r0_ref, r1_ref, r2_ref, m_ref, v_ref, *rest):
        go, do, mo, vo = rest[-4:]
        gs = g_ref[...] + sb_ref[...]
        for r_ref in (r0_ref, r1_ref, r2_ref):
            gs = gs + r_ref[...].astype(F32)
        d, m2, v2 = _adamw_math(w_ref[...], gs, m_ref[...], v_ref[...])
        go[...] = gs
        do[...] = d
        mo[...] = m2
        vo[...] = v2

    blk = (None, tr, C)
    mine = pl.BlockSpec(blk, lambda i, s: (layer, i, 0))

    def rem(k):
        return pl.BlockSpec(blk, lambda i, s: (k, i, 0))

    prev = list(prev or [])
    grid_spec = pltpu.PrefetchScalarGridSpec(
        num_scalar_prefetch=1, grid=(R // tr,),
        in_specs=[mine, pl.BlockSpec(blk, lambda i, s: (s[0], i, 0)), pl.BlockSpec(blk, lambda i, s: (s[1], i, 0)),
                  rem(0), rem(1), rem(2), mine, mine] + [pl.BlockSpec(memory_space=pl.ANY)] * len(prev),
        out_specs=[mine] * 4)
    return pl.pallas_call(body, grid_spec=grid_spec, out_shape=[jax.ShapeDtypeStruct((L, R, C), F32)] * 4,
                          input_output_aliases={9 + k: k for k in range(len(prev))},
                          compiler_params=_cparams(("parallel",)), name=name)(
                              idx, w, g8, sib4, rem3, rem3, rem3, m, v, *prev)


def _me():
    x, y, c = lax.axis_index("x"), lax.axis_index("y"), lax.axis_index("c")
    return x, y, c


def _flip(v, bit):
    return 1 - v if bit else v


def ag_small(name, x, with_sum=False):
    R, C = x.shape

    def body(x_ref, out_ref, *rest):
        if with_sum:
            sum_ref, send_sems, recv_sems, lsem = rest
        else:
            send_sems, recv_sems, lsem = rest
        mx, my, mc = _me()
        me = 4 * mx + 2 * my + mc
        local = pltpu.make_async_copy(x_ref, out_ref.at[me], lsem)
        local.start()
        sends = []
        for k in range(1, NDEV):
            peer = (_flip(mx, k & 4), _flip(my, k & 2), _flip(mc, k & 1))
            cp = pltpu.make_async_remote_copy(src_ref=x_ref, dst_ref=out_ref.at[me], send_sem=send_sems.at[k - 1],
                                              recv_sem=recv_sems.at[k - 1], device_id=peer, device_id_type=MESH)
            cp.start()
            sends.append(cp)
        for k in range(1, NDEV):
            px, py, pc = _flip(mx, k & 4), _flip(my, k & 2), _flip(mc, k & 1)
            pltpu.make_async_remote_copy(src_ref=x_ref, dst_ref=out_ref.at[4 * px + 2 * py + pc],
                                         send_sem=send_sems.at[k - 1], recv_sem=recv_sems.at[k - 1],
                                         device_id=(px, py, pc), device_id_type=MESH).wait_recv()
        for cp in sends:
            cp.wait_send()
        local.wait()
        if with_sum:
            acc = out_ref[0]
            for d in range(1, NDEV):
                acc = acc + out_ref[d]
            sum_ref[...] = acc

    out_shape = [jax.ShapeDtypeStruct((NDEV, R, C), F32)]
    if with_sum:
        out_shape.append(jax.ShapeDtypeStruct((R, C), F32))
    vm = pl.BlockSpec(memory_space=pltpu.VMEM)
    res = pl.pallas_call(
        body, out_shape=out_shape, in_specs=[vm], out_specs=[vm] * len(out_shape),
        scratch_shapes=[pltpu.SemaphoreType.DMA((NDEV - 1,)), pltpu.SemaphoreType.DMA((NDEV - 1,)),
                        pltpu.SemaphoreType.DMA],
        compiler_params=pltpu.CompilerParams(vmem_limit_bytes=VMEM_LIMIT), name=name)(x)
    return res if with_sum else res[0]


def ag_big(name, shards):
    n = len(shards)

    def body(*refs):
        ins, outs = refs[:n], refs[n:2 * n]
        send_sems, recv_sems, lsems = refs[2 * n:]
        mx, my, mc = _me()
        me = (mx, my, mc)
        sibling = (mx, my, 1 - mc)
        chips = [(1 - mx, my), (mx, 1 - my), (1 - mx, 1 - my)]

        def idx(p):
            return 4 * p[0] + 2 * p[1] + p[2]

        def copy(t, k, block, to, src=None):
            dst = outs[t].at[idx(block)]
            return pltpu.make_async_remote_copy(
                src_ref=dst if src is None else src, dst_ref=dst, send_sem=send_sems.at[7 * t + k],
                recv_sem=recv_sems.at[7 * t + k], device_id=to, device_id_type=MESH)

        started = []
        locals_ = []
        for t in range(n):
            mine = pltpu.make_async_copy(ins[t], outs[t].at[idx(me)], lsems.at[t])
            mine.start()
            locals_.append(mine)
            first = [copy(t, 0, me, sibling, src=ins[t])]
            first += [copy(t, 1 + j, me, (*chip, mc), src=ins[t]) for j, chip in enumerate(chips)]
            for cp in first:
                cp.start()
            started += first
        for t in range(n):
            for j, chip in enumerate(chips):
                copy(t, 1 + j, (*chip, mc), me).wait_recv()
                fwd = copy(t, 4 + j, (*chip, mc), sibling)
                fwd.start()
                started.append(fwd)
        for t in range(n):
            copy(t, 0, sibling, me).wait_recv()
            for j, chip in enumerate(chips):
                copy(t, 4 + j, (*chip, 1 - mc), me).wait_recv()
        for cp in started:
            cp.wait_send()
        for mine in locals_:
            mine.wait()

    anyspec = pl.BlockSpec(memory_space=pl.ANY)
    return pl.pallas_call(
        body, out_shape=[jax.ShapeDtypeStruct((NDEV,) + s.shape, s.dtype) for s in shards],
        in_specs=[anyspec] * n, out_specs=[anyspec] * n,
        scratch_shapes=[pltpu.SemaphoreType.DMA((7 * n,)), pltpu.SemaphoreType.DMA((7 * n,)),
                        pltpu.SemaphoreType.DMA((n,))],
        name=name)(*shards)


def _idx(p):
    return 4 * p[0] + 2 * p[1] + p[2]


def _remote(src, dst, ss, rs, k, to):
    return pltpu.make_async_remote_copy(src_ref=src, dst_ref=dst, send_sem=ss.at[k], recv_sem=rs.at[k],
                                        device_id=to, device_id_type=MESH)


def ex_ag_chips(shards):
    n = len(shards)

    def copies(ci, co, ss, rs, base):
        mx, my, mc = _me()
        me = (mx, my, mc)
        peers = [(mx, my, 1 - mc), (1 - mx, my, mc), (mx, 1 - my, mc), (1 - mx, 1 - my, mc)]
        sends, recvs, local = [], [], []
        for t in range(n):
            b = base + 5 * t
            for k, peer in enumerate(peers):
                sends.append(_remote(ci[t], co[t].at[_idx(me)], ss, rs, b + k, peer))
                recvs.append(_remote(ci[t], co[t].at[_idx(peer)], ss, rs, b + k, peer))
            local.append(pltpu.make_async_copy(ci[t], co[t].at[_idx(me)], ss.at[b + 4]))
        return sends, recvs, local

    def start(ci, co, ss, rs, base):
        sends, _, local = copies(ci, co, ss, rs, base)
        for cp in local + sends:
            cp.start()

    def finish(ci, co, ss, rs, base):
        sends, recvs, local = copies(ci, co, ss, rs, base)
        for cp in recvs:
            cp.wait_recv()
        for cp in sends:
            cp.wait_send()
        for cp in local:
            cp.wait()

    outs = [jax.ShapeDtypeStruct((NDEV,) + s.shape, s.dtype) for s in shards]
    return Exchange(shards, outs, {}, 5 * n, start, finish)


def ex_ag_sibling(bufs):
    n = len(bufs)

    def copies(co, ss, rs, base):
        mx, my, mc = _me()
        sibling = (mx, my, 1 - mc)
        chips = [(1 - mx, my), (mx, 1 - my), (1 - mx, 1 - my)]
        sends, recvs = [], []
        for t in range(n):
            for j, chip in enumerate(chips):
                mine, theirs = co[t].at[_idx((*chip, mc))], co[t].at[_idx((*chip, 1 - mc))]
                sends.append(_remote(mine, mine, ss, rs, base + 3 * t + j, sibling))
                recvs.append(_remote(mine, theirs, ss, rs, base + 3 * t + j, sibling))
        return sends, recvs

    def start(ci, co, ss, rs, base):
        for cp in copies(co, ss, rs, base)[0]:
            cp.start()

    def finish(ci, co, ss, rs, base):
        sends, recvs = copies(co, ss, rs, base)
        for cp in recvs:
            cp.wait_recv()
        for cp in sends:
            cp.wait_send()

    outs = [jax.ShapeDtypeStruct(b.shape, b.dtype) for b in bufs]
    return Exchange(bufs, outs, {t: t for t in range(n)}, 3 * n, start, finish)


def ex_rs_sibling(grads):
    n = len(grads)

    def copies(ci, co, ss, rs, base):
        mx, my, mc = _me()
        return [_remote(ci[t].at[pl.ds((1 - mc) * 4, 4)], co[t], ss, rs, base + t, (mx, my, 1 - mc)) for t in range(n)]

    def start(ci, co, ss, rs, base):
        for cp in copies(ci, co, ss, rs, base):
            cp.start()

    def finish(ci, co, ss, rs, base):
        for cp in copies(ci, co, ss, rs, base):
            cp.wait()

    outs = [jax.ShapeDtypeStruct((4,) + g.shape[1:], g.dtype) for g in grads]
    return Exchange(grads, outs, {}, n, start, finish)


def ex_rs_chips(parts):
    n = len(parts)

    def copies(ci, co, ss, rs, base):
        mx, my, mc = _me()
        cps = []
        for t in range(n):
            for k in range(1, 4):
                px, py = _flip(mx, k & 2), _flip(my, k & 1)
                cps.append(_remote(ci[t].at[2 * px + py], co[t].at[k - 1], ss, rs, base + 3 * t + k - 1, (px, py, mc)))
        return cps

    def start(ci, co, ss, rs, base):
        for cp in copies(ci, co, ss, rs, base):
            cp.start()

    def finish(ci, co, ss, rs, base):
        for cp in copies(ci, co, ss, rs, base):
            cp.wait()

    outs = [jax.ShapeDtypeStruct((3,) + p.shape[1:], p.dtype) for p in parts]
    return Exchange(parts, outs, {}, 3 * n, start, finish)


def run_exchanges(name, xs):
    x = merge_exchanges(xs)
    n_ci, n_co = len(x.ins), len(x.out_shapes)

    def body(*refs):
        ci, co = refs[:n_ci], refs[n_ci:n_ci + n_co]
        ss, rs = refs[n_ci + n_co:]
        x.start(ci, co, ss, rs, 0)
        x.finish(ci, co, ss, rs, 0)

    hbm = pl.BlockSpec(memory_space=pl.ANY)
    return pl.pallas_call(
        body, out_shape=x.out_shapes, in_specs=[hbm] * n_ci, out_specs=[hbm] * n_co, input_output_aliases=x.aliases,
        scratch_shapes=[pltpu.SemaphoreType.DMA((x.n_sems,)), pltpu.SemaphoreType.DMA((x.n_sems,))], name=name)(*x.ins)


def _rope_tables(S, nctx):
    t = jnp.arange(S)
    row = (t // GRID_W).astype(F32)
    col = (t % GRID_W).astype(F32)
    pairs = HEAD // 4
    inv = ROPE_THETA ** (-jnp.arange(pairs, dtype=F32) / pairs)
    ang_r = row[:, None] * inv
    ang_c = col[:, None] * inv
    ang = jnp.concatenate([ang_r, ang_r, ang_c, ang_c], axis=-1)
    cos = jnp.concatenate([jnp.ones((nctx, HEAD), F32), jnp.cos(ang)], axis=0)
    sin = jnp.concatenate([jnp.zeros((nctx, HEAD), F32), jnp.sin(ang)], axis=0)
    lane = jnp.arange(HEAD)[None, :]
    first = (lane & 32) == 0
    return cos, jnp.where(first, -sin, 0.0), jnp.where(first, 0.0, sin)


def _pad_rows(v, rows):
    v = v.reshape(-1).astype(F32)
    return jnp.pad(v, (0, rows * 128 - v.shape[0])).reshape(rows, 128)


def _rows8(n):
    return -(-n // 1024) * 8


def kernel(x, c, ctx, c_ctx, ada_w, ada_b, norm_w, mlp_w1, mlp_w2, ev_w_in, ev_w_out, ev_q_norm, ev_k_norm, ev_sink, od_w_in, od_w_out, od_rpb, final_norm_w, loss_target, m_c_ctx, m_ada_w, m_ada_b, m_norm_w, m_mlp_w1, m_mlp_w2, m_ev_w_in, m_ev_w_out, m_ev_q_norm, m_ev_k_norm, m_ev_sink, m_od_w_in, m_od_w_out, m_od_rpb, m_final_norm_w, v_c_ctx, v_ada_w, v_ada_b, v_norm_w, v_mlp_w1, v_mlp_w2, v_ev_w_in, v_ev_w_out, v_ev_q_norm, v_ev_k_norm, v_ev_sink, v_od_w_in, v_od_w_out, v_od_rpb, v_final_norm_w):
    S, D = x.shape[1], x.shape[2]
    NC = ctx.shape[1]
    T = NC + S
    assert NC == ROW_TILE and S % GRID_W == 0
    ada_cols = ada_w.shape[2]
    nw_cols = norm_w.shape[2]
    me = 4 * lax.axis_index("x") + 2 * lax.axis_index("y") + lax.axis_index("c")

    pack1 = jnp.concatenate([_pad_rows(c, _rows8(D)), _pad_rows(norm_w, _rows8(4 * nw_cols))], axis=0)
    g1 = ag_small("ag_c_normw", pack1)
    c_all = g1[:, :D // 128].reshape(NDEV, D)
    nw_rows = _rows8(D)
    nw = g1[:, nw_rows:nw_rows + 4 * nw_cols // 128].reshape(NDEV, 2, 2, nw_cols)
    nw = nw.transpose(1, 2, 0, 3).reshape(2, 2, D)
    cin = jnp.concatenate([c_all, jnp.broadcast_to(c_ctx[None], (NDEV, D))], axis=0)
    act = _vmem_call("silu_c", lambda v: _silu(v).astype(BF16), jax.ShapeDtypeStruct((2 * NDEV, D), BF16), cin)
    ada_b_loc = lax.dynamic_slice_in_dim(ada_b, me * ada_cols, ada_cols, axis=1)
    mods = [mm_nn(f"mod{i}", act, ada_w[i], _epi_bias, [F32], extras=(ada_b_loc[i:i + 1],), extra_kinds=('n',))[0]
            for i in range(2)]
    gm = ag_small("ag_mod", jnp.concatenate(mods, axis=1))
    gm = gm.reshape(NDEV, 2 * NDEV, 2, ada_cols).transpose(2, 1, 0, 3).reshape(2, 2 * NDEV, NDEV * ada_cols)
    mod_lat = lax.dynamic_index_in_dim(gm, me, axis=1, keepdims=False)
    mod_ctx = gm[:, NDEV]
    mod2 = jnp.stack([mod_ctx, mod_lat], axis=1).reshape(2, 2, 6, D)

    def chunk(i, j):
        return mod2[i, :, j, :]

    def b16(w):
        return w.astype(BF16)

    (w_in_e,) = ag_big("ag_weights_l0_qkv", [b16(ev_w_in[0])])
    w_in_e = w_in_e.transpose(1, 0, 2).reshape(D, -1)

    cos, sa, sb = _rope_tables(S, NC)
    bias8 = na_span_bias(na_bias_table(od_rpb[0]))
    sink = ev_sink[0]
    TQ_F, TQ_B = 256, 256

    X0 = jnp.concatenate([ctx[0], x[0]], axis=0)
    h_a = norm_mod("l0_norm1", X0, nw[0, 0][None], chunk(0, 0), chunk(0, 1), NC)
    (qkv0,), (w_out_e_half,) = mm_nn("l0_qkv", h_a, w_in_e, _epi_store(F32), [F32], tn_cap=1024,
                                     carry=[ex_ag_chips([b16(ev_w_out[0])])])
    qkvh0 = prep_even("l0_prep", qkv0, ev_q_norm, ev_k_norm, cos, sa, sb)
    (o0, stats0), (w1_0_half, w2_0_half, w_out_o_half, w_out_e) = attn_even_fwd(
        "l0_attn", qkvh0, sink, NC, TQ_F,
        carry=[ex_ag_chips([b16(mlp_w1[0]), b16(mlp_w2[0]), b16(od_w_out[0])]), ex_ag_sibling([w_out_e_half])])
    w_out_e = w_out_e.reshape(-1, D)
    tm0 = _tile(T, 1100)
    (X1, y0), (w1_0, w2_0) = mm_nn("l0_out", o0, w_out_e, _epi_resid_gate(NC, tm0), [F32, F32],
                                   extras=(X0, chunk(0, 2)), extra_kinds=('mn', 'n'),
                                   carry=[ex_ag_sibling([w1_0_half, w2_0_half])])
    h_b = norm_mod("l0_norm2", X1, nw[0, 1][None], chunk(0, 3), chunk(0, 4), NC)
    (a0, r0), (w_in_o_half, w_out_o) = mm_nn(
        "l0_up", h_b, w1_0, _epi_relu2, [BF16, BF16], tn_cap=1024,
        carry=[ex_ag_chips([b16(od_w_in[0])]), ex_ag_sibling([w_out_o_half])])
    (X2, z0), (w1_1_half, w_in_o) = mm_nn(
        "l0_down", a0, w2_0.reshape(-1, D), _epi_resid_gate(NC, tm0), [F32, F32], extras=(X1, chunk(0, 5)),
        extra_kinds=('mn', 'n'), tn_cap=1024, carry=[ex_ag_chips([b16(mlp_w1[1])]), ex_ag_sibling([w_in_o_half])])
    w_out_o = w_out_o.reshape(-1, D)

    h_c = norm_mod("l1_norm1", X2, nw[1, 0][None], chunk(1, 0), chunk(1, 1), NC)
    (qkv1,), (w1_1,) = mm_nn("l1_qkv", h_c, w_in_o, _epi_store(BF16), [BF16], tn_cap=768,
                             carry=[ex_ag_sibling([w1_1_half])])
    (o1, stats1), (w2_1_half,) = attn_odd_fwd("l1_attn", qkv1, bias8, NC, carry=[ex_ag_chips([b16(mlp_w2[1])])])
    X2l = X2[NC:]
    tm1 = _tile(S, 1100)
    (X3, y1), (w2_1,) = mm_nn("l1_out", o1, w_out_o, _epi_resid_gate(0, tm1), [F32, F32],
                              extras=(X2l, chunk(1, 2)), extra_kinds=('mn', 'n'),
                              carry=[ex_ag_sibling([w2_1_half])])
    h_d = norm_mod("l1_norm2", X3, nw[1, 1][None], chunk(1, 3), chunk(1, 4), 0)
    a1, r1 = mm_nn("l1_up", h_d, w1_1, _epi_relu2, [BF16, BF16], tn_cap=1024)
    X4, z1 = mm_nn("l1_down", a1, w2_1.reshape(-1, D), _epi_resid_gate(0, tm1), [F32, F32], extras=(X3, chunk(1, 5)),
                   extra_kinds=('mn', 'n'), tn_cap=1024)
    dX4, loss_p, dfw_p, dz1, pg2_1 = final_loss("final_loss", X4, final_norm_w[None], loss_target[0], z1, chunk(1, 5))
    w_in = [w_in_e, w_in_o]
    w_out = [w_out_e, w_out_o]
    w1 = [w1_0, w1_1]
    w2 = [w2_0.reshape(-1, D), w2_1.reshape(-1, D)]

    mc4 = (lax.axis_index("c") * 4).astype(jnp.int32)
    my_chip = (2 * lax.axis_index("x") + lax.axis_index("y")).astype(jnp.int32)

    def chip_sum(tag, g8, sib4):
        return rs_chip_sum(f"rs_chip_sum_{tag}", g8, sib4, jnp.stack([mc4, my_chip]))

    du1 = mm_nt("l1_down_dx", dz1, w2[1], _epi_mul2r, BF16, extras=(r1,))
    g_w1_1 = mm_tn("l1_up_dw", h_d, du1, 1)
    g_w2_1, (sib_w1_1,) = mm_tn("l1_down_dw", a1, dz1, 0, carry=[ex_rs_sibling([g_w1_1])])
    dh_d, (rem_w1_1, sib_w2_1) = mm_nt(
        "l1_up_dx", du1, w1[1], _epi_store(F32), F32,
        carry=[ex_rs_chips([chip_sum("w1_1", g_w1_1, sib_w1_1)]), ex_rs_sibling([g_w2_1])])
    dX3, pn2_1, dy1, pg1_1 = norm_bwd("l1_norm2_bwd", X3, dh_d, dX4, nw[1, 1][None], chunk(1, 4), 0,
                                      gate=(y1, chunk(1, 2)))
    do1 = mm_nt("l1_out_dx", dy1, w_out[1], _epi_store(BF16), BF16)
    g_wout_1 = mm_tn("l1_out_dw", o1, dy1, 0)
    (dq1, dk1, dv1, dbias8), (rem_w2_1, sib_wout_1) = attn_odd_bwd(
        "l1_attn_bwd", qkv1, bias8, do1, stats1, NC,
        carry=[ex_rs_chips([chip_sum("w2_1", g_w2_1, sib_w2_1)]), ex_rs_sibling([g_wout_1])])
    dqkv1 = jnp.concatenate([jnp.pad(dq1, ((NC, 0), (0, 0))), dk1, dv1], axis=1).astype(BF16)
    dh_c, (rem_wout_1,) = mm_nt("l1_qkv_dx", dqkv1, w_in[1], _epi_store(F32), F32,
                                carry=[ex_rs_chips([chip_sum("wout_1", g_wout_1, sib_wout_1)])])
    g_win_1 = mm_tn("l1_qkv_dw", h_c, dqkv1, 1)
    dX2, pn1_1, dz0, pg2_0 = norm_bwd("l1_norm1_bwd", X2, dh_c, dX3, nw[1, 0][None], chunk(1, 1), NC, dres_skip=NC,
                                      gate=(z0, chunk(0, 5)))
    d_rpb = na_bias_grad("rpb_grad", na_span_bias_grad(dbias8))

    du0, (sib_win_1,) = mm_nt("l0_down_dx", dz0, w2[0], _epi_mul2r, BF16, extras=(r0,),
                              carry=[ex_rs_sibling([g_win_1])])
    g_w1_0, (rem_win_1,) = mm_tn("l0_up_dw", h_b, du0, 1,
                                 carry=[ex_rs_chips([chip_sum("win_1", g_win_1, sib_win_1)])])
    g_w2_0, (sib_w1_0,) = mm_tn("l0_down_dw", a0, dz0, 0, carry=[ex_rs_sibling([g_w1_0])])
    dh_b, (rem_w1_0, sib_w2_0) = mm_nt(
        "l0_up_dx", du0, w1[0], _epi_store(F32), F32,
        carry=[ex_rs_chips([chip_sum("w1_0", g_w1_0, sib_w1_0)]), ex_rs_sibling([g_w2_0])])
    dX1, pn2_0, dy0, pg1_0 = norm_bwd("l0_norm2_bwd", X1, dh_b, dX2, nw[0, 1][None], chunk(0, 4), NC,
                                      gate=(y0, chunk(0, 2)))
    do0 = mm_nt("l0_out_dx", dy0, w_out[0], _epi_store(BF16), BF16)
    g_wout_0 = mm_tn("l0_out_dw", o0, dy0, 0)
    (dq0, dk0, dv0, dsink_p), (rem_w2_0, sib_wout_0) = attn_even_bwd(
        "l0_attn_bwd", qkvh0, sink, do0, stats0, NC, TQ_B,
        carry=[ex_rs_chips([chip_sum("w2_0", g_w2_0, sib_w2_0)]), ex_rs_sibling([g_wout_0])])
    dqkv0, pqk = prep_even_bwd("l0_prep_bwd", qkv0, dq0, dk0, dv0, ev_q_norm, ev_k_norm, cos, sa, sb)
    g_win_0, (rem_wout_0,) = mm_tn("l0_qkv_dw", h_a, dqkv0, 1,
                                   carry=[ex_rs_chips([chip_sum("wout_0", g_wout_0, sib_wout_0)])])
    dh_a, (sib_win_0,) = mm_nt("l0_qkv_dx", dqkv0, w_in[0], _epi_store(F32), F32, carry=[ex_rs_sibling([g_win_0])])
    (dx_lat, pn1_0), (rem_win_0,) = norm_bwd(
        "l0_norm1_bwd", X0, dh_a, dX1, nw[0, 0][None], chunk(0, 1), NC, out_skip=NC,
        carry=[ex_rs_chips([chip_sum("win_0", g_win_0, sib_win_0)])])
    grad_x = dx_lat[None]

    def dmod(grp, pn1, pg1, pn2, pg2):
        return jnp.concatenate([pn1[grp], pn1[2 + grp], pg1[grp], pn2[grp], pn2[2 + grp], pg2[grp]])

    dmod_lat = jnp.stack([dmod(1, pn1_0, pg1_0, pn2_0, pg2_0), dmod(1, pn1_1, pg1_1, pn2_1, pg2_1)])
    dmod_ctx = jnp.stack([dmod(0, pn1_0, pg1_0, pn2_0, pg2_0), dmod(0, pn1_1, pg1_1, pn2_1, pg2_1)])
    dnw_p = jnp.stack([pn1_0[4], pn2_0[4], pn1_1[4], pn2_1[4]])
    pieces = [dmod_lat, dmod_ctx, dnw_p, pqk[0], pqk[1], dsink_p[8:, 0, 0], d_rpb, dfw_p[0], loss_p[0, 0]]
    sizes = [int(np.prod(p.shape)) for p in pieces]
    rows = [_rows8(s) for s in sizes]
    pack2 = jnp.concatenate([_pad_rows(p, r) for p, r in zip(pieces, rows)], axis=0)
    g2, s2 = ag_small("ag_small_grads", pack2, with_sum=True)
    offs = np.concatenate([[0], np.cumsum(rows)])

    def piece(arr, i, shape):
        return arr[..., offs[i]:offs[i + 1], :].reshape(arr.shape[:-2] + (-1,))[..., :sizes[i]].reshape(
            arr.shape[:-2] + shape)

    dmod_all = piece(g2, 0, (2, 6 * D))
    dmodc_sum = piece(s2, 1, (2, 6 * D))
    dnw_sum = piece(s2, 2, (2, 2, D))
    g_qn = piece(s2, 3, ev_q_norm.shape)
    g_kn = piece(s2, 4, ev_k_norm.shape)
    g_sink = piece(s2, 5, ev_sink.shape)
    g_rpb = piece(s2, 6, od_rpb.shape)
    g_fw = piece(s2, 7, final_norm_w.shape)
    loss = piece(s2, 8, ())

    dm16 = jnp.concatenate([dmod_all.transpose(1, 0, 2), dmodc_sum[:, None, :],
                            jnp.zeros((2, NDEV - 1, 6 * D), F32)], axis=1)
    dm16_loc = lax.dynamic_slice_in_dim(dm16.reshape(2, 2 * NDEV, NDEV, ada_cols), me, 1, axis=2)[:, :, 0, :]
    g_ada_b = _vmem_call("ada_b_grad", lambda v: jnp.sum(v, axis=1),
                         jax.ShapeDtypeStruct((2, 6 * D), F32), dm16)
    g_ada_w = []
    dact_p = None
    for i in range(2):
        dmb = dm16_loc[i].astype(BF16)
        g_ada_w.append(mm_tn(f"ada_w_grad{i}", act, dmb, None))
        part = mm_nt(f"ada_dact{i}", dmb, ada_w[i], _epi_store(F32), F32)
        dact_p = part if dact_p is None else dact_p + part
    _, dact = ag_small("ag_cctx", dact_p, with_sum=True)

    def cctx_grad(da, cc):
        sg = 1.0 / (1.0 + jnp.exp(-cc))
        return da[NDEV:NDEV + 1] * (sg * (1.0 + cc * (1.0 - sg)))

    g_cctx = _vmem_call("cctx_grad", cctx_grad, jax.ShapeDtypeStruct((1, D), F32), dact, c_ctx[None])[0]

    grads = [g_win_0, g_wout_0, g_w1_0, g_w2_0, g_win_1, g_wout_1, g_w1_1, g_w2_1]
    sib = [sib_win_0, sib_wout_0, sib_w1_0, sib_w2_0, sib_win_1, sib_wout_1, sib_w1_1, sib_w2_1]
    rem = [rem_win_0, rem_wout_0, rem_w1_0, rem_w2_0, rem_win_1, rem_wout_1, rem_w1_1, rem_w2_1]
    own_idx = jnp.stack([mc4 + my_chip, my_chip])

    def big(tag, w, m, v, ts):
        res = None
        for l, t in enumerate(ts):
            res = adamw_rs(f"adamw_{tag}_{l}", w, grads[t], sib[t], rem[t], m, v, own_idx, l, res)
        return tuple(res)

    r_ev_w_in = big('ev_w_in', ev_w_in, m_ev_w_in, v_ev_w_in, [0])
    r_ev_w_out = big('ev_w_out', ev_w_out, m_ev_w_out, v_ev_w_out, [1])
    r_mlp_w1 = big('mlp_w1', mlp_w1, m_mlp_w1, v_mlp_w1, [2, 6])
    r_mlp_w2 = big('mlp_w2', mlp_w2, m_mlp_w2, v_mlp_w2, [3, 7])
    r_od_w_in = big('od_w_in', od_w_in, m_od_w_in, v_od_w_in, [4])
    r_od_w_out = big('od_w_out', od_w_out, m_od_w_out, v_od_w_out, [5])

    g_ada = jnp.stack(g_ada_w)
    r_ada_w = adamw_rows("adamw_ada_w", ada_w.reshape(2 * D, ada_cols), g_ada.reshape(2 * D, ada_cols),
                         m_ada_w.reshape(2 * D, ada_cols), v_ada_w.reshape(2 * D, ada_cols))
    r_ada_w = tuple(u.reshape(2, D, ada_cols) for u in r_ada_w)

    g_nw_loc = lax.dynamic_slice_in_dim(dnw_sum, me * nw_cols, nw_cols, axis=2)
    small = [(c_ctx, g_cctx, m_c_ctx, v_c_ctx), (ada_b, g_ada_b, m_ada_b, v_ada_b),
             (norm_w, g_nw_loc, m_norm_w, v_norm_w), (ev_q_norm, g_qn, m_ev_q_norm, v_ev_q_norm),
             (ev_k_norm, g_kn, m_ev_k_norm, v_ev_k_norm), (ev_sink, g_sink, m_ev_sink, v_ev_sink),
             (od_rpb, g_rpb, m_od_rpb, v_od_rpb), (final_norm_w, g_fw, m_final_norm_w, v_final_norm_w)]
    srows = [_rows8(int(np.prod(w.shape))) for w, _, _, _ in small]
    packs = [jnp.concatenate([_pad_rows(tup[k], r) for tup, r in zip(small, srows)], axis=0) for k in range(4)]
    sres = adamw_rows("adamw_small", *packs)
    soffs = np.concatenate([[0], np.cumsum(srows)])

    def unpack(arr, i):
        w = small[i][0]
        return arr[soffs[i]:soffs[i + 1]].reshape(-1)[:int(np.prod(w.shape))].reshape(w.shape)

    sm = [[unpack(sres[k], i) for i in range(len(small))] for k in range(4)]

    def outs(k):
        big_k = {'ada_w': r_ada_w[k], 'mlp_w1': r_mlp_w1[k], 'mlp_w2': r_mlp_w2[k], 'ev_w_in': r_ev_w_in[k],
                 'ev_w_out': r_ev_w_out[k], 'od_w_in': r_od_w_in[k], 'od_w_out': r_od_w_out[k]}
        return (sm[k][0], big_k['ada_w'], sm[k][1], sm[k][2], big_k['mlp_w1'], big_k['mlp_w2'], big_k['ev_w_in'],
                big_k['ev_w_out'], sm[k][3], sm[k][4], sm[k][5], big_k['od_w_in'], big_k['od_w_out'], sm[k][6],
                sm[k][7])

    return (loss, grad_x, *outs(0), *outs(1), *outs(2), *outs(3))
```

```python
import numpy as np
import jax
import jax.numpy as jnp
from jax import lax
from jax.experimental import pallas as pl
from jax.experimental.pallas import tpu as pltpu

F32 = jnp.float32
BF16 = jnp.bfloat16
MESH = pl.DeviceIdType.MESH

NDEV = 8
HEAD = 128
GRID_W = 64
NA_KH, NA_KW = 8, 16
WINDOW = 128
ROPE_THETA = 10000.0
EPS = 1e-6
NEG = -1e30
SCALE = HEAD ** -0.5
ROW_TILE = 256
VMEM_LIMIT = 56 * 1024 * 1024

ADAM_LR, ADAM_B1, ADAM_B2, ADAM_EPS, ADAM_WD, ADAM_STEP = 0.001, 0.9, 0.999, 1e-08, 0.01, 10

NT = (((1,), (1,)), ((), ()))
NN = (((1,), (0,)), ((), ()))
TN = (((0,), (0,)), ((), ()))


def _cparams(sem):
    return pltpu.CompilerParams(dimension_semantics=sem, vmem_limit_bytes=VMEM_LIMIT)


def _tile(n, cap):
    if n <= cap:
        return n
    t = cap - cap % 64
    while t >= 64:
        if n % t == 0:
            return t
        t -= 64
    raise ValueError((n, cap))


def _dot(a, b, dims):
    return lax.dot_general(a.astype(BF16), b.astype(BF16), dims, preferred_element_type=F32)


def _slot(d):
    return (d % 2) * 4 + d // 2


class Exchange:
    def __init__(self, ins, out_shapes, aliases, n_sems, start, finish):
        self.ins, self.out_shapes, self.aliases, self.n_sems = list(ins), list(out_shapes), dict(aliases), n_sems
        self.start, self.finish = start, finish


def merge_exchanges(xs):
    ins, outs, aliases, bases, n = [], [], {}, [], 0
    for x in xs:
        bases.append((len(ins), len(outs), n))
        aliases.update({len(ins) + i: len(outs) + o for i, o in x.aliases.items()})
        ins += x.ins
        outs += x.out_shapes
        n += x.n_sems

    def run(which):
        def f(ci, co, ss, rs, base):
            for x, (i0, o0, s0) in zip(xs, bases):
                getattr(x, which)(ci[i0:i0 + len(x.ins)], co[o0:o0 + len(x.out_shapes)], ss, rs, base + s0)
        return f

    return Exchange(ins, outs, aliases, n, run('start'), run('finish'))


def _call(name, body, grid, ins, in_specs, out_shape, out_specs, scratch, sems, carry=None):
    if not carry:
        return pl.pallas_call(body, grid=grid, in_specs=in_specs, out_specs=out_specs, out_shape=out_shape,
                              scratch_shapes=scratch, compiler_params=_cparams(sems), name=name)(*ins)
    x = merge_exchanges(carry)
    n_in, n_ci, n_out, n_co, n_sc = len(ins), len(x.ins), len(out_shape), len(x.out_shapes), len(scratch)

    def wrapped(*refs):
        p = [0]

        def take(k):
            p[0] += k
            return refs[p[0] - k:p[0]]

        a, ci, o, co, sc = take(n_in), take(n_ci), take(n_out), take(n_co), take(n_sc)
        ss, rs = take(2)
        first = pl.program_id(0) == 0
        last = pl.program_id(0) == grid[0] - 1
        for d in range(1, len(grid)):
            first = jnp.logical_and(first, pl.program_id(d) == 0)
            last = jnp.logical_and(last, pl.program_id(d) == grid[d] - 1)

        @pl.when(first)
        def _():
            x.start(ci, co, ss, rs, 0)

        body(*a, *o, *sc)

        @pl.when(last)
        def _():
            x.finish(ci, co, ss, rs, 0)

    hbm = pl.BlockSpec(memory_space=pl.ANY)
    res = pl.pallas_call(
        wrapped, grid=grid, in_specs=list(in_specs) + [hbm] * n_ci, out_specs=list(out_specs) + [hbm] * n_co,
        out_shape=list(out_shape) + x.out_shapes,
        input_output_aliases={n_in + i: n_out + o for i, o in x.aliases.items()},
        scratch_shapes=list(scratch) + [pltpu.SemaphoreType.DMA((x.n_sems,)), pltpu.SemaphoreType.DMA((x.n_sems,))],
        compiler_params=_cparams(("arbitrary",) * len(grid)), name=name)(*ins, *x.ins)
    return list(res[:n_out]) + [list(res[n_out:])]


def _mm_core(name, grid, ins, in_specs, out_shape, out_specs, dims, acc_shape, epi, carry=None):
    nk = grid[2]
    n_extra = len(ins) - 2

    def body_single(*refs):
        epi(_dot(refs[0][...], refs[1][...], dims), refs[2:2 + n_extra], refs[2 + n_extra:])

    def body(*refs):
        a_ref, b_ref = refs[0], refs[1]
        ex = refs[2:2 + n_extra]
        outs = refs[2 + n_extra:-1]
        acc = refs[-1]
        k = pl.program_id(2)

        @pl.when(k == 0)
        def _():
            acc[...] = _dot(a_ref[...], b_ref[...], dims)

        @pl.when(jnp.logical_and(k > 0, k < nk - 1))
        def _():
            acc[...] += _dot(a_ref[...], b_ref[...], dims)

        @pl.when(k == nk - 1)
        def _():
            epi(acc[...] + _dot(a_ref[...], b_ref[...], dims), ex, outs)

    if nk == 1:
        return _call(name, body_single, grid, ins, in_specs, out_shape, out_specs, [],
                     ("parallel", "parallel", "arbitrary"), carry)
    return _call(name, body, grid, ins, in_specs, out_shape, out_specs, [pltpu.VMEM(acc_shape, F32)],
                 ("parallel", "parallel", "arbitrary"), carry)


def _split(res, n, carry):
    own = res[0] if n == 1 else list(res[:n])
    return (own, res[n]) if carry else own


def _epi_store(dtype):
    def epi(acc, ex, outs):
        outs[0][...] = acc.astype(dtype)
    return epi


def _epi_bias(acc, ex, outs):
    outs[0][...] = acc + ex[0][...]


def _epi_relu2(acc, ex, outs):
    r = jnp.maximum(acc, 0.0)
    outs[0][...] = (r * r).astype(BF16)
    outs[1][...] = r.astype(BF16)


def _epi_mul2r(acc, ex, outs):
    outs[0][...] = (acc * (2.0 * ex[0][...].astype(F32))).astype(BF16)


def _epi_resid_gate(nctx, tm):
    def epi(acc, ex, outs):
        rows = pl.program_id(0) * tm + lax.broadcasted_iota(jnp.int32, (tm, 1), 0)
        g = jnp.where(rows < nctx, ex[1][0:1, :], ex[1][1:2, :])
        outs[0][...] = ex[0][...] + g * acc
        outs[1][...] = acc
    return epi


def mm_nn(name, a, w, epi, outs, extras=(), extra_kinds=(), tm_cap=1100, tn_cap=512, tk_cap=2048, carry=None):
    M, K = a.shape
    if w.ndim == 3:
        ns = w.shape[2]
        N = NDEV * ns
        tn = _tile(ns, tn_cap)
        nper = ns // tn
    else:
        N = w.shape[1]
        tn = _tile(N, tn_cap)
    tm = _tile(M, tm_cap)
    tk = _tile(K, tk_cap)
    grid = (M // tm, N // tn, K // tk)
    a_spec = pl.BlockSpec((tm, tk), lambda i, j, k: (i, k))
    if w.ndim == 3:
        b_spec = pl.BlockSpec((None, tk, tn), lambda i, j, k: (j // nper, k, j % nper))
    else:
        b_spec = pl.BlockSpec((tk, tn), lambda i, j, k: (k, j))
    ex_specs = []
    for e, kind in zip(extras, extra_kinds):
        if kind == 'mn':
            ex_specs.append(pl.BlockSpec((tm, tn), lambda i, j, k: (i, j)))
        else:
            ex_specs.append(pl.BlockSpec((e.shape[0], tn), lambda i, j, k: (0, j)))
    out_shape = [jax.ShapeDtypeStruct((M, N), dt) for dt in outs]
    out_specs = [pl.BlockSpec((tm, tn), lambda i, j, k: (i, j)) for _ in outs]
    res = _mm_core(name, grid, (a, w, *extras), [a_spec, b_spec, *ex_specs], out_shape, out_specs, NN, (tm, tn), epi,
                   carry)
    return (list(res[:len(outs)]), res[len(outs)]) if carry else res


def mm_nt(name, a, w, epi, out_dtype, extras=(), tm_cap=1100, to_cap=1024, tc_cap=2048, carry=None):
    M, N = a.shape
    tm = _tile(M, tm_cap)
    if w.ndim == 3:
        Kw, ns = w.shape[1], w.shape[2]
        tc = _tile(ns, tc_cap)
        cper = ns // tc
    else:
        Kw = w.shape[0]
        tc = _tile(N, tc_cap)
    to = _tile(Kw, to_cap)
    grid = (M // tm, Kw // to, N // tc)
    a_spec = pl.BlockSpec((tm, tc), lambda i, j, k: (i, k))
    if w.ndim == 3:
        b_spec = pl.BlockSpec((None, to, tc), lambda i, j, k: (k // cper, j, k % cper))
    else:
        b_spec = pl.BlockSpec((to, tc), lambda i, j, k: (j, k))
    ex_specs = [pl.BlockSpec((tm, to), lambda i, j, k: (i, j)) for _ in extras]
    out_shape = [jax.ShapeDtypeStruct((M, Kw), out_dtype)]
    out_specs = [pl.BlockSpec((tm, to), lambda i, j, k: (i, j))]
    return _split(_mm_core(name, grid, (a, w, *extras), [a_spec, b_spec, *ex_specs], out_shape, out_specs, NT, (tm, to),
                           epi, carry), 1, carry)


def mm_tn(name, a, b, shard_axis, to_cap=1024, tn_cap=1024, tc_cap=2200, carry=None):
    M, Ka = a.shape
    N = b.shape[1]
    tc = _tile(M, tc_cap)
    if shard_axis is None:
        to, tn = _tile(Ka, to_cap), _tile(N, tn_cap)
        shape = (Ka, N)
        oblk = (to, tn)
        omap = lambda i, j, k: (i, j)
    elif shard_axis == 1:
        ns = N // NDEV
        to, tn = _tile(Ka, to_cap), _tile(ns, tn_cap)
        per = ns // tn
        shape = (NDEV, Ka, ns)
        oblk = (None, to, tn)
        omap = lambda i, j, k: (_slot(j // per), i, j % per)
    else:
        rs = Ka // NDEV
        to, tn = _tile(rs, to_cap), _tile(N, tn_cap)
        per = rs // to
        shape = (NDEV, rs, N)
        oblk = (None, to, tn)
        omap = lambda i, j, k: (_slot(i // per), i % per, j)
    grid = (Ka // to, N // tn, M // tc)
    a_spec = pl.BlockSpec((tc, to), lambda i, j, k: (k, i))
    b_spec = pl.BlockSpec((tc, tn), lambda i, j, k: (k, j))
    out_shape = [jax.ShapeDtypeStruct(shape, F32)]
    out_specs = [pl.BlockSpec(oblk, omap)]
    return _split(_mm_core(name, grid, (a, b), [a_spec, b_spec], out_shape, out_specs, TN, (to, tn), _epi_store(F32),
                           carry), 1, carry)


def _row_spec(D):
    return pl.BlockSpec((ROW_TILE, D), lambda i: (i, 0))


def _const_spec(r, D):
    return pl.BlockSpec((r, D), lambda i: (0, 0))


def _grp(ref, is_ctx):
    return jnp.where(is_ctx, ref[0:1, :], ref[1:2, :])


def norm_mod(name, x, nw, sh, sc, nctx):
    R, D = x.shape
    assert R % ROW_TILE == 0 and nctx % ROW_TILE == 0

    def body(x_ref, nw_ref, sh_ref, sc_ref, o_ref):
        is_ctx = pl.program_id(0) * ROW_TILE < nctx
        xv = x_ref[...]
        rstd = lax.rsqrt(jnp.mean(xv * xv, axis=-1, keepdims=True) + EPS)
        n = xv * rstd * nw_ref[...]
        o_ref[...] = (n * (1.0 + _grp(sc_ref, is_ctx)) + _grp(sh_ref, is_ctx)).astype(BF16)

    return pl.pallas_call(
        body, grid=(R // ROW_TILE,),
        in_specs=[_row_spec(D), _const_spec(1, D), _const_spec(2, D), _const_spec(2, D)],
        out_specs=_row_spec(D), out_shape=jax.ShapeDtypeStruct((R, D), BF16),
        compiler_params=_cparams(("parallel",)), name=name)(x, nw, sh, sc)


def _gate_rows(dxv, y_ref, g_ref, is_ctx, dy_ref, gpart_ref):
    dy_ref[...] = (dxv * _grp(g_ref, is_ctx)).astype(BF16)
    s = jnp.sum(dxv * y_ref[...], axis=0, keepdims=True)
    zero = jnp.zeros_like(s)
    gpart_ref[0:1, :] += jnp.where(is_ctx, s, zero)
    gpart_ref[1:2, :] += jnp.where(is_ctx, zero, s)


def norm_bwd(name, x, dh, dres, nw, sc, nctx, dres_skip=0, out_skip=0, carry=None, gate=None):
    R, D = x.shape
    assert R % ROW_TILE == 0 and nctx % ROW_TILE == 0 and dres_skip % ROW_TILE == 0 and out_skip % ROW_TILE == 0
    res_tiles, out_tiles = dres_skip // ROW_TILE, out_skip // ROW_TILE

    def body(x_ref, dh_ref, dres_ref, nw_ref, sc_ref, *rest):
        if gate is None:
            dx_ref, part_ref = rest
        else:
            y_ref, g_ref, dx_ref, part_ref, dy_ref, gpart_ref = rest
        i = pl.program_id(0)
        is_ctx = i * ROW_TILE < nctx

        @pl.when(i == 0)
        def _():
            part_ref[...] = jnp.zeros_like(part_ref)
            if gate is not None:
                gpart_ref[...] = jnp.zeros_like(gpart_ref)

        xv = x_ref[...]
        dhv = dh_ref[...]
        w = nw_ref[...]
        rstd = lax.rsqrt(jnp.mean(xv * xv, axis=-1, keepdims=True) + EPS)
        xhat = xv * rstd
        n = xhat * w
        dn = dhv * (1.0 + _grp(sc_ref, is_ctx))
        dxhat = dn * w
        dres = dres_ref[...]
        if res_tiles:
            dres = jnp.where(i < res_tiles, 0.0, dres)
        dxv = dres + rstd * (dxhat - xhat * jnp.mean(dxhat * xhat, axis=-1, keepdims=True))
        dx_ref[...] = dxv
        s_sh = jnp.sum(dhv, axis=0, keepdims=True)
        s_sc = jnp.sum(dhv * n, axis=0, keepdims=True)
        s_nw = jnp.sum(dn * xhat, axis=0, keepdims=True)
        zero = jnp.zeros_like(s_sh)
        part_ref[0:1, :] += jnp.where(is_ctx, s_sh, zero)
        part_ref[1:2, :] += jnp.where(is_ctx, zero, s_sh)
        part_ref[2:3, :] += jnp.where(is_ctx, s_sc, zero)
        part_ref[3:4, :] += jnp.where(is_ctx, zero, s_sc)
        part_ref[4:5, :] += s_nw
        if gate is not None:
            _gate_rows(dxv, y_ref, g_ref, is_ctx, dy_ref, gpart_ref)

    ins = [x, dh, dres, nw, sc]
    in_specs = [_row_spec(D), _row_spec(D), pl.BlockSpec((ROW_TILE, D), lambda i: (jnp.maximum(i - res_tiles, 0), 0)),
                _const_spec(1, D), _const_spec(2, D)]
    out_shape = [jax.ShapeDtypeStruct((R - out_skip, D), F32), jax.ShapeDtypeStruct((8, D), F32)]
    out_specs = [pl.BlockSpec((ROW_TILE, D), lambda i: (jnp.maximum(i - out_tiles, 0), 0)), _const_spec(8, D)]
    if gate is not None:
        assert out_skip == 0
        ins += list(gate)
        in_specs += [_row_spec(D), _const_spec(2, D)]
        out_shape += [jax.ShapeDtypeStruct((R, D), BF16), jax.ShapeDtypeStruct((8, D), F32)]
        out_specs += [_row_spec(D), _const_spec(8, D)]
    res = _call(name, body, (R // ROW_TILE,), ins, in_specs, out_shape, out_specs, [], ("arbitrary",), carry)
    return _split(res, len(out_shape), carry)


def final_loss(name, x, fw, tgt, y, g):
    S, D = x.shape

    def body(x_ref, fw_ref, t_ref, y_ref, g_ref, dx_ref, loss_ref, dfw_ref, dy_ref, gpart_ref):
        i = pl.program_id(0)

        @pl.when(i == 0)
        def _():
            loss_ref[...] = jnp.zeros_like(loss_ref)
            dfw_ref[...] = jnp.zeros_like(dfw_ref)
            gpart_ref[...] = jnp.zeros_like(gpart_ref)

        xv = x_ref[...]
        w = fw_ref[...]
        rstd = lax.rsqrt(jnp.mean(xv * xv, axis=-1, keepdims=True) + EPS)
        xhat = xv * rstd
        e = xhat * w - t_ref[...]
        loss_ref[...] += 0.5 * jnp.sum(jnp.mean(e * e, axis=-1, keepdims=True))
        dout = e * (1.0 / D)
        dfw_ref[0:1, :] += jnp.sum(dout * xhat, axis=0, keepdims=True)
        dxhat = dout * w
        dxv = rstd * (dxhat - xhat * jnp.mean(dxhat * xhat, axis=-1, keepdims=True))
        dx_ref[...] = dxv
        _gate_rows(dxv, y_ref, g_ref, False, dy_ref, gpart_ref)

    return pl.pallas_call(
        body, grid=(S // ROW_TILE,),
        in_specs=[_row_spec(D), _const_spec(1, D), _row_spec(D), _row_spec(D), _const_spec(2, D)],
        out_specs=[_row_spec(D), pl.BlockSpec((8, 128), lambda i: (0, 0)), _const_spec(8, D), _row_spec(D),
                   _const_spec(8, D)],
        out_shape=[jax.ShapeDtypeStruct((S, D), F32), jax.ShapeDtypeStruct((8, 128), F32),
                   jax.ShapeDtypeStruct((8, D), F32), jax.ShapeDtypeStruct((S, D), BF16),
                   jax.ShapeDtypeStruct((8, D), F32)],
        compiler_params=_cparams(("arbitrary",)), name=name)(x, fw, tgt, y, g)


def _rope(x, cos, sa, sb):
    return x * cos + pltpu.roll(x, 96, 1) * sa + pltpu.roll(x, 32, 1) * sb


def _rope_t(dy, cos, sa, sb):
    return dy * cos + pltpu.roll(dy * sa, 32, 1) + pltpu.roll(dy * sb, 96, 1)


_EVEN_KINDS = ['qa'] * 8 + ['ka'] * 2 + ['v'] * 2 + ['qb'] * 8 + ['kb'] * 2 + ['v'] * 2
_EVEN_DSRC = ([('q', j) for j in range(8)] + [('k', 0), ('k', 1), ('v', 0), ('v', 1)]
              + [('q', 8 + j) for j in range(8)] + [('k', 2), ('k', 3), ('v', 2), ('v', 3)])


def _cols(j):
    return slice(j * HEAD, (j + 1) * HEAD)


def prep_even(name, qkv, qn, kn, cos, sa, sb):
    T, W = qkv.shape

    def body(x_ref, qn_ref, kn_ref, cos_ref, sa_ref, sb_ref, o_ref):
        cos_, sa_, sb_ = cos_ref[...], sa_ref[...], sb_ref[...]
        for j, kind in enumerate(_EVEN_KINDS):
            x = x_ref[:, _cols(j)]
            if kind in ('qa', 'ka'):
                rstd = lax.rsqrt(jnp.mean(x * x, axis=-1, keepdims=True) + EPS)
                x = x * rstd * (qn_ref[...] if kind == 'qa' else kn_ref[...])
            if kind != 'v':
                x = _rope(x, cos_, sa_, sb_)
            o_ref[:, _cols(j)] = x.astype(BF16)

    blk = pl.BlockSpec((ROW_TILE, W), lambda i: (i, 0))
    tab = pl.BlockSpec((ROW_TILE, HEAD), lambda i: (i, 0))
    one = pl.BlockSpec((1, HEAD), lambda i: (0, 0))
    return pl.pallas_call(
        body, grid=(T // ROW_TILE,), in_specs=[blk, one, one, tab, tab, tab], out_specs=blk,
        out_shape=jax.ShapeDtypeStruct(qkv.shape, BF16),
        compiler_params=_cparams(("parallel",)), name=name)(qkv, qn, kn, cos, sa, sb)


def prep_even_bwd(name, qkv, dq, dk, dv, qn, kn, cos, sa, sb):
    T, W = qkv.shape

    def body(x_ref, dq_ref, dk_ref, dv_ref, qn_ref, kn_ref, cos_ref, sa_ref, sb_ref, o_ref, part_ref):
        @pl.when(pl.program_id(0) == 0)
        def _():
            part_ref[...] = jnp.zeros_like(part_ref)

        cos_, sa_, sb_ = cos_ref[...], sa_ref[...], sb_ref[...]
        src = {'q': dq_ref, 'k': dk_ref, 'v': dv_ref}
        sums = {'qa': None, 'ka': None}
        for j, kind in enumerate(_EVEN_KINDS):
            which, blk_j = _EVEN_DSRC[j]
            d = src[which][:, _cols(blk_j)]
            if kind != 'v':
                d = _rope_t(d, cos_, sa_, sb_)
            if kind in ('qa', 'ka'):
                x = x_ref[:, _cols(j)]
                rstd = lax.rsqrt(jnp.mean(x * x, axis=-1, keepdims=True) + EPS)
                xhat = x * rstd
                s = jnp.sum(d * xhat, axis=0, keepdims=True)
                sums[kind] = s if sums[kind] is None else sums[kind] + s
                dxhat = d * (qn_ref[...] if kind == 'qa' else kn_ref[...])
                d = rstd * (dxhat - xhat * jnp.mean(dxhat * xhat, axis=-1, keepdims=True))
            o_ref[:, _cols(j)] = d.astype(BF16)
        part_ref[0:1, :] += sums['qa']
        part_ref[1:2, :] += sums['ka']

    def rows(w):
        return pl.BlockSpec((ROW_TILE, w), lambda i: (i, 0))

    one = pl.BlockSpec((1, HEAD), lambda i: (0, 0))
    return pl.pallas_call(
        body, grid=(T // ROW_TILE,),
        in_specs=[rows(W), rows(dq.shape[1]), rows(dk.shape[1]), rows(dv.shape[1]), one, one,
                  rows(HEAD), rows(HEAD), rows(HEAD)],
        out_specs=[rows(W), pl.BlockSpec((8, HEAD), lambda i: (0, 0))],
        out_shape=[jax.ShapeDtypeStruct(qkv.shape, BF16), jax.ShapeDtypeStruct((8, HEAD), F32)],
        compiler_params=_cparams(("arbitrary",)), name=name)(qkv, dq, dk, dv, qn, kn, cos, sa, sb)


def _even_maps():
    qmap = lambda h, qb: (qb, jnp.where(h < 8, h, h + 4))
    kmap = lambda h, qb: (0, jnp.where(h < 8, 8 + h // 4, 18 + h // 4))
    vmap = lambda h, qb: (0, jnp.where(h < 8, 10 + h // 4, 20 + h // 4))
    return qmap, kmap, vmap


def _softmax_parts(parts, extra=None, stats=None, normalize=True):
    if stats is None:
        m = parts[0].max(axis=-1, keepdims=True)
        for p in parts[1:]:
            m = jnp.maximum(m, p.max(axis=-1, keepdims=True))
        if extra is not None:
            m = jnp.maximum(m, extra)
    else:
        m = stats[0]
    es = [jnp.exp(p - m) for p in parts]
    ex = None if extra is None else jnp.exp(extra - m)
    if stats is None:
        l = es[0].sum(axis=-1, keepdims=True)
        for e in es[1:]:
            l = l + e.sum(axis=-1, keepdims=True)
        if extra is not None:
            l = l + ex
        inv = 1.0 / l
    else:
        inv = stats[1]
    if not normalize:
        return es, ex, (m, inv)
    return [e * inv for e in es], (None if ex is None else ex * inv), (m, inv)


def _win_scores(q, k_ref, qb, tq, nctx, S):
    L = tq + 2 * WINDOW
    nqc = nctx // tq
    qlat = (qb - nqc) * tq
    start = pl.multiple_of(jnp.clip(qlat - WINDOW, 0, S - L), 128)
    kc = k_ref[0:nctx, :]
    kw = k_ref[pl.ds(nctx + start, L), :]
    s_c = _dot(q, kc, NT) * SCALE
    s_w = _dot(q, kw, NT) * SCALE
    qpos = qlat + lax.broadcasted_iota(jnp.int32, (tq, 1), 0)
    kpos = start + lax.broadcasted_iota(jnp.int32, (1, L), 1)
    valid = jnp.logical_and(jnp.abs(kpos - qpos) <= WINDOW, qb >= nqc)
    return s_c, jnp.where(valid, s_w, NEG), start, L


def _softmax_raw(raw, stats=None, normalize=True):
    m = raw.max(axis=-1, keepdims=True) if stats is None else stats[0]
    e = jnp.exp2((raw - m) * (SCALE * np.log2(np.e)))
    inv = 1.0 / e.sum(axis=-1, keepdims=True) if stats is None else stats[1]
    return (e * inv if normalize else e), (m, inv)


def _glob_keys(qb, tq, nctx, T):
    is_ctx = qb < nctx // tq
    return [(is_ctx, slice(0, nctx)), (jnp.logical_not(is_ctx), slice(0, T))]


def _stats_spec(tq):
    return pl.BlockSpec((None, None, tq, 2), lambda h, qb: (h, qb, 0, 0))


def attn_even_fwd(name, qkvh, sink, nctx, tq, carry=None):
    T = qkvh.shape[0]
    S = T - nctx
    qmap, kmap, vmap = _even_maps()

    def body(sink_ref, q_ref, k_ref, v_ref, o_ref, st_ref):
        h, qb = pl.program_id(0), pl.program_id(1)
        q = q_ref[...]

        for pred, keys in _glob_keys(qb, tq, nctx, T):
            @pl.when(jnp.logical_and(h < 8, pred))
            def _():
                e, (m, inv) = _softmax_raw(_dot(q, k_ref[keys, :], NT), normalize=False)
                st_ref[:, 0:1] = m
                st_ref[:, 1:2] = inv
                o_ref[...] = (_dot(e, v_ref[keys, :], NN) * inv).astype(BF16)

        @pl.when(h >= 8)
        def _():
            s_c, s_w, start, L = _win_scores(q, k_ref, qb, tq, nctx, S)
            sk = jnp.full((tq, 1), sink_ref[jnp.maximum(h - 8, 0)], F32)
            (e_c, e_w), _, (m, inv) = _softmax_parts([s_c, s_w], sk, normalize=False)
            st_ref[:, 0:1] = m
            st_ref[:, 1:2] = inv
            o = _dot(e_c, v_ref[0:nctx, :], NN) + _dot(e_w, v_ref[pl.ds(nctx + start, L), :], NN)
            o_ref[...] = (o * inv).astype(BF16)

    res = _call(name, body, (16, T // tq), (sink, qkvh, qkvh, qkvh),
                [pl.BlockSpec(memory_space=pltpu.SMEM), pl.BlockSpec((tq, HEAD), qmap),
                 pl.BlockSpec((T, HEAD), kmap), pl.BlockSpec((T, HEAD), vmap)],
                [jax.ShapeDtypeStruct((T, 16 * HEAD), BF16), jax.ShapeDtypeStruct((16, T // tq, tq, 2), F32)],
                [pl.BlockSpec((tq, HEAD), lambda h, qb: (qb, h)), _stats_spec(tq)],
                [], ("parallel", "arbitrary"), carry)
    return _split(res, 2, carry)


def _row_dot(do, o):
    return jnp.sum(do.astype(F32) * o.astype(F32), axis=-1, keepdims=True)


def attn_even_bwd(name, qkvh, sink, do, o, stats, nctx, tq, carry=None):
    T = qkvh.shape[0]
    assert stats.shape == (16, T // tq, tq, 2)
    S = T - nctx
    qmap, kmap, vmap = _even_maps()

    def body(sink_ref, q_ref, k_ref, v_ref, do_ref, o_ref, st_ref, dq_ref, dk_ref, dv_ref, ds_ref):
        h, qb = pl.program_id(0), pl.program_id(1)
        q = q_ref[...]
        dov = do_ref[...]
        row = _row_dot(dov, o_ref[...])

        @pl.when(jnp.logical_and(h % 4 == 0, qb == 0))
        def _():
            dk_ref[...] = jnp.zeros_like(dk_ref)
            dv_ref[...] = jnp.zeros_like(dv_ref)

        @pl.when(qb == 0)
        def _():
            ds_ref[...] = jnp.zeros_like(ds_ref)

        for pred, keys in _glob_keys(qb, tq, nctx, T):
            @pl.when(jnp.logical_and(h < 8, pred))
            def _():
                p, _ = _softmax_raw(_dot(q, k_ref[keys, :], NT), (st_ref[:, 0:1], st_ref[:, 1:2]))
                dp = _dot(dov, v_ref[keys, :], NT)
                dsb = (p * (dp - row) * SCALE).astype(BF16)
                dq_ref[...] = _dot(dsb, k_ref[keys, :], NN)
                dk_ref[keys, :] += _dot(dsb, q, TN)
                dv_ref[keys, :] += _dot(p, dov, TN)

        @pl.when(h >= 8)
        def _():
            s_c, s_w, start, L = _win_scores(q, k_ref, qb, tq, nctx, S)
            sk = jnp.full((tq, 1), sink_ref[jnp.maximum(h - 8, 0)], F32)
            (p_c, p_w), p_s, _ = _softmax_parts([s_c, s_w], sk, (st_ref[:, 0:1], st_ref[:, 1:2]))
            win = pl.ds(nctx + start, L)
            dp_c = _dot(dov, v_ref[0:nctx, :], NT)
            dp_w = _dot(dov, v_ref[win, :], NT)
            ds_c = (p_c * (dp_c - row) * SCALE).astype(BF16)
            ds_w = (p_w * (dp_w - row) * SCALE).astype(BF16)
            dq_ref[...] = _dot(ds_c, k_ref[0:nctx, :], NN) + _dot(ds_w, k_ref[win, :], NN)
            dk_ref[0:nctx, :] += _dot(ds_c, q, TN)
            dk_ref[win, :] += _dot(ds_w, q, TN)
            dv_ref[0:nctx, :] += _dot(p_c, dov, TN)
            dv_ref[win, :] += _dot(p_w, dov, TN)
            ds_ref[...] += jnp.sum(-(p_s * row))

    kv_out = pl.BlockSpec((T, HEAD), lambda h, qb: (0, h // 4))
    head_blk = pl.BlockSpec((tq, HEAD), lambda h, qb: (qb, h))
    res = _call(name, body, (16, T // tq), (sink, qkvh, qkvh, qkvh, do, o, stats),
                [pl.BlockSpec(memory_space=pltpu.SMEM), pl.BlockSpec((tq, HEAD), qmap),
                 pl.BlockSpec((T, HEAD), kmap), pl.BlockSpec((T, HEAD), vmap), head_blk, head_blk, _stats_spec(tq)],
                [jax.ShapeDtypeStruct((T, 16 * HEAD), F32), jax.ShapeDtypeStruct((T, 4 * HEAD), F32),
                 jax.ShapeDtypeStruct((T, 4 * HEAD), F32), jax.ShapeDtypeStruct((16, 8, 128), F32)],
                [pl.BlockSpec((tq, HEAD), lambda h, qb: (qb, h)), kv_out, kv_out,
                 pl.BlockSpec((None, 8, 128), lambda h, qb: (h, 0, 0))],
                [], ("arbitrary", "arbitrary"), carry)
    return _split(res, 4, carry)


NA_GROUP = 4
NA_SPAN = NA_KH + NA_GROUP - 1
_NA_PLAN = [[(j, 0) for j in range(NA_GROUP)],
            [(NA_KH // 2, j) for j in range(NA_GROUP)],
            [(NA_KH // 2 + j, NA_GROUP - 1) for j in range(NA_GROUP)]]


def _na_group(g, n_groups, rows):
    last = g == n_groups - 1
    kind = jnp.where(g == 0, 0, jnp.where(last, 2, 1))
    first_row = jnp.where(g == 0, 0, jnp.where(last, rows - NA_SPAN, NA_GROUP * g - NA_KH // 2))
    return kind, first_row


def na_span_bias(bias8):
    LW, LS = NA_KH * GRID_W, NA_SPAN * GRID_W
    kinds = []
    for plan in _NA_PLAN:
        strips = [jnp.pad(bias8[:, off], ((0, 0), (0, 0), (s * GRID_W, LS - LW - s * GRID_W)), constant_values=NEG)
                  for off, s in plan]
        kinds.append(jnp.concatenate(strips, axis=1))
    return jnp.stack(kinds, axis=1)


def na_span_bias_grad(db):
    LW = NA_KH * GRID_W
    out = [None] * NA_KH
    for kind, plan in enumerate(_NA_PLAN):
        for j, (off, s) in enumerate(plan):
            piece = db[:, kind, j * GRID_W:(j + 1) * GRID_W, s * GRID_W:s * GRID_W + LW]
            out[off] = piece if out[off] is None else out[off] + piece
    return jnp.stack(out, axis=1)


def _na_specs(T, nctx, n_groups, rows):
    LS = NA_SPAN * GRID_W
    tq = NA_GROUP * GRID_W
    assert nctx % tq == 0 and n_groups >= 3
    q_spec = pl.BlockSpec((tq, HEAD), lambda h, g: (g + nctx // tq, h))
    k_spec = pl.BlockSpec((T, HEAD), lambda h, g: (0, 16 + h))
    v_spec = pl.BlockSpec((T, HEAD), lambda h, g: (0, 32 + h))
    b_spec = pl.BlockSpec((None, None, tq, LS), lambda h, g: (h, _na_group(g, n_groups, rows)[0], 0, 0))
    row_spec = pl.BlockSpec((tq, HEAD), lambda h, g: (g, h))
    return q_spec, k_spec, v_spec, b_spec, row_spec


def _na_scores(q, k_ref, b_ref, g, n_groups, rows, nctx):
    first_row = _na_group(g, n_groups, rows)[1]
    win = pl.ds(pl.multiple_of(nctx + first_row * GRID_W, GRID_W), NA_SPAN * GRID_W)
    s_c = _dot(q, k_ref[0:nctx, :], NT) * SCALE
    s_w = _dot(q, k_ref[win, :], NT) * SCALE + b_ref[...]
    return s_c, s_w, win


def attn_odd_fwd(name, qkv, bias_s, nctx, carry=None):
    T = qkv.shape[0]
    S = T - nctx
    rows = S // GRID_W
    n_groups = rows // NA_GROUP
    q_spec, k_spec, v_spec, b_spec, row_spec = _na_specs(T, nctx, n_groups, rows)

    def body(q_ref, k_ref, v_ref, b_ref, o_ref, st_ref):
        s_c, s_w, win = _na_scores(q_ref[...], k_ref, b_ref, pl.program_id(1), n_groups, rows, nctx)
        (e_c, e_w), _, (m, inv) = _softmax_parts([s_c, s_w], normalize=False)
        st_ref[:, 0:1] = m
        st_ref[:, 1:2] = inv
        o_ref[...] = ((_dot(e_c, v_ref[0:nctx, :], NN) + _dot(e_w, v_ref[win, :], NN)) * inv).astype(BF16)

    tq = NA_GROUP * GRID_W
    res = _call(name, body, (16, n_groups), (qkv, qkv, qkv, bias_s), [q_spec, k_spec, v_spec, b_spec],
                [jax.ShapeDtypeStruct((S, 16 * HEAD), BF16), jax.ShapeDtypeStruct((16, n_groups, tq, 2), F32)],
                [row_spec, _stats_spec(tq)], [], ("parallel", "arbitrary"), carry)
    return _split(res, 2, carry)


def attn_odd_bwd(name, qkv, bias_s, do, o, stats, nctx, carry=None):
    T = qkv.shape[0]
    S = T - nctx
    rows = S // GRID_W
    n_groups = rows // NA_GROUP
    q_spec, k_spec, v_spec, b_spec, row_spec = _na_specs(T, nctx, n_groups, rows)

    def body(q_ref, k_ref, v_ref, b_ref, do_ref, o_ref, st_ref, dq_ref, dk_ref, dv_ref, db_ref):
        g = pl.program_id(1)
        q = q_ref[...]
        dov = do_ref[...]
        row = _row_dot(dov, o_ref[...])

        @pl.when(g == 0)
        def _():
            dk_ref[...] = jnp.zeros_like(dk_ref)
            dv_ref[...] = jnp.zeros_like(dv_ref)

        s_c, s_w, win = _na_scores(q, k_ref, b_ref, g, n_groups, rows, nctx)
        (p_c, p_w), _, _ = _softmax_parts([s_c, s_w], None, (st_ref[:, 0:1], st_ref[:, 1:2]))
        dp_c = _dot(dov, v_ref[0:nctx, :], NT)
        dp_w = _dot(dov, v_ref[win, :], NT)
        dsw = p_w * (dp_w - row)
        first_visit = jnp.logical_or(g <= 1, g == n_groups - 1)

        @pl.when(first_visit)
        def _():
            db_ref[...] = dsw

        @pl.when(jnp.logical_not(first_visit))
        def _():
            db_ref[...] += dsw

        ds_c = (p_c * (dp_c - row) * SCALE).astype(BF16)
        ds_w = (dsw * SCALE).astype(BF16)
        dq_ref[...] = _dot(ds_c, k_ref[0:nctx, :], NN) + _dot(ds_w, k_ref[win, :], NN)
        dk_ref[0:nctx, :] += _dot(ds_c, q, TN)
        dk_ref[win, :] += _dot(ds_w, q, TN)
        dv_ref[0:nctx, :] += _dot(p_c, dov, TN)
        dv_ref[win, :] += _dot(p_w, dov, TN)

    kv_out = pl.BlockSpec((T, HEAD), lambda h, g: (0, h))
    res = _call(name, body, (16, n_groups), (qkv, qkv, qkv, bias_s, do, o, stats),
                [q_spec, k_spec, v_spec, b_spec, row_spec, row_spec, _stats_spec(NA_GROUP * GRID_W)],
                [jax.ShapeDtypeStruct((S, 16 * HEAD), F32), jax.ShapeDtypeStruct((T, 16 * HEAD), F32),
                 jax.ShapeDtypeStruct((T, 16 * HEAD), F32), jax.ShapeDtypeStruct(bias_s.shape, F32)],
                [row_spec, kv_out, kv_out, b_spec], [], ("arbitrary", "arbitrary"), carry)
    return _split(res, 4, carry)


def _na_onehots():
    o = np.arange(NA_KH)[:, None]
    i = np.arange(NA_KH)[None, :]
    a = i - o + NA_KH - 1
    A = (a[..., None] == np.arange(2 * NA_KH - 1)).astype(np.float32)
    qc = np.arange(GRID_W)[:, None]
    kc = np.arange(GRID_W)[None, :]
    b = np.clip(kc - qc + NA_KW - 1, 0, 2 * NA_KW - 2)
    cs = np.clip(qc - NA_KW // 2, 0, GRID_W - NA_KW)
    valid = (kc >= cs) & (kc < cs + NA_KW)
    B = ((b[..., None] == np.arange(2 * NA_KW - 1)) & valid[..., None]).astype(np.float32)
    return A, B, valid


def na_bias_table(rpb):
    A, B, valid = _na_onehots()
    hp = lax.Precision.HIGHEST
    t = jnp.einsum('hab,oia->hoib', rpb, jnp.asarray(A), precision=hp)
    bias = jnp.einsum('hoib,qkb->hoqik', t, jnp.asarray(B), precision=hp)
    bias = jnp.where(jnp.asarray(valid)[None, None, :, None, :], bias, NEG)
    return bias.reshape(rpb.shape[0], NA_KH, GRID_W, NA_KH * GRID_W)


def na_bias_grad(name, dbias8):
    A, B, _ = _na_onehots()
    H = dbias8.shape[0]
    nb, na = 2 * NA_KW - 1, 2 * NA_KH - 1
    d = dbias8.reshape(H, NA_KH, GRID_W, NA_KH, GRID_W).transpose(0, 1, 3, 2, 4)
    d = d.reshape(H * NA_KH * NA_KH, GRID_W * GRID_W)
    Bp = np.zeros((GRID_W * GRID_W, 128), np.float32)
    Bp[:, :nb] = B.reshape(GRID_W * GRID_W, nb)
    Ap = np.zeros((16, NA_KH * NA_KH), np.float32)
    Ap[:na] = A.reshape(NA_KH * NA_KH, na).T
    rows_per_head = NA_KH * NA_KH

    def split3(x):
        hi = x.astype(BF16)
        r1 = x - hi.astype(F32)
        mid = r1.astype(BF16)
        return hi, mid, (r1 - mid.astype(F32)).astype(BF16)

    def body(d_ref, b_ref, a_ref, o_ref):
        bm, am = b_ref[...], a_ref[...]
        g = sum(lax.dot_general(p, bm, NN, preferred_element_type=F32) for p in split3(d_ref[...]))
        o_ref[...] = sum(lax.dot_general(am, p, NN, preferred_element_type=F32) for p in split3(g))

    out = pl.pallas_call(
        body, grid=(H,),
        in_specs=[pl.BlockSpec((rows_per_head, GRID_W * GRID_W), lambda h: (h, 0)),
                  pl.BlockSpec((GRID_W * GRID_W, 128), lambda h: (0, 0)),
                  pl.BlockSpec((16, rows_per_head), lambda h: (0, 0))],
        out_specs=pl.BlockSpec((None, 16, 128), lambda h: (h, 0, 0)),
        out_shape=jax.ShapeDtypeStruct((H, 16, 128), F32),
        compiler_params=_cparams(("parallel",)), name=name)(d, jnp.asarray(Bp, BF16), jnp.asarray(Ap, BF16))
    return out[:, :na, :nb]


def _vmem_call(name, fn, out_shape, *arrays):
    def body(*refs):
        n = len(arrays)
        res = fn(*[r[...] for r in refs[:n]])
        if not isinstance(res, (tuple, list)):
            res = (res,)
        for o, v in zip(refs[n:], res):
            o[...] = v
    return pl.pallas_call(body, out_shape=out_shape, name=name,
                          compiler_params=pltpu.CompilerParams(vmem_limit_bytes=VMEM_LIMIT))(*arrays)


def _silu(v):
    return v / (1.0 + jnp.exp(-v))


def _adamw_math(w, g, m, v):
    m2 = ADAM_B1 * m + (1.0 - ADAM_B1) * g
    v2 = ADAM_B2 * v + (1.0 - ADAM_B2) * (g * g)
    m_hat = m2 / (1.0 - ADAM_B1 ** ADAM_STEP)
    v_hat = v2 / (1.0 - ADAM_B2 ** ADAM_STEP)
    delta = -ADAM_LR * (m_hat / (jnp.sqrt(v_hat) + ADAM_EPS) + ADAM_WD * w)
    return delta, m2, v2


def _ew_tile(R, C):
    return _tile(R, max(64, (262144 // C) // 64 * 64))


def adamw_rows(name, w, g, m, v, extra_g=None):
    R, C = w.shape
    tr = _ew_tile(R, C)
    extra_g = list(extra_g or [])
    ne = len(extra_g)

    def body(*refs):
        w_ref, g_ref, m_ref, v_ref = refs[:4]
        gs = g_ref[...]
        for e in refs[4:4 + ne]:
            gs = gs + e[...].astype(F32)
        go, do, mo, vo = refs[4 + ne:]
        d, m2, v2 = _adamw_math(w_ref[...], gs, m_ref[...], v_ref[...])
        go[...] = gs
        do[...] = d
        mo[...] = m2
        vo[...] = v2

    spec = pl.BlockSpec((tr, C), lambda i: (i, 0))
    return pl.pallas_call(
        body, grid=(R // tr,), in_specs=[spec] * (4 + ne), out_specs=[spec] * 4,
        out_shape=[jax.ShapeDtypeStruct((R, C), F32)] * 4,
        compiler_params=_cparams(("parallel",)), name=name)(w, g, m, v, *extra_g)


def rs_chip_sum(name, g8, sib4, where):
    _, R, C = g8.shape
    tr = _ew_tile(R, C)

    def body(s_ref, g_ref, b_ref, o_ref):
        o_ref[...] = (g_ref[...] + b_ref[...]).astype(BF16)

    def chip(q, s):
        return (s[1] + 1 + q) % 4

    blk = (None, tr, C)
    grid_spec = pltpu.PrefetchScalarGridSpec(
        num_scalar_prefetch=1, grid=(3, R // tr),
        in_specs=[pl.BlockSpec(blk, lambda q, i, s: (s[0] + chip(q, s), i, 0)),
                  pl.BlockSpec(blk, lambda q, i, s: (chip(q, s), i, 0))],
        out_specs=pl.BlockSpec(blk, lambda q, i, s: (chip(q, s), i, 0)))
    return pl.pallas_call(body, grid_spec=grid_spec, out_shape=jax.ShapeDtypeStruct((4, R, C), BF16),
                          compiler_params=_cparams(("parallel", "parallel")), name=name)(where, g8, sib4)


def adamw_rs(name, w, g8, sib4, rem3, m, v, idx, layer, prev=None):
    L, R, C = w.shape
    tr = _ew_tile(R, C)

    def body(s_ref, w_ref, g_ref, sb_ref, r0_ref, r1_ref, r2_ref, m_ref, v_ref, *rest):
        go, do, mo, vo = rest[-4:]
        gs = g_ref[...] + sb_ref[...]
        for r_ref in (r0_ref, r1_ref, r2_ref):
            gs = gs + r_ref[...].astype(F32)
        d, m2, v2 = _adamw_math(w_ref[...], gs, m_ref[...], v_ref[...])
        go[...] = gs
        do[...] = d
        mo[...] = m2
        vo[...] = v2

    blk = (None, tr, C)
    mine = pl.BlockSpec(blk, lambda i, s: (layer, i, 0))

    def rem(k):
        return pl.BlockSpec(blk, lambda i, s: (k, i, 0))

    prev = list(prev or [])
    grid_spec = pltpu.PrefetchScalarGridSpec(
        num_scalar_prefetch=1, grid=(R // tr,),
        in_specs=[mine, pl.BlockSpec(blk, lambda i, s: (s[0], i, 0)), pl.BlockSpec(blk, lambda i, s: (s[1], i, 0)),
                  rem(0), rem(1), rem(2), mine, mine] + [pl.BlockSpec(memory_space=pl.ANY)] * len(prev),
        out_specs=[mine] * 4)
    return pl.pallas_call(body, grid_spec=grid_spec, out_shape=[jax.ShapeDtypeStruct((L, R, C), F32)] * 4,
                          input_output_aliases={9 + k: k for k in range(len(prev))},
                          compiler_params=_cparams(("parallel",)), name=name)(
                              idx, w, g8, sib4, rem3, rem3, rem3, m, v, *prev)


def _me():
    x, y, c = lax.axis_index("x"), lax.axis_index("y"), lax.axis_index("c")
    return x, y, c


def _flip(v, bit):
    return 1 - v if bit else v


def ag_small(name, x, with_sum=False):
    R, C = x.shape

    def body(x_ref, out_ref, *rest):
        if with_sum:
            sum_ref, send_sems, recv_sems, lsem = rest
        else:
            send_sems, recv_sems, lsem = rest
        mx, my, mc = _me()
        me = 4 * mx + 2 * my + mc
        local = pltpu.make_async_copy(x_ref, out_ref.at[me], lsem)
        local.start()
        sends = []
        for k in range(1, NDEV):
            peer = (_flip(mx, k & 4), _flip(my, k & 2), _flip(mc, k & 1))
            cp = pltpu.make_async_remote_copy(src_ref=x_ref, dst_ref=out_ref.at[me], send_sem=send_sems.at[k - 1],
                                              recv_sem=recv_sems.at[k - 1], device_id=peer, device_id_type=MESH)
            cp.start()
            sends.append(cp)
        for k in range(1, NDEV):
            px, py, pc = _flip(mx, k & 4), _flip(my, k & 2), _flip(mc, k & 1)
            pltpu.make_async_remote_copy(src_ref=x_ref, dst_ref=out_ref.at[4 * px + 2 * py + pc],
                                         send_sem=send_sems.at[k - 1], recv_sem=recv_sems.at[k - 1],
                                         device_id=(px, py, pc), device_id_type=MESH).wait_recv()
        for cp in sends:
            cp.wait_send()
        local.wait()
        if with_sum:
            acc = out_ref[0]
            for d in range(1, NDEV):
                acc = acc + out_ref[d]
            sum_ref[...] = acc

    out_shape = [jax.ShapeDtypeStruct((NDEV, R, C), F32)]
    if with_sum:
        out_shape.append(jax.ShapeDtypeStruct((R, C), F32))
    vm = pl.BlockSpec(memory_space=pltpu.VMEM)
    res = pl.pallas_call(
        body, out_shape=out_shape, in_specs=[vm], out_specs=[vm] * len(out_shape),
        scratch_shapes=[pltpu.SemaphoreType.DMA((NDEV - 1,)), pltpu.SemaphoreType.DMA((NDEV - 1,)),
                        pltpu.SemaphoreType.DMA],
        compiler_params=pltpu.CompilerParams(vmem_limit_bytes=VMEM_LIMIT), name=name)(x)
    return res if with_sum else res[0]


def ag_big(name, shards):
    n = len(shards)

    def body(*refs):
        ins, outs = refs[:n], refs[n:2 * n]
        send_sems, recv_sems, lsems = refs[2 * n:]
        mx, my, mc = _me()
        me = (mx, my, mc)
        sibling = (mx, my, 1 - mc)
        chips = [(1 - mx, my), (mx, 1 - my), (1 - mx, 1 - my)]

        def idx(p):
            return 4 * p[0] + 2 * p[1] + p[2]

        def copy(t, k, block, to, src=None):
            dst = outs[t].at[idx(block)]
            return pltpu.make_async_remote_copy(
                src_ref=dst if src is None else src, dst_ref=dst, send_sem=send_sems.at[7 * t + k],
                recv_sem=recv_sems.at[7 * t + k], device_id=to, device_id_type=MESH)

        started = []
        locals_ = []
        for t in range(n):
            mine = pltpu.make_async_copy(ins[t], outs[t].at[idx(me)], lsems.at[t])
            mine.start()
            locals_.append(mine)
            first = [copy(t, 0, me, sibling, src=ins[t])]
            first += [copy(t, 1 + j, me, (*chip, mc), src=ins[t]) for j, chip in enumerate(chips)]
            for cp in first:
                cp.start()
            started += first
        for t in range(n):
            for j, chip in enumerate(chips):
                copy(t, 1 + j, (*chip, mc), me).wait_recv()
                fwd = copy(t, 4 + j, (*chip, mc), sibling)
                fwd.start()
                started.append(fwd)
        for t in range(n):
            copy(t, 0, sibling, me).wait_recv()
            for j, chip in enumerate(chips):
                copy(t, 4 + j, (*chip, 1 - mc), me).wait_recv()
        for cp in started:
            cp.wait_send()
        for mine in locals_:
            mine.wait()

    anyspec = pl.BlockSpec(memory_space=pl.ANY)
    return pl.pallas_call(
        body, out_shape=[jax.ShapeDtypeStruct((NDEV,) + s.shape, s.dtype) for s in shards],
        in_specs=[anyspec] * n, out_specs=[anyspec] * n,
        scratch_shapes=[pltpu.SemaphoreType.DMA((7 * n,)), pltpu.SemaphoreType.DMA((7 * n,)),
                        pltpu.SemaphoreType.DMA((n,))],
        name=name)(*shards)


def _idx(p):
    return 4 * p[0] + 2 * p[1] + p[2]


def _remote(src, dst, ss, rs, k, to):
    return pltpu.make_async_remote_copy(src_ref=src, dst_ref=dst, send_sem=ss.at[k], recv_sem=rs.at[k],
                                        device_id=to, device_id_type=MESH)


def ex_ag_chips(shards):
    n = len(shards)

    def copies(ci, co, ss, rs, base):
        mx, my, mc = _me()
        me = (mx, my, mc)
        peers = [(mx, my, 1 - mc), (1 - mx, my, mc), (mx, 1 - my, mc), (1 - mx, 1 - my, mc)]
        sends, recvs, local = [], [], []
        for t in range(n):
            b = base + 5 * t
            for k, peer in enumerate(peers):
                sends.append(_remote(ci[t], co[t].at[_idx(me)], ss, rs, b + k, peer))
                recvs.append(_remote(ci[t], co[t].at[_idx(peer)], ss, rs, b + k, peer))
            local.append(pltpu.make_async_copy(ci[t], co[t].at[_idx(me)], ss.at[b + 4]))
        return sends, recvs, local

    def start(ci, co, ss, rs, base):
        sends, _, local = copies(ci, co, ss, rs, base)
        for cp in local + sends:
            cp.start()

    def finish(ci, co, ss, rs, base):
        sends, recvs, local = copies(ci, co, ss, rs, base)
        for cp in recvs:
            cp.wait_recv()
        for cp in sends:
            cp.wait_send()
        for cp in local:
            cp.wait()

    outs = [jax.ShapeDtypeStruct((NDEV,) + s.shape, s.dtype) for s in shards]
    return Exchange(shards, outs, {}, 5 * n, start, finish)


def ex_ag_sibling(bufs):
    n = len(bufs)

    def copies(co, ss, rs, base):
        mx, my, mc = _me()
        sibling = (mx, my, 1 - mc)
        chips = [(1 - mx, my), (mx, 1 - my), (1 - mx, 1 - my)]
        sends, recvs = [], []
        for t in range(n):
            for j, chip in enumerate(chips):
                mine, theirs = co[t].at[_idx((*chip, mc))], co[t].at[_idx((*chip, 1 - mc))]
                sends.append(_remote(mine, mine, ss, rs, base + 3 * t + j, sibling))
                recvs.append(_remote(mine, theirs, ss, rs, base + 3 * t + j, sibling))
        return sends, recvs

    def start(ci, co, ss, rs, base):
        for cp in copies(co, ss, rs, base)[0]:
            cp.start()

    def finish(ci, co, ss, rs, base):
        sends, recvs = copies(co, ss, rs, base)
        for cp in recvs:
            cp.wait_recv()
        for cp in sends:
            cp.wait_send()

    outs = [jax.ShapeDtypeStruct(b.shape, b.dtype) for b in bufs]
    return Exchange(bufs, outs, {t: t for t in range(n)}, 3 * n, start, finish)


def ex_rs_sibling(grads):
    n = len(grads)

    def copies(ci, co, ss, rs, base):
        mx, my, mc = _me()
        return [_remote(ci[t].at[pl.ds((1 - mc) * 4, 4)], co[t], ss, rs, base + t, (mx, my, 1 - mc)) for t in range(n)]

    def start(ci, co, ss, rs, base):
        for cp in copies(ci, co, ss, rs, base):
            cp.start()

    def finish(ci, co, ss, rs, base):
        for cp in copies(ci, co, ss, rs, base):
            cp.wait()

    outs = [jax.ShapeDtypeStruct((4,) + g.shape[1:], g.dtype) for g in grads]
    return Exchange(grads, outs, {}, n, start, finish)


def ex_rs_chips(parts):
    n = len(parts)

    def copies(ci, co, ss, rs, base):
        mx, my, mc = _me()
        cps = []
        for t in range(n):
            for k in range(1, 4):
                px, py = _flip(mx, k & 2), _flip(my, k & 1)
                cps.append(_remote(ci[t].at[2 * px + py], co[t].at[k - 1], ss, rs, base + 3 * t + k - 1, (px, py, mc)))
        return cps

    def start(ci, co, ss, rs, base):
        for cp in copies(ci, co, ss, rs, base):
            cp.start()

    def finish(ci, co, ss, rs, base):
        for cp in copies(ci, co, ss, rs, base):
            cp.wait()

    outs = [jax.ShapeDtypeStruct((3,) + p.shape[1:], p.dtype) for p in parts]
    return Exchange(parts, outs, {}, 3 * n, start, finish)


def run_exchanges(name, xs):
    x = merge_exchanges(xs)
    n_ci, n_co = len(x.ins), len(x.out_shapes)

    def body(*refs):
        ci, co = refs[:n_ci], refs[n_ci:n_ci + n_co]
        ss, rs = refs[n_ci + n_co:]
        x.start(ci, co, ss, rs, 0)
        x.finish(ci, co, ss, rs, 0)

    hbm = pl.BlockSpec(memory_space=pl.ANY)
    return pl.pallas_call(
        body, out_shape=x.out_shapes, in_specs=[hbm] * n_ci, out_specs=[hbm] * n_co, input_output_aliases=x.aliases,
        scratch_shapes=[pltpu.SemaphoreType.DMA((x.n_sems,)), pltpu.SemaphoreType.DMA((x.n_sems,))], name=name)(*x.ins)


def _rope_tables(S, nctx):
    t = jnp.arange(S)
    row = (t // GRID_W).astype(F32)
    col = (t % GRID_W).astype(F32)
    pairs = HEAD // 4
    inv = ROPE_THETA ** (-jnp.arange(pairs, dtype=F32) / pairs)
    ang_r = row[:, None] * inv
    ang_c = col[:, None] * inv
    ang = jnp.concatenate([ang_r, ang_r, ang_c, ang_c], axis=-1)
    cos = jnp.concatenate([jnp.ones((nctx, HEAD), F32), jnp.cos(ang)], axis=0)
    sin = jnp.concatenate([jnp.zeros((nctx, HEAD), F32), jnp.sin(ang)], axis=0)
    lane = jnp.arange(HEAD)[None, :]
    first = (lane & 32) == 0
    return cos, jnp.where(first, -sin, 0.0), jnp.where(first, 0.0, sin)


def _pad_rows(v, rows):
    v = v.reshape(-1).astype(F32)
    return jnp.pad(v, (0, rows * 128 - v.shape[0])).reshape(rows, 128)


def _rows8(n):
    return -(-n // 1024) * 8


def kernel(x, c, ctx, c_ctx, ada_w, ada_b, norm_w, mlp_w1, mlp_w2, ev_w_in, ev_w_out, ev_q_norm, ev_k_norm, ev_sink, od_w_in, od_w_out, od_rpb, final_norm_w, loss_target, m_c_ctx, m_ada_w, m_ada_b, m_norm_w, m_mlp_w1, m_mlp_w2, m_ev_w_in, m_ev_w_out, m_ev_q_norm, m_ev_k_norm, m_ev_sink, m_od_w_in, m_od_w_out, m_od_rpb, m_final_norm_w, v_c_ctx, v_ada_w, v_ada_b, v_norm_w, v_mlp_w1, v_mlp_w2, v_ev_w_in, v_ev_w_out, v_ev_q_norm, v_ev_k_norm, v_ev_sink, v_od_w_in, v_od_w_out, v_od_rpb, v_final_norm_w):
    S, D = x.shape[1], x.shape[2]
    NC = ctx.shape[1]
    T = NC + S
    assert NC == ROW_TILE and S % GRID_W == 0
    ada_cols = ada_w.shape[2]
    nw_cols = norm_w.shape[2]
    me = 4 * lax.axis_index("x") + 2 * lax.axis_index("y") + lax.axis_index("c")

    pack1 = jnp.concatenate([_pad_rows(c, _rows8(D)), _pad_rows(norm_w, _rows8(4 * nw_cols))], axis=0)
    g1 = ag_small("ag_c_normw", pack1)
    c_all = g1[:, :D // 128].reshape(NDEV, D)
    nw_rows = _rows8(D)
    nw = g1[:, nw_rows:nw_rows + 4 * nw_cols // 128].reshape(NDEV, 2, 2, nw_cols)
    nw = nw.transpose(1, 2, 0, 3).reshape(2, 2, D)
    cin = jnp.concatenate([c_all, jnp.broadcast_to(c_ctx[None], (NDEV, D))], axis=0)
    act = _vmem_call("silu_c", lambda v: _silu(v).astype(BF16), jax.ShapeDtypeStruct((2 * NDEV, D), BF16), cin)
    ada_b_loc = lax.dynamic_slice_in_dim(ada_b, me * ada_cols, ada_cols, axis=1)
    mods = [mm_nn(f"mod{i}", act, ada_w[i], _epi_bias, [F32], extras=(ada_b_loc[i:i + 1],), extra_kinds=('n',))[0]
            for i in range(2)]
    gm = ag_small("ag_mod", jnp.concatenate(mods, axis=1))
    gm = gm.reshape(NDEV, 2 * NDEV, 2, ada_cols).transpose(2, 1, 0, 3).reshape(2, 2 * NDEV, NDEV * ada_cols)
    mod_lat = lax.dynamic_index_in_dim(gm, me, axis=1, keepdims=False)
    mod_ctx = gm[:, NDEV]
    mod2 = jnp.stack([mod_ctx, mod_lat], axis=1).reshape(2, 2, 6, D)

    def chunk(i, j):
        return mod2[i, :, j, :]

    def b16(w):
        return w.astype(BF16)

    (w_in_e,) = ag_big("ag_weights_l0_qkv", [b16(ev_w_in[0])])
    w_in_e = w_in_e.transpose(1, 0, 2).reshape(D, -1)

    cos, sa, sb = _rope_tables(S, NC)
    bias8 = na_span_bias(na_bias_table(od_rpb[0]))
    sink = ev_sink[0]
    TQ_F, TQ_B = 256, 256

    X0 = jnp.concatenate([ctx[0], x[0]], axis=0)
    h_a = norm_mod("l0_norm1", X0, nw[0, 0][None], chunk(0, 0), chunk(0, 1), NC)
    (qkv0,), (w_out_e_half,) = mm_nn("l0_qkv", h_a, w_in_e, _epi_store(F32), [F32], tn_cap=1024,
                                     carry=[ex_ag_chips([b16(ev_w_out[0])])])
    qkvh0 = prep_even("l0_prep", qkv0, ev_q_norm, ev_k_norm, cos, sa, sb)
    (o0, stats0), (w1_0_half, w2_0_half, w_out_o_half, w_out_e) = attn_even_fwd(
        "l0_attn", qkvh0, sink, NC, TQ_F,
        carry=[ex_ag_chips([b16(mlp_w1[0]), b16(mlp_w2[0]), b16(od_w_out[0])]), ex_ag_sibling([w_out_e_half])])
    w_out_e = w_out_e.reshape(-1, D)
    tm0 = _tile(T, 1100)
    (X1, y0), (w1_0, w2_0) = mm_nn("l0_out", o0, w_out_e, _epi_resid_gate(NC, tm0), [F32, F32],
                                   extras=(X0, chunk(0, 2)), extra_kinds=('mn', 'n'),
                                   carry=[ex_ag_sibling([w1_0_half, w2_0_half])])
    h_b = norm_mod("l0_norm2", X1, nw[0, 1][None], chunk(0, 3), chunk(0, 4), NC)
    (a0, r0), (w_in_o_half, w_out_o) = mm_nn(
        "l0_up", h_b, w1_0, _epi_relu2, [BF16, BF16], tn_cap=1024,
        carry=[ex_ag_chips([b16(od_w_in[0])]), ex_ag_sibling([w_out_o_half])])
    (X2, z0), (w1_1_half, w_in_o) = mm_nn(
        "l0_down", a0, w2_0.reshape(-1, D), _epi_resid_gate(NC, tm0), [F32, F32], extras=(X1, chunk(0, 5)),
        extra_kinds=('mn', 'n'), tn_cap=1024, carry=[ex_ag_chips([b16(mlp_w1[1])]), ex_ag_sibling([w_in_o_half])])
    w_out_o = w_out_o.reshape(-1, D)

    h_c = norm_mod("l1_norm1", X2, nw[1, 0][None], chunk(1, 0), chunk(1, 1), NC)
    (qkv1,), (w1_1,) = mm_nn("l1_qkv", h_c, w_in_o, _epi_store(BF16), [BF16], tn_cap=768,
                             carry=[ex_ag_sibling([w1_1_half])])
    (o1, stats1), (w2_1_half,) = attn_odd_fwd("l1_attn", qkv1, bias8, NC, carry=[ex_ag_chips([b16(mlp_w2[1])])])
    X2l = X2[NC:]
    tm1 = _tile(S, 1100)
    (X3, y1), (w2_1,) = mm_nn("l1_out", o1, w_out_o, _epi_resid_gate(0, tm1), [F32, F32],
                              extras=(X2l, chunk(1, 2)), extra_kinds=('mn', 'n'),
                              carry=[ex_ag_sibling([w2_1_half])])
    h_d = norm_mod("l1_norm2", X3, nw[1, 1][None], chunk(1, 3), chunk(1, 4), 0)
    a1, r1 = mm_nn("l1_up", h_d, w1_1, _epi_relu2, [BF16, BF16], tn_cap=1024)
    X4, z1 = mm_nn("l1_down", a1, w2_1.reshape(-1, D), _epi_resid_gate(0, tm1), [F32, F32], extras=(X3, chunk(1, 5)),
                   extra_kinds=('mn', 'n'), tn_cap=1024)
    dX4, loss_p, dfw_p, dz1, pg2_1 = final_loss("final_loss", X4, final_norm_w[None], loss_target[0], z1, chunk(1, 5))
    w_in = [w_in_e, w_in_o]
    w_out = [w_out_e, w_out_o]
    w1 = [w1_0, w1_1]
    w2 = [w2_0.reshape(-1, D), w2_1.reshape(-1, D)]

    mc4 = (lax.axis_index("c") * 4).astype(jnp.int32)
    my_chip = (2 * lax.axis_index("x") + lax.axis_index("y")).astype(jnp.int32)

    def chip_sum(tag, g8, sib4):
        return rs_chip_sum(f"rs_chip_sum_{tag}", g8, sib4, jnp.stack([mc4, my_chip]))

    du1 = mm_nt("l1_down_dx", dz1, w2[1], _epi_mul2r, BF16, extras=(r1,))
    g_w1_1 = mm_tn("l1_up_dw", h_d, du1, 1)
    g_w2_1, (sib_w1_1,) = mm_tn("l1_down_dw", a1, dz1, 0, carry=[ex_rs_sibling([g_w1_1])])
    dh_d, (rem_w1_1, sib_w2_1) = mm_nt(
        "l1_up_dx", du1, w1[1], _epi_store(F32), F32,
        carry=[ex_rs_chips([chip_sum("w1_1", g_w1_1, sib_w1_1)]), ex_rs_sibling([g_w2_1])])
    dX3, pn2_1, dy1, pg1_1 = norm_bwd("l1_norm2_bwd", X3, dh_d, dX4, nw[1, 1][None], chunk(1, 4), 0,
                                      gate=(y1, chunk(1, 2)))
    do1 = mm_nt("l1_out_dx", dy1, w_out[1], _epi_store(BF16), BF16)
    g_wout_1 = mm_tn("l1_out_dw", o1, dy1, 0)
    (dq1, dk1, dv1, dbias8), (rem_w2_1, sib_wout_1) = attn_odd_bwd(
        "l1_attn_bwd", qkv1, bias8, do1, o1, stats1, NC,
        carry=[ex_rs_chips([chip_sum("w2_1", g_w2_1, sib_w2_1)]), ex_rs_sibling([g_wout_1])])
    dqkv1 = jnp.concatenate([jnp.pad(dq1, ((NC, 0), (0, 0))), dk1, dv1], axis=1).astype(BF16)
    dh_c, (rem_wout_1,) = mm_nt("l1_qkv_dx", dqkv1, w_in[1], _epi_store(F32), F32,
                                carry=[ex_rs_chips([chip_sum("wout_1", g_wout_1, sib_wout_1)])])
    g_win_1 = mm_tn("l1_qkv_dw", h_c, dqkv1, 1)
    dX2, pn1_1, dz0, pg2_0 = norm_bwd("l1_norm1_bwd", X2, dh_c, dX3, nw[1, 0][None], chunk(1, 1), NC, dres_skip=NC,
                                      gate=(z0, chunk(0, 5)))
    d_rpb = na_bias_grad("rpb_grad", na_span_bias_grad(dbias8))

    du0, (sib_win_1,) = mm_nt("l0_down_dx", dz0, w2[0], _epi_mul2r, BF16, extras=(r0,),
                              carry=[ex_rs_sibling([g_win_1])])
    g_w1_0, (rem_win_1,) = mm_tn("l0_up_dw", h_b, du0, 1,
                                 carry=[ex_rs_chips([chip_sum("win_1", g_win_1, sib_win_1)])])
    g_w2_0, (sib_w1_0,) = mm_tn("l0_down_dw", a0, dz0, 0, carry=[ex_rs_sibling([g_w1_0])])
    dh_b, (rem_w1_0, sib_w2_0) = mm_nt(
        "l0_up_dx", du0, w1[0], _epi_store(F32), F32,
        carry=[ex_rs_chips([chip_sum("w1_0", g_w1_0, sib_w1_0)]), ex_rs_sibling([g_w2_0])])
    dX1, pn2_0, dy0, pg1_0 = norm_bwd("l0_norm2_bwd", X1, dh_b, dX2, nw[0, 1][None], chunk(0, 4), NC,
                                      gate=(y0, chunk(0, 2)))
    do0 = mm_nt("l0_out_dx", dy0, w_out[0], _epi_store(BF16), BF16)
    g_wout_0 = mm_tn("l0_out_dw", o0, dy0, 0)
    (dq0, dk0, dv0, dsink_p), (rem_w2_0, sib_wout_0) = attn_even_bwd(
        "l0_attn_bwd", qkvh0, sink, do0, o0, stats0, NC, TQ_B,
        carry=[ex_rs_chips([chip_sum("w2_0", g_w2_0, sib_w2_0)]), ex_rs_sibling([g_wout_0])])
    dqkv0, pqk = prep_even_bwd("l0_prep_bwd", qkv0, dq0, dk0, dv0, ev_q_norm, ev_k_norm, cos, sa, sb)
    g_win_0, (rem_wout_0,) = mm_tn("l0_qkv_dw", h_a, dqkv0, 1,
                                   carry=[ex_rs_chips([chip_sum("wout_0", g_wout_0, sib_wout_0)])])
    dh_a, (sib_win_0,) = mm_nt("l0_qkv_dx", dqkv0, w_in[0], _epi_store(F32), F32, carry=[ex_rs_sibling([g_win_0])])
    (dx_lat, pn1_0), (rem_win_0,) = norm_bwd(
        "l0_norm1_bwd", X0, dh_a, dX1, nw[0, 0][None], chunk(0, 1), NC, out_skip=NC,
        carry=[ex_rs_chips([chip_sum("win_0", g_win_0, sib_win_0)])])
    grad_x = dx_lat[None]

    def dmod(grp, pn1, pg1, pn2, pg2):
        return jnp.concatenate([pn1[grp], pn1[2 + grp], pg1[grp], pn2[grp], pn2[2 + grp], pg2[grp]])

    dmod_lat = jnp.stack([dmod(1, pn1_0, pg1_0, pn2_0, pg2_0), dmod(1, pn1_1, pg1_1, pn2_1, pg2_1)])
    dmod_ctx = jnp.stack([dmod(0, pn1_0, pg1_0, pn2_0, pg2_0), dmod(0, pn1_1, pg1_1, pn2_1, pg2_1)])
    dnw_p = jnp.stack([pn1_0[4], pn2_0[4], pn1_1[4], pn2_1[4]])
    pieces = [dmod_lat, dmod_ctx, dnw_p, pqk[0], pqk[1], dsink_p[8:, 0, 0], d_rpb, dfw_p[0], loss_p[0, 0]]
    sizes = [int(np.prod(p.shape)) for p in pieces]
    rows = [_rows8(s) for s in sizes]
    pack2 = jnp.concatenate([_pad_rows(p, r) for p, r in zip(pieces, rows)], axis=0)
    g2, s2 = ag_small("ag_small_grads", pack2, with_sum=True)
    offs = np.concatenate([[0], np.cumsum(rows)])

    def piece(arr, i, shape):
        return arr[..., offs[i]:offs[i + 1], :].reshape(arr.shape[:-2] + (-1,))[..., :sizes[i]].reshape(
            arr.shape[:-2] + shape)

    dmod_all = piece(g2, 0, (2, 6 * D))
    dmodc_sum = piece(s2, 1, (2, 6 * D))
    dnw_sum = piece(s2, 2, (2, 2, D))
    g_qn = piece(s2, 3, ev_q_norm.shape)
    g_kn = piece(s2, 4, ev_k_norm.shape)
    g_sink = piece(s2, 5, ev_sink.shape)
    g_rpb = piece(s2, 6, od_rpb.shape)
    g_fw = piece(s2, 7, final_norm_w.shape)
    loss = piece(s2, 8, ())

    dm16 = jnp.concatenate([dmod_all.transpose(1, 0, 2), dmodc_sum[:, None, :],
                            jnp.zeros((2, NDEV - 1, 6 * D), F32)], axis=1)
    dm16_loc = lax.dynamic_slice_in_dim(dm16.reshape(2, 2 * NDEV, NDEV, ada_cols), me, 1, axis=2)[:, :, 0, :]
    g_ada_b = _vmem_call("ada_b_grad", lambda v: jnp.sum(v, axis=1),
                         jax.ShapeDtypeStruct((2, 6 * D), F32), dm16)
    g_ada_w = []
    dact_p = None
    for i in range(2):
        dmb = dm16_loc[i].astype(BF16)
        g_ada_w.append(mm_tn(f"ada_w_grad{i}", act, dmb, None))
        part = mm_nt(f"ada_dact{i}", dmb, ada_w[i], _epi_store(F32), F32)
        dact_p = part if dact_p is None else dact_p + part
    _, dact = ag_small("ag_cctx", dact_p, with_sum=True)

    def cctx_grad(da, cc):
        sg = 1.0 / (1.0 + jnp.exp(-cc))
        return da[NDEV:NDEV + 1] * (sg * (1.0 + cc * (1.0 - sg)))

    g_cctx = _vmem_call("cctx_grad", cctx_grad, jax.ShapeDtypeStruct((1, D), F32), dact, c_ctx[None])[0]

    grads = [g_win_0, g_wout_0, g_w1_0, g_w2_0, g_win_1, g_wout_1, g_w1_1, g_w2_1]
    sib = [sib_win_0, sib_wout_0, sib_w1_0, sib_w2_0, sib_win_1, sib_wout_1, sib_w1_1, sib_w2_1]
    rem = [rem_win_0, rem_wout_0, rem_w1_0, rem_w2_0, rem_win_1, rem_wout_1, rem_w1_1, rem_w2_1]
    own_idx = jnp.stack([mc4 + my_chip, my_chip])

    def big(tag, w, m, v, ts):
        res = None
        for l, t in enumerate(ts):
            res = adamw_rs(f"adamw_{tag}_{l}", w, grads[t], sib[t], rem[t], m, v, own_idx, l, res)
        return tuple(res)

    r_ev_w_in = big('ev_w_in', ev_w_in, m_ev_w_in, v_ev_w_in, [0])
    r_ev_w_out = big('ev_w_out', ev_w_out, m_ev_w_out, v_ev_w_out, [1])
    r_mlp_w1 = big('mlp_w1', mlp_w1, m_mlp_w1, v_mlp_w1, [2, 6])
    r_mlp_w2 = big('mlp_w2', mlp_w2, m_mlp_w2, v_mlp_w2, [3, 7])
    r_od_w_in = big('od_w_in', od_w_in, m_od_w_in, v_od_w_in, [4])
    r_od_w_out = big('od_w_out', od_w_out, m_od_w_out, v_od_w_out, [5])

    g_ada = jnp.stack(g_ada_w)
    r_ada_w = adamw_rows("adamw_ada_w", ada_w.reshape(2 * D, ada_cols), g_ada.reshape(2 * D, ada_cols),
                         m_ada_w.reshape(2 * D, ada_cols), v_ada_w.reshape(2 * D, ada_cols))
    r_ada_w = tuple(u.reshape(2, D, ada_cols) for u in r_ada_w)

    g_nw_loc = lax.dynamic_slice_in_dim(dnw_sum, me * nw_cols, nw_cols, axis=2)
    small = [(c_ctx, g_cctx, m_c_ctx, v_c_ctx), (ada_b, g_ada_b, m_ada_b, v_ada_b),
             (norm_w, g_nw_loc, m_norm_w, v_norm_w), (ev_q_norm, g_qn, m_ev_q_norm, v_ev_q_norm),
             (ev_k_norm, g_kn, m_ev_k_norm, v_ev_k_norm), (ev_sink, g_sink, m_ev_sink, v_ev_sink),
             (od_rpb, g_rpb, m_od_rpb, v_od_rpb), (final_norm_w, g_fw, m_final_norm_w, v_final_norm_w)]
    srows = [_rows8(int(np.prod(w.shape))) for w, _, _, _ in small]
    packs = [jnp.concatenate([_pad_rows(tup[k], r) for tup, r in zip(small, srows)], axis=0) for k in range(4)]
    sres = adamw_rows("adamw_small", *packs)
    soffs = np.concatenate([[0], np.cumsum(srows)])

    def unpack(arr, i):
        w = small[i][0]
        return arr[soffs[i]:soffs[i + 1]].reshape(-1)[:int(np.prod(w.shape))].reshape(w.shape)

    sm = [[unpack(sres[k], i) for i in range(len(small))] for k in range(4)]

    def outs(k):
        big_k = {'ada_w': r_ada_w[k], 'mlp_w1': r_mlp_w1[k], 'mlp_w2': r_mlp_w2[k], 'ev_w_in': r_ev_w_in[k],
                 'ev_w_out': r_ev_w_out[k], 'od_w_in': r_od_w_in[k], 'od_w_out': r_od_w_out[k]}
        return (sm[k][0], big_k['ada_w'], sm[k][1], sm[k][2], big_k['mlp_w1'], big_k['mlp_w2'], big_k['ev_w_in'],
                big_k['ev_w_out'], sm[k][3], sm[k][4], sm[k][5], big_k['od_w_in'], big_k['od_w_out'], sm[k][6],
                sm[k][7])

    return (loss, grad_x, *outs(0), *outs(1), *outs(2), *outs(3))
```

```python
import numpy as np
import jax
import jax.numpy as jnp
from jax import lax
from jax.experimental import pallas as pl
from jax.experimental.pallas import tpu as pltpu

F32 = jnp.float32
BF16 = jnp.bfloat16
MESH = pl.DeviceIdType.MESH

NDEV = 8
HEAD = 128
GRID_W = 64
NA_KH, NA_KW = 8, 16
WINDOW = 128
ROPE_THETA = 10000.0
EPS = 1e-6
NEG = -1e30
SCALE = HEAD ** -0.5
ROW_TILE = 256
VMEM_LIMIT = 56 * 1024 * 1024

ADAM_LR, ADAM_B1, ADAM_B2, ADAM_EPS, ADAM_WD, ADAM_STEP = 0.001, 0.9, 0.999, 1e-08, 0.01, 10

NT = (((1,), (1,)), ((), ()))
NN = (((1,), (0,)), ((), ()))
TN = (((0,), (0,)), ((), ()))


def _cparams(sem):
    return pltpu.CompilerParams(dimension_semantics=sem, vmem_limit_bytes=VMEM_LIMIT)


def _tile(n, cap):
    if n <= cap:
        return n
    t = cap - cap % 64
    while t >= 64:
        if n % t == 0:
            return t
        t -= 64
    raise ValueError((n, cap))


def _dot(a, b, dims):
    return lax.dot_general(a.astype(BF16), b.astype(BF16), dims, preferred_element_type=F32)


def _slot(d):
    return (d % 2) * 4 + d // 2


class Exchange:
    def __init__(self, ins, out_shapes, aliases, n_sems, start, finish):
        self.ins, self.out_shapes, self.aliases, self.n_sems = list(ins), list(out_shapes), dict(aliases), n_sems
        self.start, self.finish = start, finish


def merge_exchanges(xs):
    ins, outs, aliases, bases, n = [], [], {}, [], 0
    for x in xs:
        bases.append((len(ins), len(outs), n))
        aliases.update({len(ins) + i: len(outs) + o for i, o in x.aliases.items()})
        ins += x.ins
        outs += x.out_shapes
        n += x.n_sems

    def run(which):
        def f(ci, co, ss, rs, base):
            for x, (i0, o0, s0) in zip(xs, bases):
                getattr(x, which)(ci[i0:i0 + len(x.ins)], co[o0:o0 + len(x.out_shapes)], ss, rs, base + s0)
        return f

    return Exchange(ins, outs, aliases, n, run('start'), run('finish'))


def _call(name, body, grid, ins, in_specs, out_shape, out_specs, scratch, sems, carry=None):
    if not carry:
        return pl.pallas_call(body, grid=grid, in_specs=in_specs, out_specs=out_specs, out_shape=out_shape,
                              scratch_shapes=scratch, compiler_params=_cparams(sems), name=name)(*ins)
    x = merge_exchanges(carry)
    n_in, n_ci, n_out, n_co, n_sc = len(ins), len(x.ins), len(out_shape), len(x.out_shapes), len(scratch)

    def wrapped(*refs):
        p = [0]

        def take(k):
            p[0] += k
            return refs[p[0] - k:p[0]]

        a, ci, o, co, sc = take(n_in), take(n_ci), take(n_out), take(n_co), take(n_sc)
        ss, rs = take(2)
        first = pl.program_id(0) == 0
        last = pl.program_id(0) == grid[0] - 1
        for d in range(1, len(grid)):
            first = jnp.logical_and(first, pl.program_id(d) == 0)
            last = jnp.logical_and(last, pl.program_id(d) == grid[d] - 1)

        @pl.when(first)
        def _():
            x.start(ci, co, ss, rs, 0)

        body(*a, *o, *sc)

        @pl.when(last)
        def _():
            x.finish(ci, co, ss, rs, 0)

    hbm = pl.BlockSpec(memory_space=pl.ANY)
    res = pl.pallas_call(
        wrapped, grid=grid, in_specs=list(in_specs) + [hbm] * n_ci, out_specs=list(out_specs) + [hbm] * n_co,
        out_shape=list(out_shape) + x.out_shapes,
        input_output_aliases={n_in + i: n_out + o for i, o in x.aliases.items()},
        scratch_shapes=list(scratch) + [pltpu.SemaphoreType.DMA((x.n_sems,)), pltpu.SemaphoreType.DMA((x.n_sems,))],
        compiler_params=_cparams(("arbitrary",) * len(grid)), name=name)(*ins, *x.ins)
    return list(res[:n_out]) + [list(res[n_out:])]


def _mm_core(name, grid, ins, in_specs, out_shape, out_specs, dims, acc_shape, epi, carry=None):
    nk = grid[2]
    n_extra = len(ins) - 2

    def body_single(*refs):
        epi(_dot(refs[0][...], refs[1][...], dims), refs[2:2 + n_extra], refs[2 + n_extra:])

    def body(*refs):
        a_ref, b_ref = refs[0], refs[1]
        ex = refs[2:2 + n_extra]
        outs = refs[2 + n_extra:-1]
        acc = refs[-1]
        k = pl.program_id(2)

        @pl.when(k == 0)
        def _():
            acc[...] = _dot(a_ref[...], b_ref[...], dims)

        @pl.when(jnp.logical_and(k > 0, k < nk - 1))
        def _():
            acc[...] += _dot(a_ref[...], b_ref[...], dims)

        @pl.when(k == nk - 1)
        def _():
            epi(acc[...] + _dot(a_ref[...], b_ref[...], dims), ex, outs)

    if nk == 1:
        return _call(name, body_single, grid, ins, in_specs, out_shape, out_specs, [],
                     ("parallel", "parallel", "arbitrary"), carry)
    return _call(name, body, grid, ins, in_specs, out_shape, out_specs, [pltpu.VMEM(acc_shape, F32)],
                 ("parallel", "parallel", "arbitrary"), carry)


def _split(res, n, carry):
    own = res[0] if n == 1 else list(res[:n])
    return (own, res[n]) if carry else own


def _epi_store(dtype):
    def epi(acc, ex, outs):
        outs[0][...] = acc.astype(dtype)
    return epi


def _epi_bias(acc, ex, outs):
    outs[0][...] = acc + ex[0][...]


def _epi_relu2(acc, ex, outs):
    r = jnp.maximum(acc, 0.0)
    outs[0][...] = (r * r).astype(BF16)
    outs[1][...] = r.astype(BF16)


def _epi_mul2r(acc, ex, outs):
    outs[0][...] = (acc * (2.0 * ex[0][...].astype(F32))).astype(BF16)


def _epi_resid_gate(nctx, tm):
    def epi(acc, ex, outs):
        rows = pl.program_id(0) * tm + lax.broadcasted_iota(jnp.int32, (tm, 1), 0)
        g = jnp.where(rows < nctx, ex[1][0:1, :], ex[1][1:2, :])
        outs[0][...] = ex[0][...] + g * acc
        outs[1][...] = acc
    return epi


def mm_nn(name, a, w, epi, outs, extras=(), extra_kinds=(), tm_cap=1100, tn_cap=512, tk_cap=2048, carry=None):
    M, K = a.shape
    if w.ndim == 3:
        ns = w.shape[2]
        N = NDEV * ns
        tn = _tile(ns, tn_cap)
        nper = ns // tn
    else:
        N = w.shape[1]
        tn = _tile(N, tn_cap)
    tm = _tile(M, tm_cap)
    tk = _tile(K, tk_cap)
    grid = (M // tm, N // tn, K // tk)
    a_spec = pl.BlockSpec((tm, tk), lambda i, j, k: (i, k))
    if w.ndim == 3:
        b_spec = pl.BlockSpec((None, tk, tn), lambda i, j, k: (j // nper, k, j % nper))
    else:
        b_spec = pl.BlockSpec((tk, tn), lambda i, j, k: (k, j))
    ex_specs = []
    for e, kind in zip(extras, extra_kinds):
        if kind == 'mn':
            ex_specs.append(pl.BlockSpec((tm, tn), lambda i, j, k: (i, j)))
        else:
            ex_specs.append(pl.BlockSpec((e.shape[0], tn), lambda i, j, k: (0, j)))
    out_shape = [jax.ShapeDtypeStruct((M, N), dt) for dt in outs]
    out_specs = [pl.BlockSpec((tm, tn), lambda i, j, k: (i, j)) for _ in outs]
    res = _mm_core(name, grid, (a, w, *extras), [a_spec, b_spec, *ex_specs], out_shape, out_specs, NN, (tm, tn), epi,
                   carry)
    return (list(res[:len(outs)]), res[len(outs)]) if carry else res


def mm_nt(name, a, w, epi, out_dtype, extras=(), tm_cap=1100, to_cap=1024, tc_cap=2048, carry=None):
    M, N = a.shape
    tm = _tile(M, tm_cap)
    if w.ndim == 3:
        Kw, ns = w.shape[1], w.shape[2]
        tc = _tile(ns, tc_cap)
        cper = ns // tc
    else:
        Kw = w.shape[0]
        tc = _tile(N, tc_cap)
    to = _tile(Kw, to_cap)
    grid = (M // tm, Kw // to, N // tc)
    a_spec = pl.BlockSpec((tm, tc), lambda i, j, k: (i, k))
    if w.ndim == 3:
        b_spec = pl.BlockSpec((None, to, tc), lambda i, j, k: (k // cper, j, k % cper))
    else:
        b_spec = pl.BlockSpec((to, tc), lambda i, j, k: (j, k))
    ex_specs = [pl.BlockSpec((tm, to), lambda i, j, k: (i, j)) for _ in extras]
    out_shape = [jax.ShapeDtypeStruct((M, Kw), out_dtype)]
    out_specs = [pl.BlockSpec((tm, to), lambda i, j, k: (i, j))]
    return _split(_mm_core(name, grid, (a, w, *extras), [a_spec, b_spec, *ex_specs], out_shape, out_specs, NT, (tm, to),
                           epi, carry), 1, carry)


def mm_tn(name, a, b, shard_axis, to_cap=1024, tn_cap=1024, tc_cap=2200, carry=None):
    M, Ka = a.shape
    N = b.shape[1]
    tc = _tile(M, tc_cap)
    if shard_axis is None:
        to, tn = _tile(Ka, to_cap), _tile(N, tn_cap)
        shape = (Ka, N)
        oblk = (to, tn)
        omap = lambda i, j, k: (i, j)
    elif shard_axis == 1:
        ns = N // NDEV
        to, tn = _tile(Ka, to_cap), _tile(ns, tn_cap)
        per = ns // tn
        shape = (NDEV, Ka, ns)
        oblk = (None, to, tn)
        omap = lambda i, j, k: (_slot(j // per), i, j % per)
    else:
        rs = Ka // NDEV
        to, tn = _tile(rs, to_cap), _tile(N, tn_cap)
        per = rs // to
        shape = (NDEV, rs, N)
        oblk = (None, to, tn)
        omap = lambda i, j, k: (_slot(i // per), i % per, j)
    grid = (Ka // to, N // tn, M // tc)
    a_spec = pl.BlockSpec((tc, to), lambda i, j, k: (k, i))
    b_spec = pl.BlockSpec((tc, tn), lambda i, j, k: (k, j))
    out_shape = [jax.ShapeDtypeStruct(shape, F32)]
    out_specs = [pl.BlockSpec(oblk, omap)]
    return _split(_mm_core(name, grid, (a, b), [a_spec, b_spec], out_shape, out_specs, TN, (to, tn), _epi_store(F32),
                           carry), 1, carry)


def _row_spec(D):
    return pl.BlockSpec((ROW_TILE, D), lambda i: (i, 0))


def _const_spec(r, D):
    return pl.BlockSpec((r, D), lambda i: (0, 0))


def _grp(ref, is_ctx):
    return jnp.where(is_ctx, ref[0:1, :], ref[1:2, :])


def norm_mod(name, x, nw, sh, sc, nctx):
    R, D = x.shape
    assert R % ROW_TILE == 0 and nctx % ROW_TILE == 0

    def body(x_ref, nw_ref, sh_ref, sc_ref, o_ref):
        is_ctx = pl.program_id(0) * ROW_TILE < nctx
        xv = x_ref[...]
        rstd = lax.rsqrt(jnp.mean(xv * xv, axis=-1, keepdims=True) + EPS)
        n = xv * rstd * nw_ref[...]
        o_ref[...] = (n * (1.0 + _grp(sc_ref, is_ctx)) + _grp(sh_ref, is_ctx)).astype(BF16)

    return pl.pallas_call(
        body, grid=(R // ROW_TILE,),
        in_specs=[_row_spec(D), _const_spec(1, D), _const_spec(2, D), _const_spec(2, D)],
        out_specs=_row_spec(D), out_shape=jax.ShapeDtypeStruct((R, D), BF16),
        compiler_params=_cparams(("parallel",)), name=name)(x, nw, sh, sc)


def _gate_rows(dxv, y_ref, g_ref, is_ctx, dy_ref, gpart_ref):
    dy_ref[...] = (dxv * _grp(g_ref, is_ctx)).astype(BF16)
    s = jnp.sum(dxv * y_ref[...], axis=0, keepdims=True)
    zero = jnp.zeros_like(s)
    gpart_ref[0:1, :] += jnp.where(is_ctx, s, zero)
    gpart_ref[1:2, :] += jnp.where(is_ctx, zero, s)


def norm_bwd(name, x, dh, dres, nw, sc, nctx, dres_skip=0, out_skip=0, carry=None, gate=None):
    R, D = x.shape
    assert R % ROW_TILE == 0 and nctx % ROW_TILE == 0 and dres_skip % ROW_TILE == 0 and out_skip % ROW_TILE == 0
    res_tiles, out_tiles = dres_skip // ROW_TILE, out_skip // ROW_TILE

    def body(x_ref, dh_ref, dres_ref, nw_ref, sc_ref, *rest):
        if gate is None:
            dx_ref, part_ref = rest
        else:
            y_ref, g_ref, dx_ref, part_ref, dy_ref, gpart_ref = rest
        i = pl.program_id(0)
        is_ctx = i * ROW_TILE < nctx

        @pl.when(i == 0)
        def _():
            part_ref[...] = jnp.zeros_like(part_ref)
            if gate is not None:
                gpart_ref[...] = jnp.zeros_like(gpart_ref)

        xv = x_ref[...]
        dhv = dh_ref[...]
        w = nw_ref[...]
        rstd = lax.rsqrt(jnp.mean(xv * xv, axis=-1, keepdims=True) + EPS)
        xhat = xv * rstd
        n = xhat * w
        dn = dhv * (1.0 + _grp(sc_ref, is_ctx))
        dxhat = dn * w
        dres = dres_ref[...]
        if res_tiles:
            dres = jnp.where(i < res_tiles, 0.0, dres)
        dxv = dres + rstd * (dxhat - xhat * jnp.mean(dxhat * xhat, axis=-1, keepdims=True))
        dx_ref[...] = dxv
        s_sh = jnp.sum(dhv, axis=0, keepdims=True)
        s_sc = jnp.sum(dhv * n, axis=0, keepdims=True)
        s_nw = jnp.sum(dn * xhat, axis=0, keepdims=True)
        zero = jnp.zeros_like(s_sh)
        part_ref[0:1, :] += jnp.where(is_ctx, s_sh, zero)
        part_ref[1:2, :] += jnp.where(is_ctx, zero, s_sh)
        part_ref[2:3, :] += jnp.where(is_ctx, s_sc, zero)
        part_ref[3:4, :] += jnp.where(is_ctx, zero, s_sc)
        part_ref[4:5, :] += s_nw
        if gate is not None:
            _gate_rows(dxv, y_ref, g_ref, is_ctx, dy_ref, gpart_ref)

    ins = [x, dh, dres, nw, sc]
    in_specs = [_row_spec(D), _row_spec(D), pl.BlockSpec((ROW_TILE, D), lambda i: (jnp.maximum(i - res_tiles, 0), 0)),
                _const_spec(1, D), _const_spec(2, D)]
    out_shape = [jax.ShapeDtypeStruct((R - out_skip, D), F32), jax.ShapeDtypeStruct((8, D), F32)]
    out_specs = [pl.BlockSpec((ROW_TILE, D), lambda i: (jnp.maximum(i - out_tiles, 0), 0)), _const_spec(8, D)]
    if gate is not None:
        assert out_skip == 0
        ins += list(gate)
        in_specs += [_row_spec(D), _const_spec(2, D)]
        out_shape += [jax.ShapeDtypeStruct((R, D), BF16), jax.ShapeDtypeStruct((8, D), F32)]
        out_specs += [_row_spec(D), _const_spec(8, D)]
    res = _call(name, body, (R // ROW_TILE,), ins, in_specs, out_shape, out_specs, [], ("arbitrary",), carry)
    return _split(res, len(out_shape), carry)


def final_loss(name, x, fw, tgt, y, g):
    S, D = x.shape

    def body(x_ref, fw_ref, t_ref, y_ref, g_ref, dx_ref, loss_ref, dfw_ref, dy_ref, gpart_ref):
        i = pl.program_id(0)

        @pl.when(i == 0)
        def _():
            loss_ref[...] = jnp.zeros_like(loss_ref)
            dfw_ref[...] = jnp.zeros_like(dfw_ref)
            gpart_ref[...] = jnp.zeros_like(gpart_ref)

        xv = x_ref[...]
        w = fw_ref[...]
        rstd = lax.rsqrt(jnp.mean(xv * xv, axis=-1, keepdims=True) + EPS)
        xhat = xv * rstd
        e = xhat * w - t_ref[...]
        loss_ref[...] += 0.5 * jnp.sum(jnp.mean(e * e, axis=-1, keepdims=True))
        dout = e * (1.0 / D)
        dfw_ref[0:1, :] += jnp.sum(dout * xhat, axis=0, keepdims=True)
        dxhat = dout * w
        dxv = rstd * (dxhat - xhat * jnp.mean(dxhat * xhat, axis=-1, keepdims=True))
        dx_ref[...] = dxv
        _gate_rows(dxv, y_ref, g_ref, False, dy_ref, gpart_ref)

    return pl.pallas_call(
        body, grid=(S // ROW_TILE,),
        in_specs=[_row_spec(D), _const_spec(1, D), _row_spec(D), _row_spec(D), _const_spec(2, D)],
        out_specs=[_row_spec(D), pl.BlockSpec((8, 128), lambda i: (0, 0)), _const_spec(8, D), _row_spec(D),
                   _const_spec(8, D)],
        out_shape=[jax.ShapeDtypeStruct((S, D), F32), jax.ShapeDtypeStruct((8, 128), F32),
                   jax.ShapeDtypeStruct((8, D), F32), jax.ShapeDtypeStruct((S, D), BF16),
                   jax.ShapeDtypeStruct((8, D), F32)],
        compiler_params=_cparams(("arbitrary",)), name=name)(x, fw, tgt, y, g)


def _rope(x, cos, sa, sb):
    return x * cos + pltpu.roll(x, 96, 1) * sa + pltpu.roll(x, 32, 1) * sb


def _rope_t(dy, cos, sa, sb):
    return dy * cos + pltpu.roll(dy * sa, 32, 1) + pltpu.roll(dy * sb, 96, 1)


_EVEN_KINDS = ['qa'] * 8 + ['ka'] * 2 + ['v'] * 2 + ['qb'] * 8 + ['kb'] * 2 + ['v'] * 2
_EVEN_DSRC = ([('q', j) for j in range(8)] + [('k', 0), ('k', 1), ('v', 0), ('v', 1)]
              + [('q', 8 + j) for j in range(8)] + [('k', 2), ('k', 3), ('v', 2), ('v', 3)])


def _cols(j):
    return slice(j * HEAD, (j + 1) * HEAD)


def prep_even(name, qkv, qn, kn, cos, sa, sb):
    T, W = qkv.shape

    def body(x_ref, qn_ref, kn_ref, cos_ref, sa_ref, sb_ref, o_ref):
        cos_, sa_, sb_ = cos_ref[...], sa_ref[...], sb_ref[...]
        for j, kind in enumerate(_EVEN_KINDS):
            x = x_ref[:, _cols(j)]
            if kind in ('qa', 'ka'):
                rstd = lax.rsqrt(jnp.mean(x * x, axis=-1, keepdims=True) + EPS)
                x = x * rstd * (qn_ref[...] if kind == 'qa' else kn_ref[...])
            if kind != 'v':
                x = _rope(x, cos_, sa_, sb_)
            o_ref[:, _cols(j)] = x.astype(BF16)

    blk = pl.BlockSpec((ROW_TILE, W), lambda i: (i, 0))
    tab = pl.BlockSpec((ROW_TILE, HEAD), lambda i: (i, 0))
    one = pl.BlockSpec((1, HEAD), lambda i: (0, 0))
    return pl.pallas_call(
        body, grid=(T // ROW_TILE,), in_specs=[blk, one, one, tab, tab, tab], out_specs=blk,
        out_shape=jax.ShapeDtypeStruct(qkv.shape, BF16),
        compiler_params=_cparams(("parallel",)), name=name)(qkv, qn, kn, cos, sa, sb)


def prep_even_bwd(name, qkv, dq, dk, dv, qn, kn, cos, sa, sb):
    T, W = qkv.shape

    def body(x_ref, dq_ref, dk_ref, dv_ref, qn_ref, kn_ref, cos_ref, sa_ref, sb_ref, o_ref, part_ref):
        @pl.when(pl.program_id(0) == 0)
        def _():
            part_ref[...] = jnp.zeros_like(part_ref)

        cos_, sa_, sb_ = cos_ref[...], sa_ref[...], sb_ref[...]
        src = {'q': dq_ref, 'k': dk_ref, 'v': dv_ref}
        sums = {'qa': None, 'ka': None}
        for j, kind in enumerate(_EVEN_KINDS):
            which, blk_j = _EVEN_DSRC[j]
            d = src[which][:, _cols(blk_j)]
            if kind != 'v':
                d = _rope_t(d, cos_, sa_, sb_)
            if kind in ('qa', 'ka'):
                x = x_ref[:, _cols(j)]
                rstd = lax.rsqrt(jnp.mean(x * x, axis=-1, keepdims=True) + EPS)
                xhat = x * rstd
                s = jnp.sum(d * xhat, axis=0, keepdims=True)
                sums[kind] = s if sums[kind] is None else sums[kind] + s
                dxhat = d * (qn_ref[...] if kind == 'qa' else kn_ref[...])
                d = rstd * (dxhat - xhat * jnp.mean(dxhat * xhat, axis=-1, keepdims=True))
            o_ref[:, _cols(j)] = d.astype(BF16)
        part_ref[0:1, :] += sums['qa']
        part_ref[1:2, :] += sums['ka']

    def rows(w):
        return pl.BlockSpec((ROW_TILE, w), lambda i: (i, 0))

    one = pl.BlockSpec((1, HEAD), lambda i: (0, 0))
    return pl.pallas_call(
        body, grid=(T // ROW_TILE,),
        in_specs=[rows(W), rows(dq.shape[1]), rows(dk.shape[1]), rows(dv.shape[1]), one, one,
                  rows(HEAD), rows(HEAD), rows(HEAD)],
        out_specs=[rows(W), pl.BlockSpec((8, HEAD), lambda i: (0, 0))],
        out_shape=[jax.ShapeDtypeStruct(qkv.shape, BF16), jax.ShapeDtypeStruct((8, HEAD), F32)],
        compiler_params=_cparams(("arbitrary",)), name=name)(qkv, dq, dk, dv, qn, kn, cos, sa, sb)


def _even_maps():
    qmap = lambda h, qb: (qb, jnp.where(h < 8, h, h + 4))
    kmap = lambda h, qb: (0, jnp.where(h < 8, 8 + h // 4, 18 + h // 4))
    vmap = lambda h, qb: (0, jnp.where(h < 8, 10 + h // 4, 20 + h // 4))
    return qmap, kmap, vmap


def _softmax_parts(parts, extra=None, stats=None, normalize=True):
    if stats is None:
        m = parts[0].max(axis=-1, keepdims=True)
        for p in parts[1:]:
            m = jnp.maximum(m, p.max(axis=-1, keepdims=True))
        if extra is not None:
            m = jnp.maximum(m, extra)
    else:
        m = stats[0]
    es = [jnp.exp(p - m) for p in parts]
    ex = None if extra is None else jnp.exp(extra - m)
    if stats is None:
        l = es[0].sum(axis=-1, keepdims=True)
        for e in es[1:]:
            l = l + e.sum(axis=-1, keepdims=True)
        if extra is not None:
            l = l + ex
        inv = 1.0 / l
    else:
        inv = stats[1]
    if not normalize:
        return es, ex, (m, inv)
    return [e * inv for e in es], (None if ex is None else ex * inv), (m, inv)


def _win_scores(q, k_ref, qb, tq, nctx, S):
    L = tq + 2 * WINDOW
    nqc = nctx // tq
    qlat = (qb - nqc) * tq
    start = pl.multiple_of(jnp.clip(qlat - WINDOW, 0, S - L), 128)
    kc = k_ref[0:nctx, :]
    kw = k_ref[pl.ds(nctx + start, L), :]
    s_c = _dot(q, kc, NT) * SCALE
    s_w = _dot(q, kw, NT) * SCALE
    qpos = qlat + lax.broadcasted_iota(jnp.int32, (tq, 1), 0)
    kpos = start + lax.broadcasted_iota(jnp.int32, (1, L), 1)
    valid = jnp.logical_and(jnp.abs(kpos - qpos) <= WINDOW, qb >= nqc)
    return s_c, jnp.where(valid, s_w, NEG), start, L


def _softmax_raw(raw, stats=None, normalize=True):
    m = raw.max(axis=-1, keepdims=True) if stats is None else stats[0]
    e = jnp.exp2((raw - m) * (SCALE * np.log2(np.e)))
    inv = 1.0 / e.sum(axis=-1, keepdims=True) if stats is None else stats[1]
    return (e * inv if normalize else e), (m, inv)


def _glob_keys(qb, tq, nctx, T):
    is_ctx = qb < nctx // tq
    return [(is_ctx, slice(0, nctx)), (jnp.logical_not(is_ctx), slice(0, T))]


def _stats_spec(tq):
    return pl.BlockSpec((None, None, tq, 2), lambda h, qb: (h, qb, 0, 0))


def attn_even_fwd(name, qkvh, sink, nctx, tq, carry=None):
    T = qkvh.shape[0]
    S = T - nctx
    qmap, kmap, vmap = _even_maps()

    def body(sink_ref, q_ref, k_ref, v_ref, o_ref, st_ref):
        h, qb = pl.program_id(0), pl.program_id(1)
        q = q_ref[...]

        for pred, keys in _glob_keys(qb, tq, nctx, T):
            @pl.when(jnp.logical_and(h < 8, pred))
            def _():
                e, (m, inv) = _softmax_raw(_dot(q, k_ref[keys, :], NT), normalize=False)
                st_ref[:, 0:1] = m
                st_ref[:, 1:2] = inv
                o_ref[...] = (_dot(e, v_ref[keys, :], NN) * inv).astype(BF16)

        @pl.when(h >= 8)
        def _():
            s_c, s_w, start, L = _win_scores(q, k_ref, qb, tq, nctx, S)
            sk = jnp.full((tq, 1), sink_ref[jnp.maximum(h - 8, 0)], F32)
            (e_c, e_w), _, (m, inv) = _softmax_parts([s_c, s_w], sk, normalize=False)
            st_ref[:, 0:1] = m
            st_ref[:, 1:2] = inv
            o = _dot(e_c, v_ref[0:nctx, :], NN) + _dot(e_w, v_ref[pl.ds(nctx + start, L), :], NN)
            o_ref[...] = (o * inv).astype(BF16)

    res = _call(name, body, (16, T // tq), (sink, qkvh, qkvh, qkvh),
                [pl.BlockSpec(memory_space=pltpu.SMEM), pl.BlockSpec((tq, HEAD), qmap),
                 pl.BlockSpec((T, HEAD), kmap), pl.BlockSpec((T, HEAD), vmap)],
                [jax.ShapeDtypeStruct((T, 16 * HEAD), BF16), jax.ShapeDtypeStruct((16, T // tq, tq, 2), F32)],
                [pl.BlockSpec((tq, HEAD), lambda h, qb: (qb, h)), _stats_spec(tq)],
                [], ("parallel", "arbitrary"), carry)
    return _split(res, 2, carry)


def attn_even_bwd(name, qkvh, sink, do, stats, nctx, tq, carry=None):
    T = qkvh.shape[0]
    assert stats.shape == (16, T // tq, tq, 2)
    S = T - nctx
    qmap, kmap, vmap = _even_maps()

    def body(sink_ref, q_ref, k_ref, v_ref, do_ref, st_ref, dq_ref, dk_ref, dv_ref, ds_ref):
        h, qb = pl.program_id(0), pl.program_id(1)
        q = q_ref[...]
        dov = do_ref[...]

        @pl.when(jnp.logical_and(h % 4 == 0, qb == 0))
        def _():
            dk_ref[...] = jnp.zeros_like(dk_ref)
            dv_ref[...] = jnp.zeros_like(dv_ref)

        @pl.when(qb == 0)
        def _():
            ds_ref[...] = jnp.zeros_like(ds_ref)

        for pred, keys in _glob_keys(qb, tq, nctx, T):
            @pl.when(jnp.logical_and(h < 8, pred))
            def _():
                p, _ = _softmax_raw(_dot(q, k_ref[keys, :], NT), (st_ref[:, 0:1], st_ref[:, 1:2]))
                dp = _dot(dov, v_ref[keys, :], NT)
                row = jnp.sum(p * dp, axis=-1, keepdims=True)
                dsb = (p * (dp - row) * SCALE).astype(BF16)
                dq_ref[...] = _dot(dsb, k_ref[keys, :], NN)
                dk_ref[keys, :] += _dot(dsb, q, TN)
                dv_ref[keys, :] += _dot(p, dov, TN)

        @pl.when(h >= 8)
        def _():
            s_c, s_w, start, L = _win_scores(q, k_ref, qb, tq, nctx, S)
            sk = jnp.full((tq, 1), sink_ref[jnp.maximum(h - 8, 0)], F32)
            (p_c, p_w), p_s, _ = _softmax_parts([s_c, s_w], sk, (st_ref[:, 0:1], st_ref[:, 1:2]))
            win = pl.ds(nctx + start, L)
            dp_c = _dot(dov, v_ref[0:nctx, :], NT)
            dp_w = _dot(dov, v_ref[win, :], NT)
            row = jnp.sum(p_c * dp_c, axis=-1, keepdims=True) + jnp.sum(p_w * dp_w, axis=-1, keepdims=True)
            ds_c = (p_c * (dp_c - row) * SCALE).astype(BF16)
            ds_w = (p_w * (dp_w - row) * SCALE).astype(BF16)
            dq_ref[...] = _dot(ds_c, k_ref[0:nctx, :], NN) + _dot(ds_w, k_ref[win, :], NN)
            dk_ref[0:nctx, :] += _dot(ds_c, q, TN)
            dk_ref[win, :] += _dot(ds_w, q, TN)
            dv_ref[0:nctx, :] += _dot(p_c, dov, TN)
            dv_ref[win, :] += _dot(p_w, dov, TN)
            ds_ref[...] += jnp.sum(-(p_s * row))

    kv_out = pl.BlockSpec((T, HEAD), lambda h, qb: (0, h // 4))
    head_blk = pl.BlockSpec((tq, HEAD), lambda h, qb: (qb, h))
    res = _call(name, body, (16, T // tq), (sink, qkvh, qkvh, qkvh, do, stats),
                [pl.BlockSpec(memory_space=pltpu.SMEM), pl.BlockSpec((tq, HEAD), qmap),
                 pl.BlockSpec((T, HEAD), kmap), pl.BlockSpec((T, HEAD), vmap), head_blk, _stats_spec(tq)],
                [jax.ShapeDtypeStruct((T, 16 * HEAD), F32), jax.ShapeDtypeStruct((T, 4 * HEAD), F32),
                 jax.ShapeDtypeStruct((T, 4 * HEAD), F32), jax.ShapeDtypeStruct((16, 8, 128), F32)],
                [pl.BlockSpec((tq, HEAD), lambda h, qb: (qb, h)), kv_out, kv_out,
                 pl.BlockSpec((None, 8, 128), lambda h, qb: (h, 0, 0))],
                [], ("arbitrary", "arbitrary"), carry)
    return _split(res, 4, carry)


NA_GROUP = 4
NA_SPAN = NA_KH + NA_GROUP - 1
_NA_PLAN = [[(j, 0) for j in range(NA_GROUP)],
            [(NA_KH // 2, j) for j in range(NA_GROUP)],
            [(NA_KH // 2 + j, NA_GROUP - 1) for j in range(NA_GROUP)]]


def _na_group(g, n_groups, rows):
    last = g == n_groups - 1
    kind = jnp.where(g == 0, 0, jnp.where(last, 2, 1))
    first_row = jnp.where(g == 0, 0, jnp.where(last, rows - NA_SPAN, NA_GROUP * g - NA_KH // 2))
    return kind, first_row


def na_span_bias(bias8):
    LW, LS = NA_KH * GRID_W, NA_SPAN * GRID_W
    kinds = []
    for plan in _NA_PLAN:
        strips = [jnp.pad(bias8[:, off], ((0, 0), (0, 0), (s * GRID_W, LS - LW - s * GRID_W)), constant_values=NEG)
                  for off, s in plan]
        kinds.append(jnp.concatenate(strips, axis=1))
    return jnp.stack(kinds, axis=1)


def na_span_bias_grad(db):
    LW = NA_KH * GRID_W
    out = [None] * NA_KH
    for kind, plan in enumerate(_NA_PLAN):
        for j, (off, s) in enumerate(plan):
            piece = db[:, kind, j * GRID_W:(j + 1) * GRID_W, s * GRID_W:s * GRID_W + LW]
            out[off] = piece if out[off] is None else out[off] + piece
    return jnp.stack(out, axis=1)


def _na_specs(T, nctx, n_groups, rows):
    LS = NA_SPAN * GRID_W
    tq = NA_GROUP * GRID_W
    assert nctx % tq == 0 and n_groups >= 3
    q_spec = pl.BlockSpec((tq, HEAD), lambda h, g: (g + nctx // tq, h))
    k_spec = pl.BlockSpec((T, HEAD), lambda h, g: (0, 16 + h))
    v_spec = pl.BlockSpec((T, HEAD), lambda h, g: (0, 32 + h))
    b_spec = pl.BlockSpec((None, None, tq, LS), lambda h, g: (h, _na_group(g, n_groups, rows)[0], 0, 0))
    row_spec = pl.BlockSpec((tq, HEAD), lambda h, g: (g, h))
    return q_spec, k_spec, v_spec, b_spec, row_spec


def _na_scores(q, k_ref, b_ref, g, n_groups, rows, nctx):
    first_row = _na_group(g, n_groups, rows)[1]
    win = pl.ds(pl.multiple_of(nctx + first_row * GRID_W, GRID_W), NA_SPAN * GRID_W)
    s_c = _dot(q, k_ref[0:nctx, :], NT) * SCALE
    s_w = _dot(q, k_ref[win, :], NT) * SCALE + b_ref[...]
    return s_c, s_w, win


def attn_odd_fwd(name, qkv, bias_s, nctx, carry=None):
    T = qkv.shape[0]
    S = T - nctx
    rows = S // GRID_W
    n_groups = rows // NA_GROUP
    q_spec, k_spec, v_spec, b_spec, row_spec = _na_specs(T, nctx, n_groups, rows)

    def body(q_ref, k_ref, v_ref, b_ref, o_ref, st_ref):
        s_c, s_w, win = _na_scores(q_ref[...], k_ref, b_ref, pl.program_id(1), n_groups, rows, nctx)
        (e_c, e_w), _, (m, inv) = _softmax_parts([s_c, s_w], normalize=False)
        st_ref[:, 0:1] = m
        st_ref[:, 1:2] = inv
        o_ref[...] = ((_dot(e_c, v_ref[0:nctx, :], NN) + _dot(e_w, v_ref[win, :], NN)) * inv).astype(BF16)

    tq = NA_GROUP * GRID_W
    res = _call(name, body, (16, n_groups), (qkv, qkv, qkv, bias_s), [q_spec, k_spec, v_spec, b_spec],
                [jax.ShapeDtypeStruct((S, 16 * HEAD), BF16), jax.ShapeDtypeStruct((16, n_groups, tq, 2), F32)],
                [row_spec, _stats_spec(tq)], [], ("parallel", "arbitrary"), carry)
    return _split(res, 2, carry)


def attn_odd_bwd(name, qkv, bias_s, do, stats, nctx, carry=None):
    T = qkv.shape[0]
    S = T - nctx
    rows = S // GRID_W
    n_groups = rows // NA_GROUP
    q_spec, k_spec, v_spec, b_spec, row_spec = _na_specs(T, nctx, n_groups, rows)

    def body(q_ref, k_ref, v_ref, b_ref, do_ref, st_ref, dq_ref, dk_ref, dv_ref, db_ref):
        g = pl.program_id(1)
        q = q_ref[...]
        dov = do_ref[...]

        @pl.when(g == 0)
        def _():
            dk_ref[...] = jnp.zeros_like(dk_ref)
            dv_ref[...] = jnp.zeros_like(dv_ref)

        s_c, s_w, win = _na_scores(q, k_ref, b_ref, g, n_groups, rows, nctx)
        (p_c, p_w), _, _ = _softmax_parts([s_c, s_w], None, (st_ref[:, 0:1], st_ref[:, 1:2]))
        dp_c = _dot(dov, v_ref[0:nctx, :], NT)
        dp_w = _dot(dov, v_ref[win, :], NT)
        row = jnp.sum(p_c * dp_c, axis=-1, keepdims=True) + jnp.sum(p_w * dp_w, axis=-1, keepdims=True)
        dsw = p_w * (dp_w - row)
        first_visit = jnp.logical_or(g <= 1, g == n_groups - 1)

        @pl.when(first_visit)
        def _():
            db_ref[...] = dsw

        @pl.when(jnp.logical_not(first_visit))
        def _():
            db_ref[...] += dsw

        ds_c = (p_c * (dp_c - row) * SCALE).astype(BF16)
        ds_w = (dsw * SCALE).astype(BF16)
        dq_ref[...] = _dot(ds_c, k_ref[0:nctx, :], NN) + _dot(ds_w, k_ref[win, :], NN)
        dk_ref[0:nctx, :] += _dot(ds_c, q, TN)
        dk_ref[win, :] += _dot(ds_w, q, TN)
        dv_ref[0:nctx, :] += _dot(p_c, dov, TN)
        dv_ref[win, :] += _dot(p_w, dov, TN)

    kv_out = pl.BlockSpec((T, HEAD), lambda h, g: (0, h))
    res = _call(name, body, (16, n_groups), (qkv, qkv, qkv, bias_s, do, stats),
                [q_spec, k_spec, v_spec, b_spec, row_spec, _stats_spec(NA_GROUP * GRID_W)],
                [jax.ShapeDtypeStruct((S, 16 * HEAD), F32), jax.ShapeDtypeStruct((T, 16 * HEAD), F32),
                 jax.ShapeDtypeStruct((T, 16 * HEAD), F32), jax.ShapeDtypeStruct(bias_s.shape, F32)],
                [row_spec, kv_out, kv_out, b_spec], [], ("arbitrary", "arbitrary"), carry)
    return _split(res, 4, carry)


def _na_onehots():
    o = np.arange(NA_KH)[:, None]
    i = np.arange(NA_KH)[None, :]
    a = i - o + NA_KH - 1
    A = (a[..., None] == np.arange(2 * NA_KH - 1)).astype(np.float32)
    qc = np.arange(GRID_W)[:, None]
    kc = np.arange(GRID_W)[None, :]
    b = np.clip(kc - qc + NA_KW - 1, 0, 2 * NA_KW - 2)
    cs = np.clip(qc - NA_KW // 2, 0, GRID_W - NA_KW)
    valid = (kc >= cs) & (kc < cs + NA_KW)
    B = ((b[..., None] == np.arange(2 * NA_KW - 1)) & valid[..., None]).astype(np.float32)
    return A, B, valid


def na_bias_table(rpb):
    A, B, valid = _na_onehots()
    hp = lax.Precision.HIGHEST
    t = jnp.einsum('hab,oia->hoib', rpb, jnp.asarray(A), precision=hp)
    bias = jnp.einsum('hoib,qkb->hoqik', t, jnp.asarray(B), precision=hp)
    bias = jnp.where(jnp.asarray(valid)[None, None, :, None, :], bias, NEG)
    return bias.reshape(rpb.shape[0], NA_KH, GRID_W, NA_KH * GRID_W)


def na_bias_grad(name, dbias8):
    A, B, _ = _na_onehots()
    H = dbias8.shape[0]
    nb, na = 2 * NA_KW - 1, 2 * NA_KH - 1
    d = dbias8.reshape(H, NA_KH, GRID_W, NA_KH, GRID_W).transpose(0, 1, 3, 2, 4)
    d = d.reshape(H * NA_KH * NA_KH, GRID_W * GRID_W)
    Bp = np.zeros((GRID_W * GRID_W, 128), np.float32)
    Bp[:, :nb] = B.reshape(GRID_W * GRID_W, nb)
    Ap = np.zeros((16, NA_KH * NA_KH), np.float32)
    Ap[:na] = A.reshape(NA_KH * NA_KH, na).T
    rows_per_head = NA_KH * NA_KH

    def split3(x):
        hi = x.astype(BF16)
        r1 = x - hi.astype(F32)
        mid = r1.astype(BF16)
        return hi, mid, (r1 - mid.astype(F32)).astype(BF16)

    def body(d_ref, b_ref, a_ref, o_ref):
        bm, am = b_ref[...], a_ref[...]
        g = sum(lax.dot_general(p, bm, NN, preferred_element_type=F32) for p in split3(d_ref[...]))
        o_ref[...] = sum(lax.dot_general(am, p, NN, preferred_element_type=F32) for p in split3(g))

    out = pl.pallas_call(
        body, grid=(H,),
        in_specs=[pl.BlockSpec((rows_per_head, GRID_W * GRID_W), lambda h: (h, 0)),
                  pl.BlockSpec((GRID_W * GRID_W, 128), lambda h: (0, 0)),
                  pl.BlockSpec((16, rows_per_head), lambda h: (0, 0))],
        out_specs=pl.BlockSpec((None, 16, 128), lambda h: (h, 0, 0)),
        out_shape=jax.ShapeDtypeStruct((H, 16, 128), F32),
        compiler_params=_cparams(("parallel",)), name=name)(d, jnp.asarray(Bp, BF16), jnp.asarray(Ap, BF16))
    return out[:, :na, :nb]


def _vmem_call(name, fn, out_shape, *arrays):
    def body(*refs):
        n = len(arrays)
        res = fn(*[r[...] for r in refs[:n]])
        if not isinstance(res, (tuple, list)):
            res = (res,)
        for o, v in zip(refs[n:], res):
            o[...] = v
    return pl.pallas_call(body, out_shape=out_shape, name=name,
                          compiler_params=pltpu.CompilerParams(vmem_limit_bytes=VMEM_LIMIT))(*arrays)


def _silu(v):
    return v / (1.0 + jnp.exp(-v))


def _adamw_math(w, g, m, v):
    m2 = ADAM_B1 * m + (1.0 - ADAM_B1) * g
    v2 = ADAM_B2 * v + (1.0 - ADAM_B2) * (g * g)
    m_hat = m2 / (1.0 - ADAM_B1 ** ADAM_STEP)
    v_hat = v2 / (1.0 - ADAM_B2 ** ADAM_STEP)
    delta = -ADAM_LR * (m_hat / (jnp.sqrt(v_hat) + ADAM_EPS) + ADAM_WD * w)
    return delta, m2, v2


def _ew_tile(R, C):
    return _tile(R, max(64, (262144 // C) // 64 * 64))


def adamw_rows(name, w, g, m, v, extra_g=None):
    R, C = w.shape
    tr = _ew_tile(R, C)
    extra_g = list(extra_g or [])
    ne = len(extra_g)

    def body(*refs):
        w_ref, g_ref, m_ref, v_ref = refs[:4]
        gs = g_ref[...]
        for e in refs[4:4 + ne]:
            gs = gs + e[...].astype(F32)
        go, do, mo, vo = refs[4 + ne:]
        d, m2, v2 = _adamw_math(w_ref[...], gs, m_ref[...], v_ref[...])
        go[...] = gs
        do[...] = d
        mo[...] = m2
        vo[...] = v2

    spec = pl.BlockSpec((tr, C), lambda i: (i, 0))
    return pl.pallas_call(
        body, grid=(R // tr,), in_specs=[spec] * (4 + ne), out_specs=[spec] * 4,
        out_shape=[jax.ShapeDtypeStruct((R, C), F32)] * 4,
        compiler_params=_cparams(("parallel",)), name=name)(w, g, m, v, *extra_g)


def rs_chip_sum(name, g8, sib4, where):
    _, R, C = g8.shape
    tr = _ew_tile(R, C)

    def body(s_ref, g_ref, b_ref, o_ref):
        o_ref[...] = (g_ref[...] + b_ref[...]).astype(BF16)

    def chip(q, s):
        return (s[1] + 1 + q) % 4

    blk = (None, tr, C)
    grid_spec = pltpu.PrefetchScalarGridSpec(
        num_scalar_prefetch=1, grid=(3, R // tr),
        in_specs=[pl.BlockSpec(blk, lambda q, i, s: (s[0] + chip(q, s), i, 0)),
                  pl.BlockSpec(blk, lambda q, i, s: (chip(q, s), i, 0))],
        out_specs=pl.BlockSpec(blk, lambda q, i, s: (chip(q, s), i, 0)))
    return pl.pallas_call(body, grid_spec=grid_spec, out_shape=jax.ShapeDtypeStruct((4, R, C), BF16),
                          compiler_params=_cparams(("parallel", "parallel")), name=name)(where, g8, sib4)


def adamw_rs(name, w, g8, sib4, rem3, m, v, idx, layer, prev=None):
    L, R, C = w.shape
    tr = _ew_tile(R, C)

    def body(s_ref, w_ref, g_ref, sb_ref, r0_ref, r1_ref, r2_ref, m_ref, v_ref, *rest):
        go, do, mo, vo = rest[-4:]
        gs = g_ref[...] + sb_ref[...]
        for r_ref in (r0_ref, r1_ref, r2_ref):
            gs = gs + r_ref[...].astype(F32)
        d, m2, v2 = _adamw_math(w_ref[...], gs, m_ref[...], v_ref[...])
        go[...] = gs
        do[...] = d
        mo[...] = m2
        vo[...] = v2

    blk = (None, tr, C)
    mine = pl.BlockSpec(blk, lambda i, s: (layer, i, 0))

    def rem(k):
        return pl.BlockSpec(blk, lambda i, s: (k, i, 0))

    prev = list(prev or [])
    grid_spec = pltpu.PrefetchScalarGridSpec(
        num_scalar_prefetch=1, grid=(R // tr,),
        in_specs=[mine, pl.BlockSpec(blk, lambda i, s: (s[0], i, 0)), pl.BlockSpec(blk, lambda i, s: (s[1], i, 0)),
                  rem(0), rem(1), rem(2), mine, mine] + [pl.BlockSpec(memory_space=pl.ANY)] * len(prev),
        out_specs=[mine] * 4)
    return pl.pallas_call(body, grid_spec=grid_spec, out_shape=[jax.ShapeDtypeStruct((L, R, C), F32)] * 4,
                          input_output_aliases={9 + k: k for k in range(len(prev))},
                          compiler_params=_cparams(("parallel",)), name=name)(
                              idx, w, g8, sib4, rem3, rem3, rem3, m, v, *prev)


def _me():
    x, y, c = lax.axis_index("x"), lax.axis_index("y"), lax.axis_index("c")
    return x, y, c


def _flip(v, bit):
    return 1 - v if bit else v


def ag_small(name, x, with_sum=False):
    R, C = x.shape

    def body(x_ref, out_ref, *rest):
        if with_sum:
            sum_ref, send_sems, recv_sems, lsem = rest
        else:
            send_sems, recv_sems, lsem = rest
        mx, my, mc = _me()
        me = 4 * mx + 2 * my + mc
        local = pltpu.make_async_copy(x_ref, out_ref.at[me], lsem)
        local.start()
        sends = []
        for k in range(1, NDEV):
            peer = (_flip(mx, k & 4), _flip(my, k & 2), _flip(mc, k & 1))
            cp = pltpu.make_async_remote_copy(src_ref=x_ref, dst_ref=out_ref.at[me], send_sem=send_sems.at[k - 1],
                                              recv_sem=recv_sems.at[k - 1], device_id=peer, device_id_type=MESH)
            cp.start()
            sends.append(cp)
        for k in range(1, NDEV):
            px, py, pc = _flip(mx, k & 4), _flip(my, k & 2), _flip(mc, k & 1)
            pltpu.make_async_remote_copy(src_ref=x_ref, dst_ref=out_ref.at[4 * px + 2 * py + pc],
                                         send_sem=send_sems.at[k - 1], recv_sem=recv_sems.at[k - 1],
                                         device_id=(px, py, pc), device_id_type=MESH).wait_recv()
        for cp in sends:
            cp.wait_send()
        local.wait()
        if with_sum:
            acc = out_ref[0]
            for d in range(1, NDEV):
                acc = acc + out_ref[d]
            sum_ref[...] = acc

    out_shape = [jax.ShapeDtypeStruct((NDEV, R, C), F32)]
    if with_sum:
        out_shape.append(jax.ShapeDtypeStruct((R, C), F32))
    vm = pl.BlockSpec(memory_space=pltpu.VMEM)
    res = pl.pallas_call(
        body, out_shape=out_shape, in_specs=[vm], out_specs=[vm] * len(out_shape),
        scratch_shapes=[pltpu.SemaphoreType.DMA((NDEV - 1,)), pltpu.SemaphoreType.DMA((NDEV - 1,)),
                        pltpu.SemaphoreType.DMA],
        compiler_params=pltpu.CompilerParams(vmem_limit_bytes=VMEM_LIMIT), name=name)(x)
    return res if with_sum else res[0]


def ag_big(name, shards):
    n = len(shards)

    def body(*refs):
        ins, outs = refs[:n], refs[n:2 * n]
        send_sems, recv_sems, lsems = refs[2 * n:]
        mx, my, mc = _me()
        me = (mx, my, mc)
        sibling = (mx, my, 1 - mc)
        chips = [(1 - mx, my), (mx, 1 - my), (1 - mx, 1 - my)]

        def idx(p):
            return 4 * p[0] + 2 * p[1] + p[2]

        def copy(t, k, block, to, src=None):
            dst = outs[t].at[idx(block)]
            return pltpu.make_async_remote_copy(
                src_ref=dst if src is None else src, dst_ref=dst, send_sem=send_sems.at[7 * t + k],
                recv_sem=recv_sems.at[7 * t + k], device_id=to, device_id_type=MESH)

        started = []
        locals_ = []
        for t in range(n):
            mine = pltpu.make_async_copy(ins[t], outs[t].at[idx(me)], lsems.at[t])
            mine.start()
            locals_.append(mine)
            first = [copy(t, 0, me, sibling, src=ins[t])]
            first += [copy(t, 1 + j, me, (*chip, mc), src=ins[t]) for j, chip in enumerate(chips)]
            for cp in first:
                cp.start()
            started += first
        for t in range(n):
            for j, chip in enumerate(chips):
                copy(t, 1 + j, (*chip, mc), me).wait_recv()
                fwd = copy(t, 4 + j, (*chip, mc), sibling)
                fwd.start()
                started.append(fwd)
        for t in range(n):
            copy(t, 0, sibling, me).wait_recv()
            for j, chip in enumerate(chips):
                copy(t, 4 + j, (*chip, 1 - mc), me).wait_recv()
        for cp in started:
            cp.wait_send()
        for mine in locals_:
            mine.wait()

    anyspec = pl.BlockSpec(memory_space=pl.ANY)
    return pl.pallas_call(
        body, out_shape=[jax.ShapeDtypeStruct((NDEV,) + s.shape, s.dtype) for s in shards],
        in_specs=[anyspec] * n, out_specs=[anyspec] * n,
        scratch_shapes=[pltpu.SemaphoreType.DMA((7 * n,)), pltpu.SemaphoreType.DMA((7 * n,)),
                        pltpu.SemaphoreType.DMA((n,))],
        name=name)(*shards)


def _idx(p):
    return 4 * p[0] + 2 * p[1] + p[2]


def _remote(src, dst, ss, rs, k, to):
    return pltpu.make_async_remote_copy(src_ref=src, dst_ref=dst, send_sem=ss.at[k], recv_sem=rs.at[k],
                                        device_id=to, device_id_type=MESH)


def ex_ag_chips(shards):
    n = len(shards)

    def copies(ci, co, ss, rs, base):
        mx, my, mc = _me()
        me = (mx, my, mc)
        peers = [(mx, my, 1 - mc), (1 - mx, my, mc), (mx, 1 - my, mc), (1 - mx, 1 - my, mc)]
        sends, recvs, local = [], [], []
        for t in range(n):
            b = base + 5 * t
            for k, peer in enumerate(peers):
                sends.append(_remote(ci[t], co[t].at[_idx(me)], ss, rs, b + k, peer))
                recvs.append(_remote(ci[t], co[t].at[_idx(peer)], ss, rs, b + k, peer))
            local.append(pltpu.make_async_copy(ci[t], co[t].at[_idx(me)], ss.at[b + 4]))
        return sends, recvs, local

    def start(ci, co, ss, rs, base):
        sends, _, local = copies(ci, co, ss, rs, base)
        for cp in local + sends:
            cp.start()

    def finish(ci, co, ss, rs, base):
        sends, recvs, local = copies(ci, co, ss, rs, base)
        for cp in recvs:
            cp.wait_recv()
        for cp in sends:
            cp.wait_send()
        for cp in local:
            cp.wait()

    outs = [jax.ShapeDtypeStruct((NDEV,) + s.shape, s.dtype) for s in shards]
    return Exchange(shards, outs, {}, 5 * n, start, finish)


def ex_ag_sibling(bufs):
    n = len(bufs)

    def copies(co, ss, rs, base):
        mx, my, mc = _me()
        sibling = (mx, my, 1 - mc)
        chips = [(1 - mx, my), (mx, 1 - my), (1 - mx, 1 - my)]
        sends, recvs = [], []
        for t in range(n):
            for j, chip in enumerate(chips):
                mine, theirs = co[t].at[_idx((*chip, mc))], co[t].at[_idx((*chip, 1 - mc))]
                sends.append(_remote(mine, mine, ss, rs, base + 3 * t + j, sibling))
                recvs.append(_remote(mine, theirs, ss, rs, base + 3 * t + j, sibling))
        return sends, recvs

    def start(ci, co, ss, rs, base):
        for cp in copies(co, ss, rs, base)[0]:
            cp.start()

    def finish(ci, co, ss, rs, base):
        sends, recvs = copies(co, ss, rs, base)
        for cp in recvs:
            cp.wait_recv()
        for cp in sends:
            cp.wait_send()

    outs = [jax.ShapeDtypeStruct(b.shape, b.dtype) for b in bufs]
    return Exchange(bufs, outs, {t: t for t in range(n)}, 3 * n, start, finish)


def ex_rs_sibling(grads):
    n = len(grads)

    def copies(ci, co, ss, rs, base):
        mx, my, mc = _me()
        return [_remote(ci[t].at[pl.ds((1 - mc) * 4, 4)], co[t], ss, rs, base + t, (mx, my, 1 - mc)) for t in range(n)]

    def start(ci, co, ss, rs, base):
        for cp in copies(ci, co, ss, rs, base):
            cp.start()

    def finish(ci, co, ss, rs, base):
        for cp in copies(ci, co, ss, rs, base):
            cp.wait()

    outs = [jax.ShapeDtypeStruct((4,) + g.shape[1:], g.dtype) for g in grads]
    return Exchange(grads, outs, {}, n, start, finish)


def ex_rs_chips(parts):
    n = len(parts)

    def copies(ci, co, ss, rs, base):
        mx, my, mc = _me()
        cps = []
        for t in range(n):
            for k in range(1, 4):
                px, py = _flip(mx, k & 2), _flip(my, k & 1)
                cps.append(_remote(ci[t].at[2 * px + py], co[t].at[k - 1], ss, rs, base + 3 * t + k - 1, (px, py, mc)))
        return cps

    def start(ci, co, ss, rs, base):
        for cp in copies(ci, co, ss, rs, base):
            cp.start()

    def finish(ci, co, ss, rs, base):
        for cp in copies(ci, co, ss, rs, base):
            cp.wait()

    outs = [jax.ShapeDtypeStruct((3,) + p.shape[1:], p.dtype) for p in parts]
    return Exchange(parts, outs, {}, 3 * n, start, finish)


def run_exchanges(name, xs):
    x = merge_exchanges(xs)
    n_ci, n_co = len(x.ins), len(x.out_shapes)

    def body(*refs):
        ci, co = refs[:n_ci], refs[n_ci:n_ci + n_co]
        ss, rs = refs[n_ci + n_co:]
        x.start(ci, co, ss, rs, 0)
        x.finish(ci, co, ss, rs, 0)

    hbm = pl.BlockSpec(memory_space=pl.ANY)
    return pl.pallas_call(
        body, out_shape=x.out_shapes, in_specs=[hbm] * n_ci, out_specs=[hbm] * n_co, input_output_aliases=x.aliases,
        scratch_shapes=[pltpu.SemaphoreType.DMA((x.n_sems,)), pltpu.SemaphoreType.DMA((x.n_sems,))], name=name)(*x.ins)


def _rope_tables(S, nctx):
    t = jnp.arange(S)
    row = (t // GRID_W).astype(F32)
    col = (t % GRID_W).astype(F32)
    pairs = HEAD // 4
    inv = ROPE_THETA ** (-jnp.arange(pairs, dtype=F32) / pairs)
    ang_r = row[:, None] * inv
    ang_c = col[:, None] * inv
    ang = jnp.concatenate([ang_r, ang_r, ang_c, ang_c], axis=-1)
    cos = jnp.concatenate([jnp.ones((nctx, HEAD), F32), jnp.cos(ang)], axis=0)
    sin = jnp.concatenate([jnp.zeros((nctx, HEAD), F32), jnp.sin(ang)], axis=0)
    lane = jnp.arange(HEAD)[None, :]
    first = (lane & 32) == 0
    return cos, jnp.where(first, -sin, 0.0), jnp.where(first, 0.0, sin)


def _pad_rows(v, rows):
    v = v.reshape(-1).astype(F32)
    return jnp.pad(v, (0, rows * 128 - v.shape[0])).reshape(rows, 128)


def _rows8(n):
    return -(-n // 1024) * 8


def kernel(x, c, ctx, c_ctx, ada_w, ada_b, norm_w, mlp_w1, mlp_w2, ev_w_in, ev_w_out, ev_q_norm, ev_k_norm, ev_sink, od_w_in, od_w_out, od_rpb, final_norm_w, loss_target, m_c_ctx, m_ada_w, m_ada_b, m_norm_w, m_mlp_w1, m_mlp_w2, m_ev_w_in, m_ev_w_out, m_ev_q_norm, m_ev_k_norm, m_ev_sink, m_od_w_in, m_od_w_out, m_od_rpb, m_final_norm_w, v_c_ctx, v_ada_w, v_ada_b, v_norm_w, v_mlp_w1, v_mlp_w2, v_ev_w_in, v_ev_w_out, v_ev_q_norm, v_ev_k_norm, v_ev_sink, v_od_w_in, v_od_w_out, v_od_rpb, v_final_norm_w):
    S, D = x.shape[1], x.shape[2]
    NC = ctx.shape[1]
    T = NC + S
    assert NC == ROW_TILE and S % GRID_W == 0
    ada_cols = ada_w.shape[2]
    nw_cols = norm_w.shape[2]
    me = 4 * lax.axis_index("x") + 2 * lax.axis_index("y") + lax.axis_index("c")

    pack1 = jnp.concatenate([_pad_rows(c, _rows8(D)), _pad_rows(norm_w, _rows8(4 * nw_cols))], axis=0)
    g1 = ag_small("ag_c_normw", pack1)
    c_all = g1[:, :D // 128].reshape(NDEV, D)
    nw_rows = _rows8(D)
    nw = g1[:, nw_rows:nw_rows + 4 * nw_cols // 128].reshape(NDEV, 2, 2, nw_cols)
    nw = nw.transpose(1, 2, 0, 3).reshape(2, 2, D)
    cin = jnp.concatenate([c_all, jnp.broadcast_to(c_ctx[None], (NDEV, D))], axis=0)
    act = _vmem_call("silu_c", lambda v: _silu(v).astype(BF16), jax.ShapeDtypeStruct((2 * NDEV, D), BF16), cin)
    ada_b_loc = lax.dynamic_slice_in_dim(ada_b, me * ada_cols, ada_cols, axis=1)
    mods = [mm_nn(f"mod{i}", act, ada_w[i], _epi_bias, [F32], extras=(ada_b_loc[i:i + 1],), extra_kinds=('n',))[0]
            for i in range(2)]
    gm = ag_small("ag_mod", jnp.concatenate(mods, axis=1))
    gm = gm.reshape(NDEV, 2 * NDEV, 2, ada_cols).transpose(2, 1, 0, 3).reshape(2, 2 * NDEV, NDEV * ada_cols)
    mod_lat = lax.dynamic_index_in_dim(gm, me, axis=1, keepdims=False)
    mod_ctx = gm[:, NDEV]
    mod2 = jnp.stack([mod_ctx, mod_lat], axis=1).reshape(2, 2, 6, D)

    def chunk(i, j):
        return mod2[i, :, j, :]

    def b16(w):
        return w.astype(BF16)

    (w_in_e,) = ag_big("ag_weights_l0_qkv", [b16(ev_w_in[0])])
    w_in_e = w_in_e.transpose(1, 0, 2).reshape(D, -1)

    cos, sa, sb = _rope_tables(S, NC)
    bias8 = na_span_bias(na_bias_table(od_rpb[0]))
    sink = ev_sink[0]
    TQ_F, TQ_B = 256, 256

    X0 = jnp.concatenate([ctx[0], x[0]], axis=0)
    h_a = norm_mod("l0_norm1", X0, nw[0, 0][None], chunk(0, 0), chunk(0, 1), NC)
    (qkv0,), (w_out_e_half,) = mm_nn("l0_qkv", h_a, w_in_e, _epi_store(F32), [F32], tn_cap=1024,
                                     carry=[ex_ag_chips([b16(ev_w_out[0])])])
    qkvh0 = prep_even("l0_prep", qkv0, ev_q_norm, ev_k_norm, cos, sa, sb)
    (o0, stats0), (w1_0_half, w2_0_half, w_out_o_half, w_out_e) = attn_even_fwd(
        "l0_attn", qkvh0, sink, NC, TQ_F,
        carry=[ex_ag_chips([b16(mlp_w1[0]), b16(mlp_w2[0]), b16(od_w_out[0])]), ex_ag_sibling([w_out_e_half])])
    w_out_e = w_out_e.reshape(-1, D)
    tm0 = _tile(T, 1100)
    (X1, y0), (w1_0, w2_0) = mm_nn("l0_out", o0, w_out_e, _epi_resid_gate(NC, tm0), [F32, F32],
                                   extras=(X0, chunk(0, 2)), extra_kinds=('mn', 'n'),
                                   carry=[ex_ag_sibling([w1_0_half, w2_0_half])])
    h_b = norm_mod("l0_norm2", X1, nw[0, 1][None], chunk(0, 3), chunk(0, 4), NC)
    (a0, r0), (w_in_o_half, w_out_o) = mm_nn(
        "l0_up", h_b, w1_0, _epi_relu2, [BF16, BF16], tn_cap=1024,
        carry=[ex_ag_chips([b16(od_w_in[0])]), ex_ag_sibling([w_out_o_half])])
    (X2, z0), (w1_1_half, w_in_o) = mm_nn(
        "l0_down", a0, w2_0.reshape(-1, D), _epi_resid_gate(NC, tm0), [F32, F32], extras=(X1, chunk(0, 5)),
        extra_kinds=('mn', 'n'), tn_cap=1024, carry=[ex_ag_chips([b16(mlp_w1[1])]), ex_ag_sibling([w_in_o_half])])
    w_out_o = w_out_o.reshape(-1, D)

    h_c = norm_mod("l1_norm1", X2, nw[1, 0][None], chunk(1, 0), chunk(1, 1), NC)
    (qkv1,), (w1_1,) = mm_nn("l1_qkv", h_c, w_in_o, _epi_store(BF16), [BF16], tn_cap=768,
                             carry=[ex_ag_sibling([w1_1_half])])
    (o1, stats1), (w2_1_half,) = attn_odd_fwd("l1_attn", qkv1, bias8, NC, carry=[ex_ag_chips([b16(mlp_w2[1])])])
    X2l = X2[NC:]
    tm1 = _tile(S, 1100)
    (X3, y1), (w2_1,) = mm_nn("l1_out", o1, w_out_o, _epi_resid_gate(0, tm1), [F32, F32],
                              extras=(X2l, chunk(1, 2)), extra_kinds=('mn', 'n'),
                              carry=[ex_ag_sibling([w2_1_half])])
    h_d = norm_mod("l1_norm2", X3, nw[1, 1][None], chunk(1, 3), chunk(1, 4), 0)
    a1, r1 = mm_nn("l1_up", h_d, w1_1, _epi_relu2, [BF16, BF16], tn_cap=1024)
    X4, z1 = mm_nn("l1_down", a1, w2_1.reshape(-1, D), _epi_resid_gate(0, tm1), [F32, F32], extras=(X3, chunk(1, 5)),
                   extra_kinds=('mn', 'n'), tn_cap=1024)
    dX4, loss_p, dfw_p, dz1, pg2_1 = final_loss("final_loss", X4, final_norm_w[None], loss_target[0], z1, chunk(1, 5))
    w_in = [w_in_e, w_in_o]
    w_out = [w_out_e, w_out_o]
    w1 = [w1_0, w1_1]
    w2 = [w2_0.reshape(-1, D), w2_1.reshape(-1, D)]

    mc4 = (lax.axis_index("c") * 4).astype(jnp.int32)
    my_chip = (2 * lax.axis_index("x") + lax.axis_index("y")).astype(jnp.int32)

    def chip_sum(tag, g8, sib4):
        return rs_chip_sum(f"rs_chip_sum_{tag}", g8, sib4, jnp.stack([mc4, my_chip]))

    du1 = mm_nt("l1_down_dx", dz1, w2[1], _epi_mul2r, BF16, extras=(r1,))
    g_w1_1 = mm_tn("l1_up_dw", h_d, du1, 1)
    g_w2_1, (sib_w1_1,) = mm_tn("l1_down_dw", a1, dz1, 0, carry=[ex_rs_sibling([g_w1_1])])
    dh_d, (rem_w1_1, sib_w2_1) = mm_nt(
        "l1_up_dx", du1, w1[1], _epi_store(F32), F32,
        carry=[ex_rs_chips([chip_sum("w1_1", g_w1_1, sib_w1_1)]), ex_rs_sibling([g_w2_1])])
    dX3, pn2_1, dy1, pg1_1 = norm_bwd("l1_norm2_bwd", X3, dh_d, dX4, nw[1, 1][None], chunk(1, 4), 0,
                                      gate=(y1, chunk(1, 2)))
    do1 = mm_nt("l1_out_dx", dy1, w_out[1], _epi_store(BF16), BF16)
    g_wout_1 = mm_tn("l1_out_dw", o1, dy1, 0)
    (dq1, dk1, dv1, dbias8), (rem_w2_1, sib_wout_1) = attn_odd_bwd(
        "l1_attn_bwd", qkv1, bias8, do1, stats1, NC,
        carry=[ex_rs_chips([chip_sum("w2_1", g_w2_1, sib_w2_1)]), ex_rs_sibling([g_wout_1])])
    dqkv1 = jnp.concatenate([jnp.pad(dq1, ((NC, 0), (0, 0))), dk1, dv1], axis=1).astype(BF16)
    dh_c, (rem_wout_1,) = mm_nt("l1_qkv_dx", dqkv1, w_in[1], _epi_store(F32), F32,
                                carry=[ex_rs_chips([chip_sum("wout_1", g_wout_1, sib_wout_1)])])
    g_win_1 = mm_tn("l1_qkv_dw", h_c, dqkv1, 1)
    dX2, pn1_1, dz0, pg2_0 = norm_bwd("l1_norm1_bwd", X2, dh_c, dX3, nw[1, 0][None], chunk(1, 1), NC, dres_skip=NC,
                                      gate=(z0, chunk(0, 5)))
    d_rpb = na_bias_grad("rpb_grad", na_span_bias_grad(dbias8))

    du0, (sib_win_1,) = mm_nt("l0_down_dx", dz0, w2[0], _epi_mul2r, BF16, extras=(r0,),
                              carry=[ex_rs_sibling([g_win_1])])
    g_w1_0, (rem_win_1,) = mm_tn("l0_up_dw", h_b, du0, 1,
                                 carry=[ex_rs_chips([chip_sum("win_1", g_win_1, sib_win_1)])])
    g_w2_0, (sib_w1_0,) = mm_tn("l0_down_dw", a0, dz0, 0, carry=[ex_rs_sibling([g_w1_0])])
    dh_b, (rem_w1_0, sib_w2_0) = mm_nt(
        "l0_up_dx", du0, w1[0], _epi_store(F32), F32,
        carry=[ex_rs_chips([chip_sum("w1_0", g_w1_0, sib_w1_0)]), ex_rs_sibling([g_w2_0])])
    dX1, pn2_0, dy0, pg1_0 = norm_bwd("l0_norm2_bwd", X1, dh_b, dX2, nw[0, 1][None], chunk(0, 4), NC,
                                      gate=(y0, chunk(0, 2)))
    do0 = mm_nt("l0_out_dx", dy0, w_out[0], _epi_store(BF16), BF16)
    g_wout_0 = mm_tn("l0_out_dw", o0, dy0, 0)
    (dq0, dk0, dv0, dsink_p), (rem_w2_0, sib_wout_0) = attn_even_bwd(
        "l0_attn_bwd", qkvh0, sink, do0, stats0, NC, TQ_B,
        carry=[ex_rs_chips([chip_sum("w2_0", g_w2_0, sib_w2_0)]), ex_rs_sibling([g_wout_0])])
    dqkv0, pqk = prep_even_bwd("l0_prep_bwd", qkv0, dq0, dk0, dv0, ev_q_norm, ev_k_norm, cos, sa, sb)
    g_win_0, (rem_wout_0,) = mm_tn("l0_qkv_dw", h_a, dqkv0, 1,
                                   carry=[ex_rs_chips([chip_sum("wout_0", g_wout_0, sib_wout_0)])])
    dh_a, (sib_win_0,) = mm_nt("l0_qkv_dx", dqkv0, w_in[0], _epi_store(F32), F32, carry=[ex_rs_sibling([g_win_0])])
    (dx_lat, pn1_0), (rem_win_0,) = norm_bwd(
        "l0_norm1_bwd", X0, dh_a, dX1, nw[0, 0][None], chunk(0, 1), NC, out_skip=NC,
        carry=[ex_rs_chips([chip_sum("win_0", g_win_0, sib_win_0)])])
    grad_x = dx_lat[None]

    def dmod(grp, pn1, pg1, pn2, pg2):
        return jnp.concatenate([pn1[grp], pn1[2 + grp], pg1[grp], pn2[grp], pn2[2 + grp], pg2[grp]])

    dmod_lat = jnp.stack([dmod(1, pn1_0, pg1_0, pn2_0, pg2_0), dmod(1, pn1_1, pg1_1, pn2_1, pg2_1)])
    dmod_ctx = jnp.stack([dmod(0, pn1_0, pg1_0, pn2_0, pg2_0), dmod(0, pn1_1, pg1_1, pn2_1, pg2_1)])
    dnw_p = jnp.stack([pn1_0[4], pn2_0[4], pn1_1[4], pn2_1[4]])
    pieces = [dmod_lat, dmod_ctx, dnw_p, pqk[0], pqk[1], dsink_p[8:, 0, 0], d_rpb, dfw_p[0], loss_p[0, 0]]
    sizes = [int(np.prod(p.shape)) for p in pieces]
    rows = [_rows8(s) for s in sizes]
    pack2 = jnp.concatenate([_pad_rows(p, r) for p, r in zip(pieces, rows)], axis=0)
    g2, s2 = ag_small("ag_small_grads", pack2, with_sum=True)
    offs = np.concatenate([[0], np.cumsum(rows)])

    def piece(arr, i, shape):
        return arr[..., offs[i]:offs[i + 1], :].reshape(arr.shape[:-2] + (-1,))[..., :sizes[i]].reshape(
            arr.shape[:-2] + shape)

    dmod_all = piece(g2, 0, (2, 6 * D))
    dmodc_sum = piece(s2, 1, (2, 6 * D))
    dnw_sum = piece(s2, 2, (2, 2, D))
    g_qn = piece(s2, 3, ev_q_norm.shape)
    g_kn = piece(s2, 4, ev_k_norm.shape)
    g_sink = piece(s2, 5, ev_sink.shape)
    g_rpb = piece(s2, 6, od_rpb.shape)
    g_fw = piece(s2, 7, final_norm_w.shape)
    loss = piece(s2, 8, ())

    dm16 = jnp.concatenate([dmod_all.transpose(1, 0, 2), dmodc_sum[:, None, :],
                            jnp.zeros((2, NDEV - 1, 6 * D), F32)], axis=1)
    dm16_loc = lax.dynamic_slice_in_dim(dm16.reshape(2, 2 * NDEV, NDEV, ada_cols), me, 1, axis=2)[:, :, 0, :]
    g_ada_b = _vmem_call("ada_b_grad", lambda v: jnp.sum(v, axis=1),
                         jax.ShapeDtypeStruct((2, 6 * D), F32), dm16)
    g_ada_w = []
    dact_p = None
    for i in range(2):
        dmb = dm16_loc[i].astype(BF16)
        g_ada_w.append(mm_tn(f"ada_w_grad{i}", act, dmb, None))
        part = mm_nt(f"ada_dact{i}", dmb, ada_w[i], _epi_store(F32), F32)
        dact_p = part if dact_p is None else dact_p + part
    _, dact = ag_small("ag_cctx", dact_p, with_sum=True)

    def cctx_grad(da, cc):
        sg = 1.0 / (1.0 + jnp.exp(-cc))
        return da[NDEV:NDEV + 1] * (sg * (1.0 + cc * (1.0 - sg)))

    g_cctx = _vmem_call("cctx_grad", cctx_grad, jax.ShapeDtypeStruct((1, D), F32), dact, c_ctx[None])[0]

    grads = [g_win_0, g_wout_0, g_w1_0, g_w2_0, g_win_1, g_wout_1, g_w1_1, g_w2_1]
    sib = [sib_win_0, sib_wout_0, sib_w1_0, sib_w2_0, sib_win_1, sib_wout_1, sib_w1_1, sib_w2_1]
    rem = [rem_win_0, rem_wout_0, rem_w1_0, rem_w2_0, rem_win_1, rem_wout_1, rem_w1_1, rem_w2_1]
    own_idx = jnp.stack([mc4 + my_chip, my_chip])

    def big(tag, w, m, v, ts):
        res = None
        for l, t in enumerate(ts):
            res = adamw_rs(f"adamw_{tag}_{l}", w, grads[t], sib[t], rem[t], m, v, own_idx, l, res)
        return tuple(res)

    r_ev_w_in = big('ev_w_in', ev_w_in, m_ev_w_in, v_ev_w_in, [0])
    r_ev_w_out = big('ev_w_out', ev_w_out, m_ev_w_out, v_ev_w_out, [1])
    r_mlp_w1 = big('mlp_w1', mlp_w1, m_mlp_w1, v_mlp_w1, [2, 6])
    r_mlp_w2 = big('mlp_w2', mlp_w2, m_mlp_w2, v_mlp_w2, [3, 7])
    r_od_w_in = big('od_w_in', od_w_in, m_od_w_in, v_od_w_in, [4])
    r_od_w_out = big('od_w_out', od_w_out, m_od_w_out, v_od_w_out, [5])

    g_ada = jnp.stack(g_ada_w)
    r_ada_w = adamw_rows("adamw_ada_w", ada_w.reshape(2 * D, ada_cols), g_ada.reshape(2 * D, ada_cols),
                         m_ada_w.reshape(2 * D, ada_cols), v_ada_w.reshape(2 * D, ada_cols))
    r_ada_w = tuple(u.reshape(2, D, ada_cols) for u in r_ada_w)

    g_nw_loc = lax.dynamic_slice_in_dim(dnw_sum, me * nw_cols, nw_cols, axis=2)
    small = [(c_ctx, g_cctx, m_c_ctx, v_c_ctx), (ada_b, g_ada_b, m_ada_b, v_ada_b),
             (norm_w, g_nw_loc, m_norm_w, v_norm_w), (ev_q_norm, g_qn, m_ev_q_norm, v_ev_q_norm),
             (ev_k_norm, g_kn, m_ev_k_norm, v_ev_k_norm), (ev_sink, g_sink, m_ev_sink, v_ev_sink),
             (od_rpb, g_rpb, m_od_rpb, v_od_rpb), (final_norm_w, g_fw, m_final_norm_w, v_final_norm_w)]
    srows = [_rows8(int(np.prod(w.shape))) for w, _, _, _ in small]
    packs = [jnp.concatenate([_pad_rows(tup[k], r) for tup, r in zip(small, srows)], axis=0) for k in range(4)]
    sres = adamw_rows("adamw_small", *packs)
    soffs = np.concatenate([[0], np.cumsum(srows)])

    def unpack(arr, i):
        w = small[i][0]
        return arr[soffs[i]:soffs[i + 1]].reshape(-1)[:int(np.prod(w.shape))].reshape(w.shape)

    sm = [[unpack(sres[k], i) for i in range(len(small))] for k in range(4)]

    def outs(k):
        big_k = {'ada_w': r_ada_w[k], 'mlp_w1': r_mlp_w1[k], 'mlp_w2': r_mlp_w2[k], 'ev_w_in': r_ev_w_in[k],
                 'ev_w_out': r_ev_w_out[k], 'od_w_in': r_od_w_in[k], 'od_w_out': r_od_w_out[k]}
        return (sm[k][0], big_k['ada_w'], sm[k][1], sm[k][2], big_k['mlp_w1'], big_k['mlp_w2'], big_k['ev_w_in'],
                big_k['ev_w_out'], sm[k][3], sm[k][4], sm[k][5], big_k['od_w_in'], big_k['od_w_out'], sm[k][6],
                sm[k][7])

    return (loss, grad_x, *outs(0), *outs(1), *outs(2), *outs(3))
```

```python
import numpy as np
import jax
import jax.numpy as jnp
from jax import lax
from jax.experimental import pallas as pl
from jax.experimental.pallas import tpu as pltpu

F32 = jnp.float32
BF16 = jnp.bfloat16
MESH = pl.DeviceIdType.MESH

NDEV = 8
HEAD = 128
GRID_W = 64
NA_KH, NA_KW = 8, 16
WINDOW = 128
ROPE_THETA = 10000.0
EPS = 1e-6
NEG = -1e30
SCALE = HEAD ** -0.5
ROW_TILE = 256
VMEM_LIMIT = 56 * 1024 * 1024

ADAM_LR, ADAM_B1, ADAM_B2, ADAM_EPS, ADAM_WD, ADAM_STEP = 0.001, 0.9, 0.999, 1e-08, 0.01, 10

NT = (((1,), (1,)), ((), ()))
NN = (((1,), (0,)), ((), ()))
TN = (((0,), (0,)), ((), ()))


def _cparams(sem):
    return pltpu.CompilerParams(dimension_semantics=sem, vmem_limit_bytes=VMEM_LIMIT)


def _tile(n, cap):
    if n <= cap:
        return n
    t = cap - cap % 64
    while t >= 64:
        if n % t == 0:
            return t
        t -= 64
    raise ValueError((n, cap))


def _dot(a, b, dims):
    return lax.dot_general(a.astype(BF16), b.astype(BF16), dims, preferred_element_type=F32)


def _slot(d):
    return (d % 2) * 4 + d // 2


class Exchange:
    def __init__(self, ins, out_shapes, aliases, n_sems, start, finish):
        self.ins, self.out_shapes, self.aliases, self.n_sems = list(ins), list(out_shapes), dict(aliases), n_sems
        self.start, self.finish = start, finish


def merge_exchanges(xs):
    ins, outs, aliases, bases, n = [], [], {}, [], 0
    for x in xs:
        bases.append((len(ins), len(outs), n))
        aliases.update({len(ins) + i: len(outs) + o for i, o in x.aliases.items()})
        ins += x.ins
        outs += x.out_shapes
        n += x.n_sems

    def run(which):
        def f(ci, co, ss, rs, base):
            for x, (i0, o0, s0) in zip(xs, bases):
                getattr(x, which)(ci[i0:i0 + len(x.ins)], co[o0:o0 + len(x.out_shapes)], ss, rs, base + s0)
        return f

    return Exchange(ins, outs, aliases, n, run('start'), run('finish'))


def _call(name, body, grid, ins, in_specs, out_shape, out_specs, scratch, sems, carry=None):
    if not carry:
        return pl.pallas_call(body, grid=grid, in_specs=in_specs, out_specs=out_specs, out_shape=out_shape,
                              scratch_shapes=scratch, compiler_params=_cparams(sems), name=name)(*ins)
    x = merge_exchanges(carry)
    n_in, n_ci, n_out, n_co, n_sc = len(ins), len(x.ins), len(out_shape), len(x.out_shapes), len(scratch)

    def wrapped(*refs):
        p = [0]

        def take(k):
            p[0] += k
            return refs[p[0] - k:p[0]]

        a, ci, o, co, sc = take(n_in), take(n_ci), take(n_out), take(n_co), take(n_sc)
        ss, rs = take(2)
        first = pl.program_id(0) == 0
        last = pl.program_id(0) == grid[0] - 1
        for d in range(1, len(grid)):
            first = jnp.logical_and(first, pl.program_id(d) == 0)
            last = jnp.logical_and(last, pl.program_id(d) == grid[d] - 1)

        @pl.when(first)
        def _():
            x.start(ci, co, ss, rs, 0)

        body(*a, *o, *sc)

        @pl.when(last)
        def _():
            x.finish(ci, co, ss, rs, 0)

    hbm = pl.BlockSpec(memory_space=pl.ANY)
    res = pl.pallas_call(
        wrapped, grid=grid, in_specs=list(in_specs) + [hbm] * n_ci, out_specs=list(out_specs) + [hbm] * n_co,
        out_shape=list(out_shape) + x.out_shapes,
        input_output_aliases={n_in + i: n_out + o for i, o in x.aliases.items()},
        scratch_shapes=list(scratch) + [pltpu.SemaphoreType.DMA((x.n_sems,)), pltpu.SemaphoreType.DMA((x.n_sems,))],
        compiler_params=_cparams(("arbitrary",) * len(grid)), name=name)(*ins, *x.ins)
    return list(res[:n_out]) + [list(res[n_out:])]


def _mm_core(name, grid, ins, in_specs, out_shape, out_specs, dims, acc_shape, epi, carry=None):
    nk = grid[2]
    n_extra = len(ins) - 2

    def body_single(*refs):
        epi(_dot(refs[0][...], refs[1][...], dims), refs[2:2 + n_extra], refs[2 + n_extra:])

    def body(*refs):
        a_ref, b_ref = refs[0], refs[1]
        ex = refs[2:2 + n_extra]
        outs = refs[2 + n_extra:-1]
        acc = refs[-1]
        k = pl.program_id(2)

        @pl.when(k == 0)
        def _():
            acc[...] = _dot(a_ref[...], b_ref[...], dims)

        @pl.when(jnp.logical_and(k > 0, k < nk - 1))
        def _():
            acc[...] += _dot(a_ref[...], b_ref[...], dims)

        @pl.when(k == nk - 1)
        def _():
            epi(acc[...] + _dot(a_ref[...], b_ref[...], dims), ex, outs)

    if nk == 1:
        return _call(name, body_single, grid, ins, in_specs, out_shape, out_specs, [],
                     ("parallel", "parallel", "arbitrary"), carry)
    return _call(name, body, grid, ins, in_specs, out_shape, out_specs, [pltpu.VMEM(acc_shape, F32)],
                 ("parallel", "parallel", "arbitrary"), carry)


def _split(res, n, carry):
    own = res[0] if n == 1 else list(res[:n])
    return (own, res[n]) if carry else own


def _epi_store(dtype):
    def epi(acc, ex, outs):
        outs[0][...] = acc.astype(dtype)
    return epi


def _epi_bias(acc, ex, outs):
    outs[0][...] = acc + ex[0][...]


def _epi_relu2(acc, ex, outs):
    r = jnp.maximum(acc, 0.0)
    outs[0][...] = (r * r).astype(BF16)
    outs[1][...] = r.astype(BF16)


def _epi_mul2r(acc, ex, outs):
    outs[0][...] = (acc * (2.0 * ex[0][...].astype(F32))).astype(BF16)


def _epi_resid_gate(nctx, tm):
    def epi(acc, ex, outs):
        rows = pl.program_id(0) * tm + lax.broadcasted_iota(jnp.int32, (tm, 1), 0)
        g = jnp.where(rows < nctx, ex[1][0:1, :], ex[1][1:2, :])
        outs[0][...] = ex[0][...] + g * acc
        outs[1][...] = acc
    return epi


def mm_nn(name, a, w, epi, outs, extras=(), extra_kinds=(), tm_cap=1100, tn_cap=512, tk_cap=2048, carry=None):
    M, K = a.shape
    if w.ndim == 3:
        ns = w.shape[2]
        N = NDEV * ns
        tn = _tile(ns, tn_cap)
        nper = ns // tn
    else:
        N = w.shape[1]
        tn = _tile(N, tn_cap)
    tm = _tile(M, tm_cap)
    tk = _tile(K, tk_cap)
    grid = (M // tm, N // tn, K // tk)
    a_spec = pl.BlockSpec((tm, tk), lambda i, j, k: (i, k))
    if w.ndim == 3:
        b_spec = pl.BlockSpec((None, tk, tn), lambda i, j, k: (j // nper, k, j % nper))
    else:
        b_spec = pl.BlockSpec((tk, tn), lambda i, j, k: (k, j))
    ex_specs = []
    for e, kind in zip(extras, extra_kinds):
        if kind == 'mn':
            ex_specs.append(pl.BlockSpec((tm, tn), lambda i, j, k: (i, j)))
        else:
            ex_specs.append(pl.BlockSpec((e.shape[0], tn), lambda i, j, k: (0, j)))
    out_shape = [jax.ShapeDtypeStruct((M, N), dt) for dt in outs]
    out_specs = [pl.BlockSpec((tm, tn), lambda i, j, k: (i, j)) for _ in outs]
    res = _mm_core(name, grid, (a, w, *extras), [a_spec, b_spec, *ex_specs], out_shape, out_specs, NN, (tm, tn), epi,
                   carry)
    return (list(res[:len(outs)]), res[len(outs)]) if carry else res


def mm_nt(name, a, w, epi, out_dtype, extras=(), tm_cap=1100, to_cap=1024, tc_cap=2048, carry=None):
    M, N = a.shape
    tm = _tile(M, tm_cap)
    if w.ndim == 3:
        Kw, ns = w.shape[1], w.shape[2]
        tc = _tile(ns, tc_cap)
        cper = ns // tc
    else:
        Kw = w.shape[0]
        tc = _tile(N, tc_cap)
    to = _tile(Kw, to_cap)
    grid = (M // tm, Kw // to, N // tc)
    a_spec = pl.BlockSpec((tm, tc), lambda i, j, k: (i, k))
    if w.ndim == 3:
        b_spec = pl.BlockSpec((None, to, tc), lambda i, j, k: (k // cper, j, k % cper))
    else:
        b_spec = pl.BlockSpec((to, tc), lambda i, j, k: (j, k))
    ex_specs = [pl.BlockSpec((tm, to), lambda i, j, k: (i, j)) for _ in extras]
    out_shape = [jax.ShapeDtypeStruct((M, Kw), out_dtype)]
    out_specs = [pl.BlockSpec((tm, to), lambda i, j, k: (i, j))]
    return _split(_mm_core(name, grid, (a, w, *extras), [a_spec, b_spec, *ex_specs], out_shape, out_specs, NT, (tm, to),
                           epi, carry), 1, carry)


def mm_tn(name, a, b, shard_axis, to_cap=1024, tn_cap=1024, tc_cap=2200, carry=None):
    M, Ka = a.shape
    N = b.shape[1]
    tc = _tile(M, tc_cap)
    if shard_axis is None:
        to, tn = _tile(Ka, to_cap), _tile(N, tn_cap)
        shape = (Ka, N)
        oblk = (to, tn)
        omap = lambda i, j, k: (i, j)
    elif shard_axis == 1:
        ns = N // NDEV
        to, tn = _tile(Ka, to_cap), _tile(ns, tn_cap)
        per = ns // tn
        shape = (NDEV, Ka, ns)
        oblk = (None, to, tn)
        omap = lambda i, j, k: (_slot(j // per), i, j % per)
    else:
        rs = Ka // NDEV
        to, tn = _tile(rs, to_cap), _tile(N, tn_cap)
        per = rs // to
        shape = (NDEV, rs, N)
        oblk = (None, to, tn)
        omap = lambda i, j, k: (_slot(i // per), i % per, j)
    grid = (Ka // to, N // tn, M // tc)
    a_spec = pl.BlockSpec((tc, to), lambda i, j, k: (k, i))
    b_spec = pl.BlockSpec((tc, tn), lambda i, j, k: (k, j))
    out_shape = [jax.ShapeDtypeStruct(shape, F32)]
    out_specs = [pl.BlockSpec(oblk, omap)]
    return _split(_mm_core(name, grid, (a, b), [a_spec, b_spec], out_shape, out_specs, TN, (to, tn), _epi_store(F32),
                           carry), 1, carry)


def _row_spec(D):
    return pl.BlockSpec((ROW_TILE, D), lambda i: (i, 0))


def _const_spec(r, D):
    return pl.BlockSpec((r, D), lambda i: (0, 0))


def _grp(ref, is_ctx):
    return jnp.where(is_ctx, ref[0:1, :], ref[1:2, :])


def norm_mod(name, x, nw, sh, sc, nctx):
    R, D = x.shape
    assert R % ROW_TILE == 0 and nctx % ROW_TILE == 0

    def body(x_ref, nw_ref, sh_ref, sc_ref, o_ref):
        is_ctx = pl.program_id(0) * ROW_TILE < nctx
        xv = x_ref[...]
        rstd = lax.rsqrt(jnp.mean(xv * xv, axis=-1, keepdims=True) + EPS)
        n = xv * rstd * nw_ref[...]
        o_ref[...] = (n * (1.0 + _grp(sc_ref, is_ctx)) + _grp(sh_ref, is_ctx)).astype(BF16)

    return pl.pallas_call(
        body, grid=(R // ROW_TILE,),
        in_specs=[_row_spec(D), _const_spec(1, D), _const_spec(2, D), _const_spec(2, D)],
        out_specs=_row_spec(D), out_shape=jax.ShapeDtypeStruct((R, D), BF16),
        compiler_params=_cparams(("parallel",)), name=name)(x, nw, sh, sc)


def _gate_rows(dxv, y_ref, g_ref, is_ctx, dy_ref, gpart_ref):
    dy_ref[...] = (dxv * _grp(g_ref, is_ctx)).astype(BF16)
    s = jnp.sum(dxv * y_ref[...], axis=0, keepdims=True)
    zero = jnp.zeros_like(s)
    gpart_ref[0:1, :] += jnp.where(is_ctx, s, zero)
    gpart_ref[1:2, :] += jnp.where(is_ctx, zero, s)


def norm_bwd(name, x, dh, dres, nw, sc, nctx, dres_skip=0, out_skip=0, carry=None, gate=None):
    R, D = x.shape
    assert R % ROW_TILE == 0 and nctx % ROW_TILE == 0 and dres_skip % ROW_TILE == 0 and out_skip % ROW_TILE == 0
    res_tiles, out_tiles = dres_skip // ROW_TILE, out_skip // ROW_TILE

    def body(x_ref, dh_ref, dres_ref, nw_ref, sc_ref, *rest):
        if gate is None:
            dx_ref, part_ref = rest
        else:
            y_ref, g_ref, dx_ref, part_ref, dy_ref, gpart_ref = rest
        i = pl.program_id(0)
        is_ctx = i * ROW_TILE < nctx

        @pl.when(i == 0)
        def _():
            part_ref[...] = jnp.zeros_like(part_ref)
            if gate is not None:
                gpart_ref[...] = jnp.zeros_like(gpart_ref)

        xv = x_ref[...]
        dhv = dh_ref[...]
        w = nw_ref[...]
        rstd = lax.rsqrt(jnp.mean(xv * xv, axis=-1, keepdims=True) + EPS)
        xhat = xv * rstd
        n = xhat * w
        dn = dhv * (1.0 + _grp(sc_ref, is_ctx))
        dxhat = dn * w
        dres = dres_ref[...]
        if res_tiles:
            dres = jnp.where(i < res_tiles, 0.0, dres)
        dxv = dres + rstd * (dxhat - xhat * jnp.mean(dxhat * xhat, axis=-1, keepdims=True))
        dx_ref[...] = dxv
        s_sh = jnp.sum(dhv, axis=0, keepdims=True)
        s_sc = jnp.sum(dhv * n, axis=0, keepdims=True)
        s_nw = jnp.sum(dn * xhat, axis=0, keepdims=True)
        zero = jnp.zeros_like(s_sh)
        part_ref[0:1, :] += jnp.where(is_ctx, s_sh, zero)
        part_ref[1:2, :] += jnp.where(is_ctx, zero, s_sh)
        part_ref[2:3, :] += jnp.where(is_ctx, s_sc, zero)
        part_ref[3:4, :] += jnp.where(is_ctx, zero, s_sc)
        part_ref[4:5, :] += s_nw
        if gate is not None:
            _gate_rows(dxv, y_ref, g_ref, is_ctx, dy_ref, gpart_ref)

    ins = [x, dh, dres, nw, sc]
    in_specs = [_row_spec(D), _row_spec(D), pl.BlockSpec((ROW_TILE, D), lambda i: (jnp.maximum(i - res_tiles, 0), 0)),
                _const_spec(1, D), _const_spec(2, D)]
    out_shape = [jax.ShapeDtypeStruct((R - out_skip, D), F32), jax.ShapeDtypeStruct((8, D), F32)]
    out_specs = [pl.BlockSpec((ROW_TILE, D), lambda i: (jnp.maximum(i - out_tiles, 0), 0)), _const_spec(8, D)]
    if gate is not None:
        assert out_skip == 0
        ins += list(gate)
        in_specs += [_row_spec(D), _const_spec(2, D)]
        out_shape += [jax.ShapeDtypeStruct((R, D), BF16), jax.ShapeDtypeStruct((8, D), F32)]
        out_specs += [_row_spec(D), _const_spec(8, D)]
    res = _call(name, body, (R // ROW_TILE,), ins, in_specs, out_shape, out_specs, [], ("arbitrary",), carry)
    return _split(res, len(out_shape), carry)


def final_loss(name, x, fw, tgt, y, g):
    S, D = x.shape

    def body(x_ref, fw_ref, t_ref, y_ref, g_ref, dx_ref, loss_ref, dfw_ref, dy_ref, gpart_ref):
        i = pl.program_id(0)

        @pl.when(i == 0)
        def _():
            loss_ref[...] = jnp.zeros_like(loss_ref)
            dfw_ref[...] = jnp.zeros_like(dfw_ref)
            gpart_ref[...] = jnp.zeros_like(gpart_ref)

        xv = x_ref[...]
        w = fw_ref[...]
        rstd = lax.rsqrt(jnp.mean(xv * xv, axis=-1, keepdims=True) + EPS)
        xhat = xv * rstd
        e = xhat * w - t_ref[...]
        loss_ref[...] += 0.5 * jnp.sum(jnp.mean(e * e, axis=-1, keepdims=True))
        dout = e * (1.0 / D)
        dfw_ref[0:1, :] += jnp.sum(dout * xhat, axis=0, keepdims=True)
        dxhat = dout * w
        dxv = rstd * (dxhat - xhat * jnp.mean(dxhat * xhat, axis=-1, keepdims=True))
        dx_ref[...] = dxv
        _gate_rows(dxv, y_ref, g_ref, False, dy_ref, gpart_ref)

    return pl.pallas_call(
        body, grid=(S // ROW_TILE,),
        in_specs=[_row_spec(D), _const_spec(1, D), _row_spec(D), _row_spec(D), _const_spec(2, D)],
        out_specs=[_row_spec(D), pl.BlockSpec((8, 128), lambda i: (0, 0)), _const_spec(8, D), _row_spec(D),
                   _const_spec(8, D)],
        out_shape=[jax.ShapeDtypeStruct((S, D), F32), jax.ShapeDtypeStruct((8, 128), F32),
                   jax.ShapeDtypeStruct((8, D), F32), jax.ShapeDtypeStruct((S, D), BF16),
                   jax.ShapeDtypeStruct((8, D), F32)],
        compiler_params=_cparams(("arbitrary",)), name=name)(x, fw, tgt, y, g)


def _rope(x, cos, sa, sb):
    return x * cos + pltpu.roll(x, 96, 1) * sa + pltpu.roll(x, 32, 1) * sb


def _rope_t(dy, cos, sa, sb):
    return dy * cos + pltpu.roll(dy * sa, 32, 1) + pltpu.roll(dy * sb, 96, 1)


_EVEN_KINDS = ['qa'] * 8 + ['ka'] * 2 + ['v'] * 2 + ['qb'] * 8 + ['kb'] * 2 + ['v'] * 2
_EVEN_DSRC = ([('q', j) for j in range(8)] + [('k', 0), ('k', 1), ('v', 0), ('v', 1)]
              + [('q', 8 + j) for j in range(8)] + [('k', 2), ('k', 3), ('v', 2), ('v', 3)])


def _cols(j):
    return slice(j * HEAD, (j + 1) * HEAD)


def prep_even(name, qkv, qn, kn, cos, sa, sb):
    T, W = qkv.shape

    def body(x_ref, qn_ref, kn_ref, cos_ref, sa_ref, sb_ref, o_ref):
        cos_, sa_, sb_ = cos_ref[...], sa_ref[...], sb_ref[...]
        for j, kind in enumerate(_EVEN_KINDS):
            x = x_ref[:, _cols(j)]
            if kind in ('qa', 'ka'):
                rstd = lax.rsqrt(jnp.mean(x * x, axis=-1, keepdims=True) + EPS)
                x = x * rstd * (qn_ref[...] if kind == 'qa' else kn_ref[...])
            if kind != 'v':
                x = _rope(x, cos_, sa_, sb_)
            o_ref[:, _cols(j)] = x.astype(BF16)

    blk = pl.BlockSpec((ROW_TILE, W), lambda i: (i, 0))
    tab = pl.BlockSpec((ROW_TILE, HEAD), lambda i: (i, 0))
    one = pl.BlockSpec((1, HEAD), lambda i: (0, 0))
    return pl.pallas_call(
        body, grid=(T // ROW_TILE,), in_specs=[blk, one, one, tab, tab, tab], out_specs=blk,
        out_shape=jax.ShapeDtypeStruct(qkv.shape, BF16),
        compiler_params=_cparams(("parallel",)), name=name)(qkv, qn, kn, cos, sa, sb)


def prep_even_bwd(name, qkv, dq, dk, dv, qn, kn, cos, sa, sb):
    T, W = qkv.shape

    def body(x_ref, dq_ref, dk_ref, dv_ref, qn_ref, kn_ref, cos_ref, sa_ref, sb_ref, o_ref, part_ref):
        @pl.when(pl.program_id(0) == 0)
        def _():
            part_ref[...] = jnp.zeros_like(part_ref)

        cos_, sa_, sb_ = cos_ref[...], sa_ref[...], sb_ref[...]
        src = {'q': dq_ref, 'k': dk_ref, 'v': dv_ref}
        sums = {'qa': None, 'ka': None}
        for j, kind in enumerate(_EVEN_KINDS):
            which, blk_j = _EVEN_DSRC[j]
            d = src[which][:, _cols(blk_j)]
            if kind != 'v':
                d = _rope_t(d, cos_, sa_, sb_)
            if kind in ('qa', 'ka'):
                x = x_ref[:, _cols(j)]
                rstd = lax.rsqrt(jnp.mean(x * x, axis=-1, keepdims=True) + EPS)
                xhat = x * rstd
                s = jnp.sum(d * xhat, axis=0, keepdims=True)
                sums[kind] = s if sums[kind] is None else sums[kind] + s
                dxhat = d * (qn_ref[...] if kind == 'qa' else kn_ref[...])
                d = rstd * (dxhat - xhat * jnp.mean(dxhat * xhat, axis=-1, keepdims=True))
            o_ref[:, _cols(j)] = d.astype(BF16)
        part_ref[0:1, :] += sums['qa']
        part_ref[1:2, :] += sums['ka']

    def rows(w):
        return pl.BlockSpec((ROW_TILE, w), lambda i: (i, 0))

    one = pl.BlockSpec((1, HEAD), lambda i: (0, 0))
    return pl.pallas_call(
        body, grid=(T // ROW_TILE,),
        in_specs=[rows(W), rows(dq.shape[1]), rows(dk.shape[1]), rows(dv.shape[1]), one, one,
                  rows(HEAD), rows(HEAD), rows(HEAD)],
        out_specs=[rows(W), pl.BlockSpec((8, HEAD), lambda i: (0, 0))],
        out_shape=[jax.ShapeDtypeStruct(qkv.shape, BF16), jax.ShapeDtypeStruct((8, HEAD), F32)],
        compiler_params=_cparams(("arbitrary",)), name=name)(qkv, dq, dk, dv, qn, kn, cos, sa, sb)


def _even_maps():
    qmap = lambda h, qb: (qb, jnp.where(h < 8, h, h + 4))
    kmap = lambda h, qb: (0, jnp.where(h < 8, 8 + h // 4, 18 + h // 4))
    vmap = lambda h, qb: (0, jnp.where(h < 8, 10 + h // 4, 20 + h // 4))
    return qmap, kmap, vmap


def _softmax_parts(parts, extra=None, stats=None, normalize=True):
    if stats is None:
        m = parts[0].max(axis=-1, keepdims=True)
        for p in parts[1:]:
            m = jnp.maximum(m, p.max(axis=-1, keepdims=True))
        if extra is not None:
            m = jnp.maximum(m, extra)
    else:
        m = stats[0]
    es = [jnp.exp(p - m) for p in parts]
    ex = None if extra is None else jnp.exp(extra - m)
    if stats is None:
        l = es[0].sum(axis=-1, keepdims=True)
        for e in es[1:]:
            l = l + e.sum(axis=-1, keepdims=True)
        if extra is not None:
            l = l + ex
        inv = 1.0 / l
    else:
        inv = stats[1]
    if not normalize:
        return es, ex, (m, inv)
    return [e * inv for e in es], (None if ex is None else ex * inv), (m, inv)


def _win_scores(q, k_ref, qb, tq, nctx, S):
    L = tq + 2 * WINDOW
    nqc = nctx // tq
    qlat = (qb - nqc) * tq
    start = pl.multiple_of(jnp.clip(qlat - WINDOW, 0, S - L), 128)
    kc = k_ref[0:nctx, :]
    kw = k_ref[pl.ds(nctx + start, L), :]
    s_c = _dot(q, kc, NT) * SCALE
    s_w = _dot(q, kw, NT) * SCALE
    qpos = qlat + lax.broadcasted_iota(jnp.int32, (tq, 1), 0)
    kpos = start + lax.broadcasted_iota(jnp.int32, (1, L), 1)
    valid = jnp.logical_and(jnp.abs(kpos - qpos) <= WINDOW, qb >= nqc)
    return s_c, jnp.where(valid, s_w, NEG), start, L


def _softmax_raw(raw, stats=None, normalize=True):
    m = raw.max(axis=-1, keepdims=True) if stats is None else stats[0]
    e = jnp.exp2((raw - m) * (SCALE * np.log2(np.e)))
    inv = 1.0 / e.sum(axis=-1, keepdims=True) if stats is None else stats[1]
    return (e * inv if normalize else e), (m, inv)


def _glob_keys(qb, tq, nctx, T):
    is_ctx = qb < nctx // tq
    return [(is_ctx, slice(0, nctx)), (jnp.logical_not(is_ctx), slice(0, T))]


def _stats_spec(tq):
    return pl.BlockSpec((None, None, tq, 2), lambda h, qb: (h, qb, 0, 0))


def attn_even_fwd(name, qkvh, sink, nctx, tq, carry=None):
    T = qkvh.shape[0]
    S = T - nctx
    qmap, kmap, vmap = _even_maps()

    def body(sink_ref, q_ref, k_ref, v_ref, o_ref, st_ref):
        h, qb = pl.program_id(0), pl.program_id(1)
        q = q_ref[...]

        for pred, keys in _glob_keys(qb, tq, nctx, T):
            @pl.when(jnp.logical_and(h < 8, pred))
            def _():
                e, (m, inv) = _softmax_raw(_dot(q, k_ref[keys, :], NT), normalize=False)
                st_ref[:, 0:1] = m
                st_ref[:, 1:2] = inv
                o_ref[...] = (_dot(e, v_ref[keys, :], NN) * inv).astype(BF16)

        @pl.when(h >= 8)
        def _():
            s_c, s_w, start, L = _win_scores(q, k_ref, qb, tq, nctx, S)
            sk = jnp.full((tq, 1), sink_ref[jnp.maximum(h - 8, 0)], F32)
            (e_c, e_w), _, (m, inv) = _softmax_parts([s_c, s_w], sk, normalize=False)
            st_ref[:, 0:1] = m
            st_ref[:, 1:2] = inv
            o = _dot(e_c, v_ref[0:nctx, :], NN) + _dot(e_w, v_ref[pl.ds(nctx + start, L), :], NN)
            o_ref[...] = (o * inv).astype(BF16)

    res = _call(name, body, (16, T // tq), (sink, qkvh, qkvh, qkvh),
                [pl.BlockSpec(memory_space=pltpu.SMEM), pl.BlockSpec((tq, HEAD), qmap),
                 pl.BlockSpec((T, HEAD), kmap), pl.BlockSpec((T, HEAD), vmap)],
                [jax.ShapeDtypeStruct((T, 16 * HEAD), BF16), jax.ShapeDtypeStruct((16, T // tq, tq, 2), F32)],
                [pl.BlockSpec((tq, HEAD), lambda h, qb: (qb, h)), _stats_spec(tq)],
                [], ("parallel", "arbitrary"), carry)
    return _split(res, 2, carry)


GLOB_KEY_CHUNKS = 4


def attn_even_bwd(name, qkvh, sink, do, o, stats, nctx, tq, carry=None):
    T = qkvh.shape[0]
    assert stats.shape == (16, T // tq, tq, 2)
    S = T - nctx
    qmap, kmap, vmap = _even_maps()

    def body(sink_ref, q_ref, k_ref, v_ref, do_ref, o_ref, st_ref, dq_ref, dk_ref, dv_ref, ds_ref):
        h, qb = pl.program_id(0), pl.program_id(1)
        q = q_ref[...]
        dov = do_ref[...]

        @pl.when(jnp.logical_and(h % 4 == 0, qb == 0))
        def _():
            dk_ref[...] = jnp.zeros_like(dk_ref)
            dv_ref[...] = jnp.zeros_like(dv_ref)

        @pl.when(qb == 0)
        def _():
            ds_ref[...] = jnp.zeros_like(ds_ref)

        for pred, keys in _glob_keys(qb, tq, nctx, T):
            @pl.when(jnp.logical_and(h < 8, pred))
            def _():
                stats = (st_ref[:, 0:1], st_ref[:, 1:2])
                row = jnp.sum(dov.astype(F32) * o_ref[...].astype(F32), axis=-1, keepdims=True)
                n_keys = keys.stop - keys.start
                n_chunks = GLOB_KEY_CHUNKS if n_keys % (16 * GLOB_KEY_CHUNKS) == 0 and n_keys > nctx else 1
                ck = n_keys // n_chunks
                dq = None
                for c in range(n_chunks):
                    ks = slice(keys.start + c * ck, keys.start + (c + 1) * ck)
                    p, _ = _softmax_raw(_dot(q, k_ref[ks, :], NT), stats)
                    dp = _dot(dov, v_ref[ks, :], NT)
                    dsb = (p * (dp - row) * SCALE).astype(BF16)
                    part = _dot(dsb, k_ref[ks, :], NN)
                    dq = part if dq is None else dq + part
                    dk_ref[ks, :] += _dot(dsb, q, TN)
                    dv_ref[ks, :] += _dot(p, dov, TN)
                dq_ref[...] = dq

        @pl.when(h >= 8)
        def _():
            s_c, s_w, start, L = _win_scores(q, k_ref, qb, tq, nctx, S)
            sk = jnp.full((tq, 1), sink_ref[jnp.maximum(h - 8, 0)], F32)
            (p_c, p_w), p_s, _ = _softmax_parts([s_c, s_w], sk, (st_ref[:, 0:1], st_ref[:, 1:2]))
            win = pl.ds(nctx + start, L)
            dp_c = _dot(dov, v_ref[0:nctx, :], NT)
            dp_w = _dot(dov, v_ref[win, :], NT)
            row = jnp.sum(p_c * dp_c, axis=-1, keepdims=True) + jnp.sum(p_w * dp_w, axis=-1, keepdims=True)
            ds_c = (p_c * (dp_c - row) * SCALE).astype(BF16)
            ds_w = (p_w * (dp_w - row) * SCALE).astype(BF16)
            dq_ref[...] = _dot(ds_c, k_ref[0:nctx, :], NN) + _dot(ds_w, k_ref[win, :], NN)
            dk_ref[0:nctx, :] += _dot(ds_c, q, TN)
            dk_ref[win, :] += _dot(ds_w, q, TN)
            dv_ref[0:nctx, :] += _dot(p_c, dov, TN)
            dv_ref[win, :] += _dot(p_w, dov, TN)
            ds_ref[...] += jnp.sum(-(p_s * row))

    kv_out = pl.BlockSpec((T, HEAD), lambda h, qb: (0, h // 4))
    head_blk = pl.BlockSpec((tq, HEAD), lambda h, qb: (qb, h))
    res = _call(name, body, (16, T // tq), (sink, qkvh, qkvh, qkvh, do, o, stats),
                [pl.BlockSpec(memory_space=pltpu.SMEM), pl.BlockSpec((tq, HEAD), qmap),
                 pl.BlockSpec((T, HEAD), kmap), pl.BlockSpec((T, HEAD), vmap), head_blk, head_blk, _stats_spec(tq)],
                [jax.ShapeDtypeStruct((T, 16 * HEAD), F32), jax.ShapeDtypeStruct((T, 4 * HEAD), F32),
                 jax.ShapeDtypeStruct((T, 4 * HEAD), F32), jax.ShapeDtypeStruct((16, 8, 128), F32)],
                [pl.BlockSpec((tq, HEAD), lambda h, qb: (qb, h)), kv_out, kv_out,
                 pl.BlockSpec((None, 8, 128), lambda h, qb: (h, 0, 0))],
                [], ("arbitrary", "arbitrary"), carry)
    return _split(res, 4, carry)


NA_GROUP = 4
NA_SPAN = NA_KH + NA_GROUP - 1
_NA_PLAN = [[(j, 0) for j in range(NA_GROUP)],
            [(NA_KH // 2, j) for j in range(NA_GROUP)],
            [(NA_KH // 2 + j, NA_GROUP - 1) for j in range(NA_GROUP)]]


def _na_group(g, n_groups, rows):
    last = g == n_groups - 1
    kind = jnp.where(g == 0, 0, jnp.where(last, 2, 1))
    first_row = jnp.where(g == 0, 0, jnp.where(last, rows - NA_SPAN, NA_GROUP * g - NA_KH // 2))
    return kind, first_row


def na_span_bias(bias8):
    LW, LS = NA_KH * GRID_W, NA_SPAN * GRID_W
    kinds = []
    for plan in _NA_PLAN:
        strips = [jnp.pad(bias8[:, off], ((0, 0), (0, 0), (s * GRID_W, LS - LW - s * GRID_W)), constant_values=NEG)
                  for off, s in plan]
        kinds.append(jnp.concatenate(strips, axis=1))
    return jnp.stack(kinds, axis=1)


def na_span_bias_grad(db):
    LW = NA_KH * GRID_W
    out = [None] * NA_KH
    for kind, plan in enumerate(_NA_PLAN):
        for j, (off, s) in enumerate(plan):
            piece = db[:, kind, j * GRID_W:(j + 1) * GRID_W, s * GRID_W:s * GRID_W + LW]
            out[off] = piece if out[off] is None else out[off] + piece
    return jnp.stack(out, axis=1)


def _na_specs(T, nctx, n_groups, rows):
    LS = NA_SPAN * GRID_W
    tq = NA_GROUP * GRID_W
    assert nctx % tq == 0 and n_groups >= 3
    q_spec = pl.BlockSpec((tq, HEAD), lambda h, g: (g + nctx // tq, h))
    k_spec = pl.BlockSpec((T, HEAD), lambda h, g: (0, 16 + h))
    v_spec = pl.BlockSpec((T, HEAD), lambda h, g: (0, 32 + h))
    b_spec = pl.BlockSpec((None, None, tq, LS), lambda h, g: (h, _na_group(g, n_groups, rows)[0], 0, 0))
    row_spec = pl.BlockSpec((tq, HEAD), lambda h, g: (g, h))
    return q_spec, k_spec, v_spec, b_spec, row_spec


def _na_scores(q, k_ref, b_ref, g, n_groups, rows, nctx):
    first_row = _na_group(g, n_groups, rows)[1]
    win = pl.ds(pl.multiple_of(nctx + first_row * GRID_W, GRID_W), NA_SPAN * GRID_W)
    s_c = _dot(q, k_ref[0:nctx, :], NT) * SCALE
    s_w = _dot(q, k_ref[win, :], NT) * SCALE + b_ref[...]
    return s_c, s_w, win


def attn_odd_fwd(name, qkv, bias_s, nctx, carry=None):
    T = qkv.shape[0]
    S = T - nctx
    rows = S // GRID_W
    n_groups = rows // NA_GROUP
    q_spec, k_spec, v_spec, b_spec, row_spec = _na_specs(T, nctx, n_groups, rows)

    def body(q_ref, k_ref, v_ref, b_ref, o_ref, st_ref):
        s_c, s_w, win = _na_scores(q_ref[...], k_ref, b_ref, pl.program_id(1), n_groups, rows, nctx)
        (e_c, e_w), _, (m, inv) = _softmax_parts([s_c, s_w], normalize=False)
        st_ref[:, 0:1] = m
        st_ref[:, 1:2] = inv
        o_ref[...] = ((_dot(e_c, v_ref[0:nctx, :], NN) + _dot(e_w, v_ref[win, :], NN)) * inv).astype(BF16)

    tq = NA_GROUP * GRID_W
    res = _call(name, body, (16, n_groups), (qkv, qkv, qkv, bias_s), [q_spec, k_spec, v_spec, b_spec],
                [jax.ShapeDtypeStruct((S, 16 * HEAD), BF16), jax.ShapeDtypeStruct((16, n_groups, tq, 2), F32)],
                [row_spec, _stats_spec(tq)], [], ("parallel", "arbitrary"), carry)
    return _split(res, 2, carry)


def attn_odd_bwd(name, qkv, bias_s, do, stats, nctx, carry=None):
    T = qkv.shape[0]
    S = T - nctx
    rows = S // GRID_W
    n_groups = rows // NA_GROUP
    q_spec, k_spec, v_spec, b_spec, row_spec = _na_specs(T, nctx, n_groups, rows)

    def body(q_ref, k_ref, v_ref, b_ref, do_ref, st_ref, dq_ref, dk_ref, dv_ref, db_ref):
        g = pl.program_id(1)
        q = q_ref[...]
        dov = do_ref[...]

        @pl.when(g == 0)
        def _():
            dk_ref[...] = jnp.zeros_like(dk_ref)
            dv_ref[...] = jnp.zeros_like(dv_ref)

        s_c, s_w, win = _na_scores(q, k_ref, b_ref, g, n_groups, rows, nctx)
        (p_c, p_w), _, _ = _softmax_parts([s_c, s_w], None, (st_ref[:, 0:1], st_ref[:, 1:2]))
        dp_c = _dot(dov, v_ref[0:nctx, :], NT)
        dp_w = _dot(dov, v_ref[win, :], NT)
        row = jnp.sum(p_c * dp_c, axis=-1, keepdims=True) + jnp.sum(p_w * dp_w, axis=-1, keepdims=True)
        dsw = p_w * (dp_w - row)
        first_visit = jnp.logical_or(g <= 1, g == n_groups - 1)

        @pl.when(first_visit)
        def _():
            db_ref[...] = dsw

        @pl.when(jnp.logical_not(first_visit))
        def _():
            db_ref[...] += dsw

        ds_c = (p_c * (dp_c - row) * SCALE).astype(BF16)
        ds_w = (dsw * SCALE).astype(BF16)
        dq_ref[...] = _dot(ds_c, k_ref[0:nctx, :], NN) + _dot(ds_w, k_ref[win, :], NN)
        dk_ref[0:nctx, :] += _dot(ds_c, q, TN)
        dk_ref[win, :] += _dot(ds_w, q, TN)
        dv_ref[0:nctx, :] += _dot(p_c, dov, TN)
        dv_ref[win, :] += _dot(p_w, dov, TN)

    kv_out = pl.BlockSpec((T, HEAD), lambda h, g: (0, h))
    res = _call(name, body, (16, n_groups), (qkv, qkv, qkv, bias_s, do, stats),
                [q_spec, k_spec, v_spec, b_spec, row_spec, _stats_spec(NA_GROUP * GRID_W)],
                [jax.ShapeDtypeStruct((S, 16 * HEAD), F32), jax.ShapeDtypeStruct((T, 16 * HEAD), F32),
                 jax.ShapeDtypeStruct((T, 16 * HEAD), F32), jax.ShapeDtypeStruct(bias_s.shape, F32)],
                [row_spec, kv_out, kv_out, b_spec], [], ("arbitrary", "arbitrary"), carry)
    return _split(res, 4, carry)


def _na_onehots():
    o = np.arange(NA_KH)[:, None]
    i = np.arange(NA_KH)[None, :]
    a = i - o + NA_KH - 1
    A = (a[..., None] == np.arange(2 * NA_KH - 1)).astype(np.float32)
    qc = np.arange(GRID_W)[:, None]
    kc = np.arange(GRID_W)[None, :]
    b = np.clip(kc - qc + NA_KW - 1, 0, 2 * NA_KW - 2)
    cs = np.clip(qc - NA_KW // 2, 0, GRID_W - NA_KW)
    valid = (kc >= cs) & (kc < cs + NA_KW)
    B = ((b[..., None] == np.arange(2 * NA_KW - 1)) & valid[..., None]).astype(np.float32)
    return A, B, valid


def na_bias_table(rpb):
    A, B, valid = _na_onehots()
    hp = lax.Precision.HIGHEST
    t = jnp.einsum('hab,oia->hoib', rpb, jnp.asarray(A), precision=hp)
    bias = jnp.einsum('hoib,qkb->hoqik', t, jnp.asarray(B), precision=hp)
    bias = jnp.where(jnp.asarray(valid)[None, None, :, None, :], bias, NEG)
    return bias.reshape(rpb.shape[0], NA_KH, GRID_W, NA_KH * GRID_W)


def na_bias_grad(name, dbias8):
    A, B, _ = _na_onehots()
    H = dbias8.shape[0]
    nb, na = 2 * NA_KW - 1, 2 * NA_KH - 1
    d = dbias8.reshape(H, NA_KH, GRID_W, NA_KH, GRID_W).transpose(0, 1, 3, 2, 4)
    d = d.reshape(H * NA_KH * NA_KH, GRID_W * GRID_W)
    Bp = np.zeros((GRID_W * GRID_W, 128), np.float32)
    Bp[:, :nb] = B.reshape(GRID_W * GRID_W, nb)
    Ap = np.zeros((16, NA_KH * NA_KH), np.float32)
    Ap[:na] = A.reshape(NA_KH * NA_KH, na).T
    rows_per_head = NA_KH * NA_KH

    def split3(x):
        hi = x.astype(BF16)
        r1 = x - hi.astype(F32)
        mid = r1.astype(BF16)
        return hi, mid, (r1 - mid.astype(F32)).astype(BF16)

    def body(d_ref, b_ref, a_ref, o_ref):
        bm, am = b_ref[...], a_ref[...]
        g = sum(lax.dot_general(p, bm, NN, preferred_element_type=F32) for p in split3(d_ref[...]))
        o_ref[...] = sum(lax.dot_general(am, p, NN, preferred_element_type=F32) for p in split3(g))

    out = pl.pallas_call(
        body, grid=(H,),
        in_specs=[pl.BlockSpec((rows_per_head, GRID_W * GRID_W), lambda h: (h, 0)),
                  pl.BlockSpec((GRID_W * GRID_W, 128), lambda h: (0, 0)),
                  pl.BlockSpec((16, rows_per_head), lambda h: (0, 0))],
        out_specs=pl.BlockSpec((None, 16, 128), lambda h: (h, 0, 0)),
        out_shape=jax.ShapeDtypeStruct((H, 16, 128), F32),
        compiler_params=_cparams(("parallel",)), name=name)(d, jnp.asarray(Bp, BF16), jnp.asarray(Ap, BF16))
    return out[:, :na, :nb]


def _vmem_call(name, fn, out_shape, *arrays):
    def body(*refs):
        n = len(arrays)
        res = fn(*[r[...] for r in refs[:n]])
        if not isinstance(res, (tuple, list)):
            res = (res,)
        for o, v in zip(refs[n:], res):
            o[...] = v
    return pl.pallas_call(body, out_shape=out_shape, name=name,
                          compiler_params=pltpu.CompilerParams(vmem_limit_bytes=VMEM_LIMIT))(*arrays)


def _silu(v):
    return v / (1.0 + jnp.exp(-v))


def _adamw_math(w, g, m, v):
    m2 = ADAM_B1 * m + (1.0 - ADAM_B1) * g
    v2 = ADAM_B2 * v + (1.0 - ADAM_B2) * (g * g)
    m_hat = m2 / (1.0 - ADAM_B1 ** ADAM_STEP)
    v_hat = v2 / (1.0 - ADAM_B2 ** ADAM_STEP)
    delta = -ADAM_LR * (m_hat / (jnp.sqrt(v_hat) + ADAM_EPS) + ADAM_WD * w)
    return delta, m2, v2


def _ew_tile(R, C):
    return _tile(R, max(64, (262144 // C) // 64 * 64))


def adamw_rows(name, w, g, m, v, extra_g=None):
    R, C = w.shape
    tr = _ew_tile(R, C)
    extra_g = list(extra_g or [])
    ne = len(extra_g)

    def body(*refs):
        w_ref, g_ref, m_ref, v_ref = refs[:4]
        gs = g_ref[...]
        for e in refs[4:4 + ne]:
            gs = gs + e[...].astype(F32)
        go, do, mo, vo = refs[4 + ne:]
        d, m2, v2 = _adamw_math(w_ref[...], gs, m_ref[...], v_ref[...])
        go[...] = gs
        do[...] = d
        mo[...] = m2
        vo[...] = v2

    spec = pl.BlockSpec((tr, C), lambda i: (i, 0))
    return pl.pallas_call(
        body, grid=(R // tr,), in_specs=[spec] * (4 + ne), out_specs=[spec] * 4,
        out_shape=[jax.ShapeDtypeStruct((R, C), F32)] * 4,
        compiler_params=_cparams(("parallel",)), name=name)(w, g, m, v, *extra_g)


def rs_chip_sum(name, g8, sib4, where):
    _, R, C = g8.shape
    tr = _ew_tile(R, C)

    def body(s_ref, g_ref, b_ref, o_ref):
        o_ref[...] = (g_ref[...] + b_ref[...]).astype(BF16)

    def chip(q, s):
        return (s[1] + 1 + q) % 4

    blk = (None, tr, C)
    grid_spec = pltpu.PrefetchScalarGridSpec(
        num_scalar_prefetch=1, grid=(3, R // tr),
        in_specs=[pl.BlockSpec(blk, lambda q, i, s: (s[0] + chip(q, s), i, 0)),
                  pl.BlockSpec(blk, lambda q, i, s: (chip(q, s), i, 0))],
        out_specs=pl.BlockSpec(blk, lambda q, i, s: (chip(q, s), i, 0)))
    return pl.pallas_call(body, grid_spec=grid_spec, out_shape=jax.ShapeDtypeStruct((4, R, C), BF16),
                          compiler_params=_cparams(("parallel", "parallel")), name=name)(where, g8, sib4)


def adamw_rs(name, w, g8, sib4, rem3, m, v, idx, layer, prev=None):
    L, R, C = w.shape
    tr = _ew_tile(R, C)

    def body(s_ref, w_ref, g_ref, sb_ref, r0_ref, r1_ref, r2_ref, m_ref, v_ref, *rest):
        go, do, mo, vo = rest[-4:]
        gs = g_ref[...] + sb_ref[...]
        for r_ref in (r0_ref, r1_ref, r2_ref):
            gs = gs + r_ref[...].astype(F32)
        d, m2, v2 = _adamw_math(w_ref[...], gs, m_ref[...], v_ref[...])
        go[...] = gs
        do[...] = d
        mo[...] = m2
        vo[...] = v2

    blk = (None, tr, C)
    mine = pl.BlockSpec(blk, lambda i, s: (layer, i, 0))

    def rem(k):
        return pl.BlockSpec(blk, lambda i, s: (k, i, 0))

    prev = list(prev or [])
    grid_spec = pltpu.PrefetchScalarGridSpec(
        num_scalar_prefetch=1, grid=(R // tr,),
        in_specs=[mine, pl.BlockSpec(blk, lambda i, s: (s[0], i, 0)), pl.BlockSpec(blk, lambda i, s: (s[1], i, 0)),
                  rem(0), rem(1), rem(2), mine, mine] + [pl.BlockSpec(memory_space=pl.ANY)] * len(prev),
        out_specs=[mine] * 4)
    return pl.pallas_call(body, grid_spec=grid_spec, out_shape=[jax.ShapeDtypeStruct((L, R, C), F32)] * 4,
                          input_output_aliases={9 + k: k for k in range(len(prev))},
                          compiler_params=_cparams(("parallel",)), name=name)(
                              idx, w, g8, sib4, rem3, rem3, rem3, m, v, *prev)


def _me():
    x, y, c = lax.axis_index("x"), lax.axis_index("y"), lax.axis_index("c")
    return x, y, c


def _flip(v, bit):
    return 1 - v if bit else v


def ag_small(name, x, with_sum=False):
    R, C = x.shape

    def body(x_ref, out_ref, *rest):
        if with_sum:
            sum_ref, send_sems, recv_sems, lsem = rest
        else:
            send_sems, recv_sems, lsem = rest
        mx, my, mc = _me()
        me = 4 * mx + 2 * my + mc
        local = pltpu.make_async_copy(x_ref, out_ref.at[me], lsem)
        local.start()
        sends = []
        for k in range(1, NDEV):
            peer = (_flip(mx, k & 4), _flip(my, k & 2), _flip(mc, k & 1))
            cp = pltpu.make_async_remote_copy(src_ref=x_ref, dst_ref=out_ref.at[me], send_sem=send_sems.at[k - 1],
                                              recv_sem=recv_sems.at[k - 1], device_id=peer, device_id_type=MESH)
            cp.start()
            sends.append(cp)
        for k in range(1, NDEV):
            px, py, pc = _flip(mx, k & 4), _flip(my, k & 2), _flip(mc, k & 1)
            pltpu.make_async_remote_copy(src_ref=x_ref, dst_ref=out_ref.at[4 * px + 2 * py + pc],
                                         send_sem=send_sems.at[k - 1], recv_sem=recv_sems.at[k - 1],
                                         device_id=(px, py, pc), device_id_type=MESH).wait_recv()
        for cp in sends:
            cp.wait_send()
        local.wait()
        if with_sum:
            acc = out_ref[0]
            for d in range(1, NDEV):
                acc = acc + out_ref[d]
            sum_ref[...] = acc

    out_shape = [jax.ShapeDtypeStruct((NDEV, R, C), F32)]
    if with_sum:
        out_shape.append(jax.ShapeDtypeStruct((R, C), F32))
    vm = pl.BlockSpec(memory_space=pltpu.VMEM)
    res = pl.pallas_call(
        body, out_shape=out_shape, in_specs=[vm], out_specs=[vm] * len(out_shape),
        scratch_shapes=[pltpu.SemaphoreType.DMA((NDEV - 1,)), pltpu.SemaphoreType.DMA((NDEV - 1,)),
                        pltpu.SemaphoreType.DMA],
        compiler_params=pltpu.CompilerParams(vmem_limit_bytes=VMEM_LIMIT), name=name)(x)
    return res if with_sum else res[0]


def ag_big(name, shards):
    n = len(shards)

    def body(*refs):
        ins, outs = refs[:n], refs[n:2 * n]
        send_sems, recv_sems, lsems = refs[2 * n:]
        mx, my, mc = _me()
        me = (mx, my, mc)
        sibling = (mx, my, 1 - mc)
        chips = [(1 - mx, my), (mx, 1 - my), (1 - mx, 1 - my)]

        def idx(p):
            return 4 * p[0] + 2 * p[1] + p[2]

        def copy(t, k, block, to, src=None):
            dst = outs[t].at[idx(block)]
            return pltpu.make_async_remote_copy(
                src_ref=dst if src is None else src, dst_ref=dst, send_sem=send_sems.at[7 * t + k],
                recv_sem=recv_sems.at[7 * t + k], device_id=to, device_id_type=MESH)

        started = []
        locals_ = []
        for t in range(n):
            mine = pltpu.make_async_copy(ins[t], outs[t].at[idx(me)], lsems.at[t])
            mine.start()
            locals_.append(mine)
            first = [copy(t, 0, me, sibling, src=ins[t])]
            first += [copy(t, 1 + j, me, (*chip, mc), src=ins[t]) for j, chip in enumerate(chips)]
            for cp in first:
                cp.start()
            started += first
        for t in range(n):
            for j, chip in enumerate(chips):
                copy(t, 1 + j, (*chip, mc), me).wait_recv()
                fwd = copy(t, 4 + j, (*chip, mc), sibling)
                fwd.start()
                started.append(fwd)
        for t in range(n):
            copy(t, 0, sibling, me).wait_recv()
            for j, chip in enumerate(chips):
                copy(t, 4 + j, (*chip, 1 - mc), me).wait_recv()
        for cp in started:
            cp.wait_send()
        for mine in locals_:
            mine.wait()

    anyspec = pl.BlockSpec(memory_space=pl.ANY)
    return pl.pallas_call(
        body, out_shape=[jax.ShapeDtypeStruct((NDEV,) + s.shape, s.dtype) for s in shards],
        in_specs=[anyspec] * n, out_specs=[anyspec] * n,
        scratch_shapes=[pltpu.SemaphoreType.DMA((7 * n,)), pltpu.SemaphoreType.DMA((7 * n,)),
                        pltpu.SemaphoreType.DMA((n,))],
        name=name)(*shards)


def _idx(p):
    return 4 * p[0] + 2 * p[1] + p[2]


def _remote(src, dst, ss, rs, k, to):
    return pltpu.make_async_remote_copy(src_ref=src, dst_ref=dst, send_sem=ss.at[k], recv_sem=rs.at[k],
                                        device_id=to, device_id_type=MESH)


def ex_ag_chips(shards):
    n = len(shards)

    def copies(ci, co, ss, rs, base):
        mx, my, mc = _me()
        me = (mx, my, mc)
        peers = [(mx, my, 1 - mc), (1 - mx, my, mc), (mx, 1 - my, mc), (1 - mx, 1 - my, mc)]
        sends, recvs, local = [], [], []
        for t in range(n):
            b = base + 5 * t
            for k, peer in enumerate(peers):
                sends.append(_remote(ci[t], co[t].at[_idx(me)], ss, rs, b + k, peer))
                recvs.append(_remote(ci[t], co[t].at[_idx(peer)], ss, rs, b + k, peer))
            local.append(pltpu.make_async_copy(ci[t], co[t].at[_idx(me)], ss.at[b + 4]))
        return sends, recvs, local

    def start(ci, co, ss, rs, base):
        sends, _, local = copies(ci, co, ss, rs, base)
        for cp in local + sends:
            cp.start()

    def finish(ci, co, ss, rs, base):
        sends, recvs, local = copies(ci, co, ss, rs, base)
        for cp in recvs:
            cp.wait_recv()
        for cp in sends:
            cp.wait_send()
        for cp in local:
            cp.wait()

    outs = [jax.ShapeDtypeStruct((NDEV,) + s.shape, s.dtype) for s in shards]
    return Exchange(shards, outs, {}, 5 * n, start, finish)


def ex_ag_sibling(bufs):
    n = len(bufs)

    def copies(co, ss, rs, base):
        mx, my, mc = _me()
        sibling = (mx, my, 1 - mc)
        chips = [(1 - mx, my), (mx, 1 - my), (1 - mx, 1 - my)]
        sends, recvs = [], []
        for t in range(n):
            for j, chip in enumerate(chips):
                mine, theirs = co[t].at[_idx((*chip, mc))], co[t].at[_idx((*chip, 1 - mc))]
                sends.append(_remote(mine, mine, ss, rs, base + 3 * t + j, sibling))
                recvs.append(_remote(mine, theirs, ss, rs, base + 3 * t + j, sibling))
        return sends, recvs

    def start(ci, co, ss, rs, base):
        for cp in copies(co, ss, rs, base)[0]:
            cp.start()

    def finish(ci, co, ss, rs, base):
        sends, recvs = copies(co, ss, rs, base)
        for cp in recvs:
            cp.wait_recv()
        for cp in sends:
            cp.wait_send()

    outs = [jax.ShapeDtypeStruct(b.shape, b.dtype) for b in bufs]
    return Exchange(bufs, outs, {t: t for t in range(n)}, 3 * n, start, finish)


def ex_rs_sibling(grads):
    n = len(grads)

    def copies(ci, co, ss, rs, base):
        mx, my, mc = _me()
        return [_remote(ci[t].at[pl.ds((1 - mc) * 4, 4)], co[t], ss, rs, base + t, (mx, my, 1 - mc)) for t in range(n)]

    def start(ci, co, ss, rs, base):
        for cp in copies(ci, co, ss, rs, base):
            cp.start()

    def finish(ci, co, ss, rs, base):
        for cp in copies(ci, co, ss, rs, base):
            cp.wait()

    outs = [jax.ShapeDtypeStruct((4,) + g.shape[1:], g.dtype) for g in grads]
    return Exchange(grads, outs, {}, n, start, finish)


def ex_rs_chips(parts):
    n = len(parts)

    def copies(ci, co, ss, rs, base):
        mx, my, mc = _me()
        cps = []
        for t in range(n):
            for k in range(1, 4):
                px, py = _flip(mx, k & 2), _flip(my, k & 1)
                cps.append(_remote(ci[t].at[2 * px + py], co[t].at[k - 1], ss, rs, base + 3 * t + k - 1, (px, py, mc)))
        return cps

    def start(ci, co, ss, rs, base):
        for cp in copies(ci, co, ss, rs, base):
            cp.start()

    def finish(ci, co, ss, rs, base):
        for cp in copies(ci, co, ss, rs, base):
            cp.wait()

    outs = [jax.ShapeDtypeStruct((3,) + p.shape[1:], p.dtype) for p in parts]
    return Exchange(parts, outs, {}, 3 * n, start, finish)


def run_exchanges(name, xs):
    x = merge_exchanges(xs)
    n_ci, n_co = len(x.ins), len(x.out_shapes)

    def body(*refs):
        ci, co = refs[:n_ci], refs[n_ci:n_ci + n_co]
        ss, rs = refs[n_ci + n_co:]
        x.start(ci, co, ss, rs, 0)
        x.finish(ci, co, ss, rs, 0)

    hbm = pl.BlockSpec(memory_space=pl.ANY)
    return pl.pallas_call(
        body, out_shape=x.out_shapes, in_specs=[hbm] * n_ci, out_specs=[hbm] * n_co, input_output_aliases=x.aliases,
        scratch_shapes=[pltpu.SemaphoreType.DMA((x.n_sems,)), pltpu.SemaphoreType.DMA((x.n_sems,))], name=name)(*x.ins)


def _rope_tables(S, nctx):
    t = jnp.arange(S)
    row = (t // GRID_W).astype(F32)
    col = (t % GRID_W).astype(F32)
    pairs = HEAD // 4
    inv = ROPE_THETA ** (-jnp.arange(pairs, dtype=F32) / pairs)
    ang_r = row[:, None] * inv
    ang_c = col[:, None] * inv
    ang = jnp.concatenate([ang_r, ang_r, ang_c, ang_c], axis=-1)
    cos = jnp.concatenate([jnp.ones((nctx, HEAD), F32), jnp.cos(ang)], axis=0)
    sin = jnp.concatenate([jnp.zeros((nctx, HEAD), F32), jnp.sin(ang)], axis=0)
    lane = jnp.arange(HEAD)[None, :]
    first = (lane & 32) == 0
    return cos, jnp.where(first, -sin, 0.0), jnp.where(first, 0.0, sin)


def _pad_rows(v, rows):
    v = v.reshape(-1).astype(F32)
    return jnp.pad(v, (0, rows * 128 - v.shape[0])).reshape(rows, 128)


def _rows8(n):
    return -(-n // 1024) * 8


def kernel(x, c, ctx, c_ctx, ada_w, ada_b, norm_w, mlp_w1, mlp_w2, ev_w_in, ev_w_out, ev_q_norm, ev_k_norm, ev_sink, od_w_in, od_w_out, od_rpb, final_norm_w, loss_target, m_c_ctx, m_ada_w, m_ada_b, m_norm_w, m_mlp_w1, m_mlp_w2, m_ev_w_in, m_ev_w_out, m_ev_q_norm, m_ev_k_norm, m_ev_sink, m_od_w_in, m_od_w_out, m_od_rpb, m_final_norm_w, v_c_ctx, v_ada_w, v_ada_b, v_norm_w, v_mlp_w1, v_mlp_w2, v_ev_w_in, v_ev_w_out, v_ev_q_norm, v_ev_k_norm, v_ev_sink, v_od_w_in, v_od_w_out, v_od_rpb, v_final_norm_w):
    S, D = x.shape[1], x.shape[2]
    NC = ctx.shape[1]
    T = NC + S
    assert NC == ROW_TILE and S % GRID_W == 0
    ada_cols = ada_w.shape[2]
    nw_cols = norm_w.shape[2]
    me = 4 * lax.axis_index("x") + 2 * lax.axis_index("y") + lax.axis_index("c")

    pack1 = jnp.concatenate([_pad_rows(c, _rows8(D)), _pad_rows(norm_w, _rows8(4 * nw_cols))], axis=0)
    g1 = ag_small("ag_c_normw", pack1)
    c_all = g1[:, :D // 128].reshape(NDEV, D)
    nw_rows = _rows8(D)
    nw = g1[:, nw_rows:nw_rows + 4 * nw_cols // 128].reshape(NDEV, 2, 2, nw_cols)
    nw = nw.transpose(1, 2, 0, 3).reshape(2, 2, D)
    cin = jnp.concatenate([c_all, jnp.broadcast_to(c_ctx[None], (NDEV, D))], axis=0)
    act = _vmem_call("silu_c", lambda v: _silu(v).astype(BF16), jax.ShapeDtypeStruct((2 * NDEV, D), BF16), cin)
    ada_b_loc = lax.dynamic_slice_in_dim(ada_b, me * ada_cols, ada_cols, axis=1)
    mods = [mm_nn(f"mod{i}", act, ada_w[i], _epi_bias, [F32], extras=(ada_b_loc[i:i + 1],), extra_kinds=('n',))[0]
            for i in range(2)]
    gm = ag_small("ag_mod", jnp.concatenate(mods, axis=1))
    gm = gm.reshape(NDEV, 2 * NDEV, 2, ada_cols).transpose(2, 1, 0, 3).reshape(2, 2 * NDEV, NDEV * ada_cols)
    mod_lat = lax.dynamic_index_in_dim(gm, me, axis=1, keepdims=False)
    mod_ctx = gm[:, NDEV]
    mod2 = jnp.stack([mod_ctx, mod_lat], axis=1).reshape(2, 2, 6, D)

    def chunk(i, j):
        return mod2[i, :, j, :]

    def b16(w):
        return w.astype(BF16)

    (w_in_e,) = ag_big("ag_weights_l0_qkv", [b16(ev_w_in[0])])
    w_in_e = w_in_e.transpose(1, 0, 2).reshape(D, -1)

    cos, sa, sb = _rope_tables(S, NC)
    bias8 = na_span_bias(na_bias_table(od_rpb[0]))
    sink = ev_sink[0]
    TQ_F, TQ_B = 256, 256

    X0 = jnp.concatenate([ctx[0], x[0]], axis=0)
    h_a = norm_mod("l0_norm1", X0, nw[0, 0][None], chunk(0, 0), chunk(0, 1), NC)
    (qkv0,), (w_out_e_half,) = mm_nn("l0_qkv", h_a, w_in_e, _epi_store(F32), [F32], tn_cap=1024,
                                     carry=[ex_ag_chips([b16(ev_w_out[0])])])
    qkvh0 = prep_even("l0_prep", qkv0, ev_q_norm, ev_k_norm, cos, sa, sb)
    (o0, stats0), (w1_0_half, w2_0_half, w_out_o_half, w_out_e) = attn_even_fwd(
        "l0_attn", qkvh0, sink, NC, TQ_F,
        carry=[ex_ag_chips([b16(mlp_w1[0]), b16(mlp_w2[0]), b16(od_w_out[0])]), ex_ag_sibling([w_out_e_half])])
    w_out_e = w_out_e.reshape(-1, D)
    tm0 = _tile(T, 1100)
    (X1, y0), (w1_0, w2_0) = mm_nn("l0_out", o0, w_out_e, _epi_resid_gate(NC, tm0), [F32, F32],
                                   extras=(X0, chunk(0, 2)), extra_kinds=('mn', 'n'),
                                   carry=[ex_ag_sibling([w1_0_half, w2_0_half])])
    h_b = norm_mod("l0_norm2", X1, nw[0, 1][None], chunk(0, 3), chunk(0, 4), NC)
    (a0, r0), (w_in_o_half, w_out_o) = mm_nn(
        "l0_up", h_b, w1_0, _epi_relu2, [BF16, BF16], tn_cap=1024,
        carry=[ex_ag_chips([b16(od_w_in[0])]), ex_ag_sibling([w_out_o_half])])
    (X2, z0), (w1_1_half, w_in_o) = mm_nn(
        "l0_down", a0, w2_0.reshape(-1, D), _epi_resid_gate(NC, tm0), [F32, F32], extras=(X1, chunk(0, 5)),
        extra_kinds=('mn', 'n'), tn_cap=1024, carry=[ex_ag_chips([b16(mlp_w1[1])]), ex_ag_sibling([w_in_o_half])])
    w_out_o = w_out_o.reshape(-1, D)

    h_c = norm_mod("l1_norm1", X2, nw[1, 0][None], chunk(1, 0), chunk(1, 1), NC)
    (qkv1,), (w1_1,) = mm_nn("l1_qkv", h_c, w_in_o, _epi_store(BF16), [BF16], tn_cap=768,
                             carry=[ex_ag_sibling([w1_1_half])])
    (o1, stats1), (w2_1_half,) = attn_odd_fwd("l1_attn", qkv1, bias8, NC, carry=[ex_ag_chips([b16(mlp_w2[1])])])
    X2l = X2[NC:]
    tm1 = _tile(S, 1100)
    (X3, y1), (w2_1,) = mm_nn("l1_out", o1, w_out_o, _epi_resid_gate(0, tm1), [F32, F32],
                              extras=(X2l, chunk(1, 2)), extra_kinds=('mn', 'n'),
                              carry=[ex_ag_sibling([w2_1_half])])
    h_d = norm_mod("l1_norm2", X3, nw[1, 1][None], chunk(1, 3), chunk(1, 4), 0)
    a1, r1 = mm_nn("l1_up", h_d, w1_1, _epi_relu2, [BF16, BF16], tn_cap=1024)
    X4, z1 = mm_nn("l1_down", a1, w2_1.reshape(-1, D), _epi_resid_gate(0, tm1), [F32, F32], extras=(X3, chunk(1, 5)),
                   extra_kinds=('mn', 'n'), tn_cap=1024)
    dX4, loss_p, dfw_p, dz1, pg2_1 = final_loss("final_loss", X4, final_norm_w[None], loss_target[0], z1, chunk(1, 5))
    w_in = [w_in_e, w_in_o]
    w_out = [w_out_e, w_out_o]
    w1 = [w1_0, w1_1]
    w2 = [w2_0.reshape(-1, D), w2_1.reshape(-1, D)]

    mc4 = (lax.axis_index("c") * 4).astype(jnp.int32)
    my_chip = (2 * lax.axis_index("x") + lax.axis_index("y")).astype(jnp.int32)

    def chip_sum(tag, g8, sib4):
        return rs_chip_sum(f"rs_chip_sum_{tag}", g8, sib4, jnp.stack([mc4, my_chip]))

    du1 = mm_nt("l1_down_dx", dz1, w2[1], _epi_mul2r, BF16, extras=(r1,))
    g_w1_1 = mm_tn("l1_up_dw", h_d, du1, 1)
    g_w2_1, (sib_w1_1,) = mm_tn("l1_down_dw", a1, dz1, 0, carry=[ex_rs_sibling([g_w1_1])])
    dh_d, (rem_w1_1, sib_w2_1) = mm_nt(
        "l1_up_dx", du1, w1[1], _epi_store(F32), F32,
        carry=[ex_rs_chips([chip_sum("w1_1", g_w1_1, sib_w1_1)]), ex_rs_sibling([g_w2_1])])
    dX3, pn2_1, dy1, pg1_1 = norm_bwd("l1_norm2_bwd", X3, dh_d, dX4, nw[1, 1][None], chunk(1, 4), 0,
                                      gate=(y1, chunk(1, 2)))
    do1 = mm_nt("l1_out_dx", dy1, w_out[1], _epi_store(BF16), BF16)
    g_wout_1 = mm_tn("l1_out_dw", o1, dy1, 0)
    (dq1, dk1, dv1, dbias8), (rem_w2_1, sib_wout_1) = attn_odd_bwd(
        "l1_attn_bwd", qkv1, bias8, do1, stats1, NC,
        carry=[ex_rs_chips([chip_sum("w2_1", g_w2_1, sib_w2_1)]), ex_rs_sibling([g_wout_1])])
    dqkv1 = jnp.concatenate([jnp.pad(dq1, ((NC, 0), (0, 0))), dk1, dv1], axis=1).astype(BF16)
    dh_c, (rem_wout_1,) = mm_nt("l1_qkv_dx", dqkv1, w_in[1], _epi_store(F32), F32,
                                carry=[ex_rs_chips([chip_sum("wout_1", g_wout_1, sib_wout_1)])])
    g_win_1 = mm_tn("l1_qkv_dw", h_c, dqkv1, 1)
    dX2, pn1_1, dz0, pg2_0 = norm_bwd("l1_norm1_bwd", X2, dh_c, dX3, nw[1, 0][None], chunk(1, 1), NC, dres_skip=NC,
                                      gate=(z0, chunk(0, 5)))
    d_rpb = na_bias_grad("rpb_grad", na_span_bias_grad(dbias8))

    du0, (sib_win_1,) = mm_nt("l0_down_dx", dz0, w2[0], _epi_mul2r, BF16, extras=(r0,),
                              carry=[ex_rs_sibling([g_win_1])])
    g_w1_0, (rem_win_1,) = mm_tn("l0_up_dw", h_b, du0, 1,
                                 carry=[ex_rs_chips([chip_sum("win_1", g_win_1, sib_win_1)])])
    g_w2_0, (sib_w1_0,) = mm_tn("l0_down_dw", a0, dz0, 0, carry=[ex_rs_sibling([g_w1_0])])
    dh_b, (rem_w1_0, sib_w2_0) = mm_nt(
        "l0_up_dx", du0, w1[0], _epi_store(F32), F32,
        carry=[ex_rs_chips([chip_sum("w1_0", g_w1_0, sib_w1_0)]), ex_rs_sibling([g_w2_0])])
    dX1, pn2_0, dy0, pg1_0 = norm_bwd("l0_norm2_bwd", X1, dh_b, dX2, nw[0, 1][None], chunk(0, 4), NC,
                                      gate=(y0, chunk(0, 2)))
    do0 = mm_nt("l0_out_dx", dy0, w_out[0], _epi_store(BF16), BF16)
    g_wout_0 = mm_tn("l0_out_dw", o0, dy0, 0)
    (dq0, dk0, dv0, dsink_p), (rem_w2_0, sib_wout_0) = attn_even_bwd(
        "l0_attn_bwd", qkvh0, sink, do0, o0, stats0, NC, TQ_B,
        carry=[ex_rs_chips([chip_sum("w2_0", g_w2_0, sib_w2_0)]), ex_rs_sibling([g_wout_0])])
    dqkv0, pqk = prep_even_bwd("l0_prep_bwd", qkv0, dq0, dk0, dv0, ev_q_norm, ev_k_norm, cos, sa, sb)
    g_win_0, (rem_wout_0,) = mm_tn("l0_qkv_dw", h_a, dqkv0, 1,
                                   carry=[ex_rs_chips([chip_sum("wout_0", g_wout_0, sib_wout_0)])])
    dh_a, (sib_win_0,) = mm_nt("l0_qkv_dx", dqkv0, w_in[0], _epi_store(F32), F32, carry=[ex_rs_sibling([g_win_0])])
    (dx_lat, pn1_0), (rem_win_0,) = norm_bwd(
        "l0_norm1_bwd", X0, dh_a, dX1, nw[0, 0][None], chunk(0, 1), NC, out_skip=NC,
        carry=[ex_rs_chips([chip_sum("win_0", g_win_0, sib_win_0)])])
    grad_x = dx_lat[None]

    def dmod(grp, pn1, pg1, pn2, pg2):
        return jnp.concatenate([pn1[grp], pn1[2 + grp], pg1[grp], pn2[grp], pn2[2 + grp], pg2[grp]])

    dmod_lat = jnp.stack([dmod(1, pn1_0, pg1_0, pn2_0, pg2_0), dmod(1, pn1_1, pg1_1, pn2_1, pg2_1)])
    dmod_ctx = jnp.stack([dmod(0, pn1_0, pg1_0, pn2_0, pg2_0), dmod(0, pn1_1, pg1_1, pn2_1, pg2_1)])
    dnw_p = jnp.stack([pn1_0[4], pn2_0[4], pn1_1[4], pn2_1[4]])
    pieces = [dmod_lat, dmod_ctx, dnw_p, pqk[0], pqk[1], dsink_p[8:, 0, 0], d_rpb, dfw_p[0], loss_p[0, 0]]
    sizes = [int(np.prod(p.shape)) for p in pieces]
    rows = [_rows8(s) for s in sizes]
    pack2 = jnp.concatenate([_pad_rows(p, r) for p, r in zip(pieces, rows)], axis=0)
    g2, s2 = ag_small("ag_small_grads", pack2, with_sum=True)
    offs = np.concatenate([[0], np.cumsum(rows)])

    def piece(arr, i, shape):
        return arr[..., offs[i]:offs[i + 1], :].reshape(arr.shape[:-2] + (-1,))[..., :sizes[i]].reshape(
            arr.shape[:-2] + shape)

    dmod_all = piece(g2, 0, (2, 6 * D))
    dmodc_sum = piece(s2, 1, (2, 6 * D))
    dnw_sum = piece(s2, 2, (2, 2, D))
    g_qn = piece(s2, 3, ev_q_norm.shape)
    g_kn = piece(s2, 4, ev_k_norm.shape)
    g_sink = piece(s2, 5, ev_sink.shape)
    g_rpb = piece(s2, 6, od_rpb.shape)
    g_fw = piece(s2, 7, final_norm_w.shape)
    loss = piece(s2, 8, ())

    dm16 = jnp.concatenate([dmod_all.transpose(1, 0, 2), dmodc_sum[:, None, :],
                            jnp.zeros((2, NDEV - 1, 6 * D), F32)], axis=1)
    dm16_loc = lax.dynamic_slice_in_dim(dm16.reshape(2, 2 * NDEV, NDEV, ada_cols), me, 1, axis=2)[:, :, 0, :]
    g_ada_b = _vmem_call("ada_b_grad", lambda v: jnp.sum(v, axis=1),
                         jax.ShapeDtypeStruct((2, 6 * D), F32), dm16)
    g_ada_w = []
    dact_p = None
    for i in range(2):
        dmb = dm16_loc[i].astype(BF16)
        g_ada_w.append(mm_tn(f"ada_w_grad{i}", act, dmb, None))
        part = mm_nt(f"ada_dact{i}", dmb, ada_w[i], _epi_store(F32), F32)
        dact_p = part if dact_p is None else dact_p + part
    _, dact = ag_small("ag_cctx", dact_p, with_sum=True)

    def cctx_grad(da, cc):
        sg = 1.0 / (1.0 + jnp.exp(-cc))
        return da[NDEV:NDEV + 1] * (sg * (1.0 + cc * (1.0 - sg)))

    g_cctx = _vmem_call("cctx_grad", cctx_grad, jax.ShapeDtypeStruct((1, D), F32), dact, c_ctx[None])[0]

    grads = [g_win_0, g_wout_0, g_w1_0, g_w2_0, g_win_1, g_wout_1, g_w1_1, g_w2_1]
    sib = [sib_win_0, sib_wout_0, sib_w1_0, sib_w2_0, sib_win_1, sib_wout_1, sib_w1_1, sib_w2_1]
    rem = [rem_win_0, rem_wout_0, rem_w1_0, rem_w2_0, rem_win_1, rem_wout_1, rem_w1_1, rem_w2_1]
    own_idx = jnp.stack([mc4 + my_chip, my_chip])

    def big(tag, w, m, v, ts):
        res = None
        for l, t in enumerate(ts):
            res = adamw_rs(f"adamw_{tag}_{l}", w, grads[t], sib[t], rem[t], m, v, own_idx, l, res)
        return tuple(res)

    r_ev_w_in = big('ev_w_in', ev_w_in, m_ev_w_in, v_ev_w_in, [0])
    r_ev_w_out = big('ev_w_out', ev_w_out, m_ev_w_out, v_ev_w_out, [1])
    r_mlp_w1 = big('mlp_w1', mlp_w1, m_mlp_w1, v_mlp_w1, [2, 6])
    r_mlp_w2 = big('mlp_w2', mlp_w2, m_mlp_w2, v_mlp_w2, [3, 7])
    r_od_w_in = big('od_w_in', od_w_in, m_od_w_in, v_od_w_in, [4])
    r_od_w_out = big('od_w_out', od_w_out, m_od_w_out, v_od_w_out, [5])

    g_ada = jnp.stack(g_ada_w)
    r_ada_w = adamw_rows("adamw_ada_w", ada_w.reshape(2 * D, ada_cols), g_ada.reshape(2 * D, ada_cols),
                         m_ada_w.reshape(2 * D, ada_cols), v_ada_w.reshape(2 * D, ada_cols))
    r_ada_w = tuple(u.reshape(2, D, ada_cols) for u in r_ada_w)

    g_nw_loc = lax.dynamic_slice_in_dim(dnw_sum, me * nw_cols, nw_cols, axis=2)
    small = [(c_ctx, g_cctx, m_c_ctx, v_c_ctx), (ada_b, g_ada_b, m_ada_b, v_ada_b),
             (norm_w, g_nw_loc, m_norm_w, v_norm_w), (ev_q_norm, g_qn, m_ev_q_norm, v_ev_q_norm),
             (ev_k_norm, g_kn, m_ev_k_norm, v_ev_k_norm), (ev_sink, g_sink, m_ev_sink, v_ev_sink),
             (od_rpb, g_rpb, m_od_rpb, v_od_rpb), (final_norm_w, g_fw, m_final_norm_w, v_final_norm_w)]
    srows = [_rows8(int(np.prod(w.shape))) for w, _, _, _ in small]
    packs = [jnp.concatenate([_pad_rows(tup[k], r) for tup, r in zip(small, srows)], axis=0) for k in range(4)]
    sres = adamw_rows("adamw_small", *packs)
    soffs = np.concatenate([[0], np.cumsum(srows)])

    def unpack(arr, i):
        w = small[i][0]
        return arr[soffs[i]:soffs[i + 1]].reshape(-1)[:int(np.prod(w.shape))].reshape(w.shape)

    sm = [[unpack(sres[k], i) for i in range(len(small))] for k in range(4)]

    def outs(k):
        big_k = {'ada_w': r_ada_w[k], 'mlp_w1': r_mlp_w1[k], 'mlp_w2': r_mlp_w2[k], 'ev_w_in': r_ev_w_in[k],
                 'ev_w_out': r_ev_w_out[k], 'od_w_in': r_od_w_in[k], 'od_w_out': r_od_w_out[k]}
        return (sm[k][0], big_k['ada_w'], sm[k][1], sm[k][2], big_k['mlp_w1'], big_k['mlp_w2'], big_k['ev_w_in'],
                big_k['ev_w_out'], sm[k][3], sm[k][4], sm[k][5], big_k['od_w_in'], big_k['od_w_out'], sm[k][6],
                sm[k][7])

    return (loss, grad_x, *outs(0), *outs(1), *outs(2), *outs(3))
```
